```python
import jax, jax.numpy as jnp
from jax import lax
import numpy as np

D_MODEL = 2048
BATCH = 4
SEQ = 8192
DEPTH = 2

HEAD_DIM = 64
Q_BLOCK = 128
A_GROUPS = ((128, 1), (512, 4), (2048, 16))
A_HEADS_PER_GROUP = 4
A_HEADS = A_HEADS_PER_GROUP * len(A_GROUPS)
B_HEADS = 8
B_KV_HEADS = 2
B_WINDOW = 128
C_HEADS = 12
C_KV_HEADS = 2
CMP_BLOCK = 32
CMP_STRIDE = 16
CMP_HIDDEN = 256
SLC_BLOCK = 64
SLC_TOPK = 16
C_WINDOW = 512
N_BRANCHES = 3
D_FF = 5632
RMS_EPS = 1e-6
NEG_INF = -1e30

A_OUT = A_HEADS_PER_GROUP * HEAD_DIM
B_OUT = B_HEADS * HEAD_DIM
C_OUT = C_HEADS * HEAD_DIM
A_QKV_COLS = 3 * A_HEADS * HEAD_DIM
B_Q_COLS = B_HEADS * HEAD_DIM
B_KV_COLS = 2 * B_KV_HEADS * HEAD_DIM
C_Q_COLS = C_HEADS * HEAD_DIM
C_KV_COLS = 6 * C_KV_HEADS * HEAD_DIM
C_GATE_COLS = 3 * C_HEADS
GATE_COLS = N_BRANCHES * D_MODEL
IN_COLS = A_QKV_COLS + B_Q_COLS + B_KV_COLS + C_Q_COLS + C_KV_COLS + C_GATE_COLS + GATE_COLS

kernel_name = 'hybrid_dilated_sink_nsa_macaron'


def _rmsnorm(x, g):
    xf = x.astype(jnp.float32)
    y = xf * lax.rsqrt(jnp.mean(xf * xf, axis=-1, keepdims=True) + RMS_EPS)
    return (y * g.astype(jnp.float32)).astype(x.dtype)


def _alibi_slopes(n_heads):
    return jnp.asarray(2.0 ** (-8.0 * np.arange(1, n_heads + 1, dtype=np.float32) / n_heads), dtype=jnp.float32)


def _swiglu(h, w_gu, w_down):
    gate, up = jnp.split(h @ w_gu, 2, axis=-1)
    return (jax.nn.silu(gate) * up) @ w_down


def _masked_softmax(s, mask, sinks=None, with_lse=False):
    s = jnp.where(mask, s, NEG_INF)
    m = jnp.max(s, axis=-1, keepdims=True)
    if sinks is not None:
        m = jnp.maximum(m, sinks)
    e = jnp.where(mask, jnp.exp(s - m), 0.0)
    den = jnp.sum(e, axis=-1, keepdims=True)
    if sinks is not None:
        den = den + jnp.exp(sinks - m)
    p = e / jnp.where(den > 0, den, 1.0)
    if with_lse:
        return p, (m + jnp.log(den))[..., 0]
    return p


def _banded_attention(q, k, v, max_dist, slopes, dist_scale, sinks=None, with_lse=False):
    n, length, g, r, dh = q.shape
    nb = -(-max_dist // Q_BLOCK)
    nq = -(-length // Q_BLOCK)
    pad = nq * Q_BLOCK - length
    nk = (nb + 1) * Q_BLOCK
    q = jnp.pad(q, ((0, 0), (0, pad), (0, 0), (0, 0), (0, 0)))
    kv_pad = ((0, 0), (nb * Q_BLOCK, pad), (0, 0), (0, 0))
    k = jnp.pad(k, kv_pad)
    v = jnp.pad(v, kv_pad)
    rel = nb * Q_BLOCK + jnp.arange(Q_BLOCK)[:, None] - jnp.arange(nk)[None, :]
    band = (rel >= 0) & (rel <= max_dist)
    bias = -(slopes * dist_scale)[:, :, None, None] * rel.astype(jnp.float32)
    sink_logits = None if sinks is None else sinks[:, :, None, None].astype(jnp.float32)
    scale = dh ** -0.5

    def one_block(i):
        qb = lax.dynamic_slice_in_dim(q, i * Q_BLOCK, Q_BLOCK, axis=1)
        kb = lax.dynamic_slice_in_dim(k, i * Q_BLOCK, nk, axis=1)
        vb = lax.dynamic_slice_in_dim(v, i * Q_BLOCK, nk, axis=1)
        s = jnp.einsum('nqgrd,nkgd->ngrqk', qb, kb).astype(jnp.float32) * scale + bias
        key_pos = (i - nb) * Q_BLOCK + jnp.arange(nk)
        mask = band & (key_pos >= 0)[None, :]
        if with_lse:
            p, lse = _masked_softmax(s, mask, sink_logits, True)
            return jnp.einsum('ngrqk,nkgd->nqgrd', p.astype(vb.dtype), vb), lse
        p = _masked_softmax(s, mask, sink_logits)
        return jnp.einsum('ngrqk,nkgd->nqgrd', p.astype(vb.dtype), vb)

    res = lax.map(one_block, jnp.arange(nq))
    o = res[0] if with_lse else res
    o = jnp.moveaxis(o, 0, 1).reshape(n, nq * Q_BLOCK, g, r, dh)[:, :length]
    if with_lse:
        lse = res[1].transpose(1, 0, 4, 2, 3).reshape(n, nq * Q_BLOCK, g, r)[:, :length]
        return o, lse
    return o


def _dilated_mixer(q, k, v, slopes):
    b, s, _, dh = q.shape
    hpg = A_HEADS_PER_GROUP
    outs, lses = [], []
    for gi, (window, dil) in enumerate(A_GROUPS):
        hs = slice(gi * hpg, (gi + 1) * hpg)

        def to_strided(t):
            return t.reshape(b, s // dil, dil, hpg, dh).transpose(0, 2, 1, 3, 4).reshape(b * dil, s // dil, hpg, dh)

        o, lse = _banded_attention(to_strided(q[:, :, hs])[:, :, :, None], to_strided(k[:, :, hs]),
                                   to_strided(v[:, :, hs]), window // dil, slopes[hs][:, None], dil,
                                   with_lse=True)
        outs.append(o.reshape(b, dil, s // dil, hpg, dh).transpose(0, 2, 1, 3, 4).reshape(b, s, hpg, dh))
        lses.append(lse.reshape(b, dil, s // dil, hpg).transpose(0, 2, 1, 3).reshape(b, s, hpg))
    w = jax.nn.softmax(jnp.stack(lses), axis=0)
    o = jnp.einsum('gbsh,gbshd->bshd', w.astype(q.dtype), jnp.stack(outs))
    return o.reshape(b, s, hpg * dh)


def _sink_swa_mixer(q, k, v, slopes, sinks):
    b, s, _, dh = q.shape
    r = B_HEADS // B_KV_HEADS
    o = _banded_attention(q.reshape(b, s, B_KV_HEADS, r, dh), k, v, B_WINDOW - 1,
                          slopes.reshape(B_KV_HEADS, r), 1, sinks=sinks.reshape(B_KV_HEADS, r))
    return o.reshape(b, s, B_OUT)


def _compress(t, pos, w1, w2):
    b, s, g, dh = t.shape
    n_cmp = (s - CMP_BLOCK) // CMP_STRIDE + 1
    idx = CMP_STRIDE * jnp.arange(n_cmp)[:, None] + jnp.arange(CMP_BLOCK)[None, :]
    blocks = t[:, idx] + pos[None, None, :, None, :]
    blocks = blocks.transpose(0, 1, 3, 2, 4).reshape(b, n_cmp, g, CMP_BLOCK * dh)
    return jax.nn.silu(blocks @ w1) @ w2


def _nsa_mixer(q, k_cmp, v_cmp, k_slc, v_slc, k_win, v_win, gate_logits, slopes, q_gain, k_gain,
               cmp_pos, cmp_w1, cmp_w2):
    b, s, _, dh = q.shape
    g = C_KV_HEADS
    r = C_HEADS // g
    scale = dh ** -0.5
    q = _rmsnorm(q, q_gain).reshape(b, s, g, r, dh)
    slopes = slopes.reshape(g, r)
    sl = slopes[None, :, :, None, None]
    kc = _rmsnorm(_compress(k_cmp, cmp_pos[0], cmp_w1[0], cmp_w2[0]), k_gain)
    vc = _compress(v_cmp, cmp_pos[1], cmp_w1[1], cmp_w2[1])
    n_cmp = kc.shape[1]
    cmp_start = CMP_STRIDE * jnp.arange(n_cmp)
    cmp_end = cmp_start + CMP_BLOCK - 1
    n_slc = s // SLC_BLOCK
    slc_start = SLC_BLOCK * jnp.arange(n_slc)
    overlap = ((cmp_start[:, None] <= slc_start[None, :] + SLC_BLOCK - 1)
               & (cmp_end[:, None] >= slc_start[None, :])).astype(jnp.float32)
    n_top = min(SLC_TOPK, n_slc)
    k_slc = _rmsnorm(k_slc, k_gain)
    ks_blk = k_slc.reshape(b, n_slc, SLC_BLOCK, g, dh).transpose(0, 3, 1, 2, 4)
    vs_blk = v_slc.reshape(b, n_slc, SLC_BLOCK, g, dh).transpose(0, 3, 1, 2, 4)
    bi = jnp.arange(b)[:, None, None, None]
    gi = jnp.arange(g)[None, :, None, None]
    n_keys = n_top * SLC_BLOCK

    def one_block(i):
        qb = lax.dynamic_slice_in_dim(q, i * Q_BLOCK, Q_BLOCK, axis=1)
        t = i * Q_BLOCK + jnp.arange(Q_BLOCK)
        d_cmp = (t[:, None] - cmp_end[None, :]).astype(jnp.float32)
        s_cmp = jnp.einsum('bqgrd,bngd->bgrqn', qb, kc).astype(jnp.float32) * scale - sl * d_cmp
        p_cmp = _masked_softmax(s_cmp, d_cmp >= 0)
        o_cmp = jnp.einsum('bgrqn,bngd->bqgrd', p_cmp.astype(vc.dtype), vc)
        imp = jnp.einsum('bgrqn,nj->bgqj', p_cmp, overlap)
        cur = (t // SLC_BLOCK)[:, None]
        j = jnp.arange(n_slc)[None, :]
        forced = (j == 0) | (j == cur) | (j == cur - 1)
        imp = jnp.where(j <= cur, jnp.where(forced, jnp.inf, imp), -jnp.inf)
        _, sel = lax.top_k(imp, n_top)
        kg = ks_blk[bi, gi, sel].reshape(b, g, Q_BLOCK, n_keys, dh)
        vg = vs_blk[bi, gi, sel].reshape(b, g, Q_BLOCK, n_keys, dh)
        key_pos = sel[..., None] * SLC_BLOCK + jnp.arange(SLC_BLOCK)
        d_slc = (t[:, None, None] - key_pos).reshape(b, g, Q_BLOCK, n_keys).astype(jnp.float32)[:, :, None]
        s_slc = jnp.einsum('bqgrd,bgqkd->bgrqk', qb, kg).astype(jnp.float32) * scale - sl * d_slc
        p_slc = _masked_softmax(s_slc, d_slc >= 0)
        o_slc = jnp.einsum('bgrqk,bgqkd->bqgrd', p_slc.astype(vg.dtype), vg)
        return o_cmp, o_slc

    o_cmp, o_slc = lax.map(one_block, jnp.arange(s // Q_BLOCK))
    o_cmp = jnp.moveaxis(o_cmp, 0, 1).reshape(b, s, g, r, dh)
    o_slc = jnp.moveaxis(o_slc, 0, 1).reshape(b, s, g, r, dh)
    o_win = _banded_attention(q, _rmsnorm(k_win, k_gain), v_win, C_WINDOW - 1, slopes, 1)
    gate = jax.nn.sigmoid(gate_logits.astype(jnp.float32)).astype(q.dtype).reshape(b, s, g, r, 3)
    o = gate[..., 0:1] * o_cmp + gate[..., 1:2] * o_slc + gate[..., 2:3] * o_win
    return o.reshape(b, s, C_OUT)


def _token_mixing(h, w_in, qk_gain, sinks, cmp_pos, cmp_w1, cmp_w2, w_branch_a, w_branch_b, w_branch_c, w_out):
    b, s, _ = h.shape
    dh = HEAD_DIM
    proj = h @ w_in
    offs, acc = [], 0
    for width in (A_QKV_COLS, B_Q_COLS, B_KV_COLS, C_Q_COLS, C_KV_COLS, C_GATE_COLS):
        acc += width
        offs.append(acc)
    a_qkv, b_q, b_kv, c_q, c_kv, c_gate, br_gate = jnp.split(proj, offs, axis=-1)
    a_qkv = a_qkv.reshape(b, s, 3, A_HEADS, dh)
    o_a = _dilated_mixer(_rmsnorm(a_qkv[:, :, 0], qk_gain[0, 0]), _rmsnorm(a_qkv[:, :, 1], qk_gain[0, 1]),
                         a_qkv[:, :, 2], _alibi_slopes(A_HEADS))
    b_kv = b_kv.reshape(b, s, 2, B_KV_HEADS, dh)
    o_b = _sink_swa_mixer(_rmsnorm(b_q.reshape(b, s, B_HEADS, dh), qk_gain[1, 0]),
                          _rmsnorm(b_kv[:, :, 0], qk_gain[1, 1]), b_kv[:, :, 1], _alibi_slopes(B_HEADS), sinks)
    c_kv = c_kv.reshape(b, s, 6, C_KV_HEADS, dh)
    o_c = _nsa_mixer(c_q.reshape(b, s, C_HEADS, dh), c_kv[:, :, 0], c_kv[:, :, 1], c_kv[:, :, 2], c_kv[:, :, 3],
                     c_kv[:, :, 4], c_kv[:, :, 5], c_gate, _alibi_slopes(C_HEADS), qk_gain[2, 0], qk_gain[2, 1],
                     cmp_pos, cmp_w1, cmp_w2)
    gates = jax.nn.sigmoid(br_gate.astype(jnp.float32)).astype(h.dtype).reshape(b, s, N_BRANCHES, D_MODEL)
    merged = (gates[:, :, 0] * (o_a @ w_branch_a) + gates[:, :, 1] * (o_b @ w_branch_b)
              + gates[:, :, 2] * (o_c @ w_branch_c))
    return merged @ w_out


def setup_inputs(seed: int = 0) -> dict:
    key = jax.random.key(seed)
    ks = jax.random.split(key, 18)
    f32 = jnp.float32

    def dense(k, shape, fan_in):
        return jax.random.normal(k, shape, f32) * (fan_in ** -0.5)

    def gain(k, shape):
        return 1.0 + 0.05 * jax.random.normal(k, shape, f32)

    return {
        'x': jax.random.normal(ks[0], (BATCH, SEQ, D_MODEL), f32),
        'ffn1_norm': gain(ks[1], (DEPTH, D_MODEL)),
        'ffn1_w_gu': dense(ks[2], (DEPTH, D_MODEL, 2 * D_FF), D_MODEL),
        'ffn1_w_down': dense(ks[3], (DEPTH, D_FF, D_MODEL), D_FF),
        'mix_norm': gain(ks[4], (DEPTH, D_MODEL)),
        'w_in': dense(ks[5], (DEPTH, D_MODEL, IN_COLS), D_MODEL),
        'qk_gain': gain(ks[6], (DEPTH, 3, 2, HEAD_DIM)),
        'sinks': 0.5 * jax.random.normal(ks[7], (DEPTH, B_HEADS), f32),
        'cmp_pos': 0.02 * jax.random.normal(ks[8], (DEPTH, 2, CMP_BLOCK, HEAD_DIM), f32),
        'cmp_w1': dense(ks[9], (DEPTH, 2, CMP_BLOCK * HEAD_DIM, CMP_HIDDEN), CMP_BLOCK * HEAD_DIM),
        'cmp_w2': dense(ks[10], (DEPTH, 2, CMP_HIDDEN, HEAD_DIM), CMP_HIDDEN),
        'w_branch_a': dense(ks[11], (DEPTH, A_OUT, D_MODEL), A_OUT),
        'w_branch_b': dense(ks[12], (DEPTH, B_OUT, D_MODEL), B_OUT),
        'w_branch_c': dense(ks[13], (DEPTH, C_OUT, D_MODEL), C_OUT),
        'w_out': dense(ks[14], (DEPTH, D_MODEL, D_MODEL), D_MODEL),
        'ffn2_norm': gain(ks[15], (DEPTH, D_MODEL)),
        'ffn2_w_gu': dense(ks[16], (DEPTH, D_MODEL, 2 * D_FF), D_MODEL),
        'ffn2_w_down': dense(ks[17], (DEPTH, D_FF, D_MODEL), D_FF),
    }


def reference(x, ffn1_norm, ffn1_w_gu, ffn1_w_down, mix_norm, w_in, qk_gain, sinks, cmp_pos, cmp_w1, cmp_w2,
              w_branch_a, w_branch_b, w_branch_c, w_out, ffn2_norm, ffn2_w_gu, ffn2_w_down):
    for l in range(DEPTH):
        x = x + 0.5 * _swiglu(_rmsnorm(x, ffn1_norm[l]), ffn1_w_gu[l], ffn1_w_down[l])
        x = x + _token_mixing(_rmsnorm(x, mix_norm[l]), w_in[l], qk_gain[l], sinks[l], cmp_pos[l], cmp_w1[l],
                              cmp_w2[l], w_branch_a[l], w_branch_b[l], w_branch_c[l], w_out[l])
        x = x + 0.5 * _swiglu(_rmsnorm(x, ffn2_norm[l]), ffn2_w_gu[l], ffn2_w_down[l])
    return x
```

```python
import functools
import math

import numpy as np
import jax
import jax.numpy as jnp
from jax import lax
from jax.experimental import pallas as pl
from jax.experimental.pallas import tpu as pltpu

F32 = jnp.float32
BF16 = jnp.bfloat16

HEAD_DIM = 64
Q_BLOCK = 128
LANES = 128
A_GROUPS = ((128, 1), (512, 4), (2048, 16))
A_HEADS_PER_GROUP = 4
A_HEADS = 12
B_HEADS = 8
B_KV_HEADS = 2
B_WINDOW = 128
C_HEADS = 12
C_KV_HEADS = 2
C_REP = C_HEADS // C_KV_HEADS
CMP_BLOCK = 32
CMP_STRIDE = 16
CMP_HIDDEN = 256
SLC_BLOCK = 64
SLC_TOPK = 16
C_WINDOW = 512
RMS_EPS = 1e-6
NEG_INF = -1e30
SEC = 768
N_SEC = 6
QKV_COLS = SEC * N_SEC
VMEM_LIMIT = 56 * 1024 * 1024


def _slopes(n):
    return [float(2.0 ** (-8.0 * (h + 1) / n)) for h in range(n)]


def _cparams(sem):
    return pltpu.CompilerParams(dimension_semantics=sem, vmem_limit_bytes=VMEM_LIMIT)


def _nt_dot(a, b):
    return lax.dot_general(a, b, (((1,), (1,)), ((), ())), preferred_element_type=F32)


def _row_tile(t, want):
    while t % want:
        want //= 2
    return want


def _ffn_body(x_ref, g_ref, wg_ref, wu_ref, wd_ref, o_ref, h_ref, *, n_f):
    f = pl.program_id(1)

    @pl.when(f == 0)
    def _():
        x = x_ref[...]
        ms = jnp.mean(x * x, axis=-1, keepdims=True)
        h_ref[...] = (x * lax.rsqrt(ms + RMS_EPS) * g_ref[...]).astype(BF16)
        o_ref[...] = jnp.zeros_like(o_ref)

    h = h_ref[...]
    gate = jnp.dot(h, wg_ref[...], preferred_element_type=F32)
    up = jnp.dot(h, wu_ref[...], preferred_element_type=F32)
    act = (gate * jax.nn.sigmoid(gate) * up).astype(BF16)
    o_ref[...] += jnp.dot(act, wd_ref[...], preferred_element_type=F32)

    @pl.when(f == n_f - 1)
    def _():
        o_ref[...] = x_ref[...] + 0.5 * o_ref[...]


def _ffn(x, g, w_gu, w_down, *, tm=512, tf=512):
    t, d = x.shape
    d_ff = w_down.shape[0]
    tm = _row_tile(t, tm)
    n_f = d_ff // tf
    return pl.pallas_call(
        functools.partial(_ffn_body, n_f=n_f),
        grid=(t // tm, n_f),
        in_specs=[
            pl.BlockSpec((tm, d), lambda i, f: (i, 0)),
            pl.BlockSpec((1, d), lambda i, f: (0, 0)),
            pl.BlockSpec((d, tf), lambda i, f: (0, f)),
            pl.BlockSpec((d, tf), lambda i, f: (0, f + n_f)),
            pl.BlockSpec((tf, d), lambda i, f: (f, 0)),
        ],
        out_specs=pl.BlockSpec((tm, d), lambda i, f: (i, 0)),
        out_shape=jax.ShapeDtypeStruct((t, d), F32),
        scratch_shapes=[pltpu.VMEM((tm, d), BF16)],
        compiler_params=_cparams(("parallel", "arbitrary")),
    )(x, g.reshape(1, d), w_gu, w_gu, w_down)


def _proj_body(x_ref, g_ref, w_ref, *rest, epilogue, tn):
    if epilogue == "headnorm":
        flag_ref, gs_ref, bd_ref, o_ref, h_ref = rest
    else:
        o_ref, h_ref = rest

    @pl.when(pl.program_id(1) == 0)
    def _():
        x = x_ref[...]
        ms = jnp.mean(x * x, axis=-1, keepdims=True)
        h_ref[...] = (x * lax.rsqrt(ms + RMS_EPS) * g_ref[...]).astype(BF16)

    y = jnp.dot(h_ref[...], w_ref[...], preferred_element_type=F32)
    if epilogue == "sigmoid":
        o_ref[...] = jax.nn.sigmoid(y).astype(o_ref.dtype)
        return
    sq = y * y
    hi = sq.astype(BF16)
    lo = (sq - hi.astype(F32)).astype(BF16)
    bd = bd_ref[...]
    parts = []
    for c in range(tn // LANES):
        sl = slice(c * LANES, (c + 1) * LANES)
        parts.append(jnp.dot(hi[:, sl], bd, preferred_element_type=F32)
                     + jnp.dot(lo[:, sl], bd, preferred_element_type=F32))
    ss = jnp.concatenate(parts, axis=1)
    inv = jnp.where(flag_ref[...] > 0, lax.rsqrt(ss * (1.0 / HEAD_DIM) + RMS_EPS), 1.0)
    o_ref[...] = (y * inv * gs_ref[...]).astype(o_ref.dtype)


def _proj(x, g, w, *, epilogue, out_dtype, tn, tm=512, flag=None, gs=None):
    t, d = x.shape
    n = w.shape[1]
    tm = _row_tile(t, tm)
    in_specs = [
        pl.BlockSpec((tm, d), lambda i, j: (i, 0)),
        pl.BlockSpec((1, d), lambda i, j: (0, 0)),
        pl.BlockSpec((d, tn), lambda i, j: (0, j)),
    ]
    args = [x, g.reshape(1, d), w]
    if epilogue == "headnorm":
        bd = jnp.asarray(np.kron(np.eye(LANES // HEAD_DIM), np.ones((HEAD_DIM, HEAD_DIM))), BF16)
        in_specs += [
            pl.BlockSpec((1, tn), lambda i, j: (0, j)),
            pl.BlockSpec((1, tn), lambda i, j: (0, j)),
            pl.BlockSpec((LANES, LANES), lambda i, j: (0, 0)),
        ]
        args += [flag.reshape(1, n), gs.reshape(1, n), bd]
    return pl.pallas_call(
        functools.partial(_proj_body, epilogue=epilogue, tn=tn),
        grid=(t // tm, n // tn),
        in_specs=in_specs,
        out_specs=pl.BlockSpec((tm, tn), lambda i, j: (i, j)),
        out_shape=jax.ShapeDtypeStruct((t, n), out_dtype),
        scratch_shapes=[pltpu.VMEM((tm, d), BF16)],
        compiler_params=_cparams(("parallel", "arbitrary")),
    )(*args)


def _banded_body(*refs, nb, max_dist, heads, k_off, v_off, n_pairs, q_axis, use_sinks, with_lse):
    refs = list(refs)
    q_ref = refs.pop(0)
    kv_refs = [refs.pop(0) for _ in range(nb + 1)]
    sink_ref = refs.pop(0) if use_sinks else None
    o_ref = refs.pop(0)
    lse_ref = refs.pop(0) if with_lse else None

    i = pl.program_id(q_axis)
    nk = (nb + 1) * Q_BLOCK
    row = lax.broadcasted_iota(jnp.int32, (Q_BLOCK, nk), 0)
    col = lax.broadcasted_iota(jnp.int32, (Q_BLOCK, nk), 1)
    rel = nb * Q_BLOCK + row - col
    valid = (rel >= 0) & (rel <= max_dist) & (col >= (nb - i) * Q_BLOCK)
    rel_f = rel.astype(F32)
    neg_mask = jnp.where(valid, 0.0, NEG_INF)
    lane = lax.broadcasted_iota(jnp.int32, (Q_BLOCK, LANES), 1)
    low_half = lane < HEAD_DIM

    kv_cache = {}

    def kv_tile(off, kv_pair, swapped):
        key = (off, kv_pair, swapped)
        if key not in kv_cache:
            c0 = off + kv_pair * LANES
            tile = jnp.concatenate([r[:, c0:c0 + LANES] for r in kv_refs], axis=0)
            if swapped:
                tile = pltpu.roll(tile.astype(F32), HEAD_DIM, 1).astype(BF16)
            kv_cache[key] = tile
        return kv_cache[key]

    outs = [[None, None] for _ in range(n_pairs)]
    lses = [[None, None] for _ in range(n_pairs)]
    for pair, half, kv_pair, kv_half, slope, hidx in heads:
        qp = q_ref[:, pair * LANES:(pair + 1) * LANES]
        own = low_half if half == 0 else jnp.logical_not(low_half)
        qm = jnp.where(own, qp, jnp.zeros_like(qp))
        swapped = kv_half != half
        s = _nt_dot(qm, kv_tile(k_off, kv_pair, swapped))
        s = s - slope * rel_f + neg_mask
        m = jnp.max(s, axis=1, keepdims=True)
        if use_sinks:
            m = jnp.maximum(m, sink_ref[hidx])
        p = jnp.exp(s - m)
        den = jnp.sum(p, axis=1, keepdims=True)
        if use_sinks:
            den = den + jnp.exp(sink_ref[hidx] - m)
        r = jnp.dot(p.astype(BF16), kv_tile(v_off, kv_pair, swapped), preferred_element_type=F32)
        outs[pair][half] = r / den
        if with_lse:
            lses[pair][half] = m + jnp.log(den)
    for pair in range(n_pairs):
        sl = slice(pair * LANES, (pair + 1) * LANES)
        o_ref[:, sl] = jnp.where(low_half, outs[pair][0], outs[pair][1])
        if with_lse:
            lse_ref[:, sl] = jnp.where(low_half, lses[pair][0], lses[pair][1])


def _banded_call(qkv_view, *, grid, q_map, kv_map, out_map, out_rows, out_cols, nb, max_dist, heads,
                 k_off, v_off, q_axis, sinks=None, with_lse=False):
    b = qkv_view.shape[0]
    blk = (None, Q_BLOCK, SEC)
    in_specs = [pl.BlockSpec(blk, q_map)]
    args = [qkv_view]
    for back in range(nb, -1, -1):
        in_specs.append(pl.BlockSpec(blk, functools.partial(kv_map, back=back)))
        args.append(qkv_view)
    if sinks is not None:
        in_specs.append(pl.BlockSpec(memory_space=pltpu.SMEM))
        args.append(sinks)
    oblk = pl.BlockSpec((None, Q_BLOCK, out_cols), out_map)
    oshape = jax.ShapeDtypeStruct((b, out_rows, out_cols * (qkv_view.shape[2] // QKV_COLS)), F32)
    n_pairs = out_cols // LANES
    body = functools.partial(_banded_body, nb=nb, max_dist=max_dist, heads=heads, k_off=k_off, v_off=v_off,
                             n_pairs=n_pairs, q_axis=q_axis, use_sinks=sinks is not None, with_lse=with_lse)
    return pl.pallas_call(
        body,
        grid=grid,
        in_specs=in_specs,
        out_specs=[oblk, oblk] if with_lse else oblk,
        out_shape=[oshape, oshape] if with_lse else oshape,
        compiler_params=_cparams(("parallel",) * len(grid)),
    )(*args)


def _dilated_group(qkv, gi, b, s):
    window, dil = A_GROUPS[gi]
    rows = s // dil
    view = qkv.reshape(b, rows, dil * QKV_COLS)
    slopes = _slopes(A_HEADS)
    heads = []
    for hh in range(A_HEADS_PER_GROUP):
        heads.append((hh // 2, hh % 2, hh // 2, hh % 2, slopes[gi * A_HEADS_PER_GROUP + hh] * dil, hh))
    o, lse = _banded_call(
        view, grid=(b, dil, rows // Q_BLOCK),
        q_map=lambda bb, c, i: (bb, i, c * N_SEC + gi),
        kv_map=lambda bb, c, i, back: (bb, jnp.maximum(i - back, 0), c * N_SEC + gi),
        out_map=lambda bb, c, i: (bb, i, c),
        out_rows=rows, out_cols=A_HEADS_PER_GROUP * HEAD_DIM,
        nb=1, max_dist=window // dil, heads=tuple(heads), k_off=256, v_off=512, q_axis=2, with_lse=True)
    return o.reshape(b * s, -1), lse.reshape(b * s, -1)


def _sink_swa(qkv, sinks, b, s):
    view = qkv.reshape(b, s, QKV_COLS)
    slopes = _slopes(B_HEADS)
    rep = B_HEADS // B_KV_HEADS
    heads = tuple((h // 2, h % 2, 0, h // rep, slopes[h], h) for h in range(B_HEADS))
    o = _banded_call(
        view, grid=(b, s // Q_BLOCK),
        q_map=lambda bb, i: (bb, i, 3),
        kv_map=lambda bb, i, back: (bb, jnp.maximum(i - back, 0), 3),
        out_map=lambda bb, i: (bb, i, 0),
        out_rows=s, out_cols=B_HEADS * HEAD_DIM,
        nb=1, max_dist=B_WINDOW - 1, heads=heads, k_off=512, v_off=640, q_axis=1, sinks=sinks)
    return o.reshape(b * s, -1)


def _nsa_window(qkv, b, s):
    view = qkv.reshape(b, s, QKV_COLS)
    slopes = _slopes(C_HEADS)
    heads = tuple((h // 2, h % 2, 0, h // C_REP, slopes[h], h) for h in range(C_HEADS))
    nb = -(-(C_WINDOW - 1) // Q_BLOCK)
    o = _banded_call(
        view, grid=(b, s // Q_BLOCK),
        q_map=lambda bb, i: (bb, i, 4),
        kv_map=lambda bb, i, back: (bb, jnp.maximum(i - back, 0), 5),
        out_map=lambda bb, i: (bb, i, 0),
        out_rows=s, out_cols=C_HEADS * HEAD_DIM,
        nb=nb, max_dist=C_WINDOW - 1, heads=heads, k_off=512, v_off=640, q_axis=1)
    return o.reshape(b * s, -1)


def _compress_body(ck_ref, cv_ref, w1_ref, w2k_ref, w2vt_ref, pos_ref, kg_ref, kc_ref, vct_ref, *, n_chunks):
    half = CMP_STRIDE * HEAD_DIM

    def hidden(c_ref, which):
        c = c_ref[...]
        w1 = w1_ref[which]
        u = jnp.dot(c, w1[:half], preferred_element_type=F32)
        v = jnp.dot(c, w1[half:], preferred_element_type=F32)
        pc = jnp.dot(pos_ref[which], w1, preferred_element_type=F32)[0:1]
        hsum = u + pltpu.roll(v, n_chunks - 1, 0) + pc
        return (hsum * jax.nn.sigmoid(hsum)).astype(BF16)

    k = jnp.dot(hidden(ck_ref, 0), w2k_ref[...], preferred_element_type=F32)
    k = k * lax.rsqrt(jnp.mean(k * k, axis=-1, keepdims=True) + RMS_EPS) * kg_ref[...]
    kc_ref[...] = k.astype(BF16)
    vct_ref[...] = _nt_dot(w2vt_ref[...], hidden(cv_ref, 1)).astype(BF16)


def _compress(chunks, w1, w2k, w2vt, pos, k_gain):
    b, _, g, n_chunks, width = chunks.shape
    return pl.pallas_call(
        functools.partial(_compress_body, n_chunks=n_chunks),
        grid=(b, g),
        in_specs=[
            pl.BlockSpec((None, None, None, n_chunks, width), lambda bb, gg: (bb, 0, gg, 0, 0)),
            pl.BlockSpec((None, None, None, n_chunks, width), lambda bb, gg: (bb, 1, gg, 0, 0)),
            pl.BlockSpec((2, 2 * width, CMP_HIDDEN), lambda bb, gg: (0, 0, 0)),
            pl.BlockSpec((CMP_HIDDEN, HEAD_DIM), lambda bb, gg: (0, 0)),
            pl.BlockSpec((HEAD_DIM, CMP_HIDDEN), lambda bb, gg: (0, 0)),
            pl.BlockSpec((2, 8, 2 * width), lambda bb, gg: (0, 0, 0)),
            pl.BlockSpec((1, HEAD_DIM), lambda bb, gg: (0, 0)),
        ],
        out_specs=[
            pl.BlockSpec((None, None, n_chunks, HEAD_DIM), lambda bb, gg: (bb, gg, 0, 0)),
            pl.BlockSpec((None, None, HEAD_DIM, n_chunks), lambda bb, gg: (bb, gg, 0, 0)),
        ],
        out_shape=[
            jax.ShapeDtypeStruct((b, g, n_chunks, HEAD_DIM), BF16),
            jax.ShapeDtypeStruct((b, g, HEAD_DIM, n_chunks), BF16),
        ],
        compiler_params=_cparams(("parallel", "parallel")),
    )(chunks, chunks, w1, w2k, w2vt, pos, k_gain.reshape(1, HEAD_DIM))


def _cmp_body(qt_ref, kc_ref, vct_ref, ovt_ref, ot_ref, sel_ref, cnt_ref, *, n_cmp, n_top):
    g = pl.program_id(1)
    i = pl.program_id(2)
    n_pad = kc_ref.shape[0]
    n_slc = ovt_ref.shape[0]
    kc = kc_ref[...]
    vct = vct_ref[...]
    n_idx = lax.broadcasted_iota(jnp.int32, (n_pad, Q_BLOCK), 0)
    t_idx = i * Q_BLOCK + lax.broadcasted_iota(jnp.int32, (n_pad, Q_BLOCK), 1)
    d_cmp = t_idx - (CMP_STRIDE * n_idx + CMP_BLOCK - 1)
    valid = (d_cmp >= 0) & (n_idx < n_cmp)
    d_f = d_cmp.astype(F32)
    slopes = _slopes(C_HEADS)
    psum = jnp.zeros((n_pad, Q_BLOCK), F32)
    for r in range(C_REP):
        slope = jnp.where(g == 0, slopes[r], slopes[C_REP + r])
        qt = qt_ref[r * HEAD_DIM:(r + 1) * HEAD_DIM, :]
        s = jnp.dot(kc, qt, preferred_element_type=F32) - slope * d_f
        s = jnp.where(valid, s, NEG_INF)
        m = jnp.max(s, axis=0, keepdims=True)
        e = jnp.where(valid, jnp.exp(s - m), 0.0)
        den = jnp.sum(e, axis=0, keepdims=True)
        p = e / jnp.where(den > 0, den, 1.0)
        psum = psum + p
        ot_ref[r * HEAD_DIM:(r + 1) * HEAD_DIM, :] = jnp.dot(vct, p.astype(BF16), preferred_element_type=F32)
    hi = psum.astype(BF16)
    lo = (psum - hi.astype(F32)).astype(BF16)
    ovt = ovt_ref[...]
    imp = jnp.dot(ovt, hi, preferred_element_type=F32) + jnp.dot(ovt, lo, preferred_element_type=F32)
    j_idx = lax.broadcasted_iota(jnp.int32, (n_slc, Q_BLOCK), 0)
    t_q = i * Q_BLOCK + lax.broadcasted_iota(jnp.int32, (n_slc, Q_BLOCK), 1)
    cur = lax.shift_right_logical(t_q, int(math.log2(SLC_BLOCK)))
    forced = (j_idx == 0) | (j_idx == cur) | (j_idx == cur - 1)
    v = jnp.where(j_idx <= cur, jnp.where(forced, jnp.inf, imp), -1.0)
    sel = jnp.zeros((n_slc, Q_BLOCK), F32)
    for _ in range(n_top):
        m = jnp.max(v, axis=0, keepdims=True)
        first = jnp.min(jnp.where((v == m) & (m >= 0.0), j_idx, n_slc), axis=0, keepdims=True)
        pick = j_idx == first
        sel = jnp.where(pick, 1.0, sel)
        v = jnp.where(pick, -1.0, v)
    sel_b = sel.astype(BF16)
    sel_ref[...] = sel_b
    cnt_ref[...] = _nt_dot(jnp.ones((8, Q_BLOCK), BF16), sel_b)


def _cmp_select(qt, kc, vct, ovt, *, n_cmp):
    b, _, s = qt.shape
    g = kc.shape[1]
    n_pad = kc.shape[2]
    n_slc = ovt.shape[0]
    nq = s // Q_BLOCK
    rows = C_REP * HEAD_DIM
    return pl.pallas_call(
        functools.partial(_cmp_body, n_cmp=n_cmp, n_top=min(SLC_TOPK, n_slc)),
        grid=(b, g, nq),
        in_specs=[
            pl.BlockSpec((None, rows, Q_BLOCK), lambda bb, gg, i: (bb, gg, i)),
            pl.BlockSpec((None, None, n_pad, HEAD_DIM), lambda bb, gg, i: (bb, gg, 0, 0)),
            pl.BlockSpec((None, None, HEAD_DIM, n_pad), lambda bb, gg, i: (bb, gg, 0, 0)),
            pl.BlockSpec((n_slc, n_pad), lambda bb, gg, i: (0, 0)),
        ],
        out_specs=[
            pl.BlockSpec((None, rows, Q_BLOCK), lambda bb, gg, i: (bb, gg, i)),
            pl.BlockSpec((None, None, None, n_slc, Q_BLOCK), lambda bb, gg, i: (bb, gg, i, 0, 0)),
            pl.BlockSpec((None, None, None, 8, n_slc), lambda bb, gg, i: (bb, gg, i, 0, 0)),
        ],
        out_shape=[
            jax.ShapeDtypeStruct((b, g * rows, s), F32),
            jax.ShapeDtypeStruct((b, g, nq, n_slc, Q_BLOCK), BF16),
            jax.ShapeDtypeStruct((b, g, nq, 8, n_slc), F32),
        ],
        compiler_params=_cparams(("parallel", "parallel", "parallel")),
    )(qt, kc, vct, ovt)


def _slc_body(bits_ref, qt_ref, ks_ref, vst_ref, sel_ref, ot_ref, m_sc, l_sc, acc_sc, *, nq, words):
    bb = pl.program_id(0)
    g = pl.program_id(1)
    i = pl.program_id(2)
    width = C_REP * Q_BLOCK
    n_slc = sel_ref.shape[0]
    q6 = jnp.concatenate([qt_ref[r * HEAD_DIM:(r + 1) * HEAD_DIM, :] for r in range(C_REP)], axis=1)
    sel = sel_ref[...]
    slopes = _slopes(C_HEADS)
    head = lax.broadcasted_iota(jnp.int32, (1, width), 1) // Q_BLOCK
    slope_row = jnp.zeros((1, width), F32)
    for r in range(C_REP):
        slope_row = jnp.where(head == r, jnp.where(g == 0, slopes[r], slopes[C_REP + r]), slope_row)
    lane_q = lax.broadcasted_iota(jnp.int32, (Q_BLOCK, width), 1) % Q_BLOCK
    key_r = lax.broadcasted_iota(jnp.int32, (Q_BLOCK, width), 0)
    d0 = lane_q - key_r
    exp_row = lax.broadcasted_iota(jnp.int32, (Q_BLOCK, n_slc), 0) // SLC_BLOCK
    exp_col = lax.broadcasted_iota(jnp.int32, (Q_BLOCK, n_slc), 1)

    m_sc[...] = jnp.full(m_sc.shape, NEG_INF, F32)
    l_sc[...] = jnp.zeros(l_sc.shape, F32)
    acc_sc[...] = jnp.zeros(acc_sc.shape, F32)
    base = ((bb * pl.num_programs(1) + g) * nq + i) * words

    def step(jj, carry):
        word = bits_ref[base + jj // 32]
        active = lax.shift_right_logical(word, jj % 32) & 1

        @pl.when(active == 1)
        def _():
            expand = (exp_col == 2 * jj + exp_row).astype(BF16)
            picked = jnp.dot(expand, sel, preferred_element_type=F32)
            picked = jnp.concatenate([picked] * C_REP, axis=1)
            d = d0 + (i - jj) * Q_BLOCK
            ok = (picked > 0.5) & (d >= 0)
            s = jnp.dot(ks_ref[jj], q6, preferred_element_type=F32) - slope_row * d.astype(F32)
            s = jnp.where(ok, s, NEG_INF)
            m_old = m_sc[...]
            m_new = jnp.maximum(m_old, jnp.max(s, axis=0, keepdims=True))
            alpha = jnp.exp(m_old - m_new)
            p = jnp.where(ok, jnp.exp(s - m_new), 0.0)
            l_sc[...] = alpha * l_sc[...] + jnp.sum(p, axis=0, keepdims=True)
            acc_sc[...] = alpha * acc_sc[...] + jnp.dot(vst_ref[jj], p.astype(BF16), preferred_element_type=F32)
            m_sc[...] = m_new

        return carry

    lax.fori_loop(0, i + 1, step, 0)
    l = l_sc[...]
    o = acc_sc[...] / jnp.where(l > 0, l, 1.0)
    for r in range(C_REP):
        ot_ref[r * HEAD_DIM:(r + 1) * HEAD_DIM, :] = o[:, r * Q_BLOCK:(r + 1) * Q_BLOCK]


def _slc_attention(bits, qt, ks, vst, sel, *, words):
    b, _, s = qt.shape
    g, nkb = ks.shape[1], ks.shape[2]
    nq = s // Q_BLOCK
    n_slc = sel.shape[3]
    rows = C_REP * HEAD_DIM
    width = C_REP * Q_BLOCK
    grid_spec = pltpu.PrefetchScalarGridSpec(
        num_scalar_prefetch=1,
        grid=(b, g, nq),
        in_specs=[
            pl.BlockSpec((None, rows, Q_BLOCK), lambda bb, gg, i, bits: (bb, gg, i)),
            pl.BlockSpec((None, None, nkb, Q_BLOCK, HEAD_DIM), lambda bb, gg, i, bits: (bb, gg, 0, 0, 0)),
            pl.BlockSpec((None, None, nkb, HEAD_DIM, Q_BLOCK), lambda bb, gg, i, bits: (bb, gg, 0, 0, 0)),
            pl.BlockSpec((None, None, None, n_slc, Q_BLOCK), lambda bb, gg, i, bits: (bb, gg, i, 0, 0)),
        ],
        out_specs=pl.BlockSpec((None, rows, Q_BLOCK), lambda bb, gg, i, bits: (bb, gg, i)),
        scratch_shapes=[
            pltpu.VMEM((1, width), F32),
            pltpu.VMEM((1, width), F32),
            pltpu.VMEM((HEAD_DIM, width), F32),
        ],
    )
    return pl.pallas_call(
        functools.partial(_slc_body, nq=nq, words=words),
        grid_spec=grid_spec,
        out_shape=jax.ShapeDtypeStruct((b, g * rows, s), F32),
        compiler_params=_cparams(("parallel", "parallel", "parallel")),
    )(bits, qt, ks, vst, sel)


def _merge_body(x_ref, oa0, oa1, oa2, la0, la1, la2, ob_ref, ocmp_ref, oslc_ref, owin_ref, cg_ref, gates_ref,
                wa_ref, wb_ref, wc_ref, wo_ref, ex_ref, out_ref):
    d = out_ref.shape[1]
    l0, l1, l2 = la0[...], la1[...], la2[...]
    mx = jnp.maximum(jnp.maximum(l0, l1), l2)
    e0, e1, e2 = jnp.exp(l0 - mx), jnp.exp(l1 - mx), jnp.exp(l2 - mx)
    o_a = (e0 * oa0[...] + e1 * oa1[...] + e2 * oa2[...]) / (e0 + e1 + e2)
    cg = cg_ref[...]
    cg_hi = cg.astype(BF16)
    cg_lo = (cg - cg_hi.astype(F32)).astype(BF16)
    o_c = None
    for w, o_ref in enumerate((ocmp_ref, oslc_ref, owin_ref)):
        ex = ex_ref[w]
        gw = jnp.dot(cg_hi, ex, preferred_element_type=F32) + jnp.dot(cg_lo, ex, preferred_element_type=F32)
        term = gw * o_ref[...]
        o_c = term if o_c is None else o_c + term
    merged = gates_ref[:, 0:d] * jnp.dot(o_a.astype(BF16), wa_ref[...], preferred_element_type=F32)
    merged += gates_ref[:, d:2 * d] * jnp.dot(ob_ref[...].astype(BF16), wb_ref[...], preferred_element_type=F32)
    merged += gates_ref[:, 2 * d:3 * d] * jnp.dot(o_c.astype(BF16), wc_ref[...], preferred_element_type=F32)
    out_ref[...] = x_ref[...] + jnp.dot(merged.astype(BF16), wo_ref[...], preferred_element_type=F32)


def _merge(x, oa, la, ob, ocmp, oslc, owin, cg, gates, wa, wb, wc, wo, ex, *, tm=256):
    t, d = x.shape
    tm = _row_tile(t, tm)

    def rows(a):
        return pl.BlockSpec((tm, a.shape[1]), lambda i: (i, 0))

    def whole(a):
        return pl.BlockSpec(a.shape, lambda i: (0,) * a.ndim, pipeline_mode=pl.Buffered(1))

    acts = [x, *oa, *la, ob, ocmp, oslc, owin, cg, gates]
    weights = [wa, wb, wc, wo, ex]
    return pl.pallas_call(
        _merge_body,
        grid=(t // tm,),
        in_specs=[rows(a) for a in acts] + [whole(w) for w in weights],
        out_specs=pl.BlockSpec((tm, d), lambda i: (i, 0)),
        out_shape=jax.ShapeDtypeStruct((t, d), F32),
        compiler_params=_cparams(("parallel",)),
    )(*acts, *weights)


def _qkv_columns():
    a_cols = 3 * A_HEADS * HEAD_DIM
    order, normed, is_q, gain_id = [], [], [], []
    for gi in range(len(A_GROUPS)):
        for which in range(3):
            for hh in range(A_HEADS_PER_GROUP):
                h = gi * A_HEADS_PER_GROUP + hh
                for dd in range(HEAD_DIM):
                    order.append(which * A_HEADS * HEAD_DIM + h * HEAD_DIM + dd)
                    normed.append(which < 2)
                    is_q.append(which == 0)
                    gain_id.append((0, min(which, 1)))
    col = a_cols
    for width, nrm, q, gid in (
        (B_HEADS * HEAD_DIM, True, True, (1, 0)),
        (B_KV_HEADS * HEAD_DIM, True, False, (1, 1)),
        (B_KV_HEADS * HEAD_DIM, False, False, (1, 1)),
        (C_HEADS * HEAD_DIM, True, True, (2, 0)),
        (C_KV_HEADS * HEAD_DIM, False, False, (2, 1)),
        (C_KV_HEADS * HEAD_DIM, False, False, (2, 1)),
        (C_KV_HEADS * HEAD_DIM, True, False, (2, 1)),
        (C_KV_HEADS * HEAD_DIM, False, False, (2, 1)),
        (C_KV_HEADS * HEAD_DIM, True, False, (2, 1)),
        (C_KV_HEADS * HEAD_DIM, False, False, (2, 1)),
    ):
        for k in range(width):
            order.append(col + k)
            normed.append(nrm)
            is_q.append(q)
            gain_id.append(gid)
        col += width
    assert col == QKV_COLS and len(order) == QKV_COLS
    return (np.asarray(order, np.int32), np.asarray(normed, np.float32), np.asarray(is_q, np.float32),
            np.asarray(gain_id, np.int32))


def _overlap_t(n_slc, n_pad, n_cmp):
    n = np.arange(n_pad)[None, :]
    j = np.arange(n_slc)[:, None]
    start, end = CMP_STRIDE * n, CMP_STRIDE * n + CMP_BLOCK - 1
    ov = (start <= SLC_BLOCK * j + SLC_BLOCK - 1) & (end >= SLC_BLOCK * j) & (n < n_cmp)
    return jnp.asarray(ov, BF16)


def _gate_expand():
    ex = np.zeros((3, LANES, C_HEADS * HEAD_DIM), np.float32)
    for w in range(3):
        for h in range(C_HEADS):
            ex[w, h * 3 + w, h * HEAD_DIM:(h + 1) * HEAD_DIM] = 1.0
    return jnp.asarray(ex, BF16)


def _token_mixing(x, b, s, mix_norm, w_in, qk_gain, sinks, cmp_pos, cmp_w1, cmp_w2, w_a, w_b, w_c, w_out):
    t, d = x.shape
    order, normed, is_q, gain_id = _qkv_columns()
    c_gate_cols = 3 * C_HEADS
    a_cols = 3 * A_HEADS * HEAD_DIM
    w_a_sec = w_in[:, :a_cols].reshape(d, 3, len(A_GROUPS), A_HEADS_PER_GROUP * HEAD_DIM)
    w_a_sec = w_a_sec.transpose(0, 2, 1, 3).reshape(d, a_cols)
    assert np.array_equal(order[:a_cols].reshape(len(A_GROUPS), 3, -1),
                          np.arange(a_cols).reshape(3, len(A_GROUPS), -1).transpose(1, 0, 2))
    w_qkv = jnp.concatenate([w_a_sec, w_in[:, a_cols:QKV_COLS]], axis=1).astype(BF16)
    gain_cols = qk_gain[gain_id[:, 0], gain_id[:, 1]][np.arange(QKV_COLS), np.arange(QKV_COLS) % HEAD_DIM]
    gs = jnp.where(jnp.asarray(normed) > 0, gain_cols, 1.0) * jnp.where(jnp.asarray(is_q) > 0, HEAD_DIM ** -0.5, 1.0)
    qkv = _proj(x, mix_norm, w_qkv, epilogue="headnorm", out_dtype=BF16, tn=512,
                flag=jnp.asarray(normed), gs=gs.astype(F32))
    w_cg = jnp.pad(w_in[:, QKV_COLS:QKV_COLS + c_gate_cols], ((0, 0), (0, LANES - c_gate_cols))).astype(BF16)
    cg = _proj(x, mix_norm, w_cg, epilogue="sigmoid", out_dtype=F32, tn=LANES)
    gates = _proj(x, mix_norm, w_in[:, QKV_COLS + c_gate_cols:].astype(BF16), epilogue="sigmoid", out_dtype=F32,
                  tn=512)

    a_out = [_dilated_group(qkv, gi, b, s) for gi in range(len(A_GROUPS))]
    o_b = _sink_swa(qkv, sinks.astype(F32), b, s)
    o_win = _nsa_window(qkv, b, s)

    q3 = qkv.reshape(b, s, QKV_COLS)
    n_chunks = s // CMP_STRIDE
    n_cmp = (s - CMP_BLOCK) // CMP_STRIDE + 1
    c_kv = q3[:, :, 5 * SEC:5 * SEC + 2 * C_KV_HEADS * HEAD_DIM]
    chunks = c_kv.reshape(b, n_chunks, CMP_STRIDE, 2, C_KV_HEADS, HEAD_DIM).transpose(0, 3, 4, 1, 2, 5)
    chunks = chunks.reshape(b, 2, C_KV_HEADS, n_chunks, CMP_STRIDE * HEAD_DIM)
    pos = jnp.broadcast_to(cmp_pos.reshape(2, 1, CMP_BLOCK * HEAD_DIM), (2, 8, CMP_BLOCK * HEAD_DIM)).astype(BF16)
    kc, vct = _compress(chunks, cmp_w1.astype(BF16), cmp_w2[0].astype(BF16), cmp_w2[1].T.astype(BF16), pos,
                        qk_gain[2, 1])

    n_slc = s // SLC_BLOCK
    qt = q3[:, :, 4 * SEC:5 * SEC].transpose(0, 2, 1)
    ocmp_t, sel, cnt = _cmp_select(qt, kc, vct, _overlap_t(n_slc, n_chunks, n_cmp), n_cmp=n_cmp)

    nq = s // Q_BLOCK
    act = (cnt[:, :, :, 0, :] > 0).reshape(b, C_KV_HEADS, nq, nq, 2).any(axis=-1)
    words = -(-nq // 32)
    act = jnp.pad(act, ((0, 0), (0, 0), (0, 0), (0, words * 32 - nq))).reshape(b, C_KV_HEADS, nq, words, 32)
    bits = jnp.sum(act.astype(jnp.uint32) << jnp.arange(32, dtype=jnp.uint32), axis=-1, dtype=jnp.uint32)
    bits = lax.bitcast_convert_type(bits, jnp.int32).reshape(-1)

    ks = q3[:, :, 5 * SEC + 256:5 * SEC + 384].reshape(b, nq, Q_BLOCK, C_KV_HEADS, HEAD_DIM).transpose(0, 3, 1, 2, 4)
    vst = q3[:, :, 5 * SEC + 384:5 * SEC + 512].reshape(b, nq, Q_BLOCK, C_KV_HEADS, HEAD_DIM).transpose(0, 3, 1, 4, 2)
    oslc_t = _slc_attention(bits, qt, ks, vst, sel, words=words)
    o_cmp = ocmp_t.transpose(0, 2, 1).reshape(t, -1)
    o_slc = oslc_t.transpose(0, 2, 1).reshape(t, -1)

    return _merge(x, [o for o, _ in a_out], [l for _, l in a_out], o_b, o_cmp, o_slc, o_win, cg, gates,
                  w_a.astype(BF16), w_b.astype(BF16), w_c.astype(BF16), w_out.astype(BF16), _gate_expand())


def kernel(x, ffn1_norm, ffn1_w_gu, ffn1_w_down, mix_norm, w_in, qk_gain, sinks, cmp_pos, cmp_w1, cmp_w2,
           w_branch_a, w_branch_b, w_branch_c, w_out, ffn2_norm, ffn2_w_gu, ffn2_w_down):
    b, s, d = x.shape
    h = x.reshape(b * s, d)
    for l in range(ffn1_norm.shape[0]):
        h = _ffn(h, ffn1_norm[l], ffn1_w_gu[l].astype(BF16), ffn1_w_down[l].astype(BF16))
        h = _token_mixing(h, b, s, mix_norm[l], w_in[l], qk_gain[l], sinks[l], cmp_pos[l], cmp_w1[l], cmp_w2[l],
                          w_branch_a[l], w_branch_b[l], w_branch_c[l], w_out[l])
        h = _ffn(h, ffn2_norm[l], ffn2_w_gu[l].astype(BF16), ffn2_w_down[l].astype(BF16))
    return h.reshape(b, s, d)
```

```python
import functools
import math

import numpy as np
import jax
import jax.numpy as jnp
from jax import lax
from jax.experimental import pallas as pl
from jax.experimental.pallas import tpu as pltpu

F32 = jnp.float32
BF16 = jnp.bfloat16

HEAD_DIM = 64
Q_BLOCK = 128
LANES = 128
A_GROUPS = ((128, 1), (512, 4), (2048, 16))
A_HEADS_PER_GROUP = 4
A_HEADS = 12
A_OUT = A_HEADS_PER_GROUP * HEAD_DIM
B_HEADS = 8
B_KV_HEADS = 2
B_WINDOW = 128
C_HEADS = 12
C_KV_HEADS = 2
C_REP = C_HEADS // C_KV_HEADS
CMP_BLOCK = 32
CMP_STRIDE = 16
CMP_HIDDEN = 256
SLC_BLOCK = 64
SLC_TOPK = 16
C_WINDOW = 512
RMS_EPS = 1e-6
NEG_INF = -1e30
SEC = 768
N_SEC = 6
QKV_COLS = SEC * N_SEC
TM = 512
CHUNK16 = Q_BLOCK * 16
VMEM_LIMIT = 56 * 1024 * 1024


def _slopes(n):
    return [float(2.0 ** (-8.0 * (h + 1) / n)) for h in range(n)]


def _cparams(sem):
    return pltpu.CompilerParams(dimension_semantics=sem, vmem_limit_bytes=VMEM_LIMIT)


def _dot(a, b):
    return jnp.dot(a, b, preferred_element_type=F32)


def _nt_dot(a, b):
    return lax.dot_general(a, b, (((1,), (1,)), ((), ())), preferred_element_type=F32)


def _split(v):
    hi = v.astype(BF16)
    return hi, (v - hi.astype(F32)).astype(BF16)


def _resident(shape):
    return pl.BlockSpec(shape, lambda *_: (0,) * len(shape), pipeline_mode=pl.Buffered(1))


def _rms_rows(x, g):
    ms = jnp.mean(x * x, axis=-1, keepdims=True)
    return (x * lax.rsqrt(ms + RMS_EPS) * g).astype(BF16)


def _deinterleave(n, d):
    p = np.zeros((n, n), np.float32)
    r = np.arange(n // d)
    for c in range(d):
        p[c * (n // d) + r, d * r + c] = 1.0
    return p


def _ffn_body(x_ref, g_ref, wg_ref, wu_ref, wd_ref, o_ref, h_ref, *, n_f):
    f = pl.program_id(1)

    @pl.when(f == 0)
    def _():
        h_ref[...] = _rms_rows(x_ref[...], g_ref[...])
        o_ref[...] = jnp.zeros_like(o_ref)

    h = h_ref[...]
    gate = _dot(h, wg_ref[...])
    up = _dot(h, wu_ref[...])
    act = (gate * jax.nn.sigmoid(gate) * up).astype(BF16)
    o_ref[...] += _dot(act, wd_ref[...])

    @pl.when(f == n_f - 1)
    def _():
        o_ref[...] = x_ref[...] + 0.5 * o_ref[...]


def _ffn(x, g, w_gu, w_down, *, tm=512, tf=512):
    t, d = x.shape
    d_ff = w_down.shape[0]
    n_f = d_ff // tf
    return pl.pallas_call(
        functools.partial(_ffn_body, n_f=n_f),
        grid=(t // tm, n_f),
        in_specs=[
            pl.BlockSpec((tm, d), lambda i, f: (i, 0)),
            pl.BlockSpec((1, d), lambda i, f: (0, 0)),
            pl.BlockSpec((d, tf), lambda i, f: (0, f)),
            pl.BlockSpec((d, tf), lambda i, f: (0, f + n_f)),
            pl.BlockSpec((tf, d), lambda i, f: (f, 0)),
        ],
        out_specs=pl.BlockSpec((tm, d), lambda i, f: (i, 0)),
        out_shape=jax.ShapeDtypeStruct((t, d), F32),
        scratch_shapes=[pltpu.VMEM((tm, d), BF16)],
        compiler_params=_cparams(("parallel", "arbitrary")),
    )(x, g.reshape(1, d), w_gu, w_gu, w_down)


def _head_sumsq(y, bd):
    hi, lo = _split(y * y)
    parts = []
    for c in range(y.shape[1] // LANES):
        sl = slice(c * LANES, (c + 1) * LANES)
        parts.append(_dot(hi[:, sl], bd) + _dot(lo[:, sl], bd))
    return jnp.concatenate(parts, axis=1)


def _qkv_body(x_ref, g_ref, w_ref, flag_ref, gs_ref, bd_ref, p4_ref, p16_ref, eye_ref,
              a0_ref, a1_ref, a2_ref, b_ref, cq_ref, ckv_ref, cmpd_ref, qt_ref, ks_ref, vst_ref, cg_ref):
    h = _rms_rows(x_ref[...], g_ref[...])
    bd = bd_ref[...]

    def section(k):
        sl = slice(k * SEC, (k + 1) * SEC)
        if k < len(A_GROUPS):
            y = jnp.concatenate([_dot(h, w_ref[:, part * A_HEADS * HEAD_DIM + k * A_OUT:
                                                  part * A_HEADS * HEAD_DIM + (k + 1) * A_OUT])
                                 for part in range(3)], axis=1)
        else:
            y = _dot(h, w_ref[:, sl])
        inv = jnp.where(flag_ref[:, sl] > 0, lax.rsqrt(_head_sumsq(y, bd) * (1.0 / HEAD_DIM) + RMS_EPS), 1.0)
        return (y * inv * gs_ref[:, sl]).astype(BF16)

    a0_ref[...] = section(0)
    a1_ref[...] = _dot(p4_ref[...], section(1)).astype(BF16).reshape(a1_ref.shape)
    a2_ref[...] = _dot(p16_ref[...], section(2)).astype(BF16).reshape(a2_ref.shape)
    b_ref[...] = section(3)
    y_cq = section(4)
    cq_ref[...] = y_cq
    qt_ref[...] = _nt_dot(eye_ref[...], y_cq).astype(BF16)
    y_ckv = section(5)
    ckv_ref[...] = y_ckv
    cmpd_ref[...] = _dot(p16_ref[...], y_ckv[:, 0:2 * LANES]).astype(BF16).reshape(cmpd_ref.shape)
    ks_ref[...] = y_ckv[:, 2 * LANES:3 * LANES]
    eye = eye_ref[0:LANES, 0:LANES]
    for kb in range(vst_ref.shape[0]):
        vst_ref[kb] = _nt_dot(eye, y_ckv[kb * Q_BLOCK:(kb + 1) * Q_BLOCK, 3 * LANES:4 * LANES]).astype(BF16)
    cg_ref[...] = jax.nn.sigmoid(_dot(h, w_ref[:, QKV_COLS:QKV_COLS + LANES]))


def _qkv_proj(x, g, w, flag, gs, b, s):
    t, d = x.shape
    tiles_per_batch = s // TM
    per16 = CHUNK16 // TM
    bd = jnp.asarray(np.kron(np.eye(LANES // HEAD_DIM), np.ones((HEAD_DIM, HEAD_DIM))), BF16)
    p4 = jnp.asarray(_deinterleave(TM, 4), BF16)
    p16 = jnp.asarray(_deinterleave(TM, 16), BF16)
    eye = jnp.asarray(np.eye(SEC), BF16)
    nat = pl.BlockSpec((TM, SEC), lambda i: (i, 0))
    out_specs = [
        nat,
        pl.BlockSpec((None, 4, Q_BLOCK, SEC), lambda i: (i, 0, 0, 0)),
        pl.BlockSpec((None, 16, TM // 16, SEC), lambda i: (i // per16, 0, i % per16, 0)),
        nat, nat, nat,
        pl.BlockSpec((None, 16, TM // 16, 2 * LANES), lambda i: (i // tiles_per_batch, 0, i % tiles_per_batch, 0)),
        pl.BlockSpec((None, SEC, TM), lambda i: (i // tiles_per_batch, 0, i % tiles_per_batch)),
        pl.BlockSpec((TM, LANES), lambda i: (i, 0)),
        pl.BlockSpec((TM // Q_BLOCK, LANES, Q_BLOCK), lambda i: (i, 0, 0)),
        pl.BlockSpec((TM, LANES), lambda i: (i, 0)),
    ]
    out_shape = [
        jax.ShapeDtypeStruct((t, SEC), BF16),
        jax.ShapeDtypeStruct((t // TM, 4, Q_BLOCK, SEC), BF16),
        jax.ShapeDtypeStruct((t // CHUNK16, 16, Q_BLOCK, SEC), BF16),
        jax.ShapeDtypeStruct((t, SEC), BF16),
        jax.ShapeDtypeStruct((t, SEC), BF16),
        jax.ShapeDtypeStruct((t, SEC), BF16),
        jax.ShapeDtypeStruct((b, 16, s // 16, 2 * LANES), BF16),
        jax.ShapeDtypeStruct((b, SEC, s), BF16),
        jax.ShapeDtypeStruct((t, LANES), BF16),
        jax.ShapeDtypeStruct((t // Q_BLOCK, LANES, Q_BLOCK), BF16),
        jax.ShapeDtypeStruct((t, LANES), F32),
    ]
    n_w = w.shape[1]
    return pl.pallas_call(
        _qkv_body,
        grid=(t // TM,),
        in_specs=[
            pl.BlockSpec((TM, d), lambda i: (i, 0)),
            _resident((1, d)),
            _resident((d, n_w)),
            _resident((1, QKV_COLS)),
            _resident((1, QKV_COLS)),
            _resident((LANES, LANES)),
            _resident((TM, TM)),
            _resident((TM, TM)),
            _resident((SEC, SEC)),
        ],
        out_specs=out_specs,
        out_shape=out_shape,
        compiler_params=_cparams(("parallel",)),
    )(x, g.reshape(1, d), w, flag.reshape(1, -1), gs.reshape(1, -1), bd, p4, p16, eye)


def _banded_body(*refs, nb, max_dist, heads, k_off, v_off, n_pairs, q_axis, use_sinks, with_lse):
    refs = list(refs)
    q_ref = refs.pop(0)
    kv_refs = [refs.pop(0) for _ in range(nb + 1)]
    sink_ref = refs.pop(0) if use_sinks else None
    o_ref = refs.pop(0)
    lse_ref = refs.pop(0) if with_lse else None

    i = pl.program_id(q_axis)
    nk = (nb + 1) * Q_BLOCK
    row = lax.broadcasted_iota(jnp.int32, (Q_BLOCK, nk), 0)
    col = lax.broadcasted_iota(jnp.int32, (Q_BLOCK, nk), 1)
    rel = nb * Q_BLOCK + row - col
    valid = (rel >= 0) & (rel <= max_dist) & (col >= (nb - i) * Q_BLOCK)
    rel_f = rel.astype(F32)
    neg_mask = jnp.where(valid, 0.0, NEG_INF)
    lane = lax.broadcasted_iota(jnp.int32, (Q_BLOCK, LANES), 1)
    low_half = lane < HEAD_DIM

    kv_cache = {}

    def kv_tile(off, kv_pair, swapped):
        key = (off, kv_pair, swapped)
        if key not in kv_cache:
            c0 = off + kv_pair * LANES
            tile = jnp.concatenate([r[:, c0:c0 + LANES] for r in kv_refs], axis=0)
            if swapped:
                tile = pltpu.roll(tile.astype(F32), HEAD_DIM, 1).astype(BF16)
            kv_cache[key] = tile
        return kv_cache[key]

    outs = [[None, None] for _ in range(n_pairs)]
    lses = [[None, None] for _ in range(n_pairs)]
    for pair, half, kv_pair, kv_half, slope, hidx in heads:
        qp = q_ref[:, pair * LANES:(pair + 1) * LANES]
        own = low_half if half == 0 else jnp.logical_not(low_half)
        qm = jnp.where(own, qp, jnp.zeros_like(qp))
        swapped = kv_half != half
        s = _nt_dot(qm, kv_tile(k_off, kv_pair, swapped))
        s = s - slope * rel_f + neg_mask
        m = jnp.max(s, axis=1, keepdims=True)
        if use_sinks:
            m = jnp.maximum(m, sink_ref[hidx])
        p = jnp.exp(s - m)
        den = jnp.sum(p, axis=1, keepdims=True)
        if use_sinks:
            den = den + jnp.exp(sink_ref[hidx] - m)
        r = _dot(p.astype(BF16), kv_tile(v_off, kv_pair, swapped))
        outs[pair][half] = r / den
        if with_lse:
            lses[pair][half] = m + jnp.log(den)
    for pair in range(n_pairs):
        sl = slice(pair * LANES, (pair + 1) * LANES)
        o_ref[:, sl] = jnp.where(low_half, outs[pair][0], outs[pair][1])
        if with_lse:
            lse_ref[:, sl] = jnp.where(low_half, lses[pair][0], lses[pair][1])


def _banded_call(q_arr, kv_arr, *, grid, q_map, kv_map, out_map, out_lead, out_cols, nb, max_dist, heads,
                 k_off, v_off, q_axis, sinks=None, with_lse=False):
    lead = (None,) * (q_arr.ndim - 2)
    blk = lead + (Q_BLOCK, SEC)
    in_specs = [pl.BlockSpec(blk, q_map)]
    args = [q_arr]
    for back in range(nb, -1, -1):
        in_specs.append(pl.BlockSpec(blk, functools.partial(kv_map, back=back)))
        args.append(kv_arr)
    if sinks is not None:
        in_specs.append(pl.BlockSpec(memory_space=pltpu.SMEM))
        args.append(sinks)
    oblk = pl.BlockSpec(lead + (Q_BLOCK, out_cols), out_map)
    oshape = jax.ShapeDtypeStruct(out_lead + (out_cols,), F32)
    body = functools.partial(_banded_body, nb=nb, max_dist=max_dist, heads=heads, k_off=k_off, v_off=v_off,
                             n_pairs=out_cols // LANES, q_axis=q_axis, use_sinks=sinks is not None,
                             with_lse=with_lse)
    return pl.pallas_call(
        body,
        grid=grid,
        in_specs=in_specs,
        out_specs=[oblk, oblk] if with_lse else oblk,
        out_shape=[oshape, oshape] if with_lse else oshape,
        compiler_params=_cparams(("parallel",) * len(grid)),
    )(*args)


def _dilated_group(arr, gi, b, s):
    window, dil = A_GROUPS[gi]
    slopes = _slopes(A_HEADS)
    heads = tuple((hh // 2, hh % 2, hh // 2, hh % 2, slopes[gi * A_HEADS_PER_GROUP + hh] * dil, hh)
                  for hh in range(A_HEADS_PER_GROUP))
    common = dict(out_cols=A_OUT, nb=1, max_dist=window // dil, heads=heads, k_off=256, v_off=512,
                  with_lse=True)
    if dil == 1:
        return _banded_call(
            arr, arr, grid=(b, s // Q_BLOCK), q_axis=1,
            q_map=lambda bb, i: (bb, i, 0),
            kv_map=lambda bb, i, back: (bb, jnp.maximum(i - back, 0), 0),
            out_map=lambda bb, i: (bb, i, 0), out_lead=(b, s), **common)
    nc = s // (Q_BLOCK * dil)
    return _banded_call(
        arr, arr, grid=(b, dil, nc), q_axis=2,
        q_map=lambda bb, c, i: (bb * nc + i, c, 0, 0),
        kv_map=lambda bb, c, i, back: (bb * nc + jnp.maximum(i - back, 0), c, 0, 0),
        out_map=lambda bb, c, i: (bb * nc + i, c, 0, 0), out_lead=(b * nc, dil, Q_BLOCK), **common)


def _sink_swa(arr, sinks, b, s):
    slopes = _slopes(B_HEADS)
    rep = B_HEADS // B_KV_HEADS
    heads = tuple((h // 2, h % 2, 0, h // rep, slopes[h], h) for h in range(B_HEADS))
    return _banded_call(
        arr, arr, grid=(b, s // Q_BLOCK), q_axis=1,
        q_map=lambda bb, i: (bb, i, 0),
        kv_map=lambda bb, i, back: (bb, jnp.maximum(i - back, 0), 0),
        out_map=lambda bb, i: (bb, i, 0), out_lead=(b, s), out_cols=B_HEADS * HEAD_DIM,
        nb=1, max_dist=B_WINDOW - 1, heads=heads, k_off=512, v_off=640, sinks=sinks)


def _nsa_window(cq, ckv, b, s):
    slopes = _slopes(C_HEADS)
    heads = tuple((h // 2, h % 2, 0, h // C_REP, slopes[h], h) for h in range(C_HEADS))
    return _banded_call(
        cq, ckv, grid=(b, s // Q_BLOCK), q_axis=1,
        q_map=lambda bb, i: (bb, i, 0),
        kv_map=lambda bb, i, back: (bb, jnp.maximum(i - back, 0), 0),
        out_map=lambda bb, i: (bb, i, 0), out_lead=(b, s), out_cols=C_HEADS * HEAD_DIM,
        nb=-(-(C_WINDOW - 1) // Q_BLOCK), max_dist=C_WINDOW - 1, heads=heads, k_off=512, v_off=640)


def _compress_body(t_ref, wb_ref, prow_ref, w2k_ref, w2vt_ref, kg_ref, bd_ref, kc_ref, vct_ref, *, n_chunks):
    hid_cols = 2 * C_KV_HEADS * CMP_HIDDEN
    u = jnp.zeros((n_chunks, hid_cols), F32)
    v = jnp.zeros((n_chunks, hid_cols), F32)
    pc = jnp.zeros((1, hid_cols), F32)
    for c in range(CMP_STRIDE):
        tc = t_ref[c]
        u = u + _dot(tc, wb_ref[0, c])
        v = v + _dot(tc, wb_ref[1, c])
        pc = pc + _dot(prow_ref[0, c], wb_ref[0, c])[0:1] + _dot(prow_ref[1, c], wb_ref[1, c])[0:1]
    hsum = u + pltpu.roll(v, n_chunks - 1, 0) + pc
    hid = (hsum * jax.nn.sigmoid(hsum)).astype(BF16)
    half = C_KV_HEADS * CMP_HIDDEN
    k = _dot(hid[:, :half], w2k_ref[...])
    hi, lo = _split(k * k)
    ss = _dot(hi, bd_ref[...]) + _dot(lo, bd_ref[...])
    kc_ref[...] = (k * lax.rsqrt(ss * (1.0 / HEAD_DIM) + RMS_EPS) * kg_ref[...]).astype(BF16)
    vct_ref[...] = _nt_dot(w2vt_ref[...], hid[:, half:]).astype(BF16)


def _compress(cmpd, wb, prow, w2k, w2vt, kg):
    b, _, n_chunks, width = cmpd.shape
    bd = jnp.asarray(np.kron(np.eye(LANES // HEAD_DIM), np.ones((HEAD_DIM, HEAD_DIM))), BF16)
    return pl.pallas_call(
        functools.partial(_compress_body, n_chunks=n_chunks),
        grid=(b,),
        in_specs=[
            pl.BlockSpec((None, CMP_STRIDE, n_chunks, width), lambda bb: (bb, 0, 0, 0)),
            _resident(wb.shape), _resident(prow.shape), _resident(w2k.shape), _resident(w2vt.shape),
            _resident((1, LANES)), _resident((LANES, LANES)),
        ],
        out_specs=[
            pl.BlockSpec((None, n_chunks, LANES), lambda bb: (bb, 0, 0)),
            pl.BlockSpec((None, LANES, n_chunks), lambda bb: (bb, 0, 0)),
        ],
        out_shape=[
            jax.ShapeDtypeStruct((b, n_chunks, LANES), BF16),
            jax.ShapeDtypeStruct((b, LANES, n_chunks), BF16),
        ],
        compiler_params=_cparams(("parallel",)),
    )(cmpd, wb, prow, w2k, w2vt, kg, bd)


def _to_natural(ot_list, eye):
    pairs = []
    for k in range(0, len(ot_list), 2):
        hi, lo = _split(jnp.concatenate([ot_list[k], ot_list[k + 1]], axis=0))
        pairs.append(_nt_dot(eye, hi) + _nt_dot(eye, lo))
    return jnp.concatenate(pairs, axis=1)


def _cmp_body(qt_ref, kc_ref, vct_ref, ovt_ref, eye_ref, o_ref, sel_ref, cnt_ref, *, n_cmp, n_top):
    g = pl.program_id(1)
    i = pl.program_id(2)
    n_pad = kc_ref.shape[0]
    n_slc = ovt_ref.shape[0]
    kc = kc_ref[...]
    vct = vct_ref[...]
    n_idx = lax.broadcasted_iota(jnp.int32, (n_pad, Q_BLOCK), 0)
    t_idx = i * Q_BLOCK + lax.broadcasted_iota(jnp.int32, (n_pad, Q_BLOCK), 1)
    d_cmp = t_idx - (CMP_STRIDE * n_idx + CMP_BLOCK - 1)
    valid = (d_cmp >= 0) & (n_idx < n_cmp)
    d_f = d_cmp.astype(F32)
    own_rows = (lax.broadcasted_iota(jnp.int32, (LANES, Q_BLOCK), 0) // HEAD_DIM) == g
    slopes = _slopes(C_HEADS)
    psum = jnp.zeros((n_pad, Q_BLOCK), F32)
    outs = []
    for r in range(C_REP):
        slope = jnp.where(g == 0, slopes[r], slopes[C_REP + r])
        qt = qt_ref[r * HEAD_DIM:(r + 1) * HEAD_DIM, :]
        q_pad = jnp.where(own_rows, jnp.concatenate([qt, qt], axis=0), jnp.zeros((LANES, Q_BLOCK), BF16))
        s = _dot(kc, q_pad) - slope * d_f
        s = jnp.where(valid, s, NEG_INF)
        m = jnp.max(s, axis=0, keepdims=True)
        e = jnp.where(valid, jnp.exp(s - m), 0.0)
        den = jnp.sum(e, axis=0, keepdims=True)
        p = e / jnp.where(den > 0, den, 1.0)
        psum = psum + p
        both = _dot(vct, p.astype(BF16))
        outs.append(jnp.where(g == 0, both[:HEAD_DIM], both[HEAD_DIM:]))
    o_ref[...] = _to_natural(outs, eye_ref[...])
    hi, lo = _split(psum)
    ovt = ovt_ref[...]
    imp = _dot(ovt, hi) + _dot(ovt, lo)
    j_idx = lax.broadcasted_iota(jnp.int32, (n_slc, Q_BLOCK), 0)
    t_q = i * Q_BLOCK + lax.broadcasted_iota(jnp.int32, (n_slc, Q_BLOCK), 1)
    cur = lax.shift_right_logical(t_q, int(math.log2(SLC_BLOCK)))
    forced = (j_idx == 0) | (j_idx == cur) | (j_idx == cur - 1)
    v = jnp.where(j_idx <= cur, jnp.where(forced, jnp.inf, imp), -1.0)
    sel = jnp.zeros((n_slc, Q_BLOCK), F32)
    for _ in range(n_top):
        m = jnp.max(v, axis=0, keepdims=True)
        first = jnp.min(jnp.where((v == m) & (m >= 0.0), j_idx, n_slc), axis=0, keepdims=True)
        pick = j_idx == first
        sel = jnp.where(pick, 1.0, sel)
        v = jnp.where(pick, -1.0, v)
    sel_b = sel.astype(BF16)
    sel_ref[...] = sel_b
    cnt_ref[...] = _nt_dot(jnp.ones((8, Q_BLOCK), BF16), sel_b)


def _cmp_select(qt, kc, vct, ovt, *, n_cmp):
    b, _, s = qt.shape
    g = C_KV_HEADS
    n_pad = kc.shape[1]
    n_slc = ovt.shape[0]
    nq = s // Q_BLOCK
    rows = C_REP * HEAD_DIM
    eye = jnp.asarray(np.eye(LANES), BF16)
    return pl.pallas_call(
        functools.partial(_cmp_body, n_cmp=n_cmp, n_top=min(SLC_TOPK, n_slc)),
        grid=(b, g, nq),
        in_specs=[
            pl.BlockSpec((None, rows, Q_BLOCK), lambda bb, gg, i: (bb, gg, i)),
            pl.BlockSpec((None, n_pad, LANES), lambda bb, gg, i: (bb, 0, 0)),
            pl.BlockSpec((None, LANES, n_pad), lambda bb, gg, i: (bb, 0, 0)),
            pl.BlockSpec((n_slc, n_pad), lambda bb, gg, i: (0, 0)),
            pl.BlockSpec((LANES, LANES), lambda bb, gg, i: (0, 0)),
        ],
        out_specs=[
            pl.BlockSpec((Q_BLOCK, rows), lambda bb, gg, i: (bb * nq + i, gg)),
            pl.BlockSpec((None, None, None, n_slc, Q_BLOCK), lambda bb, gg, i: (bb, gg, i, 0, 0)),
            pl.BlockSpec((None, None, None, 8, n_slc), lambda bb, gg, i: (bb, gg, i, 0, 0)),
        ],
        out_shape=[
            jax.ShapeDtypeStruct((b * s, g * rows), F32),
            jax.ShapeDtypeStruct((b, g, nq, n_slc, Q_BLOCK), BF16),
            jax.ShapeDtypeStruct((b, g, nq, 8, n_slc), F32),
        ],
        compiler_params=_cparams(("parallel", "parallel", "parallel")),
    )(qt, kc, vct, ovt, eye)


def _slc_body(bits_ref, qt_ref, ks_ref, vst_ref, sel_ref, eye_ref, o_ref, m_sc, l_sc, acc_sc, *, nq, words):
    bb = pl.program_id(0)
    g = pl.program_id(1)
    i = pl.program_id(2)
    width = C_REP * Q_BLOCK
    n_slc = sel_ref.shape[0]
    own_rows = (lax.broadcasted_iota(jnp.int32, (LANES, width), 0) // HEAD_DIM) == g
    q6 = jnp.concatenate([qt_ref[r * HEAD_DIM:(r + 1) * HEAD_DIM, :] for r in range(C_REP)], axis=1)
    q6 = jnp.where(own_rows, jnp.concatenate([q6, q6], axis=0), jnp.zeros((LANES, width), BF16))
    sel = sel_ref[...]
    slopes = _slopes(C_HEADS)
    head = lax.broadcasted_iota(jnp.int32, (1, width), 1) // Q_BLOCK
    slope_row = jnp.zeros((1, width), F32)
    for r in range(C_REP):
        slope_row = jnp.where(head == r, jnp.where(g == 0, slopes[r], slopes[C_REP + r]), slope_row)
    lane_q = lax.broadcasted_iota(jnp.int32, (Q_BLOCK, width), 1) % Q_BLOCK
    key_r = lax.broadcasted_iota(jnp.int32, (Q_BLOCK, width), 0)
    d0 = lane_q - key_r
    exp_row = lax.broadcasted_iota(jnp.int32, (Q_BLOCK, n_slc), 0) // SLC_BLOCK
    exp_col = lax.broadcasted_iota(jnp.int32, (Q_BLOCK, n_slc), 1)

    m_sc[...] = jnp.full(m_sc.shape, NEG_INF, F32)
    l_sc[...] = jnp.zeros(l_sc.shape, F32)
    acc_sc[...] = jnp.zeros(acc_sc.shape, F32)
    base = ((bb * pl.num_programs(1) + g) * nq + i) * words

    def step(jj, carry):
        word = bits_ref[base + jj // 32]
        active = lax.shift_right_logical(word, jj % 32) & 1

        @pl.when(active == 1)
        def _():
            expand = (exp_col == 2 * jj + exp_row).astype(BF16)
            picked = _dot(expand, sel)
            picked = jnp.concatenate([picked] * C_REP, axis=1)
            d = d0 + (i - jj) * Q_BLOCK
            ok = (picked > 0.5) & (d >= 0)
            k_tile = ks_ref[pl.ds(pl.multiple_of(jj * Q_BLOCK, Q_BLOCK), Q_BLOCK), :]
            s = _dot(k_tile, q6) - slope_row * d.astype(F32)
            s = jnp.where(ok, s, NEG_INF)
            m_old = m_sc[...]
            m_new = jnp.maximum(m_old, jnp.max(s, axis=0, keepdims=True))
            alpha = jnp.exp(m_old - m_new)
            p = jnp.where(ok, jnp.exp(s - m_new), 0.0)
            l_sc[...] = alpha * l_sc[...] + jnp.sum(p, axis=0, keepdims=True)
            acc_sc[...] = alpha * acc_sc[...] + _dot(vst_ref[jj], p.astype(BF16))
            m_sc[...] = m_new

        return carry

    lax.fori_loop(0, i + 1, step, 0)
    l = l_sc[...]
    o = acc_sc[...] / jnp.where(l > 0, l, 1.0)
    o = jnp.where(g == 0, o[:HEAD_DIM], o[HEAD_DIM:])
    o_ref[...] = _to_natural([o[:, r * Q_BLOCK:(r + 1) * Q_BLOCK] for r in range(C_REP)], eye_ref[...])


def _slc_attention(bits, qt, ks, vst, sel, *, words):
    b, _, s = qt.shape
    g = C_KV_HEADS
    nq = s // Q_BLOCK
    n_slc = sel.shape[3]
    rows = C_REP * HEAD_DIM
    width = C_REP * Q_BLOCK
    eye = jnp.asarray(np.eye(LANES), BF16)
    grid_spec = pltpu.PrefetchScalarGridSpec(
        num_scalar_prefetch=1,
        grid=(b, g, nq),
        in_specs=[
            pl.BlockSpec((None, rows, Q_BLOCK), lambda bb, gg, i, bits: (bb, gg, i)),
            pl.BlockSpec((None, s, LANES), lambda bb, gg, i, bits: (bb, 0, 0)),
            pl.BlockSpec((None, nq, LANES, Q_BLOCK), lambda bb, gg, i, bits: (bb, 0, 0, 0)),
            pl.BlockSpec((None, None, None, n_slc, Q_BLOCK), lambda bb, gg, i, bits: (bb, gg, i, 0, 0)),
            pl.BlockSpec((LANES, LANES), lambda bb, gg, i, bits: (0, 0)),
        ],
        out_specs=pl.BlockSpec((Q_BLOCK, rows), lambda bb, gg, i, bits: (bb * nq + i, gg)),
        scratch_shapes=[
            pltpu.VMEM((1, width), F32),
            pltpu.VMEM((1, width), F32),
            pltpu.VMEM((LANES, width), F32),
        ],
    )
    return pl.pallas_call(
        functools.partial(_slc_body, nq=nq, words=words),
        grid_spec=grid_spec,
        out_shape=jax.ShapeDtypeStruct((b * s, g * rows), F32),
        compiler_params=_cparams(("parallel", "parallel", "parallel")),
    )(bits, qt, ks, vst, sel, eye)


def _merge_body(x_ref, g_ref, oa0, la0, oa1, la1, oa2, la2, ob_ref, ocmp_ref, oslc_ref, owin_ref, cg_ref,
                p4t_ref, p16t_ref, ex_ref, wg0_ref, wg1_ref, wg2_ref, wa_ref, wb_ref, wc_ref,
                out_ref, h_ref, oall_ref):
    @pl.when(pl.program_id(1) == 0)
    def _():
        h_ref[...] = _rms_rows(x_ref[...], g_ref[...])

        def natural(ref, pt_ref):
            hi, lo = _split(ref[...].reshape(TM, A_OUT))
            return _dot(pt_ref[...], hi) + _dot(pt_ref[...], lo)

        o0, l0 = oa0[...], la0[...]
        o1, l1 = natural(oa1, p4t_ref), natural(la1, p4t_ref)
        o2, l2 = natural(oa2, p16t_ref), natural(la2, p16t_ref)
        mx = jnp.maximum(jnp.maximum(l0, l1), l2)
        e0, e1, e2 = jnp.exp(l0 - mx), jnp.exp(l1 - mx), jnp.exp(l2 - mx)
        oall_ref[:, 0:A_OUT] = ((e0 * o0 + e1 * o1 + e2 * o2) / (e0 + e1 + e2)).astype(BF16)
        oall_ref[:, A_OUT:A_OUT + B_HEADS * HEAD_DIM] = ob_ref[...].astype(BF16)
        cg_hi, cg_lo = _split(cg_ref[...])
        o_c = None
        for w, o_ref in enumerate((ocmp_ref, oslc_ref, owin_ref)):
            term = (_dot(cg_hi, ex_ref[w]) + _dot(cg_lo, ex_ref[w])) * o_ref[...]
            o_c = term if o_c is None else o_c + term
        oall_ref[:, A_OUT + B_HEADS * HEAD_DIM:] = o_c.astype(BF16)

    h = h_ref[...]
    c0, c1 = A_OUT, A_OUT + B_HEADS * HEAD_DIM
    merged = jax.nn.sigmoid(_dot(h, wg0_ref[...])) * _dot(oall_ref[:, 0:c0], wa_ref[...])
    merged += jax.nn.sigmoid(_dot(h, wg1_ref[...])) * _dot(oall_ref[:, c0:c1], wb_ref[...])
    merged += jax.nn.sigmoid(_dot(h, wg2_ref[...])) * _dot(oall_ref[:, c1:], wc_ref[...])
    out_ref[...] = merged.astype(BF16)


def _merge(x, g, a_outs, ob, ocmp, oslc, owin, cg, w_gate, wa, wb, wc, ex, *, tn=512):
    t, d = x.shape
    per16 = CHUNK16 // TM
    n_t = d // tn
    (oa0, la0), (oa1, la1), (oa2, la2) = a_outs

    def rows(a):
        return pl.BlockSpec((TM, a.shape[1]), lambda i, n: (i, 0))

    a1_spec = pl.BlockSpec((None, 4, Q_BLOCK, A_OUT), lambda i, n: (i, 0, 0, 0))
    a2_spec = pl.BlockSpec((None, 16, TM // 16, A_OUT), lambda i, n: (i // per16, 0, i % per16, 0))
    p4t = jnp.asarray(_deinterleave(TM, 4).T, BF16)
    p16t = jnp.asarray(_deinterleave(TM, 16).T, BF16)
    in_specs = [
        rows(x), _resident((1, d)),
        rows(oa0), rows(la0), a1_spec, a1_spec, a2_spec, a2_spec,
        rows(ob), rows(ocmp), rows(oslc), rows(owin), rows(cg),
        _resident((TM, TM)), _resident((TM, TM)), _resident(ex.shape),
        pl.BlockSpec((d, tn), lambda i, n: (0, n)),
        pl.BlockSpec((d, tn), lambda i, n: (0, n + n_t)),
        pl.BlockSpec((d, tn), lambda i, n: (0, n + 2 * n_t)),
        pl.BlockSpec((wa.shape[0], tn), lambda i, n: (0, n)),
        pl.BlockSpec((wb.shape[0], tn), lambda i, n: (0, n)),
        pl.BlockSpec((wc.shape[0], tn), lambda i, n: (0, n)),
    ]
    return pl.pallas_call(
        _merge_body,
        grid=(t // TM, n_t),
        in_specs=in_specs,
        out_specs=pl.BlockSpec((TM, tn), lambda i, n: (i, n)),
        out_shape=jax.ShapeDtypeStruct((t, d), BF16),
        scratch_shapes=[pltpu.VMEM((TM, d), BF16), pltpu.VMEM((TM, wa.shape[0] + wb.shape[0] + wc.shape[0]), BF16)],
        compiler_params=_cparams(("parallel", "arbitrary")),
    )(x, g.reshape(1, d), oa0, la0, oa1, la1, oa2, la2, ob, ocmp, oslc, owin, cg, p4t, p16t, ex,
      w_gate, w_gate, w_gate, wa, wb, wc)


def _out_body(x_ref, m_ref, w_ref, o_ref):
    o_ref[...] = x_ref[...] + _dot(m_ref[...], w_ref[...])


def _out_proj(x, merged, w_out):
    t, d = x.shape
    return pl.pallas_call(
        _out_body,
        grid=(t // TM,),
        in_specs=[pl.BlockSpec((TM, d), lambda i: (i, 0)), pl.BlockSpec((TM, d), lambda i: (i, 0)),
                  _resident((d, d))],
        out_specs=pl.BlockSpec((TM, d), lambda i: (i, 0)),
        out_shape=jax.ShapeDtypeStruct((t, d), F32),
        compiler_params=_cparams(("parallel",)),
    )(x, merged, w_out)


def _qkv_column_params(qk_gain):
    flag, gain, scale = [], [], []
    one = jnp.ones((HEAD_DIM,), F32)

    def add(n_heads, normed, is_q, gvec):
        for _ in range(n_heads):
            flag.append(np.full((HEAD_DIM,), 1.0 if normed else 0.0, np.float32))
            gain.append(gvec if normed else one)
            scale.append(np.full((HEAD_DIM,), HEAD_DIM ** -0.5 if is_q else 1.0, np.float32))

    for _ in range(len(A_GROUPS)):
        add(A_HEADS_PER_GROUP, True, True, qk_gain[0, 0])
        add(A_HEADS_PER_GROUP, True, False, qk_gain[0, 1])
        add(A_HEADS_PER_GROUP, False, False, one)
    add(B_HEADS, True, True, qk_gain[1, 0])
    add(B_KV_HEADS, True, False, qk_gain[1, 1])
    add(B_KV_HEADS, False, False, one)
    add(C_HEADS, True, True, qk_gain[2, 0])
    for normed in (False, False, True, False, True, False):
        add(C_KV_HEADS, normed, False, qk_gain[2, 1])
    flag = np.concatenate(flag)
    assert flag.shape[0] == QKV_COLS
    return jnp.asarray(flag), jnp.concatenate(gain) * jnp.asarray(np.concatenate(scale))


def _overlap_t(n_slc, n_pad, n_cmp):
    n = np.arange(n_pad)[None, :]
    j = np.arange(n_slc)[:, None]
    start, end = CMP_STRIDE * n, CMP_STRIDE * n + CMP_BLOCK - 1
    ov = (start <= SLC_BLOCK * j + SLC_BLOCK - 1) & (end >= SLC_BLOCK * j) & (n < n_cmp)
    return jnp.asarray(ov, BF16)


def _gate_expand():
    ex = np.zeros((3, LANES, C_HEADS * HEAD_DIM), np.float32)
    for w in range(3):
        for h in range(C_HEADS):
            ex[w, h * 3 + w, h * HEAD_DIM:(h + 1) * HEAD_DIM] = 1.0
    return jnp.asarray(ex, BF16)


def _compress_weights(cmp_pos, cmp_w1, cmp_w2):
    n_q = 2 * C_KV_HEADS
    w1 = cmp_w1.reshape(2, 2, CMP_STRIDE, HEAD_DIM, CMP_HIDDEN)
    w1q = jnp.repeat(w1, C_KV_HEADS, axis=0)
    wb = jnp.einsum("qhcdn,qp->hcqdpn", w1q, jnp.eye(n_q, dtype=F32))
    wb = wb.reshape(2, CMP_STRIDE, n_q * HEAD_DIM, n_q * CMP_HIDDEN).astype(BF16)
    pos = cmp_pos.reshape(2, 2, CMP_STRIDE, HEAD_DIM)
    prow = jnp.repeat(pos, C_KV_HEADS, axis=0).transpose(1, 2, 0, 3).reshape(2, CMP_STRIDE, 1, n_q * HEAD_DIM)
    prow = jnp.broadcast_to(prow, (2, CMP_STRIDE, 8, n_q * HEAD_DIM)).astype(BF16)
    eye_g = jnp.eye(C_KV_HEADS, dtype=F32)
    w2k = jnp.kron(eye_g, cmp_w2[0]).astype(BF16)
    w2vt = jnp.kron(eye_g, cmp_w2[1]).T.astype(BF16)
    return wb, prow, w2k, w2vt


def _token_mixing(x, b, s, mix_norm, w_in, qk_gain, sinks, cmp_pos, cmp_w1, cmp_w2, w_a, w_b, w_c, w_out):
    t, d = x.shape
    assert s % CHUNK16 == 0 and d % 512 == 0
    c_gate_cols = 3 * C_HEADS
    w_qkv = w_in[:, :QKV_COLS + LANES].astype(BF16)
    flag, gs = _qkv_column_params(qk_gain)
    a0, a1, a2, bsec, cq, ckv, cmpd, qt, ks, vst, cg = _qkv_proj(x, mix_norm, w_qkv, flag, gs, b, s)

    a_outs = [_dilated_group(a0.reshape(b, s, SEC), 0, b, s), _dilated_group(a1, 1, b, s),
              _dilated_group(a2, 2, b, s)]
    a_outs[0] = tuple(v.reshape(t, A_OUT) for v in a_outs[0])
    o_b = _sink_swa(bsec.reshape(b, s, SEC), sinks.astype(F32), b, s).reshape(t, -1)
    o_win = _nsa_window(cq.reshape(b, s, SEC), ckv.reshape(b, s, SEC), b, s).reshape(t, -1)

    n_chunks = s // CMP_STRIDE
    n_cmp = (s - CMP_BLOCK) // CMP_STRIDE + 1
    n_slc = s // SLC_BLOCK
    nq = s // Q_BLOCK
    kg = jnp.tile(qk_gain[2, 1], C_KV_HEADS).reshape(1, LANES)
    kc, vct = _compress(cmpd, *_compress_weights(cmp_pos, cmp_w1, cmp_w2), kg)
    o_cmp, sel, cnt = _cmp_select(qt, kc, vct, _overlap_t(n_slc, n_chunks, n_cmp), n_cmp=n_cmp)

    act = (cnt[:, :, :, 0, :] > 0).reshape(b, C_KV_HEADS, nq, nq, 2).any(axis=-1)
    words = -(-nq // 32)
    act = jnp.pad(act, ((0, 0), (0, 0), (0, 0), (0, words * 32 - nq))).reshape(b, C_KV_HEADS, nq, words, 32)
    bits = jnp.sum(act.astype(jnp.uint32) << jnp.arange(32, dtype=jnp.uint32), axis=-1, dtype=jnp.uint32)
    bits = lax.bitcast_convert_type(bits, jnp.int32).reshape(-1)
    o_slc = _slc_attention(bits, qt, ks.reshape(b, s, LANES), vst.reshape(b, nq, LANES, Q_BLOCK), sel, words=words)

    merged = _merge(x, mix_norm, a_outs, o_b, o_cmp, o_slc, o_win, cg,
                    w_in[:, QKV_COLS + c_gate_cols:].astype(BF16), w_a.astype(BF16), w_b.astype(BF16),
                    w_c.astype(BF16), _gate_expand())
    return _out_proj(x, merged, w_out.astype(BF16))


def kernel(x, ffn1_norm, ffn1_w_gu, ffn1_w_down, mix_norm, w_in, qk_gain, sinks, cmp_pos, cmp_w1, cmp_w2,
           w_branch_a, w_branch_b, w_branch_c, w_out, ffn2_norm, ffn2_w_gu, ffn2_w_down):
    b, s, d = x.shape
    h = x.reshape(b * s, d)
    for l in range(ffn1_norm.shape[0]):
        h = _ffn(h, ffn1_norm[l], ffn1_w_gu[l].astype(BF16), ffn1_w_down[l].astype(BF16))
        h = _token_mixing(h, b, s, mix_norm[l], w_in[l], qk_gain[l], sinks[l], cmp_pos[l], cmp_w1[l], cmp_w2[l],
                          w_branch_a[l], w_branch_b[l], w_branch_c[l], w_out[l])
        h = _ffn(h, ffn2_norm[l], ffn2_w_gu[l].astype(BF16), ffn2_w_down[l].astype(BF16))
    return h.reshape(b, s, d)
```

```python
import functools
import math

import numpy as np
import jax
import jax.numpy as jnp
from jax import lax
from jax.experimental import pallas as pl
from jax.experimental.pallas import tpu as pltpu

F32 = jnp.float32
BF16 = jnp.bfloat16

HEAD_DIM = 64
Q_BLOCK = 128
LANES = 128
A_GROUPS = ((128, 1), (512, 4), (2048, 16))
A_HEADS_PER_GROUP = 4
A_HEADS = 12
A_OUT = A_HEADS_PER_GROUP * HEAD_DIM
B_HEADS = 8
B_KV_HEADS = 2
B_WINDOW = 128
C_HEADS = 12
C_KV_HEADS = 2
C_REP = C_HEADS // C_KV_HEADS
CMP_BLOCK = 32
CMP_STRIDE = 16
CMP_HIDDEN = 256
SLC_BLOCK = 64
SLC_TOPK = 16
C_WINDOW = 512
RMS_EPS = 1e-6
NEG_INF = -1e30
SEC = 768
N_SEC = 6
QKV_COLS = SEC * N_SEC
TM = 512
CHUNK16 = Q_BLOCK * 16
VMEM_LIMIT = 56 * 1024 * 1024


def _slopes(n):
    return [float(2.0 ** (-8.0 * (h + 1) / n)) for h in range(n)]


def _cparams(sem):
    return pltpu.CompilerParams(dimension_semantics=sem, vmem_limit_bytes=VMEM_LIMIT)


def _dot(a, b):
    return jnp.dot(a, b, preferred_element_type=F32)


def _nt_dot(a, b):
    return lax.dot_general(a, b, (((1,), (1,)), ((), ())), preferred_element_type=F32)


def _split(v):
    hi = v.astype(BF16)
    return hi, (v - hi.astype(F32)).astype(BF16)


def _resident(shape):
    return pl.BlockSpec(shape, lambda *_: (0,) * len(shape), pipeline_mode=pl.Buffered(1))


def _rms_rows(x, g):
    ms = jnp.mean(x * x, axis=-1, keepdims=True)
    return (x * lax.rsqrt(ms + RMS_EPS) * g).astype(BF16)


def _deinterleave(n, d):
    p = np.zeros((n, n), np.float32)
    r = np.arange(n // d)
    for c in range(d):
        p[c * (n // d) + r, d * r + c] = 1.0
    return p


AUG_POS = 0
AUG_ROWS = 8
MASK_BIG = 1e30
SLC_GROUP = 3


def _key_pattern(n):
    pat = np.zeros((n, LANES), np.float32)
    pat[:, AUG_POS] = pat[:, AUG_POS + 1] = np.arange(n) % Q_BLOCK
    return pat


def _block_spread(n_tiles):
    m = np.zeros((n_tiles * AUG_ROWS, 2 * n_tiles), np.float32)
    jj = np.arange(n_tiles)
    for e in range(2):
        m[AUG_ROWS * jj + e, 2 * jj + e] = 1.0
    return m


def _ffn_body(x_ref, g_ref, wg_ref, wu_ref, wd_ref, o_ref, h_ref, *, n_f):
    f = pl.program_id(1)

    @pl.when(f == 0)
    def _():
        h_ref[...] = _rms_rows(x_ref[...], g_ref[...])
        o_ref[...] = jnp.zeros_like(o_ref)

    h = h_ref[...]
    gate = _dot(h, wg_ref[...])
    up = _dot(h, wu_ref[...])
    act = (gate * jax.nn.sigmoid(gate) * up).astype(BF16)
    o_ref[...] += _dot(act, wd_ref[...])

    @pl.when(f == n_f - 1)
    def _():
        o_ref[...] = x_ref[...] + 0.5 * o_ref[...]


def _ffn(x, g, w_gu, w_down, layer, *, tm=512, tf=512):
    t, d = x.shape
    d_ff = w_down.shape[1]
    n_f = d_ff // tf
    return pl.pallas_call(
        functools.partial(_ffn_body, n_f=n_f),
        grid=(t // tm, n_f),
        in_specs=[
            pl.BlockSpec((tm, d), lambda i, f: (i, 0)),
            pl.BlockSpec((1, d), lambda i, f: (0, 0)),
            pl.BlockSpec((None, d, tf), lambda i, f: (layer, 0, f)),
            pl.BlockSpec((None, d, tf), lambda i, f: (layer, 0, f + n_f)),
            pl.BlockSpec((None, tf, d), lambda i, f: (layer, f, 0)),
        ],
        out_specs=pl.BlockSpec((tm, d), lambda i, f: (i, 0)),
        out_shape=jax.ShapeDtypeStruct((t, d), F32),
        scratch_shapes=[pltpu.VMEM((tm, d), BF16)],
        compiler_params=_cparams(("parallel", "arbitrary")),
    )(x, g.reshape(1, d), w_gu, w_gu, w_down)


def _head_sumsq(y, bd):
    hi, lo = _split(y * y)
    parts = []
    for c in range(y.shape[1] // LANES):
        sl = slice(c * LANES, (c + 1) * LANES)
        parts.append(_dot(hi[:, sl], bd) + _dot(lo[:, sl], bd))
    return jnp.concatenate(parts, axis=1)


def _qkv_body(x_ref, g_ref, w_ref, flag_ref, gs_ref, bd_ref, p4_ref, p16_ref, eye_ref, kpat_ref,
              a0_ref, a1_ref, a2_ref, b_ref, cq_ref, ckv_ref, cmpd_ref, qt_ref, ks_ref, vst_ref, cg_ref):
    h = _rms_rows(x_ref[...], g_ref[...])
    bd = bd_ref[...]

    def section(k):
        sl = slice(k * SEC, (k + 1) * SEC)
        if k < len(A_GROUPS):
            y = jnp.concatenate([_dot(h, w_ref[:, part * A_HEADS * HEAD_DIM + k * A_OUT:
                                                  part * A_HEADS * HEAD_DIM + (k + 1) * A_OUT])
                                 for part in range(3)], axis=1)
        else:
            y = _dot(h, w_ref[:, sl])
        inv = jnp.where(flag_ref[:, sl] > 0, lax.rsqrt(_head_sumsq(y, bd) * (1.0 / HEAD_DIM) + RMS_EPS), 1.0)
        return (y * inv * gs_ref[:, sl]).astype(BF16)

    a0_ref[...] = section(0)
    a1_ref[...] = _dot(p4_ref[...], section(1)).astype(BF16).reshape(a1_ref.shape)
    a2_ref[...] = _dot(p16_ref[...], section(2)).astype(BF16).reshape(a2_ref.shape)
    b_ref[...] = section(3)
    y_cq = section(4)
    cq_ref[...] = y_cq
    qt_ref[...] = _nt_dot(eye_ref[...], y_cq).astype(BF16)
    y_ckv = section(5)
    ckv_ref[...] = y_ckv
    cmpd_ref[...] = _dot(p16_ref[...], y_ckv[:, 0:2 * LANES]).astype(BF16).reshape(cmpd_ref.shape)
    ks_ref[:, 0:LANES] = y_ckv[:, 2 * LANES:3 * LANES]
    ks_ref[:, LANES:2 * LANES] = kpat_ref[...]
    eye = eye_ref[0:LANES, 0:LANES]
    for kb in range(vst_ref.shape[0]):
        vt = _nt_dot(eye, y_ckv[kb * Q_BLOCK:(kb + 1) * Q_BLOCK, 3 * LANES:4 * LANES]).astype(BF16)
        for gg in range(C_KV_HEADS):
            vst_ref[kb, gg] = vt[gg * HEAD_DIM:(gg + 1) * HEAD_DIM]
    cg_ref[...] = jax.nn.sigmoid(_dot(h, w_ref[:, QKV_COLS:QKV_COLS + LANES]))


def _qkv_proj(x, g, w, flag, gs, b, s):
    t, d = x.shape
    tiles_per_batch = s // TM
    per16 = CHUNK16 // TM
    bd = jnp.asarray(np.kron(np.eye(LANES // HEAD_DIM), np.ones((HEAD_DIM, HEAD_DIM))), BF16)
    p4 = jnp.asarray(_deinterleave(TM, 4), BF16)
    p16 = jnp.asarray(_deinterleave(TM, 16), BF16)
    eye = jnp.asarray(np.eye(SEC), BF16)
    nat = pl.BlockSpec((TM, SEC), lambda i: (i, 0))
    out_specs = [
        nat,
        pl.BlockSpec((None, 4, Q_BLOCK, SEC), lambda i: (i, 0, 0, 0)),
        pl.BlockSpec((None, 16, TM // 16, SEC), lambda i: (i // per16, 0, i % per16, 0)),
        nat, nat, nat,
        pl.BlockSpec((None, 16, TM // 16, 2 * LANES), lambda i: (i // tiles_per_batch, 0, i % tiles_per_batch, 0)),
        pl.BlockSpec((None, SEC, TM), lambda i: (i // tiles_per_batch, 0, i % tiles_per_batch)),
        pl.BlockSpec((TM, 2 * LANES), lambda i: (i, 0)),
        pl.BlockSpec((TM // Q_BLOCK, C_KV_HEADS, HEAD_DIM, Q_BLOCK), lambda i: (i, 0, 0, 0)),
        pl.BlockSpec((TM, LANES), lambda i: (i, 0)),
    ]
    out_shape = [
        jax.ShapeDtypeStruct((t, SEC), BF16),
        jax.ShapeDtypeStruct((t // TM, 4, Q_BLOCK, SEC), BF16),
        jax.ShapeDtypeStruct((t // CHUNK16, 16, Q_BLOCK, SEC), BF16),
        jax.ShapeDtypeStruct((t, SEC), BF16),
        jax.ShapeDtypeStruct((t, SEC), BF16),
        jax.ShapeDtypeStruct((t, SEC), BF16),
        jax.ShapeDtypeStruct((b, 16, s // 16, 2 * LANES), BF16),
        jax.ShapeDtypeStruct((b, SEC, s), BF16),
        jax.ShapeDtypeStruct((t, 2 * LANES), BF16),
        jax.ShapeDtypeStruct((t // Q_BLOCK, C_KV_HEADS, HEAD_DIM, Q_BLOCK), BF16),
        jax.ShapeDtypeStruct((t, LANES), F32),
    ]
    n_w = w.shape[1]
    return pl.pallas_call(
        _qkv_body,
        grid=(t // TM,),
        in_specs=[
            pl.BlockSpec((TM, d), lambda i: (i, 0)),
            _resident((1, d)),
            _resident((d, n_w)),
            _resident((1, QKV_COLS)),
            _resident((1, QKV_COLS)),
            _resident((LANES, LANES)),
            _resident((TM, TM)),
            _resident((TM, TM)),
            _resident((SEC, SEC)),
            _resident((TM, LANES)),
        ],
        out_specs=out_specs,
        out_shape=out_shape,
        compiler_params=_cparams(("parallel",)),
    )(x, g.reshape(1, d), w, flag.reshape(1, -1), gs.reshape(1, -1), bd, p4, p16, eye,
      jnp.asarray(_key_pattern(TM), BF16))


def _banded_body(*refs, nb, max_dist, heads, k_off, v_off, n_pairs, q_axis, use_sinks, with_lse):
    refs = list(refs)
    q_ref = refs.pop(0)
    kv_refs = [refs.pop(0) for _ in range(nb + 1)]
    sink_ref = refs.pop(0) if use_sinks else None
    o_ref = refs.pop(0)
    lse_ref = refs.pop(0) if with_lse else None

    i = pl.program_id(q_axis)
    nk = (nb + 1) * Q_BLOCK
    row = lax.broadcasted_iota(jnp.int32, (Q_BLOCK, nk), 0)
    col = lax.broadcasted_iota(jnp.int32, (Q_BLOCK, nk), 1)
    rel = nb * Q_BLOCK + row - col
    valid = (rel >= 0) & (rel <= max_dist) & (col >= (nb - i) * Q_BLOCK)
    rel_f = rel.astype(F32)
    neg_mask = jnp.where(valid, 0.0, NEG_INF)
    lane = lax.broadcasted_iota(jnp.int32, (Q_BLOCK, LANES), 1)
    low_half = lane < HEAD_DIM

    kv_cache = {}

    def kv_tile(off, kv_pair, swapped):
        key = (off, kv_pair, swapped)
        if key not in kv_cache:
            c0 = off + kv_pair * LANES
            tile = jnp.concatenate([r[:, c0:c0 + LANES] for r in kv_refs], axis=0)
            if swapped:
                tile = pltpu.roll(tile.astype(F32), HEAD_DIM, 1).astype(BF16)
            kv_cache[key] = tile
        return kv_cache[key]

    outs = [[None, None] for _ in range(n_pairs)]
    lses = [[None, None] for _ in range(n_pairs)]
    for pair, half, kv_pair, kv_half, slope, hidx in heads:
        qp = q_ref[:, pair * LANES:(pair + 1) * LANES]
        own = low_half if half == 0 else jnp.logical_not(low_half)
        qm = jnp.where(own, qp, jnp.zeros_like(qp))
        swapped = kv_half != half
        s = _nt_dot(qm, kv_tile(k_off, kv_pair, swapped))
        s = s - slope * rel_f + neg_mask
        m = jnp.max(s, axis=1, keepdims=True)
        if use_sinks:
            m = jnp.maximum(m, sink_ref[hidx])
        p = jnp.exp(s - m)
        den = jnp.sum(p, axis=1, keepdims=True)
        if use_sinks:
            den = den + jnp.exp(sink_ref[hidx] - m)
        r = _dot(p.astype(BF16), kv_tile(v_off, kv_pair, swapped))
        outs[pair][half] = r / den
        if with_lse:
            lses[pair][half] = m + jnp.log(den)
    for pair in range(n_pairs):
        sl = slice(pair * LANES, (pair + 1) * LANES)
        o_ref[:, sl] = jnp.where(low_half, outs[pair][0], outs[pair][1])
        if with_lse:
            lse_ref[:, sl] = jnp.where(low_half, lses[pair][0], lses[pair][1])


def _banded_call(q_arr, kv_arr, *, grid, q_map, kv_map, out_map, out_lead, out_cols, nb, max_dist, heads,
                 k_off, v_off, q_axis, sinks=None, with_lse=False):
    lead = (None,) * (q_arr.ndim - 2)
    blk = lead + (Q_BLOCK, SEC)
    in_specs = [pl.BlockSpec(blk, q_map)]
    args = [q_arr]
    for back in range(nb, -1, -1):
        in_specs.append(pl.BlockSpec(blk, functools.partial(kv_map, back=back)))
        args.append(kv_arr)
    if sinks is not None:
        in_specs.append(pl.BlockSpec(memory_space=pltpu.SMEM))
        args.append(sinks)
    oblk = pl.BlockSpec(lead + (Q_BLOCK, out_cols), out_map)
    oshape = jax.ShapeDtypeStruct(out_lead + (out_cols,), F32)
    body = functools.partial(_banded_body, nb=nb, max_dist=max_dist, heads=heads, k_off=k_off, v_off=v_off,
                             n_pairs=out_cols // LANES, q_axis=q_axis, use_sinks=sinks is not None,
                             with_lse=with_lse)
    return pl.pallas_call(
        body,
        grid=grid,
        in_specs=in_specs,
        out_specs=[oblk, oblk] if with_lse else oblk,
        out_shape=[oshape, oshape] if with_lse else oshape,
        compiler_params=_cparams(("parallel",) * len(grid)),
    )(*args)


def _dilated_group(arr, gi, b, s):
    window, dil = A_GROUPS[gi]
    slopes = _slopes(A_HEADS)
    heads = tuple((hh // 2, hh % 2, hh // 2, hh % 2, slopes[gi * A_HEADS_PER_GROUP + hh] * dil, hh)
                  for hh in range(A_HEADS_PER_GROUP))
    common = dict(out_cols=A_OUT, nb=1, max_dist=window // dil, heads=heads, k_off=256, v_off=512,
                  with_lse=True)
    if dil == 1:
        return _banded_call(
            arr, arr, grid=(b, s // Q_BLOCK), q_axis=1,
            q_map=lambda bb, i: (bb, i, 0),
            kv_map=lambda bb, i, back: (bb, jnp.maximum(i - back, 0), 0),
            out_map=lambda bb, i: (bb, i, 0), out_lead=(b, s), **common)
    nc = s // (Q_BLOCK * dil)
    return _banded_call(
        arr, arr, grid=(b, dil, nc), q_axis=2,
        q_map=lambda bb, c, i: (bb * nc + i, c, 0, 0),
        kv_map=lambda bb, c, i, back: (bb * nc + jnp.maximum(i - back, 0), c, 0, 0),
        out_map=lambda bb, c, i: (bb * nc + i, c, 0, 0), out_lead=(b * nc, dil, Q_BLOCK), **common)


def _sink_swa(arr, sinks, b, s):
    slopes = _slopes(B_HEADS)
    rep = B_HEADS // B_KV_HEADS
    heads = tuple((h // 2, h % 2, 0, h // rep, slopes[h], h) for h in range(B_HEADS))
    return _banded_call(
        arr, arr, grid=(b, s // Q_BLOCK), q_axis=1,
        q_map=lambda bb, i: (bb, i, 0),
        kv_map=lambda bb, i, back: (bb, jnp.maximum(i - back, 0), 0),
        out_map=lambda bb, i: (bb, i, 0), out_lead=(b, s), out_cols=B_HEADS * HEAD_DIM,
        nb=1, max_dist=B_WINDOW - 1, heads=heads, k_off=512, v_off=640, sinks=sinks)


def _nsa_window(cq, ckv, b, s):
    slopes = _slopes(C_HEADS)
    heads = tuple((h // 2, h % 2, 0, h // C_REP, slopes[h], h) for h in range(C_HEADS))
    return _banded_call(
        cq, ckv, grid=(b, s // Q_BLOCK), q_axis=1,
        q_map=lambda bb, i: (bb, i, 0),
        kv_map=lambda bb, i, back: (bb, jnp.maximum(i - back, 0), 0),
        out_map=lambda bb, i: (bb, i, 0), out_lead=(b, s), out_cols=C_HEADS * HEAD_DIM,
        nb=-(-(C_WINDOW - 1) // Q_BLOCK), max_dist=C_WINDOW - 1, heads=heads, k_off=512, v_off=640)


def _compress_body(t_ref, wb_ref, prow_ref, w2k_ref, w2vt_ref, kg_ref, bd_ref, kc_ref, vct_ref, *, n_chunks):
    hid_cols = 2 * C_KV_HEADS * CMP_HIDDEN
    u = jnp.zeros((n_chunks, hid_cols), F32)
    v = jnp.zeros((n_chunks, hid_cols), F32)
    pc = jnp.zeros((1, hid_cols), F32)
    for c in range(CMP_STRIDE):
        tc = t_ref[c]
        u = u + _dot(tc, wb_ref[0, c])
        v = v + _dot(tc, wb_ref[1, c])
        pc = pc + _dot(prow_ref[0, c], wb_ref[0, c])[0:1] + _dot(prow_ref[1, c], wb_ref[1, c])[0:1]
    hsum = u + pltpu.roll(v, n_chunks - 1, 0) + pc
    hid = (hsum * jax.nn.sigmoid(hsum)).astype(BF16)
    half = C_KV_HEADS * CMP_HIDDEN
    k = _dot(hid[:, :half], w2k_ref[...])
    hi, lo = _split(k * k)
    ss = _dot(hi, bd_ref[...]) + _dot(lo, bd_ref[...])
    kc_ref[...] = (k * lax.rsqrt(ss * (1.0 / HEAD_DIM) + RMS_EPS) * kg_ref[...]).astype(BF16)
    vct_ref[...] = _nt_dot(w2vt_ref[...], hid[:, half:]).astype(BF16)


def _compress(cmpd, wb, prow, w2k, w2vt, kg):
    b, _, n_chunks, width = cmpd.shape
    bd = jnp.asarray(np.kron(np.eye(LANES // HEAD_DIM), np.ones((HEAD_DIM, HEAD_DIM))), BF16)
    return pl.pallas_call(
        functools.partial(_compress_body, n_chunks=n_chunks),
        grid=(b,),
        in_specs=[
            pl.BlockSpec((None, CMP_STRIDE, n_chunks, width), lambda bb: (bb, 0, 0, 0)),
            _resident(wb.shape), _resident(prow.shape), _resident(w2k.shape), _resident(w2vt.shape),
            _resident((1, LANES)), _resident((LANES, LANES)),
        ],
        out_specs=[
            pl.BlockSpec((None, n_chunks, LANES), lambda bb: (bb, 0, 0)),
            pl.BlockSpec((None, LANES, n_chunks), lambda bb: (bb, 0, 0)),
        ],
        out_shape=[
            jax.ShapeDtypeStruct((b, n_chunks, LANES), BF16),
            jax.ShapeDtypeStruct((b, LANES, n_chunks), BF16),
        ],
        compiler_params=_cparams(("parallel",)),
    )(cmpd, wb, prow, w2k, w2vt, kg, bd)


def _to_natural(ot_list, eye):
    pairs = []
    for k in range(0, len(ot_list), 2):
        hi, lo = _split(jnp.concatenate([ot_list[k], ot_list[k + 1]], axis=0))
        pairs.append(_nt_dot(eye, hi) + _nt_dot(eye, lo))
    return jnp.concatenate(pairs, axis=1)


def _cmp_body(qt_ref, kc_ref, vct_ref, ovt_ref, eye_ref, spread_ref, o_ref, selm_ref, cnt_ref, *, n_cmp, n_top):
    g = pl.program_id(1)
    i = pl.program_id(2)
    n_pad = kc_ref.shape[0]
    n_slc = ovt_ref.shape[0]
    kc = kc_ref[...]
    vct = vct_ref[...]
    n_idx = lax.broadcasted_iota(jnp.int32, (n_pad, Q_BLOCK), 0)
    t_idx = i * Q_BLOCK + lax.broadcasted_iota(jnp.int32, (n_pad, Q_BLOCK), 1)
    d_cmp = t_idx - (CMP_STRIDE * n_idx + CMP_BLOCK - 1)
    valid = (d_cmp >= 0) & (n_idx < n_cmp)
    d_f = d_cmp.astype(F32)
    own_rows = (lax.broadcasted_iota(jnp.int32, (LANES, Q_BLOCK), 0) // HEAD_DIM) == g
    slopes = _slopes(C_HEADS)
    psum = jnp.zeros((n_pad, Q_BLOCK), F32)
    outs = []
    for r in range(C_REP):
        slope = jnp.where(g == 0, slopes[r], slopes[C_REP + r])
        qt = qt_ref[r * HEAD_DIM:(r + 1) * HEAD_DIM, :]
        q_pad = jnp.where(own_rows, jnp.concatenate([qt, qt], axis=0), jnp.zeros((LANES, Q_BLOCK), BF16))
        s = _dot(kc, q_pad) - slope * d_f
        s = jnp.where(valid, s, NEG_INF)
        m = jnp.max(s, axis=0, keepdims=True)
        e = jnp.where(valid, jnp.exp(s - m), 0.0)
        den = jnp.sum(e, axis=0, keepdims=True)
        p = e / jnp.where(den > 0, den, 1.0)
        psum = psum + p
        both = _dot(vct, p.astype(BF16))
        outs.append(jnp.where(g == 0, both[:HEAD_DIM], both[HEAD_DIM:]))
    o_ref[...] = _to_natural(outs, eye_ref[...])
    hi, lo = _split(psum)
    ovt = ovt_ref[...]
    imp = _dot(ovt, hi) + _dot(ovt, lo)
    j_idx = lax.broadcasted_iota(jnp.int32, (n_slc, Q_BLOCK), 0)
    t_q = i * Q_BLOCK + lax.broadcasted_iota(jnp.int32, (n_slc, Q_BLOCK), 1)
    cur = lax.shift_right_logical(t_q, int(math.log2(SLC_BLOCK)))
    forced = (j_idx == 0) | (j_idx == cur) | (j_idx == cur - 1)
    v = jnp.where(j_idx <= cur, jnp.where(forced, jnp.inf, imp), -1.0)
    sel = jnp.zeros((n_slc, Q_BLOCK), F32)
    for _ in range(n_top):
        m = jnp.max(v, axis=0, keepdims=True)
        first = jnp.min(jnp.where((v == m) & (m >= 0.0), j_idx, n_slc), axis=0, keepdims=True)
        pick = j_idx == first
        sel = jnp.where(pick, 1.0, sel)
        v = jnp.where(pick, -1.0, v)
    neg = jnp.where(sel > 0, 0.0, -MASK_BIG).astype(BF16)
    selm_ref[...] = _dot(spread_ref[...], neg)
    cnt_ref[...] = _nt_dot(jnp.ones((8, Q_BLOCK), BF16), sel.astype(BF16))


def _cmp_select(qt, kc, vct, ovt, *, n_cmp):
    b, _, s = qt.shape
    g = C_KV_HEADS
    n_pad = kc.shape[1]
    n_slc = ovt.shape[0]
    nq = s // Q_BLOCK
    rows = C_REP * HEAD_DIM
    eye = jnp.asarray(np.eye(LANES), BF16)
    return pl.pallas_call(
        functools.partial(_cmp_body, n_cmp=n_cmp, n_top=min(SLC_TOPK, n_slc)),
        grid=(b, g, nq),
        in_specs=[
            pl.BlockSpec((None, rows, Q_BLOCK), lambda bb, gg, i: (bb, gg, i)),
            pl.BlockSpec((None, n_pad, LANES), lambda bb, gg, i: (bb, 0, 0)),
            pl.BlockSpec((None, LANES, n_pad), lambda bb, gg, i: (bb, 0, 0)),
            pl.BlockSpec((n_slc, n_pad), lambda bb, gg, i: (0, 0)),
            pl.BlockSpec((LANES, LANES), lambda bb, gg, i: (0, 0)),
            pl.BlockSpec((nq * AUG_ROWS, n_slc), lambda bb, gg, i: (0, 0)),
        ],
        out_specs=[
            pl.BlockSpec((Q_BLOCK, rows), lambda bb, gg, i: (bb * nq + i, gg)),
            pl.BlockSpec((None, None, None, nq * AUG_ROWS, Q_BLOCK), lambda bb, gg, i: (bb, gg, i, 0, 0)),
            pl.BlockSpec((None, None, None, 8, n_slc), lambda bb, gg, i: (bb, gg, i, 0, 0)),
        ],
        out_shape=[
            jax.ShapeDtypeStruct((b * s, g * rows), F32),
            jax.ShapeDtypeStruct((b, g, nq, nq * AUG_ROWS, Q_BLOCK), F32),
            jax.ShapeDtypeStruct((b, g, nq, 8, n_slc), F32),
        ],
        compiler_params=_cparams(("parallel", "parallel", "parallel")),
    )(qt, kc, vct, ovt, eye, jnp.asarray(_block_spread(nq), BF16))


def _slc_body(list_ref, qt_ref, ks_ref, vst_ref, selm_ref, eye_ref, o_ref, qaug, m_sc, l_sc, acc_sc, *, nq, stride):
    bb = pl.program_id(0)
    g = pl.program_id(1)
    i = pl.program_id(2)
    width = C_REP * Q_BLOCK
    slopes = _slopes(C_HEADS)
    slope_s = [jnp.where(g == 0, slopes[r], slopes[C_REP + r]) for r in range(C_REP)]

    own_rows = (lax.broadcasted_iota(jnp.int32, (LANES, width), 0) // HEAD_DIM) == g
    q6 = jnp.concatenate([qt_ref[r * HEAD_DIM:(r + 1) * HEAD_DIM, :] for r in range(C_REP)], axis=1)
    qaug[0:LANES, :] = jnp.where(own_rows, jnp.concatenate([q6, q6], axis=0), jnp.zeros((LANES, width), BF16))
    head = lax.broadcasted_iota(jnp.int32, (LANES, width), 1) // Q_BLOCK
    row = lax.broadcasted_iota(jnp.int32, (LANES, width), 0)
    slope_t = jnp.zeros((LANES, width), F32)
    for r in range(C_REP):
        slope_t = jnp.where(head == r, slope_s[r], slope_t)
    s_hi, s_lo = _split(slope_t)
    slope_rows = jnp.where(row == AUG_POS, s_hi.astype(F32), jnp.where(row == AUG_POS + 1, s_lo.astype(F32), 0.0))
    qaug[LANES:, :] = slope_rows.astype(BF16)

    q_loc = lax.broadcasted_iota(jnp.int32, (Q_BLOCK, Q_BLOCK), 1)
    k_loc = lax.broadcasted_iota(jnp.int32, (Q_BLOCK, Q_BLOCK), 0)
    causal = jnp.where(k_loc > q_loc, -MASK_BIG, 0.0)

    m_sc[...] = jnp.full(m_sc.shape, NEG_INF, F32)
    l_sc[...] = jnp.zeros(l_sc.shape, F32)
    acc_sc[...] = jnp.zeros(acc_sc.shape, F32)

    def scores(jj):
        k_tile = ks_ref[pl.ds(pl.multiple_of(jj * Q_BLOCK, Q_BLOCK), Q_BLOCK), :]
        rows = selm_ref[pl.ds(pl.multiple_of(jj * AUG_ROWS, AUG_ROWS), AUG_ROWS), :]
        mask = jnp.concatenate([jnp.broadcast_to(rows[e:e + 1], (SLC_BLOCK, Q_BLOCK)) for e in range(2)], axis=0)
        return _dot(k_tile, qaug[...]), mask

    def accumulate(tiles):
        ps = [[] for _ in tiles]
        alphas = []
        for r in range(C_REP):
            sl = slice(r * Q_BLOCK, (r + 1) * Q_BLOCK)
            m_old = m_sc[:, sl]
            m_new = m_old
            parts = []
            for jj, st, mask, extra in tiles:
                s = st[:, sl] + mask
                c = slope_s[r] * ((jj - i) * Q_BLOCK).astype(F32) + extra
                m_new = jnp.maximum(m_new, jnp.max(s, axis=0, keepdims=True) + c)
                parts.append((s, c))
            alpha = jnp.exp(m_old - m_new)
            l_new = alpha * l_sc[:, sl]
            for k, (s, c) in enumerate(parts):
                p = jnp.exp(s + (c - m_new))
                l_new = l_new + jnp.sum(p, axis=0, keepdims=True)
                ps[k].append(p.astype(BF16))
            l_sc[:, sl] = l_new
            m_sc[:, sl] = m_new
            alphas.append(alpha)
        pv = None
        for k, (jj, _, _, _) in enumerate(tiles):
            term = _dot(vst_ref[jj], jnp.concatenate(ps[k], axis=1))
            pv = term if pv is None else pv + term
        acc_sc[...] = jnp.concatenate(alphas, axis=1) * acc_sc[...] + pv

    st, mask = scores(i)
    accumulate([(i, st, mask + causal, 0.0)])

    base = ((bb * pl.num_programs(1) + g) * nq + i) * stride
    count = list_ref[base]

    def step(k, carry):
        tiles = []
        for u in range(SLC_GROUP):
            jj = list_ref[base + 1 + SLC_GROUP * k + u]
            pad = jnp.where(SLC_GROUP * k + u < count, 0.0, -MASK_BIG)
            tiles.append((jj, *scores(jj), pad))
        accumulate(tiles)
        return carry

    lax.fori_loop(0, (count + SLC_GROUP - 1) // SLC_GROUP, step, 0)
    l = l_sc[...]
    o = acc_sc[...] / jnp.where(l > 0, l, 1.0)
    o_ref[...] = _to_natural([o[:, r * Q_BLOCK:(r + 1) * Q_BLOCK] for r in range(C_REP)], eye_ref[...])


def _slc_attention(lists, qt, ks, vst, sel, *, stride):
    b, _, s = qt.shape
    g = C_KV_HEADS
    nq = s // Q_BLOCK
    rows = C_REP * HEAD_DIM
    width = C_REP * Q_BLOCK
    eye = jnp.asarray(np.eye(LANES), BF16)
    grid_spec = pltpu.PrefetchScalarGridSpec(
        num_scalar_prefetch=1,
        grid=(b, g, nq),
        in_specs=[
            pl.BlockSpec((None, rows, Q_BLOCK), lambda bb, gg, i, bits: (bb, gg, i)),
            pl.BlockSpec((None, s, 2 * LANES), lambda bb, gg, i, bits: (bb, 0, 0)),
            pl.BlockSpec((None, nq, None, HEAD_DIM, Q_BLOCK), lambda bb, gg, i, bits: (bb, 0, gg, 0, 0)),
            pl.BlockSpec((None, None, None, nq * AUG_ROWS, Q_BLOCK), lambda bb, gg, i, bits: (bb, gg, i, 0, 0)),
            pl.BlockSpec((LANES, LANES), lambda bb, gg, i, bits: (0, 0)),
        ],
        out_specs=pl.BlockSpec((Q_BLOCK, rows), lambda bb, gg, i, bits: (bb * nq + i, gg)),
        scratch_shapes=[
            pltpu.VMEM((2 * LANES, width), BF16),
            pltpu.VMEM((1, width), F32),
            pltpu.VMEM((1, width), F32),
            pltpu.VMEM((HEAD_DIM, width), F32),
        ],
    )
    return pl.pallas_call(
        functools.partial(_slc_body, nq=nq, stride=stride),
        grid_spec=grid_spec,
        out_shape=jax.ShapeDtypeStruct((b * s, g * rows), F32),
        compiler_params=_cparams(("parallel", "parallel", "parallel")),
    )(lists, qt, ks, vst, sel, eye)


def _merge_body(x_ref, g_ref, oa0, la0, oa1, la1, oa2, la2, ob_ref, ocmp_ref, oslc_ref, owin_ref, cg_ref,
                p4t_ref, p16t_ref, ex_ref, wg0_ref, wg1_ref, wg2_ref, wa_ref, wb_ref, wc_ref,
                out_ref, h_ref, oall_ref):
    @pl.when(pl.program_id(1) == 0)
    def _():
        h_ref[...] = _rms_rows(x_ref[...], g_ref[...])

        def natural(ref, pt_ref):
            hi, lo = _split(ref[...].reshape(TM, A_OUT))
            return _dot(pt_ref[...], hi) + _dot(pt_ref[...], lo)

        o0, l0 = oa0[...], la0[...]
        o1, l1 = natural(oa1, p4t_ref), natural(la1, p4t_ref)
        o2, l2 = natural(oa2, p16t_ref), natural(la2, p16t_ref)
        mx = jnp.maximum(jnp.maximum(l0, l1), l2)
        e0, e1, e2 = jnp.exp(l0 - mx), jnp.exp(l1 - mx), jnp.exp(l2 - mx)
        oall_ref[:, 0:A_OUT] = ((e0 * o0 + e1 * o1 + e2 * o2) / (e0 + e1 + e2)).astype(BF16)
        oall_ref[:, A_OUT:A_OUT + B_HEADS * HEAD_DIM] = ob_ref[...].astype(BF16)
        cg_hi, cg_lo = _split(cg_ref[...])
        o_c = None
        for w, o_ref in enumerate((ocmp_ref, oslc_ref, owin_ref)):
            term = (_dot(cg_hi, ex_ref[w]) + _dot(cg_lo, ex_ref[w])) * o_ref[...]
            o_c = term if o_c is None else o_c + term
        oall_ref[:, A_OUT + B_HEADS * HEAD_DIM:] = o_c.astype(BF16)

    h = h_ref[...]
    c0, c1 = A_OUT, A_OUT + B_HEADS * HEAD_DIM
    merged = jax.nn.sigmoid(_dot(h, wg0_ref[...])) * _dot(oall_ref[:, 0:c0], wa_ref[...])
    merged += jax.nn.sigmoid(_dot(h, wg1_ref[...])) * _dot(oall_ref[:, c0:c1], wb_ref[...])
    merged += jax.nn.sigmoid(_dot(h, wg2_ref[...])) * _dot(oall_ref[:, c1:], wc_ref[...])
    out_ref[...] = merged.astype(BF16)


def _merge(x, g, a_outs, ob, ocmp, oslc, owin, cg, w_gate, wa, wb, wc, ex, *, tn=512):
    t, d = x.shape
    per16 = CHUNK16 // TM
    n_t = d // tn
    (oa0, la0), (oa1, la1), (oa2, la2) = a_outs

    def rows(a):
        return pl.BlockSpec((TM, a.shape[1]), lambda i, n: (i, 0))

    a1_spec = pl.BlockSpec((None, 4, Q_BLOCK, A_OUT), lambda i, n: (i, 0, 0, 0))
    a2_spec = pl.BlockSpec((None, 16, TM // 16, A_OUT), lambda i, n: (i // per16, 0, i % per16, 0))
    p4t = jnp.asarray(_deinterleave(TM, 4).T, BF16)
    p16t = jnp.asarray(_deinterleave(TM, 16).T, BF16)
    in_specs = [
        rows(x), _resident((1, d)),
        rows(oa0), rows(la0), a1_spec, a1_spec, a2_spec, a2_spec,
        rows(ob), rows(ocmp), rows(oslc), rows(owin), rows(cg),
        _resident((TM, TM)), _resident((TM, TM)), _resident(ex.shape),
        pl.BlockSpec((d, tn), lambda i, n: (0, n)),
        pl.BlockSpec((d, tn), lambda i, n: (0, n + n_t)),
        pl.BlockSpec((d, tn), lambda i, n: (0, n + 2 * n_t)),
        pl.BlockSpec((wa.shape[0], tn), lambda i, n: (0, n)),
        pl.BlockSpec((wb.shape[0], tn), lambda i, n: (0, n)),
        pl.BlockSpec((wc.shape[0], tn), lambda i, n: (0, n)),
    ]
    return pl.pallas_call(
        _merge_body,
        grid=(t // TM, n_t),
        in_specs=in_specs,
        out_specs=pl.BlockSpec((TM, tn), lambda i, n: (i, n)),
        out_shape=jax.ShapeDtypeStruct((t, d), BF16),
        scratch_shapes=[pltpu.VMEM((TM, d), BF16), pltpu.VMEM((TM, wa.shape[0] + wb.shape[0] + wc.shape[0]), BF16)],
        compiler_params=_cparams(("parallel", "arbitrary")),
    )(x, g.reshape(1, d), oa0, la0, oa1, la1, oa2, la2, ob, ocmp, oslc, owin, cg, p4t, p16t, ex,
      w_gate, w_gate, w_gate, wa, wb, wc)


def _out_body(x_ref, m_ref, w_ref, o_ref):
    o_ref[...] = x_ref[...] + _dot(m_ref[...], w_ref[...])


def _out_proj(x, merged, w_out):
    t, d = x.shape
    return pl.pallas_call(
        _out_body,
        grid=(t // TM,),
        in_specs=[pl.BlockSpec((TM, d), lambda i: (i, 0)), pl.BlockSpec((TM, d), lambda i: (i, 0)),
                  _resident((d, d))],
        out_specs=pl.BlockSpec((TM, d), lambda i: (i, 0)),
        out_shape=jax.ShapeDtypeStruct((t, d), F32),
        compiler_params=_cparams(("parallel",)),
    )(x, merged, w_out)


def _qkv_column_params(qk_gain):
    flag, gain, scale = [], [], []
    one = jnp.ones((HEAD_DIM,), F32)

    def add(n_heads, normed, is_q, gvec):
        for _ in range(n_heads):
            flag.append(np.full((HEAD_DIM,), 1.0 if normed else 0.0, np.float32))
            gain.append(gvec if normed else one)
            scale.append(np.full((HEAD_DIM,), HEAD_DIM ** -0.5 if is_q else 1.0, np.float32))

    for _ in range(len(A_GROUPS)):
        add(A_HEADS_PER_GROUP, True, True, qk_gain[0, 0])
        add(A_HEADS_PER_GROUP, True, False, qk_gain[0, 1])
        add(A_HEADS_PER_GROUP, False, False, one)
    add(B_HEADS, True, True, qk_gain[1, 0])
    add(B_KV_HEADS, True, False, qk_gain[1, 1])
    add(B_KV_HEADS, False, False, one)
    add(C_HEADS, True, True, qk_gain[2, 0])
    for normed in (False, False, True, False, True, False):
        add(C_KV_HEADS, normed, False, qk_gain[2, 1])
    flag = np.concatenate(flag)
    assert flag.shape[0] == QKV_COLS
    return jnp.asarray(flag), jnp.concatenate(gain) * jnp.asarray(np.concatenate(scale))


def _overlap_t(n_slc, n_pad, n_cmp):
    n = np.arange(n_pad)[None, :]
    j = np.arange(n_slc)[:, None]
    start, end = CMP_STRIDE * n, CMP_STRIDE * n + CMP_BLOCK - 1
    ov = (start <= SLC_BLOCK * j + SLC_BLOCK - 1) & (end >= SLC_BLOCK * j) & (n < n_cmp)
    return jnp.asarray(ov, BF16)


def _gate_expand():
    ex = np.zeros((3, LANES, C_HEADS * HEAD_DIM), np.float32)
    for w in range(3):
        for h in range(C_HEADS):
            ex[w, h * 3 + w, h * HEAD_DIM:(h + 1) * HEAD_DIM] = 1.0
    return jnp.asarray(ex, BF16)


def _compress_weights(cmp_pos, cmp_w1, cmp_w2):
    n_q = 2 * C_KV_HEADS
    w1 = cmp_w1.reshape(2, 2, CMP_STRIDE, HEAD_DIM, CMP_HIDDEN)
    w1q = jnp.repeat(w1, C_KV_HEADS, axis=0)
    wb = jnp.einsum("qhcdn,qp->hcqdpn", w1q, jnp.eye(n_q, dtype=F32))
    wb = wb.reshape(2, CMP_STRIDE, n_q * HEAD_DIM, n_q * CMP_HIDDEN).astype(BF16)
    pos = cmp_pos.reshape(2, 2, CMP_STRIDE, HEAD_DIM)
    prow = jnp.repeat(pos, C_KV_HEADS, axis=0).transpose(1, 2, 0, 3).reshape(2, CMP_STRIDE, 1, n_q * HEAD_DIM)
    prow = jnp.broadcast_to(prow, (2, CMP_STRIDE, 8, n_q * HEAD_DIM)).astype(BF16)
    eye_g = jnp.eye(C_KV_HEADS, dtype=F32)
    w2k = jnp.kron(eye_g, cmp_w2[0]).astype(BF16)
    w2vt = jnp.kron(eye_g, cmp_w2[1]).T.astype(BF16)
    return wb, prow, w2k, w2vt


def _token_mixing(x, b, s, mix_norm, w_in, qk_gain, sinks, cmp_pos, cmp_w1, cmp_w2, w_a, w_b, w_c, w_out):
    t, d = x.shape
    assert s % CHUNK16 == 0 and d % 512 == 0
    c_gate_cols = 3 * C_HEADS
    w_qkv = w_in[:, :QKV_COLS + LANES].astype(BF16)
    flag, gs = _qkv_column_params(qk_gain)
    a0, a1, a2, bsec, cq, ckv, cmpd, qt, ks, vst, cg = _qkv_proj(x, mix_norm, w_qkv, flag, gs, b, s)

    a_outs = [_dilated_group(a0.reshape(b, s, SEC), 0, b, s), _dilated_group(a1, 1, b, s),
              _dilated_group(a2, 2, b, s)]
    a_outs[0] = tuple(v.reshape(t, A_OUT) for v in a_outs[0])
    o_b = _sink_swa(bsec.reshape(b, s, SEC), sinks.astype(F32), b, s).reshape(t, -1)
    o_win = _nsa_window(cq.reshape(b, s, SEC), ckv.reshape(b, s, SEC), b, s).reshape(t, -1)

    n_chunks = s // CMP_STRIDE
    n_cmp = (s - CMP_BLOCK) // CMP_STRIDE + 1
    n_slc = s // SLC_BLOCK
    nq = s // Q_BLOCK
    kg = jnp.tile(qk_gain[2, 1], C_KV_HEADS).reshape(1, LANES)
    kc, vct = _compress(cmpd, *_compress_weights(cmp_pos, cmp_w1, cmp_w2), kg)
    o_cmp, sel, cnt = _cmp_select(qt, kc, vct, _overlap_t(n_slc, n_chunks, n_cmp), n_cmp=n_cmp)

    act = (cnt[:, :, :, 0, :] > 0).reshape(b, C_KV_HEADS, nq, nq, 2).any(axis=-1)
    act = act & (jnp.arange(nq)[None, :] < jnp.arange(nq)[:, None])
    order = jnp.argsort(jnp.logical_not(act), axis=-1, stable=True).astype(jnp.int32)
    count = jnp.sum(act, axis=-1, dtype=jnp.int32)[..., None]
    lists = jnp.concatenate([count, order] + [jnp.zeros_like(count)] * (SLC_GROUP - 1), axis=-1)
    o_slc = _slc_attention(lists.reshape(-1), qt, ks.reshape(b, s, 2 * LANES),
                           vst.reshape(b, nq, C_KV_HEADS, HEAD_DIM, Q_BLOCK), sel, stride=nq + SLC_GROUP)

    merged = _merge(x, mix_norm, a_outs, o_b, o_cmp, o_slc, o_win, cg,
                    w_in[:, QKV_COLS + c_gate_cols:].astype(BF16), w_a.astype(BF16), w_b.astype(BF16),
                    w_c.astype(BF16), _gate_expand())
    return _out_proj(x, merged, w_out.astype(BF16))


def kernel(x, ffn1_norm, ffn1_w_gu, ffn1_w_down, mix_norm, w_in, qk_gain, sinks, cmp_pos, cmp_w1, cmp_w2,
           w_branch_a, w_branch_b, w_branch_c, w_out, ffn2_norm, ffn2_w_gu, ffn2_w_down):
    b, s, d = x.shape
    h = x.reshape(b * s, d)
    w1_gu, w1_down = ffn1_w_gu.astype(BF16), ffn1_w_down.astype(BF16)
    w2_gu, w2_down = ffn2_w_gu.astype(BF16), ffn2_w_down.astype(BF16)
    for l in range(ffn1_norm.shape[0]):
        h = _ffn(h, ffn1_norm[l], w1_gu, w1_down, l)
        h = _token_mixing(h, b, s, mix_norm[l], w_in[l], qk_gain[l], sinks[l], cmp_pos[l], cmp_w1[l], cmp_w2[l],
                          w_branch_a[l], w_branch_b[l], w_branch_c[l], w_out[l])
        h = _ffn(h, ffn2_norm[l], w2_gu, w2_down, l)
    return h.reshape(b, s, d)
```

```python
import functools
import math

import numpy as np
import jax
import jax.numpy as jnp
from jax import lax
from jax.experimental import pallas as pl
from jax.experimental.pallas import tpu as pltpu

F32 = jnp.float32
BF16 = jnp.bfloat16

HEAD_DIM = 64
Q_BLOCK = 128
LANES = 128
A_GROUPS = ((128, 1), (512, 4), (2048, 16))
A_HEADS_PER_GROUP = 4
A_HEADS = 12
A_OUT = A_HEADS_PER_GROUP * HEAD_DIM
B_HEADS = 8
B_KV_HEADS = 2
B_WINDOW = 128
C_HEADS = 12
C_KV_HEADS = 2
C_REP = C_HEADS // C_KV_HEADS
CMP_BLOCK = 32
CMP_STRIDE = 16
CMP_HIDDEN = 256
SLC_BLOCK = 64
SLC_TOPK = 16
C_WINDOW = 512
RMS_EPS = 1e-6
NEG_INF = -1e30
SEC = 768
N_SEC = 6
QKV_COLS = SEC * N_SEC
TM = 512
CHUNK16 = Q_BLOCK * 16
VMEM_LIMIT = 56 * 1024 * 1024


def _slopes(n):
    return [float(2.0 ** (-8.0 * (h + 1) / n)) for h in range(n)]


def _cparams(sem):
    return pltpu.CompilerParams(dimension_semantics=sem, vmem_limit_bytes=VMEM_LIMIT)


def _dot(a, b):
    return jnp.dot(a, b, preferred_element_type=F32)


def _nt_dot(a, b):
    return lax.dot_general(a, b, (((1,), (1,)), ((), ())), preferred_element_type=F32)


def _split(v):
    hi = v.astype(BF16)
    return hi, (v - hi.astype(F32)).astype(BF16)


def _resident(shape):
    return pl.BlockSpec(shape, lambda *_: (0,) * len(shape), pipeline_mode=pl.Buffered(1))


def _rms_rows(x, g):
    ms = jnp.mean(x * x, axis=-1, keepdims=True)
    return (x * lax.rsqrt(ms + RMS_EPS) * g).astype(BF16)


def _deinterleave(n, d):
    p = np.zeros((n, n), np.float32)
    r = np.arange(n // d)
    for c in range(d):
        p[c * (n // d) + r, d * r + c] = 1.0
    return p


AUG_POS = 0
AUG_ROWS = 8
MASK_BIG = 1e30
SLC_GROUP = 3


def _key_pattern(n):
    pat = np.zeros((n, LANES), np.float32)
    pat[:, AUG_POS] = pat[:, AUG_POS + 1] = np.arange(n) % Q_BLOCK
    return pat


def _block_spread(n_tiles):
    m = np.zeros((n_tiles * AUG_ROWS, 2 * n_tiles), np.float32)
    jj = np.arange(n_tiles)
    for e in range(2):
        m[AUG_ROWS * jj + e, 2 * jj + e] = 1.0
    return m


def _ffn_body(x_ref, g_ref, wg_ref, wu_ref, wd_ref, o_ref, h_ref, *, n_f):
    f = pl.program_id(1)

    @pl.when(f == 0)
    def _():
        h_ref[...] = _rms_rows(x_ref[...], g_ref[...])
        o_ref[...] = jnp.zeros_like(o_ref)

    h = h_ref[...]
    gate = _dot(h, wg_ref[...])
    up = _dot(h, wu_ref[...])
    act = (gate * jax.nn.sigmoid(gate) * up).astype(BF16)
    o_ref[...] += _dot(act, wd_ref[...])

    @pl.when(f == n_f - 1)
    def _():
        o_ref[...] = x_ref[...] + 0.5 * o_ref[...]


def _ffn(x, g, w_gu, w_down, layer, *, tm=512, tf=512):
    t, d = x.shape
    d_ff = w_down.shape[1]
    n_f = d_ff // tf
    return pl.pallas_call(
        functools.partial(_ffn_body, n_f=n_f),
        grid=(t // tm, n_f),
        in_specs=[
            pl.BlockSpec((tm, d), lambda i, f: (i, 0)),
            pl.BlockSpec((1, d), lambda i, f: (0, 0)),
            pl.BlockSpec((None, d, tf), lambda i, f: (layer, 0, f)),
            pl.BlockSpec((None, d, tf), lambda i, f: (layer, 0, f + n_f)),
            pl.BlockSpec((None, tf, d), lambda i, f: (layer, f, 0)),
        ],
        out_specs=pl.BlockSpec((tm, d), lambda i, f: (i, 0)),
        out_shape=jax.ShapeDtypeStruct((t, d), F32),
        scratch_shapes=[pltpu.VMEM((tm, d), BF16)],
        compiler_params=_cparams(("parallel", "arbitrary")),
    )(x, g.reshape(1, d), w_gu, w_gu, w_down)


NORM_TILE = 256
NORM_TILES = {0: (0, 1), 1: (0, 1), 2: (0, 1), 3: (0, 1, 2), 4: (0, 1, 2), 5: (1, 2)}


def _head_sumsq(y, bd):
    return _dot((y * y).astype(BF16), bd)


def _qkv_body(x_ref, g_ref, w_ref, flag_ref, gs_ref, bd_ref, p4_ref, p16_ref, eye_ref, kpat_ref,
              a0_ref, a1_ref, a2_ref, b_ref, cq_ref, ckv_ref, cmpd_ref, qt_ref, ks_ref, vst_ref, cg_ref):
    h = _rms_rows(x_ref[...], g_ref[...])
    bd = bd_ref[...]

    def section(k):
        sl = slice(k * SEC, (k + 1) * SEC)
        if k < len(A_GROUPS):
            y = jnp.concatenate([_dot(h, w_ref[:, part * A_HEADS * HEAD_DIM + k * A_OUT:
                                                  part * A_HEADS * HEAD_DIM + (k + 1) * A_OUT])
                                 for part in range(3)], axis=1)
        else:
            y = _dot(h, w_ref[:, sl])
        tiles = []
        for c in range(SEC // NORM_TILE):
            yc = y[:, c * NORM_TILE:(c + 1) * NORM_TILE]
            if c in NORM_TILES[k]:
                cols = slice(k * SEC + c * NORM_TILE, k * SEC + (c + 1) * NORM_TILE)
                inv = lax.rsqrt(_head_sumsq(yc, bd) * (1.0 / HEAD_DIM) + RMS_EPS)
                yc = yc * jnp.where(flag_ref[:, cols] > 0, inv, 1.0) * gs_ref[:, cols]
            tiles.append(yc.astype(BF16))
        return jnp.concatenate(tiles, axis=1)

    a0_ref[...] = section(0)
    a1_ref[...] = _dot(p4_ref[...], section(1)).astype(BF16).reshape(a1_ref.shape)
    a2_ref[...] = _dot(p16_ref[...], section(2)).astype(BF16).reshape(a2_ref.shape)
    b_ref[...] = section(3)
    y_cq = section(4)
    cq_ref[...] = y_cq
    qt_ref[...] = _nt_dot(eye_ref[...], y_cq).astype(BF16)
    y_ckv = section(5)
    ckv_ref[...] = y_ckv
    cmpd_ref[...] = _dot(p16_ref[...], y_ckv[:, 0:2 * LANES]).astype(BF16).reshape(cmpd_ref.shape)
    ks_ref[:, 0:LANES] = y_ckv[:, 2 * LANES:3 * LANES]
    ks_ref[:, LANES:2 * LANES] = kpat_ref[...]
    eye = eye_ref[0:LANES, 0:LANES]
    for kb in range(vst_ref.shape[0]):
        vt = _nt_dot(eye, y_ckv[kb * Q_BLOCK:(kb + 1) * Q_BLOCK, 3 * LANES:4 * LANES]).astype(BF16)
        for gg in range(C_KV_HEADS):
            vst_ref[kb, gg] = vt[gg * HEAD_DIM:(gg + 1) * HEAD_DIM]
    cg_ref[...] = jax.nn.sigmoid(_dot(h, w_ref[:, QKV_COLS:QKV_COLS + LANES]))


def _qkv_proj(x, g, w, flag, gs, b, s):
    t, d = x.shape
    tiles_per_batch = s // TM
    per16 = CHUNK16 // TM
    bd = jnp.asarray(np.kron(np.eye(NORM_TILE // HEAD_DIM), np.ones((HEAD_DIM, HEAD_DIM))), BF16)
    p4 = jnp.asarray(_deinterleave(TM, 4), BF16)
    p16 = jnp.asarray(_deinterleave(TM, 16), BF16)
    eye = jnp.asarray(np.eye(SEC), BF16)
    nat = pl.BlockSpec((TM, SEC), lambda i: (i, 0))
    out_specs = [
        nat,
        pl.BlockSpec((None, 4, Q_BLOCK, SEC), lambda i: (i, 0, 0, 0)),
        pl.BlockSpec((None, 16, TM // 16, SEC), lambda i: (i // per16, 0, i % per16, 0)),
        nat, nat, nat,
        pl.BlockSpec((None, 16, TM // 16, 2 * LANES), lambda i: (i // tiles_per_batch, 0, i % tiles_per_batch, 0)),
        pl.BlockSpec((None, SEC, TM), lambda i: (i // tiles_per_batch, 0, i % tiles_per_batch)),
        pl.BlockSpec((TM, 2 * LANES), lambda i: (i, 0)),
        pl.BlockSpec((TM // Q_BLOCK, C_KV_HEADS, HEAD_DIM, Q_BLOCK), lambda i: (i, 0, 0, 0)),
        pl.BlockSpec((TM, LANES), lambda i: (i, 0)),
    ]
    out_shape = [
        jax.ShapeDtypeStruct((t, SEC), BF16),
        jax.ShapeDtypeStruct((t // TM, 4, Q_BLOCK, SEC), BF16),
        jax.ShapeDtypeStruct((t // CHUNK16, 16, Q_BLOCK, SEC), BF16),
        jax.ShapeDtypeStruct((t, SEC), BF16),
        jax.ShapeDtypeStruct((t, SEC), BF16),
        jax.ShapeDtypeStruct((t, SEC), BF16),
        jax.ShapeDtypeStruct((b, 16, s // 16, 2 * LANES), BF16),
        jax.ShapeDtypeStruct((b, SEC, s), BF16),
        jax.ShapeDtypeStruct((t, 2 * LANES), BF16),
        jax.ShapeDtypeStruct((t // Q_BLOCK, C_KV_HEADS, HEAD_DIM, Q_BLOCK), BF16),
        jax.ShapeDtypeStruct((t, LANES), F32),
    ]
    n_w = w.shape[1]
    return pl.pallas_call(
        _qkv_body,
        grid=(t // TM,),
        in_specs=[
            pl.BlockSpec((TM, d), lambda i: (i, 0)),
            _resident((1, d)),
            _resident((d, n_w)),
            _resident((1, QKV_COLS)),
            _resident((1, QKV_COLS)),
            _resident((NORM_TILE, NORM_TILE)),
            _resident((TM, TM)),
            _resident((TM, TM)),
            _resident((SEC, SEC)),
            _resident((TM, LANES)),
        ],
        out_specs=out_specs,
        out_shape=out_shape,
        compiler_params=_cparams(("parallel",)),
    )(x, g.reshape(1, d), w, flag.reshape(1, -1), gs.reshape(1, -1), bd, p4, p16, eye,
      jnp.asarray(_key_pattern(TM), BF16))


def _banded_body(*refs, nb, heads, k_off, v_off, n_pairs, q_axis, use_sinks, with_lse, stack):
    refs = list(refs)
    q_ref = refs.pop(0)
    kv_refs = [refs.pop(0) for _ in range(nb + 1)]
    qc_ref, kaug_ref, band_ref = refs.pop(0), refs.pop(0), refs.pop(0)
    sink_ref = refs.pop(0) if use_sinks else None
    o_ref = refs.pop(0)
    lse_ref = refs.pop(0) if with_lse else None

    i = pl.program_id(q_axis)
    nk = (nb + 1) * Q_BLOCK
    col = lax.broadcasted_iota(jnp.int32, (Q_BLOCK, nk), 1)
    mask = band_ref[...] + jnp.where(col < (nb - i) * Q_BLOCK, -MASK_BIG, 0.0)
    rel_f = (nb * Q_BLOCK + lax.broadcasted_iota(jnp.int32, (Q_BLOCK, nk), 0) - col).astype(F32)
    lane = lax.broadcasted_iota(jnp.int32, (Q_BLOCK, LANES), 1)
    low_half = lane < HEAD_DIM
    kaug = kaug_ref[...]

    kv_cache = {}

    def kv_tile(off, kv_pair, swapped):
        key = (off, kv_pair, swapped)
        if key not in kv_cache:
            c0 = off + kv_pair * LANES
            tile = jnp.concatenate([r[:, c0:c0 + LANES] for r in kv_refs], axis=0)
            if swapped:
                tile = pltpu.roll(tile.astype(F32), HEAD_DIM, 1).astype(BF16)
            kv_cache[key] = tile
        return kv_cache[key]

    classes = {}
    for head in heads:
        pair, half, kv_pair, kv_half, slope, hidx = head
        key = (kv_pair, kv_half != half) if stack else hidx
        classes.setdefault(key, []).append(head)

    outs = [[None, None] for _ in range(n_pairs)]
    lses = [[None, None] for _ in range(n_pairs)]
    for members in classes.values():
        n_h = len(members)
        kv_pair, swapped = members[0][2], members[0][3] != members[0][1]
        q_rows = []
        for pair, half, _, _, _, hidx in members:
            qp = q_ref[:, pair * LANES:(pair + 1) * LANES]
            own = low_half if half == 0 else jnp.logical_not(low_half)
            qm = jnp.where(own, qp, jnp.zeros_like(qp))
            q_rows.append(jnp.concatenate([qm, qc_ref[hidx]], axis=1) if stack else qm)
        if stack:
            k_aug = jnp.concatenate([kv_tile(k_off, kv_pair, swapped), kaug], axis=1)
            s = _nt_dot(jnp.concatenate(q_rows, axis=0), k_aug)
            s = (s.reshape(n_h, Q_BLOCK, nk) + mask[None]).reshape(n_h * Q_BLOCK, nk)
        else:
            s = _nt_dot(q_rows[0], kv_tile(k_off, kv_pair, swapped)) - members[0][4] * rel_f + mask
        m = jnp.max(s, axis=1, keepdims=True)
        if use_sinks:
            assert n_h == 1
            sink = sink_ref[members[0][5]]
            m = jnp.maximum(m, sink)
        p = jnp.exp(s - m)
        den = jnp.sum(p, axis=1, keepdims=True)
        if use_sinks:
            den = den + jnp.exp(sink - m)
        r = _dot(p.astype(BF16), kv_tile(v_off, kv_pair, swapped)) / den
        lse = m + jnp.log(den) if with_lse else None
        for k, (pair, half, _, _, _, _) in enumerate(members):
            outs[pair][half] = r[k * Q_BLOCK:(k + 1) * Q_BLOCK]
            if with_lse:
                lses[pair][half] = jnp.broadcast_to(lse[k * Q_BLOCK:(k + 1) * Q_BLOCK], (Q_BLOCK, LANES))
    for pair in range(n_pairs):
        sl = slice(pair * LANES, (pair + 1) * LANES)
        o_ref[:, sl] = jnp.where(low_half, outs[pair][0], outs[pair][1])
        if with_lse:
            lse_ref[:, sl] = jnp.where(low_half, lses[pair][0], lses[pair][1])


def _banded_consts(heads, nb, max_dist):
    nk = (nb + 1) * Q_BLOCK
    slope = np.asarray([h[4] for h in heads], np.float32)[:, None]
    q_dist = (nb * Q_BLOCK + np.arange(Q_BLOCK, dtype=np.float32))[None, :]
    ones = np.ones_like(q_dist)
    vals = jnp.asarray(np.stack([slope * ones, slope * Q_BLOCK * ones, -slope * q_dist], axis=-1))
    hi = vals.astype(BF16)
    lo = (vals - hi.astype(F32)).astype(BF16)
    cols = jnp.stack([hi[..., 0], lo[..., 0], hi[..., 1], lo[..., 1], hi[..., 2], lo[..., 2]], axis=-1)
    qc = jnp.pad(cols, ((0, 0), (0, 0), (0, LANES - cols.shape[-1])))
    kaug = np.zeros((nk, LANES), np.float32)
    kaug[:, 0] = kaug[:, 1] = np.arange(nk) % Q_BLOCK
    kaug[:, 2] = kaug[:, 3] = np.arange(nk) // Q_BLOCK
    kaug[:, 4] = kaug[:, 5] = 1.0
    rel = nb * Q_BLOCK + np.arange(Q_BLOCK)[:, None] - np.arange(nk)[None, :]
    band = np.where((rel >= 0) & (rel <= max_dist), 0.0, -MASK_BIG).astype(np.float32)
    return [qc, jnp.asarray(kaug, BF16), jnp.asarray(band)]


def _banded_call(q_arr, kv_arr, *, grid, q_map, kv_map, out_map, out_lead, out_cols, nb, max_dist, heads,
                 k_off, v_off, q_axis, sinks=None, with_lse=False, stack=False):
    lead = (None,) * (q_arr.ndim - 2)
    blk = lead + (Q_BLOCK, SEC)
    in_specs = [pl.BlockSpec(blk, q_map)]
    args = [q_arr]
    for back in range(nb, -1, -1):
        in_specs.append(pl.BlockSpec(blk, functools.partial(kv_map, back=back)))
        args.append(kv_arr)
    consts = _banded_consts(heads, nb, max_dist)
    in_specs += [pl.BlockSpec(c.shape, lambda *_, nd=c.ndim: (0,) * nd) for c in consts]
    args += consts
    if sinks is not None:
        in_specs.append(pl.BlockSpec(memory_space=pltpu.SMEM))
        args.append(sinks)
    oblk = pl.BlockSpec(lead + (Q_BLOCK, out_cols), out_map)
    oshape = jax.ShapeDtypeStruct(out_lead + (out_cols,), F32)
    body = functools.partial(_banded_body, nb=nb, heads=heads, k_off=k_off, v_off=v_off,
                             n_pairs=out_cols // LANES, q_axis=q_axis, use_sinks=sinks is not None,
                             with_lse=with_lse, stack=stack)
    return pl.pallas_call(
        body,
        grid=grid,
        in_specs=in_specs,
        out_specs=[oblk, oblk] if with_lse else oblk,
        out_shape=[oshape, oshape] if with_lse else oshape,
        compiler_params=_cparams(("parallel",) * len(grid)),
    )(*args)


def _dilated_group(arr, gi, b, s):
    window, dil = A_GROUPS[gi]
    slopes = _slopes(A_HEADS)
    heads = tuple((hh // 2, hh % 2, hh // 2, hh % 2, slopes[gi * A_HEADS_PER_GROUP + hh] * dil, hh)
                  for hh in range(A_HEADS_PER_GROUP))
    common = dict(out_cols=A_OUT, nb=1, max_dist=window // dil, heads=heads, k_off=256, v_off=512,
                  with_lse=True)
    if dil == 1:
        return _banded_call(
            arr, arr, grid=(b, s // Q_BLOCK), q_axis=1,
            q_map=lambda bb, i: (bb, i, 0),
            kv_map=lambda bb, i, back: (bb, jnp.maximum(i - back, 0), 0),
            out_map=lambda bb, i: (bb, i, 0), out_lead=(b, s), **common)
    nc = s // (Q_BLOCK * dil)
    return _banded_call(
        arr, arr, grid=(b, dil, nc), q_axis=2,
        q_map=lambda bb, c, i: (bb * nc + i, c, 0, 0),
        kv_map=lambda bb, c, i, back: (bb * nc + jnp.maximum(i - back, 0), c, 0, 0),
        out_map=lambda bb, c, i: (bb * nc + i, c, 0, 0), out_lead=(b * nc, dil, Q_BLOCK), **common)


def _sink_swa(arr, sinks, b, s):
    slopes = _slopes(B_HEADS)
    rep = B_HEADS // B_KV_HEADS
    heads = tuple((h // 2, h % 2, 0, h // rep, slopes[h], h) for h in range(B_HEADS))
    return _banded_call(
        arr, arr, grid=(b, s // Q_BLOCK), q_axis=1,
        q_map=lambda bb, i: (bb, i, 0),
        kv_map=lambda bb, i, back: (bb, jnp.maximum(i - back, 0), 0),
        out_map=lambda bb, i: (bb, i, 0), out_lead=(b, s), out_cols=B_HEADS * HEAD_DIM,
        nb=1, max_dist=B_WINDOW - 1, heads=heads, k_off=512, v_off=640, sinks=sinks)


def _nsa_window(cq, ckv, b, s):
    slopes = _slopes(C_HEADS)
    heads = tuple((h // 2, h % 2, 0, h // C_REP, slopes[h], h) for h in range(C_HEADS))
    return _banded_call(
        cq, ckv, grid=(b, s // Q_BLOCK), q_axis=1,
        q_map=lambda bb, i: (bb, i, 0),
        kv_map=lambda bb, i, back: (bb, jnp.maximum(i - back, 0), 0),
        out_map=lambda bb, i: (bb, i, 0), out_lead=(b, s), out_cols=C_HEADS * HEAD_DIM,
        nb=-(-(C_WINDOW - 1) // Q_BLOCK), max_dist=C_WINDOW - 1, heads=heads, k_off=512, v_off=640, stack=True)


def _compress_body(t_ref, wb_ref, prow_ref, w2k_ref, w2vt_ref, kg_ref, bd_ref, kc_ref, vct_ref, *, n_chunks):
    hid_cols = 2 * C_KV_HEADS * CMP_HIDDEN
    u = jnp.zeros((n_chunks, hid_cols), F32)
    v = jnp.zeros((n_chunks, hid_cols), F32)
    pc = jnp.zeros((1, hid_cols), F32)
    for c in range(CMP_STRIDE):
        tc = t_ref[c]
        u = u + _dot(tc, wb_ref[0, c])
        v = v + _dot(tc, wb_ref[1, c])
        pc = pc + _dot(prow_ref[0, c], wb_ref[0, c])[0:1] + _dot(prow_ref[1, c], wb_ref[1, c])[0:1]
    hsum = u + pltpu.roll(v, n_chunks - 1, 0) + pc
    hid = (hsum * jax.nn.sigmoid(hsum)).astype(BF16)
    half = C_KV_HEADS * CMP_HIDDEN
    k = _dot(hid[:, :half], w2k_ref[...])
    hi, lo = _split(k * k)
    ss = _dot(hi, bd_ref[...]) + _dot(lo, bd_ref[...])
    kc_ref[...] = (k * lax.rsqrt(ss * (1.0 / HEAD_DIM) + RMS_EPS) * kg_ref[...]).astype(BF16)
    vct_ref[...] = _nt_dot(w2vt_ref[...], hid[:, half:]).astype(BF16)


def _compress(cmpd, wb, prow, w2k, w2vt, kg):
    b, _, n_chunks, width = cmpd.shape
    bd = jnp.asarray(np.kron(np.eye(LANES // HEAD_DIM), np.ones((HEAD_DIM, HEAD_DIM))), BF16)
    return pl.pallas_call(
        functools.partial(_compress_body, n_chunks=n_chunks),
        grid=(b,),
        in_specs=[
            pl.BlockSpec((None, CMP_STRIDE, n_chunks, width), lambda bb: (bb, 0, 0, 0)),
            _resident(wb.shape), _resident(prow.shape), _resident(w2k.shape), _resident(w2vt.shape),
            _resident((1, LANES)), _resident((LANES, LANES)),
        ],
        out_specs=[
            pl.BlockSpec((None, n_chunks, LANES), lambda bb: (bb, 0, 0)),
            pl.BlockSpec((None, LANES, n_chunks), lambda bb: (bb, 0, 0)),
        ],
        out_shape=[
            jax.ShapeDtypeStruct((b, n_chunks, LANES), BF16),
            jax.ShapeDtypeStruct((b, LANES, n_chunks), BF16),
        ],
        compiler_params=_cparams(("parallel",)),
    )(cmpd, wb, prow, w2k, w2vt, kg, bd)


def _to_natural(ot_list, eye):
    pairs = []
    for k in range(0, len(ot_list), 2):
        hi, lo = _split(jnp.concatenate([ot_list[k], ot_list[k + 1]], axis=0))
        pairs.append(_nt_dot(eye, hi) + _nt_dot(eye, lo))
    return jnp.concatenate(pairs, axis=1)


def _cmp_body(qt_ref, kc_ref, vct_ref, ovt_ref, eye_ref, spread_ref, o_ref, selm_ref, cnt_ref, *, n_cmp, n_top):
    g = pl.program_id(1)
    i = pl.program_id(2)
    n_pad = kc_ref.shape[0]
    n_slc = ovt_ref.shape[0]
    kc = kc_ref[...]
    vct = vct_ref[...]
    n_idx = lax.broadcasted_iota(jnp.int32, (n_pad, Q_BLOCK), 0)
    t_idx = i * Q_BLOCK + lax.broadcasted_iota(jnp.int32, (n_pad, Q_BLOCK), 1)
    d_cmp = t_idx - (CMP_STRIDE * n_idx + CMP_BLOCK - 1)
    valid = (d_cmp >= 0) & (n_idx < n_cmp)
    d_f = d_cmp.astype(F32)
    own_rows = (lax.broadcasted_iota(jnp.int32, (LANES, Q_BLOCK), 0) // HEAD_DIM) == g
    slopes = _slopes(C_HEADS)
    psum = jnp.zeros((n_pad, Q_BLOCK), F32)
    outs = []
    for r in range(C_REP):
        slope = jnp.where(g == 0, slopes[r], slopes[C_REP + r])
        qt = qt_ref[r * HEAD_DIM:(r + 1) * HEAD_DIM, :]
        q_pad = jnp.where(own_rows, jnp.concatenate([qt, qt], axis=0), jnp.zeros((LANES, Q_BLOCK), BF16))
        s = _dot(kc, q_pad) - slope * d_f
        s = jnp.where(valid, s, NEG_INF)
        m = jnp.max(s, axis=0, keepdims=True)
        e = jnp.where(valid, jnp.exp(s - m), 0.0)
        den = jnp.sum(e, axis=0, keepdims=True)
        p = e / jnp.where(den > 0, den, 1.0)
        psum = psum + p
        both = _dot(vct, p.astype(BF16))
        outs.append(jnp.where(g == 0, both[:HEAD_DIM], both[HEAD_DIM:]))
    o_ref[...] = _to_natural(outs, eye_ref[...])
    hi, lo = _split(psum)
    ovt = ovt_ref[...]
    imp = _dot(ovt, hi) + _dot(ovt, lo)
    j_idx = lax.broadcasted_iota(jnp.int32, (n_slc, Q_BLOCK), 0)
    t_q = i * Q_BLOCK + lax.broadcasted_iota(jnp.int32, (n_slc, Q_BLOCK), 1)
    cur = lax.shift_right_logical(t_q, int(math.log2(SLC_BLOCK)))
    forced = (j_idx == 0) | (j_idx == cur) | (j_idx == cur - 1)
    v = jnp.where(j_idx <= cur, jnp.where(forced, jnp.inf, imp), -1.0)
    sel = jnp.zeros((n_slc, Q_BLOCK), F32)
    for _ in range(n_top):
        m = jnp.max(v, axis=0, keepdims=True)
        first = jnp.min(jnp.where((v == m) & (m >= 0.0), j_idx, n_slc), axis=0, keepdims=True)
        pick = j_idx == first
        sel = jnp.where(pick, 1.0, sel)
        v = jnp.where(pick, -1.0, v)
    neg = jnp.where(sel > 0, 0.0, -MASK_BIG).astype(BF16)
    selm_ref[...] = _dot(spread_ref[...], neg)
    cnt_ref[...] = _nt_dot(jnp.ones((8, Q_BLOCK), BF16), sel.astype(BF16))


def _cmp_select(qt, kc, vct, ovt, *, n_cmp):
    b, _, s = qt.shape
    g = C_KV_HEADS
    n_pad = kc.shape[1]
    n_slc = ovt.shape[0]
    nq = s // Q_BLOCK
    rows = C_REP * HEAD_DIM
    eye = jnp.asarray(np.eye(LANES), BF16)
    return pl.pallas_call(
        functools.partial(_cmp_body, n_cmp=n_cmp, n_top=min(SLC_TOPK, n_slc)),
        grid=(b, g, nq),
        in_specs=[
            pl.BlockSpec((None, rows, Q_BLOCK), lambda bb, gg, i: (bb, gg, i)),
            pl.BlockSpec((None, n_pad, LANES), lambda bb, gg, i: (bb, 0, 0)),
            pl.BlockSpec((None, LANES, n_pad), lambda bb, gg, i: (bb, 0, 0)),
            pl.BlockSpec((n_slc, n_pad), lambda bb, gg, i: (0, 0)),
            pl.BlockSpec((LANES, LANES), lambda bb, gg, i: (0, 0)),
            pl.BlockSpec((nq * AUG_ROWS, n_slc), lambda bb, gg, i: (0, 0)),
        ],
        out_specs=[
            pl.BlockSpec((Q_BLOCK, rows), lambda bb, gg, i: (bb * nq + i, gg)),
            pl.BlockSpec((None, None, None, nq * AUG_ROWS, Q_BLOCK), lambda bb, gg, i: (bb, gg, i, 0, 0)),
            pl.BlockSpec((None, None, None, 8, n_slc), lambda bb, gg, i: (bb, gg, i, 0, 0)),
        ],
        out_shape=[
            jax.ShapeDtypeStruct((b * s, g * rows), F32),
            jax.ShapeDtypeStruct((b, g, nq, nq * AUG_ROWS, Q_BLOCK), F32),
            jax.ShapeDtypeStruct((b, g, nq, 8, n_slc), F32),
        ],
        compiler_params=_cparams(("parallel", "parallel", "parallel")),
    )(qt, kc, vct, ovt, eye, jnp.asarray(_block_spread(nq), BF16))


def _slc_body(list_ref, qt_ref, ks_ref, vst_ref, selm_ref, eye_ref, o_ref, qaug, m_sc, l_sc, acc_sc, *, nq, stride):
    bb = pl.program_id(0)
    g = pl.program_id(1)
    i = pl.program_id(2)
    width = C_REP * Q_BLOCK
    slopes = _slopes(C_HEADS)
    slope_s = [jnp.where(g == 0, slopes[r], slopes[C_REP + r]) for r in range(C_REP)]

    own_rows = (lax.broadcasted_iota(jnp.int32, (LANES, width), 0) // HEAD_DIM) == g
    q6 = jnp.concatenate([qt_ref[r * HEAD_DIM:(r + 1) * HEAD_DIM, :] for r in range(C_REP)], axis=1)
    qaug[0:LANES, :] = jnp.where(own_rows, jnp.concatenate([q6, q6], axis=0), jnp.zeros((LANES, width), BF16))
    head = lax.broadcasted_iota(jnp.int32, (LANES, width), 1) // Q_BLOCK
    row = lax.broadcasted_iota(jnp.int32, (LANES, width), 0)
    slope_t = jnp.zeros((LANES, width), F32)
    for r in range(C_REP):
        slope_t = jnp.where(head == r, slope_s[r], slope_t)
    s_hi, s_lo = _split(slope_t)
    slope_rows = jnp.where(row == AUG_POS, s_hi.astype(F32), jnp.where(row == AUG_POS + 1, s_lo.astype(F32), 0.0))
    qaug[LANES:, :] = slope_rows.astype(BF16)

    q_loc = lax.broadcasted_iota(jnp.int32, (Q_BLOCK, Q_BLOCK), 1)
    k_loc = lax.broadcasted_iota(jnp.int32, (Q_BLOCK, Q_BLOCK), 0)
    causal = jnp.where(k_loc > q_loc, -MASK_BIG, 0.0)

    m_sc[...] = jnp.full(m_sc.shape, NEG_INF, F32)
    l_sc[...] = jnp.zeros(l_sc.shape, F32)
    acc_sc[...] = jnp.zeros(acc_sc.shape, F32)

    def scores(jj):
        k_tile = ks_ref[pl.ds(pl.multiple_of(jj * Q_BLOCK, Q_BLOCK), Q_BLOCK), :]
        rows = selm_ref[pl.ds(pl.multiple_of(jj * AUG_ROWS, AUG_ROWS), AUG_ROWS), :]
        mask = jnp.concatenate([jnp.broadcast_to(rows[e:e + 1], (SLC_BLOCK, Q_BLOCK)) for e in range(2)], axis=0)
        return _dot(k_tile, qaug[...]), mask

    def accumulate(tiles):
        ps = [[] for _ in tiles]
        alphas = []
        for r in range(C_REP):
            sl = slice(r * Q_BLOCK, (r + 1) * Q_BLOCK)
            m_old = m_sc[:, sl]
            m_new = m_old
            parts = []
            for jj, st, mask, extra in tiles:
                s = st[:, sl] + mask
                c = slope_s[r] * ((jj - i) * Q_BLOCK).astype(F32) + extra
                m_new = jnp.maximum(m_new, jnp.max(s, axis=0, keepdims=True) + c)
                parts.append((s, c))
            alpha = jnp.exp(m_old - m_new)
            l_new = alpha * l_sc[:, sl]
            for k, (s, c) in enumerate(parts):
                p = jnp.exp(s + (c - m_new))
                l_new = l_new + jnp.sum(p, axis=0, keepdims=True)
                ps[k].append(p.astype(BF16))
            l_sc[:, sl] = l_new
            m_sc[:, sl] = m_new
            alphas.append(alpha)
        pv = None
        for k, (jj, _, _, _) in enumerate(tiles):
            term = _dot(vst_ref[jj], jnp.concatenate(ps[k], axis=1))
            pv = term if pv is None else pv + term
        acc_sc[...] = jnp.concatenate(alphas, axis=1) * acc_sc[...] + pv

    st, mask = scores(i)
    accumulate([(i, st, mask + causal, 0.0)])

    base = ((bb * pl.num_programs(1) + g) * nq + i) * stride
    count = list_ref[base]

    def step(k, carry):
        tiles = []
        for u in range(SLC_GROUP):
            jj = list_ref[base + 1 + SLC_GROUP * k + u]
            pad = jnp.where(SLC_GROUP * k + u < count, 0.0, -MASK_BIG)
            tiles.append((jj, *scores(jj), pad))
        accumulate(tiles)
        return carry

    lax.fori_loop(0, (count + SLC_GROUP - 1) // SLC_GROUP, step, 0)
    l = l_sc[...]
    o = acc_sc[...] / jnp.where(l > 0, l, 1.0)
    o_ref[...] = _to_natural([o[:, r * Q_BLOCK:(r + 1) * Q_BLOCK] for r in range(C_REP)], eye_ref[...])


def _slc_attention(lists, qt, ks, vst, sel, *, stride):
    b, _, s = qt.shape
    g = C_KV_HEADS
    nq = s // Q_BLOCK
    rows = C_REP * HEAD_DIM
    width = C_REP * Q_BLOCK
    eye = jnp.asarray(np.eye(LANES), BF16)
    grid_spec = pltpu.PrefetchScalarGridSpec(
        num_scalar_prefetch=1,
        grid=(b, g, nq),
        in_specs=[
            pl.BlockSpec((None, rows, Q_BLOCK), lambda bb, gg, i, bits: (bb, gg, i)),
            pl.BlockSpec((None, s, 2 * LANES), lambda bb, gg, i, bits: (bb, 0, 0)),
            pl.BlockSpec((None, nq, None, HEAD_DIM, Q_BLOCK), lambda bb, gg, i, bits: (bb, 0, gg, 0, 0)),
            pl.BlockSpec((None, None, None, nq * AUG_ROWS, Q_BLOCK), lambda bb, gg, i, bits: (bb, gg, i, 0, 0)),
            pl.BlockSpec((LANES, LANES), lambda bb, gg, i, bits: (0, 0)),
        ],
        out_specs=pl.BlockSpec((Q_BLOCK, rows), lambda bb, gg, i, bits: (bb * nq + i, gg)),
        scratch_shapes=[
            pltpu.VMEM((2 * LANES, width), BF16),
            pltpu.VMEM((1, width), F32),
            pltpu.VMEM((1, width), F32),
            pltpu.VMEM((HEAD_DIM, width), F32),
        ],
    )
    return pl.pallas_call(
        functools.partial(_slc_body, nq=nq, stride=stride),
        grid_spec=grid_spec,
        out_shape=jax.ShapeDtypeStruct((b * s, g * rows), F32),
        compiler_params=_cparams(("parallel", "parallel", "parallel")),
    )(lists, qt, ks, vst, sel, eye)


def _merge_body(x_ref, g_ref, oa0, la0, oa1, la1, oa2, la2, ob_ref, ocmp_ref, oslc_ref, owin_ref, cg_ref,
                p4t_ref, p16t_ref, ex_ref, wg0_ref, wg1_ref, wg2_ref, wa_ref, wb_ref, wc_ref,
                out_ref, h_ref, oall_ref):
    @pl.when(pl.program_id(1) == 0)
    def _():
        h_ref[...] = _rms_rows(x_ref[...], g_ref[...])

        def natural(ref, pt_ref):
            hi, lo = _split(ref[...].reshape(TM, A_OUT))
            return _dot(pt_ref[...], hi) + _dot(pt_ref[...], lo)

        o0, l0 = oa0[...], la0[...]
        o1, l1 = natural(oa1, p4t_ref), natural(la1, p4t_ref)
        o2, l2 = natural(oa2, p16t_ref), natural(la2, p16t_ref)
        mx = jnp.maximum(jnp.maximum(l0, l1), l2)
        e0, e1, e2 = jnp.exp(l0 - mx), jnp.exp(l1 - mx), jnp.exp(l2 - mx)
        oall_ref[:, 0:A_OUT] = ((e0 * o0 + e1 * o1 + e2 * o2) / (e0 + e1 + e2)).astype(BF16)
        oall_ref[:, A_OUT:A_OUT + B_HEADS * HEAD_DIM] = ob_ref[...].astype(BF16)
        cg_hi, cg_lo = _split(cg_ref[...])
        o_c = None
        for w, o_ref in enumerate((ocmp_ref, oslc_ref, owin_ref)):
            term = (_dot(cg_hi, ex_ref[w]) + _dot(cg_lo, ex_ref[w])) * o_ref[...]
            o_c = term if o_c is None else o_c + term
        oall_ref[:, A_OUT + B_HEADS * HEAD_DIM:] = o_c.astype(BF16)

    h = h_ref[...]
    c0, c1 = A_OUT, A_OUT + B_HEADS * HEAD_DIM
    merged = jax.nn.sigmoid(_dot(h, wg0_ref[...])) * _dot(oall_ref[:, 0:c0], wa_ref[...])
    merged += jax.nn.sigmoid(_dot(h, wg1_ref[...])) * _dot(oall_ref[:, c0:c1], wb_ref[...])
    merged += jax.nn.sigmoid(_dot(h, wg2_ref[...])) * _dot(oall_ref[:, c1:], wc_ref[...])
    out_ref[...] = merged.astype(BF16)


def _merge(x, g, a_outs, ob, ocmp, oslc, owin, cg, w_gate, wa, wb, wc, ex, *, tn=512):
    t, d = x.shape
    per16 = CHUNK16 // TM
    n_t = d // tn
    (oa0, la0), (oa1, la1), (oa2, la2) = a_outs

    def rows(a):
        return pl.BlockSpec((TM, a.shape[1]), lambda i, n: (i, 0))

    a1_spec = pl.BlockSpec((None, 4, Q_BLOCK, A_OUT), lambda i, n: (i, 0, 0, 0))
    a2_spec = pl.BlockSpec((None, 16, TM // 16, A_OUT), lambda i, n: (i // per16, 0, i % per16, 0))
    p4t = jnp.asarray(_deinterleave(TM, 4).T, BF16)
    p16t = jnp.asarray(_deinterleave(TM, 16).T, BF16)
    in_specs = [
        rows(x), _resident((1, d)),
        rows(oa0), rows(la0), a1_spec, a1_spec, a2_spec, a2_spec,
        rows(ob), rows(ocmp), rows(oslc), rows(owin), rows(cg),
        _resident((TM, TM)), _resident((TM, TM)), _resident(ex.shape),
        pl.BlockSpec((d, tn), lambda i, n: (0, n)),
        pl.BlockSpec((d, tn), lambda i, n: (0, n + n_t)),
        pl.BlockSpec((d, tn), lambda i, n: (0, n + 2 * n_t)),
        pl.BlockSpec((wa.shape[0], tn), lambda i, n: (0, n)),
        pl.BlockSpec((wb.shape[0], tn), lambda i, n: (0, n)),
        pl.BlockSpec((wc.shape[0], tn), lambda i, n: (0, n)),
    ]
    return pl.pallas_call(
        _merge_body,
        grid=(t // TM, n_t),
        in_specs=in_specs,
        out_specs=pl.BlockSpec((TM, tn), lambda i, n: (i, n)),
        out_shape=jax.ShapeDtypeStruct((t, d), BF16),
        scratch_shapes=[pltpu.VMEM((TM, d), BF16), pltpu.VMEM((TM, wa.shape[0] + wb.shape[0] + wc.shape[0]), BF16)],
        compiler_params=_cparams(("parallel", "arbitrary")),
    )(x, g.reshape(1, d), oa0, la0, oa1, la1, oa2, la2, ob, ocmp, oslc, owin, cg, p4t, p16t, ex,
      w_gate, w_gate, w_gate, wa, wb, wc)


def _out_body(x_ref, m_ref, w_ref, o_ref):
    o_ref[...] = x_ref[...] + _dot(m_ref[...], w_ref[...])


def _out_proj(x, merged, w_out):
    t, d = x.shape
    return pl.pallas_call(
        _out_body,
        grid=(t // TM,),
        in_specs=[pl.BlockSpec((TM, d), lambda i: (i, 0)), pl.BlockSpec((TM, d), lambda i: (i, 0)),
                  _resident((d, d))],
        out_specs=pl.BlockSpec((TM, d), lambda i: (i, 0)),
        out_shape=jax.ShapeDtypeStruct((t, d), F32),
        compiler_params=_cparams(("parallel",)),
    )(x, merged, w_out)


def _qkv_column_params(qk_gain):
    flag, gain, scale = [], [], []
    one = jnp.ones((HEAD_DIM,), F32)

    def add(n_heads, normed, is_q, gvec):
        for _ in range(n_heads):
            flag.append(np.full((HEAD_DIM,), 1.0 if normed else 0.0, np.float32))
            gain.append(gvec if normed else one)
            scale.append(np.full((HEAD_DIM,), HEAD_DIM ** -0.5 if is_q else 1.0, np.float32))

    for _ in range(len(A_GROUPS)):
        add(A_HEADS_PER_GROUP, True, True, qk_gain[0, 0])
        add(A_HEADS_PER_GROUP, True, False, qk_gain[0, 1])
        add(A_HEADS_PER_GROUP, False, False, one)
    add(B_HEADS, True, True, qk_gain[1, 0])
    add(B_KV_HEADS, True, False, qk_gain[1, 1])
    add(B_KV_HEADS, False, False, one)
    add(C_HEADS, True, True, qk_gain[2, 0])
    for normed in (False, False, True, False, True, False):
        add(C_KV_HEADS, normed, False, qk_gain[2, 1])
    flag = np.concatenate(flag)
    assert flag.shape[0] == QKV_COLS
    return jnp.asarray(flag), jnp.concatenate(gain) * jnp.asarray(np.concatenate(scale))


def _overlap_t(n_slc, n_pad, n_cmp):
    n = np.arange(n_pad)[None, :]
    j = np.arange(n_slc)[:, None]
    start, end = CMP_STRIDE * n, CMP_STRIDE * n + CMP_BLOCK - 1
    ov = (start <= SLC_BLOCK * j + SLC_BLOCK - 1) & (end >= SLC_BLOCK * j) & (n < n_cmp)
    return jnp.asarray(ov, BF16)


def _gate_expand():
    ex = np.zeros((3, LANES, C_HEADS * HEAD_DIM), np.float32)
    for w in range(3):
        for h in range(C_HEADS):
            ex[w, h * 3 + w, h * HEAD_DIM:(h + 1) * HEAD_DIM] = 1.0
    return jnp.asarray(ex, BF16)


def _compress_weights(cmp_pos, cmp_w1, cmp_w2):
    n_q = 2 * C_KV_HEADS
    w1 = cmp_w1.reshape(2, 2, CMP_STRIDE, HEAD_DIM, CMP_HIDDEN)
    w1q = jnp.repeat(w1, C_KV_HEADS, axis=0)
    wb = jnp.einsum("qhcdn,qp->hcqdpn", w1q, jnp.eye(n_q, dtype=F32))
    wb = wb.reshape(2, CMP_STRIDE, n_q * HEAD_DIM, n_q * CMP_HIDDEN).astype(BF16)
    pos = cmp_pos.reshape(2, 2, CMP_STRIDE, HEAD_DIM)
    prow = jnp.repeat(pos, C_KV_HEADS, axis=0).transpose(1, 2, 0, 3).reshape(2, CMP_STRIDE, 1, n_q * HEAD_DIM)
    prow = jnp.broadcast_to(prow, (2, CMP_STRIDE, 8, n_q * HEAD_DIM)).astype(BF16)
    eye_g = jnp.eye(C_KV_HEADS, dtype=F32)
    w2k = jnp.kron(eye_g, cmp_w2[0]).astype(BF16)
    w2vt = jnp.kron(eye_g, cmp_w2[1]).T.astype(BF16)
    return wb, prow, w2k, w2vt


def _token_mixing(x, b, s, mix_norm, w_in, qk_gain, sinks, cmp_pos, cmp_w1, cmp_w2, w_a, w_b, w_c, w_out):
    t, d = x.shape
    assert s % CHUNK16 == 0 and d % 512 == 0
    c_gate_cols = 3 * C_HEADS
    w_qkv = w_in[:, :QKV_COLS + LANES].astype(BF16)
    flag, gs = _qkv_column_params(qk_gain)
    a0, a1, a2, bsec, cq, ckv, cmpd, qt, ks, vst, cg = _qkv_proj(x, mix_norm, w_qkv, flag, gs, b, s)

    a_outs = [_dilated_group(a0.reshape(b, s, SEC), 0, b, s), _dilated_group(a1, 1, b, s),
              _dilated_group(a2, 2, b, s)]
    a_outs[0] = tuple(v.reshape(t, A_OUT) for v in a_outs[0])
    o_b = _sink_swa(bsec.reshape(b, s, SEC), sinks.astype(F32), b, s).reshape(t, -1)
    o_win = _nsa_window(cq.reshape(b, s, SEC), ckv.reshape(b, s, SEC), b, s).reshape(t, -1)

    n_chunks = s // CMP_STRIDE
    n_cmp = (s - CMP_BLOCK) // CMP_STRIDE + 1
    n_slc = s // SLC_BLOCK
    nq = s // Q_BLOCK
    kg = jnp.tile(qk_gain[2, 1], C_KV_HEADS).reshape(1, LANES)
    kc, vct = _compress(cmpd, *_compress_weights(cmp_pos, cmp_w1, cmp_w2), kg)
    o_cmp, sel, cnt = _cmp_select(qt, kc, vct, _overlap_t(n_slc, n_chunks, n_cmp), n_cmp=n_cmp)

    act = (cnt[:, :, :, 0, :] > 0).reshape(b, C_KV_HEADS, nq, nq, 2).any(axis=-1)
    act = act & (jnp.arange(nq)[None, :] < jnp.arange(nq)[:, None])
    order = jnp.argsort(jnp.logical_not(act), axis=-1, stable=True).astype(jnp.int32)
    count = jnp.sum(act, axis=-1, dtype=jnp.int32)[..., None]
    lists = jnp.concatenate([count, order] + [jnp.zeros_like(count)] * (SLC_GROUP - 1), axis=-1)
    o_slc = _slc_attention(lists.reshape(-1), qt, ks.reshape(b, s, 2 * LANES),
                           vst.reshape(b, nq, C_KV_HEADS, HEAD_DIM, Q_BLOCK), sel, stride=nq + SLC_GROUP)

    merged = _merge(x, mix_norm, a_outs, o_b, o_cmp, o_slc, o_win, cg,
                    w_in[:, QKV_COLS + c_gate_cols:].astype(BF16), w_a.astype(BF16), w_b.astype(BF16),
                    w_c.astype(BF16), _gate_expand())
    return _out_proj(x, merged, w_out.astype(BF16))


def kernel(x, ffn1_norm, ffn1_w_gu, ffn1_w_down, mix_norm, w_in, qk_gain, sinks, cmp_pos, cmp_w1, cmp_w2,
           w_branch_a, w_branch_b, w_branch_c, w_out, ffn2_norm, ffn2_w_gu, ffn2_w_down):
    b, s, d = x.shape
    h = x.reshape(b * s, d)
    w1_gu, w1_down = ffn1_w_gu.astype(BF16), ffn1_w_down.astype(BF16)
    w2_gu, w2_down = ffn2_w_gu.astype(BF16), ffn2_w_down.astype(BF16)
    for l in range(ffn1_norm.shape[0]):
        h = _ffn(h, ffn1_norm[l], w1_gu, w1_down, l)
        h = _token_mixing(h, b, s, mix_norm[l], w_in[l], qk_gain[l], sinks[l], cmp_pos[l], cmp_w1[l], cmp_w2[l],
                          w_branch_a[l], w_branch_b[l], w_branch_c[l], w_out[l])
        h = _ffn(h, ffn2_norm[l], w2_gu, w2_down, l)
    return h.reshape(b, s, d)
```

```python
import functools
import math

import numpy as np
import jax
import jax.numpy as jnp
from jax import lax
from jax.experimental import pallas as pl
from jax.experimental.pallas import tpu as pltpu

F32 = jnp.float32
BF16 = jnp.bfloat16

HEAD_DIM = 64
Q_BLOCK = 128
LANES = 128
A_GROUPS = ((128, 1), (512, 4), (2048, 16))
A_HEADS_PER_GROUP = 4
A_HEADS = 12
A_OUT = A_HEADS_PER_GROUP * HEAD_DIM
B_HEADS = 8
B_KV_HEADS = 2
B_WINDOW = 128
C_HEADS = 12
C_KV_HEADS = 2
C_REP = C_HEADS // C_KV_HEADS
CMP_BLOCK = 32
CMP_STRIDE = 16
CMP_HIDDEN = 256
SLC_BLOCK = 64
SLC_TOPK = 16
C_WINDOW = 512
RMS_EPS = 1e-6
NEG_INF = -1e30
SEC = 768
N_SEC = 6
QKV_COLS = SEC * N_SEC
TM = 512
CHUNK16 = Q_BLOCK * 16
VMEM_LIMIT = 56 * 1024 * 1024


def _slopes(n):
    return [float(2.0 ** (-8.0 * (h + 1) / n)) for h in range(n)]


def _cparams(sem):
    return pltpu.CompilerParams(dimension_semantics=sem, vmem_limit_bytes=VMEM_LIMIT)


def _dot(a, b):
    return jnp.dot(a, b, preferred_element_type=F32)


def _nt_dot(a, b):
    return lax.dot_general(a, b, (((1,), (1,)), ((), ())), preferred_element_type=F32)


def _split(v):
    hi = v.astype(BF16)
    return hi, (v - hi.astype(F32)).astype(BF16)


def _resident(shape):
    return pl.BlockSpec(shape, lambda *_: (0,) * len(shape), pipeline_mode=pl.Buffered(1))


def _rms_rows(x, g):
    ms = jnp.mean(x * x, axis=-1, keepdims=True)
    return (x * lax.rsqrt(ms + RMS_EPS) * g).astype(BF16)


def _deinterleave(n, d):
    p = np.zeros((n, n), np.float32)
    r = np.arange(n // d)
    for c in range(d):
        p[c * (n // d) + r, d * r + c] = 1.0
    return p


AUG_POS = 0
AUG_ROWS = 8
MASK_BIG = 1e30
SLC_GROUP = 3


def _key_pattern(n):
    pat = np.zeros((n, LANES), np.float32)
    pat[:, AUG_POS] = pat[:, AUG_POS + 1] = np.arange(n) % Q_BLOCK
    return pat


def _block_spread(n_tiles):
    m = np.zeros((n_tiles * AUG_ROWS, 2 * n_tiles), np.float32)
    jj = np.arange(n_tiles)
    for e in range(2):
        m[AUG_ROWS * jj + e, 2 * jj + e] = 1.0
    return m


def _ffn_body(*refs, n_f, pre):
    if pre:
        x_ref, m_ref, wo_ref, g_ref, wg_ref, wu_ref, wd_ref, o_ref, h_ref, res_ref = refs
    else:
        x_ref, g_ref, wg_ref, wu_ref, wd_ref, o_ref, h_ref = refs
        res_ref = x_ref
    f = pl.program_id(1)

    @pl.when(f == 0)
    def _():
        if pre:
            res_ref[...] = x_ref[...] + _dot(m_ref[...], wo_ref[...])
        h_ref[...] = _rms_rows(res_ref[...], g_ref[...])
        o_ref[...] = jnp.zeros_like(o_ref)

    h = h_ref[...]
    gate = _dot(h, wg_ref[...])
    up = _dot(h, wu_ref[...])
    act = (gate * jax.nn.sigmoid(gate) * up).astype(BF16)
    o_ref[...] += _dot(act, wd_ref[...])

    @pl.when(f == n_f - 1)
    def _():
        o_ref[...] = res_ref[...] + 0.5 * o_ref[...]


def _ffn(x, g, w_gu, w_down, layer, *, merged=None, w_out=None, tm=512, tf=512):
    t, d = x.shape
    d_ff = w_down.shape[1]
    n_f = d_ff // tf
    pre = merged is not None
    rows = pl.BlockSpec((tm, d), lambda i, f: (i, 0))
    in_specs = [rows] + ([rows, _resident((d, d))] if pre else []) + [
        pl.BlockSpec((1, d), lambda i, f: (0, 0)),
        pl.BlockSpec((None, d, tf), lambda i, f: (layer, 0, f)),
        pl.BlockSpec((None, d, tf), lambda i, f: (layer, 0, f + n_f)),
        pl.BlockSpec((None, tf, d), lambda i, f: (layer, f, 0)),
    ]
    args = [x] + ([merged, w_out] if pre else []) + [g.reshape(1, d), w_gu, w_gu, w_down]
    return pl.pallas_call(
        functools.partial(_ffn_body, n_f=n_f, pre=pre),
        grid=(t // tm, n_f),
        in_specs=in_specs,
        out_specs=rows,
        out_shape=jax.ShapeDtypeStruct((t, d), F32),
        scratch_shapes=[pltpu.VMEM((tm, d), BF16)] + ([pltpu.VMEM((tm, d), F32)] if pre else []),
        compiler_params=_cparams(("parallel", "arbitrary")),
    )(*args)


NORM_TILE = 256
NORM_TILES = {0: (0, 1), 1: (0, 1), 2: (0, 1), 3: (0, 1, 2), 4: (0, 1, 2), 5: (1, 2)}


def _head_sumsq(y, bd):
    return _dot((y * y).astype(BF16), bd)


def _qkv_body(x_ref, g_ref, w_ref, flag_ref, gs_ref, bd_ref, p4_ref, p16_ref, eye_ref, kpat_ref,
              a0_ref, a1_ref, a2_ref, b_ref, cq_ref, ckv_ref, cmpd_ref, qt_ref, ks_ref, vst_ref, cg_ref):
    h = _rms_rows(x_ref[...], g_ref[...])
    bd = bd_ref[...]

    def section(k):
        sl = slice(k * SEC, (k + 1) * SEC)
        if k < len(A_GROUPS):
            y = jnp.concatenate([_dot(h, w_ref[:, part * A_HEADS * HEAD_DIM + k * A_OUT:
                                                  part * A_HEADS * HEAD_DIM + (k + 1) * A_OUT])
                                 for part in range(3)], axis=1)
        else:
            y = _dot(h, w_ref[:, sl])
        tiles = []
        for c in range(SEC // NORM_TILE):
            yc = y[:, c * NORM_TILE:(c + 1) * NORM_TILE]
            if c in NORM_TILES[k]:
                cols = slice(k * SEC + c * NORM_TILE, k * SEC + (c + 1) * NORM_TILE)
                inv = lax.rsqrt(_head_sumsq(yc, bd) * (1.0 / HEAD_DIM) + RMS_EPS)
                yc = yc * jnp.where(flag_ref[:, cols] > 0, inv, 1.0) * gs_ref[:, cols]
            tiles.append(yc.astype(BF16))
        return jnp.concatenate(tiles, axis=1)

    a0_ref[...] = section(0)
    a1_ref[...] = _dot(p4_ref[...], section(1)).astype(BF16).reshape(a1_ref.shape)
    a2_ref[...] = _dot(p16_ref[...], section(2)).astype(BF16).reshape(a2_ref.shape)
    b_ref[...] = section(3)
    y_cq = section(4)
    cq_ref[...] = y_cq
    qt_ref[...] = _nt_dot(eye_ref[...], y_cq).astype(BF16)
    y_ckv = section(5)
    ckv_ref[...] = y_ckv
    cmpd_ref[...] = _dot(p16_ref[...], y_ckv[:, 0:2 * LANES]).astype(BF16).reshape(cmpd_ref.shape)
    ks_ref[:, 0:LANES] = y_ckv[:, 2 * LANES:3 * LANES]
    ks_ref[:, LANES:2 * LANES] = kpat_ref[...]
    eye = eye_ref[0:LANES, 0:LANES]
    for kb in range(vst_ref.shape[0]):
        vt = _nt_dot(eye, y_ckv[kb * Q_BLOCK:(kb + 1) * Q_BLOCK, 3 * LANES:4 * LANES]).astype(BF16)
        for gg in range(C_KV_HEADS):
            vst_ref[kb, gg] = vt[gg * HEAD_DIM:(gg + 1) * HEAD_DIM]
    cg_ref[...] = jax.nn.sigmoid(_dot(h, w_ref[:, QKV_COLS:QKV_COLS + LANES]))


def _qkv_proj(x, g, w, flag, gs, b, s):
    t, d = x.shape
    tiles_per_batch = s // TM
    per16 = CHUNK16 // TM
    bd = jnp.asarray(np.kron(np.eye(NORM_TILE // HEAD_DIM), np.ones((HEAD_DIM, HEAD_DIM))), BF16)
    p4 = jnp.asarray(_deinterleave(TM, 4), BF16)
    p16 = jnp.asarray(_deinterleave(TM, 16), BF16)
    eye = jnp.asarray(np.eye(SEC), BF16)
    nat = pl.BlockSpec((TM, SEC), lambda i: (i, 0))
    out_specs = [
        nat,
        pl.BlockSpec((None, 4, Q_BLOCK, SEC), lambda i: (i, 0, 0, 0)),
        pl.BlockSpec((None, 16, TM // 16, SEC), lambda i: (i // per16, 0, i % per16, 0)),
        nat, nat, nat,
        pl.BlockSpec((None, 16, TM // 16, 2 * LANES), lambda i: (i // tiles_per_batch, 0, i % tiles_per_batch, 0)),
        pl.BlockSpec((None, SEC, TM), lambda i: (i // tiles_per_batch, 0, i % tiles_per_batch)),
        pl.BlockSpec((TM, 2 * LANES), lambda i: (i, 0)),
        pl.BlockSpec((TM // Q_BLOCK, C_KV_HEADS, HEAD_DIM, Q_BLOCK), lambda i: (i, 0, 0, 0)),
        pl.BlockSpec((TM, LANES), lambda i: (i, 0)),
    ]
    out_shape = [
        jax.ShapeDtypeStruct((t, SEC), BF16),
        jax.ShapeDtypeStruct((t // TM, 4, Q_BLOCK, SEC), BF16),
        jax.ShapeDtypeStruct((t // CHUNK16, 16, Q_BLOCK, SEC), BF16),
        jax.ShapeDtypeStruct((t, SEC), BF16),
        jax.ShapeDtypeStruct((t, SEC), BF16),
        jax.ShapeDtypeStruct((t, SEC), BF16),
        jax.ShapeDtypeStruct((b, 16, s // 16, 2 * LANES), BF16),
        jax.ShapeDtypeStruct((b, SEC, s), BF16),
        jax.ShapeDtypeStruct((t, 2 * LANES), BF16),
        jax.ShapeDtypeStruct((t // Q_BLOCK, C_KV_HEADS, HEAD_DIM, Q_BLOCK), BF16),
        jax.ShapeDtypeStruct((t, LANES), F32),
    ]
    n_w = w.shape[1]
    return pl.pallas_call(
        _qkv_body,
        grid=(t // TM,),
        in_specs=[
            pl.BlockSpec((TM, d), lambda i: (i, 0)),
            _resident((1, d)),
            _resident((d, n_w)),
            _resident((1, QKV_COLS)),
            _resident((1, QKV_COLS)),
            _resident((NORM_TILE, NORM_TILE)),
            _resident((TM, TM)),
            _resident((TM, TM)),
            _resident((SEC, SEC)),
            _resident((TM, LANES)),
        ],
        out_specs=out_specs,
        out_shape=out_shape,
        compiler_params=_cparams(("parallel",)),
    )(x, g.reshape(1, d), w, flag.reshape(1, -1), gs.reshape(1, -1), bd, p4, p16, eye,
      jnp.asarray(_key_pattern(TM), BF16))


def _banded_body(*refs, nb, heads, k_off, v_off, n_pairs, q_axis, use_sinks, with_lse, stack):
    refs = list(refs)
    q_ref = refs.pop(0)
    kv_refs = [refs.pop(0) for _ in range(nb + 1)]
    qc_ref, kaug_ref, band_ref = refs.pop(0), refs.pop(0), refs.pop(0)
    sink_ref = refs.pop(0) if use_sinks else None
    o_ref = refs.pop(0)
    lse_ref = refs.pop(0) if with_lse else None

    i = pl.program_id(q_axis)
    nk = (nb + 1) * Q_BLOCK
    col = lax.broadcasted_iota(jnp.int32, (Q_BLOCK, nk), 1)
    mask = band_ref[...] + jnp.where(col < (nb - i) * Q_BLOCK, -MASK_BIG, 0.0)
    rel_f = (nb * Q_BLOCK + lax.broadcasted_iota(jnp.int32, (Q_BLOCK, nk), 0) - col).astype(F32)
    lane = lax.broadcasted_iota(jnp.int32, (Q_BLOCK, LANES), 1)
    low_half = lane < HEAD_DIM
    kaug = kaug_ref[...]

    kv_cache = {}

    def kv_tile(off, kv_pair, swapped):
        key = (off, kv_pair, swapped)
        if key not in kv_cache:
            c0 = off + kv_pair * LANES
            tile = jnp.concatenate([r[:, c0:c0 + LANES] for r in kv_refs], axis=0)
            if swapped:
                tile = pltpu.roll(tile.astype(F32), HEAD_DIM, 1).astype(BF16)
            kv_cache[key] = tile
        return kv_cache[key]

    classes = {}
    for head in heads:
        pair, half, kv_pair, kv_half, slope, hidx = head
        key = (kv_pair, kv_half != half) if stack else hidx
        classes.setdefault(key, []).append(head)

    outs = [[None, None] for _ in range(n_pairs)]
    lses = [[None, None] for _ in range(n_pairs)]
    for members in classes.values():
        n_h = len(members)
        kv_pair, swapped = members[0][2], members[0][3] != members[0][1]
        q_rows = []
        for pair, half, _, _, _, hidx in members:
            qp = q_ref[:, pair * LANES:(pair + 1) * LANES]
            own = low_half if half == 0 else jnp.logical_not(low_half)
            qm = jnp.where(own, qp, jnp.zeros_like(qp))
            q_rows.append(jnp.concatenate([qm, qc_ref[hidx]], axis=1) if stack else qm)
        if stack:
            k_aug = jnp.concatenate([kv_tile(k_off, kv_pair, swapped), kaug], axis=1)
            s = _nt_dot(jnp.concatenate(q_rows, axis=0), k_aug)
            s = (s.reshape(n_h, Q_BLOCK, nk) + mask[None]).reshape(n_h * Q_BLOCK, nk)
        else:
            s = _nt_dot(q_rows[0], kv_tile(k_off, kv_pair, swapped)) - members[0][4] * rel_f + mask
        m = jnp.max(s, axis=1, keepdims=True)
        if use_sinks:
            assert n_h == 1
            sink = sink_ref[members[0][5]]
            m = jnp.maximum(m, sink)
        p = jnp.exp(s - m)
        den = jnp.sum(p, axis=1, keepdims=True)
        if use_sinks:
            den = den + jnp.exp(sink - m)
        r = _dot(p.astype(BF16), kv_tile(v_off, kv_pair, swapped)) / den
        lse = m + jnp.log(den) if with_lse else None
        for k, (pair, half, _, _, _, _) in enumerate(members):
            outs[pair][half] = r[k * Q_BLOCK:(k + 1) * Q_BLOCK]
            if with_lse:
                lses[pair][half] = jnp.broadcast_to(lse[k * Q_BLOCK:(k + 1) * Q_BLOCK], (Q_BLOCK, LANES))
    for pair in range(n_pairs):
        sl = slice(pair * LANES, (pair + 1) * LANES)
        o_ref[:, sl] = jnp.where(low_half, outs[pair][0], outs[pair][1])
        if with_lse:
            lse_ref[:, sl] = jnp.where(low_half, lses[pair][0], lses[pair][1])


def _banded_consts(heads, nb, max_dist):
    nk = (nb + 1) * Q_BLOCK
    slope = np.asarray([h[4] for h in heads], np.float32)[:, None]
    q_dist = (nb * Q_BLOCK + np.arange(Q_BLOCK, dtype=np.float32))[None, :]
    ones = np.ones_like(q_dist)
    vals = jnp.asarray(np.stack([slope * ones, slope * Q_BLOCK * ones, -slope * q_dist], axis=-1))
    hi = vals.astype(BF16)
    lo = (vals - hi.astype(F32)).astype(BF16)
    cols = jnp.stack([hi[..., 0], lo[..., 0], hi[..., 1], lo[..., 1], hi[..., 2], lo[..., 2]], axis=-1)
    qc = jnp.pad(cols, ((0, 0), (0, 0), (0, LANES - cols.shape[-1])))
    kaug = np.zeros((nk, LANES), np.float32)
    kaug[:, 0] = kaug[:, 1] = np.arange(nk) % Q_BLOCK
    kaug[:, 2] = kaug[:, 3] = np.arange(nk) // Q_BLOCK
    kaug[:, 4] = kaug[:, 5] = 1.0
    rel = nb * Q_BLOCK + np.arange(Q_BLOCK)[:, None] - np.arange(nk)[None, :]
    band = np.where((rel >= 0) & (rel <= max_dist), 0.0, -MASK_BIG).astype(np.float32)
    return [qc, jnp.asarray(kaug, BF16), jnp.asarray(band)]


def _banded_call(q_arr, kv_arr, *, grid, q_map, kv_map, out_map, out_lead, out_cols, nb, max_dist, heads,
                 k_off, v_off, q_axis, sinks=None, with_lse=False, stack=False):
    lead = (None,) * (q_arr.ndim - 2)
    blk = lead + (Q_BLOCK, SEC)
    in_specs = [pl.BlockSpec(blk, q_map)]
    args = [q_arr]
    for back in range(nb, -1, -1):
        in_specs.append(pl.BlockSpec(blk, functools.partial(kv_map, back=back)))
        args.append(kv_arr)
    consts = _banded_consts(heads, nb, max_dist)
    in_specs += [pl.BlockSpec(c.shape, lambda *_, nd=c.ndim: (0,) * nd) for c in consts]
    args += consts
    if sinks is not None:
        in_specs.append(pl.BlockSpec(memory_space=pltpu.SMEM))
        args.append(sinks)
    oblk = pl.BlockSpec(lead + (Q_BLOCK, out_cols), out_map)
    oshape = jax.ShapeDtypeStruct(out_lead + (out_cols,), F32)
    body = functools.partial(_banded_body, nb=nb, heads=heads, k_off=k_off, v_off=v_off,
                             n_pairs=out_cols // LANES, q_axis=q_axis, use_sinks=sinks is not None,
                             with_lse=with_lse, stack=stack)
    return pl.pallas_call(
        body,
        grid=grid,
        in_specs=in_specs,
        out_specs=[oblk, oblk] if with_lse else oblk,
        out_shape=[oshape, oshape] if with_lse else oshape,
        compiler_params=_cparams(("parallel",) * len(grid)),
    )(*args)


def _dilated_group(arr, gi, b, s):
    window, dil = A_GROUPS[gi]
    slopes = _slopes(A_HEADS)
    heads = tuple((hh // 2, hh % 2, hh // 2, hh % 2, slopes[gi * A_HEADS_PER_GROUP + hh] * dil, hh)
                  for hh in range(A_HEADS_PER_GROUP))
    common = dict(out_cols=A_OUT, nb=1, max_dist=window // dil, heads=heads, k_off=256, v_off=512,
                  with_lse=True)
    if dil == 1:
        return _banded_call(
            arr, arr, grid=(b, s // Q_BLOCK), q_axis=1,
            q_map=lambda bb, i: (bb, i, 0),
            kv_map=lambda bb, i, back: (bb, jnp.maximum(i - back, 0), 0),
            out_map=lambda bb, i: (bb, i, 0), out_lead=(b, s), **common)
    nc = s // (Q_BLOCK * dil)
    return _banded_call(
        arr, arr, grid=(b, dil, nc), q_axis=2,
        q_map=lambda bb, c, i: (bb * nc + i, c, 0, 0),
        kv_map=lambda bb, c, i, back: (bb * nc + jnp.maximum(i - back, 0), c, 0, 0),
        out_map=lambda bb, c, i: (bb * nc + i, c, 0, 0), out_lead=(b * nc, dil, Q_BLOCK), **common)


def _sink_swa(arr, sinks, b, s):
    slopes = _slopes(B_HEADS)
    rep = B_HEADS // B_KV_HEADS
    heads = tuple((h // 2, h % 2, 0, h // rep, slopes[h], h) for h in range(B_HEADS))
    return _banded_call(
        arr, arr, grid=(b, s // Q_BLOCK), q_axis=1,
        q_map=lambda bb, i: (bb, i, 0),
        kv_map=lambda bb, i, back: (bb, jnp.maximum(i - back, 0), 0),
        out_map=lambda bb, i: (bb, i, 0), out_lead=(b, s), out_cols=B_HEADS * HEAD_DIM,
        nb=1, max_dist=B_WINDOW - 1, heads=heads, k_off=512, v_off=640, sinks=sinks)


def _nsa_window(cq, ckv, b, s):
    slopes = _slopes(C_HEADS)
    heads = tuple((h // 2, h % 2, 0, h // C_REP, slopes[h], h) for h in range(C_HEADS))
    return _banded_call(
        cq, ckv, grid=(b, s // Q_BLOCK), q_axis=1,
        q_map=lambda bb, i: (bb, i, 0),
        kv_map=lambda bb, i, back: (bb, jnp.maximum(i - back, 0), 0),
        out_map=lambda bb, i: (bb, i, 0), out_lead=(b, s), out_cols=C_HEADS * HEAD_DIM,
        nb=-(-(C_WINDOW - 1) // Q_BLOCK), max_dist=C_WINDOW - 1, heads=heads, k_off=512, v_off=640, stack=True)


def _compress_body(t_ref, wb_ref, prow_ref, w2k_ref, w2vt_ref, kg_ref, bd_ref, kc_ref, vct_ref, *, n_chunks):
    hid_cols = 2 * C_KV_HEADS * CMP_HIDDEN
    u = jnp.zeros((n_chunks, hid_cols), F32)
    v = jnp.zeros((n_chunks, hid_cols), F32)
    pc = jnp.zeros((1, hid_cols), F32)
    for c in range(CMP_STRIDE):
        tc = t_ref[c]
        u = u + _dot(tc, wb_ref[0, c])
        v = v + _dot(tc, wb_ref[1, c])
        pc = pc + _dot(prow_ref[0, c], wb_ref[0, c])[0:1] + _dot(prow_ref[1, c], wb_ref[1, c])[0:1]
    hsum = u + pltpu.roll(v, n_chunks - 1, 0) + pc
    hid = (hsum * jax.nn.sigmoid(hsum)).astype(BF16)
    half = C_KV_HEADS * CMP_HIDDEN
    k = _dot(hid[:, :half], w2k_ref[...])
    hi, lo = _split(k * k)
    ss = _dot(hi, bd_ref[...]) + _dot(lo, bd_ref[...])
    kc_ref[...] = (k * lax.rsqrt(ss * (1.0 / HEAD_DIM) + RMS_EPS) * kg_ref[...]).astype(BF16)
    vct_ref[...] = _nt_dot(w2vt_ref[...], hid[:, half:]).astype(BF16)


def _compress(cmpd, wb, prow, w2k, w2vt, kg):
    b, _, n_chunks, width = cmpd.shape
    bd = jnp.asarray(np.kron(np.eye(LANES // HEAD_DIM), np.ones((HEAD_DIM, HEAD_DIM))), BF16)
    return pl.pallas_call(
        functools.partial(_compress_body, n_chunks=n_chunks),
        grid=(b,),
        in_specs=[
            pl.BlockSpec((None, CMP_STRIDE, n_chunks, width), lambda bb: (bb, 0, 0, 0)),
            _resident(wb.shape), _resident(prow.shape), _resident(w2k.shape), _resident(w2vt.shape),
            _resident((1, LANES)), _resident((LANES, LANES)),
        ],
        out_specs=[
            pl.BlockSpec((None, n_chunks, LANES), lambda bb: (bb, 0, 0)),
            pl.BlockSpec((None, LANES, n_chunks), lambda bb: (bb, 0, 0)),
        ],
        out_shape=[
            jax.ShapeDtypeStruct((b, n_chunks, LANES), BF16),
            jax.ShapeDtypeStruct((b, LANES, n_chunks), BF16),
        ],
        compiler_params=_cparams(("parallel",)),
    )(cmpd, wb, prow, w2k, w2vt, kg, bd)


def _to_natural(ot_list, eye):
    pairs = []
    for k in range(0, len(ot_list), 2):
        hi, lo = _split(jnp.concatenate([ot_list[k], ot_list[k + 1]], axis=0))
        pairs.append(_nt_dot(eye, hi) + _nt_dot(eye, lo))
    return jnp.concatenate(pairs, axis=1)


def _cmp_body(qt_ref, kc_ref, vct_ref, ovt_ref, eye_ref, spread_ref, cmask_ref,
              o_ref, selm_ref, cnt_ref, imp_sc, *, n_top):
    g = pl.program_id(1)
    i = pl.program_id(2)
    n_pad = kc_ref.shape[0]
    n_slc = ovt_ref.shape[0]
    per_q = Q_BLOCK // CMP_STRIDE
    own_rows = (lax.broadcasted_iota(jnp.int32, (LANES, Q_BLOCK), 0) // HEAD_DIM) == g
    slopes = _slopes(C_HEADS)

    def attend(rows):
        kc = kc_ref[0:rows, :]
        vct = vct_ref[:, 0:rows]
        mask = cmask_ref[pl.ds(pl.multiple_of(n_pad - per_q * i, 8), rows), :]
        n_f = (CMP_STRIDE * lax.broadcasted_iota(jnp.int32, (rows, Q_BLOCK), 0)).astype(F32)
        psum = jnp.zeros((rows, Q_BLOCK), F32)
        outs = []
        for r in range(C_REP):
            slope = jnp.where(g == 0, slopes[r], slopes[C_REP + r])
            qt = qt_ref[r * HEAD_DIM:(r + 1) * HEAD_DIM, :]
            q_pad = jnp.where(own_rows, jnp.concatenate([qt, qt], axis=0), jnp.zeros((LANES, Q_BLOCK), BF16))
            s = _dot(kc, q_pad) + (slope * n_f + mask)
            m = jnp.maximum(jnp.max(s, axis=0, keepdims=True), -1e20)
            e = jnp.exp(s - m)
            den = jnp.sum(e, axis=0, keepdims=True)
            p = e * (1.0 / jnp.where(den > 0, den, 1.0))
            psum = psum + p
            both = _dot(vct, p.astype(BF16))
            outs.append(jnp.where(g == 0, both[:HEAD_DIM], both[HEAD_DIM:]))
        o_ref[...] = _to_natural(outs, eye_ref[...])
        hi, lo = _split(psum)
        ovt = ovt_ref[:, 0:rows]
        imp_sc[...] = _dot(ovt, hi) + _dot(ovt, lo)

    n_var = n_pad // LANES
    for var in range(n_var):
        pl.when(i // (LANES // per_q) == var)(functools.partial(attend, (var + 1) * LANES))

    imp = imp_sc[...]
    j_idx = lax.broadcasted_iota(jnp.int32, (n_slc, Q_BLOCK), 0)
    t_q = i * Q_BLOCK + lax.broadcasted_iota(jnp.int32, (n_slc, Q_BLOCK), 1)
    cur = lax.shift_right_logical(t_q, int(math.log2(SLC_BLOCK)))
    forced = ((j_idx == 0) | (j_idx == cur) | (j_idx == cur - 1)) & (j_idx <= cur)
    v = jnp.where((j_idx <= cur) & jnp.logical_not(forced), imp, -1.0)
    sel = jnp.where(forced, 1.0, 0.0)
    for _ in range(n_top - 3):
        m = jnp.max(v, axis=0, keepdims=True)
        first = jnp.min(jnp.where((v == m) & (m >= 0.0), j_idx, n_slc), axis=0, keepdims=True)
        pick = j_idx == first
        sel = jnp.where(pick, 1.0, sel)
        v = jnp.where(pick, -1.0, v)
    neg = jnp.where(sel > 0, 0.0, -MASK_BIG).astype(BF16)
    selm_ref[...] = _dot(spread_ref[...], neg)
    cnt_ref[...] = _nt_dot(jnp.ones((8, Q_BLOCK), BF16), sel.astype(BF16))


def _cmp_select(qt, kc, vct, ovt, *, n_cmp):
    b, _, s = qt.shape
    g = C_KV_HEADS
    n_pad = kc.shape[1]
    n_slc = ovt.shape[0]
    nq = s // Q_BLOCK
    rows = C_REP * HEAD_DIM
    eye = jnp.asarray(np.eye(LANES), BF16)
    n_top = min(SLC_TOPK, n_slc)
    assert n_top > 3 and n_cmp == n_pad - 1 and n_pad % LANES == 0
    n_rel = np.arange(-n_pad, n_pad)[:, None]
    cmask = np.where(CMP_STRIDE * n_rel + CMP_BLOCK - 1 <= np.arange(Q_BLOCK)[None, :], 0.0, -MASK_BIG)
    return pl.pallas_call(
        functools.partial(_cmp_body, n_top=n_top),
        grid=(b, g, nq),
        in_specs=[
            pl.BlockSpec((None, rows, Q_BLOCK), lambda bb, gg, i: (bb, gg, i)),
            pl.BlockSpec((None, n_pad, LANES), lambda bb, gg, i: (bb, 0, 0)),
            pl.BlockSpec((None, LANES, n_pad), lambda bb, gg, i: (bb, 0, 0)),
            pl.BlockSpec((n_slc, n_pad), lambda bb, gg, i: (0, 0)),
            pl.BlockSpec((LANES, LANES), lambda bb, gg, i: (0, 0)),
            pl.BlockSpec((nq * AUG_ROWS, n_slc), lambda bb, gg, i: (0, 0)),
            pl.BlockSpec((2 * n_pad, Q_BLOCK), lambda bb, gg, i: (0, 0)),
        ],
        out_specs=[
            pl.BlockSpec((Q_BLOCK, rows), lambda bb, gg, i: (bb * nq + i, gg)),
            pl.BlockSpec((None, None, None, nq * AUG_ROWS, Q_BLOCK), lambda bb, gg, i: (bb, gg, i, 0, 0)),
            pl.BlockSpec((None, None, None, 8, n_slc), lambda bb, gg, i: (bb, gg, i, 0, 0)),
        ],
        out_shape=[
            jax.ShapeDtypeStruct((b * s, g * rows), F32),
            jax.ShapeDtypeStruct((b, g, nq, nq * AUG_ROWS, Q_BLOCK), F32),
            jax.ShapeDtypeStruct((b, g, nq, 8, n_slc), F32),
        ],
        scratch_shapes=[pltpu.VMEM((n_slc, Q_BLOCK), F32)],
        compiler_params=_cparams(("parallel", "parallel", "parallel")),
    )(qt, kc, vct, ovt, eye, jnp.asarray(_block_spread(nq), BF16), jnp.asarray(cmask, F32))


def _slc_body(list_ref, qt_ref, ks_ref, vst_ref, selm_ref, eye_ref, o_ref, qaug, m_sc, l_sc, acc_sc, *, nq, stride):
    bb = pl.program_id(0)
    g = pl.program_id(1)
    i = pl.program_id(2)
    width = C_REP * Q_BLOCK
    slopes = _slopes(C_HEADS)
    slope_s = [jnp.where(g == 0, slopes[r], slopes[C_REP + r]) for r in range(C_REP)]

    own_rows = (lax.broadcasted_iota(jnp.int32, (LANES, width), 0) // HEAD_DIM) == g
    q6 = jnp.concatenate([qt_ref[r * HEAD_DIM:(r + 1) * HEAD_DIM, :] for r in range(C_REP)], axis=1)
    qaug[0:LANES, :] = jnp.where(own_rows, jnp.concatenate([q6, q6], axis=0), jnp.zeros((LANES, width), BF16))
    head = lax.broadcasted_iota(jnp.int32, (LANES, width), 1) // Q_BLOCK
    row = lax.broadcasted_iota(jnp.int32, (LANES, width), 0)
    slope_t = jnp.zeros((LANES, width), F32)
    for r in range(C_REP):
        slope_t = jnp.where(head == r, slope_s[r], slope_t)
    s_hi, s_lo = _split(slope_t)
    slope_rows = jnp.where(row == AUG_POS, s_hi.astype(F32), jnp.where(row == AUG_POS + 1, s_lo.astype(F32), 0.0))
    qaug[LANES:, :] = slope_rows.astype(BF16)

    q_loc = lax.broadcasted_iota(jnp.int32, (Q_BLOCK, Q_BLOCK), 1)
    k_loc = lax.broadcasted_iota(jnp.int32, (Q_BLOCK, Q_BLOCK), 0)
    causal = jnp.where(k_loc > q_loc, -MASK_BIG, 0.0)

    m_sc[...] = jnp.full(m_sc.shape, NEG_INF, F32)
    l_sc[...] = jnp.zeros(l_sc.shape, F32)
    acc_sc[...] = jnp.zeros(acc_sc.shape, F32)

    def scores(jj):
        k_tile = ks_ref[pl.ds(pl.multiple_of(jj * Q_BLOCK, Q_BLOCK), Q_BLOCK), :]
        rows = selm_ref[pl.ds(pl.multiple_of(jj * AUG_ROWS, AUG_ROWS), AUG_ROWS), :]
        mask = jnp.concatenate([jnp.broadcast_to(rows[e:e + 1], (SLC_BLOCK, Q_BLOCK)) for e in range(2)], axis=0)
        return _dot(k_tile, qaug[...]), mask

    def accumulate(tiles):
        ps = [[] for _ in tiles]
        alphas = []
        for r in range(C_REP):
            sl = slice(r * Q_BLOCK, (r + 1) * Q_BLOCK)
            m_old = m_sc[:, sl]
            m_new = m_old
            parts = []
            for jj, st, mask, extra in tiles:
                s = st[:, sl] + mask
                c = slope_s[r] * ((jj - i) * Q_BLOCK).astype(F32) + extra
                m_new = jnp.maximum(m_new, jnp.max(s, axis=0, keepdims=True) + c)
                parts.append((s, c))
            alpha = jnp.exp(m_old - m_new)
            l_new = alpha * l_sc[:, sl]
            for k, (s, c) in enumerate(parts):
                p = jnp.exp(s + (c - m_new))
                l_new = l_new + jnp.sum(p, axis=0, keepdims=True)
                ps[k].append(p.astype(BF16))
            l_sc[:, sl] = l_new
            m_sc[:, sl] = m_new
            alphas.append(alpha)
        pv = None
        for k, (jj, _, _, _) in enumerate(tiles):
            term = _dot(vst_ref[jj], jnp.concatenate(ps[k], axis=1))
            pv = term if pv is None else pv + term
        acc_sc[...] = jnp.concatenate(alphas, axis=1) * acc_sc[...] + pv

    base = ((bb * pl.num_programs(1) + g) * nq + i) * stride
    count = list_ref[base]

    def listed(slot):
        jj = list_ref[base + 1 + slot]
        return (jj, *scores(jj), jnp.where(slot < count, 0.0, -MASK_BIG))

    st, mask = scores(i)
    accumulate([(i, st, mask + causal, 0.0)] + [listed(u) for u in range(SLC_GROUP - 1)])

    def step(k, carry):
        accumulate([listed(SLC_GROUP - 1 + SLC_GROUP * k + u) for u in range(SLC_GROUP)])
        return carry

    rest = jnp.maximum(count - (SLC_GROUP - 1), 0)
    lax.fori_loop(0, (rest + SLC_GROUP - 1) // SLC_GROUP, step, 0)
    l = l_sc[...]
    o = acc_sc[...] / jnp.where(l > 0, l, 1.0)
    o_ref[...] = _to_natural([o[:, r * Q_BLOCK:(r + 1) * Q_BLOCK] for r in range(C_REP)], eye_ref[...])


def _slc_attention(lists, qt, ks, vst, sel, *, stride):
    b, _, s = qt.shape
    g = C_KV_HEADS
    nq = s // Q_BLOCK
    rows = C_REP * HEAD_DIM
    width = C_REP * Q_BLOCK
    eye = jnp.asarray(np.eye(LANES), BF16)
    grid_spec = pltpu.PrefetchScalarGridSpec(
        num_scalar_prefetch=1,
        grid=(b, g, nq),
        in_specs=[
            pl.BlockSpec((None, rows, Q_BLOCK), lambda bb, gg, i, bits: (bb, gg, i)),
            pl.BlockSpec((None, s, 2 * LANES), lambda bb, gg, i, bits: (bb, 0, 0)),
            pl.BlockSpec((None, nq, None, HEAD_DIM, Q_BLOCK), lambda bb, gg, i, bits: (bb, 0, gg, 0, 0)),
            pl.BlockSpec((None, None, None, nq * AUG_ROWS, Q_BLOCK), lambda bb, gg, i, bits: (bb, gg, i, 0, 0)),
            pl.BlockSpec((LANES, LANES), lambda bb, gg, i, bits: (0, 0)),
        ],
        out_specs=pl.BlockSpec((Q_BLOCK, rows), lambda bb, gg, i, bits: (bb * nq + i, gg)),
        scratch_shapes=[
            pltpu.VMEM((2 * LANES, width), BF16),
            pltpu.VMEM((1, width), F32),
            pltpu.VMEM((1, width), F32),
            pltpu.VMEM((HEAD_DIM, width), F32),
        ],
    )
    return pl.pallas_call(
        functools.partial(_slc_body, nq=nq, stride=stride),
        grid_spec=grid_spec,
        out_shape=jax.ShapeDtypeStruct((b * s, g * rows), F32),
        compiler_params=_cparams(("parallel", "parallel", "parallel")),
    )(lists, qt, ks, vst, sel, eye)


def _merge_body(x_ref, g_ref, oa0, la0, oa1, la1, oa2, la2, ob_ref, ocmp_ref, oslc_ref, owin_ref, cg_ref,
                p4t_ref, p16t_ref, ex_ref, wg0_ref, wg1_ref, wg2_ref, wa_ref, wb_ref, wc_ref,
                out_ref, h_ref, oall_ref):
    @pl.when(pl.program_id(1) == 0)
    def _():
        h_ref[...] = _rms_rows(x_ref[...], g_ref[...])

        def natural(ref, pt_ref):
            hi, lo = _split(ref[...].reshape(TM, A_OUT))
            return _dot(pt_ref[...], hi) + _dot(pt_ref[...], lo)

        o0, l0 = oa0[...], la0[...]
        o1, l1 = natural(oa1, p4t_ref), natural(la1, p4t_ref)
        o2, l2 = natural(oa2, p16t_ref), natural(la2, p16t_ref)
        mx = jnp.maximum(jnp.maximum(l0, l1), l2)
        e0, e1, e2 = jnp.exp(l0 - mx), jnp.exp(l1 - mx), jnp.exp(l2 - mx)
        oall_ref[:, 0:A_OUT] = ((e0 * o0 + e1 * o1 + e2 * o2) / (e0 + e1 + e2)).astype(BF16)
        oall_ref[:, A_OUT:A_OUT + B_HEADS * HEAD_DIM] = ob_ref[...].astype(BF16)
        cg_hi, cg_lo = _split(cg_ref[...])
        o_c = None
        for w, o_ref in enumerate((ocmp_ref, oslc_ref, owin_ref)):
            term = (_dot(cg_hi, ex_ref[w]) + _dot(cg_lo, ex_ref[w])) * o_ref[...]
            o_c = term if o_c is None else o_c + term
        oall_ref[:, A_OUT + B_HEADS * HEAD_DIM:] = o_c.astype(BF16)

    h = h_ref[...]
    c0, c1 = A_OUT, A_OUT + B_HEADS * HEAD_DIM
    merged = jax.nn.sigmoid(_dot(h, wg0_ref[...])) * _dot(oall_ref[:, 0:c0], wa_ref[...])
    merged += jax.nn.sigmoid(_dot(h, wg1_ref[...])) * _dot(oall_ref[:, c0:c1], wb_ref[...])
    merged += jax.nn.sigmoid(_dot(h, wg2_ref[...])) * _dot(oall_ref[:, c1:], wc_ref[...])
    out_ref[...] = merged.astype(BF16)


def _merge(x, g, a_outs, ob, ocmp, oslc, owin, cg, w_gate, wa, wb, wc, ex, *, tn=512):
    t, d = x.shape
    per16 = CHUNK16 // TM
    n_t = d // tn
    (oa0, la0), (oa1, la1), (oa2, la2) = a_outs

    def rows(a):
        return pl.BlockSpec((TM, a.shape[1]), lambda i, n: (i, 0))

    a1_spec = pl.BlockSpec((None, 4, Q_BLOCK, A_OUT), lambda i, n: (i, 0, 0, 0))
    a2_spec = pl.BlockSpec((None, 16, TM // 16, A_OUT), lambda i, n: (i // per16, 0, i % per16, 0))
    p4t = jnp.asarray(_deinterleave(TM, 4).T, BF16)
    p16t = jnp.asarray(_deinterleave(TM, 16).T, BF16)
    in_specs = [
        rows(x), _resident((1, d)),
        rows(oa0), rows(la0), a1_spec, a1_spec, a2_spec, a2_spec,
        rows(ob), rows(ocmp), rows(oslc), rows(owin), rows(cg),
        _resident((TM, TM)), _resident((TM, TM)), _resident(ex.shape),
        pl.BlockSpec((d, tn), lambda i, n: (0, n)),
        pl.BlockSpec((d, tn), lambda i, n: (0, n + n_t)),
        pl.BlockSpec((d, tn), lambda i, n: (0, n + 2 * n_t)),
        pl.BlockSpec((wa.shape[0], tn), lambda i, n: (0, n)),
        pl.BlockSpec((wb.shape[0], tn), lambda i, n: (0, n)),
        pl.BlockSpec((wc.shape[0], tn), lambda i, n: (0, n)),
    ]
    return pl.pallas_call(
        _merge_body,
        grid=(t // TM, n_t),
        in_specs=in_specs,
        out_specs=pl.BlockSpec((TM, tn), lambda i, n: (i, n)),
        out_shape=jax.ShapeDtypeStruct((t, d), BF16),
        scratch_shapes=[pltpu.VMEM((TM, d), BF16), pltpu.VMEM((TM, wa.shape[0] + wb.shape[0] + wc.shape[0]), BF16)],
        compiler_params=_cparams(("parallel", "arbitrary")),
    )(x, g.reshape(1, d), oa0, la0, oa1, la1, oa2, la2, ob, ocmp, oslc, owin, cg, p4t, p16t, ex,
      w_gate, w_gate, w_gate, wa, wb, wc)


def _qkv_column_params(qk_gain):
    flag, gain, scale = [], [], []
    one = jnp.ones((HEAD_DIM,), F32)

    def add(n_heads, normed, is_q, gvec):
        for _ in range(n_heads):
            flag.append(np.full((HEAD_DIM,), 1.0 if normed else 0.0, np.float32))
            gain.append(gvec if normed else one)
            scale.append(np.full((HEAD_DIM,), HEAD_DIM ** -0.5 if is_q else 1.0, np.float32))

    for _ in range(len(A_GROUPS)):
        add(A_HEADS_PER_GROUP, True, True, qk_gain[0, 0])
        add(A_HEADS_PER_GROUP, True, False, qk_gain[0, 1])
        add(A_HEADS_PER_GROUP, False, False, one)
    add(B_HEADS, True, True, qk_gain[1, 0])
    add(B_KV_HEADS, True, False, qk_gain[1, 1])
    add(B_KV_HEADS, False, False, one)
    add(C_HEADS, True, True, qk_gain[2, 0])
    for normed in (False, False, True, False, True, False):
        add(C_KV_HEADS, normed, False, qk_gain[2, 1])
    flag = np.concatenate(flag)
    assert flag.shape[0] == QKV_COLS
    return jnp.asarray(flag), jnp.concatenate(gain) * jnp.asarray(np.concatenate(scale))


def _overlap_t(n_slc, n_pad, n_cmp):
    n = np.arange(n_pad)[None, :]
    j = np.arange(n_slc)[:, None]
    start, end = CMP_STRIDE * n, CMP_STRIDE * n + CMP_BLOCK - 1
    ov = (start <= SLC_BLOCK * j + SLC_BLOCK - 1) & (end >= SLC_BLOCK * j) & (n < n_cmp)
    return jnp.asarray(ov, BF16)


def _gate_expand():
    ex = np.zeros((3, LANES, C_HEADS * HEAD_DIM), np.float32)
    for w in range(3):
        for h in range(C_HEADS):
            ex[w, h * 3 + w, h * HEAD_DIM:(h + 1) * HEAD_DIM] = 1.0
    return jnp.asarray(ex, BF16)


def _compress_weights(cmp_pos, cmp_w1, cmp_w2):
    n_q = 2 * C_KV_HEADS
    w1 = cmp_w1.reshape(2, 2, CMP_STRIDE, HEAD_DIM, CMP_HIDDEN)
    w1q = jnp.repeat(w1, C_KV_HEADS, axis=0)
    wb = jnp.einsum("qhcdn,qp->hcqdpn", w1q, jnp.eye(n_q, dtype=F32))
    wb = wb.reshape(2, CMP_STRIDE, n_q * HEAD_DIM, n_q * CMP_HIDDEN).astype(BF16)
    pos = cmp_pos.reshape(2, 2, CMP_STRIDE, HEAD_DIM)
    prow = jnp.repeat(pos, C_KV_HEADS, axis=0).transpose(1, 2, 0, 3).reshape(2, CMP_STRIDE, 1, n_q * HEAD_DIM)
    prow = jnp.broadcast_to(prow, (2, CMP_STRIDE, 8, n_q * HEAD_DIM)).astype(BF16)
    eye_g = jnp.eye(C_KV_HEADS, dtype=F32)
    w2k = jnp.kron(eye_g, cmp_w2[0]).astype(BF16)
    w2vt = jnp.kron(eye_g, cmp_w2[1]).T.astype(BF16)
    return wb, prow, w2k, w2vt


def _token_mixing(x, b, s, mix_norm, w_in, qk_gain, sinks, cmp_pos, cmp_w1, cmp_w2, w_a, w_b, w_c):
    t, d = x.shape
    assert s % CHUNK16 == 0 and d % 512 == 0
    c_gate_cols = 3 * C_HEADS
    w_qkv = w_in[:, :QKV_COLS + LANES].astype(BF16)
    flag, gs = _qkv_column_params(qk_gain)
    a0, a1, a2, bsec, cq, ckv, cmpd, qt, ks, vst, cg = _qkv_proj(x, mix_norm, w_qkv, flag, gs, b, s)

    a_outs = [_dilated_group(a0.reshape(b, s, SEC), 0, b, s), _dilated_group(a1, 1, b, s),
              _dilated_group(a2, 2, b, s)]
    a_outs[0] = tuple(v.reshape(t, A_OUT) for v in a_outs[0])
    o_b = _sink_swa(bsec.reshape(b, s, SEC), sinks.astype(F32), b, s).reshape(t, -1)
    o_win = _nsa_window(cq.reshape(b, s, SEC), ckv.reshape(b, s, SEC), b, s).reshape(t, -1)

    n_chunks = s // CMP_STRIDE
    n_cmp = (s - CMP_BLOCK) // CMP_STRIDE + 1
    n_slc = s // SLC_BLOCK
    nq = s // Q_BLOCK
    kg = jnp.tile(qk_gain[2, 1], C_KV_HEADS).reshape(1, LANES)
    kc, vct = _compress(cmpd, *_compress_weights(cmp_pos, cmp_w1, cmp_w2), kg)
    o_cmp, sel, cnt = _cmp_select(qt, kc, vct, _overlap_t(n_slc, n_chunks, n_cmp), n_cmp=n_cmp)

    act = (cnt[:, :, :, 0, :] > 0).reshape(b, C_KV_HEADS, nq, nq, 2).any(axis=-1)
    act = act & (jnp.arange(nq)[None, :] < jnp.arange(nq)[:, None])
    order = jnp.argsort(jnp.logical_not(act), axis=-1, stable=True).astype(jnp.int32)
    count = jnp.sum(act, axis=-1, dtype=jnp.int32)[..., None]
    lists = jnp.concatenate([count, order] + [jnp.zeros_like(count)] * (SLC_GROUP - 1), axis=-1)
    o_slc = _slc_attention(lists.reshape(-1), qt, ks.reshape(b, s, 2 * LANES),
                           vst.reshape(b, nq, C_KV_HEADS, HEAD_DIM, Q_BLOCK), sel, stride=nq + SLC_GROUP)

    return _merge(x, mix_norm, a_outs, o_b, o_cmp, o_slc, o_win, cg,
                  w_in[:, QKV_COLS + c_gate_cols:].astype(BF16), w_a.astype(BF16), w_b.astype(BF16),
                  w_c.astype(BF16), _gate_expand())


def kernel(x, ffn1_norm, ffn1_w_gu, ffn1_w_down, mix_norm, w_in, qk_gain, sinks, cmp_pos, cmp_w1, cmp_w2,
           w_branch_a, w_branch_b, w_branch_c, w_out, ffn2_norm, ffn2_w_gu, ffn2_w_down):
    b, s, d = x.shape
    h = x.reshape(b * s, d)
    w1_gu, w1_down = ffn1_w_gu.astype(BF16), ffn1_w_down.astype(BF16)
    w2_gu, w2_down = ffn2_w_gu.astype(BF16), ffn2_w_down.astype(BF16)
    for l in range(ffn1_norm.shape[0]):
        h = _ffn(h, ffn1_norm[l], w1_gu, w1_down, l)
        merged = _token_mixing(h, b, s, mix_norm[l], w_in[l], qk_gain[l], sinks[l], cmp_pos[l], cmp_w1[l],
                               cmp_w2[l], w_branch_a[l], w_branch_b[l], w_branch_c[l])
        h = _ffn(h, ffn2_norm[l], w2_gu, w2_down, l, merged=merged, w_out=w_out[l].astype(BF16))
    return h.reshape(b, s, d)
```

```python
import functools
import math

import numpy as np
import jax
import jax.numpy as jnp
from jax import lax
from jax.experimental import pallas as pl
from jax.experimental.pallas import tpu as pltpu

F32 = jnp.float32
BF16 = jnp.bfloat16

HEAD_DIM = 64
Q_BLOCK = 128
LANES = 128
A_GROUPS = ((128, 1), (512, 4), (2048, 16))
A_HEADS_PER_GROUP = 4
A_HEADS = 12
A_OUT = A_HEADS_PER_GROUP * HEAD_DIM
B_HEADS = 8
B_KV_HEADS = 2
B_WINDOW = 128
C_HEADS = 12
C_KV_HEADS = 2
C_REP = C_HEADS // C_KV_HEADS
CMP_BLOCK = 32
CMP_STRIDE = 16
CMP_HIDDEN = 256
SLC_BLOCK = 64
SLC_TOPK = 16
C_WINDOW = 512
RMS_EPS = 1e-6
NEG_INF = -1e30
SEC = 768
N_SEC = 6
QKV_COLS = SEC * N_SEC
TM = 512
CHUNK16 = Q_BLOCK * 16
VMEM_LIMIT = 56 * 1024 * 1024


def _slopes(n):
    return [float(2.0 ** (-8.0 * (h + 1) / n)) for h in range(n)]


def _cparams(sem):
    return pltpu.CompilerParams(dimension_semantics=sem, vmem_limit_bytes=VMEM_LIMIT)


def _dot(a, b):
    return jnp.dot(a, b, preferred_element_type=F32)


def _nt_dot(a, b):
    return lax.dot_general(a, b, (((1,), (1,)), ((), ())), preferred_element_type=F32)


def _split(v):
    hi = v.astype(BF16)
    return hi, (v - hi.astype(F32)).astype(BF16)


def _resident(shape):
    return pl.BlockSpec(shape, lambda *_: (0,) * len(shape), pipeline_mode=pl.Buffered(1))


def _rms_rows(x, g):
    ms = jnp.mean(x * x, axis=-1, keepdims=True)
    return (x * lax.rsqrt(ms + RMS_EPS) * g).astype(BF16)


def _deinterleave(n, d):
    p = np.zeros((n, n), np.float32)
    r = np.arange(n // d)
    for c in range(d):
        p[c * (n // d) + r, d * r + c] = 1.0
    return p


AUG_POS = 0
AUG_ROWS = 8
MASK_BIG = 1e30
SLC_GROUP = 5


def _key_pattern(n):
    pat = np.zeros((n, LANES), np.float32)
    pat[:, AUG_POS] = pat[:, AUG_POS + 1] = np.arange(n) % Q_BLOCK
    return pat


def _block_spread(n_tiles):
    m = np.zeros((n_tiles * AUG_ROWS, 2 * n_tiles), np.float32)
    jj = np.arange(n_tiles)
    for e in range(2):
        m[AUG_ROWS * jj + e, 2 * jj + e] = 1.0
    return m


def _ffn_body(x_ref, g_ref, wg_ref, wu_ref, wd_ref, o_ref, h_ref, *, n_f):
    f = pl.program_id(1)

    @pl.when(f == 0)
    def _():
        h_ref[...] = _rms_rows(x_ref[...], g_ref[...])
        o_ref[...] = jnp.zeros_like(o_ref)

    h = h_ref[...]
    gate = _dot(h, wg_ref[...])
    up = _dot(h, wu_ref[...])
    act = (gate * jax.nn.sigmoid(gate) * up).astype(BF16)
    o_ref[...] += _dot(act, wd_ref[...])

    @pl.when(f == n_f - 1)
    def _():
        o_ref[...] = x_ref[...] + 0.5 * o_ref[...]


def _ffn(x, g, w_gu, w_down, layer, *, tm=1024, tf=512):
    t, d = x.shape
    d_ff = w_down.shape[1]
    n_f = d_ff // tf
    rows = pl.BlockSpec((tm, d), lambda i, f: (i, 0))
    return pl.pallas_call(
        functools.partial(_ffn_body, n_f=n_f),
        grid=(t // tm, n_f),
        in_specs=[
            rows,
            pl.BlockSpec((1, d), lambda i, f: (0, 0)),
            pl.BlockSpec((None, d, tf), lambda i, f: (layer, 0, f)),
            pl.BlockSpec((None, d, tf), lambda i, f: (layer, 0, f + n_f)),
            pl.BlockSpec((None, tf, d), lambda i, f: (layer, f, 0)),
        ],
        out_specs=rows,
        out_shape=jax.ShapeDtypeStruct((t, d), F32),
        scratch_shapes=[pltpu.VMEM((tm, d), BF16)],
        compiler_params=_cparams(("parallel", "arbitrary")),
    )(x, g.reshape(1, d), w_gu, w_gu, w_down)


NORM_TILE = 256
NORM_TILES = {0: (0, 1), 1: (0, 1), 2: (0, 1), 3: (0, 1, 2), 4: (0, 1, 2), 5: (1, 2)}


def _head_sumsq(y, bd):
    return _dot((y * y).astype(BF16), bd)


def _qkv_body(x_ref, g_ref, w_ref, flag_ref, gs_ref, bd_ref, p4_ref, p16_ref, eye_ref, kpat_ref,
              a0_ref, a1_ref, a2_ref, b_ref, cq_ref, ckv_ref, cmpd_ref, qt_ref, ks_ref, vst_ref, cg_ref):
    h = _rms_rows(x_ref[...], g_ref[...])
    bd = bd_ref[...]

    def section(k):
        sl = slice(k * SEC, (k + 1) * SEC)
        if k < len(A_GROUPS):
            y = jnp.concatenate([_dot(h, w_ref[:, part * A_HEADS * HEAD_DIM + k * A_OUT:
                                                  part * A_HEADS * HEAD_DIM + (k + 1) * A_OUT])
                                 for part in range(3)], axis=1)
        else:
            y = _dot(h, w_ref[:, sl])
        tiles = []
        for c in range(SEC // NORM_TILE):
            yc = y[:, c * NORM_TILE:(c + 1) * NORM_TILE]
            if c in NORM_TILES[k]:
                cols = slice(k * SEC + c * NORM_TILE, k * SEC + (c + 1) * NORM_TILE)
                inv = lax.rsqrt(_head_sumsq(yc, bd) * (1.0 / HEAD_DIM) + RMS_EPS)
                yc = yc * jnp.where(flag_ref[:, cols] > 0, inv, 1.0) * gs_ref[:, cols]
            tiles.append(yc.astype(BF16))
        return jnp.concatenate(tiles, axis=1)

    a0_ref[...] = section(0)
    a1_ref[...] = _dot(p4_ref[...], section(1)).astype(BF16).reshape(a1_ref.shape)
    a2_ref[...] = _dot(p16_ref[...], section(2)).astype(BF16).reshape(a2_ref.shape)
    b_ref[...] = section(3)
    y_cq = section(4)
    cq_ref[...] = y_cq
    qt_ref[...] = _nt_dot(eye_ref[...], y_cq).astype(BF16)
    y_ckv = section(5)
    ckv_ref[...] = y_ckv
    cmpd_ref[...] = _dot(p16_ref[...], y_ckv[:, 0:2 * LANES]).astype(BF16).reshape(cmpd_ref.shape)
    ks_ref[:, 0:LANES] = y_ckv[:, 2 * LANES:3 * LANES]
    ks_ref[:, LANES:2 * LANES] = kpat_ref[...]
    eye = eye_ref[0:LANES, 0:LANES]
    for kb in range(vst_ref.shape[0]):
        vt = _nt_dot(eye, y_ckv[kb * Q_BLOCK:(kb + 1) * Q_BLOCK, 3 * LANES:4 * LANES]).astype(BF16)
        for gg in range(C_KV_HEADS):
            vst_ref[kb, gg] = vt[gg * HEAD_DIM:(gg + 1) * HEAD_DIM]
    cg_ref[...] = jax.nn.sigmoid(_dot(h, w_ref[:, QKV_COLS:QKV_COLS + LANES]))


def _qkv_proj(x, g, w, flag, gs, b, s):
    t, d = x.shape
    tiles_per_batch = s // TM
    per16 = CHUNK16 // TM
    bd = jnp.asarray(np.kron(np.eye(NORM_TILE // HEAD_DIM), np.ones((HEAD_DIM, HEAD_DIM))), BF16)
    p4 = jnp.asarray(_deinterleave(TM, 4), BF16)
    p16 = jnp.asarray(_deinterleave(TM, 16), BF16)
    eye = jnp.asarray(np.eye(SEC), BF16)
    nat = pl.BlockSpec((TM, SEC), lambda i: (i, 0))
    out_specs = [
        nat,
        pl.BlockSpec((None, 4, Q_BLOCK, SEC), lambda i: (i, 0, 0, 0)),
        pl.BlockSpec((None, 16, TM // 16, SEC), lambda i: (i // per16, 0, i % per16, 0)),
        nat, nat, nat,
        pl.BlockSpec((None, 16, TM // 16, 2 * LANES), lambda i: (i // tiles_per_batch, 0, i % tiles_per_batch, 0)),
        pl.BlockSpec((None, SEC, TM), lambda i: (i // tiles_per_batch, 0, i % tiles_per_batch)),
        pl.BlockSpec((TM, 2 * LANES), lambda i: (i, 0)),
        pl.BlockSpec((TM // Q_BLOCK, C_KV_HEADS, HEAD_DIM, Q_BLOCK), lambda i: (i, 0, 0, 0)),
        pl.BlockSpec((TM, LANES), lambda i: (i, 0)),
    ]
    out_shape = [
        jax.ShapeDtypeStruct((t, SEC), BF16),
        jax.ShapeDtypeStruct((t // TM, 4, Q_BLOCK, SEC), BF16),
        jax.ShapeDtypeStruct((t // CHUNK16, 16, Q_BLOCK, SEC), BF16),
        jax.ShapeDtypeStruct((t, SEC), BF16),
        jax.ShapeDtypeStruct((t, SEC), BF16),
        jax.ShapeDtypeStruct((t, SEC), BF16),
        jax.ShapeDtypeStruct((b, 16, s // 16, 2 * LANES), BF16),
        jax.ShapeDtypeStruct((b, SEC, s), BF16),
        jax.ShapeDtypeStruct((t, 2 * LANES), BF16),
        jax.ShapeDtypeStruct((t // Q_BLOCK, C_KV_HEADS, HEAD_DIM, Q_BLOCK), BF16),
        jax.ShapeDtypeStruct((t, LANES), F32),
    ]
    n_w = w.shape[1]
    return pl.pallas_call(
        _qkv_body,
        grid=(t // TM,),
        in_specs=[
            pl.BlockSpec((TM, d), lambda i: (i, 0)),
            _resident((1, d)),
            _resident((d, n_w)),
            _resident((1, QKV_COLS)),
            _resident((1, QKV_COLS)),
            _resident((NORM_TILE, NORM_TILE)),
            _resident((TM, TM)),
            _resident((TM, TM)),
            _resident((SEC, SEC)),
            _resident((TM, LANES)),
        ],
        out_specs=out_specs,
        out_shape=out_shape,
        compiler_params=_cparams(("parallel",)),
    )(x, g.reshape(1, d), w, flag.reshape(1, -1), gs.reshape(1, -1), bd, p4, p16, eye,
      jnp.asarray(_key_pattern(TM), BF16))


def _banded_body(*refs, nb, heads, k_off, v_off, n_pairs, q_axis, use_sinks, with_lse, stack):
    refs = list(refs)
    q_ref = refs.pop(0)
    kv_refs = [refs.pop(0) for _ in range(nb + 1)]
    qc_ref, kaug_ref, band_ref = refs.pop(0), refs.pop(0), refs.pop(0)
    sink_ref = refs.pop(0) if use_sinks else None
    o_ref = refs.pop(0)
    lse_ref = refs.pop(0) if with_lse else None

    i = pl.program_id(q_axis)
    nk = (nb + 1) * Q_BLOCK
    col = lax.broadcasted_iota(jnp.int32, (Q_BLOCK, nk), 1)
    mask = band_ref[...] + jnp.where(col < (nb - i) * Q_BLOCK, -MASK_BIG, 0.0)
    rel_f = (nb * Q_BLOCK + lax.broadcasted_iota(jnp.int32, (Q_BLOCK, nk), 0) - col).astype(F32)
    lane = lax.broadcasted_iota(jnp.int32, (Q_BLOCK, LANES), 1)
    low_half = lane < HEAD_DIM
    kaug = kaug_ref[...]

    kv_cache = {}

    def kv_tile(off, kv_pair, swapped):
        key = (off, kv_pair, swapped)
        if key not in kv_cache:
            c0 = off + kv_pair * LANES
            tile = jnp.concatenate([r[:, c0:c0 + LANES] for r in kv_refs], axis=0)
            if swapped:
                tile = pltpu.roll(tile.astype(F32), HEAD_DIM, 1).astype(BF16)
            kv_cache[key] = tile
        return kv_cache[key]

    classes = {}
    for head in heads:
        pair, half, kv_pair, kv_half, slope, hidx = head
        key = (kv_pair, kv_half != half) if stack else hidx
        classes.setdefault(key, []).append(head)

    outs = [[None, None] for _ in range(n_pairs)]
    lses = [[None, None] for _ in range(n_pairs)]
    for members in classes.values():
        n_h = len(members)
        kv_pair, swapped = members[0][2], members[0][3] != members[0][1]
        q_rows = []
        for pair, half, _, _, _, hidx in members:
            qp = q_ref[:, pair * LANES:(pair + 1) * LANES]
            own = low_half if half == 0 else jnp.logical_not(low_half)
            qm = jnp.where(own, qp, jnp.zeros_like(qp))
            q_rows.append(jnp.concatenate([qm, qc_ref[hidx]], axis=1) if stack else qm)
        if stack:
            k_aug = jnp.concatenate([kv_tile(k_off, kv_pair, swapped), kaug], axis=1)
            s = _nt_dot(jnp.concatenate(q_rows, axis=0), k_aug)
            s = (s.reshape(n_h, Q_BLOCK, nk) + mask[None]).reshape(n_h * Q_BLOCK, nk)
        else:
            s = _nt_dot(q_rows[0], kv_tile(k_off, kv_pair, swapped)) - members[0][4] * rel_f + mask
        m = jnp.max(s, axis=1, keepdims=True)
        if use_sinks:
            assert n_h == 1
            sink = sink_ref[members[0][5]]
            m = jnp.maximum(m, sink)
        p = jnp.exp(s - m)
        den = jnp.sum(p, axis=1, keepdims=True)
        if use_sinks:
            den = den + jnp.exp(sink - m)
        r = _dot(p.astype(BF16), kv_tile(v_off, kv_pair, swapped)) / den
        lse = m + jnp.log(den) if with_lse else None
        for k, (pair, half, _, _, _, _) in enumerate(members):
            outs[pair][half] = r[k * Q_BLOCK:(k + 1) * Q_BLOCK]
            if with_lse:
                lses[pair][half] = jnp.broadcast_to(lse[k * Q_BLOCK:(k + 1) * Q_BLOCK], (Q_BLOCK, LANES))
    for pair in range(n_pairs):
        sl = slice(pair * LANES, (pair + 1) * LANES)
        o_ref[:, sl] = jnp.where(low_half, outs[pair][0], outs[pair][1])
        if with_lse:
            lse_ref[:, sl] = jnp.where(low_half, lses[pair][0], lses[pair][1])


def _banded_consts(heads, nb, max_dist):
    nk = (nb + 1) * Q_BLOCK
    slope = np.asarray([h[4] for h in heads], np.float32)[:, None]
    q_dist = (nb * Q_BLOCK + np.arange(Q_BLOCK, dtype=np.float32))[None, :]
    ones = np.ones_like(q_dist)
    vals = jnp.asarray(np.stack([slope * ones, slope * Q_BLOCK * ones, -slope * q_dist], axis=-1))
    hi = vals.astype(BF16)
    lo = (vals - hi.astype(F32)).astype(BF16)
    cols = jnp.stack([hi[..., 0], lo[..., 0], hi[..., 1], lo[..., 1], hi[..., 2], lo[..., 2]], axis=-1)
    qc = jnp.pad(cols, ((0, 0), (0, 0), (0, LANES - cols.shape[-1])))
    kaug = np.zeros((nk, LANES), np.float32)
    kaug[:, 0] = kaug[:, 1] = np.arange(nk) % Q_BLOCK
    kaug[:, 2] = kaug[:, 3] = np.arange(nk) // Q_BLOCK
    kaug[:, 4] = kaug[:, 5] = 1.0
    rel = nb * Q_BLOCK + np.arange(Q_BLOCK)[:, None] - np.arange(nk)[None, :]
    band = np.where((rel >= 0) & (rel <= max_dist), 0.0, -MASK_BIG).astype(np.float32)
    return [qc, jnp.asarray(kaug, BF16), jnp.asarray(band)]


def _banded_call(q_arr, kv_arr, *, grid, q_map, kv_map, out_map, out_lead, out_cols, nb, max_dist, heads,
                 k_off, v_off, q_axis, sinks=None, with_lse=False, stack=False):
    lead = (None,) * (q_arr.ndim - 2)
    blk = lead + (Q_BLOCK, SEC)
    in_specs = [pl.BlockSpec(blk, q_map)]
    args = [q_arr]
    for back in range(nb, -1, -1):
        in_specs.append(pl.BlockSpec(blk, functools.partial(kv_map, back=back)))
        args.append(kv_arr)
    consts = _banded_consts(heads, nb, max_dist)
    in_specs += [pl.BlockSpec(c.shape, lambda *_, nd=c.ndim: (0,) * nd) for c in consts]
    args += consts
    if sinks is not None:
        in_specs.append(pl.BlockSpec(memory_space=pltpu.SMEM))
        args.append(sinks)
    oblk = pl.BlockSpec(lead + (Q_BLOCK, out_cols), out_map)
    oshape = jax.ShapeDtypeStruct(out_lead + (out_cols,), F32)
    body = functools.partial(_banded_body, nb=nb, heads=heads, k_off=k_off, v_off=v_off,
                             n_pairs=out_cols // LANES, q_axis=q_axis, use_sinks=sinks is not None,
                             with_lse=with_lse, stack=stack)
    return pl.pallas_call(
        body,
        grid=grid,
        in_specs=in_specs,
        out_specs=[oblk, oblk] if with_lse else oblk,
        out_shape=[oshape, oshape] if with_lse else oshape,
        compiler_params=_cparams(("parallel",) * len(grid)),
    )(*args)


def _dilated_group(arr, gi, b, s):
    window, dil = A_GROUPS[gi]
    slopes = _slopes(A_HEADS)
    heads = tuple((hh // 2, hh % 2, hh // 2, hh % 2, slopes[gi * A_HEADS_PER_GROUP + hh] * dil, hh)
                  for hh in range(A_HEADS_PER_GROUP))
    common = dict(out_cols=A_OUT, nb=1, max_dist=window // dil, heads=heads, k_off=256, v_off=512,
                  with_lse=True)
    if dil == 1:
        return _banded_call(
            arr, arr, grid=(b, s // Q_BLOCK), q_axis=1,
            q_map=lambda bb, i: (bb, i, 0),
            kv_map=lambda bb, i, back: (bb, jnp.maximum(i - back, 0), 0),
            out_map=lambda bb, i: (bb, i, 0), out_lead=(b, s), **common)
    nc = s // (Q_BLOCK * dil)
    return _banded_call(
        arr, arr, grid=(b, dil, nc), q_axis=2,
        q_map=lambda bb, c, i: (bb * nc + i, c, 0, 0),
        kv_map=lambda bb, c, i, back: (bb * nc + jnp.maximum(i - back, 0), c, 0, 0),
        out_map=lambda bb, c, i: (bb * nc + i, c, 0, 0), out_lead=(b * nc, dil, Q_BLOCK), **common)


def _sink_swa(arr, sinks, b, s):
    slopes = _slopes(B_HEADS)
    rep = B_HEADS // B_KV_HEADS
    heads = tuple((h // 2, h % 2, 0, h // rep, slopes[h], h) for h in range(B_HEADS))
    return _banded_call(
        arr, arr, grid=(b, s // Q_BLOCK), q_axis=1,
        q_map=lambda bb, i: (bb, i, 0),
        kv_map=lambda bb, i, back: (bb, jnp.maximum(i - back, 0), 0),
        out_map=lambda bb, i: (bb, i, 0), out_lead=(b, s), out_cols=B_HEADS * HEAD_DIM,
        nb=1, max_dist=B_WINDOW - 1, heads=heads, k_off=512, v_off=640, sinks=sinks)


def _nsa_window(cq, ckv, b, s):
    slopes = _slopes(C_HEADS)
    heads = tuple((h // 2, h % 2, 0, h // C_REP, slopes[h], h) for h in range(C_HEADS))
    return _banded_call(
        cq, ckv, grid=(b, s // Q_BLOCK), q_axis=1,
        q_map=lambda bb, i: (bb, i, 0),
        kv_map=lambda bb, i, back: (bb, jnp.maximum(i - back, 0), 0),
        out_map=lambda bb, i: (bb, i, 0), out_lead=(b, s), out_cols=C_HEADS * HEAD_DIM,
        nb=-(-(C_WINDOW - 1) // Q_BLOCK), max_dist=C_WINDOW - 1, heads=heads, k_off=512, v_off=640, stack=True)


def _compress_body(t_ref, wb_ref, prow_ref, w2k_ref, w2vt_ref, kg_ref, bd_ref, kc_ref, vct_ref, *, n_chunks):
    hid_cols = 2 * C_KV_HEADS * CMP_HIDDEN
    u = jnp.zeros((n_chunks, hid_cols), F32)
    v = jnp.zeros((n_chunks, hid_cols), F32)
    pc = jnp.zeros((1, hid_cols), F32)
    for c in range(CMP_STRIDE):
        tc = t_ref[c]
        u = u + _dot(tc, wb_ref[0, c])
        v = v + _dot(tc, wb_ref[1, c])
        pc = pc + _dot(prow_ref[0, c], wb_ref[0, c])[0:1] + _dot(prow_ref[1, c], wb_ref[1, c])[0:1]
    hsum = u + pltpu.roll(v, n_chunks - 1, 0) + pc
    hid = (hsum * jax.nn.sigmoid(hsum)).astype(BF16)
    half = C_KV_HEADS * CMP_HIDDEN
    k = _dot(hid[:, :half], w2k_ref[...])
    hi, lo = _split(k * k)
    ss = _dot(hi, bd_ref[...]) + _dot(lo, bd_ref[...])
    kc_ref[...] = (k * lax.rsqrt(ss * (1.0 / HEAD_DIM) + RMS_EPS) * kg_ref[...]).astype(BF16)
    vct_ref[...] = _nt_dot(w2vt_ref[...], hid[:, half:]).astype(BF16)


def _compress(cmpd, wb, prow, w2k, w2vt, kg):
    b, _, n_chunks, width = cmpd.shape
    bd = jnp.asarray(np.kron(np.eye(LANES // HEAD_DIM), np.ones((HEAD_DIM, HEAD_DIM))), BF16)
    return pl.pallas_call(
        functools.partial(_compress_body, n_chunks=n_chunks),
        grid=(b,),
        in_specs=[
            pl.BlockSpec((None, CMP_STRIDE, n_chunks, width), lambda bb: (bb, 0, 0, 0)),
            _resident(wb.shape), _resident(prow.shape), _resident(w2k.shape), _resident(w2vt.shape),
            _resident((1, LANES)), _resident((LANES, LANES)),
        ],
        out_specs=[
            pl.BlockSpec((None, n_chunks, LANES), lambda bb: (bb, 0, 0)),
            pl.BlockSpec((None, LANES, n_chunks), lambda bb: (bb, 0, 0)),
        ],
        out_shape=[
            jax.ShapeDtypeStruct((b, n_chunks, LANES), BF16),
            jax.ShapeDtypeStruct((b, LANES, n_chunks), BF16),
        ],
        compiler_params=_cparams(("parallel",)),
    )(cmpd, wb, prow, w2k, w2vt, kg, bd)


def _to_natural(ot_list, eye):
    pairs = []
    for k in range(0, len(ot_list), 2):
        hi, lo = _split(jnp.concatenate([ot_list[k], ot_list[k + 1]], axis=0))
        pairs.append(_nt_dot(eye, hi) + _nt_dot(eye, lo))
    return jnp.concatenate(pairs, axis=1)


def _cmp_body(qt_ref, kc_ref, vct_ref, ovt_ref, eye_ref, spread_ref, cmask_ref,
              o_ref, selm_ref, cnt_ref, imp_sc, *, n_top):
    g = pl.program_id(1)
    i = pl.program_id(2)
    n_pad = kc_ref.shape[0]
    n_slc = ovt_ref.shape[0]
    per_q = Q_BLOCK // CMP_STRIDE
    own_rows = (lax.broadcasted_iota(jnp.int32, (LANES, Q_BLOCK), 0) // HEAD_DIM) == g
    slopes = _slopes(C_HEADS)

    def attend(rows):
        kc = kc_ref[0:rows, :]
        vct = vct_ref[:, 0:rows]
        mask = cmask_ref[pl.ds(pl.multiple_of(n_pad - per_q * i, 8), rows), :]
        n_f = (CMP_STRIDE * lax.broadcasted_iota(jnp.int32, (rows, Q_BLOCK), 0)).astype(F32)
        psum = jnp.zeros((rows, Q_BLOCK), F32)
        outs = []
        for r in range(C_REP):
            slope = jnp.where(g == 0, slopes[r], slopes[C_REP + r])
            qt = qt_ref[r * HEAD_DIM:(r + 1) * HEAD_DIM, :]
            q_pad = jnp.where(own_rows, jnp.concatenate([qt, qt], axis=0), jnp.zeros((LANES, Q_BLOCK), BF16))
            s = _dot(kc, q_pad) + (slope * n_f + mask)
            m = jnp.maximum(jnp.max(s, axis=0, keepdims=True), -1e20)
            e = jnp.exp(s - m)
            den = jnp.sum(e, axis=0, keepdims=True)
            p = e * (1.0 / jnp.where(den > 0, den, 1.0))
            psum = psum + p
            both = _dot(vct, p.astype(BF16))
            outs.append(jnp.where(g == 0, both[:HEAD_DIM], both[HEAD_DIM:]))
        o_ref[...] = _to_natural(outs, eye_ref[...])
        hi, lo = _split(psum)
        ovt = ovt_ref[:, 0:rows]
        imp_sc[...] = _dot(ovt, hi) + _dot(ovt, lo)

    n_var = n_pad // LANES
    for var in range(n_var):
        pl.when(i // (LANES // per_q) == var)(functools.partial(attend, (var + 1) * LANES))

    imp = imp_sc[...]
    j_idx = lax.broadcasted_iota(jnp.int32, (n_slc, Q_BLOCK), 0)
    t_q = i * Q_BLOCK + lax.broadcasted_iota(jnp.int32, (n_slc, Q_BLOCK), 1)
    cur = lax.shift_right_logical(t_q, int(math.log2(SLC_BLOCK)))
    forced = ((j_idx == 0) | (j_idx == cur) | (j_idx == cur - 1)) & (j_idx <= cur)
    v = jnp.where((j_idx <= cur) & jnp.logical_not(forced), imp, -1.0)
    sel = jnp.where(forced, 1.0, 0.0)
    for _ in range(n_top - 3):
        m = jnp.max(v, axis=0, keepdims=True)
        first = jnp.min(jnp.where((v == m) & (m >= 0.0), j_idx, n_slc), axis=0, keepdims=True)
        pick = j_idx == first
        sel = jnp.where(pick, 1.0, sel)
        v = jnp.where(pick, -1.0, v)
    neg = jnp.where(sel > 0, 0.0, -MASK_BIG).astype(BF16)
    selm_ref[...] = _dot(spread_ref[...], neg)
    cnt_ref[...] = _nt_dot(jnp.ones((8, Q_BLOCK), BF16), sel.astype(BF16))


def _cmp_select(qt, kc, vct, ovt, *, n_cmp):
    b, _, s = qt.shape
    g = C_KV_HEADS
    n_pad = kc.shape[1]
    n_slc = ovt.shape[0]
    nq = s // Q_BLOCK
    rows = C_REP * HEAD_DIM
    eye = jnp.asarray(np.eye(LANES), BF16)
    n_top = min(SLC_TOPK, n_slc)
    assert n_top > 3 and n_cmp == n_pad - 1 and n_pad % LANES == 0
    n_rel = np.arange(-n_pad, n_pad)[:, None]
    cmask = np.where(CMP_STRIDE * n_rel + CMP_BLOCK - 1 <= np.arange(Q_BLOCK)[None, :], 0.0, -MASK_BIG)
    return pl.pallas_call(
        functools.partial(_cmp_body, n_top=n_top),
        grid=(b, g, nq),
        in_specs=[
            pl.BlockSpec((None, rows, Q_BLOCK), lambda bb, gg, i: (bb, gg, i)),
            pl.BlockSpec((None, n_pad, LANES), lambda bb, gg, i: (bb, 0, 0)),
            pl.BlockSpec((None, LANES, n_pad), lambda bb, gg, i: (bb, 0, 0)),
            pl.BlockSpec((n_slc, n_pad), lambda bb, gg, i: (0, 0)),
            pl.BlockSpec((LANES, LANES), lambda bb, gg, i: (0, 0)),
            pl.BlockSpec((nq * AUG_ROWS, n_slc), lambda bb, gg, i: (0, 0)),
            pl.BlockSpec((2 * n_pad, Q_BLOCK), lambda bb, gg, i: (0, 0)),
        ],
        out_specs=[
            pl.BlockSpec((Q_BLOCK, rows), lambda bb, gg, i: (bb * nq + i, gg)),
            pl.BlockSpec((None, None, None, nq * AUG_ROWS, Q_BLOCK), lambda bb, gg, i: (bb, gg, i, 0, 0)),
            pl.BlockSpec((None, None, None, 8, n_slc), lambda bb, gg, i: (bb, gg, i, 0, 0)),
        ],
        out_shape=[
            jax.ShapeDtypeStruct((b * s, g * rows), F32),
            jax.ShapeDtypeStruct((b, g, nq, nq * AUG_ROWS, Q_BLOCK), F32),
            jax.ShapeDtypeStruct((b, g, nq, 8, n_slc), F32),
        ],
        scratch_shapes=[pltpu.VMEM((n_slc, Q_BLOCK), F32)],
        compiler_params=_cparams(("parallel", "parallel", "parallel")),
    )(qt, kc, vct, ovt, eye, jnp.asarray(_block_spread(nq), BF16), jnp.asarray(cmask, F32))


def _slc_body(list_ref, qt_ref, ks_ref, vst_ref, selm_ref, eye_ref, o_ref, qaug, m_sc, l_sc, acc_sc, *, nq, stride):
    bb = pl.program_id(0)
    g = pl.program_id(1)
    i = pl.program_id(2)
    width = C_REP * Q_BLOCK
    slopes = _slopes(C_HEADS)
    slope_s = [jnp.where(g == 0, slopes[r], slopes[C_REP + r]) for r in range(C_REP)]

    own_rows = (lax.broadcasted_iota(jnp.int32, (LANES, width), 0) // HEAD_DIM) == g
    q6 = jnp.concatenate([qt_ref[r * HEAD_DIM:(r + 1) * HEAD_DIM, :] for r in range(C_REP)], axis=1)
    qaug[0:LANES, :] = jnp.where(own_rows, jnp.concatenate([q6, q6], axis=0), jnp.zeros((LANES, width), BF16))
    head = lax.broadcasted_iota(jnp.int32, (LANES, width), 1) // Q_BLOCK
    row = lax.broadcasted_iota(jnp.int32, (LANES, width), 0)
    slope_t = jnp.zeros((LANES, width), F32)
    for r in range(C_REP):
        slope_t = jnp.where(head == r, slope_s[r], slope_t)
    s_hi, s_lo = _split(slope_t)
    slope_rows = jnp.where(row == AUG_POS, s_hi.astype(F32), jnp.where(row == AUG_POS + 1, s_lo.astype(F32), 0.0))
    qaug[LANES:, :] = slope_rows.astype(BF16)

    q_loc = lax.broadcasted_iota(jnp.int32, (Q_BLOCK, Q_BLOCK), 1)
    k_loc = lax.broadcasted_iota(jnp.int32, (Q_BLOCK, Q_BLOCK), 0)
    causal = jnp.where(k_loc > q_loc, -MASK_BIG, 0.0)

    m_sc[...] = jnp.full(m_sc.shape, NEG_INF, F32)
    l_sc[...] = jnp.zeros(l_sc.shape, F32)
    acc_sc[...] = jnp.zeros(acc_sc.shape, F32)

    def scores(jj):
        k_tile = ks_ref[pl.ds(pl.multiple_of(jj * Q_BLOCK, Q_BLOCK), Q_BLOCK), :]
        rows = selm_ref[pl.ds(pl.multiple_of(jj * AUG_ROWS, AUG_ROWS), AUG_ROWS), :]
        mask = jnp.concatenate([jnp.broadcast_to(rows[e:e + 1], (SLC_BLOCK, Q_BLOCK)) for e in range(2)], axis=0)
        return _dot(k_tile, qaug[...]), mask

    def accumulate(tiles):
        ps = [[] for _ in tiles]
        alphas = []
        for r in range(C_REP):
            sl = slice(r * Q_BLOCK, (r + 1) * Q_BLOCK)
            m_old = m_sc[:, sl]
            m_new = m_old
            parts = []
            for jj, st, mask, extra in tiles:
                s = st[:, sl] + mask
                c = slope_s[r] * ((jj - i) * Q_BLOCK).astype(F32) + extra
                m_new = jnp.maximum(m_new, jnp.max(s, axis=0, keepdims=True) + c)
                parts.append((s, c))
            alpha = jnp.exp(m_old - m_new)
            l_new = alpha * l_sc[:, sl]
            for k, (s, c) in enumerate(parts):
                p = jnp.exp(s + (c - m_new))
                l_new = l_new + jnp.sum(p, axis=0, keepdims=True)
                ps[k].append(p.astype(BF16))
            l_sc[:, sl] = l_new
            m_sc[:, sl] = m_new
            alphas.append(alpha)
        pv = None
        for k, (jj, _, _, _) in enumerate(tiles):
            term = _dot(vst_ref[jj], jnp.concatenate(ps[k], axis=1))
            pv = term if pv is None else pv + term
        acc_sc[...] = jnp.concatenate(alphas, axis=1) * acc_sc[...] + pv

    base = ((bb * pl.num_programs(1) + g) * nq + i) * stride
    count = list_ref[base]

    def listed(slot):
        jj = list_ref[base + 1 + slot]
        return (jj, *scores(jj), jnp.where(slot < count, 0.0, -MASK_BIG))

    st, mask = scores(i)
    accumulate([(i, st, mask + causal, 0.0)] + [listed(u) for u in range(SLC_GROUP - 1)])

    def step(k, carry):
        accumulate([listed(SLC_GROUP - 1 + SLC_GROUP * k + u) for u in range(SLC_GROUP)])
        return carry

    rest = jnp.maximum(count - (SLC_GROUP - 1), 0)
    lax.fori_loop(0, (rest + SLC_GROUP - 1) // SLC_GROUP, step, 0)
    l = l_sc[...]
    o = acc_sc[...] / jnp.where(l > 0, l, 1.0)
    o_ref[...] = _to_natural([o[:, r * Q_BLOCK:(r + 1) * Q_BLOCK] for r in range(C_REP)], eye_ref[...])


def _slc_attention(lists, qt, ks, vst, sel, *, stride):
    b, _, s = qt.shape
    g = C_KV_HEADS
    nq = s // Q_BLOCK
    rows = C_REP * HEAD_DIM
    width = C_REP * Q_BLOCK
    eye = jnp.asarray(np.eye(LANES), BF16)
    grid_spec = pltpu.PrefetchScalarGridSpec(
        num_scalar_prefetch=1,
        grid=(b, g, nq),
        in_specs=[
            pl.BlockSpec((None, rows, Q_BLOCK), lambda bb, gg, i, bits: (bb, gg, i)),
            pl.BlockSpec((None, s, 2 * LANES), lambda bb, gg, i, bits: (bb, 0, 0)),
            pl.BlockSpec((None, nq, None, HEAD_DIM, Q_BLOCK), lambda bb, gg, i, bits: (bb, 0, gg, 0, 0)),
            pl.BlockSpec((None, None, None, nq * AUG_ROWS, Q_BLOCK), lambda bb, gg, i, bits: (bb, gg, i, 0, 0)),
            pl.BlockSpec((LANES, LANES), lambda bb, gg, i, bits: (0, 0)),
        ],
        out_specs=pl.BlockSpec((Q_BLOCK, rows), lambda bb, gg, i, bits: (bb * nq + i, gg)),
        scratch_shapes=[
            pltpu.VMEM((2 * LANES, width), BF16),
            pltpu.VMEM((1, width), F32),
            pltpu.VMEM((1, width), F32),
            pltpu.VMEM((HEAD_DIM, width), F32),
        ],
    )
    return pl.pallas_call(
        functools.partial(_slc_body, nq=nq, stride=stride),
        grid_spec=grid_spec,
        out_shape=jax.ShapeDtypeStruct((b * s, g * rows), F32),
        compiler_params=_cparams(("parallel", "parallel", "parallel")),
    )(lists, qt, ks, vst, sel, eye)


def _merge_body(x_ref, g_ref, oa0, la0, oa1, la1, oa2, la2, ob_ref, ocmp_ref, oslc_ref, owin_ref, cg_ref,
                p4t_ref, p16t_ref, ex_ref, wg0_ref, wg1_ref, wg2_ref, wa_ref, wb_ref, wc_ref,
                out_ref, h_ref, oall_ref):
    @pl.when(pl.program_id(1) == 0)
    def _():
        h_ref[...] = _rms_rows(x_ref[...], g_ref[...])

        def natural(ref, pt_ref):
            hi, lo = _split(ref[...].reshape(TM, A_OUT))
            return _dot(pt_ref[...], hi) + _dot(pt_ref[...], lo)

        o0, l0 = oa0[...], la0[...]
        o1, l1 = natural(oa1, p4t_ref), natural(la1, p4t_ref)
        o2, l2 = natural(oa2, p16t_ref), natural(la2, p16t_ref)
        mx = jnp.maximum(jnp.maximum(l0, l1), l2)
        e0, e1, e2 = jnp.exp(l0 - mx), jnp.exp(l1 - mx), jnp.exp(l2 - mx)
        oall_ref[:, 0:A_OUT] = ((e0 * o0 + e1 * o1 + e2 * o2) / (e0 + e1 + e2)).astype(BF16)
        oall_ref[:, A_OUT:A_OUT + B_HEADS * HEAD_DIM] = ob_ref[...].astype(BF16)
        cg_split = jnp.concatenate(_split(cg_ref[...]), axis=1)
        o_c = None
        for w, o_ref in enumerate((ocmp_ref, oslc_ref, owin_ref)):
            term = _dot(cg_split, ex_ref[w]) * o_ref[...]
            o_c = term if o_c is None else o_c + term
        oall_ref[:, A_OUT + B_HEADS * HEAD_DIM:] = o_c.astype(BF16)

    h = h_ref[...]
    c0, c1 = A_OUT, A_OUT + B_HEADS * HEAD_DIM
    merged = jax.nn.sigmoid(_dot(h, wg0_ref[...])) * _dot(oall_ref[:, 0:c0], wa_ref[...])
    merged += jax.nn.sigmoid(_dot(h, wg1_ref[...])) * _dot(oall_ref[:, c0:c1], wb_ref[...])
    merged += jax.nn.sigmoid(_dot(h, wg2_ref[...])) * _dot(oall_ref[:, c1:], wc_ref[...])
    out_ref[...] = merged.astype(BF16)


def _merge(x, g, a_outs, ob, ocmp, oslc, owin, cg, w_gate, wa, wb, wc, ex, *, tn=512):
    t, d = x.shape
    per16 = CHUNK16 // TM
    n_t = d // tn
    (oa0, la0), (oa1, la1), (oa2, la2) = a_outs

    def rows(a):
        return pl.BlockSpec((TM, a.shape[1]), lambda i, n: (i, 0))

    a1_spec = pl.BlockSpec((None, 4, Q_BLOCK, A_OUT), lambda i, n: (i, 0, 0, 0))
    a2_spec = pl.BlockSpec((None, 16, TM // 16, A_OUT), lambda i, n: (i // per16, 0, i % per16, 0))
    p4t = jnp.asarray(_deinterleave(TM, 4).T, BF16)
    p16t = jnp.asarray(_deinterleave(TM, 16).T, BF16)
    in_specs = [
        rows(x), _resident((1, d)),
        rows(oa0), rows(la0), a1_spec, a1_spec, a2_spec, a2_spec,
        rows(ob), rows(ocmp), rows(oslc), rows(owin), rows(cg),
        _resident((TM, TM)), _resident((TM, TM)), _resident(ex.shape),
        pl.BlockSpec((d, tn), lambda i, n: (0, n)),
        pl.BlockSpec((d, tn), lambda i, n: (0, n + n_t)),
        pl.BlockSpec((d, tn), lambda i, n: (0, n + 2 * n_t)),
        pl.BlockSpec((wa.shape[0], tn), lambda i, n: (0, n)),
        pl.BlockSpec((wb.shape[0], tn), lambda i, n: (0, n)),
        pl.BlockSpec((wc.shape[0], tn), lambda i, n: (0, n)),
    ]
    return pl.pallas_call(
        _merge_body,
        grid=(t // TM, n_t),
        in_specs=in_specs,
        out_specs=pl.BlockSpec((TM, tn), lambda i, n: (i, n)),
        out_shape=jax.ShapeDtypeStruct((t, d), BF16),
        scratch_shapes=[pltpu.VMEM((TM, d), BF16), pltpu.VMEM((TM, wa.shape[0] + wb.shape[0] + wc.shape[0]), BF16)],
        compiler_params=_cparams(("parallel", "arbitrary")),
    )(x, g.reshape(1, d), oa0, la0, oa1, la1, oa2, la2, ob, ocmp, oslc, owin, cg, p4t, p16t, ex,
      w_gate, w_gate, w_gate, wa, wb, wc)


def _out_body(x_ref, m_ref, w_ref, o_ref):
    o_ref[...] = x_ref[...] + _dot(m_ref[...], w_ref[...])


def _out_proj(x, merged, w_out):
    t, d = x.shape
    rows = pl.BlockSpec((TM, d), lambda i: (i, 0))
    return pl.pallas_call(
        _out_body,
        grid=(t // TM,),
        in_specs=[rows, rows, _resident((d, d))],
        out_specs=rows,
        out_shape=jax.ShapeDtypeStruct((t, d), F32),
        compiler_params=_cparams(("parallel",)),
    )(x, merged, w_out)


def _qkv_column_params(qk_gain):
    flag, gain, scale = [], [], []
    one = jnp.ones((HEAD_DIM,), F32)

    def add(n_heads, normed, is_q, gvec):
        for _ in range(n_heads):
            flag.append(np.full((HEAD_DIM,), 1.0 if normed else 0.0, np.float32))
            gain.append(gvec if normed else one)
            scale.append(np.full((HEAD_DIM,), HEAD_DIM ** -0.5 if is_q else 1.0, np.float32))

    for _ in range(len(A_GROUPS)):
        add(A_HEADS_PER_GROUP, True, True, qk_gain[0, 0])
        add(A_HEADS_PER_GROUP, True, False, qk_gain[0, 1])
        add(A_HEADS_PER_GROUP, False, False, one)
    add(B_HEADS, True, True, qk_gain[1, 0])
    add(B_KV_HEADS, True, False, qk_gain[1, 1])
    add(B_KV_HEADS, False, False, one)
    add(C_HEADS, True, True, qk_gain[2, 0])
    for normed in (False, False, True, False, True, False):
        add(C_KV_HEADS, normed, False, qk_gain[2, 1])
    flag = np.concatenate(flag)
    assert flag.shape[0] == QKV_COLS
    return jnp.asarray(flag), jnp.concatenate(gain) * jnp.asarray(np.concatenate(scale))


def _overlap_t(n_slc, n_pad, n_cmp):
    n = np.arange(n_pad)[None, :]
    j = np.arange(n_slc)[:, None]
    start, end = CMP_STRIDE * n, CMP_STRIDE * n + CMP_BLOCK - 1
    ov = (start <= SLC_BLOCK * j + SLC_BLOCK - 1) & (end >= SLC_BLOCK * j) & (n < n_cmp)
    return jnp.asarray(ov, BF16)


def _gate_expand():
    ex = np.zeros((3, LANES, C_HEADS * HEAD_DIM), np.float32)
    for w in range(3):
        for h in range(C_HEADS):
            ex[w, h * 3 + w, h * HEAD_DIM:(h + 1) * HEAD_DIM] = 1.0
    return jnp.asarray(np.concatenate([ex, ex], axis=1), BF16)


def _compress_weights(cmp_pos, cmp_w1, cmp_w2):
    n_q = 2 * C_KV_HEADS
    w1 = cmp_w1.reshape(2, 2, CMP_STRIDE, HEAD_DIM, CMP_HIDDEN)
    w1q = jnp.repeat(w1, C_KV_HEADS, axis=0)
    wb = jnp.einsum("qhcdn,qp->hcqdpn", w1q, jnp.eye(n_q, dtype=F32))
    wb = wb.reshape(2, CMP_STRIDE, n_q * HEAD_DIM, n_q * CMP_HIDDEN).astype(BF16)
    pos = cmp_pos.reshape(2, 2, CMP_STRIDE, HEAD_DIM)
    prow = jnp.repeat(pos, C_KV_HEADS, axis=0).transpose(1, 2, 0, 3).reshape(2, CMP_STRIDE, 1, n_q * HEAD_DIM)
    prow = jnp.broadcast_to(prow, (2, CMP_STRIDE, 8, n_q * HEAD_DIM)).astype(BF16)
    eye_g = jnp.eye(C_KV_HEADS, dtype=F32)
    w2k = jnp.kron(eye_g, cmp_w2[0]).astype(BF16)
    w2vt = jnp.kron(eye_g, cmp_w2[1]).T.astype(BF16)
    return wb, prow, w2k, w2vt


def _token_mixing(x, b, s, mix_norm, w_in, qk_gain, sinks, cmp_pos, cmp_w1, cmp_w2, w_a, w_b, w_c):
    t, d = x.shape
    assert s % CHUNK16 == 0 and d % 512 == 0
    c_gate_cols = 3 * C_HEADS
    w_qkv = w_in[:, :QKV_COLS + LANES].astype(BF16)
    flag, gs = _qkv_column_params(qk_gain)
    a0, a1, a2, bsec, cq, ckv, cmpd, qt, ks, vst, cg = _qkv_proj(x, mix_norm, w_qkv, flag, gs, b, s)

    a_outs = [_dilated_group(a0.reshape(b, s, SEC), 0, b, s), _dilated_group(a1, 1, b, s),
              _dilated_group(a2, 2, b, s)]
    a_outs[0] = tuple(v.reshape(t, A_OUT) for v in a_outs[0])
    o_b = _sink_swa(bsec.reshape(b, s, SEC), sinks.astype(F32), b, s).reshape(t, -1)
    o_win = _nsa_window(cq.reshape(b, s, SEC), ckv.reshape(b, s, SEC), b, s).reshape(t, -1)

    n_chunks = s // CMP_STRIDE
    n_cmp = (s - CMP_BLOCK) // CMP_STRIDE + 1
    n_slc = s // SLC_BLOCK
    nq = s // Q_BLOCK
    kg = jnp.tile(qk_gain[2, 1], C_KV_HEADS).reshape(1, LANES)
    kc, vct = _compress(cmpd, *_compress_weights(cmp_pos, cmp_w1, cmp_w2), kg)
    o_cmp, sel, cnt = _cmp_select(qt, kc, vct, _overlap_t(n_slc, n_chunks, n_cmp), n_cmp=n_cmp)

    act = (cnt[:, :, :, 0, :] > 0).reshape(b, C_KV_HEADS, nq, nq, 2).any(axis=-1)
    act = act & (jnp.arange(nq)[None, :] < jnp.arange(nq)[:, None])
    order = jnp.argsort(jnp.logical_not(act), axis=-1, stable=True).astype(jnp.int32)
    count = jnp.sum(act, axis=-1, dtype=jnp.int32)[..., None]
    lists = jnp.concatenate([count, order] + [jnp.zeros_like(count)] * (SLC_GROUP - 1), axis=-1)
    o_slc = _slc_attention(lists.reshape(-1), qt, ks.reshape(b, s, 2 * LANES),
                           vst.reshape(b, nq, C_KV_HEADS, HEAD_DIM, Q_BLOCK), sel, stride=nq + SLC_GROUP)

    return _merge(x, mix_norm, a_outs, o_b, o_cmp, o_slc, o_win, cg,
                  w_in[:, QKV_COLS + c_gate_cols:].astype(BF16), w_a.astype(BF16), w_b.astype(BF16),
                  w_c.astype(BF16), _gate_expand())


def kernel(x, ffn1_norm, ffn1_w_gu, ffn1_w_down, mix_norm, w_in, qk_gain, sinks, cmp_pos, cmp_w1, cmp_w2,
           w_branch_a, w_branch_b, w_branch_c, w_out, ffn2_norm, ffn2_w_gu, ffn2_w_down):
    b, s, d = x.shape
    h = x.reshape(b * s, d)
    w1_gu, w1_down = ffn1_w_gu.astype(BF16), ffn1_w_down.astype(BF16)
    w2_gu, w2_down = ffn2_w_gu.astype(BF16), ffn2_w_down.astype(BF16)
    for l in range(ffn1_norm.shape[0]):
        h = _ffn(h, ffn1_norm[l], w1_gu, w1_down, l)
        merged = _token_mixing(h, b, s, mix_norm[l], w_in[l], qk_gain[l], sinks[l], cmp_pos[l], cmp_w1[l],
                               cmp_w2[l], w_branch_a[l], w_branch_b[l], w_branch_c[l])
        h = _out_proj(h, merged, w_out[l].astype(BF16))
        h = _ffn(h, ffn2_norm[l], w2_gu, w2_down, l)
    return h.reshape(b, s, d)
```

```python
import functools
import math

import numpy as np
import jax
import jax.numpy as jnp
from jax import lax
from jax.experimental import pallas as pl
from jax.experimental.pallas import tpu as pltpu

F32 = jnp.float32
BF16 = jnp.bfloat16

HEAD_DIM = 64
Q_BLOCK = 128
LANES = 128
A_GROUPS = ((128, 1), (512, 4), (2048, 16))
A_HEADS_PER_GROUP = 4
A_HEADS = 12
A_OUT = A_HEADS_PER_GROUP * HEAD_DIM
B_HEADS = 8
B_KV_HEADS = 2
B_WINDOW = 128
C_HEADS = 12
C_KV_HEADS = 2
C_REP = C_HEADS // C_KV_HEADS
CMP_BLOCK = 32
CMP_STRIDE = 16
CMP_HIDDEN = 256
SLC_BLOCK = 64
SLC_TOPK = 16
C_WINDOW = 512
RMS_EPS = 1e-6
NEG_INF = -1e30
SEC = 768
N_SEC = 6
QKV_COLS = SEC * N_SEC
TM = 512
CHUNK16 = Q_BLOCK * 16
VMEM_LIMIT = 56 * 1024 * 1024


def _slopes(n):
    return [float(2.0 ** (-8.0 * (h + 1) / n)) for h in range(n)]


def _cparams(sem):
    return pltpu.CompilerParams(dimension_semantics=sem, vmem_limit_bytes=VMEM_LIMIT)


def _dot(a, b):
    return jnp.dot(a, b, preferred_element_type=F32)


def _nt_dot(a, b):
    return lax.dot_general(a, b, (((1,), (1,)), ((), ())), preferred_element_type=F32)


def _split(v):
    hi = v.astype(BF16)
    return hi, (v - hi.astype(F32)).astype(BF16)


def _resident(shape):
    return pl.BlockSpec(shape, lambda *_: (0,) * len(shape), pipeline_mode=pl.Buffered(1))


def _rms_rows(x, g):
    ms = jnp.mean(x * x, axis=-1, keepdims=True)
    return (x * lax.rsqrt(ms + RMS_EPS) * g).astype(BF16)


def _deinterleave(n, d):
    p = np.zeros((n, n), np.float32)
    r = np.arange(n // d)
    for c in range(d):
        p[c * (n // d) + r, d * r + c] = 1.0
    return p


AUG_POS = 0
AUG_ROWS = 8
MASK_BIG = 1e30
SLC_GROUP = 5


def _key_pattern(n):
    pat = np.zeros((n, LANES), np.float32)
    pat[:, AUG_POS] = pat[:, AUG_POS + 1] = np.arange(n) % Q_BLOCK
    return pat


def _block_spread(n_tiles):
    m = np.zeros((n_tiles * AUG_ROWS, 2 * n_tiles), np.float32)
    jj = np.arange(n_tiles)
    for e in range(2):
        m[AUG_ROWS * jj + e, 2 * jj + e] = 1.0
    return m


def _ffn_body(x_ref, g_ref, wg_ref, wu_ref, wd_ref, o_ref, h_ref, *, n_f):
    f = pl.program_id(1)

    @pl.when(f == 0)
    def _():
        h_ref[...] = _rms_rows(x_ref[...], g_ref[...])
        o_ref[...] = jnp.zeros_like(o_ref)

    h = h_ref[...]
    gate = _dot(h, wg_ref[...])
    up = _dot(h, wu_ref[...])
    act = (gate * jax.nn.sigmoid(gate) * up).astype(BF16)
    o_ref[...] += _dot(act, wd_ref[...])

    @pl.when(f == n_f - 1)
    def _():
        o_ref[...] = x_ref[...] + 0.5 * o_ref[...]


def _ffn(x, g, w_gu, w_down, layer, *, tm=1024, tf=512):
    t, d = x.shape
    d_ff = w_down.shape[1]
    n_f = d_ff // tf
    rows = pl.BlockSpec((tm, d), lambda i, f: (i, 0))
    return pl.pallas_call(
        functools.partial(_ffn_body, n_f=n_f),
        grid=(t // tm, n_f),
        in_specs=[
            rows,
            pl.BlockSpec((1, d), lambda i, f: (0, 0)),
            pl.BlockSpec((None, d, tf), lambda i, f: (layer, 0, f)),
            pl.BlockSpec((None, d, tf), lambda i, f: (layer, 0, f + n_f)),
            pl.BlockSpec((None, tf, d), lambda i, f: (layer, f, 0)),
        ],
        out_specs=rows,
        out_shape=jax.ShapeDtypeStruct((t, d), F32),
        scratch_shapes=[pltpu.VMEM((tm, d), BF16)],
        compiler_params=_cparams(("parallel", "arbitrary")),
    )(x, g.reshape(1, d), w_gu, w_gu, w_down)


NORM_TILE = 256
NORM_TILES = {0: (0, 1), 1: (0, 1), 2: (0, 1), 3: (0, 1, 2), 4: (0, 1, 2), 5: (1, 2)}


def _head_sumsq(y, bd):
    return _dot((y * y).astype(BF16), bd)


def _qkv_body(x_ref, g_ref, w_ref, flag_ref, gs_ref, bd_ref, p4_ref, p16_ref, eye_ref, kpat_ref,
              a0_ref, a1_ref, a2_ref, b_ref, cq_ref, ckv_ref, cmpd_ref, qt_ref, ks_ref, vst_ref, cg_ref):
    h = _rms_rows(x_ref[...], g_ref[...])
    bd = bd_ref[...]

    def section(k):
        sl = slice(k * SEC, (k + 1) * SEC)
        if k < len(A_GROUPS):
            y = jnp.concatenate([_dot(h, w_ref[:, part * A_HEADS * HEAD_DIM + k * A_OUT:
                                                  part * A_HEADS * HEAD_DIM + (k + 1) * A_OUT])
                                 for part in range(3)], axis=1)
        else:
            y = _dot(h, w_ref[:, sl])
        tiles = []
        for c in range(SEC // NORM_TILE):
            yc = y[:, c * NORM_TILE:(c + 1) * NORM_TILE]
            if c in NORM_TILES[k]:
                cols = slice(k * SEC + c * NORM_TILE, k * SEC + (c + 1) * NORM_TILE)
                inv = lax.rsqrt(_head_sumsq(yc, bd) * (1.0 / HEAD_DIM) + RMS_EPS)
                yc = yc * jnp.where(flag_ref[:, cols] > 0, inv, 1.0) * gs_ref[:, cols]
            tiles.append(yc.astype(BF16))
        return jnp.concatenate(tiles, axis=1)

    a0_ref[...] = section(0)
    a1_ref[...] = _dot(p4_ref[...], section(1)).astype(BF16).reshape(a1_ref.shape)
    a2_ref[...] = _dot(p16_ref[...], section(2)).astype(BF16).reshape(a2_ref.shape)
    b_ref[...] = section(3)
    y_cq = section(4)
    cq_ref[...] = y_cq
    qt_ref[...] = _nt_dot(eye_ref[...], y_cq).astype(BF16)
    y_ckv = section(5)
    ckv_ref[...] = y_ckv
    cmpd_ref[...] = _dot(p16_ref[...], y_ckv[:, 0:2 * LANES]).astype(BF16).reshape(cmpd_ref.shape)
    ks_ref[:, 0:LANES] = y_ckv[:, 2 * LANES:3 * LANES]
    ks_ref[:, LANES:2 * LANES] = kpat_ref[...]
    eye = eye_ref[0:LANES, 0:LANES]
    for kb in range(vst_ref.shape[0]):
        vt = _nt_dot(eye, y_ckv[kb * Q_BLOCK:(kb + 1) * Q_BLOCK, 3 * LANES:4 * LANES]).astype(BF16)
        for gg in range(C_KV_HEADS):
            vst_ref[kb, gg] = vt[gg * HEAD_DIM:(gg + 1) * HEAD_DIM]
    cg_ref[...] = jax.nn.sigmoid(_dot(h, w_ref[:, QKV_COLS:QKV_COLS + LANES]))


def _qkv_proj(x, g, w, flag, gs, b, s):
    t, d = x.shape
    tiles_per_batch = s // TM
    per16 = CHUNK16 // TM
    bd = jnp.asarray(np.kron(np.eye(NORM_TILE // HEAD_DIM), np.ones((HEAD_DIM, HEAD_DIM))), BF16)
    p4 = jnp.asarray(_deinterleave(TM, 4), BF16)
    p16 = jnp.asarray(_deinterleave(TM, 16), BF16)
    eye = jnp.asarray(np.eye(SEC), BF16)
    nat = pl.BlockSpec((TM, SEC), lambda i: (i, 0))
    out_specs = [
        nat,
        pl.BlockSpec((None, 4, Q_BLOCK, SEC), lambda i: (i, 0, 0, 0)),
        pl.BlockSpec((None, 16, TM // 16, SEC), lambda i: (i // per16, 0, i % per16, 0)),
        nat, nat, nat,
        pl.BlockSpec((None, 16, TM // 16, 2 * LANES), lambda i: (i // tiles_per_batch, 0, i % tiles_per_batch, 0)),
        pl.BlockSpec((None, SEC, TM), lambda i: (i // tiles_per_batch, 0, i % tiles_per_batch)),
        pl.BlockSpec((TM, 2 * LANES), lambda i: (i, 0)),
        pl.BlockSpec((TM // Q_BLOCK, C_KV_HEADS, HEAD_DIM, Q_BLOCK), lambda i: (i, 0, 0, 0)),
        pl.BlockSpec((TM, LANES), lambda i: (i, 0)),
    ]
    out_shape = [
        jax.ShapeDtypeStruct((t, SEC), BF16),
        jax.ShapeDtypeStruct((t // TM, 4, Q_BLOCK, SEC), BF16),
        jax.ShapeDtypeStruct((t // CHUNK16, 16, Q_BLOCK, SEC), BF16),
        jax.ShapeDtypeStruct((t, SEC), BF16),
        jax.ShapeDtypeStruct((t, SEC), BF16),
        jax.ShapeDtypeStruct((t, SEC), BF16),
        jax.ShapeDtypeStruct((b, 16, s // 16, 2 * LANES), BF16),
        jax.ShapeDtypeStruct((b, SEC, s), BF16),
        jax.ShapeDtypeStruct((t, 2 * LANES), BF16),
        jax.ShapeDtypeStruct((t // Q_BLOCK, C_KV_HEADS, HEAD_DIM, Q_BLOCK), BF16),
        jax.ShapeDtypeStruct((t, LANES), F32),
    ]
    n_w = w.shape[1]
    return pl.pallas_call(
        _qkv_body,
        grid=(t // TM,),
        in_specs=[
            pl.BlockSpec((TM, d), lambda i: (i, 0)),
            _resident((1, d)),
            _resident((d, n_w)),
            _resident((1, QKV_COLS)),
            _resident((1, QKV_COLS)),
            _resident((NORM_TILE, NORM_TILE)),
            _resident((TM, TM)),
            _resident((TM, TM)),
            _resident((SEC, SEC)),
            _resident((TM, LANES)),
        ],
        out_specs=out_specs,
        out_shape=out_shape,
        compiler_params=_cparams(("parallel",)),
    )(x, g.reshape(1, d), w, flag.reshape(1, -1), gs.reshape(1, -1), bd, p4, p16, eye,
      jnp.asarray(_key_pattern(TM), BF16))


def _banded_body(*refs, nb, heads, k_off, v_off, n_pairs, q_axis, use_sinks, with_lse, stack, fold):
    refs = list(refs)
    q_ref = refs.pop(0)
    kv_refs = [refs.pop(0) for _ in range(nb + 1)]
    qc_ref, kaug_ref, band_ref = refs.pop(0), refs.pop(0), refs.pop(0)
    sink_ref = refs.pop(0) if use_sinks else None
    o_ref = refs.pop(0)
    lse_ref = refs.pop(0) if with_lse else None

    i = pl.program_id(q_axis)
    nk = (nb + 1) * Q_BLOCK
    col = lax.broadcasted_iota(jnp.int32, (Q_BLOCK, nk), 1)
    mask = band_ref[...] + jnp.where(col < (nb - i) * Q_BLOCK, -MASK_BIG, 0.0)
    rel_f = (nb * Q_BLOCK + lax.broadcasted_iota(jnp.int32, (Q_BLOCK, nk), 0) - col).astype(F32)
    lane = lax.broadcasted_iota(jnp.int32, (Q_BLOCK, LANES), 1)
    low_half = lane < HEAD_DIM
    kaug = kaug_ref[...]

    kv_cache = {}

    def kv_tile(off, kv_pair, swapped):
        key = (off, kv_pair, swapped)
        if key not in kv_cache:
            c0 = off + kv_pair * LANES
            tile = jnp.concatenate([r[:, c0:c0 + LANES] for r in kv_refs], axis=0)
            if swapped:
                tile = pltpu.roll(tile.astype(F32), HEAD_DIM, 1).astype(BF16)
            kv_cache[key] = tile
        return kv_cache[key]

    classes = {}
    for head in heads:
        pair, half, kv_pair, kv_half, slope, hidx = head
        key = (kv_pair, kv_half != half) if stack else hidx
        classes.setdefault(key, []).append(head)

    outs = [[None, None] for _ in range(n_pairs)]
    lses = [[None, None] for _ in range(n_pairs)]
    groups = list(classes.values())

    scores = []
    for members in groups:
        n_h = len(members)
        kv_pair, swapped = members[0][2], members[0][3] != members[0][1]
        q_rows = []
        for pair, half, _, _, _, hidx in members:
            qp = q_ref[:, pair * LANES:(pair + 1) * LANES]
            own = low_half if half == 0 else jnp.logical_not(low_half)
            qm = jnp.where(own, qp, jnp.zeros_like(qp))
            q_rows.append(jnp.concatenate([qm, qc_ref[hidx]], axis=1) if fold else qm)
        if fold:
            k_aug = jnp.concatenate([kv_tile(k_off, kv_pair, swapped), kaug], axis=1)
            s = _nt_dot(jnp.concatenate(q_rows, axis=0), k_aug)
            s = (s.reshape(n_h, Q_BLOCK, nk) + mask[None]).reshape(n_h * Q_BLOCK, nk)
        else:
            bias = jnp.concatenate([mask - member[4] * rel_f for member in members], axis=0)
            s = _nt_dot(jnp.concatenate(q_rows, axis=0), kv_tile(k_off, kv_pair, swapped)) + bias
        scores.append(s)

    probs = []
    for members, s in zip(groups, scores):
        m = jnp.max(s, axis=1, keepdims=True)
        if use_sinks:
            assert len(members) == 1
            sink = sink_ref[members[0][5]]
            m = jnp.maximum(m, sink)
        p = jnp.exp(s - m)
        den = jnp.sum(p, axis=1, keepdims=True)
        if use_sinks:
            den = den + jnp.exp(sink - m)
        probs.append((p.astype(BF16), m, den))

    for members, (p, m, den) in zip(groups, probs):
        kv_pair, swapped = members[0][2], members[0][3] != members[0][1]
        r = _dot(p, kv_tile(v_off, kv_pair, swapped)) / den
        lse = m + jnp.log(den) if with_lse else None
        for k, (pair, half, _, _, _, _) in enumerate(members):
            outs[pair][half] = r[k * Q_BLOCK:(k + 1) * Q_BLOCK]
            if with_lse:
                lses[pair][half] = jnp.broadcast_to(lse[k * Q_BLOCK:(k + 1) * Q_BLOCK], (Q_BLOCK, LANES))
    for pair in range(n_pairs):
        sl = slice(pair * LANES, (pair + 1) * LANES)
        o_ref[:, sl] = jnp.where(low_half, outs[pair][0], outs[pair][1])
        if with_lse:
            lse_ref[:, sl] = jnp.where(low_half, lses[pair][0], lses[pair][1])


def _banded_consts(heads, nb, max_dist):
    nk = (nb + 1) * Q_BLOCK
    slope = np.asarray([h[4] for h in heads], np.float32)[:, None]
    q_dist = (nb * Q_BLOCK + np.arange(Q_BLOCK, dtype=np.float32))[None, :]
    ones = np.ones_like(q_dist)
    vals = jnp.asarray(np.stack([slope * ones, slope * Q_BLOCK * ones, -slope * q_dist], axis=-1))
    hi = vals.astype(BF16)
    lo = (vals - hi.astype(F32)).astype(BF16)
    cols = jnp.stack([hi[..., 0], lo[..., 0], hi[..., 1], lo[..., 1], hi[..., 2], lo[..., 2]], axis=-1)
    qc = jnp.pad(cols, ((0, 0), (0, 0), (0, LANES - cols.shape[-1])))
    kaug = np.zeros((nk, LANES), np.float32)
    kaug[:, 0] = kaug[:, 1] = np.arange(nk) % Q_BLOCK
    kaug[:, 2] = kaug[:, 3] = np.arange(nk) // Q_BLOCK
    kaug[:, 4] = kaug[:, 5] = 1.0
    rel = nb * Q_BLOCK + np.arange(Q_BLOCK)[:, None] - np.arange(nk)[None, :]
    band = np.where((rel >= 0) & (rel <= max_dist), 0.0, -MASK_BIG).astype(np.float32)
    return [qc, jnp.asarray(kaug, BF16), jnp.asarray(band)]


def _banded_call(q_arr, kv_arr, *, grid, q_map, kv_map, out_map, out_lead, out_cols, nb, max_dist, heads,
                 k_off, v_off, q_axis, sinks=None, with_lse=False, stack=False, fold=False):
    lead = (None,) * (q_arr.ndim - 2)
    blk = lead + (Q_BLOCK, SEC)
    in_specs = [pl.BlockSpec(blk, q_map)]
    args = [q_arr]
    for back in range(nb, -1, -1):
        in_specs.append(pl.BlockSpec(blk, functools.partial(kv_map, back=back)))
        args.append(kv_arr)
    consts = _banded_consts(heads, nb, max_dist)
    in_specs += [pl.BlockSpec(c.shape, lambda *_, nd=c.ndim: (0,) * nd) for c in consts]
    args += consts
    if sinks is not None:
        in_specs.append(pl.BlockSpec(memory_space=pltpu.SMEM))
        args.append(sinks)
    oblk = pl.BlockSpec(lead + (Q_BLOCK, out_cols), out_map)
    oshape = jax.ShapeDtypeStruct(out_lead + (out_cols,), F32)
    body = functools.partial(_banded_body, nb=nb, heads=heads, k_off=k_off, v_off=v_off,
                             n_pairs=out_cols // LANES, q_axis=q_axis, use_sinks=sinks is not None,
                             with_lse=with_lse, stack=stack, fold=fold)
    return pl.pallas_call(
        body,
        grid=grid,
        in_specs=in_specs,
        out_specs=[oblk, oblk] if with_lse else oblk,
        out_shape=[oshape, oshape] if with_lse else oshape,
        compiler_params=_cparams(("parallel",) * len(grid)),
    )(*args)


def _dilated_group(arr, gi, b, s):
    window, dil = A_GROUPS[gi]
    slopes = _slopes(A_HEADS)
    heads = tuple((hh // 2, hh % 2, hh // 2, hh % 2, slopes[gi * A_HEADS_PER_GROUP + hh] * dil, hh)
                  for hh in range(A_HEADS_PER_GROUP))
    common = dict(out_cols=A_OUT, nb=1, max_dist=window // dil, heads=heads, k_off=256, v_off=512,
                  with_lse=True, stack=False)
    if dil == 1:
        return _banded_call(
            arr, arr, grid=(b, s // Q_BLOCK), q_axis=1,
            q_map=lambda bb, i: (bb, i, 0),
            kv_map=lambda bb, i, back: (bb, jnp.maximum(i - back, 0), 0),
            out_map=lambda bb, i: (bb, i, 0), out_lead=(b, s), **common)
    nc = s // (Q_BLOCK * dil)
    return _banded_call(
        arr, arr, grid=(b, dil, nc), q_axis=2,
        q_map=lambda bb, c, i: (bb * nc + i, c, 0, 0),
        kv_map=lambda bb, c, i, back: (bb * nc + jnp.maximum(i - back, 0), c, 0, 0),
        out_map=lambda bb, c, i: (bb * nc + i, c, 0, 0), out_lead=(b * nc, dil, Q_BLOCK), **common)


def _sink_swa(arr, sinks, b, s):
    slopes = _slopes(B_HEADS)
    rep = B_HEADS // B_KV_HEADS
    heads = tuple((h // 2, h % 2, 0, h // rep, slopes[h], h) for h in range(B_HEADS))
    return _banded_call(
        arr, arr, grid=(b, s // Q_BLOCK), q_axis=1,
        q_map=lambda bb, i: (bb, i, 0),
        kv_map=lambda bb, i, back: (bb, jnp.maximum(i - back, 0), 0),
        out_map=lambda bb, i: (bb, i, 0), out_lead=(b, s), out_cols=B_HEADS * HEAD_DIM,
        nb=1, max_dist=B_WINDOW - 1, heads=heads, k_off=512, v_off=640, sinks=sinks)


def _nsa_window(cq, ckv, b, s):
    slopes = _slopes(C_HEADS)
    heads = tuple((h // 2, h % 2, 0, h // C_REP, slopes[h], h) for h in range(C_HEADS))
    return _banded_call(
        cq, ckv, grid=(b, s // Q_BLOCK), q_axis=1,
        q_map=lambda bb, i: (bb, i, 0),
        kv_map=lambda bb, i, back: (bb, jnp.maximum(i - back, 0), 0),
        out_map=lambda bb, i: (bb, i, 0), out_lead=(b, s), out_cols=C_HEADS * HEAD_DIM,
        nb=-(-(C_WINDOW - 1) // Q_BLOCK), max_dist=C_WINDOW - 1, heads=heads, k_off=512, v_off=640, stack=True, fold=True)


def _compress_body(t_ref, wb_ref, prow_ref, w2k_ref, w2vt_ref, kg_ref, bd_ref, kc_ref, vct_ref, *, n_chunks):
    hid_cols = 2 * C_KV_HEADS * CMP_HIDDEN
    u = jnp.zeros((n_chunks, hid_cols), F32)
    v = jnp.zeros((n_chunks, hid_cols), F32)
    pc = jnp.zeros((1, hid_cols), F32)
    for c in range(CMP_STRIDE):
        tc = t_ref[c]
        u = u + _dot(tc, wb_ref[0, c])
        v = v + _dot(tc, wb_ref[1, c])
        pc = pc + _dot(prow_ref[0, c], wb_ref[0, c])[0:1] + _dot(prow_ref[1, c], wb_ref[1, c])[0:1]
    hsum = u + pltpu.roll(v, n_chunks - 1, 0) + pc
    hid = (hsum * jax.nn.sigmoid(hsum)).astype(BF16)
    half = C_KV_HEADS * CMP_HIDDEN
    k = _dot(hid[:, :half], w2k_ref[...])
    hi, lo = _split(k * k)
    ss = _dot(hi, bd_ref[...]) + _dot(lo, bd_ref[...])
    kc_ref[...] = (k * lax.rsqrt(ss * (1.0 / HEAD_DIM) + RMS_EPS) * kg_ref[...]).astype(BF16)
    vct_ref[...] = _nt_dot(w2vt_ref[...], hid[:, half:]).astype(BF16)


def _compress(cmpd, wb, prow, w2k, w2vt, kg):
    b, _, n_chunks, width = cmpd.shape
    bd = jnp.asarray(np.kron(np.eye(LANES // HEAD_DIM), np.ones((HEAD_DIM, HEAD_DIM))), BF16)
    return pl.pallas_call(
        functools.partial(_compress_body, n_chunks=n_chunks),
        grid=(b,),
        in_specs=[
            pl.BlockSpec((None, CMP_STRIDE, n_chunks, width), lambda bb: (bb, 0, 0, 0)),
            _resident(wb.shape), _resident(prow.shape), _resident(w2k.shape), _resident(w2vt.shape),
            _resident((1, LANES)), _resident((LANES, LANES)),
        ],
        out_specs=[
            pl.BlockSpec((None, n_chunks, LANES), lambda bb: (bb, 0, 0)),
            pl.BlockSpec((None, LANES, n_chunks), lambda bb: (bb, 0, 0)),
        ],
        out_shape=[
            jax.ShapeDtypeStruct((b, n_chunks, LANES), BF16),
            jax.ShapeDtypeStruct((b, LANES, n_chunks), BF16),
        ],
        compiler_params=_cparams(("parallel",)),
    )(cmpd, wb, prow, w2k, w2vt, kg, bd)


def _to_natural(ot_list, eye):
    pairs = []
    for k in range(0, len(ot_list), 2):
        hi, lo = _split(jnp.concatenate([ot_list[k], ot_list[k + 1]], axis=0))
        pairs.append(_nt_dot(eye, hi) + _nt_dot(eye, lo))
    return jnp.concatenate(pairs, axis=1)


def _cmp_body(qt_ref, kc_ref, vct_ref, ovt_ref, eye_ref, spread_ref, cmask_ref,
              o_ref, selm_ref, cnt_ref, imp_sc, *, n_top):
    g = pl.program_id(1)
    i = pl.program_id(2)
    n_pad = kc_ref.shape[0]
    n_slc = ovt_ref.shape[0]
    per_q = Q_BLOCK // CMP_STRIDE
    own_rows = (lax.broadcasted_iota(jnp.int32, (LANES, Q_BLOCK), 0) // HEAD_DIM) == g
    slopes = _slopes(C_HEADS)

    def attend(rows):
        kc = kc_ref[0:rows, :]
        vct = vct_ref[:, 0:rows]
        mask = cmask_ref[pl.ds(pl.multiple_of(n_pad - per_q * i, 8), rows), :]
        n_f = (CMP_STRIDE * lax.broadcasted_iota(jnp.int32, (rows, Q_BLOCK), 0)).astype(F32)
        scores = []
        for r in range(C_REP):
            slope = jnp.where(g == 0, slopes[r], slopes[C_REP + r])
            qt = qt_ref[r * HEAD_DIM:(r + 1) * HEAD_DIM, :]
            q_pad = jnp.where(own_rows, jnp.concatenate([qt, qt], axis=0), jnp.zeros((LANES, Q_BLOCK), BF16))
            scores.append(_dot(kc, q_pad) + (slope * n_f + mask))
        psum = jnp.zeros((rows, Q_BLOCK), F32)
        probs = []
        for s in scores:
            m = jnp.maximum(jnp.max(s, axis=0, keepdims=True), -1e20)
            e = jnp.exp(s - m)
            den = jnp.sum(e, axis=0, keepdims=True)
            p = e * (1.0 / jnp.where(den > 0, den, 1.0))
            psum = psum + p
            probs.append(p.astype(BF16))
        outs = []
        for p in probs:
            both = _dot(vct, p)
            outs.append(jnp.where(g == 0, both[:HEAD_DIM], both[HEAD_DIM:]))
        o_ref[...] = _to_natural(outs, eye_ref[...])
        hi, lo = _split(psum)
        ovt = ovt_ref[:, 0:rows]
        imp_sc[...] = _dot(ovt, hi) + _dot(ovt, lo)

    n_var = n_pad // LANES
    for var in range(n_var):
        pl.when(i // (LANES // per_q) == var)(functools.partial(attend, (var + 1) * LANES))

    imp = imp_sc[...]
    j_idx = lax.broadcasted_iota(jnp.int32, (n_slc, Q_BLOCK), 0)
    t_q = i * Q_BLOCK + lax.broadcasted_iota(jnp.int32, (n_slc, Q_BLOCK), 1)
    cur = lax.shift_right_logical(t_q, int(math.log2(SLC_BLOCK)))
    forced = ((j_idx == 0) | (j_idx == cur) | (j_idx == cur - 1)) & (j_idx <= cur)
    v = jnp.where((j_idx <= cur) & jnp.logical_not(forced), imp, -1.0)
    sel = jnp.where(forced, 1.0, 0.0)
    for _ in range(n_top - 3):
        m = jnp.max(v, axis=0, keepdims=True)
        first = jnp.min(jnp.where((v == m) & (m >= 0.0), j_idx, n_slc), axis=0, keepdims=True)
        pick = j_idx == first
        sel = jnp.where(pick, 1.0, sel)
        v = jnp.where(pick, -1.0, v)
    neg = jnp.where(sel > 0, 0.0, -MASK_BIG).astype(BF16)
    selm_ref[...] = _dot(spread_ref[...], neg)
    cnt_ref[...] = _nt_dot(jnp.ones((8, Q_BLOCK), BF16), sel.astype(BF16))


def _cmp_select(qt, kc, vct, ovt, *, n_cmp):
    b, _, s = qt.shape
    g = C_KV_HEADS
    n_pad = kc.shape[1]
    n_slc = ovt.shape[0]
    nq = s // Q_BLOCK
    rows = C_REP * HEAD_DIM
    eye = jnp.asarray(np.eye(LANES), BF16)
    n_top = min(SLC_TOPK, n_slc)
    assert n_top > 3 and n_cmp == n_pad - 1 and n_pad % LANES == 0
    n_rel = np.arange(-n_pad, n_pad)[:, None]
    cmask = np.where(CMP_STRIDE * n_rel + CMP_BLOCK - 1 <= np.arange(Q_BLOCK)[None, :], 0.0, -MASK_BIG)
    return pl.pallas_call(
        functools.partial(_cmp_body, n_top=n_top),
        grid=(b, g, nq),
        in_specs=[
            pl.BlockSpec((None, rows, Q_BLOCK), lambda bb, gg, i: (bb, gg, i)),
            pl.BlockSpec((None, n_pad, LANES), lambda bb, gg, i: (bb, 0, 0)),
            pl.BlockSpec((None, LANES, n_pad), lambda bb, gg, i: (bb, 0, 0)),
            pl.BlockSpec((n_slc, n_pad), lambda bb, gg, i: (0, 0)),
            pl.BlockSpec((LANES, LANES), lambda bb, gg, i: (0, 0)),
            pl.BlockSpec((nq * AUG_ROWS, n_slc), lambda bb, gg, i: (0, 0)),
            pl.BlockSpec((2 * n_pad, Q_BLOCK), lambda bb, gg, i: (0, 0)),
        ],
        out_specs=[
            pl.BlockSpec((Q_BLOCK, rows), lambda bb, gg, i: (bb * nq + i, gg)),
            pl.BlockSpec((None, None, None, nq * AUG_ROWS, Q_BLOCK), lambda bb, gg, i: (bb, gg, i, 0, 0)),
            pl.BlockSpec((None, None, None, 8, n_slc), lambda bb, gg, i: (bb, gg, i, 0, 0)),
        ],
        out_shape=[
            jax.ShapeDtypeStruct((b * s, g * rows), F32),
            jax.ShapeDtypeStruct((b, g, nq, nq * AUG_ROWS, Q_BLOCK), F32),
            jax.ShapeDtypeStruct((b, g, nq, 8, n_slc), F32),
        ],
        scratch_shapes=[pltpu.VMEM((n_slc, Q_BLOCK), F32)],
        compiler_params=_cparams(("parallel", "parallel", "parallel")),
    )(qt, kc, vct, ovt, eye, jnp.asarray(_block_spread(nq), BF16), jnp.asarray(cmask, F32))


def _slc_body(list_ref, qt_ref, ks_ref, vst_ref, selm_ref, eye_ref, o_ref, qaug, m_sc, l_sc, acc_sc, *, nq, stride):
    bb = pl.program_id(0)
    g = pl.program_id(1)
    i = pl.program_id(2)
    width = C_REP * Q_BLOCK
    slopes = _slopes(C_HEADS)
    slope_s = [jnp.where(g == 0, slopes[r], slopes[C_REP + r]) for r in range(C_REP)]

    own_rows = (lax.broadcasted_iota(jnp.int32, (LANES, width), 0) // HEAD_DIM) == g
    q6 = jnp.concatenate([qt_ref[r * HEAD_DIM:(r + 1) * HEAD_DIM, :] for r in range(C_REP)], axis=1)
    qaug[0:LANES, :] = jnp.where(own_rows, jnp.concatenate([q6, q6], axis=0), jnp.zeros((LANES, width), BF16))
    head = lax.broadcasted_iota(jnp.int32, (LANES, width), 1) // Q_BLOCK
    row = lax.broadcasted_iota(jnp.int32, (LANES, width), 0)
    slope_t = jnp.zeros((LANES, width), F32)
    for r in range(C_REP):
        slope_t = jnp.where(head == r, slope_s[r], slope_t)
    s_hi, s_lo = _split(slope_t)
    slope_rows = jnp.where(row == AUG_POS, s_hi.astype(F32), jnp.where(row == AUG_POS + 1, s_lo.astype(F32), 0.0))
    qaug[LANES:, :] = slope_rows.astype(BF16)

    q_loc = lax.broadcasted_iota(jnp.int32, (Q_BLOCK, Q_BLOCK), 1)
    k_loc = lax.broadcasted_iota(jnp.int32, (Q_BLOCK, Q_BLOCK), 0)
    causal = jnp.where(k_loc > q_loc, -MASK_BIG, 0.0)

    m_sc[...] = jnp.full(m_sc.shape, NEG_INF, F32)
    l_sc[...] = jnp.zeros(l_sc.shape, F32)
    acc_sc[...] = jnp.zeros(acc_sc.shape, F32)

    def scores(jj):
        k_tile = ks_ref[pl.ds(pl.multiple_of(jj * Q_BLOCK, Q_BLOCK), Q_BLOCK), :]
        rows = selm_ref[pl.ds(pl.multiple_of(jj * AUG_ROWS, AUG_ROWS), AUG_ROWS), :]
        mask = jnp.concatenate([jnp.broadcast_to(rows[e:e + 1], (SLC_BLOCK, Q_BLOCK)) for e in range(2)], axis=0)
        return _dot(k_tile, qaug[...]), mask

    def accumulate(tiles):
        ps = [[] for _ in tiles]
        alphas, heads_parts = [], []
        for r in range(C_REP):
            sl = slice(r * Q_BLOCK, (r + 1) * Q_BLOCK)
            m_old = m_sc[:, sl]
            m_new = m_old
            parts = []
            for jj, st, mask, extra in tiles:
                s = st[:, sl] + mask
                c = slope_s[r] * ((jj - i) * Q_BLOCK).astype(F32) + extra
                m_new = jnp.maximum(m_new, jnp.max(s, axis=0, keepdims=True) + c)
                parts.append((s, c))
            alphas.append(jnp.exp(m_old - m_new))
            m_sc[:, sl] = m_new
            heads_parts.append((parts, m_new))
        for r, (parts, m_new) in enumerate(heads_parts):
            sl = slice(r * Q_BLOCK, (r + 1) * Q_BLOCK)
            l_new = alphas[r] * l_sc[:, sl]
            for k, (s, c) in enumerate(parts):
                p = jnp.exp(s + (c - m_new))
                l_new = l_new + jnp.sum(p, axis=0, keepdims=True)
                ps[k].append(p.astype(BF16))
            l_sc[:, sl] = l_new
        pv = None
        for k, (jj, _, _, _) in enumerate(tiles):
            term = _dot(vst_ref[jj], jnp.concatenate(ps[k], axis=1))
            pv = term if pv is None else pv + term
        acc_sc[...] = jnp.concatenate(alphas, axis=1) * acc_sc[...] + pv

    base = ((bb * pl.num_programs(1) + g) * nq + i) * stride
    count = list_ref[base]

    def listed(slot):
        jj = list_ref[base + 1 + slot]
        return (jj, *scores(jj), jnp.where(slot < count, 0.0, -MASK_BIG))

    st, mask = scores(i)
    accumulate([(i, st, mask + causal, 0.0)] + [listed(u) for u in range(SLC_GROUP - 1)])

    def step(k, carry):
        accumulate([listed(SLC_GROUP - 1 + SLC_GROUP * k + u) for u in range(SLC_GROUP)])
        return carry

    rest = jnp.maximum(count - (SLC_GROUP - 1), 0)
    lax.fori_loop(0, (rest + SLC_GROUP - 1) // SLC_GROUP, step, 0)
    l = l_sc[...]
    o = acc_sc[...] / jnp.where(l > 0, l, 1.0)
    o_ref[...] = _to_natural([o[:, r * Q_BLOCK:(r + 1) * Q_BLOCK] for r in range(C_REP)], eye_ref[...])


def _slc_attention(lists, qt, ks, vst, sel, *, stride):
    b, _, s = qt.shape
    g = C_KV_HEADS
    nq = s // Q_BLOCK
    rows = C_REP * HEAD_DIM
    width = C_REP * Q_BLOCK
    eye = jnp.asarray(np.eye(LANES), BF16)
    grid_spec = pltpu.PrefetchScalarGridSpec(
        num_scalar_prefetch=1,
        grid=(b, g, nq),
        in_specs=[
            pl.BlockSpec((None, rows, Q_BLOCK), lambda bb, gg, i, bits: (bb, gg, i)),
            pl.BlockSpec((None, s, 2 * LANES), lambda bb, gg, i, bits: (bb, 0, 0)),
            pl.BlockSpec((None, nq, None, HEAD_DIM, Q_BLOCK), lambda bb, gg, i, bits: (bb, 0, gg, 0, 0)),
            pl.BlockSpec((None, None, None, nq * AUG_ROWS, Q_BLOCK), lambda bb, gg, i, bits: (bb, gg, i, 0, 0)),
            pl.BlockSpec((LANES, LANES), lambda bb, gg, i, bits: (0, 0)),
        ],
        out_specs=pl.BlockSpec((Q_BLOCK, rows), lambda bb, gg, i, bits: (bb * nq + i, gg)),
        scratch_shapes=[
            pltpu.VMEM((2 * LANES, width), BF16),
            pltpu.VMEM((1, width), F32),
            pltpu.VMEM((1, width), F32),
            pltpu.VMEM((HEAD_DIM, width), F32),
        ],
    )
    return pl.pallas_call(
        functools.partial(_slc_body, nq=nq, stride=stride),
        grid_spec=grid_spec,
        out_shape=jax.ShapeDtypeStruct((b * s, g * rows), F32),
        compiler_params=_cparams(("parallel", "parallel", "parallel")),
    )(lists, qt, ks, vst, sel, eye)


def _merge_body(x_ref, g_ref, oa0, la0, oa1, la1, oa2, la2, ob_ref, ocmp_ref, oslc_ref, owin_ref, cg_ref,
                p4t_ref, p16t_ref, ex_ref, wg0_ref, wg1_ref, wg2_ref, wa_ref, wb_ref, wc_ref,
                out_ref, h_ref, oall_ref):
    @pl.when(pl.program_id(1) == 0)
    def _():
        h_ref[...] = _rms_rows(x_ref[...], g_ref[...])

        def natural(ref, pt_ref):
            hi, lo = _split(ref[...].reshape(TM, A_OUT))
            return _dot(pt_ref[...], hi) + _dot(pt_ref[...], lo)

        o0, l0 = oa0[...], la0[...]
        o1, l1 = natural(oa1, p4t_ref), natural(la1, p4t_ref)
        o2, l2 = natural(oa2, p16t_ref), natural(la2, p16t_ref)
        mx = jnp.maximum(jnp.maximum(l0, l1), l2)
        e0, e1, e2 = jnp.exp(l0 - mx), jnp.exp(l1 - mx), jnp.exp(l2 - mx)
        oall_ref[:, 0:A_OUT] = ((e0 * o0 + e1 * o1 + e2 * o2) / (e0 + e1 + e2)).astype(BF16)
        oall_ref[:, A_OUT:A_OUT + B_HEADS * HEAD_DIM] = ob_ref[...].astype(BF16)
        cg_split = jnp.concatenate(_split(cg_ref[...]), axis=1)
        o_c = None
        for w, o_ref in enumerate((ocmp_ref, oslc_ref, owin_ref)):
            term = _dot(cg_split, ex_ref[w]) * o_ref[...]
            o_c = term if o_c is None else o_c + term
        oall_ref[:, A_OUT + B_HEADS * HEAD_DIM:] = o_c.astype(BF16)

    h = h_ref[...]
    c0, c1 = A_OUT, A_OUT + B_HEADS * HEAD_DIM
    merged = jax.nn.sigmoid(_dot(h, wg0_ref[...])) * _dot(oall_ref[:, 0:c0], wa_ref[...])
    merged += jax.nn.sigmoid(_dot(h, wg1_ref[...])) * _dot(oall_ref[:, c0:c1], wb_ref[...])
    merged += jax.nn.sigmoid(_dot(h, wg2_ref[...])) * _dot(oall_ref[:, c1:], wc_ref[...])
    out_ref[...] = merged.astype(BF16)


def _merge(x, g, a_outs, ob, ocmp, oslc, owin, cg, w_gate, wa, wb, wc, ex, *, tn=512):
    t, d = x.shape
    per16 = CHUNK16 // TM
    n_t = d // tn
    (oa0, la0), (oa1, la1), (oa2, la2) = a_outs

    def rows(a):
        return pl.BlockSpec((TM, a.shape[1]), lambda i, n: (i, 0))

    a1_spec = pl.BlockSpec((None, 4, Q_BLOCK, A_OUT), lambda i, n: (i, 0, 0, 0))
    a2_spec = pl.BlockSpec((None, 16, TM // 16, A_OUT), lambda i, n: (i // per16, 0, i % per16, 0))
    p4t = jnp.asarray(_deinterleave(TM, 4).T, BF16)
    p16t = jnp.asarray(_deinterleave(TM, 16).T, BF16)
    in_specs = [
        rows(x), _resident((1, d)),
        rows(oa0), rows(la0), a1_spec, a1_spec, a2_spec, a2_spec,
        rows(ob), rows(ocmp), rows(oslc), rows(owin), rows(cg),
        _resident((TM, TM)), _resident((TM, TM)), _resident(ex.shape),
        pl.BlockSpec((d, tn), lambda i, n: (0, n)),
        pl.BlockSpec((d, tn), lambda i, n: (0, n + n_t)),
        pl.BlockSpec((d, tn), lambda i, n: (0, n + 2 * n_t)),
        pl.BlockSpec((wa.shape[0], tn), lambda i, n: (0, n)),
        pl.BlockSpec((wb.shape[0], tn), lambda i, n: (0, n)),
        pl.BlockSpec((wc.shape[0], tn), lambda i, n: (0, n)),
    ]
    return pl.pallas_call(
        _merge_body,
        grid=(t // TM, n_t),
        in_specs=in_specs,
        out_specs=pl.BlockSpec((TM, tn), lambda i, n: (i, n)),
        out_shape=jax.ShapeDtypeStruct((t, d), BF16),
        scratch_shapes=[pltpu.VMEM((TM, d), BF16), pltpu.VMEM((TM, wa.shape[0] + wb.shape[0] + wc.shape[0]), BF16)],
        compiler_params=_cparams(("parallel", "arbitrary")),
    )(x, g.reshape(1, d), oa0, la0, oa1, la1, oa2, la2, ob, ocmp, oslc, owin, cg, p4t, p16t, ex,
      w_gate, w_gate, w_gate, wa, wb, wc)


def _out_body(x_ref, m_ref, w_ref, o_ref):
    o_ref[...] = x_ref[...] + _dot(m_ref[...], w_ref[...])


def _out_proj(x, merged, w_out):
    t, d = x.shape
    rows = pl.BlockSpec((TM, d), lambda i: (i, 0))
    return pl.pallas_call(
        _out_body,
        grid=(t // TM,),
        in_specs=[rows, rows, _resident((d, d))],
        out_specs=rows,
        out_shape=jax.ShapeDtypeStruct((t, d), F32),
        compiler_params=_cparams(("parallel",)),
    )(x, merged, w_out)


def _qkv_column_params(qk_gain):
    flag, gain, scale = [], [], []
    one = jnp.ones((HEAD_DIM,), F32)

    def add(n_heads, normed, is_q, gvec):
        for _ in range(n_heads):
            flag.append(np.full((HEAD_DIM,), 1.0 if normed else 0.0, np.float32))
            gain.append(gvec if normed else one)
            scale.append(np.full((HEAD_DIM,), HEAD_DIM ** -0.5 if is_q else 1.0, np.float32))

    for _ in range(len(A_GROUPS)):
        add(A_HEADS_PER_GROUP, True, True, qk_gain[0, 0])
        add(A_HEADS_PER_GROUP, True, False, qk_gain[0, 1])
        add(A_HEADS_PER_GROUP, False, False, one)
    add(B_HEADS, True, True, qk_gain[1, 0])
    add(B_KV_HEADS, True, False, qk_gain[1, 1])
    add(B_KV_HEADS, False, False, one)
    add(C_HEADS, True, True, qk_gain[2, 0])
    for normed in (False, False, True, False, True, False):
        add(C_KV_HEADS, normed, False, qk_gain[2, 1])
    flag = np.concatenate(flag)
    assert flag.shape[0] == QKV_COLS
    return jnp.asarray(flag), jnp.concatenate(gain) * jnp.asarray(np.concatenate(scale))


def _overlap_t(n_slc, n_pad, n_cmp):
    n = np.arange(n_pad)[None, :]
    j = np.arange(n_slc)[:, None]
    start, end = CMP_STRIDE * n, CMP_STRIDE * n + CMP_BLOCK - 1
    ov = (start <= SLC_BLOCK * j + SLC_BLOCK - 1) & (end >= SLC_BLOCK * j) & (n < n_cmp)
    return jnp.asarray(ov, BF16)


def _gate_expand():
    ex = np.zeros((3, LANES, C_HEADS * HEAD_DIM), np.float32)
    for w in range(3):
        for h in range(C_HEADS):
            ex[w, h * 3 + w, h * HEAD_DIM:(h + 1) * HEAD_DIM] = 1.0
    return jnp.asarray(np.concatenate([ex, ex], axis=1), BF16)


def _compress_weights(cmp_pos, cmp_w1, cmp_w2):
    n_q = 2 * C_KV_HEADS
    w1 = cmp_w1.reshape(2, 2, CMP_STRIDE, HEAD_DIM, CMP_HIDDEN)
    w1q = jnp.repeat(w1, C_KV_HEADS, axis=0)
    wb = jnp.einsum("qhcdn,qp->hcqdpn", w1q, jnp.eye(n_q, dtype=F32))
    wb = wb.reshape(2, CMP_STRIDE, n_q * HEAD_DIM, n_q * CMP_HIDDEN).astype(BF16)
    pos = cmp_pos.reshape(2, 2, CMP_STRIDE, HEAD_DIM)
    prow = jnp.repeat(pos, C_KV_HEADS, axis=0).transpose(1, 2, 0, 3).reshape(2, CMP_STRIDE, 1, n_q * HEAD_DIM)
    prow = jnp.broadcast_to(prow, (2, CMP_STRIDE, 8, n_q * HEAD_DIM)).astype(BF16)
    eye_g = jnp.eye(C_KV_HEADS, dtype=F32)
    w2k = jnp.kron(eye_g, cmp_w2[0]).astype(BF16)
    w2vt = jnp.kron(eye_g, cmp_w2[1]).T.astype(BF16)
    return wb, prow, w2k, w2vt


def _token_mixing(x, b, s, mix_norm, w_in, qk_gain, sinks, cmp_pos, cmp_w1, cmp_w2, w_a, w_b, w_c):
    t, d = x.shape
    assert s % CHUNK16 == 0 and d % 512 == 0
    c_gate_cols = 3 * C_HEADS
    w_qkv = w_in[:, :QKV_COLS + LANES].astype(BF16)
    flag, gs = _qkv_column_params(qk_gain)
    a0, a1, a2, bsec, cq, ckv, cmpd, qt, ks, vst, cg = _qkv_proj(x, mix_norm, w_qkv, flag, gs, b, s)

    a_outs = [_dilated_group(a0.reshape(b, s, SEC), 0, b, s), _dilated_group(a1, 1, b, s),
              _dilated_group(a2, 2, b, s)]
    a_outs[0] = tuple(v.reshape(t, A_OUT) for v in a_outs[0])
    o_b = _sink_swa(bsec.reshape(b, s, SEC), sinks.astype(F32), b, s).reshape(t, -1)
    o_win = _nsa_window(cq.reshape(b, s, SEC), ckv.reshape(b, s, SEC), b, s).reshape(t, -1)

    n_chunks = s // CMP_STRIDE
    n_cmp = (s - CMP_BLOCK) // CMP_STRIDE + 1
    n_slc = s // SLC_BLOCK
    nq = s // Q_BLOCK
    kg = jnp.tile(qk_gain[2, 1], C_KV_HEADS).reshape(1, LANES)
    kc, vct = _compress(cmpd, *_compress_weights(cmp_pos, cmp_w1, cmp_w2), kg)
    o_cmp, sel, cnt = _cmp_select(qt, kc, vct, _overlap_t(n_slc, n_chunks, n_cmp), n_cmp=n_cmp)

    act = (cnt[:, :, :, 0, :] > 0).reshape(b, C_KV_HEADS, nq, nq, 2).any(axis=-1)
    act = act & (jnp.arange(nq)[None, :] < jnp.arange(nq)[:, None])
    order = jnp.argsort(jnp.logical_not(act), axis=-1, stable=True).astype(jnp.int32)
    count = jnp.sum(act, axis=-1, dtype=jnp.int32)[..., None]
    lists = jnp.concatenate([count, order] + [jnp.zeros_like(count)] * (SLC_GROUP - 1), axis=-1)
    o_slc = _slc_attention(lists.reshape(-1), qt, ks.reshape(b, s, 2 * LANES),
                           vst.reshape(b, nq, C_KV_HEADS, HEAD_DIM, Q_BLOCK), sel, stride=nq + SLC_GROUP)

    return _merge(x, mix_norm, a_outs, o_b, o_cmp, o_slc, o_win, cg,
                  w_in[:, QKV_COLS + c_gate_cols:].astype(BF16), w_a.astype(BF16), w_b.astype(BF16),
                  w_c.astype(BF16), _gate_expand())


def kernel(x, ffn1_norm, ffn1_w_gu, ffn1_w_down, mix_norm, w_in, qk_gain, sinks, cmp_pos, cmp_w1, cmp_w2,
           w_branch_a, w_branch_b, w_branch_c, w_out, ffn2_norm, ffn2_w_gu, ffn2_w_down):
    b, s, d = x.shape
    h = x.reshape(b * s, d)
    w1_gu, w1_down = ffn1_w_gu.astype(BF16), ffn1_w_down.astype(BF16)
    w2_gu, w2_down = ffn2_w_gu.astype(BF16), ffn2_w_down.astype(BF16)
    for l in range(ffn1_norm.shape[0]):
        h = _ffn(h, ffn1_norm[l], w1_gu, w1_down, l)
        merged = _token_mixing(h, b, s, mix_norm[l], w_in[l], qk_gain[l], sinks[l], cmp_pos[l], cmp_w1[l],
                               cmp_w2[l], w_branch_a[l], w_branch_b[l], w_branch_c[l])
        h = _out_proj(h, merged, w_out[l].astype(BF16))
        h = _ffn(h, ffn2_norm[l], w2_gu, w2_down, l)
    return h.reshape(b, s, d)
```

```python
import functools
import math

import numpy as np
import jax
import jax.numpy as jnp
from jax import lax
from jax.experimental import pallas as pl
from jax.experimental.pallas import tpu as pltpu

F32 = jnp.float32
BF16 = jnp.bfloat16

HEAD_DIM = 64
Q_BLOCK = 128
LANES = 128
A_GROUPS = ((128, 1), (512, 4), (2048, 16))
A_HEADS_PER_GROUP = 4
A_HEADS = 12
A_OUT = A_HEADS_PER_GROUP * HEAD_DIM
B_HEADS = 8
B_KV_HEADS = 2
B_WINDOW = 128
C_HEADS = 12
C_KV_HEADS = 2
C_REP = C_HEADS // C_KV_HEADS
CMP_BLOCK = 32
CMP_STRIDE = 16
CMP_HIDDEN = 256
SLC_BLOCK = 64
SLC_TOPK = 16
C_WINDOW = 512
RMS_EPS = 1e-6
NEG_INF = -1e30
SEC = 768
N_SEC = 6
QKV_COLS = SEC * N_SEC
TM = 512
CHUNK16 = Q_BLOCK * 16
VMEM_LIMIT = 56 * 1024 * 1024


def _slopes(n):
    return [float(2.0 ** (-8.0 * (h + 1) / n)) for h in range(n)]


def _cparams(sem):
    return pltpu.CompilerParams(dimension_semantics=sem, vmem_limit_bytes=VMEM_LIMIT)


def _dot(a, b):
    return jnp.dot(a, b, preferred_element_type=F32)


def _nt_dot(a, b):
    return lax.dot_general(a, b, (((1,), (1,)), ((), ())), preferred_element_type=F32)


def _split(v):
    hi = v.astype(BF16)
    return hi, (v - hi.astype(F32)).astype(BF16)


def _resident(shape):
    return pl.BlockSpec(shape, lambda *_: (0,) * len(shape), pipeline_mode=pl.Buffered(1))


def _rms_rows(x, g):
    ms = jnp.mean(x * x, axis=-1, keepdims=True)
    return (x * lax.rsqrt(ms + RMS_EPS) * g).astype(BF16)


def _deinterleave(n, d):
    p = np.zeros((n, n), np.float32)
    r = np.arange(n // d)
    for c in range(d):
        p[c * (n // d) + r, d * r + c] = 1.0
    return p


AUG_POS = 0
AUG_ROWS = 8
MASK_BIG = 1e30
SLC_GROUP = 5


def _key_pattern(n):
    pat = np.zeros((n, LANES), np.float32)
    pat[:, AUG_POS] = pat[:, AUG_POS + 1] = np.arange(n) % Q_BLOCK
    return pat


def _block_spread(n_tiles):
    m = np.zeros((n_tiles * AUG_ROWS, 2 * n_tiles), np.float32)
    jj = np.arange(n_tiles)
    for e in range(2):
        m[AUG_ROWS * jj + e, 2 * jj + e] = 1.0
    return m


def _ffn_body(x_ref, g_ref, wg_ref, wu_ref, wd_ref, o_ref, h_ref, *, n_f):
    f = pl.program_id(1)

    @pl.when(f == 0)
    def _():
        h_ref[...] = _rms_rows(x_ref[...], g_ref[...])
        o_ref[...] = jnp.zeros_like(o_ref)

    h = h_ref[...]
    gate = _dot(h, wg_ref[...])
    up = _dot(h, wu_ref[...])
    act = (gate * jax.nn.sigmoid(gate) * up).astype(BF16)
    o_ref[...] += _dot(act, wd_ref[...])

    @pl.when(f == n_f - 1)
    def _():
        o_ref[...] = x_ref[...] + 0.5 * o_ref[...]


def _ffn(x, g, w_gu, w_down, layer, *, tm=1024, tf=512):
    t, d = x.shape
    d_ff = w_down.shape[1]
    n_f = d_ff // tf
    rows = pl.BlockSpec((tm, d), lambda i, f: (i, 0))
    return pl.pallas_call(
        functools.partial(_ffn_body, n_f=n_f),
        grid=(t // tm, n_f),
        in_specs=[
            rows,
            pl.BlockSpec((1, d), lambda i, f: (0, 0)),
            pl.BlockSpec((None, d, tf), lambda i, f: (layer, 0, f)),
            pl.BlockSpec((None, d, tf), lambda i, f: (layer, 0, f + n_f)),
            pl.BlockSpec((None, tf, d), lambda i, f: (layer, f, 0)),
        ],
        out_specs=rows,
        out_shape=jax.ShapeDtypeStruct((t, d), F32),
        scratch_shapes=[pltpu.VMEM((tm, d), BF16)],
        compiler_params=_cparams(("parallel", "arbitrary")),
    )(x, g.reshape(1, d), w_gu, w_gu, w_down)


NORM_TILE = 256
NORM_TILES = {0: (0, 1), 1: (0, 1), 2: (0, 1), 3: (0, 1, 2), 4: (0, 1, 2), 5: (1, 2)}


def _head_sumsq(y, bd):
    return _dot((y * y).astype(BF16), bd)


def _qkv_body(x_ref, g_ref, w_ref, flag_ref, gs_ref, bd_ref, p4_ref, p16_ref, eye_ref, kpat_ref,
              a0_ref, a1_ref, a2_ref, b_ref, cq_ref, ckv_ref, cmpd_ref, qt_ref, ks_ref, vst_ref, cg_ref):
    h = _rms_rows(x_ref[...], g_ref[...])
    bd = bd_ref[...]

    def section(k):
        sl = slice(k * SEC, (k + 1) * SEC)
        if k < len(A_GROUPS):
            y = jnp.concatenate([_dot(h, w_ref[:, part * A_HEADS * HEAD_DIM + k * A_OUT:
                                                  part * A_HEADS * HEAD_DIM + (k + 1) * A_OUT])
                                 for part in range(3)], axis=1)
        else:
            y = _dot(h, w_ref[:, sl])
        tiles = []
        for c in range(SEC // NORM_TILE):
            yc = y[:, c * NORM_TILE:(c + 1) * NORM_TILE]
            if c in NORM_TILES[k]:
                cols = slice(k * SEC + c * NORM_TILE, k * SEC + (c + 1) * NORM_TILE)
                inv = lax.rsqrt(_head_sumsq(yc, bd) * (1.0 / HEAD_DIM) + RMS_EPS)
                yc = yc * jnp.where(flag_ref[:, cols] > 0, inv, 1.0) * gs_ref[:, cols]
            tiles.append(yc.astype(BF16))
        return jnp.concatenate(tiles, axis=1)

    a0_ref[...] = section(0)
    a1_ref[...] = _dot(p4_ref[...], section(1)).astype(BF16).reshape(a1_ref.shape)
    a2_ref[...] = _dot(p16_ref[...], section(2)).astype(BF16).reshape(a2_ref.shape)
    b_ref[...] = section(3)
    y_cq = section(4)
    cq_ref[...] = y_cq
    qt_ref[...] = _nt_dot(eye_ref[...], y_cq).astype(BF16)
    y_ckv = section(5)
    ckv_ref[...] = y_ckv
    cmpd_ref[...] = _dot(p16_ref[...], y_ckv[:, 0:2 * LANES]).astype(BF16).reshape(cmpd_ref.shape)
    ks_ref[:, 0:LANES] = y_ckv[:, 2 * LANES:3 * LANES]
    ks_ref[:, LANES:2 * LANES] = kpat_ref[...]
    eye = eye_ref[0:LANES, 0:LANES]
    for kb in range(vst_ref.shape[0]):
        vt = _nt_dot(eye, y_ckv[kb * Q_BLOCK:(kb + 1) * Q_BLOCK, 3 * LANES:4 * LANES]).astype(BF16)
        for gg in range(C_KV_HEADS):
            vst_ref[kb, gg] = vt[gg * HEAD_DIM:(gg + 1) * HEAD_DIM]
    cg_ref[...] = jax.nn.sigmoid(_dot(h, w_ref[:, QKV_COLS:QKV_COLS + LANES]))


def _qkv_proj(x, g, w, flag, gs, b, s):
    t, d = x.shape
    tiles_per_batch = s // TM
    per16 = CHUNK16 // TM
    bd = jnp.asarray(np.kron(np.eye(NORM_TILE // HEAD_DIM), np.ones((HEAD_DIM, HEAD_DIM))), BF16)
    p4 = jnp.asarray(_deinterleave(TM, 4), BF16)
    p16 = jnp.asarray(_deinterleave(TM, 16), BF16)
    eye = jnp.asarray(np.eye(SEC), BF16)
    nat = pl.BlockSpec((TM, SEC), lambda i: (i, 0))
    out_specs = [
        nat,
        pl.BlockSpec((None, 4, Q_BLOCK, SEC), lambda i: (i, 0, 0, 0)),
        pl.BlockSpec((None, 16, TM // 16, SEC), lambda i: (i // per16, 0, i % per16, 0)),
        nat, nat, nat,
        pl.BlockSpec((None, 16, TM // 16, 2 * LANES), lambda i: (i // tiles_per_batch, 0, i % tiles_per_batch, 0)),
        pl.BlockSpec((None, SEC, TM), lambda i: (i // tiles_per_batch, 0, i % tiles_per_batch)),
        pl.BlockSpec((TM, 2 * LANES), lambda i: (i, 0)),
        pl.BlockSpec((TM // Q_BLOCK, C_KV_HEADS, HEAD_DIM, Q_BLOCK), lambda i: (i, 0, 0, 0)),
        pl.BlockSpec((TM, LANES), lambda i: (i, 0)),
    ]
    out_shape = [
        jax.ShapeDtypeStruct((t, SEC), BF16),
        jax.ShapeDtypeStruct((t // TM, 4, Q_BLOCK, SEC), BF16),
        jax.ShapeDtypeStruct((t // CHUNK16, 16, Q_BLOCK, SEC), BF16),
        jax.ShapeDtypeStruct((t, SEC), BF16),
        jax.ShapeDtypeStruct((t, SEC), BF16),
        jax.ShapeDtypeStruct((t, SEC), BF16),
        jax.ShapeDtypeStruct((b, 16, s // 16, 2 * LANES), BF16),
        jax.ShapeDtypeStruct((b, SEC, s), BF16),
        jax.ShapeDtypeStruct((t, 2 * LANES), BF16),
        jax.ShapeDtypeStruct((t // Q_BLOCK, C_KV_HEADS, HEAD_DIM, Q_BLOCK), BF16),
        jax.ShapeDtypeStruct((t, LANES), F32),
    ]
    n_w = w.shape[1]
    return pl.pallas_call(
        _qkv_body,
        grid=(t // TM,),
        in_specs=[
            pl.BlockSpec((TM, d), lambda i: (i, 0)),
            _resident((1, d)),
            _resident((d, n_w)),
            _resident((1, QKV_COLS)),
            _resident((1, QKV_COLS)),
            _resident((NORM_TILE, NORM_TILE)),
            _resident((TM, TM)),
            _resident((TM, TM)),
            _resident((SEC, SEC)),
            _resident((TM, LANES)),
        ],
        out_specs=out_specs,
        out_shape=out_shape,
        compiler_params=_cparams(("parallel",)),
    )(x, g.reshape(1, d), w, flag.reshape(1, -1), gs.reshape(1, -1), bd, p4, p16, eye,
      jnp.asarray(_key_pattern(TM), BF16))


def _banded_body(*refs, nb, heads, k_off, v_off, n_pairs, q_axis, use_sinks, with_lse, stack, fold):
    refs = list(refs)
    q_ref = refs.pop(0)
    kv_refs = [refs.pop(0) for _ in range(nb + 1)]
    qc_ref, kaug_ref, band_ref = refs.pop(0), refs.pop(0), refs.pop(0)
    sink_ref = refs.pop(0) if use_sinks else None
    o_ref = refs.pop(0)
    lse_ref = refs.pop(0) if with_lse else None

    i = pl.program_id(q_axis)
    nk = (nb + 1) * Q_BLOCK
    col = lax.broadcasted_iota(jnp.int32, (Q_BLOCK, nk), 1)
    mask = band_ref[...] + jnp.where(col < (nb - i) * Q_BLOCK, -MASK_BIG, 0.0)
    rel_f = (nb * Q_BLOCK + lax.broadcasted_iota(jnp.int32, (Q_BLOCK, nk), 0) - col).astype(F32)
    lane = lax.broadcasted_iota(jnp.int32, (Q_BLOCK, LANES), 1)
    low_half = lane < HEAD_DIM
    kaug = kaug_ref[...]

    kv_cache = {}

    def kv_tile(off, kv_pair, swapped):
        key = (off, kv_pair, swapped)
        if key not in kv_cache:
            c0 = off + kv_pair * LANES
            tile = jnp.concatenate([r[:, c0:c0 + LANES] for r in kv_refs], axis=0)
            if swapped:
                tile = pltpu.roll(tile.astype(F32), HEAD_DIM, 1).astype(BF16)
            kv_cache[key] = tile
        return kv_cache[key]

    classes = {}
    for head in heads:
        pair, half, kv_pair, kv_half, slope, hidx = head
        key = (kv_pair, kv_half != half) if stack else hidx
        classes.setdefault(key, []).append(head)

    outs = [[None, None] for _ in range(n_pairs)]
    lses = [[None, None] for _ in range(n_pairs)]
    groups = list(classes.values())

    scores = []
    for members in groups:
        n_h = len(members)
        kv_pair, swapped = members[0][2], members[0][3] != members[0][1]
        q_rows = []
        for pair, half, _, _, _, hidx in members:
            qp = q_ref[:, pair * LANES:(pair + 1) * LANES]
            own = low_half if half == 0 else jnp.logical_not(low_half)
            qm = jnp.where(own, qp, jnp.zeros_like(qp))
            q_rows.append(jnp.concatenate([qm, qc_ref[hidx]], axis=1) if fold else qm)
        if fold:
            k_aug = jnp.concatenate([kv_tile(k_off, kv_pair, swapped), kaug], axis=1)
            s = _nt_dot(jnp.concatenate(q_rows, axis=0), k_aug)
            s = (s.reshape(n_h, Q_BLOCK, nk) + mask[None]).reshape(n_h * Q_BLOCK, nk)
        else:
            bias = jnp.concatenate([mask - member[4] * rel_f for member in members], axis=0)
            s = _nt_dot(jnp.concatenate(q_rows, axis=0), kv_tile(k_off, kv_pair, swapped)) + bias
        scores.append(s)

    probs = []
    for members, s in zip(groups, scores):
        m = jnp.max(s, axis=1, keepdims=True)
        if use_sinks:
            assert len(members) == 1
            sink = sink_ref[members[0][5]]
            m = jnp.maximum(m, sink)
        p = jnp.exp(s - m)
        den = jnp.sum(p, axis=1, keepdims=True)
        if use_sinks:
            den = den + jnp.exp(sink - m)
        probs.append((p.astype(BF16), m, den))

    for members, (p, m, den) in zip(groups, probs):
        kv_pair, swapped = members[0][2], members[0][3] != members[0][1]
        r = _dot(p, kv_tile(v_off, kv_pair, swapped)) / den
        lse = m + jnp.log(den) if with_lse else None
        for k, (pair, half, _, _, _, _) in enumerate(members):
            outs[pair][half] = r[k * Q_BLOCK:(k + 1) * Q_BLOCK]
            if with_lse:
                lses[pair][half] = jnp.broadcast_to(lse[k * Q_BLOCK:(k + 1) * Q_BLOCK], (Q_BLOCK, LANES))
    for pair in range(n_pairs):
        sl = slice(pair * LANES, (pair + 1) * LANES)
        o_ref[:, sl] = jnp.where(low_half, outs[pair][0], outs[pair][1])
        if with_lse:
            lse_ref[:, sl] = jnp.where(low_half, lses[pair][0], lses[pair][1])


def _banded_consts(heads, nb, max_dist):
    nk = (nb + 1) * Q_BLOCK
    slope = np.asarray([h[4] for h in heads], np.float32)[:, None]
    q_dist = (nb * Q_BLOCK + np.arange(Q_BLOCK, dtype=np.float32))[None, :]
    ones = np.ones_like(q_dist)
    vals = jnp.asarray(np.stack([slope * ones, slope * Q_BLOCK * ones, -slope * q_dist], axis=-1))
    hi = vals.astype(BF16)
    lo = (vals - hi.astype(F32)).astype(BF16)
    cols = jnp.stack([hi[..., 0], lo[..., 0], hi[..., 1], lo[..., 1], hi[..., 2], lo[..., 2]], axis=-1)
    qc = jnp.pad(cols, ((0, 0), (0, 0), (0, LANES - cols.shape[-1])))
    kaug = np.zeros((nk, LANES), np.float32)
    kaug[:, 0] = kaug[:, 1] = np.arange(nk) % Q_BLOCK
    kaug[:, 2] = kaug[:, 3] = np.arange(nk) // Q_BLOCK
    kaug[:, 4] = kaug[:, 5] = 1.0
    rel = nb * Q_BLOCK + np.arange(Q_BLOCK)[:, None] - np.arange(nk)[None, :]
    band = np.where((rel >= 0) & (rel <= max_dist), 0.0, -MASK_BIG).astype(np.float32)
    return [qc, jnp.asarray(kaug, BF16), jnp.asarray(band)]


def _banded_call(q_arr, kv_arr, *, grid, q_map, kv_map, out_map, out_lead, out_cols, nb, max_dist, heads,
                 k_off, v_off, q_axis, sinks=None, with_lse=False, stack=False, fold=False):
    lead = (None,) * (q_arr.ndim - 2)
    blk = lead + (Q_BLOCK, SEC)
    in_specs = [pl.BlockSpec(blk, q_map)]
    args = [q_arr]
    for back in range(nb, -1, -1):
        in_specs.append(pl.BlockSpec(blk, functools.partial(kv_map, back=back)))
        args.append(kv_arr)
    consts = _banded_consts(heads, nb, max_dist)
    in_specs += [pl.BlockSpec(c.shape, lambda *_, nd=c.ndim: (0,) * nd) for c in consts]
    args += consts
    if sinks is not None:
        in_specs.append(pl.BlockSpec(memory_space=pltpu.SMEM))
        args.append(sinks)
    oblk = pl.BlockSpec(lead + (Q_BLOCK, out_cols), out_map)
    oshape = jax.ShapeDtypeStruct(out_lead + (out_cols,), F32)
    body = functools.partial(_banded_body, nb=nb, heads=heads, k_off=k_off, v_off=v_off,
                             n_pairs=out_cols // LANES, q_axis=q_axis, use_sinks=sinks is not None,
                             with_lse=with_lse, stack=stack, fold=fold)
    return pl.pallas_call(
        body,
        grid=grid,
        in_specs=in_specs,
        out_specs=[oblk, oblk] if with_lse else oblk,
        out_shape=[oshape, oshape] if with_lse else oshape,
        compiler_params=_cparams(("parallel",) * len(grid)),
    )(*args)


def _dilated_group(arr, gi, b, s):
    window, dil = A_GROUPS[gi]
    slopes = _slopes(A_HEADS)
    heads = tuple((hh // 2, hh % 2, hh // 2, hh % 2, slopes[gi * A_HEADS_PER_GROUP + hh] * dil, hh)
                  for hh in range(A_HEADS_PER_GROUP))
    common = dict(out_cols=A_OUT, nb=1, max_dist=window // dil, heads=heads, k_off=256, v_off=512,
                  with_lse=True, stack=False)
    if dil == 1:
        return _banded_call(
            arr, arr, grid=(b, s // Q_BLOCK), q_axis=1,
            q_map=lambda bb, i: (bb, i, 0),
            kv_map=lambda bb, i, back: (bb, jnp.maximum(i - back, 0), 0),
            out_map=lambda bb, i: (bb, i, 0), out_lead=(b, s), **common)
    nc = s // (Q_BLOCK * dil)
    return _banded_call(
        arr, arr, grid=(b, dil, nc), q_axis=2,
        q_map=lambda bb, c, i: (bb * nc + i, c, 0, 0),
        kv_map=lambda bb, c, i, back: (bb * nc + jnp.maximum(i - back, 0), c, 0, 0),
        out_map=lambda bb, c, i: (bb * nc + i, c, 0, 0), out_lead=(b * nc, dil, Q_BLOCK), **common)


def _sink_swa(arr, sinks, b, s):
    slopes = _slopes(B_HEADS)
    rep = B_HEADS // B_KV_HEADS
    heads = tuple((h // 2, h % 2, 0, h // rep, slopes[h], h) for h in range(B_HEADS))
    return _banded_call(
        arr, arr, grid=(b, s // Q_BLOCK), q_axis=1,
        q_map=lambda bb, i: (bb, i, 0),
        kv_map=lambda bb, i, back: (bb, jnp.maximum(i - back, 0), 0),
        out_map=lambda bb, i: (bb, i, 0), out_lead=(b, s), out_cols=B_HEADS * HEAD_DIM,
        nb=1, max_dist=B_WINDOW - 1, heads=heads, k_off=512, v_off=640, sinks=sinks)


def _nsa_window(cq, ckv, b, s):
    slopes = _slopes(C_HEADS)
    heads = tuple((h // 2, h % 2, 0, h // C_REP, slopes[h], h) for h in range(C_HEADS))
    return _banded_call(
        cq, ckv, grid=(b, s // Q_BLOCK), q_axis=1,
        q_map=lambda bb, i: (bb, i, 0),
        kv_map=lambda bb, i, back: (bb, jnp.maximum(i - back, 0), 0),
        out_map=lambda bb, i: (bb, i, 0), out_lead=(b, s), out_cols=C_HEADS * HEAD_DIM,
        nb=-(-(C_WINDOW - 1) // Q_BLOCK), max_dist=C_WINDOW - 1, heads=heads, k_off=512, v_off=640, stack=True, fold=True)


def _compress_body(t_ref, wb_ref, prow_ref, w2k_ref, w2vt_ref, kg_ref, bd_ref, kc_ref, vct_ref, *, n_chunks):
    hid_cols = 2 * C_KV_HEADS * CMP_HIDDEN
    u = jnp.zeros((n_chunks, hid_cols), F32)
    v = jnp.zeros((n_chunks, hid_cols), F32)
    pc = jnp.zeros((1, hid_cols), F32)
    for c in range(CMP_STRIDE):
        tc = t_ref[c]
        u = u + _dot(tc, wb_ref[0, c])
        v = v + _dot(tc, wb_ref[1, c])
        pc = pc + _dot(prow_ref[0, c], wb_ref[0, c])[0:1] + _dot(prow_ref[1, c], wb_ref[1, c])[0:1]
    hsum = u + pltpu.roll(v, n_chunks - 1, 0) + pc
    hid = (hsum * jax.nn.sigmoid(hsum)).astype(BF16)
    half = C_KV_HEADS * CMP_HIDDEN
    k = _dot(hid[:, :half], w2k_ref[...])
    hi, lo = _split(k * k)
    ss = _dot(hi, bd_ref[...]) + _dot(lo, bd_ref[...])
    kc_ref[...] = (k * lax.rsqrt(ss * (1.0 / HEAD_DIM) + RMS_EPS) * kg_ref[...]).astype(BF16)
    vct_ref[...] = _nt_dot(w2vt_ref[...], hid[:, half:]).astype(BF16)


def _compress(cmpd, wb, prow, w2k, w2vt, kg):
    b, _, n_chunks, width = cmpd.shape
    bd = jnp.asarray(np.kron(np.eye(LANES // HEAD_DIM), np.ones((HEAD_DIM, HEAD_DIM))), BF16)
    return pl.pallas_call(
        functools.partial(_compress_body, n_chunks=n_chunks),
        grid=(b,),
        in_specs=[
            pl.BlockSpec((None, CMP_STRIDE, n_chunks, width), lambda bb: (bb, 0, 0, 0)),
            _resident(wb.shape), _resident(prow.shape), _resident(w2k.shape), _resident(w2vt.shape),
            _resident((1, LANES)), _resident((LANES, LANES)),
        ],
        out_specs=[
            pl.BlockSpec((None, n_chunks, LANES), lambda bb: (bb, 0, 0)),
            pl.BlockSpec((None, LANES, n_chunks), lambda bb: (bb, 0, 0)),
        ],
        out_shape=[
            jax.ShapeDtypeStruct((b, n_chunks, LANES), BF16),
            jax.ShapeDtypeStruct((b, LANES, n_chunks), BF16),
        ],
        compiler_params=_cparams(("parallel",)),
    )(cmpd, wb, prow, w2k, w2vt, kg, bd)


def _to_natural(ot_list, eye):
    pairs = []
    for k in range(0, len(ot_list), 2):
        hi, lo = _split(jnp.concatenate([ot_list[k], ot_list[k + 1]], axis=0))
        pairs.append(_nt_dot(eye, hi) + _nt_dot(eye, lo))
    return jnp.concatenate(pairs, axis=1)


def _cmp_body(qt_ref, kc_ref, vct_ref, ovt_ref, eye_ref, spread_ref, cmask_ref,
              o_ref, selm_ref, cnt_ref, sel_sc, *, n_top):
    g = pl.program_id(1)
    i = pl.program_id(2)
    n_pad = kc_ref.shape[0]
    n_slc = ovt_ref.shape[0]
    per_q = Q_BLOCK // CMP_STRIDE
    own_rows = (lax.broadcasted_iota(jnp.int32, (LANES, Q_BLOCK), 0) // HEAD_DIM) == g
    slopes = _slopes(C_HEADS)

    def attend(rows):
        kc = kc_ref[0:rows, :]
        vct = vct_ref[:, 0:rows]
        mask = cmask_ref[pl.ds(pl.multiple_of(n_pad - per_q * i, 8), rows), :]
        n_f = (CMP_STRIDE * lax.broadcasted_iota(jnp.int32, (rows, Q_BLOCK), 0)).astype(F32)
        psum = jnp.zeros((rows, Q_BLOCK), F32)
        outs = []
        for r in range(C_REP):
            slope = jnp.where(g == 0, slopes[r], slopes[C_REP + r])
            qt = qt_ref[r * HEAD_DIM:(r + 1) * HEAD_DIM, :]
            q_pad = jnp.where(own_rows, jnp.concatenate([qt, qt], axis=0), jnp.zeros((LANES, Q_BLOCK), BF16))
            s = _dot(kc, q_pad) + (slope * n_f + mask)
            m = jnp.maximum(jnp.max(s, axis=0, keepdims=True), -1e20)
            e = jnp.exp(s - m)
            den = jnp.sum(e, axis=0, keepdims=True)
            p = e * (1.0 / jnp.where(den > 0, den, 1.0))
            psum = psum + p
            both = _dot(vct, p.astype(BF16))
            outs.append(jnp.where(g == 0, both[:HEAD_DIM], both[HEAD_DIM:]))
        o_ref[...] = _to_natural(outs, eye_ref[...])

        n_j = rows * CMP_STRIDE // SLC_BLOCK
        hi, lo = _split(psum)
        ovt = ovt_ref[0:n_j, 0:rows]
        imp = _dot(ovt, hi) + _dot(ovt, lo)
        j_idx = lax.broadcasted_iota(jnp.int32, (n_j, Q_BLOCK), 0)
        t_q = i * Q_BLOCK + lax.broadcasted_iota(jnp.int32, (n_j, Q_BLOCK), 1)
        cur = lax.shift_right_logical(t_q, int(math.log2(SLC_BLOCK)))
        forced = ((j_idx == 0) | (j_idx == cur) | (j_idx == cur - 1)) & (j_idx <= cur)
        v = jnp.where((j_idx <= cur) & jnp.logical_not(forced), imp, -1.0)
        sel = jnp.where(forced, 1.0, 0.0)
        for _ in range(n_top - 3):
            m = jnp.max(v, axis=0, keepdims=True)
            first = jnp.min(jnp.where((v == m) & (m >= 0.0), j_idx, n_slc), axis=0, keepdims=True)
            pick = j_idx == first
            sel = jnp.where(pick, 1.0, sel)
            v = jnp.where(pick, -1.0, v)
        sel_sc[0:n_j, :] = sel
        if n_j < n_slc:
            sel_sc[n_j:, :] = jnp.zeros((n_slc - n_j, Q_BLOCK), F32)

    n_var = n_pad // LANES
    for var in range(n_var):
        pl.when(i // (LANES // per_q) == var)(functools.partial(attend, (var + 1) * LANES))

    sel = sel_sc[...]
    neg = jnp.where(sel > 0, 0.0, -MASK_BIG).astype(BF16)
    selm_ref[...] = _dot(spread_ref[...], neg)
    cnt_ref[...] = _nt_dot(jnp.ones((8, Q_BLOCK), BF16), sel.astype(BF16))


def _cmp_select(qt, kc, vct, ovt, *, n_cmp):
    b, _, s = qt.shape
    g = C_KV_HEADS
    n_pad = kc.shape[1]
    n_slc = ovt.shape[0]
    nq = s // Q_BLOCK
    rows = C_REP * HEAD_DIM
    eye = jnp.asarray(np.eye(LANES), BF16)
    n_top = min(SLC_TOPK, n_slc)
    assert n_top > 3 and n_cmp == n_pad - 1 and n_pad % LANES == 0
    n_rel = np.arange(-n_pad, n_pad)[:, None]
    cmask = np.where(CMP_STRIDE * n_rel + CMP_BLOCK - 1 <= np.arange(Q_BLOCK)[None, :], 0.0, -MASK_BIG)
    return pl.pallas_call(
        functools.partial(_cmp_body, n_top=n_top),
        grid=(b, g, nq),
        in_specs=[
            pl.BlockSpec((None, rows, Q_BLOCK), lambda bb, gg, i: (bb, gg, i)),
            pl.BlockSpec((None, n_pad, LANES), lambda bb, gg, i: (bb, 0, 0)),
            pl.BlockSpec((None, LANES, n_pad), lambda bb, gg, i: (bb, 0, 0)),
            pl.BlockSpec((n_slc, n_pad), lambda bb, gg, i: (0, 0)),
            pl.BlockSpec((LANES, LANES), lambda bb, gg, i: (0, 0)),
            pl.BlockSpec((nq * AUG_ROWS, n_slc), lambda bb, gg, i: (0, 0)),
            pl.BlockSpec((2 * n_pad, Q_BLOCK), lambda bb, gg, i: (0, 0)),
        ],
        out_specs=[
            pl.BlockSpec((Q_BLOCK, rows), lambda bb, gg, i: (bb * nq + i, gg)),
            pl.BlockSpec((None, None, None, nq * AUG_ROWS, Q_BLOCK), lambda bb, gg, i: (bb, gg, i, 0, 0)),
            pl.BlockSpec((None, None, None, 8, n_slc), lambda bb, gg, i: (bb, gg, i, 0, 0)),
        ],
        out_shape=[
            jax.ShapeDtypeStruct((b * s, g * rows), F32),
            jax.ShapeDtypeStruct((b, g, nq, nq * AUG_ROWS, Q_BLOCK), F32),
            jax.ShapeDtypeStruct((b, g, nq, 8, n_slc), F32),
        ],
        scratch_shapes=[pltpu.VMEM((n_slc, Q_BLOCK), F32)],
        compiler_params=_cparams(("parallel", "parallel", "parallel")),
    )(qt, kc, vct, ovt, eye, jnp.asarray(_block_spread(nq), BF16), jnp.asarray(cmask, F32))


def _slc_body(list_ref, qt_ref, ks_ref, vst_ref, selm_ref, eye_ref, o_ref, qaug, m_sc, l_sc, acc_sc, *, nq, stride):
    bb = pl.program_id(0)
    g = pl.program_id(1)
    i = pl.program_id(2)
    width = C_REP * Q_BLOCK
    slopes = _slopes(C_HEADS)
    slope_s = [jnp.where(g == 0, slopes[r], slopes[C_REP + r]) for r in range(C_REP)]

    own_rows = (lax.broadcasted_iota(jnp.int32, (LANES, width), 0) // HEAD_DIM) == g
    q6 = jnp.concatenate([qt_ref[r * HEAD_DIM:(r + 1) * HEAD_DIM, :] for r in range(C_REP)], axis=1)
    qaug[0:LANES, :] = jnp.where(own_rows, jnp.concatenate([q6, q6], axis=0), jnp.zeros((LANES, width), BF16))
    head = lax.broadcasted_iota(jnp.int32, (LANES, width), 1) // Q_BLOCK
    row = lax.broadcasted_iota(jnp.int32, (LANES, width), 0)
    slope_t = jnp.zeros((LANES, width), F32)
    for r in range(C_REP):
        slope_t = jnp.where(head == r, slope_s[r], slope_t)
    s_hi, s_lo = _split(slope_t)
    slope_rows = jnp.where(row == AUG_POS, s_hi.astype(F32), jnp.where(row == AUG_POS + 1, s_lo.astype(F32), 0.0))
    qaug[LANES:, :] = slope_rows.astype(BF16)

    q_loc = lax.broadcasted_iota(jnp.int32, (Q_BLOCK, Q_BLOCK), 1)
    k_loc = lax.broadcasted_iota(jnp.int32, (Q_BLOCK, Q_BLOCK), 0)
    causal = jnp.where(k_loc > q_loc, -MASK_BIG, 0.0)

    m_sc[...] = jnp.full(m_sc.shape, NEG_INF, F32)
    l_sc[...] = jnp.zeros(l_sc.shape, F32)
    acc_sc[...] = jnp.zeros(acc_sc.shape, F32)

    def scores(jj):
        k_tile = ks_ref[pl.ds(pl.multiple_of(jj * Q_BLOCK, Q_BLOCK), Q_BLOCK), :]
        rows = selm_ref[pl.ds(pl.multiple_of(jj * AUG_ROWS, AUG_ROWS), AUG_ROWS), :]
        mask = jnp.concatenate([jnp.broadcast_to(rows[e:e + 1], (SLC_BLOCK, Q_BLOCK)) for e in range(2)], axis=0)
        return _dot(k_tile, qaug[...]), mask

    def accumulate(tiles):
        ps = [[] for _ in tiles]
        alphas, heads_parts = [], []
        for r in range(C_REP):
            sl = slice(r * Q_BLOCK, (r + 1) * Q_BLOCK)
            m_old = m_sc[:, sl]
            m_new = m_old
            parts = []
            for jj, st, mask, extra in tiles:
                s = st[:, sl] + mask
                c = slope_s[r] * ((jj - i) * Q_BLOCK).astype(F32) + extra
                m_new = jnp.maximum(m_new, jnp.max(s, axis=0, keepdims=True) + c)
                parts.append((s, c))
            alphas.append(jnp.exp(m_old - m_new))
            m_sc[:, sl] = m_new
            heads_parts.append((parts, m_new))
        for r, (parts, m_new) in enumerate(heads_parts):
            sl = slice(r * Q_BLOCK, (r + 1) * Q_BLOCK)
            l_new = alphas[r] * l_sc[:, sl]
            for k, (s, c) in enumerate(parts):
                p = jnp.exp(s + (c - m_new))
                l_new = l_new + jnp.sum(p, axis=0, keepdims=True)
                ps[k].append(p.astype(BF16))
            l_sc[:, sl] = l_new
        pv = None
        for k, (jj, _, _, _) in enumerate(tiles):
            term = _dot(vst_ref[jj], jnp.concatenate(ps[k], axis=1))
            pv = term if pv is None else pv + term
        acc_sc[...] = jnp.concatenate(alphas, axis=1) * acc_sc[...] + pv

    base = ((bb * pl.num_programs(1) + g) * nq + i) * stride
    count = list_ref[base]

    def listed(slot):
        jj = list_ref[base + 1 + slot]
        return (jj, *scores(jj), jnp.where(slot < count, 0.0, -MASK_BIG))

    st, mask = scores(i)
    accumulate([(i, st, mask + causal, 0.0)] + [listed(u) for u in range(SLC_GROUP - 1)])

    def step(k, carry):
        accumulate([listed(SLC_GROUP - 1 + SLC_GROUP * k + u) for u in range(SLC_GROUP)])
        return carry

    rest = jnp.maximum(count - (SLC_GROUP - 1), 0)
    lax.fori_loop(0, (rest + SLC_GROUP - 1) // SLC_GROUP, step, 0)
    l = l_sc[...]
    o = acc_sc[...] / jnp.where(l > 0, l, 1.0)
    o_ref[...] = _to_natural([o[:, r * Q_BLOCK:(r + 1) * Q_BLOCK] for r in range(C_REP)], eye_ref[...])


def _slc_attention(lists, qt, ks, vst, sel, *, stride):
    b, _, s = qt.shape
    g = C_KV_HEADS
    nq = s // Q_BLOCK
    rows = C_REP * HEAD_DIM
    width = C_REP * Q_BLOCK
    eye = jnp.asarray(np.eye(LANES), BF16)
    grid_spec = pltpu.PrefetchScalarGridSpec(
        num_scalar_prefetch=1,
        grid=(b, g, nq),
        in_specs=[
            pl.BlockSpec((None, rows, Q_BLOCK), lambda bb, gg, i, bits: (bb, gg, i)),
            pl.BlockSpec((None, s, 2 * LANES), lambda bb, gg, i, bits: (bb, 0, 0)),
            pl.BlockSpec((None, nq, None, HEAD_DIM, Q_BLOCK), lambda bb, gg, i, bits: (bb, 0, gg, 0, 0)),
            pl.BlockSpec((None, None, None, nq * AUG_ROWS, Q_BLOCK), lambda bb, gg, i, bits: (bb, gg, i, 0, 0)),
            pl.BlockSpec((LANES, LANES), lambda bb, gg, i, bits: (0, 0)),
        ],
        out_specs=pl.BlockSpec((Q_BLOCK, rows), lambda bb, gg, i, bits: (bb * nq + i, gg)),
        scratch_shapes=[
            pltpu.VMEM((2 * LANES, width), BF16),
            pltpu.VMEM((1, width), F32),
            pltpu.VMEM((1, width), F32),
            pltpu.VMEM((HEAD_DIM, width), F32),
        ],
    )
    return pl.pallas_call(
        functools.partial(_slc_body, nq=nq, stride=stride),
        grid_spec=grid_spec,
        out_shape=jax.ShapeDtypeStruct((b * s, g * rows), F32),
        compiler_params=_cparams(("parallel", "parallel", "parallel")),
    )(lists, qt, ks, vst, sel, eye)


def _merge_body(x_ref, g_ref, oa0, la0, oa1, la1, oa2, la2, ob_ref, ocmp_ref, oslc_ref, owin_ref, cg_ref,
                p4t_ref, p16t_ref, ex_ref, wg0_ref, wg1_ref, wg2_ref, wa_ref, wb_ref, wc_ref,
                out_ref, h_ref, oall_ref):
    @pl.when(pl.program_id(1) == 0)
    def _():
        h_ref[...] = _rms_rows(x_ref[...], g_ref[...])

        def natural(ref, pt_ref):
            hi, lo = _split(ref[...].reshape(TM, A_OUT))
            return _dot(pt_ref[...], hi) + _dot(pt_ref[...], lo)

        o0, l0 = oa0[...], la0[...]
        o1, l1 = natural(oa1, p4t_ref), natural(la1, p4t_ref)
        o2, l2 = natural(oa2, p16t_ref), natural(la2, p16t_ref)
        mx = jnp.maximum(jnp.maximum(l0, l1), l2)
        e0, e1, e2 = jnp.exp(l0 - mx), jnp.exp(l1 - mx), jnp.exp(l2 - mx)
        oall_ref[:, 0:A_OUT] = ((e0 * o0 + e1 * o1 + e2 * o2) / (e0 + e1 + e2)).astype(BF16)
        oall_ref[:, A_OUT:A_OUT + B_HEADS * HEAD_DIM] = ob_ref[...].astype(BF16)
        cg_split = jnp.concatenate(_split(cg_ref[...]), axis=1)
        o_c = None
        for w, o_ref in enumerate((ocmp_ref, oslc_ref, owin_ref)):
            term = _dot(cg_split, ex_ref[w]) * o_ref[...]
            o_c = term if o_c is None else o_c + term
        oall_ref[:, A_OUT + B_HEADS * HEAD_DIM:] = o_c.astype(BF16)

    h = h_ref[...]
    c0, c1 = A_OUT, A_OUT + B_HEADS * HEAD_DIM
    merged = jax.nn.sigmoid(_dot(h, wg0_ref[...])) * _dot(oall_ref[:, 0:c0], wa_ref[...])
    merged += jax.nn.sigmoid(_dot(h, wg1_ref[...])) * _dot(oall_ref[:, c0:c1], wb_ref[...])
    merged += jax.nn.sigmoid(_dot(h, wg2_ref[...])) * _dot(oall_ref[:, c1:], wc_ref[...])
    out_ref[...] = merged.astype(BF16)


def _merge(x, g, a_outs, ob, ocmp, oslc, owin, cg, w_gate, wa, wb, wc, ex, *, tn=512):
    t, d = x.shape
    per16 = CHUNK16 // TM
    n_t = d // tn
    (oa0, la0), (oa1, la1), (oa2, la2) = a_outs

    def rows(a):
        return pl.BlockSpec((TM, a.shape[1]), lambda i, n: (i, 0))

    a1_spec = pl.BlockSpec((None, 4, Q_BLOCK, A_OUT), lambda i, n: (i, 0, 0, 0))
    a2_spec = pl.BlockSpec((None, 16, TM // 16, A_OUT), lambda i, n: (i // per16, 0, i % per16, 0))
    p4t = jnp.asarray(_deinterleave(TM, 4).T, BF16)
    p16t = jnp.asarray(_deinterleave(TM, 16).T, BF16)
    in_specs = [
        rows(x), _resident((1, d)),
        rows(oa0), rows(la0), a1_spec, a1_spec, a2_spec, a2_spec,
        rows(ob), rows(ocmp), rows(oslc), rows(owin), rows(cg),
        _resident((TM, TM)), _resident((TM, TM)), _resident(ex.shape),
        pl.BlockSpec((d, tn), lambda i, n: (0, n)),
        pl.BlockSpec((d, tn), lambda i, n: (0, n + n_t)),
        pl.BlockSpec((d, tn), lambda i, n: (0, n + 2 * n_t)),
        pl.BlockSpec((wa.shape[0], tn), lambda i, n: (0, n)),
        pl.BlockSpec((wb.shape[0], tn), lambda i, n: (0, n)),
        pl.BlockSpec((wc.shape[0], tn), lambda i, n: (0, n)),
    ]
    return pl.pallas_call(
        _merge_body,
        grid=(t // TM, n_t),
        in_specs=in_specs,
        out_specs=pl.BlockSpec((TM, tn), lambda i, n: (i, n)),
        out_shape=jax.ShapeDtypeStruct((t, d), BF16),
        scratch_shapes=[pltpu.VMEM((TM, d), BF16), pltpu.VMEM((TM, wa.shape[0] + wb.shape[0] + wc.shape[0]), BF16)],
        compiler_params=_cparams(("parallel", "arbitrary")),
    )(x, g.reshape(1, d), oa0, la0, oa1, la1, oa2, la2, ob, ocmp, oslc, owin, cg, p4t, p16t, ex,
      w_gate, w_gate, w_gate, wa, wb, wc)


def _out_body(x_ref, m_ref, w_ref, o_ref):
    o_ref[...] = x_ref[...] + _dot(m_ref[...], w_ref[...])


def _out_proj(x, merged, w_out):
    t, d = x.shape
    rows = pl.BlockSpec((TM, d), lambda i: (i, 0))
    return pl.pallas_call(
        _out_body,
        grid=(t // TM,),
        in_specs=[rows, rows, _resident((d, d))],
        out_specs=rows,
        out_shape=jax.ShapeDtypeStruct((t, d), F32),
        compiler_params=_cparams(("parallel",)),
    )(x, merged, w_out)


def _qkv_column_params(qk_gain):
    flag, gain, scale = [], [], []
    one = jnp.ones((HEAD_DIM,), F32)

    def add(n_heads, normed, is_q, gvec):
        for _ in range(n_heads):
            flag.append(np.full((HEAD_DIM,), 1.0 if normed else 0.0, np.float32))
            gain.append(gvec if normed else one)
            scale.append(np.full((HEAD_DIM,), HEAD_DIM ** -0.5 if is_q else 1.0, np.float32))

    for _ in range(len(A_GROUPS)):
        add(A_HEADS_PER_GROUP, True, True, qk_gain[0, 0])
        add(A_HEADS_PER_GROUP, True, False, qk_gain[0, 1])
        add(A_HEADS_PER_GROUP, False, False, one)
    add(B_HEADS, True, True, qk_gain[1, 0])
    add(B_KV_HEADS, True, False, qk_gain[1, 1])
    add(B_KV_HEADS, False, False, one)
    add(C_HEADS, True, True, qk_gain[2, 0])
    for normed in (False, False, True, False, True, False):
        add(C_KV_HEADS, normed, False, qk_gain[2, 1])
    flag = np.concatenate(flag)
    assert flag.shape[0] == QKV_COLS
    return jnp.asarray(flag), jnp.concatenate(gain) * jnp.asarray(np.concatenate(scale))


def _overlap_t(n_slc, n_pad, n_cmp):
    n = np.arange(n_pad)[None, :]
    j = np.arange(n_slc)[:, None]
    start, end = CMP_STRIDE * n, CMP_STRIDE * n + CMP_BLOCK - 1
    ov = (start <= SLC_BLOCK * j + SLC_BLOCK - 1) & (end >= SLC_BLOCK * j) & (n < n_cmp)
    return jnp.asarray(ov, BF16)


def _gate_expand():
    ex = np.zeros((3, LANES, C_HEADS * HEAD_DIM), np.float32)
    for w in range(3):
        for h in range(C_HEADS):
            ex[w, h * 3 + w, h * HEAD_DIM:(h + 1) * HEAD_DIM] = 1.0
    return jnp.asarray(np.concatenate([ex, ex], axis=1), BF16)


def _compress_weights(cmp_pos, cmp_w1, cmp_w2):
    n_q = 2 * C_KV_HEADS
    w1 = cmp_w1.reshape(2, 2, CMP_STRIDE, HEAD_DIM, CMP_HIDDEN)
    w1q = jnp.repeat(w1, C_KV_HEADS, axis=0)
    wb = jnp.einsum("qhcdn,qp->hcqdpn", w1q, jnp.eye(n_q, dtype=F32))
    wb = wb.reshape(2, CMP_STRIDE, n_q * HEAD_DIM, n_q * CMP_HIDDEN).astype(BF16)
    pos = cmp_pos.reshape(2, 2, CMP_STRIDE, HEAD_DIM)
    prow = jnp.repeat(pos, C_KV_HEADS, axis=0).transpose(1, 2, 0, 3).reshape(2, CMP_STRIDE, 1, n_q * HEAD_DIM)
    prow = jnp.broadcast_to(prow, (2, CMP_STRIDE, 8, n_q * HEAD_DIM)).astype(BF16)
    eye_g = jnp.eye(C_KV_HEADS, dtype=F32)
    w2k = jnp.kron(eye_g, cmp_w2[0]).astype(BF16)
    w2vt = jnp.kron(eye_g, cmp_w2[1]).T.astype(BF16)
    return wb, prow, w2k, w2vt


def _token_mixing(x, b, s, mix_norm, w_in, qk_gain, sinks, cmp_pos, cmp_w1, cmp_w2, w_a, w_b, w_c):
    t, d = x.shape
    assert s % CHUNK16 == 0 and d % 512 == 0
    c_gate_cols = 3 * C_HEADS
    w_qkv = w_in[:, :QKV_COLS + LANES].astype(BF16)
    flag, gs = _qkv_column_params(qk_gain)
    a0, a1, a2, bsec, cq, ckv, cmpd, qt, ks, vst, cg = _qkv_proj(x, mix_norm, w_qkv, flag, gs, b, s)

    a_outs = [_dilated_group(a0.reshape(b, s, SEC), 0, b, s), _dilated_group(a1, 1, b, s),
              _dilated_group(a2, 2, b, s)]
    a_outs[0] = tuple(v.reshape(t, A_OUT) for v in a_outs[0])
    o_b = _sink_swa(bsec.reshape(b, s, SEC), sinks.astype(F32), b, s).reshape(t, -1)
    o_win = _nsa_window(cq.reshape(b, s, SEC), ckv.reshape(b, s, SEC), b, s).reshape(t, -1)

    n_chunks = s // CMP_STRIDE
    n_cmp = (s - CMP_BLOCK) // CMP_STRIDE + 1
    n_slc = s // SLC_BLOCK
    nq = s // Q_BLOCK
    kg = jnp.tile(qk_gain[2, 1], C_KV_HEADS).reshape(1, LANES)
    kc, vct = _compress(cmpd, *_compress_weights(cmp_pos, cmp_w1, cmp_w2), kg)
    o_cmp, sel, cnt = _cmp_select(qt, kc, vct, _overlap_t(n_slc, n_chunks, n_cmp), n_cmp=n_cmp)

    act = (cnt[:, :, :, 0, :] > 0).reshape(b, C_KV_HEADS, nq, nq, 2).any(axis=-1)
    act = act & (jnp.arange(nq)[None, :] < jnp.arange(nq)[:, None])
    order = jnp.argsort(jnp.logical_not(act), axis=-1, stable=True).astype(jnp.int32)
    count = jnp.sum(act, axis=-1, dtype=jnp.int32)[..., None]
    lists = jnp.concatenate([count, order] + [jnp.zeros_like(count)] * (SLC_GROUP - 1), axis=-1)
    o_slc = _slc_attention(lists.reshape(-1), qt, ks.reshape(b, s, 2 * LANES),
                           vst.reshape(b, nq, C_KV_HEADS, HEAD_DIM, Q_BLOCK), sel, stride=nq + SLC_GROUP)

    return _merge(x, mix_norm, a_outs, o_b, o_cmp, o_slc, o_win, cg,
                  w_in[:, QKV_COLS + c_gate_cols:].astype(BF16), w_a.astype(BF16), w_b.astype(BF16),
                  w_c.astype(BF16), _gate_expand())


def kernel(x, ffn1_norm, ffn1_w_gu, ffn1_w_down, mix_norm, w_in, qk_gain, sinks, cmp_pos, cmp_w1, cmp_w2,
           w_branch_a, w_branch_b, w_branch_c, w_out, ffn2_norm, ffn2_w_gu, ffn2_w_down):
    b, s, d = x.shape
    h = x.reshape(b * s, d)
    w1_gu, w1_down = ffn1_w_gu.astype(BF16), ffn1_w_down.astype(BF16)
    w2_gu, w2_down = ffn2_w_gu.astype(BF16), ffn2_w_down.astype(BF16)
    for l in range(ffn1_norm.shape[0]):
        h = _ffn(h, ffn1_norm[l], w1_gu, w1_down, l)
        merged = _token_mixing(h, b, s, mix_norm[l], w_in[l], qk_gain[l], sinks[l], cmp_pos[l], cmp_w1[l],
                               cmp_w2[l], w_branch_a[l], w_branch_b[l], w_branch_c[l])
        h = _out_proj(h, merged, w_out[l].astype(BF16))
        h = _ffn(h, ffn2_norm[l], w2_gu, w2_down, l)
    return h.reshape(b, s, d)
```

```python
import functools
import math

import numpy as np
import jax
import jax.numpy as jnp
from jax import lax
from jax.experimental import pallas as pl
from jax.experimental.pallas import tpu as pltpu

F32 = jnp.float32
BF16 = jnp.bfloat16

HEAD_DIM = 64
Q_BLOCK = 128
LANES = 128
A_GROUPS = ((128, 1), (512, 4), (2048, 16))
A_HEADS_PER_GROUP = 4
A_HEADS = 12
A_OUT = A_HEADS_PER_GROUP * HEAD_DIM
B_HEADS = 8
B_KV_HEADS = 2
B_WINDOW = 128
C_HEADS = 12
C_KV_HEADS = 2
C_REP = C_HEADS // C_KV_HEADS
CMP_BLOCK = 32
CMP_STRIDE = 16
CMP_HIDDEN = 256
SLC_BLOCK = 64
SLC_TOPK = 16
C_WINDOW = 512
RMS_EPS = 1e-6
NEG_INF = -1e30
SEC = 768
N_SEC = 6
QKV_COLS = SEC * N_SEC
TM = 512
CHUNK16 = Q_BLOCK * 16
VMEM_LIMIT = 56 * 1024 * 1024


def _slopes(n):
    return [float(2.0 ** (-8.0 * (h + 1) / n)) for h in range(n)]


def _cparams(sem):
    return pltpu.CompilerParams(dimension_semantics=sem, vmem_limit_bytes=VMEM_LIMIT)


def _dot(a, b):
    return jnp.dot(a, b, preferred_element_type=F32)


def _nt_dot(a, b):
    return lax.dot_general(a, b, (((1,), (1,)), ((), ())), preferred_element_type=F32)


def _split(v):
    hi = v.astype(BF16)
    return hi, (v - hi.astype(F32)).astype(BF16)


def _resident(shape):
    return pl.BlockSpec(shape, lambda *_: (0,) * len(shape), pipeline_mode=pl.Buffered(1))


def _rms_rows(x, g):
    ms = jnp.mean(x * x, axis=-1, keepdims=True)
    return (x * lax.rsqrt(ms + RMS_EPS) * g).astype(BF16)


def _deinterleave(n, d):
    p = np.zeros((n, n), np.float32)
    r = np.arange(n // d)
    for c in range(d):
        p[c * (n // d) + r, d * r + c] = 1.0
    return p


AUG_POS = 0
AUG_ROWS = 16
MASK_BIG = 1e30
SLC_GROUP = 5
LOG2E = math.log2(math.e)


def _key_pattern(n):
    pat = np.zeros((n, LANES), np.float32)
    pat[:, AUG_POS] = pat[:, AUG_POS + 1] = np.arange(n) % Q_BLOCK
    return pat


def _slot_pattern():
    pat = np.zeros((SLC_GROUP, Q_BLOCK, 2 * LANES), np.float32)
    r = np.arange(Q_BLOCK)
    for u in range(SLC_GROUP):
        pat[u, :, LANES + AUG_ROWS * (u + 1)] = r < SLC_BLOCK
        pat[u, :, LANES + AUG_ROWS * (u + 1) + 1] = r >= SLC_BLOCK
    return pat


def _block_spread(n_tiles):
    m = np.zeros((n_tiles * AUG_ROWS, 2 * n_tiles), np.float32)
    jj = np.arange(n_tiles)
    for e in range(2):
        m[AUG_ROWS * jj + e, 2 * jj + e] = 1.0
    return m


def _ffn_body(x_ref, g_ref, wg_ref, wu_ref, wd_ref, o_ref, h_ref, *, n_f):
    f = pl.program_id(1)

    @pl.when(f == 0)
    def _():
        h_ref[...] = _rms_rows(x_ref[...], g_ref[...])
        o_ref[...] = jnp.zeros_like(o_ref)

    h = h_ref[...]
    gate = _dot(h, wg_ref[...])
    up = _dot(h, wu_ref[...])
    act = (gate * jax.nn.sigmoid(gate) * up).astype(BF16)
    o_ref[...] += _dot(act, wd_ref[...])

    @pl.when(f == n_f - 1)
    def _():
        o_ref[...] = x_ref[...] + 0.5 * o_ref[...]


def _ffn(x, g, w_gu, w_down, layer, *, tm=1024, tf=512):
    t, d = x.shape
    d_ff = w_down.shape[1]
    n_f = d_ff // tf
    rows = pl.BlockSpec((tm, d), lambda i, f: (i, 0))
    return pl.pallas_call(
        functools.partial(_ffn_body, n_f=n_f),
        grid=(t // tm, n_f),
        in_specs=[
            rows,
            pl.BlockSpec((1, d), lambda i, f: (0, 0)),
            pl.BlockSpec((None, d, tf), lambda i, f: (layer, 0, f)),
            pl.BlockSpec((None, d, tf), lambda i, f: (layer, 0, f + n_f)),
            pl.BlockSpec((None, tf, d), lambda i, f: (layer, f, 0)),
        ],
        out_specs=rows,
        out_shape=jax.ShapeDtypeStruct((t, d), F32),
        scratch_shapes=[pltpu.VMEM((tm, d), BF16)],
        compiler_params=_cparams(("parallel", "arbitrary")),
    )(x, g.reshape(1, d), w_gu, w_gu, w_down)


NORM_TILE = 256
NORM_TILES = {0: (0, 1), 1: (0, 1), 2: (0, 1), 3: (0, 1, 2), 4: (0, 1, 2), 5: (1, 2)}


def _head_sumsq(y, bd):
    return _dot((y * y).astype(BF16), bd)


def _qkv_body(x_ref, g_ref, w_ref, flag_ref, gs_ref, bd_ref, p4_ref, p16_ref, eye_ref, kpat_ref,
              a0_ref, a1_ref, a2_ref, b_ref, cq_ref, ckv_ref, cmpd_ref, qt_ref, ks_ref, vst_ref, cg_ref):
    h = _rms_rows(x_ref[...], g_ref[...])
    bd = bd_ref[...]

    def section(k):
        sl = slice(k * SEC, (k + 1) * SEC)
        if k < len(A_GROUPS):
            y = jnp.concatenate([_dot(h, w_ref[:, part * A_HEADS * HEAD_DIM + k * A_OUT:
                                                  part * A_HEADS * HEAD_DIM + (k + 1) * A_OUT])
                                 for part in range(3)], axis=1)
        else:
            y = _dot(h, w_ref[:, sl])
        tiles = []
        for c in range(SEC // NORM_TILE):
            yc = y[:, c * NORM_TILE:(c + 1) * NORM_TILE]
            if c in NORM_TILES[k]:
                cols = slice(k * SEC + c * NORM_TILE, k * SEC + (c + 1) * NORM_TILE)
                inv = lax.rsqrt(_head_sumsq(yc, bd) * (1.0 / HEAD_DIM) + RMS_EPS)
                yc = yc * jnp.where(flag_ref[:, cols] > 0, inv, 1.0) * gs_ref[:, cols]
            tiles.append(yc.astype(BF16))
        return jnp.concatenate(tiles, axis=1)

    a0_ref[...] = section(0)
    a1_ref[...] = _dot(p4_ref[...], section(1)).astype(BF16).reshape(a1_ref.shape)
    a2_ref[...] = _dot(p16_ref[...], section(2)).astype(BF16).reshape(a2_ref.shape)
    b_ref[...] = section(3)
    y_cq = section(4)
    cq_ref[...] = y_cq
    qt_ref[...] = _nt_dot(eye_ref[...], y_cq).astype(BF16)
    y_ckv = section(5)
    ckv_ref[...] = y_ckv
    cmpd_ref[...] = _dot(p16_ref[...], y_ckv[:, 0:2 * LANES]).astype(BF16).reshape(cmpd_ref.shape)
    ks_ref[:, 0:LANES] = y_ckv[:, 2 * LANES:3 * LANES]
    ks_ref[:, LANES:2 * LANES] = kpat_ref[...]
    eye = eye_ref[0:LANES, 0:LANES]
    for kb in range(vst_ref.shape[0]):
        vt = _nt_dot(eye, y_ckv[kb * Q_BLOCK:(kb + 1) * Q_BLOCK, 3 * LANES:4 * LANES]).astype(BF16)
        for gg in range(C_KV_HEADS):
            vst_ref[kb, gg] = vt[gg * HEAD_DIM:(gg + 1) * HEAD_DIM]
    cg_ref[...] = jax.nn.sigmoid(_dot(h, w_ref[:, QKV_COLS:QKV_COLS + LANES]))


def _qkv_proj(x, g, w, flag, gs, b, s):
    t, d = x.shape
    tiles_per_batch = s // TM
    per16 = CHUNK16 // TM
    bd = jnp.asarray(np.kron(np.eye(NORM_TILE // HEAD_DIM), np.ones((HEAD_DIM, HEAD_DIM))), BF16)
    p4 = jnp.asarray(_deinterleave(TM, 4), BF16)
    p16 = jnp.asarray(_deinterleave(TM, 16), BF16)
    eye = jnp.asarray(np.eye(SEC), BF16)
    nat = pl.BlockSpec((TM, SEC), lambda i: (i, 0))
    out_specs = [
        nat,
        pl.BlockSpec((None, 4, Q_BLOCK, SEC), lambda i: (i, 0, 0, 0)),
        pl.BlockSpec((None, 16, TM // 16, SEC), lambda i: (i // per16, 0, i % per16, 0)),
        nat, nat, nat,
        pl.BlockSpec((None, 16, TM // 16, 2 * LANES), lambda i: (i // tiles_per_batch, 0, i % tiles_per_batch, 0)),
        pl.BlockSpec((None, SEC, TM), lambda i: (i // tiles_per_batch, 0, i % tiles_per_batch)),
        pl.BlockSpec((TM, 2 * LANES), lambda i: (i, 0)),
        pl.BlockSpec((TM // Q_BLOCK, C_KV_HEADS, HEAD_DIM, Q_BLOCK), lambda i: (i, 0, 0, 0)),
        pl.BlockSpec((TM, LANES), lambda i: (i, 0)),
    ]
    out_shape = [
        jax.ShapeDtypeStruct((t, SEC), BF16),
        jax.ShapeDtypeStruct((t // TM, 4, Q_BLOCK, SEC), BF16),
        jax.ShapeDtypeStruct((t // CHUNK16, 16, Q_BLOCK, SEC), BF16),
        jax.ShapeDtypeStruct((t, SEC), BF16),
        jax.ShapeDtypeStruct((t, SEC), BF16),
        jax.ShapeDtypeStruct((t, SEC), BF16),
        jax.ShapeDtypeStruct((b, 16, s // 16, 2 * LANES), BF16),
        jax.ShapeDtypeStruct((b, SEC, s), BF16),
        jax.ShapeDtypeStruct((t, 2 * LANES), BF16),
        jax.ShapeDtypeStruct((t // Q_BLOCK, C_KV_HEADS, HEAD_DIM, Q_BLOCK), BF16),
        jax.ShapeDtypeStruct((t, LANES), F32),
    ]
    n_w = w.shape[1]
    return pl.pallas_call(
        _qkv_body,
        grid=(t // TM,),
        in_specs=[
            pl.BlockSpec((TM, d), lambda i: (i, 0)),
            _resident((1, d)),
            _resident((d, n_w)),
            _resident((1, QKV_COLS)),
            _resident((1, QKV_COLS)),
            _resident((NORM_TILE, NORM_TILE)),
            _resident((TM, TM)),
            _resident((TM, TM)),
            _resident((SEC, SEC)),
            _resident((TM, LANES)),
        ],
        out_specs=out_specs,
        out_shape=out_shape,
        compiler_params=_cparams(("parallel",)),
    )(x, g.reshape(1, d), w, flag.reshape(1, -1), gs.reshape(1, -1), bd, p4, p16, eye,
      jnp.asarray(_key_pattern(TM), BF16))


def _banded_body(*refs, nb, heads, k_off, v_off, n_pairs, q_axis, use_sinks, with_lse, stack, fold, log2_units):
    refs = list(refs)
    q_ref = refs.pop(0)
    kv_refs = [refs.pop(0) for _ in range(nb + 1)]
    qc_ref, kaug_ref, band_ref = refs.pop(0), refs.pop(0), refs.pop(0)
    sink_ref = refs.pop(0) if use_sinks else None
    o_ref = refs.pop(0)
    lse_ref = refs.pop(0) if with_lse else None

    i = pl.program_id(q_axis)
    nk = (nb + 1) * Q_BLOCK
    col = lax.broadcasted_iota(jnp.int32, (Q_BLOCK, nk), 1)
    mask = band_ref[...] + jnp.where(col < (nb - i) * Q_BLOCK, -MASK_BIG, 0.0)
    rel_f = (nb * Q_BLOCK + lax.broadcasted_iota(jnp.int32, (Q_BLOCK, nk), 0) - col).astype(F32)
    lane = lax.broadcasted_iota(jnp.int32, (Q_BLOCK, LANES), 1)
    low_half = lane < HEAD_DIM
    kaug = kaug_ref[...]

    kv_cache = {}

    def kv_tile(off, kv_pair, swapped):
        key = (off, kv_pair, swapped)
        if key not in kv_cache:
            c0 = off + kv_pair * LANES
            tile = jnp.concatenate([r[:, c0:c0 + LANES] for r in kv_refs], axis=0)
            if swapped:
                tile = pltpu.roll(tile.astype(F32), HEAD_DIM, 1).astype(BF16)
            kv_cache[key] = tile
        return kv_cache[key]

    classes = {}
    for head in heads:
        pair, half, kv_pair, kv_half, slope, hidx = head
        key = (kv_pair, kv_half != half) if stack else hidx
        classes.setdefault(key, []).append(head)

    outs = [[None, None] for _ in range(n_pairs)]
    lses = [[None, None] for _ in range(n_pairs)]
    groups = list(classes.values())

    scores = []
    for members in groups:
        n_h = len(members)
        kv_pair, swapped = members[0][2], members[0][3] != members[0][1]
        q_rows = []
        for pair, half, _, _, _, hidx in members:
            qp = q_ref[:, pair * LANES:(pair + 1) * LANES]
            own = low_half if half == 0 else jnp.logical_not(low_half)
            qm = jnp.where(own, qp, jnp.zeros_like(qp))
            q_rows.append(jnp.concatenate([qm, qc_ref[hidx]], axis=1) if fold else qm)
        if fold:
            k_aug = jnp.concatenate([kv_tile(k_off, kv_pair, swapped), kaug], axis=1)
            s = _nt_dot(jnp.concatenate(q_rows, axis=0), k_aug)
            s = (s.reshape(n_h, Q_BLOCK, nk) + mask[None]).reshape(n_h * Q_BLOCK, nk)
        else:
            bias = jnp.concatenate([mask - member[4] * rel_f for member in members], axis=0)
            s = _nt_dot(jnp.concatenate(q_rows, axis=0), kv_tile(k_off, kv_pair, swapped)) + bias
        scores.append(s)

    probs = []
    for members, s in zip(groups, scores):
        m = jnp.max(s, axis=1, keepdims=True)
        if use_sinks:
            assert len(members) == 1
            sink = sink_ref[members[0][5]]
            m = jnp.maximum(m, sink)
        p = jnp.exp2(s - m) if log2_units else jnp.exp(s - m)
        den = jnp.sum(p, axis=1, keepdims=True)
        if use_sinks:
            den = den + jnp.exp(sink - m)
        probs.append((p.astype(BF16), m, den))

    for members, (p, m, den) in zip(groups, probs):
        kv_pair, swapped = members[0][2], members[0][3] != members[0][1]
        r = _dot(p, kv_tile(v_off, kv_pair, swapped)) / den
        lse = m + jnp.log(den) if with_lse else None
        for k, (pair, half, _, _, _, _) in enumerate(members):
            outs[pair][half] = r[k * Q_BLOCK:(k + 1) * Q_BLOCK]
            if with_lse:
                lses[pair][half] = jnp.broadcast_to(lse[k * Q_BLOCK:(k + 1) * Q_BLOCK], (Q_BLOCK, LANES))
    for pair in range(n_pairs):
        sl = slice(pair * LANES, (pair + 1) * LANES)
        o_ref[:, sl] = jnp.where(low_half, outs[pair][0], outs[pair][1])
        if with_lse:
            lse_ref[:, sl] = jnp.where(low_half, lses[pair][0], lses[pair][1])


def _banded_consts(heads, nb, max_dist):
    nk = (nb + 1) * Q_BLOCK
    slope = np.asarray([h[4] for h in heads], np.float32)[:, None]
    q_dist = (nb * Q_BLOCK + np.arange(Q_BLOCK, dtype=np.float32))[None, :]
    ones = np.ones_like(q_dist)
    vals = jnp.asarray(np.stack([slope * ones, slope * Q_BLOCK * ones, -slope * q_dist], axis=-1))
    hi = vals.astype(BF16)
    lo = (vals - hi.astype(F32)).astype(BF16)
    cols = jnp.stack([hi[..., 0], lo[..., 0], hi[..., 1], lo[..., 1], hi[..., 2], lo[..., 2]], axis=-1)
    qc = jnp.pad(cols, ((0, 0), (0, 0), (0, LANES - cols.shape[-1])))
    kaug = np.zeros((nk, LANES), np.float32)
    kaug[:, 0] = kaug[:, 1] = np.arange(nk) % Q_BLOCK
    kaug[:, 2] = kaug[:, 3] = np.arange(nk) // Q_BLOCK
    kaug[:, 4] = kaug[:, 5] = 1.0
    rel = nb * Q_BLOCK + np.arange(Q_BLOCK)[:, None] - np.arange(nk)[None, :]
    band = np.where((rel >= 0) & (rel <= max_dist), 0.0, -MASK_BIG).astype(np.float32)
    return [qc, jnp.asarray(kaug, BF16), jnp.asarray(band)]


def _banded_call(q_arr, kv_arr, *, grid, q_map, kv_map, out_map, out_lead, out_cols, nb, max_dist, heads,
                 k_off, v_off, q_axis, sinks=None, with_lse=False, stack=False, fold=False, log2_units=False):
    assert not (log2_units and (with_lse or sinks is not None))
    lead = (None,) * (q_arr.ndim - 2)
    blk = lead + (Q_BLOCK, SEC)
    in_specs = [pl.BlockSpec(blk, q_map)]
    args = [q_arr]
    for back in range(nb, -1, -1):
        in_specs.append(pl.BlockSpec(blk, functools.partial(kv_map, back=back)))
        args.append(kv_arr)
    consts = _banded_consts(heads, nb, max_dist)
    in_specs += [pl.BlockSpec(c.shape, lambda *_, nd=c.ndim: (0,) * nd) for c in consts]
    args += consts
    if sinks is not None:
        in_specs.append(pl.BlockSpec(memory_space=pltpu.SMEM))
        args.append(sinks)
    oblk = pl.BlockSpec(lead + (Q_BLOCK, out_cols), out_map)
    oshape = jax.ShapeDtypeStruct(out_lead + (out_cols,), F32)
    body = functools.partial(_banded_body, nb=nb, heads=heads, k_off=k_off, v_off=v_off,
                             n_pairs=out_cols // LANES, q_axis=q_axis, use_sinks=sinks is not None,
                             with_lse=with_lse, stack=stack, fold=fold, log2_units=log2_units)
    return pl.pallas_call(
        body,
        grid=grid,
        in_specs=in_specs,
        out_specs=[oblk, oblk] if with_lse else oblk,
        out_shape=[oshape, oshape] if with_lse else oshape,
        compiler_params=_cparams(("parallel",) * len(grid)),
    )(*args)


def _dilated_group(arr, gi, b, s):
    window, dil = A_GROUPS[gi]
    slopes = _slopes(A_HEADS)
    heads = tuple((hh // 2, hh % 2, hh // 2, hh % 2, slopes[gi * A_HEADS_PER_GROUP + hh] * dil, hh)
                  for hh in range(A_HEADS_PER_GROUP))
    common = dict(out_cols=A_OUT, nb=1, max_dist=window // dil, heads=heads, k_off=256, v_off=512,
                  with_lse=True, stack=False)
    if dil == 1:
        return _banded_call(
            arr, arr, grid=(b, s // Q_BLOCK), q_axis=1,
            q_map=lambda bb, i: (bb, i, 0),
            kv_map=lambda bb, i, back: (bb, jnp.maximum(i - back, 0), 0),
            out_map=lambda bb, i: (bb, i, 0), out_lead=(b, s), **common)
    nc = s // (Q_BLOCK * dil)
    return _banded_call(
        arr, arr, grid=(b, dil, nc), q_axis=2,
        q_map=lambda bb, c, i: (bb * nc + i, c, 0, 0),
        kv_map=lambda bb, c, i, back: (bb * nc + jnp.maximum(i - back, 0), c, 0, 0),
        out_map=lambda bb, c, i: (bb * nc + i, c, 0, 0), out_lead=(b * nc, dil, Q_BLOCK), **common)


def _sink_swa(arr, sinks, b, s):
    slopes = _slopes(B_HEADS)
    rep = B_HEADS // B_KV_HEADS
    heads = tuple((h // 2, h % 2, 0, h // rep, slopes[h], h) for h in range(B_HEADS))
    return _banded_call(
        arr, arr, grid=(b, s // Q_BLOCK), q_axis=1,
        q_map=lambda bb, i: (bb, i, 0),
        kv_map=lambda bb, i, back: (bb, jnp.maximum(i - back, 0), 0),
        out_map=lambda bb, i: (bb, i, 0), out_lead=(b, s), out_cols=B_HEADS * HEAD_DIM,
        nb=1, max_dist=B_WINDOW - 1, heads=heads, k_off=512, v_off=640, sinks=sinks)


def _nsa_window(cq, ckv, b, s):
    slopes = _slopes(C_HEADS)
    heads = tuple((h // 2, h % 2, 0, h // C_REP, slopes[h] * LOG2E, h) for h in range(C_HEADS))
    return _banded_call(
        cq, ckv, grid=(b, s // Q_BLOCK), q_axis=1, log2_units=True,
        q_map=lambda bb, i: (bb, i, 0),
        kv_map=lambda bb, i, back: (bb, jnp.maximum(i - back, 0), 0),
        out_map=lambda bb, i: (bb, i, 0), out_lead=(b, s), out_cols=C_HEADS * HEAD_DIM,
        nb=-(-(C_WINDOW - 1) // Q_BLOCK), max_dist=C_WINDOW - 1, heads=heads, k_off=512, v_off=640, stack=True, fold=True)


def _compress_body(t_ref, wb_ref, prow_ref, w2k_ref, w2vt_ref, kg_ref, bd_ref, kc_ref, vct_ref, *, n_chunks):
    hid_cols = 2 * C_KV_HEADS * CMP_HIDDEN
    u = jnp.zeros((n_chunks, hid_cols), F32)
    v = jnp.zeros((n_chunks, hid_cols), F32)
    pc = jnp.zeros((1, hid_cols), F32)
    for c in range(CMP_STRIDE):
        tc = t_ref[c]
        u = u + _dot(tc, wb_ref[0, c])
        v = v + _dot(tc, wb_ref[1, c])
        pc = pc + _dot(prow_ref[0, c], wb_ref[0, c])[0:1] + _dot(prow_ref[1, c], wb_ref[1, c])[0:1]
    hsum = u + pltpu.roll(v, n_chunks - 1, 0) + pc
    hid = (hsum * jax.nn.sigmoid(hsum)).astype(BF16)
    half = C_KV_HEADS * CMP_HIDDEN
    k = _dot(hid[:, :half], w2k_ref[...])
    hi, lo = _split(k * k)
    ss = _dot(hi, bd_ref[...]) + _dot(lo, bd_ref[...])
    kc_ref[...] = (k * lax.rsqrt(ss * (1.0 / HEAD_DIM) + RMS_EPS) * kg_ref[...]).astype(BF16)
    vct_ref[...] = _nt_dot(w2vt_ref[...], hid[:, half:]).astype(BF16)


def _compress(cmpd, wb, prow, w2k, w2vt, kg):
    b, _, n_chunks, width = cmpd.shape
    bd = jnp.asarray(np.kron(np.eye(LANES // HEAD_DIM), np.ones((HEAD_DIM, HEAD_DIM))), BF16)
    return pl.pallas_call(
        functools.partial(_compress_body, n_chunks=n_chunks),
        grid=(b,),
        in_specs=[
            pl.BlockSpec((None, CMP_STRIDE, n_chunks, width), lambda bb: (bb, 0, 0, 0)),
            _resident(wb.shape), _resident(prow.shape), _resident(w2k.shape), _resident(w2vt.shape),
            _resident((1, LANES)), _resident((LANES, LANES)),
        ],
        out_specs=[
            pl.BlockSpec((None, n_chunks, LANES), lambda bb: (bb, 0, 0)),
            pl.BlockSpec((None, LANES, n_chunks), lambda bb: (bb, 0, 0)),
        ],
        out_shape=[
            jax.ShapeDtypeStruct((b, n_chunks, LANES), BF16),
            jax.ShapeDtypeStruct((b, LANES, n_chunks), BF16),
        ],
        compiler_params=_cparams(("parallel",)),
    )(cmpd, wb, prow, w2k, w2vt, kg, bd)


def _to_natural(ot_list, eye):
    pairs = []
    for k in range(0, len(ot_list), 2):
        hi, lo = _split(jnp.concatenate([ot_list[k], ot_list[k + 1]], axis=0))
        pairs.append(_nt_dot(eye, hi) + _nt_dot(eye, lo))
    return jnp.concatenate(pairs, axis=1)


def _cmp_body(qt_ref, kc_ref, vct_ref, ovt_ref, eye_ref, spread_ref, cmask_ref,
              o_ref, selm_ref, cnt_ref, sel_sc, *, n_top):
    g = pl.program_id(1)
    i = pl.program_id(2)
    n_pad = kc_ref.shape[0]
    n_slc = ovt_ref.shape[0]
    per_q = Q_BLOCK // CMP_STRIDE
    own_rows = (lax.broadcasted_iota(jnp.int32, (LANES, Q_BLOCK), 0) // HEAD_DIM) == g
    slopes = [sl * LOG2E for sl in _slopes(C_HEADS)]

    def attend(rows):
        kc = kc_ref[0:rows, :]
        vct = vct_ref[:, 0:rows]
        mask = cmask_ref[pl.ds(pl.multiple_of(n_pad - per_q * i, 8), rows), :]
        n_f = (CMP_STRIDE * lax.broadcasted_iota(jnp.int32, (rows, Q_BLOCK), 0)).astype(F32)
        psum = jnp.zeros((rows, Q_BLOCK), F32)
        outs = []
        for r in range(C_REP):
            slope = jnp.where(g == 0, slopes[r], slopes[C_REP + r])
            qt = qt_ref[r * HEAD_DIM:(r + 1) * HEAD_DIM, :]
            q_pad = jnp.where(own_rows, jnp.concatenate([qt, qt], axis=0), jnp.zeros((LANES, Q_BLOCK), BF16))
            s = _dot(kc, q_pad) + (slope * n_f + mask)
            m = jnp.maximum(jnp.max(s, axis=0, keepdims=True), -1e20)
            e = jnp.exp2(s - m)
            den = jnp.sum(e, axis=0, keepdims=True)
            p = e * (1.0 / jnp.where(den > 0, den, 1.0))
            psum = psum + p
            both = _dot(vct, p.astype(BF16))
            outs.append(jnp.where(g == 0, both[:HEAD_DIM], both[HEAD_DIM:]))
        o_ref[...] = _to_natural(outs, eye_ref[...])

        n_j = rows * CMP_STRIDE // SLC_BLOCK
        hi, lo = _split(psum)
        ovt = ovt_ref[0:n_j, 0:rows]
        imp = _dot(ovt, hi) + _dot(ovt, lo)
        j_idx = lax.broadcasted_iota(jnp.int32, (n_j, Q_BLOCK), 0)
        t_q = i * Q_BLOCK + lax.broadcasted_iota(jnp.int32, (n_j, Q_BLOCK), 1)
        cur = lax.shift_right_logical(t_q, int(math.log2(SLC_BLOCK)))
        forced = ((j_idx == 0) | (j_idx == cur) | (j_idx == cur - 1)) & (j_idx <= cur)
        v = jnp.where((j_idx <= cur) & jnp.logical_not(forced), imp, -1.0)
        sel = jnp.where(forced, 1.0, 0.0)
        for _ in range(n_top - 3):
            m = jnp.max(v, axis=0, keepdims=True)
            first = jnp.min(jnp.where((v == m) & (m >= 0.0), j_idx, n_slc), axis=0, keepdims=True)
            pick = j_idx == first
            sel = jnp.where(pick, 1.0, sel)
            v = jnp.where(pick, -1.0, v)
        sel_sc[0:n_j, :] = sel
        if n_j < n_slc:
            sel_sc[n_j:, :] = jnp.zeros((n_slc - n_j, Q_BLOCK), F32)

    n_var = n_pad // LANES
    for var in range(n_var):
        pl.when(i // (LANES // per_q) == var)(functools.partial(attend, (var + 1) * LANES))

    sel = sel_sc[...]
    neg = jnp.where(sel > 0, 0.0, -MASK_BIG).astype(BF16)
    selm_ref[...] = _dot(spread_ref[...], neg).astype(BF16)
    cnt_ref[...] = _nt_dot(jnp.ones((8, Q_BLOCK), BF16), sel.astype(BF16))


def _cmp_select(qt, kc, vct, ovt, *, n_cmp):
    b, _, s = qt.shape
    g = C_KV_HEADS
    n_pad = kc.shape[1]
    n_slc = ovt.shape[0]
    nq = s // Q_BLOCK
    rows = C_REP * HEAD_DIM
    eye = jnp.asarray(np.eye(LANES), BF16)
    n_top = min(SLC_TOPK, n_slc)
    assert n_top > 3 and n_cmp == n_pad - 1 and n_pad % LANES == 0
    n_rel = np.arange(-n_pad, n_pad)[:, None]
    cmask = np.where(CMP_STRIDE * n_rel + CMP_BLOCK - 1 <= np.arange(Q_BLOCK)[None, :], 0.0, -MASK_BIG)
    return pl.pallas_call(
        functools.partial(_cmp_body, n_top=n_top),
        grid=(b, g, nq),
        in_specs=[
            pl.BlockSpec((None, rows, Q_BLOCK), lambda bb, gg, i: (bb, gg, i)),
            pl.BlockSpec((None, n_pad, LANES), lambda bb, gg, i: (bb, 0, 0)),
            pl.BlockSpec((None, LANES, n_pad), lambda bb, gg, i: (bb, 0, 0)),
            pl.BlockSpec((n_slc, n_pad), lambda bb, gg, i: (0, 0)),
            pl.BlockSpec((LANES, LANES), lambda bb, gg, i: (0, 0)),
            pl.BlockSpec((nq * AUG_ROWS, n_slc), lambda bb, gg, i: (0, 0)),
            pl.BlockSpec((2 * n_pad, Q_BLOCK), lambda bb, gg, i: (0, 0)),
        ],
        out_specs=[
            pl.BlockSpec((Q_BLOCK, rows), lambda bb, gg, i: (bb * nq + i, gg)),
            pl.BlockSpec((None, None, None, nq * AUG_ROWS, Q_BLOCK), lambda bb, gg, i: (bb, gg, i, 0, 0)),
            pl.BlockSpec((None, None, None, 8, n_slc), lambda bb, gg, i: (bb, gg, i, 0, 0)),
        ],
        out_shape=[
            jax.ShapeDtypeStruct((b * s, g * rows), F32),
            jax.ShapeDtypeStruct((b, g, nq, nq * AUG_ROWS, Q_BLOCK), BF16),
            jax.ShapeDtypeStruct((b, g, nq, 8, n_slc), F32),
        ],
        scratch_shapes=[pltpu.VMEM((n_slc, Q_BLOCK), F32)],
        compiler_params=_cparams(("parallel", "parallel", "parallel")),
    )(qt, kc, vct, ovt, eye, jnp.asarray(_block_spread(nq), BF16), jnp.asarray(cmask, F32))


def _slc_body(list_ref, qt_ref, ks_ref, vst_ref, selm_ref, eye_ref, slot_ref, o_ref, qaug, m_sc, l_sc, acc_sc,
              *, nq, stride):
    bb = pl.program_id(0)
    g = pl.program_id(1)
    i = pl.program_id(2)
    width = C_REP * Q_BLOCK
    slopes = [sl * LOG2E for sl in _slopes(C_HEADS)]
    slope_s = [jnp.where(g == 0, slopes[r], slopes[C_REP + r]) for r in range(C_REP)]

    own_rows = (lax.broadcasted_iota(jnp.int32, (LANES, width), 0) // HEAD_DIM) == g
    q6 = jnp.concatenate([qt_ref[r * HEAD_DIM:(r + 1) * HEAD_DIM, :] for r in range(C_REP)], axis=1)
    qaug[0:LANES, :] = jnp.where(own_rows, jnp.concatenate([q6, q6], axis=0), jnp.zeros((LANES, width), BF16))
    head = lax.broadcasted_iota(jnp.int32, (LANES, width), 1) // Q_BLOCK
    row = lax.broadcasted_iota(jnp.int32, (LANES, width), 0)
    slope_t = jnp.zeros((LANES, width), F32)
    for r in range(C_REP):
        slope_t = jnp.where(head == r, slope_s[r], slope_t)
    s_hi, s_lo = _split(slope_t)
    slope_rows = jnp.where(row == AUG_POS, s_hi.astype(F32), jnp.where(row == AUG_POS + 1, s_lo.astype(F32), 0.0))
    qaug[LANES:, :] = slope_rows.astype(BF16)

    q_loc = lax.broadcasted_iota(jnp.int32, (Q_BLOCK, Q_BLOCK), 1)
    k_loc = lax.broadcasted_iota(jnp.int32, (Q_BLOCK, Q_BLOCK), 0)
    causal = jnp.where(k_loc > q_loc, -MASK_BIG, 0.0)

    m_sc[...] = jnp.full(m_sc.shape, NEG_INF, F32)
    l_sc[...] = jnp.zeros(l_sc.shape, F32)
    acc_sc[...] = jnp.zeros(acc_sc.shape, F32)

    def accumulate(tiles, own_first):
        keys = []
        for u, (jj, _) in enumerate(tiles):
            keys.append(ks_ref[pl.ds(pl.multiple_of(jj * Q_BLOCK, Q_BLOCK), Q_BLOCK), :] + slot_ref[u])
            rows = selm_ref[pl.ds(pl.multiple_of(jj * AUG_ROWS, AUG_ROWS), AUG_ROWS), :]
            lo = LANES + AUG_ROWS * (u + 1)
            qaug[lo:lo + AUG_ROWS, :] = jnp.concatenate([rows] * C_REP, axis=1)
        st = _dot(jnp.concatenate(keys, axis=0), qaug[...])
        ps, alphas = [], []
        for r in range(C_REP):
            sl = slice(r * Q_BLOCK, (r + 1) * Q_BLOCK)
            m_old = m_sc[:, sl]
            m_new = m_old
            parts = []
            for u, (jj, extra) in enumerate(tiles):
                s = st[u * Q_BLOCK:(u + 1) * Q_BLOCK, sl]
                if own_first and u == 0:
                    s = s + causal
                c = slope_s[r] * ((jj - i) * Q_BLOCK).astype(F32) + extra
                m_new = jnp.maximum(m_new, jnp.max(s, axis=0, keepdims=True) + c)
                parts.append((s, c))
            alpha = jnp.exp2(m_old - m_new)
            l_new = alpha * l_sc[:, sl]
            p_rows = []
            for s, c in parts:
                p = jnp.exp2(s + (c - m_new))
                l_new = l_new + jnp.sum(p, axis=0, keepdims=True)
                p_rows.append(p.astype(BF16))
            l_sc[:, sl] = l_new
            m_sc[:, sl] = m_new
            ps.append(jnp.concatenate(p_rows, axis=0))
            alphas.append(alpha)
        values = jnp.concatenate([vst_ref[jj] for jj, _ in tiles], axis=1)
        pv = _dot(values, jnp.concatenate(ps, axis=1))
        acc_sc[...] = jnp.concatenate(alphas, axis=1) * acc_sc[...] + pv

    base = ((bb * pl.num_programs(1) + g) * nq + i) * stride
    count = list_ref[base]

    def listed(slot):
        return list_ref[base + 1 + slot], jnp.where(slot < count, 0.0, -MASK_BIG)

    accumulate([(i, 0.0)] + [listed(u) for u in range(SLC_GROUP - 1)], True)

    def step(k, carry):
        accumulate([listed(SLC_GROUP - 1 + SLC_GROUP * k + u) for u in range(SLC_GROUP)], False)
        return carry

    rest = jnp.maximum(count - (SLC_GROUP - 1), 0)
    lax.fori_loop(0, (rest + SLC_GROUP - 1) // SLC_GROUP, step, 0)
    l = l_sc[...]
    o = acc_sc[...] / jnp.where(l > 0, l, 1.0)
    o_ref[...] = _to_natural([o[:, r * Q_BLOCK:(r + 1) * Q_BLOCK] for r in range(C_REP)], eye_ref[...])


def _slc_attention(lists, qt, ks, vst, sel, *, stride):
    b, _, s = qt.shape
    g = C_KV_HEADS
    nq = s // Q_BLOCK
    rows = C_REP * HEAD_DIM
    width = C_REP * Q_BLOCK
    eye = jnp.asarray(np.eye(LANES), BF16)
    assert AUG_ROWS * (SLC_GROUP + 1) <= LANES
    slots = jnp.asarray(_slot_pattern(), BF16)
    grid_spec = pltpu.PrefetchScalarGridSpec(
        num_scalar_prefetch=1,
        grid=(b, g, nq),
        in_specs=[
            pl.BlockSpec((None, rows, Q_BLOCK), lambda bb, gg, i, bits: (bb, gg, i)),
            pl.BlockSpec((None, s, 2 * LANES), lambda bb, gg, i, bits: (bb, 0, 0)),
            pl.BlockSpec((None, nq, None, HEAD_DIM, Q_BLOCK), lambda bb, gg, i, bits: (bb, 0, gg, 0, 0)),
            pl.BlockSpec((None, None, None, nq * AUG_ROWS, Q_BLOCK), lambda bb, gg, i, bits: (bb, gg, i, 0, 0)),
            pl.BlockSpec((LANES, LANES), lambda bb, gg, i, bits: (0, 0)),
            pl.BlockSpec(slots.shape, lambda bb, gg, i, bits: (0, 0, 0)),
        ],
        out_specs=pl.BlockSpec((Q_BLOCK, rows), lambda bb, gg, i, bits: (bb * nq + i, gg)),
        scratch_shapes=[
            pltpu.VMEM((2 * LANES, width), BF16),
            pltpu.VMEM((1, width), F32),
            pltpu.VMEM((1, width), F32),
            pltpu.VMEM((HEAD_DIM, width), F32),
        ],
    )
    return pl.pallas_call(
        functools.partial(_slc_body, nq=nq, stride=stride),
        grid_spec=grid_spec,
        out_shape=jax.ShapeDtypeStruct((b * s, g * rows), F32),
        compiler_params=_cparams(("parallel", "parallel", "parallel")),
    )(lists, qt, ks, vst, sel, eye, slots)


def _merge_body(x_ref, g_ref, oa0, la0, oa1, la1, oa2, la2, ob_ref, ocmp_ref, oslc_ref, owin_ref, cg_ref,
                p4t_ref, p16t_ref, ex_ref, wg0_ref, wg1_ref, wg2_ref, wa_ref, wb_ref, wc_ref,
                out_ref, h_ref, oall_ref):
    @pl.when(pl.program_id(1) == 0)
    def _():
        h_ref[...] = _rms_rows(x_ref[...], g_ref[...])

        def natural(ref, pt_ref):
            hi, lo = _split(ref[...].reshape(TM, A_OUT))
            return _dot(pt_ref[...], hi) + _dot(pt_ref[...], lo)

        o0, l0 = oa0[...], la0[...]
        o1, l1 = natural(oa1, p4t_ref), natural(la1, p4t_ref)
        o2, l2 = natural(oa2, p16t_ref), natural(la2, p16t_ref)
        mx = jnp.maximum(jnp.maximum(l0, l1), l2)
        e0, e1, e2 = jnp.exp(l0 - mx), jnp.exp(l1 - mx), jnp.exp(l2 - mx)
        oall_ref[:, 0:A_OUT] = ((e0 * o0 + e1 * o1 + e2 * o2) / (e0 + e1 + e2)).astype(BF16)
        oall_ref[:, A_OUT:A_OUT + B_HEADS * HEAD_DIM] = ob_ref[...].astype(BF16)
        cg_split = jnp.concatenate(_split(cg_ref[...]), axis=1)
        o_c = None
        for w, o_ref in enumerate((ocmp_ref, oslc_ref, owin_ref)):
            term = _dot(cg_split, ex_ref[w]) * o_ref[...]
            o_c = term if o_c is None else o_c + term
        oall_ref[:, A_OUT + B_HEADS * HEAD_DIM:] = o_c.astype(BF16)

    h = h_ref[...]
    c0, c1 = A_OUT, A_OUT + B_HEADS * HEAD_DIM
    merged = jax.nn.sigmoid(_dot(h, wg0_ref[...])) * _dot(oall_ref[:, 0:c0], wa_ref[...])
    merged += jax.nn.sigmoid(_dot(h, wg1_ref[...])) * _dot(oall_ref[:, c0:c1], wb_ref[...])
    merged += jax.nn.sigmoid(_dot(h, wg2_ref[...])) * _dot(oall_ref[:, c1:], wc_ref[...])
    out_ref[...] = merged.astype(BF16)


def _merge(x, g, a_outs, ob, ocmp, oslc, owin, cg, w_gate, wa, wb, wc, ex, *, tn=512):
    t, d = x.shape
    per16 = CHUNK16 // TM
    n_t = d // tn
    (oa0, la0), (oa1, la1), (oa2, la2) = a_outs

    def rows(a):
        return pl.BlockSpec((TM, a.shape[1]), lambda i, n: (i, 0))

    a1_spec = pl.BlockSpec((None, 4, Q_BLOCK, A_OUT), lambda i, n: (i, 0, 0, 0))
    a2_spec = pl.BlockSpec((None, 16, TM // 16, A_OUT), lambda i, n: (i // per16, 0, i % per16, 0))
    p4t = jnp.asarray(_deinterleave(TM, 4).T, BF16)
    p16t = jnp.asarray(_deinterleave(TM, 16).T, BF16)
    in_specs = [
        rows(x), _resident((1, d)),
        rows(oa0), rows(la0), a1_spec, a1_spec, a2_spec, a2_spec,
        rows(ob), rows(ocmp), rows(oslc), rows(owin), rows(cg),
        _resident((TM, TM)), _resident((TM, TM)), _resident(ex.shape),
        pl.BlockSpec((d, tn), lambda i, n: (0, n)),
        pl.BlockSpec((d, tn), lambda i, n: (0, n + n_t)),
        pl.BlockSpec((d, tn), lambda i, n: (0, n + 2 * n_t)),
        pl.BlockSpec((wa.shape[0], tn), lambda i, n: (0, n)),
        pl.BlockSpec((wb.shape[0], tn), lambda i, n: (0, n)),
        pl.BlockSpec((wc.shape[0], tn), lambda i, n: (0, n)),
    ]
    return pl.pallas_call(
        _merge_body,
        grid=(t // TM, n_t),
        in_specs=in_specs,
        out_specs=pl.BlockSpec((TM, tn), lambda i, n: (i, n)),
        out_shape=jax.ShapeDtypeStruct((t, d), BF16),
        scratch_shapes=[pltpu.VMEM((TM, d), BF16), pltpu.VMEM((TM, wa.shape[0] + wb.shape[0] + wc.shape[0]), BF16)],
        compiler_params=_cparams(("parallel", "arbitrary")),
    )(x, g.reshape(1, d), oa0, la0, oa1, la1, oa2, la2, ob, ocmp, oslc, owin, cg, p4t, p16t, ex,
      w_gate, w_gate, w_gate, wa, wb, wc)


def _out_body(x_ref, m_ref, w_ref, o_ref):
    o_ref[...] = x_ref[...] + _dot(m_ref[...], w_ref[...])


def _out_proj(x, merged, w_out):
    t, d = x.shape
    rows = pl.BlockSpec((TM, d), lambda i: (i, 0))
    return pl.pallas_call(
        _out_body,
        grid=(t // TM,),
        in_specs=[rows, rows, _resident((d, d))],
        out_specs=rows,
        out_shape=jax.ShapeDtypeStruct((t, d), F32),
        compiler_params=_cparams(("parallel",)),
    )(x, merged, w_out)


def _qkv_column_params(qk_gain):
    flag, gain, scale = [], [], []
    one = jnp.ones((HEAD_DIM,), F32)

    def add(n_heads, normed, is_q, gvec, units=1.0):
        for _ in range(n_heads):
            flag.append(np.full((HEAD_DIM,), 1.0 if normed else 0.0, np.float32))
            gain.append(gvec if normed else one)
            scale.append(np.full((HEAD_DIM,), units * HEAD_DIM ** -0.5 if is_q else 1.0, np.float32))

    for _ in range(len(A_GROUPS)):
        add(A_HEADS_PER_GROUP, True, True, qk_gain[0, 0])
        add(A_HEADS_PER_GROUP, True, False, qk_gain[0, 1])
        add(A_HEADS_PER_GROUP, False, False, one)
    add(B_HEADS, True, True, qk_gain[1, 0])
    add(B_KV_HEADS, True, False, qk_gain[1, 1])
    add(B_KV_HEADS, False, False, one)
    add(C_HEADS, True, True, qk_gain[2, 0], units=LOG2E)
    for normed in (False, False, True, False, True, False):
        add(C_KV_HEADS, normed, False, qk_gain[2, 1])
    flag = np.concatenate(flag)
    assert flag.shape[0] == QKV_COLS
    return jnp.asarray(flag), jnp.concatenate(gain) * jnp.asarray(np.concatenate(scale))


def _overlap_t(n_slc, n_pad, n_cmp):
    n = np.arange(n_pad)[None, :]
    j = np.arange(n_slc)[:, None]
    start, end = CMP_STRIDE * n, CMP_STRIDE * n + CMP_BLOCK - 1
    ov = (start <= SLC_BLOCK * j + SLC_BLOCK - 1) & (end >= SLC_BLOCK * j) & (n < n_cmp)
    return jnp.asarray(ov, BF16)


def _gate_expand():
    ex = np.zeros((3, LANES, C_HEADS * HEAD_DIM), np.float32)
    for w in range(3):
        for h in range(C_HEADS):
            ex[w, h * 3 + w, h * HEAD_DIM:(h + 1) * HEAD_DIM] = 1.0
    return jnp.asarray(np.concatenate([ex, ex], axis=1), BF16)


def _compress_weights(cmp_pos, cmp_w1, cmp_w2):
    n_q = 2 * C_KV_HEADS
    w1 = cmp_w1.reshape(2, 2, CMP_STRIDE, HEAD_DIM, CMP_HIDDEN)
    w1q = jnp.repeat(w1, C_KV_HEADS, axis=0)
    wb = jnp.einsum("qhcdn,qp->hcqdpn", w1q, jnp.eye(n_q, dtype=F32))
    wb = wb.reshape(2, CMP_STRIDE, n_q * HEAD_DIM, n_q * CMP_HIDDEN).astype(BF16)
    pos = cmp_pos.reshape(2, 2, CMP_STRIDE, HEAD_DIM)
    prow = jnp.repeat(pos, C_KV_HEADS, axis=0).transpose(1, 2, 0, 3).reshape(2, CMP_STRIDE, 1, n_q * HEAD_DIM)
    prow = jnp.broadcast_to(prow, (2, CMP_STRIDE, 8, n_q * HEAD_DIM)).astype(BF16)
    eye_g = jnp.eye(C_KV_HEADS, dtype=F32)
    w2k = jnp.kron(eye_g, cmp_w2[0]).astype(BF16)
    w2vt = jnp.kron(eye_g, cmp_w2[1]).T.astype(BF16)
    return wb, prow, w2k, w2vt


def _token_mixing(x, b, s, mix_norm, w_in, qk_gain, sinks, cmp_pos, cmp_w1, cmp_w2, w_a, w_b, w_c):
    t, d = x.shape
    assert s % CHUNK16 == 0 and d % 512 == 0
    c_gate_cols = 3 * C_HEADS
    w_qkv = w_in[:, :QKV_COLS + LANES].astype(BF16)
    flag, gs = _qkv_column_params(qk_gain)
    a0, a1, a2, bsec, cq, ckv, cmpd, qt, ks, vst, cg = _qkv_proj(x, mix_norm, w_qkv, flag, gs, b, s)

    a_outs = [_dilated_group(a0.reshape(b, s, SEC), 0, b, s), _dilated_group(a1, 1, b, s),
              _dilated_group(a2, 2, b, s)]
    a_outs[0] = tuple(v.reshape(t, A_OUT) for v in a_outs[0])
    o_b = _sink_swa(bsec.reshape(b, s, SEC), sinks.astype(F32), b, s).reshape(t, -1)
    o_win = _nsa_window(cq.reshape(b, s, SEC), ckv.reshape(b, s, SEC), b, s).reshape(t, -1)

    n_chunks = s // CMP_STRIDE
    n_cmp = (s - CMP_BLOCK) // CMP_STRIDE + 1
    n_slc = s // SLC_BLOCK
    nq = s // Q_BLOCK
    kg = jnp.tile(qk_gain[2, 1], C_KV_HEADS).reshape(1, LANES)
    kc, vct = _compress(cmpd, *_compress_weights(cmp_pos, cmp_w1, cmp_w2), kg)
    o_cmp, sel, cnt = _cmp_select(qt, kc, vct, _overlap_t(n_slc, n_chunks, n_cmp), n_cmp=n_cmp)

    act = (cnt[:, :, :, 0, :] > 0).reshape(b, C_KV_HEADS, nq, nq, 2).any(axis=-1)
    act = act & (jnp.arange(nq)[None, :] < jnp.arange(nq)[:, None])
    order = jnp.argsort(jnp.logical_not(act), axis=-1, stable=True).astype(jnp.int32)
    count = jnp.sum(act, axis=-1, dtype=jnp.int32)[..., None]
    lists = jnp.concatenate([count, order] + [jnp.zeros_like(count)] * (SLC_GROUP - 1), axis=-1)
    o_slc = _slc_attention(lists.reshape(-1), qt, ks.reshape(b, s, 2 * LANES),
                           vst.reshape(b, nq, C_KV_HEADS, HEAD_DIM, Q_BLOCK), sel, stride=nq + SLC_GROUP)

    return _merge(x, mix_norm, a_outs, o_b, o_cmp, o_slc, o_win, cg,
                  w_in[:, QKV_COLS + c_gate_cols:].astype(BF16), w_a.astype(BF16), w_b.astype(BF16),
                  w_c.astype(BF16), _gate_expand())


def kernel(x, ffn1_norm, ffn1_w_gu, ffn1_w_down, mix_norm, w_in, qk_gain, sinks, cmp_pos, cmp_w1, cmp_w2,
           w_branch_a, w_branch_b, w_branch_c, w_out, ffn2_norm, ffn2_w_gu, ffn2_w_down):
    b, s, d = x.shape
    h = x.reshape(b * s, d)
    w1_gu, w1_down = ffn1_w_gu.astype(BF16), ffn1_w_down.astype(BF16)
    w2_gu, w2_down = ffn2_w_gu.astype(BF16), ffn2_w_down.astype(BF16)
    for l in range(ffn1_norm.shape[0]):
        h = _ffn(h, ffn1_norm[l], w1_gu, w1_down, l)
        merged = _token_mixing(h, b, s, mix_norm[l], w_in[l], qk_gain[l], sinks[l], cmp_pos[l], cmp_w1[l],
                               cmp_w2[l], w_branch_a[l], w_branch_b[l], w_branch_c[l])
        h = _out_proj(h, merged, w_out[l].astype(BF16))
        h = _ffn(h, ffn2_norm[l], w2_gu, w2_down, l)
    return h.reshape(b, s, d)
```

```python
import functools
import math

import numpy as np
import jax
import jax.numpy as jnp
from jax import lax
from jax.experimental import pallas as pl
from jax.experimental.pallas import tpu as pltpu

F32 = jnp.float32
BF16 = jnp.bfloat16

HEAD_DIM = 64
Q_BLOCK = 128
LANES = 128
A_GROUPS = ((128, 1), (512, 4), (2048, 16))
A_HEADS_PER_GROUP = 4
A_HEADS = 12
A_OUT = A_HEADS_PER_GROUP * HEAD_DIM
B_HEADS = 8
B_KV_HEADS = 2
B_WINDOW = 128
C_HEADS = 12
C_KV_HEADS = 2
C_REP = C_HEADS // C_KV_HEADS
CMP_BLOCK = 32
CMP_STRIDE = 16
CMP_HIDDEN = 256
SLC_BLOCK = 64
SLC_TOPK = 16
C_WINDOW = 512
RMS_EPS = 1e-6
NEG_INF = -1e30
SEC = 768
N_SEC = 6
QKV_COLS = SEC * N_SEC
TM = 512
CHUNK16 = Q_BLOCK * 16
VMEM_LIMIT = 56 * 1024 * 1024


def _slopes(n):
    return [float(2.0 ** (-8.0 * (h + 1) / n)) for h in range(n)]


def _cparams(sem):
    return pltpu.CompilerParams(dimension_semantics=sem, vmem_limit_bytes=VMEM_LIMIT)


def _dot(a, b):
    return jnp.dot(a, b, preferred_element_type=F32)


def _nt_dot(a, b):
    return lax.dot_general(a, b, (((1,), (1,)), ((), ())), preferred_element_type=F32)


def _split(v):
    hi = v.astype(BF16)
    return hi, (v - hi.astype(F32)).astype(BF16)


def _resident(shape):
    return pl.BlockSpec(shape, lambda *_: (0,) * len(shape), pipeline_mode=pl.Buffered(1))


def _rms_rows(x, g):
    ms = jnp.mean(x * x, axis=-1, keepdims=True)
    return (x * lax.rsqrt(ms + RMS_EPS) * g).astype(BF16)


def _deinterleave(n, d):
    p = np.zeros((n, n), np.float32)
    r = np.arange(n // d)
    for c in range(d):
        p[c * (n // d) + r, d * r + c] = 1.0
    return p


AUG_POS = 0
AUG_ROWS = 16
MASK_BIG = 1e30
SLC_GROUP = 7
LOG2E = math.log2(math.e)


def _key_pattern(n):
    pat = np.zeros((n, LANES), np.float32)
    pat[:, AUG_POS] = pat[:, AUG_POS + 1] = np.arange(n) % Q_BLOCK
    return pat


def _slot_pattern():
    pat = np.zeros((SLC_GROUP, Q_BLOCK, 2 * LANES), np.float32)
    r = np.arange(Q_BLOCK)
    for u in range(SLC_GROUP):
        pat[u, :, LANES + AUG_ROWS * (u + 1)] = r < SLC_BLOCK
        pat[u, :, LANES + AUG_ROWS * (u + 1) + 1] = r >= SLC_BLOCK
    return pat


def _block_spread(n_tiles):
    m = np.zeros((n_tiles * AUG_ROWS, 2 * n_tiles), np.float32)
    jj = np.arange(n_tiles)
    for e in range(2):
        m[AUG_ROWS * jj + e, 2 * jj + e] = 1.0
    return m


def _ffn_body(x_ref, g_ref, wg_ref, wu_ref, wd_ref, o_ref, h_ref, *, n_f):
    f = pl.program_id(1)

    @pl.when(f == 0)
    def _():
        h_ref[...] = _rms_rows(x_ref[...], g_ref[...])
        o_ref[...] = jnp.zeros_like(o_ref)

    h = h_ref[...]
    gate = _dot(h, wg_ref[...])
    up = _dot(h, wu_ref[...])
    act = (gate * jax.nn.sigmoid(gate) * up).astype(BF16)
    o_ref[...] += _dot(act, wd_ref[...])

    @pl.when(f == n_f - 1)
    def _():
        o_ref[...] = x_ref[...] + 0.5 * o_ref[...]


def _ffn(x, g, w_gu, w_down, layer, *, tm=1024, tf=512):
    t, d = x.shape
    d_ff = w_down.shape[1]
    n_f = d_ff // tf
    rows = pl.BlockSpec((tm, d), lambda i, f: (i, 0))
    return pl.pallas_call(
        functools.partial(_ffn_body, n_f=n_f),
        grid=(t // tm, n_f),
        in_specs=[
            rows,
            pl.BlockSpec((1, d), lambda i, f: (0, 0)),
            pl.BlockSpec((None, d, tf), lambda i, f: (layer, 0, f)),
            pl.BlockSpec((None, d, tf), lambda i, f: (layer, 0, f + n_f)),
            pl.BlockSpec((None, tf, d), lambda i, f: (layer, f, 0)),
        ],
        out_specs=rows,
        out_shape=jax.ShapeDtypeStruct((t, d), F32),
        scratch_shapes=[pltpu.VMEM((tm, d), BF16)],
        compiler_params=_cparams(("parallel", "arbitrary")),
    )(x, g.reshape(1, d), w_gu, w_gu, w_down)


NORM_TILE = 256
NORM_TILES = {0: (0, 1), 1: (0, 1), 2: (0, 1), 3: (0, 1, 2), 4: (0, 1, 2), 5: (1, 2)}


def _head_sumsq(y, bd):
    return _dot((y * y).astype(BF16), bd)


def _qkv_body(x_ref, g_ref, w_ref, flag_ref, gs_ref, bd_ref, p4_ref, p16_ref, eye_ref, kpat_ref,
              a0_ref, a1_ref, a2_ref, b_ref, cq_ref, ckv_ref, cmpd_ref, qt_ref, ks_ref, vst_ref, cg_ref):
    h = _rms_rows(x_ref[...], g_ref[...])
    bd = bd_ref[...]

    def section(k):
        sl = slice(k * SEC, (k + 1) * SEC)
        if k < len(A_GROUPS):
            y = jnp.concatenate([_dot(h, w_ref[:, part * A_HEADS * HEAD_DIM + k * A_OUT:
                                                  part * A_HEADS * HEAD_DIM + (k + 1) * A_OUT])
                                 for part in range(3)], axis=1)
        else:
            y = _dot(h, w_ref[:, sl])
        tiles = []
        for c in range(SEC // NORM_TILE):
            yc = y[:, c * NORM_TILE:(c + 1) * NORM_TILE]
            if c in NORM_TILES[k]:
                cols = slice(k * SEC + c * NORM_TILE, k * SEC + (c + 1) * NORM_TILE)
                inv = lax.rsqrt(_head_sumsq(yc, bd) * (1.0 / HEAD_DIM) + RMS_EPS)
                yc = yc * jnp.where(flag_ref[:, cols] > 0, inv, 1.0) * gs_ref[:, cols]
            tiles.append(yc.astype(BF16))
        return jnp.concatenate(tiles, axis=1)

    a0_ref[...] = section(0)
    a1_ref[...] = _dot(p4_ref[...], section(1)).astype(BF16).reshape(a1_ref.shape)
    a2_ref[...] = _dot(p16_ref[...], section(2)).astype(BF16).reshape(a2_ref.shape)
    b_ref[...] = section(3)
    y_cq = section(4)
    cq_ref[...] = y_cq
    qt_ref[...] = _nt_dot(eye_ref[...], y_cq).astype(BF16)
    y_ckv = section(5)
    ckv_ref[...] = y_ckv
    cmpd_ref[...] = _dot(p16_ref[...], y_ckv[:, 0:2 * LANES]).astype(BF16).reshape(cmpd_ref.shape)
    ks_ref[:, 0:LANES] = y_ckv[:, 2 * LANES:3 * LANES]
    ks_ref[:, LANES:2 * LANES] = kpat_ref[...]
    eye = eye_ref[0:LANES, 0:LANES]
    for kb in range(vst_ref.shape[0]):
        vt = _nt_dot(eye, y_ckv[kb * Q_BLOCK:(kb + 1) * Q_BLOCK, 3 * LANES:4 * LANES]).astype(BF16)
        for gg in range(C_KV_HEADS):
            vst_ref[kb, gg] = vt[gg * HEAD_DIM:(gg + 1) * HEAD_DIM]
    cg_ref[...] = jax.nn.sigmoid(_dot(h, w_ref[:, QKV_COLS:QKV_COLS + LANES]))


def _qkv_proj(x, g, w, flag, gs, b, s):
    t, d = x.shape
    tiles_per_batch = s // TM
    per16 = CHUNK16 // TM
    bd = jnp.asarray(np.kron(np.eye(NORM_TILE // HEAD_DIM), np.ones((HEAD_DIM, HEAD_DIM))), BF16)
    p4 = jnp.asarray(_deinterleave(TM, 4), BF16)
    p16 = jnp.asarray(_deinterleave(TM, 16), BF16)
    eye = jnp.asarray(np.eye(SEC), BF16)
    nat = pl.BlockSpec((TM, SEC), lambda i: (i, 0))
    out_specs = [
        nat,
        pl.BlockSpec((None, 4, Q_BLOCK, SEC), lambda i: (i, 0, 0, 0)),
        pl.BlockSpec((None, 16, TM // 16, SEC), lambda i: (i // per16, 0, i % per16, 0)),
        nat, nat, nat,
        pl.BlockSpec((None, 16, TM // 16, 2 * LANES), lambda i: (i // tiles_per_batch, 0, i % tiles_per_batch, 0)),
        pl.BlockSpec((None, SEC, TM), lambda i: (i // tiles_per_batch, 0, i % tiles_per_batch)),
        pl.BlockSpec((TM, 2 * LANES), lambda i: (i, 0)),
        pl.BlockSpec((TM // Q_BLOCK, C_KV_HEADS, HEAD_DIM, Q_BLOCK), lambda i: (i, 0, 0, 0)),
        pl.BlockSpec((TM, LANES), lambda i: (i, 0)),
    ]
    out_shape = [
        jax.ShapeDtypeStruct((t, SEC), BF16),
        jax.ShapeDtypeStruct((t // TM, 4, Q_BLOCK, SEC), BF16),
        jax.ShapeDtypeStruct((t // CHUNK16, 16, Q_BLOCK, SEC), BF16),
        jax.ShapeDtypeStruct((t, SEC), BF16),
        jax.ShapeDtypeStruct((t, SEC), BF16),
        jax.ShapeDtypeStruct((t, SEC), BF16),
        jax.ShapeDtypeStruct((b, 16, s // 16, 2 * LANES), BF16),
        jax.ShapeDtypeStruct((b, SEC, s), BF16),
        jax.ShapeDtypeStruct((t, 2 * LANES), BF16),
        jax.ShapeDtypeStruct((t // Q_BLOCK, C_KV_HEADS, HEAD_DIM, Q_BLOCK), BF16),
        jax.ShapeDtypeStruct((t, LANES), F32),
    ]
    n_w = w.shape[1]
    return pl.pallas_call(
        _qkv_body,
        grid=(t // TM,),
        in_specs=[
            pl.BlockSpec((TM, d), lambda i: (i, 0)),
            _resident((1, d)),
            _resident((d, n_w)),
            _resident((1, QKV_COLS)),
            _resident((1, QKV_COLS)),
            _resident((NORM_TILE, NORM_TILE)),
            _resident((TM, TM)),
            _resident((TM, TM)),
            _resident((SEC, SEC)),
            _resident((TM, LANES)),
        ],
        out_specs=out_specs,
        out_shape=out_shape,
        compiler_params=_cparams(("parallel",)),
    )(x, g.reshape(1, d), w, flag.reshape(1, -1), gs.reshape(1, -1), bd, p4, p16, eye,
      jnp.asarray(_key_pattern(TM), BF16))


def _banded_body(*refs, nb, qb, self_keys, heads, k_off, v_off, n_pairs, q_axis, use_sinks, with_lse, stack, fold,
                 log2_units):
    refs = list(refs)
    q_ref = refs.pop(0)
    kv_refs = [refs.pop(0) for _ in range(1 if self_keys else nb + 1)]
    qc_ref, kaug_ref, band_ref = refs.pop(0), refs.pop(0), refs.pop(0)
    sink_ref = refs.pop(0) if use_sinks else None
    o_ref = refs.pop(0)
    lse_ref = refs.pop(0) if with_lse else None

    i = pl.program_id(q_axis)
    nk = (nb + 1) * Q_BLOCK
    col = lax.broadcasted_iota(jnp.int32, (Q_BLOCK, nk), 1)
    band = band_ref[...]
    first_mask = band + jnp.where(col < (nb - i * qb) * Q_BLOCK, -MASK_BIG, 0.0)
    rel_f = (nb * Q_BLOCK + lax.broadcasted_iota(jnp.int32, (Q_BLOCK, nk), 0) - col).astype(F32)
    lane = lax.broadcasted_iota(jnp.int32, (Q_BLOCK, LANES), 1)
    low_half = lane < HEAD_DIM
    kaug = kaug_ref[...]

    def q_cols(sub, c0):
        if len(q_ref.shape) == 3:
            return q_ref[sub, :, c0:c0 + LANES]
        return q_ref[sub * Q_BLOCK:(sub + 1) * Q_BLOCK, c0:c0 + LANES]

    kv_cache = {}

    def kv_tile(sub, off, kv_pair, swapped):
        key = (sub, off, kv_pair, swapped)
        if key not in kv_cache:
            c0 = off + kv_pair * LANES
            if self_keys:
                blocks = [kv_refs[0][:, c0:c0 + LANES] if sub == 0 else q_cols(sub - 1, c0), q_cols(sub, c0)]
            else:
                blocks = [r[:, c0:c0 + LANES] for r in kv_refs]
            tile = jnp.concatenate(blocks, axis=0)
            if swapped:
                tile = pltpu.roll(tile.astype(F32), HEAD_DIM, 1).astype(BF16)
            kv_cache[key] = tile
        return kv_cache[key]

    classes = {}
    for head in heads:
        pair, half, kv_pair, kv_half, slope, hidx = head
        key = (kv_pair, kv_half != half) if stack else hidx
        classes.setdefault(key, []).append(head)

    outs = [[[None, None] for _ in range(n_pairs)] for _ in range(qb)]
    lses = [[[None, None] for _ in range(n_pairs)] for _ in range(qb)]
    groups = [(sub, members) for sub in range(qb) for members in classes.values()]

    scores = []
    for sub, members in groups:
        n_h = len(members)
        kv_pair, swapped = members[0][2], members[0][3] != members[0][1]
        mask = first_mask if sub == 0 else band
        q_rows = []
        for pair, half, _, _, _, hidx in members:
            qp = q_cols(sub, pair * LANES)
            own = low_half if half == 0 else jnp.logical_not(low_half)
            qm = jnp.where(own, qp, jnp.zeros_like(qp))
            q_rows.append(jnp.concatenate([qm, qc_ref[hidx]], axis=1) if fold else qm)
        if fold:
            k_aug = jnp.concatenate([kv_tile(sub, k_off, kv_pair, swapped), kaug], axis=1)
            s = _nt_dot(jnp.concatenate(q_rows, axis=0), k_aug)
            s = (s.reshape(n_h, Q_BLOCK, nk) + mask[None]).reshape(n_h * Q_BLOCK, nk)
        else:
            bias = jnp.concatenate([mask - member[4] * rel_f for member in members], axis=0)
            s = _nt_dot(jnp.concatenate(q_rows, axis=0), kv_tile(sub, k_off, kv_pair, swapped)) + bias
        scores.append(s)

    probs = []
    for (_, members), s in zip(groups, scores):
        m = jnp.max(s, axis=1, keepdims=True)
        if use_sinks:
            assert len(members) == 1
            sink = sink_ref[members[0][5]]
            m = jnp.maximum(m, sink)
        p = jnp.exp2(s - m) if log2_units else jnp.exp(s - m)
        den = jnp.sum(p, axis=1, keepdims=True)
        if use_sinks:
            den = den + jnp.exp(sink - m)
        probs.append((p.astype(BF16), m, den))

    for (sub, members), (p, m, den) in zip(groups, probs):
        kv_pair, swapped = members[0][2], members[0][3] != members[0][1]
        r = _dot(p, kv_tile(sub, v_off, kv_pair, swapped)) / den
        lse = m + jnp.log(den) if with_lse else None
        for k, (pair, half, _, _, _, _) in enumerate(members):
            outs[sub][pair][half] = r[k * Q_BLOCK:(k + 1) * Q_BLOCK]
            if with_lse:
                lses[sub][pair][half] = jnp.broadcast_to(lse[k * Q_BLOCK:(k + 1) * Q_BLOCK], (Q_BLOCK, LANES))

    def store(ref, sub, sl, value):
        if len(ref.shape) == 3:
            ref[sub, :, sl] = value
        else:
            ref[sub * Q_BLOCK:(sub + 1) * Q_BLOCK, sl] = value

    for sub in range(qb):
        for pair in range(n_pairs):
            sl = slice(pair * LANES, (pair + 1) * LANES)
            store(o_ref, sub, sl, jnp.where(low_half, outs[sub][pair][0], outs[sub][pair][1]))
            if with_lse:
                store(lse_ref, sub, sl, jnp.where(low_half, lses[sub][pair][0], lses[sub][pair][1]))


def _banded_consts(heads, nb, max_dist):
    nk = (nb + 1) * Q_BLOCK
    slope = np.asarray([h[4] for h in heads], np.float32)[:, None]
    q_dist = (nb * Q_BLOCK + np.arange(Q_BLOCK, dtype=np.float32))[None, :]
    ones = np.ones_like(q_dist)
    vals = jnp.asarray(np.stack([slope * ones, slope * Q_BLOCK * ones, -slope * q_dist], axis=-1))
    hi = vals.astype(BF16)
    lo = (vals - hi.astype(F32)).astype(BF16)
    cols = jnp.stack([hi[..., 0], lo[..., 0], hi[..., 1], lo[..., 1], hi[..., 2], lo[..., 2]], axis=-1)
    qc = jnp.pad(cols, ((0, 0), (0, 0), (0, LANES - cols.shape[-1])))
    kaug = np.zeros((nk, LANES), np.float32)
    kaug[:, 0] = kaug[:, 1] = np.arange(nk) % Q_BLOCK
    kaug[:, 2] = kaug[:, 3] = np.arange(nk) // Q_BLOCK
    kaug[:, 4] = kaug[:, 5] = 1.0
    rel = nb * Q_BLOCK + np.arange(Q_BLOCK)[:, None] - np.arange(nk)[None, :]
    band = np.where((rel >= 0) & (rel <= max_dist), 0.0, -MASK_BIG).astype(np.float32)
    return [qc, jnp.asarray(kaug, BF16), jnp.asarray(band)]


BAND_QB = 4


def _banded_call(q_arr, kv_arr, *, grid, q_spec, kv_specs, out_spec, out_shape, out_cols, nb, qb, max_dist, heads,
                 k_off, v_off, q_axis, sinks=None, with_lse=False, stack=False, fold=False, log2_units=False):
    assert not (log2_units and (with_lse or sinks is not None))
    self_keys = len(kv_specs) == 1 and nb == 1
    assert self_keys or (qb == 1 and len(kv_specs) == nb + 1)
    consts = _banded_consts(heads, nb, max_dist)
    in_specs = [q_spec] + list(kv_specs) + [pl.BlockSpec(c.shape, lambda *_, nd=c.ndim: (0,) * nd) for c in consts]
    args = [q_arr] + [kv_arr] * len(kv_specs) + consts
    if sinks is not None:
        in_specs.append(pl.BlockSpec(memory_space=pltpu.SMEM))
        args.append(sinks)
    oshape = jax.ShapeDtypeStruct(out_shape, F32)
    body = functools.partial(_banded_body, nb=nb, qb=qb, self_keys=self_keys, heads=heads, k_off=k_off, v_off=v_off,
                             n_pairs=out_cols // LANES, q_axis=q_axis, use_sinks=sinks is not None,
                             with_lse=with_lse, stack=stack, fold=fold, log2_units=log2_units)
    return pl.pallas_call(
        body,
        grid=grid,
        in_specs=in_specs,
        out_specs=[out_spec, out_spec] if with_lse else out_spec,
        out_shape=[oshape, oshape] if with_lse else oshape,
        compiler_params=_cparams(("parallel",) * len(grid)),
    )(*args)


def _row_specs(b, s, qb, out_cols):
    q_spec = pl.BlockSpec((None, qb * Q_BLOCK, SEC), lambda bb, i: (bb, i, 0))
    prev = pl.BlockSpec((None, Q_BLOCK, SEC), lambda bb, i: (bb, jnp.maximum(qb * i - 1, 0), 0))
    out_spec = pl.BlockSpec((None, qb * Q_BLOCK, out_cols), lambda bb, i: (bb, i, 0))
    return dict(grid=(b, s // (qb * Q_BLOCK)), q_axis=1, q_spec=q_spec, kv_specs=[prev], out_spec=out_spec,
                out_shape=(b, s, out_cols), qb=qb)


def _dilated_group(arr, gi, b, s):
    window, dil = A_GROUPS[gi]
    slopes = _slopes(A_HEADS)
    heads = tuple((hh // 2, hh % 2, hh // 2, hh % 2, slopes[gi * A_HEADS_PER_GROUP + hh] * dil, hh)
                  for hh in range(A_HEADS_PER_GROUP))
    common = dict(out_cols=A_OUT, nb=1, max_dist=window // dil, heads=heads, k_off=256, v_off=512,
                  with_lse=True, stack=False)
    if dil == 1:
        return _banded_call(arr, arr, **_row_specs(b, s, BAND_QB, A_OUT), **common)
    nc = s // (Q_BLOCK * dil)
    qb = BAND_QB if nc % BAND_QB == 0 else 1
    q_spec = pl.BlockSpec((qb, None, Q_BLOCK, SEC), lambda bb, c, i: ((bb * nc) // qb + i, c, 0, 0))
    prev = pl.BlockSpec((None, None, Q_BLOCK, SEC), lambda bb, c, i: (bb * nc + jnp.maximum(qb * i - 1, 0), c, 0, 0))
    out_spec = pl.BlockSpec((qb, None, Q_BLOCK, A_OUT), lambda bb, c, i: ((bb * nc) // qb + i, c, 0, 0))
    return _banded_call(arr, arr, grid=(b, dil, nc // qb), q_axis=2, q_spec=q_spec, kv_specs=[prev],
                        out_spec=out_spec, out_shape=(b * nc, dil, Q_BLOCK, A_OUT), qb=qb, **common)


def _sink_swa(arr, sinks, b, s):
    slopes = _slopes(B_HEADS)
    rep = B_HEADS // B_KV_HEADS
    heads = tuple((h // 2, h % 2, 0, h // rep, slopes[h], h) for h in range(B_HEADS))
    return _banded_call(arr, arr, **_row_specs(b, s, BAND_QB, B_HEADS * HEAD_DIM), out_cols=B_HEADS * HEAD_DIM,
                        nb=1, max_dist=B_WINDOW - 1, heads=heads, k_off=512, v_off=640, sinks=sinks)


def _nsa_window(cq, ckv, b, s):
    slopes = _slopes(C_HEADS)
    heads = tuple((h // 2, h % 2, 0, h // C_REP, slopes[h] * LOG2E, h) for h in range(C_HEADS))
    nb = -(-(C_WINDOW - 1) // Q_BLOCK)
    blk = (None, Q_BLOCK, SEC)
    kv_specs = [pl.BlockSpec(blk, lambda bb, i, back=back: (bb, jnp.maximum(i - back, 0), 0))
                for back in range(nb, -1, -1)]
    out_cols = C_HEADS * HEAD_DIM
    return _banded_call(
        cq, ckv, grid=(b, s // Q_BLOCK), q_axis=1, qb=1, log2_units=True,
        q_spec=pl.BlockSpec(blk, lambda bb, i: (bb, i, 0)), kv_specs=kv_specs,
        out_spec=pl.BlockSpec((None, Q_BLOCK, out_cols), lambda bb, i: (bb, i, 0)), out_shape=(b, s, out_cols),
        out_cols=out_cols, nb=nb, max_dist=C_WINDOW - 1, heads=heads, k_off=512, v_off=640, stack=True, fold=True)


def _compress_body(t_ref, wb_ref, prow_ref, w2k_ref, w2vt_ref, kg_ref, bd_ref, kc_ref, vct_ref, *, n_chunks):
    hid_cols = 2 * C_KV_HEADS * CMP_HIDDEN
    u = jnp.zeros((n_chunks, hid_cols), F32)
    v = jnp.zeros((n_chunks, hid_cols), F32)
    pc = jnp.zeros((1, hid_cols), F32)
    for c in range(CMP_STRIDE):
        tc = t_ref[c]
        u = u + _dot(tc, wb_ref[0, c])
        v = v + _dot(tc, wb_ref[1, c])
        pc = pc + _dot(prow_ref[0, c], wb_ref[0, c])[0:1] + _dot(prow_ref[1, c], wb_ref[1, c])[0:1]
    hsum = u + pltpu.roll(v, n_chunks - 1, 0) + pc
    hid = (hsum * jax.nn.sigmoid(hsum)).astype(BF16)
    half = C_KV_HEADS * CMP_HIDDEN
    k = _dot(hid[:, :half], w2k_ref[...])
    hi, lo = _split(k * k)
    ss = _dot(hi, bd_ref[...]) + _dot(lo, bd_ref[...])
    kc_ref[...] = (k * lax.rsqrt(ss * (1.0 / HEAD_DIM) + RMS_EPS) * kg_ref[...]).astype(BF16)
    vct_ref[...] = _nt_dot(w2vt_ref[...], hid[:, half:]).astype(BF16)


def _compress(cmpd, wb, prow, w2k, w2vt, kg):
    b, _, n_chunks, width = cmpd.shape
    bd = jnp.asarray(np.kron(np.eye(LANES // HEAD_DIM), np.ones((HEAD_DIM, HEAD_DIM))), BF16)
    return pl.pallas_call(
        functools.partial(_compress_body, n_chunks=n_chunks),
        grid=(b,),
        in_specs=[
            pl.BlockSpec((None, CMP_STRIDE, n_chunks, width), lambda bb: (bb, 0, 0, 0)),
            _resident(wb.shape), _resident(prow.shape), _resident(w2k.shape), _resident(w2vt.shape),
            _resident((1, LANES)), _resident((LANES, LANES)),
        ],
        out_specs=[
            pl.BlockSpec((None, n_chunks, LANES), lambda bb: (bb, 0, 0)),
            pl.BlockSpec((None, LANES, n_chunks), lambda bb: (bb, 0, 0)),
        ],
        out_shape=[
            jax.ShapeDtypeStruct((b, n_chunks, LANES), BF16),
            jax.ShapeDtypeStruct((b, LANES, n_chunks), BF16),
        ],
        compiler_params=_cparams(("parallel",)),
    )(cmpd, wb, prow, w2k, w2vt, kg, bd)


def _to_natural(ot_list, eye):
    pairs = []
    for k in range(0, len(ot_list), 2):
        hi, lo = _split(jnp.concatenate([ot_list[k], ot_list[k + 1]], axis=0))
        pairs.append(_nt_dot(eye, hi) + _nt_dot(eye, lo))
    return jnp.concatenate(pairs, axis=1)


def _cmp_body(qt_ref, kc_ref, vct_ref, ovt_ref, eye_ref, spread_ref, cmask_ref,
              o_ref, selm_ref, cnt_ref, sel_sc, *, n_top):
    g = pl.program_id(1)
    i = pl.program_id(2)
    n_pad = kc_ref.shape[0]
    n_slc = ovt_ref.shape[0]
    per_q = Q_BLOCK // CMP_STRIDE
    own_rows = (lax.broadcasted_iota(jnp.int32, (LANES, Q_BLOCK), 0) // HEAD_DIM) == g
    slopes = [sl * LOG2E for sl in _slopes(C_HEADS)]

    def attend(rows):
        kc = kc_ref[0:rows, :]
        vct = vct_ref[:, 0:rows]
        mask = cmask_ref[pl.ds(pl.multiple_of(n_pad - per_q * i, 8), rows), :]
        n_f = (CMP_STRIDE * lax.broadcasted_iota(jnp.int32, (rows, Q_BLOCK), 0)).astype(F32)
        psum = jnp.zeros((rows, Q_BLOCK), F32)
        outs = []
        for r in range(C_REP):
            slope = jnp.where(g == 0, slopes[r], slopes[C_REP + r])
            qt = qt_ref[r * HEAD_DIM:(r + 1) * HEAD_DIM, :]
            q_pad = jnp.where(own_rows, jnp.concatenate([qt, qt], axis=0), jnp.zeros((LANES, Q_BLOCK), BF16))
            s = _dot(kc, q_pad) + (slope * n_f + mask)
            m = jnp.maximum(jnp.max(s, axis=0, keepdims=True), -1e20)
            e = jnp.exp2(s - m)
            den = jnp.sum(e, axis=0, keepdims=True)
            p = e * (1.0 / jnp.where(den > 0, den, 1.0))
            psum = psum + p
            both = _dot(vct, p.astype(BF16))
            outs.append(jnp.where(g == 0, both[:HEAD_DIM], both[HEAD_DIM:]))
        o_ref[...] = _to_natural(outs, eye_ref[...])

        n_j = rows * CMP_STRIDE // SLC_BLOCK
        hi, lo = _split(psum)
        ovt = ovt_ref[0:n_j, 0:rows]
        imp = _dot(ovt, hi) + _dot(ovt, lo)
        j_idx = lax.broadcasted_iota(jnp.int32, (n_j, Q_BLOCK), 0)
        t_q = i * Q_BLOCK + lax.broadcasted_iota(jnp.int32, (n_j, Q_BLOCK), 1)
        cur = lax.shift_right_logical(t_q, int(math.log2(SLC_BLOCK)))
        forced = ((j_idx == 0) | (j_idx == cur) | (j_idx == cur - 1)) & (j_idx <= cur)
        v = jnp.where((j_idx <= cur) & jnp.logical_not(forced), imp, -1.0)
        sel = jnp.where(forced, 1.0, 0.0)
        for _ in range(n_top - 3):
            m = jnp.max(v, axis=0, keepdims=True)
            first = jnp.min(jnp.where((v == m) & (m >= 0.0), j_idx, n_slc), axis=0, keepdims=True)
            pick = j_idx == first
            sel = jnp.where(pick, 1.0, sel)
            v = jnp.where(pick, -1.0, v)
        sel_sc[0:n_j, :] = sel
        if n_j < n_slc:
            sel_sc[n_j:, :] = jnp.zeros((n_slc - n_j, Q_BLOCK), F32)

    n_var = n_pad // LANES
    for var in range(n_var):
        pl.when(i // (LANES // per_q) == var)(functools.partial(attend, (var + 1) * LANES))

    sel = sel_sc[...]
    neg = jnp.where(sel > 0, 0.0, -MASK_BIG).astype(BF16)
    selm_ref[...] = _dot(spread_ref[...], neg).astype(BF16)
    cnt_ref[...] = _nt_dot(jnp.ones((8, Q_BLOCK), BF16), sel.astype(BF16))


def _cmp_select(qt, kc, vct, ovt, *, n_cmp):
    b, _, s = qt.shape
    g = C_KV_HEADS
    n_pad = kc.shape[1]
    n_slc = ovt.shape[0]
    nq = s // Q_BLOCK
    rows = C_REP * HEAD_DIM
    eye = jnp.asarray(np.eye(LANES), BF16)
    n_top = min(SLC_TOPK, n_slc)
    assert n_top > 3 and n_cmp == n_pad - 1 and n_pad % LANES == 0
    n_rel = np.arange(-n_pad, n_pad)[:, None]
    cmask = np.where(CMP_STRIDE * n_rel + CMP_BLOCK - 1 <= np.arange(Q_BLOCK)[None, :], 0.0, -MASK_BIG)
    return pl.pallas_call(
        functools.partial(_cmp_body, n_top=n_top),
        grid=(b, g, nq),
        in_specs=[
            pl.BlockSpec((None, rows, Q_BLOCK), lambda bb, gg, i: (bb, gg, i)),
            pl.BlockSpec((None, n_pad, LANES), lambda bb, gg, i: (bb, 0, 0)),
            pl.BlockSpec((None, LANES, n_pad), lambda bb, gg, i: (bb, 0, 0)),
            pl.BlockSpec((n_slc, n_pad), lambda bb, gg, i: (0, 0)),
            pl.BlockSpec((LANES, LANES), lambda bb, gg, i: (0, 0)),
            pl.BlockSpec((nq * AUG_ROWS, n_slc), lambda bb, gg, i: (0, 0)),
            pl.BlockSpec((2 * n_pad, Q_BLOCK), lambda bb, gg, i: (0, 0)),
        ],
        out_specs=[
            pl.BlockSpec((Q_BLOCK, rows), lambda bb, gg, i: (bb * nq + i, gg)),
            pl.BlockSpec((None, None, None, nq * AUG_ROWS, Q_BLOCK), lambda bb, gg, i: (bb, gg, i, 0, 0)),
            pl.BlockSpec((None, None, None, 8, n_slc), lambda bb, gg, i: (bb, gg, i, 0, 0)),
        ],
        out_shape=[
            jax.ShapeDtypeStruct((b * s, g * rows), F32),
            jax.ShapeDtypeStruct((b, g, nq, nq * AUG_ROWS, Q_BLOCK), BF16),
            jax.ShapeDtypeStruct((b, g, nq, 8, n_slc), F32),
        ],
        scratch_shapes=[pltpu.VMEM((n_slc, Q_BLOCK), F32)],
        compiler_params=_cparams(("parallel", "parallel", "parallel")),
    )(qt, kc, vct, ovt, eye, jnp.asarray(_block_spread(nq), BF16), jnp.asarray(cmask, F32))


def _slc_body(list_ref, qt_ref, ks_ref, vst_ref, selm_ref, eye_ref, slot_ref, o_ref, qaug, m_sc, l_sc, acc_sc,
              *, nq, stride):
    bb = pl.program_id(0)
    g = pl.program_id(1)
    i = pl.program_id(2)
    width = C_REP * Q_BLOCK
    slopes = [sl * LOG2E for sl in _slopes(C_HEADS)]
    slope_s = [jnp.where(g == 0, slopes[r], slopes[C_REP + r]) for r in range(C_REP)]

    own_rows = (lax.broadcasted_iota(jnp.int32, (LANES, width), 0) // HEAD_DIM) == g
    q6 = jnp.concatenate([qt_ref[r * HEAD_DIM:(r + 1) * HEAD_DIM, :] for r in range(C_REP)], axis=1)
    qaug[0:LANES, :] = jnp.where(own_rows, jnp.concatenate([q6, q6], axis=0), jnp.zeros((LANES, width), BF16))
    head = lax.broadcasted_iota(jnp.int32, (LANES, width), 1) // Q_BLOCK
    row = lax.broadcasted_iota(jnp.int32, (LANES, width), 0)
    slope_t = jnp.zeros((LANES, width), F32)
    for r in range(C_REP):
        slope_t = jnp.where(head == r, slope_s[r], slope_t)
    s_hi, s_lo = _split(slope_t)
    slope_rows = jnp.where(row == AUG_POS, s_hi.astype(F32), jnp.where(row == AUG_POS + 1, s_lo.astype(F32), 0.0))
    qaug[LANES:, :] = slope_rows.astype(BF16)

    q_loc = lax.broadcasted_iota(jnp.int32, (Q_BLOCK, Q_BLOCK), 1)
    k_loc = lax.broadcasted_iota(jnp.int32, (Q_BLOCK, Q_BLOCK), 0)
    causal = jnp.where(k_loc > q_loc, -MASK_BIG, 0.0)

    m_sc[...] = jnp.full(m_sc.shape, NEG_INF, F32)
    l_sc[...] = jnp.zeros(l_sc.shape, F32)
    acc_sc[...] = jnp.zeros(acc_sc.shape, F32)

    def accumulate(tiles, own_first):
        keys = []
        for u, (jj, _) in enumerate(tiles):
            keys.append(ks_ref[pl.ds(pl.multiple_of(jj * Q_BLOCK, Q_BLOCK), Q_BLOCK), :] + slot_ref[u])
            rows = selm_ref[pl.ds(pl.multiple_of(jj * AUG_ROWS, AUG_ROWS), AUG_ROWS), :]
            lo = LANES + AUG_ROWS * (u + 1)
            qaug[lo:lo + AUG_ROWS, :] = jnp.concatenate([rows] * C_REP, axis=1)
        st = _dot(jnp.concatenate(keys, axis=0), qaug[...])
        ps, alphas = [], []
        for r in range(C_REP):
            sl = slice(r * Q_BLOCK, (r + 1) * Q_BLOCK)
            m_old = m_sc[:, sl]
            m_new = m_old
            parts = []
            for u, (jj, extra) in enumerate(tiles):
                s = st[u * Q_BLOCK:(u + 1) * Q_BLOCK, sl]
                if own_first and u == 0:
                    s = s + causal
                c = slope_s[r] * ((jj - i) * Q_BLOCK).astype(F32) + extra
                m_new = jnp.maximum(m_new, jnp.max(s, axis=0, keepdims=True) + c)
                parts.append((s, c))
            alpha = jnp.exp2(m_old - m_new)
            l_new = alpha * l_sc[:, sl]
            p_rows = []
            for s, c in parts:
                p = jnp.exp2(s + (c - m_new))
                l_new = l_new + jnp.sum(p, axis=0, keepdims=True)
                p_rows.append(p.astype(BF16))
            l_sc[:, sl] = l_new
            m_sc[:, sl] = m_new
            ps.append(jnp.concatenate(p_rows, axis=0))
            alphas.append(alpha)
        values = jnp.concatenate([vst_ref[jj] for jj, _ in tiles], axis=1)
        pv = _dot(values, jnp.concatenate(ps, axis=1))
        acc_sc[...] = jnp.concatenate(alphas, axis=1) * acc_sc[...] + pv

    base = ((bb * pl.num_programs(1) + g) * nq + i) * stride
    count = list_ref[base]

    def listed(slot):
        return list_ref[base + 1 + slot], jnp.where(slot < count, 0.0, -MASK_BIG)

    accumulate([(i, 0.0)] + [listed(u) for u in range(SLC_GROUP - 1)], True)

    def step(k, carry):
        accumulate([listed(SLC_GROUP - 1 + SLC_GROUP * k + u) for u in range(SLC_GROUP)], False)
        return carry

    rest = jnp.maximum(count - (SLC_GROUP - 1), 0)
    lax.fori_loop(0, (rest + SLC_GROUP - 1) // SLC_GROUP, step, 0)
    l = l_sc[...]
    o = acc_sc[...] / jnp.where(l > 0, l, 1.0)
    o_ref[...] = _to_natural([o[:, r * Q_BLOCK:(r + 1) * Q_BLOCK] for r in range(C_REP)], eye_ref[...])


def _slc_attention(lists, qt, ks, vst, sel, *, stride):
    b, _, s = qt.shape
    g = C_KV_HEADS
    nq = s // Q_BLOCK
    rows = C_REP * HEAD_DIM
    width = C_REP * Q_BLOCK
    eye = jnp.asarray(np.eye(LANES), BF16)
    assert AUG_ROWS * (SLC_GROUP + 1) <= LANES
    slots = jnp.asarray(_slot_pattern(), BF16)
    grid_spec = pltpu.PrefetchScalarGridSpec(
        num_scalar_prefetch=1,
        grid=(b, g, nq),
        in_specs=[
            pl.BlockSpec((None, rows, Q_BLOCK), lambda bb, gg, i, bits: (bb, gg, i)),
            pl.BlockSpec((None, s, 2 * LANES), lambda bb, gg, i, bits: (bb, 0, 0)),
            pl.BlockSpec((None, nq, None, HEAD_DIM, Q_BLOCK), lambda bb, gg, i, bits: (bb, 0, gg, 0, 0)),
            pl.BlockSpec((None, None, None, nq * AUG_ROWS, Q_BLOCK), lambda bb, gg, i, bits: (bb, gg, i, 0, 0)),
            pl.BlockSpec((LANES, LANES), lambda bb, gg, i, bits: (0, 0)),
            pl.BlockSpec(slots.shape, lambda bb, gg, i, bits: (0, 0, 0)),
        ],
        out_specs=pl.BlockSpec((Q_BLOCK, rows), lambda bb, gg, i, bits: (bb * nq + i, gg)),
        scratch_shapes=[
            pltpu.VMEM((2 * LANES, width), BF16),
            pltpu.VMEM((1, width), F32),
            pltpu.VMEM((1, width), F32),
            pltpu.VMEM((HEAD_DIM, width), F32),
        ],
    )
    return pl.pallas_call(
        functools.partial(_slc_body, nq=nq, stride=stride),
        grid_spec=grid_spec,
        out_shape=jax.ShapeDtypeStruct((b * s, g * rows), F32),
        compiler_params=_cparams(("parallel", "parallel", "parallel")),
    )(lists, qt, ks, vst, sel, eye, slots)


def _merge_body(x_ref, g_ref, oa0, la0, oa1, la1, oa2, la2, ob_ref, ocmp_ref, oslc_ref, owin_ref, cg_ref,
                p4t_ref, p16t_ref, ex_ref, wg0_ref, wg1_ref, wg2_ref, wa_ref, wb_ref, wc_ref,
                out_ref, h_ref, oall_ref):
    @pl.when(pl.program_id(1) == 0)
    def _():
        h_ref[...] = _rms_rows(x_ref[...], g_ref[...])

        def natural(ref, pt_ref):
            hi, lo = _split(ref[...].reshape(TM, A_OUT))
            return _dot(pt_ref[...], hi) + _dot(pt_ref[...], lo)

        o0, l0 = oa0[...], la0[...]
        o1, l1 = natural(oa1, p4t_ref), natural(la1, p4t_ref)
        o2, l2 = natural(oa2, p16t_ref), natural(la2, p16t_ref)
        mx = jnp.maximum(jnp.maximum(l0, l1), l2)
        e0, e1, e2 = jnp.exp(l0 - mx), jnp.exp(l1 - mx), jnp.exp(l2 - mx)
        oall_ref[:, 0:A_OUT] = ((e0 * o0 + e1 * o1 + e2 * o2) / (e0 + e1 + e2)).astype(BF16)
        oall_ref[:, A_OUT:A_OUT + B_HEADS * HEAD_DIM] = ob_ref[...].astype(BF16)
        cg_split = jnp.concatenate(_split(cg_ref[...]), axis=1)
        o_c = None
        for w, o_ref in enumerate((ocmp_ref, oslc_ref, owin_ref)):
            term = _dot(cg_split, ex_ref[w]) * o_ref[...]
            o_c = term if o_c is None else o_c + term
        oall_ref[:, A_OUT + B_HEADS * HEAD_DIM:] = o_c.astype(BF16)

    h = h_ref[...]
    c0, c1 = A_OUT, A_OUT + B_HEADS * HEAD_DIM
    merged = jax.nn.sigmoid(_dot(h, wg0_ref[...])) * _dot(oall_ref[:, 0:c0], wa_ref[...])
    merged += jax.nn.sigmoid(_dot(h, wg1_ref[...])) * _dot(oall_ref[:, c0:c1], wb_ref[...])
    merged += jax.nn.sigmoid(_dot(h, wg2_ref[...])) * _dot(oall_ref[:, c1:], wc_ref[...])
    out_ref[...] = merged.astype(BF16)


def _merge(x, g, a_outs, ob, ocmp, oslc, owin, cg, w_gate, wa, wb, wc, ex, *, tn=512):
    t, d = x.shape
    per16 = CHUNK16 // TM
    n_t = d // tn
    (oa0, la0), (oa1, la1), (oa2, la2) = a_outs

    def rows(a):
        return pl.BlockSpec((TM, a.shape[1]), lambda i, n: (i, 0))

    a1_spec = pl.BlockSpec((None, 4, Q_BLOCK, A_OUT), lambda i, n: (i, 0, 0, 0))
    a2_spec = pl.BlockSpec((None, 16, TM // 16, A_OUT), lambda i, n: (i // per16, 0, i % per16, 0))
    p4t = jnp.asarray(_deinterleave(TM, 4).T, BF16)
    p16t = jnp.asarray(_deinterleave(TM, 16).T, BF16)
    in_specs = [
        rows(x), _resident((1, d)),
        rows(oa0), rows(la0), a1_spec, a1_spec, a2_spec, a2_spec,
        rows(ob), rows(ocmp), rows(oslc), rows(owin), rows(cg),
        _resident((TM, TM)), _resident((TM, TM)), _resident(ex.shape),
        pl.BlockSpec((d, tn), lambda i, n: (0, n)),
        pl.BlockSpec((d, tn), lambda i, n: (0, n + n_t)),
        pl.BlockSpec((d, tn), lambda i, n: (0, n + 2 * n_t)),
        pl.BlockSpec((wa.shape[0], tn), lambda i, n: (0, n)),
        pl.BlockSpec((wb.shape[0], tn), lambda i, n: (0, n)),
        pl.BlockSpec((wc.shape[0], tn), lambda i, n: (0, n)),
    ]
    return pl.pallas_call(
        _merge_body,
        grid=(t // TM, n_t),
        in_specs=in_specs,
        out_specs=pl.BlockSpec((TM, tn), lambda i, n: (i, n)),
        out_shape=jax.ShapeDtypeStruct((t, d), BF16),
        scratch_shapes=[pltpu.VMEM((TM, d), BF16), pltpu.VMEM((TM, wa.shape[0] + wb.shape[0] + wc.shape[0]), BF16)],
        compiler_params=_cparams(("parallel", "arbitrary")),
    )(x, g.reshape(1, d), oa0, la0, oa1, la1, oa2, la2, ob, ocmp, oslc, owin, cg, p4t, p16t, ex,
      w_gate, w_gate, w_gate, wa, wb, wc)


def _out_body(x_ref, m_ref, w_ref, o_ref):
    o_ref[...] = x_ref[...] + _dot(m_ref[...], w_ref[...])


def _out_proj(x, merged, w_out):
    t, d = x.shape
    rows = pl.BlockSpec((TM, d), lambda i: (i, 0))
    return pl.pallas_call(
        _out_body,
        grid=(t // TM,),
        in_specs=[rows, rows, _resident((d, d))],
        out_specs=rows,
        out_shape=jax.ShapeDtypeStruct((t, d), F32),
        compiler_params=_cparams(("parallel",)),
    )(x, merged, w_out)


def _qkv_column_params(qk_gain):
    flag, gain, scale = [], [], []
    one = jnp.ones((HEAD_DIM,), F32)

    def add(n_heads, normed, is_q, gvec, units=1.0):
        for _ in range(n_heads):
            flag.append(np.full((HEAD_DIM,), 1.0 if normed else 0.0, np.float32))
            gain.append(gvec if normed else one)
            scale.append(np.full((HEAD_DIM,), units * HEAD_DIM ** -0.5 if is_q else 1.0, np.float32))

    for _ in range(len(A_GROUPS)):
        add(A_HEADS_PER_GROUP, True, True, qk_gain[0, 0])
        add(A_HEADS_PER_GROUP, True, False, qk_gain[0, 1])
        add(A_HEADS_PER_GROUP, False, False, one)
    add(B_HEADS, True, True, qk_gain[1, 0])
    add(B_KV_HEADS, True, False, qk_gain[1, 1])
    add(B_KV_HEADS, False, False, one)
    add(C_HEADS, True, True, qk_gain[2, 0], units=LOG2E)
    for normed in (False, False, True, False, True, False):
        add(C_KV_HEADS, normed, False, qk_gain[2, 1])
    flag = np.concatenate(flag)
    assert flag.shape[0] == QKV_COLS
    return jnp.asarray(flag), jnp.concatenate(gain) * jnp.asarray(np.concatenate(scale))


def _overlap_t(n_slc, n_pad, n_cmp):
    n = np.arange(n_pad)[None, :]
    j = np.arange(n_slc)[:, None]
    start, end = CMP_STRIDE * n, CMP_STRIDE * n + CMP_BLOCK - 1
    ov = (start <= SLC_BLOCK * j + SLC_BLOCK - 1) & (end >= SLC_BLOCK * j) & (n < n_cmp)
    return jnp.asarray(ov, BF16)


def _gate_expand():
    ex = np.zeros((3, LANES, C_HEADS * HEAD_DIM), np.float32)
    for w in range(3):
        for h in range(C_HEADS):
            ex[w, h * 3 + w, h * HEAD_DIM:(h + 1) * HEAD_DIM] = 1.0
    return jnp.asarray(np.concatenate([ex, ex], axis=1), BF16)


def _compress_weights(cmp_pos, cmp_w1, cmp_w2):
    n_q = 2 * C_KV_HEADS
    w1 = cmp_w1.reshape(2, 2, CMP_STRIDE, HEAD_DIM, CMP_HIDDEN)
    w1q = jnp.repeat(w1, C_KV_HEADS, axis=0)
    wb = jnp.einsum("qhcdn,qp->hcqdpn", w1q, jnp.eye(n_q, dtype=F32))
    wb = wb.reshape(2, CMP_STRIDE, n_q * HEAD_DIM, n_q * CMP_HIDDEN).astype(BF16)
    pos = cmp_pos.reshape(2, 2, CMP_STRIDE, HEAD_DIM)
    prow = jnp.repeat(pos, C_KV_HEADS, axis=0).transpose(1, 2, 0, 3).reshape(2, CMP_STRIDE, 1, n_q * HEAD_DIM)
    prow = jnp.broadcast_to(prow, (2, CMP_STRIDE, 8, n_q * HEAD_DIM)).astype(BF16)
    eye_g = jnp.eye(C_KV_HEADS, dtype=F32)
    w2k = jnp.kron(eye_g, cmp_w2[0]).astype(BF16)
    w2vt = jnp.kron(eye_g, cmp_w2[1]).T.astype(BF16)
    return wb, prow, w2k, w2vt


def _token_mixing(x, b, s, mix_norm, w_in, qk_gain, sinks, cmp_pos, cmp_w1, cmp_w2, w_a, w_b, w_c):
    t, d = x.shape
    assert s % CHUNK16 == 0 and d % 512 == 0
    c_gate_cols = 3 * C_HEADS
    w_qkv = w_in[:, :QKV_COLS + LANES].astype(BF16)
    flag, gs = _qkv_column_params(qk_gain)
    a0, a1, a2, bsec, cq, ckv, cmpd, qt, ks, vst, cg = _qkv_proj(x, mix_norm, w_qkv, flag, gs, b, s)

    a_outs = [_dilated_group(a0.reshape(b, s, SEC), 0, b, s), _dilated_group(a1, 1, b, s),
              _dilated_group(a2, 2, b, s)]
    a_outs[0] = tuple(v.reshape(t, A_OUT) for v in a_outs[0])
    o_b = _sink_swa(bsec.reshape(b, s, SEC), sinks.astype(F32), b, s).reshape(t, -1)
    o_win = _nsa_window(cq.reshape(b, s, SEC), ckv.reshape(b, s, SEC), b, s).reshape(t, -1)

    n_chunks = s // CMP_STRIDE
    n_cmp = (s - CMP_BLOCK) // CMP_STRIDE + 1
    n_slc = s // SLC_BLOCK
    nq = s // Q_BLOCK
    kg = jnp.tile(qk_gain[2, 1], C_KV_HEADS).reshape(1, LANES)
    kc, vct = _compress(cmpd, *_compress_weights(cmp_pos, cmp_w1, cmp_w2), kg)
    o_cmp, sel, cnt = _cmp_select(qt, kc, vct, _overlap_t(n_slc, n_chunks, n_cmp), n_cmp=n_cmp)

    act = (cnt[:, :, :, 0, :] > 0).reshape(b, C_KV_HEADS, nq, nq, 2).any(axis=-1)
    act = act & (jnp.arange(nq)[None, :] < jnp.arange(nq)[:, None])
    order = jnp.argsort(jnp.logical_not(act), axis=-1, stable=True).astype(jnp.int32)
    count = jnp.sum(act, axis=-1, dtype=jnp.int32)[..., None]
    lists = jnp.concatenate([count, order] + [jnp.zeros_like(count)] * (SLC_GROUP - 1), axis=-1)
    o_slc = _slc_attention(lists.reshape(-1), qt, ks.reshape(b, s, 2 * LANES),
                           vst.reshape(b, nq, C_KV_HEADS, HEAD_DIM, Q_BLOCK), sel, stride=nq + SLC_GROUP)

    return _merge(x, mix_norm, a_outs, o_b, o_cmp, o_slc, o_win, cg,
                  w_in[:, QKV_COLS + c_gate_cols:].astype(BF16), w_a.astype(BF16), w_b.astype(BF16),
                  w_c.astype(BF16), _gate_expand())


def kernel(x, ffn1_norm, ffn1_w_gu, ffn1_w_down, mix_norm, w_in, qk_gain, sinks, cmp_pos, cmp_w1, cmp_w2,
           w_branch_a, w_branch_b, w_branch_c, w_out, ffn2_norm, ffn2_w_gu, ffn2_w_down):
    b, s, d = x.shape
    h = x.reshape(b * s, d)
    w1_gu, w1_down = ffn1_w_gu.astype(BF16), ffn1_w_down.astype(BF16)
    w2_gu, w2_down = ffn2_w_gu.astype(BF16), ffn2_w_down.astype(BF16)
    for l in range(ffn1_norm.shape[0]):
        h = _ffn(h, ffn1_norm[l], w1_gu, w1_down, l)
        merged = _token_mixing(h, b, s, mix_norm[l], w_in[l], qk_gain[l], sinks[l], cmp_pos[l], cmp_w1[l],
                               cmp_w2[l], w_branch_a[l], w_branch_b[l], w_branch_c[l])
        h = _out_proj(h, merged, w_out[l].astype(BF16))
        h = _ffn(h, ffn2_norm[l], w2_gu, w2_down, l)
    return h.reshape(b, s, d)
```

```python
import functools
import math

import numpy as np
import jax
import jax.numpy as jnp
from jax import lax
from jax.experimental import pallas as pl
from jax.experimental.pallas import tpu as pltpu

F32 = jnp.float32
BF16 = jnp.bfloat16

HEAD_DIM = 64
Q_BLOCK = 128
LANES = 128
A_GROUPS = ((128, 1), (512, 4), (2048, 16))
A_HEADS_PER_GROUP = 4
A_HEADS = 12
A_OUT = A_HEADS_PER_GROUP * HEAD_DIM
B_HEADS = 8
B_KV_HEADS = 2
B_WINDOW = 128
C_HEADS = 12
C_KV_HEADS = 2
C_REP = C_HEADS // C_KV_HEADS
CMP_BLOCK = 32
CMP_STRIDE = 16
CMP_HIDDEN = 256
SLC_BLOCK = 64
SLC_TOPK = 16
C_WINDOW = 512
RMS_EPS = 1e-6
NEG_INF = -1e30
SEC = 768
N_SEC = 6
QKV_COLS = SEC * N_SEC
TM = 512
CHUNK16 = Q_BLOCK * 16
VMEM_LIMIT = 56 * 1024 * 1024


def _slopes(n):
    return [float(2.0 ** (-8.0 * (h + 1) / n)) for h in range(n)]


def _cparams(sem):
    return pltpu.CompilerParams(dimension_semantics=sem, vmem_limit_bytes=VMEM_LIMIT)


def _dot(a, b):
    return jnp.dot(a, b, preferred_element_type=F32)


def _nt_dot(a, b):
    return lax.dot_general(a, b, (((1,), (1,)), ((), ())), preferred_element_type=F32)


def _split(v):
    hi = v.astype(BF16)
    return hi, (v - hi.astype(F32)).astype(BF16)


def _resident(shape):
    return pl.BlockSpec(shape, lambda *_: (0,) * len(shape), pipeline_mode=pl.Buffered(1))


def _rms_rows(x, g):
    ms = jnp.mean(x * x, axis=-1, keepdims=True)
    return (x * lax.rsqrt(ms + RMS_EPS) * g).astype(BF16)


def _deinterleave(n, d):
    p = np.zeros((n, n), np.float32)
    r = np.arange(n // d)
    for c in range(d):
        p[c * (n // d) + r, d * r + c] = 1.0
    return p


AUG_POS = 0
AUG_ROWS = 16
MASK_BIG = 1e30
SLC_GROUP = 7
LOG2E = math.log2(math.e)


def _key_pattern(n):
    pat = np.zeros((n, LANES), np.float32)
    pat[:, AUG_POS] = pat[:, AUG_POS + 1] = np.arange(n) % Q_BLOCK
    return pat


def _slot_pattern():
    pat = np.zeros((SLC_GROUP, Q_BLOCK, 2 * LANES), np.float32)
    r = np.arange(Q_BLOCK)
    for u in range(SLC_GROUP):
        pat[u, :, LANES + AUG_ROWS * (u + 1)] = r < SLC_BLOCK
        pat[u, :, LANES + AUG_ROWS * (u + 1) + 1] = r >= SLC_BLOCK
    return pat


def _block_spread(n_tiles):
    m = np.zeros((n_tiles * AUG_ROWS, 2 * n_tiles), np.float32)
    jj = np.arange(n_tiles)
    for e in range(2):
        m[AUG_ROWS * jj + e, 2 * jj + e] = 1.0
    return m


def _ffn_body(x_ref, g_ref, wg_ref, wu_ref, wd_ref, o_ref, h_ref, *, n_f):
    f = pl.program_id(1)

    @pl.when(f == 0)
    def _():
        h_ref[...] = _rms_rows(x_ref[...], g_ref[...])
        o_ref[...] = jnp.zeros_like(o_ref)

    h = h_ref[...]
    gate = _dot(h, wg_ref[...])
    up = _dot(h, wu_ref[...])
    act = (gate * jax.nn.sigmoid(gate) * up).astype(BF16)
    o_ref[...] += _dot(act, wd_ref[...])

    @pl.when(f == n_f - 1)
    def _():
        o_ref[...] = x_ref[...] + 0.5 * o_ref[...]


def _ffn(x, g, w_gu, w_down, layer, *, tm=1024, tf=512):
    t, d = x.shape
    d_ff = w_down.shape[1]
    n_f = d_ff // tf
    rows = pl.BlockSpec((tm, d), lambda i, f: (i, 0))
    return pl.pallas_call(
        functools.partial(_ffn_body, n_f=n_f),
        grid=(t // tm, n_f),
        in_specs=[
            rows,
            pl.BlockSpec((1, d), lambda i, f: (0, 0)),
            pl.BlockSpec((None, d, tf), lambda i, f: (layer, 0, f)),
            pl.BlockSpec((None, d, tf), lambda i, f: (layer, 0, f + n_f)),
            pl.BlockSpec((None, tf, d), lambda i, f: (layer, f, 0)),
        ],
        out_specs=rows,
        out_shape=jax.ShapeDtypeStruct((t, d), F32),
        scratch_shapes=[pltpu.VMEM((tm, d), BF16)],
        compiler_params=_cparams(("parallel", "arbitrary")),
    )(x, g.reshape(1, d), w_gu, w_gu, w_down)


NORM_TILE = 256
NORM_TILES = {0: (0, 1), 1: (0, 1), 2: (0, 1), 3: (0, 1, 2), 4: (0, 1, 2), 5: (1, 2)}


def _head_sumsq(y, bd):
    return _dot((y * y).astype(BF16), bd)


def _qkv_body(x_ref, g_ref, w_ref, flag_ref, gs_ref, bd_ref, p4_ref, p16_ref, eye_ref, kpat_ref,
              a0_ref, a1_ref, a2_ref, b_ref, cq_ref, ckv_ref, cmpd_ref, qt_ref, ks_ref, vst_ref, cg_ref):
    h = _rms_rows(x_ref[...], g_ref[...])
    bd = bd_ref[...]

    def section(k):
        sl = slice(k * SEC, (k + 1) * SEC)
        if k < len(A_GROUPS):
            y = jnp.concatenate([_dot(h, w_ref[:, part * A_HEADS * HEAD_DIM + k * A_OUT:
                                                  part * A_HEADS * HEAD_DIM + (k + 1) * A_OUT])
                                 for part in range(3)], axis=1)
        else:
            y = _dot(h, w_ref[:, sl])
        tiles = []
        for c in range(SEC // NORM_TILE):
            yc = y[:, c * NORM_TILE:(c + 1) * NORM_TILE]
            if c in NORM_TILES[k]:
                cols = slice(k * SEC + c * NORM_TILE, k * SEC + (c + 1) * NORM_TILE)
                inv = lax.rsqrt(_head_sumsq(yc, bd) * (1.0 / HEAD_DIM) + RMS_EPS)
                yc = yc * jnp.where(flag_ref[:, cols] > 0, inv, 1.0) * gs_ref[:, cols]
            tiles.append(yc.astype(BF16))
        return jnp.concatenate(tiles, axis=1)

    a0_ref[...] = section(0)
    a1_ref[...] = _dot(p4_ref[...], section(1)).astype(BF16).reshape(a1_ref.shape)
    a2_ref[...] = _dot(p16_ref[...], section(2)).astype(BF16).reshape(a2_ref.shape)
    b_ref[...] = section(3)
    y_cq = section(4)
    cq_ref[...] = y_cq
    qt_ref[...] = _nt_dot(eye_ref[...], y_cq).astype(BF16)
    y_ckv = section(5)
    ckv_ref[...] = y_ckv
    cmpd_ref[...] = _dot(p16_ref[...], y_ckv[:, 0:2 * LANES]).astype(BF16).reshape(cmpd_ref.shape)
    ks_ref[:, 0:LANES] = y_ckv[:, 2 * LANES:3 * LANES]
    ks_ref[:, LANES:2 * LANES] = kpat_ref[...]
    eye = eye_ref[0:LANES, 0:LANES]
    for kb in range(vst_ref.shape[0]):
        vt = _nt_dot(eye, y_ckv[kb * Q_BLOCK:(kb + 1) * Q_BLOCK, 3 * LANES:4 * LANES]).astype(BF16)
        for gg in range(C_KV_HEADS):
            vst_ref[kb, gg] = vt[gg * HEAD_DIM:(gg + 1) * HEAD_DIM]
    cg_ref[...] = jax.nn.sigmoid(_dot(h, w_ref[:, QKV_COLS:QKV_COLS + LANES]))


def _qkv_proj(x, g, w, flag, gs, b, s):
    t, d = x.shape
    tiles_per_batch = s // TM
    per16 = CHUNK16 // TM
    bd = jnp.asarray(np.kron(np.eye(NORM_TILE // HEAD_DIM), np.ones((HEAD_DIM, HEAD_DIM))), BF16)
    p4 = jnp.asarray(_deinterleave(TM, 4), BF16)
    p16 = jnp.asarray(_deinterleave(TM, 16), BF16)
    eye = jnp.asarray(np.eye(SEC), BF16)
    nat = pl.BlockSpec((TM, SEC), lambda i: (i, 0))
    out_specs = [
        nat,
        pl.BlockSpec((None, 4, Q_BLOCK, SEC), lambda i: (i, 0, 0, 0)),
        pl.BlockSpec((None, 16, TM // 16, SEC), lambda i: (i // per16, 0, i % per16, 0)),
        nat, nat, nat,
        pl.BlockSpec((None, 16, TM // 16, 2 * LANES), lambda i: (i // tiles_per_batch, 0, i % tiles_per_batch, 0)),
        pl.BlockSpec((None, SEC, TM), lambda i: (i // tiles_per_batch, 0, i % tiles_per_batch)),
        pl.BlockSpec((TM, 2 * LANES), lambda i: (i, 0)),
        pl.BlockSpec((TM // Q_BLOCK, C_KV_HEADS, HEAD_DIM, Q_BLOCK), lambda i: (i, 0, 0, 0)),
        pl.BlockSpec((TM, LANES), lambda i: (i, 0)),
    ]
    out_shape = [
        jax.ShapeDtypeStruct((t, SEC), BF16),
        jax.ShapeDtypeStruct((t // TM, 4, Q_BLOCK, SEC), BF16),
        jax.ShapeDtypeStruct((t // CHUNK16, 16, Q_BLOCK, SEC), BF16),
        jax.ShapeDtypeStruct((t, SEC), BF16),
        jax.ShapeDtypeStruct((t, SEC), BF16),
        jax.ShapeDtypeStruct((t, SEC), BF16),
        jax.ShapeDtypeStruct((b, 16, s // 16, 2 * LANES), BF16),
        jax.ShapeDtypeStruct((b, SEC, s), BF16),
        jax.ShapeDtypeStruct((t, 2 * LANES), BF16),
        jax.ShapeDtypeStruct((t // Q_BLOCK, C_KV_HEADS, HEAD_DIM, Q_BLOCK), BF16),
        jax.ShapeDtypeStruct((t, LANES), F32),
    ]
    n_w = w.shape[1]
    return pl.pallas_call(
        _qkv_body,
        grid=(t // TM,),
        in_specs=[
            pl.BlockSpec((TM, d), lambda i: (i, 0)),
            _resident((1, d)),
            _resident((d, n_w)),
            _resident((1, QKV_COLS)),
            _resident((1, QKV_COLS)),
            _resident((NORM_TILE, NORM_TILE)),
            _resident((TM, TM)),
            _resident((TM, TM)),
            _resident((SEC, SEC)),
            _resident((TM, LANES)),
        ],
        out_specs=out_specs,
        out_shape=out_shape,
        compiler_params=_cparams(("parallel",)),
    )(x, g.reshape(1, d), w, flag.reshape(1, -1), gs.reshape(1, -1), bd, p4, p16, eye,
      jnp.asarray(_key_pattern(TM), BF16))


def _banded_body(*refs, nb, qb, keys, heads, k_off, v_off, n_pairs, q_axis, use_sinks, with_lse, stack, fold,
                 log2_units):
    refs = list(refs)
    q_ref = refs.pop(0)
    kv_refs = [refs.pop(0) for _ in range({"self": 1, "window": 2, "blocks": nb + 1}[keys])]
    qc_ref, kaug_ref, band_ref = refs.pop(0), refs.pop(0), refs.pop(0)
    sink_ref = refs.pop(0) if use_sinks else None
    o_ref = refs.pop(0)
    lse_ref = refs.pop(0) if with_lse else None

    i = pl.program_id(q_axis)
    nk = (nb + 1) * Q_BLOCK
    col = lax.broadcasted_iota(jnp.int32, (Q_BLOCK, nk), 1)
    band = band_ref[...]

    def start_mask(sub):
        if sub > 0 and sub >= nb:
            return band
        return band + jnp.where(col < (nb - (i * qb + sub)) * Q_BLOCK, -MASK_BIG, 0.0)

    rel_f = (nb * Q_BLOCK + lax.broadcasted_iota(jnp.int32, (Q_BLOCK, nk), 0) - col).astype(F32)
    lane = lax.broadcasted_iota(jnp.int32, (Q_BLOCK, LANES), 1)
    low_half = lane < HEAD_DIM
    kaug = kaug_ref[...]

    def q_cols(sub, c0):
        if len(q_ref.shape) == 3:
            return q_ref[sub, :, c0:c0 + LANES]
        return q_ref[sub * Q_BLOCK:(sub + 1) * Q_BLOCK, c0:c0 + LANES]

    kv_cache = {}

    def swap_halves(tile, swapped):
        return pltpu.roll(tile.astype(F32), HEAD_DIM, 1).astype(BF16) if swapped else tile

    def kv_tile(sub, off, kv_pair, swapped):
        c0 = off + kv_pair * LANES
        if keys == "window":
            key = (off, kv_pair, swapped)
            if key not in kv_cache:
                kv_cache[key] = swap_halves(jnp.concatenate([r[:, c0:c0 + LANES] for r in kv_refs], axis=0), swapped)
            first = qb - nb + sub
            return kv_cache[key][first * Q_BLOCK:(first + nb + 1) * Q_BLOCK]
        key = (sub, off, kv_pair, swapped)
        if key not in kv_cache:
            if keys == "self":
                blocks = [kv_refs[0][:, c0:c0 + LANES] if sub == 0 else q_cols(sub - 1, c0), q_cols(sub, c0)]
            else:
                blocks = [r[:, c0:c0 + LANES] for r in kv_refs]
            kv_cache[key] = swap_halves(jnp.concatenate(blocks, axis=0), swapped)
        return kv_cache[key]

    classes = {}
    for head in heads:
        pair, half, kv_pair, kv_half, slope, hidx = head
        key = (kv_pair, kv_half != half) if stack else hidx
        classes.setdefault(key, []).append(head)

    outs = [[[None, None] for _ in range(n_pairs)] for _ in range(qb)]
    lses = [[[None, None] for _ in range(n_pairs)] for _ in range(qb)]
    groups = [(sub, members) for sub in range(qb) for members in classes.values()]

    scores = []
    for sub, members in groups:
        n_h = len(members)
        kv_pair, swapped = members[0][2], members[0][3] != members[0][1]
        mask = start_mask(sub)
        q_rows = []
        for pair, half, _, _, _, hidx in members:
            qp = q_cols(sub, pair * LANES)
            own = low_half if half == 0 else jnp.logical_not(low_half)
            qm = jnp.where(own, qp, jnp.zeros_like(qp))
            q_rows.append(jnp.concatenate([qm, qc_ref[hidx]], axis=1) if fold else qm)
        if fold:
            k_aug = jnp.concatenate([kv_tile(sub, k_off, kv_pair, swapped), kaug], axis=1)
            s = _nt_dot(jnp.concatenate(q_rows, axis=0), k_aug)
            s = (s.reshape(n_h, Q_BLOCK, nk) + mask[None]).reshape(n_h * Q_BLOCK, nk)
        else:
            bias = jnp.concatenate([mask - member[4] * rel_f for member in members], axis=0)
            s = _nt_dot(jnp.concatenate(q_rows, axis=0), kv_tile(sub, k_off, kv_pair, swapped)) + bias
        scores.append(s)

    probs = []
    for (_, members), s in zip(groups, scores):
        m = jnp.max(s, axis=1, keepdims=True)
        if use_sinks:
            assert len(members) == 1
            sink = sink_ref[members[0][5]]
            m = jnp.maximum(m, sink)
        p = jnp.exp2(s - m) if log2_units else jnp.exp(s - m)
        den = jnp.sum(p, axis=1, keepdims=True)
        if use_sinks:
            den = den + jnp.exp(sink - m)
        probs.append((p.astype(BF16), m, den))

    for (sub, members), (p, m, den) in zip(groups, probs):
        kv_pair, swapped = members[0][2], members[0][3] != members[0][1]
        r = _dot(p, kv_tile(sub, v_off, kv_pair, swapped)) / den
        lse = m + jnp.log(den) if with_lse else None
        for k, (pair, half, _, _, _, _) in enumerate(members):
            outs[sub][pair][half] = r[k * Q_BLOCK:(k + 1) * Q_BLOCK]
            if with_lse:
                lses[sub][pair][half] = jnp.broadcast_to(lse[k * Q_BLOCK:(k + 1) * Q_BLOCK], (Q_BLOCK, LANES))

    def store(ref, sub, sl, value):
        if len(ref.shape) == 3:
            ref[sub, :, sl] = value
        else:
            ref[sub * Q_BLOCK:(sub + 1) * Q_BLOCK, sl] = value

    for sub in range(qb):
        for pair in range(n_pairs):
            sl = slice(pair * LANES, (pair + 1) * LANES)
            store(o_ref, sub, sl, jnp.where(low_half, outs[sub][pair][0], outs[sub][pair][1]))
            if with_lse:
                store(lse_ref, sub, sl, jnp.where(low_half, lses[sub][pair][0], lses[sub][pair][1]))


def _banded_consts(heads, nb, max_dist):
    nk = (nb + 1) * Q_BLOCK
    slope = np.asarray([h[4] for h in heads], np.float32)[:, None]
    q_dist = (nb * Q_BLOCK + np.arange(Q_BLOCK, dtype=np.float32))[None, :]
    ones = np.ones_like(q_dist)
    vals = jnp.asarray(np.stack([slope * ones, slope * Q_BLOCK * ones, -slope * q_dist], axis=-1))
    hi = vals.astype(BF16)
    lo = (vals - hi.astype(F32)).astype(BF16)
    cols = jnp.stack([hi[..., 0], lo[..., 0], hi[..., 1], lo[..., 1], hi[..., 2], lo[..., 2]], axis=-1)
    qc = jnp.pad(cols, ((0, 0), (0, 0), (0, LANES - cols.shape[-1])))
    kaug = np.zeros((nk, LANES), np.float32)
    kaug[:, 0] = kaug[:, 1] = np.arange(nk) % Q_BLOCK
    kaug[:, 2] = kaug[:, 3] = np.arange(nk) // Q_BLOCK
    kaug[:, 4] = kaug[:, 5] = 1.0
    rel = nb * Q_BLOCK + np.arange(Q_BLOCK)[:, None] - np.arange(nk)[None, :]
    band = np.where((rel >= 0) & (rel <= max_dist), 0.0, -MASK_BIG).astype(np.float32)
    return [qc, jnp.asarray(kaug, BF16), jnp.asarray(band)]


BAND_QB = 4
CMP_QB = 1


def _banded_call(q_arr, kv_arr, *, grid, q_spec, kv_specs, out_spec, out_shape, out_cols, nb, qb, max_dist, heads,
                 k_off, v_off, q_axis, sinks=None, with_lse=False, stack=False, fold=False, log2_units=False):
    assert not (log2_units and (with_lse or sinks is not None))
    if len(kv_specs) == 1 and nb == 1:
        keys = "self"
    elif qb > 1:
        keys = "window"
        assert len(kv_specs) == 2 and nb <= qb
    else:
        keys = "blocks"
        assert len(kv_specs) == nb + 1
    consts = _banded_consts(heads, nb, max_dist)
    in_specs = [q_spec] + list(kv_specs) + [pl.BlockSpec(c.shape, lambda *_, nd=c.ndim: (0,) * nd) for c in consts]
    args = [q_arr] + [kv_arr] * len(kv_specs) + consts
    if sinks is not None:
        in_specs.append(pl.BlockSpec(memory_space=pltpu.SMEM))
        args.append(sinks)
    oshape = jax.ShapeDtypeStruct(out_shape, F32)
    body = functools.partial(_banded_body, nb=nb, qb=qb, keys=keys, heads=heads, k_off=k_off, v_off=v_off,
                             n_pairs=out_cols // LANES, q_axis=q_axis, use_sinks=sinks is not None,
                             with_lse=with_lse, stack=stack, fold=fold, log2_units=log2_units)
    return pl.pallas_call(
        body,
        grid=grid,
        in_specs=in_specs,
        out_specs=[out_spec, out_spec] if with_lse else out_spec,
        out_shape=[oshape, oshape] if with_lse else oshape,
        compiler_params=_cparams(("parallel",) * len(grid)),
    )(*args)


def _row_specs(b, s, qb, out_cols):
    q_spec = pl.BlockSpec((None, qb * Q_BLOCK, SEC), lambda bb, i: (bb, i, 0))
    prev = pl.BlockSpec((None, Q_BLOCK, SEC), lambda bb, i: (bb, jnp.maximum(qb * i - 1, 0), 0))
    out_spec = pl.BlockSpec((None, qb * Q_BLOCK, out_cols), lambda bb, i: (bb, i, 0))
    return dict(grid=(b, s // (qb * Q_BLOCK)), q_axis=1, q_spec=q_spec, kv_specs=[prev], out_spec=out_spec,
                out_shape=(b, s, out_cols), qb=qb)


def _dilated_group(arr, gi, b, s):
    window, dil = A_GROUPS[gi]
    slopes = _slopes(A_HEADS)
    heads = tuple((hh // 2, hh % 2, hh // 2, hh % 2, slopes[gi * A_HEADS_PER_GROUP + hh] * dil, hh)
                  for hh in range(A_HEADS_PER_GROUP))
    common = dict(out_cols=A_OUT, nb=1, max_dist=window // dil, heads=heads, k_off=256, v_off=512,
                  with_lse=True, stack=False)
    if dil == 1:
        return _banded_call(arr, arr, **_row_specs(b, s, BAND_QB, A_OUT), **common)
    nc = s // (Q_BLOCK * dil)
    qb = BAND_QB if nc % BAND_QB == 0 else 1
    q_spec = pl.BlockSpec((qb, None, Q_BLOCK, SEC), lambda bb, c, i: ((bb * nc) // qb + i, c, 0, 0))
    prev = pl.BlockSpec((None, None, Q_BLOCK, SEC), lambda bb, c, i: (bb * nc + jnp.maximum(qb * i - 1, 0), c, 0, 0))
    out_spec = pl.BlockSpec((qb, None, Q_BLOCK, A_OUT), lambda bb, c, i: ((bb * nc) // qb + i, c, 0, 0))
    return _banded_call(arr, arr, grid=(b, dil, nc // qb), q_axis=2, q_spec=q_spec, kv_specs=[prev],
                        out_spec=out_spec, out_shape=(b * nc, dil, Q_BLOCK, A_OUT), qb=qb, **common)


def _sink_swa(arr, sinks, b, s):
    slopes = _slopes(B_HEADS)
    rep = B_HEADS // B_KV_HEADS
    heads = tuple((h // 2, h % 2, 0, h // rep, slopes[h], h) for h in range(B_HEADS))
    return _banded_call(arr, arr, **_row_specs(b, s, BAND_QB, B_HEADS * HEAD_DIM), out_cols=B_HEADS * HEAD_DIM,
                        nb=1, max_dist=B_WINDOW - 1, heads=heads, k_off=512, v_off=640, sinks=sinks)


def _nsa_window(cq, ckv, b, s):
    slopes = _slopes(C_HEADS)
    heads = tuple((h // 2, h % 2, 0, h // C_REP, slopes[h] * LOG2E, h) for h in range(C_HEADS))
    nb = -(-(C_WINDOW - 1) // Q_BLOCK)
    qb = nb
    blk = (None, qb * Q_BLOCK, SEC)
    kv_specs = [pl.BlockSpec(blk, lambda bb, i: (bb, jnp.maximum(i - 1, 0), 0)),
                pl.BlockSpec(blk, lambda bb, i: (bb, i, 0))]
    out_cols = C_HEADS * HEAD_DIM
    return _banded_call(
        cq, ckv, grid=(b, s // (qb * Q_BLOCK)), q_axis=1, qb=qb, log2_units=True,
        q_spec=pl.BlockSpec(blk, lambda bb, i: (bb, i, 0)), kv_specs=kv_specs,
        out_spec=pl.BlockSpec((None, qb * Q_BLOCK, out_cols), lambda bb, i: (bb, i, 0)),
        out_shape=(b, s, out_cols), out_cols=out_cols, nb=nb, max_dist=C_WINDOW - 1, heads=heads,
        k_off=512, v_off=640, stack=True, fold=True)


def _compress_body(t_ref, wb_ref, prow_ref, w2k_ref, w2vt_ref, kg_ref, bd_ref, kc_ref, vct_ref, *, n_chunks):
    hid_cols = 2 * C_KV_HEADS * CMP_HIDDEN
    u = jnp.zeros((n_chunks, hid_cols), F32)
    v = jnp.zeros((n_chunks, hid_cols), F32)
    pc = jnp.zeros((1, hid_cols), F32)
    for c in range(CMP_STRIDE):
        tc = t_ref[c]
        u = u + _dot(tc, wb_ref[0, c])
        v = v + _dot(tc, wb_ref[1, c])
        pc = pc + _dot(prow_ref[0, c], wb_ref[0, c])[0:1] + _dot(prow_ref[1, c], wb_ref[1, c])[0:1]
    hsum = u + pltpu.roll(v, n_chunks - 1, 0) + pc
    hid = (hsum * jax.nn.sigmoid(hsum)).astype(BF16)
    half = C_KV_HEADS * CMP_HIDDEN
    k = _dot(hid[:, :half], w2k_ref[...])
    hi, lo = _split(k * k)
    ss = _dot(hi, bd_ref[...]) + _dot(lo, bd_ref[...])
    kc_ref[...] = (k * lax.rsqrt(ss * (1.0 / HEAD_DIM) + RMS_EPS) * kg_ref[...]).astype(BF16)
    vct_ref[...] = _nt_dot(w2vt_ref[...], hid[:, half:]).astype(BF16)


def _compress(cmpd, wb, prow, w2k, w2vt, kg):
    b, _, n_chunks, width = cmpd.shape
    bd = jnp.asarray(np.kron(np.eye(LANES // HEAD_DIM), np.ones((HEAD_DIM, HEAD_DIM))), BF16)
    return pl.pallas_call(
        functools.partial(_compress_body, n_chunks=n_chunks),
        grid=(b,),
        in_specs=[
            pl.BlockSpec((None, CMP_STRIDE, n_chunks, width), lambda bb: (bb, 0, 0, 0)),
            _resident(wb.shape), _resident(prow.shape), _resident(w2k.shape), _resident(w2vt.shape),
            _resident((1, LANES)), _resident((LANES, LANES)),
        ],
        out_specs=[
            pl.BlockSpec((None, n_chunks, LANES), lambda bb: (bb, 0, 0)),
            pl.BlockSpec((None, LANES, n_chunks), lambda bb: (bb, 0, 0)),
        ],
        out_shape=[
            jax.ShapeDtypeStruct((b, n_chunks, LANES), BF16),
            jax.ShapeDtypeStruct((b, LANES, n_chunks), BF16),
        ],
        compiler_params=_cparams(("parallel",)),
    )(cmpd, wb, prow, w2k, w2vt, kg, bd)


def _to_natural(ot_list, eye):
    pairs = []
    for k in range(0, len(ot_list), 2):
        hi, lo = _split(jnp.concatenate([ot_list[k], ot_list[k + 1]], axis=0))
        pairs.append(_nt_dot(eye, hi) + _nt_dot(eye, lo))
    return jnp.concatenate(pairs, axis=1)


def _cmp_body(qt_ref, kc_ref, vct_ref, ovt_ref, eye_ref, spread_ref, cmask_ref,
              o_ref, selm_ref, cnt_ref, sel_sc, *, n_top):
    g = pl.program_id(1)
    i = pl.program_id(2)
    n_pad = kc_ref.shape[0]
    n_slc = ovt_ref.shape[0]
    qw = qt_ref.shape[1]
    qb = qw // Q_BLOCK
    per_q = Q_BLOCK // CMP_STRIDE
    own_rows = (lax.broadcasted_iota(jnp.int32, (LANES, qw), 0) // HEAD_DIM) == g
    slopes = [sl * LOG2E for sl in _slopes(C_HEADS)]

    def attend(rows):
        kc = kc_ref[0:rows, :]
        vct = vct_ref[:, 0:rows]
        mask = jnp.concatenate(
            [cmask_ref[pl.ds(pl.multiple_of(n_pad - per_q * (qb * i + h), 8), rows), :] for h in range(qb)], axis=1)
        n_f = (CMP_STRIDE * lax.broadcasted_iota(jnp.int32, (rows, qw), 0)).astype(F32)
        psum = jnp.zeros((rows, qw), F32)
        outs = []
        for r in range(C_REP):
            slope = jnp.where(g == 0, slopes[r], slopes[C_REP + r])
            qt = qt_ref[r * HEAD_DIM:(r + 1) * HEAD_DIM, :]
            q_pad = jnp.where(own_rows, jnp.concatenate([qt, qt], axis=0), jnp.zeros((LANES, qw), BF16))
            s = _dot(kc, q_pad) + (slope * n_f + mask)
            m = jnp.maximum(jnp.max(s, axis=0, keepdims=True), -1e20)
            e = jnp.exp2(s - m)
            den = jnp.sum(e, axis=0, keepdims=True)
            p = e * (1.0 / jnp.where(den > 0, den, 1.0))
            psum = psum + p
            both = _dot(vct, p.astype(BF16))
            outs.append(jnp.where(g == 0, both[:HEAD_DIM], both[HEAD_DIM:]))
        o_ref[...] = _to_natural(outs, eye_ref[...])

        n_j = rows * CMP_STRIDE // SLC_BLOCK
        hi, lo = _split(psum)
        ovt = ovt_ref[0:n_j, 0:rows]
        imp = _dot(ovt, hi) + _dot(ovt, lo)
        j_idx = lax.broadcasted_iota(jnp.int32, (n_j, qw), 0)
        t_q = i * qw + lax.broadcasted_iota(jnp.int32, (n_j, qw), 1)
        cur = lax.shift_right_logical(t_q, int(math.log2(SLC_BLOCK)))
        forced = ((j_idx == 0) | (j_idx == cur) | (j_idx == cur - 1)) & (j_idx <= cur)
        v = jnp.where((j_idx <= cur) & jnp.logical_not(forced), imp, -1.0)
        sel = jnp.where(forced, 1.0, 0.0)
        for _ in range(n_top - 3):
            m = jnp.max(v, axis=0, keepdims=True)
            first = jnp.min(jnp.where((v == m) & (m >= 0.0), j_idx, n_slc), axis=0, keepdims=True)
            pick = j_idx == first
            sel = jnp.where(pick, 1.0, sel)
            v = jnp.where(pick, -1.0, v)
        sel_sc[0:n_j, :] = sel
        if n_j < n_slc:
            sel_sc[n_j:, :] = jnp.zeros((n_slc - n_j, qw), F32)

    n_var = n_pad // LANES
    for var in range(n_var):
        pl.when((qb * i + qb - 1) // (LANES // per_q) == var)(functools.partial(attend, (var + 1) * LANES))

    sel = sel_sc[...]
    neg = jnp.where(sel > 0, 0.0, -MASK_BIG).astype(BF16)
    mask_rows = _dot(spread_ref[...], neg).astype(BF16)
    sel_b = sel.astype(BF16)
    for h in range(qb):
        lanes = slice(h * Q_BLOCK, (h + 1) * Q_BLOCK)
        selm_ref[h] = mask_rows[:, lanes]
        cnt_ref[h] = _nt_dot(jnp.ones((8, Q_BLOCK), BF16), sel_b[:, lanes])


def _cmp_select(qt, kc, vct, ovt, *, n_cmp):
    b, _, s = qt.shape
    g = C_KV_HEADS
    n_pad = kc.shape[1]
    n_slc = ovt.shape[0]
    nq = s // Q_BLOCK
    rows = C_REP * HEAD_DIM
    qb = CMP_QB if nq % CMP_QB == 0 else 1
    qw = qb * Q_BLOCK
    steps = nq // qb
    eye = jnp.asarray(np.eye(qw), BF16)
    n_top = min(SLC_TOPK, n_slc)
    assert n_top > 3 and n_cmp == n_pad - 1 and n_pad % LANES == 0
    n_rel = np.arange(-n_pad, n_pad)[:, None]
    cmask = np.where(CMP_STRIDE * n_rel + CMP_BLOCK - 1 <= np.arange(Q_BLOCK)[None, :], 0.0, -MASK_BIG)
    return pl.pallas_call(
        functools.partial(_cmp_body, n_top=n_top),
        grid=(b, g, steps),
        in_specs=[
            pl.BlockSpec((None, rows, qw), lambda bb, gg, i: (bb, gg, i)),
            pl.BlockSpec((None, n_pad, LANES), lambda bb, gg, i: (bb, 0, 0)),
            pl.BlockSpec((None, LANES, n_pad), lambda bb, gg, i: (bb, 0, 0)),
            pl.BlockSpec((n_slc, n_pad), lambda bb, gg, i: (0, 0)),
            pl.BlockSpec((qw, qw), lambda bb, gg, i: (0, 0)),
            pl.BlockSpec((nq * AUG_ROWS, n_slc), lambda bb, gg, i: (0, 0)),
            pl.BlockSpec((2 * n_pad, Q_BLOCK), lambda bb, gg, i: (0, 0)),
        ],
        out_specs=[
            pl.BlockSpec((qw, rows), lambda bb, gg, i: (bb * steps + i, gg)),
            pl.BlockSpec((None, None, qb, nq * AUG_ROWS, Q_BLOCK), lambda bb, gg, i: (bb, gg, i, 0, 0)),
            pl.BlockSpec((None, None, qb, 8, n_slc), lambda bb, gg, i: (bb, gg, i, 0, 0)),
        ],
        out_shape=[
            jax.ShapeDtypeStruct((b * s, g * rows), F32),
            jax.ShapeDtypeStruct((b, g, nq, nq * AUG_ROWS, Q_BLOCK), BF16),
            jax.ShapeDtypeStruct((b, g, nq, 8, n_slc), F32),
        ],
        scratch_shapes=[pltpu.VMEM((n_slc, qw), F32)],
        compiler_params=_cparams(("parallel", "parallel", "parallel")),
    )(qt, kc, vct, ovt, eye, jnp.asarray(_block_spread(nq), BF16), jnp.asarray(cmask, F32))


def _slc_body(list_ref, qt_ref, ks_ref, vst_ref, selm_ref, eye_ref, slot_ref, o_ref, qaug, m_sc, l_sc, acc_sc,
              *, nq, stride):
    bb = pl.program_id(0)
    g = pl.program_id(1)
    i = pl.program_id(2)
    width = C_REP * Q_BLOCK
    slopes = [sl * LOG2E for sl in _slopes(C_HEADS)]
    slope_s = [jnp.where(g == 0, slopes[r], slopes[C_REP + r]) for r in range(C_REP)]

    own_rows = (lax.broadcasted_iota(jnp.int32, (LANES, width), 0) // HEAD_DIM) == g
    q6 = jnp.concatenate([qt_ref[r * HEAD_DIM:(r + 1) * HEAD_DIM, :] for r in range(C_REP)], axis=1)
    qaug[0:LANES, :] = jnp.where(own_rows, jnp.concatenate([q6, q6], axis=0), jnp.zeros((LANES, width), BF16))
    head = lax.broadcasted_iota(jnp.int32, (LANES, width), 1) // Q_BLOCK
    row = lax.broadcasted_iota(jnp.int32, (LANES, width), 0)
    slope_t = jnp.zeros((LANES, width), F32)
    for r in range(C_REP):
        slope_t = jnp.where(head == r, slope_s[r], slope_t)
    s_hi, s_lo = _split(slope_t)
    slope_rows = jnp.where(row == AUG_POS, s_hi.astype(F32), jnp.where(row == AUG_POS + 1, s_lo.astype(F32), 0.0))
    qaug[LANES:, :] = slope_rows.astype(BF16)

    q_loc = lax.broadcasted_iota(jnp.int32, (Q_BLOCK, Q_BLOCK), 1)
    k_loc = lax.broadcasted_iota(jnp.int32, (Q_BLOCK, Q_BLOCK), 0)
    causal = jnp.where(k_loc > q_loc, -MASK_BIG, 0.0)

    m_sc[...] = jnp.full(m_sc.shape, NEG_INF, F32)
    l_sc[...] = jnp.zeros(l_sc.shape, F32)
    acc_sc[...] = jnp.zeros(acc_sc.shape, F32)

    def accumulate(tiles, own_first):
        keys = []
        for u, (jj, _) in enumerate(tiles):
            keys.append(ks_ref[pl.ds(pl.multiple_of(jj * Q_BLOCK, Q_BLOCK), Q_BLOCK), :] + slot_ref[u])
            rows = selm_ref[pl.ds(pl.multiple_of(jj * AUG_ROWS, AUG_ROWS), AUG_ROWS), :]
            lo = LANES + AUG_ROWS * (u + 1)
            qaug[lo:lo + AUG_ROWS, :] = jnp.concatenate([rows] * C_REP, axis=1)
        st = _dot(jnp.concatenate(keys, axis=0), qaug[...])
        ps, alphas = [], []
        for r in range(C_REP):
            sl = slice(r * Q_BLOCK, (r + 1) * Q_BLOCK)
            m_old = m_sc[:, sl]
            m_new = m_old
            parts = []
            for u, (jj, extra) in enumerate(tiles):
                s = st[u * Q_BLOCK:(u + 1) * Q_BLOCK, sl]
                if own_first and u == 0:
                    s = s + causal
                c = slope_s[r] * ((jj - i) * Q_BLOCK).astype(F32) + extra
                m_new = jnp.maximum(m_new, jnp.max(s, axis=0, keepdims=True) + c)
                parts.append((s, c))
            alpha = jnp.exp2(m_old - m_new)
            l_new = alpha * l_sc[:, sl]
            p_rows = []
            for s, c in parts:
                p = jnp.exp2(s + (c - m_new))
                l_new = l_new + jnp.sum(p, axis=0, keepdims=True)
                p_rows.append(p.astype(BF16))
            l_sc[:, sl] = l_new
            m_sc[:, sl] = m_new
            ps.append(jnp.concatenate(p_rows, axis=0))
            alphas.append(alpha)
        values = jnp.concatenate([vst_ref[jj] for jj, _ in tiles], axis=1)
        pv = _dot(values, jnp.concatenate(ps, axis=1))
        acc_sc[...] = jnp.concatenate(alphas, axis=1) * acc_sc[...] + pv

    base = ((bb * pl.num_programs(1) + g) * nq + i) * stride
    count = list_ref[base]

    def listed(slot):
        return list_ref[base + 1 + slot], jnp.where(slot < count, 0.0, -MASK_BIG)

    accumulate([(i, 0.0)] + [listed(u) for u in range(SLC_GROUP - 1)], True)

    def step(k, carry):
        accumulate([listed(SLC_GROUP - 1 + SLC_GROUP * k + u) for u in range(SLC_GROUP)], False)
        return carry

    rest = jnp.maximum(count - (SLC_GROUP - 1), 0)
    lax.fori_loop(0, (rest + SLC_GROUP - 1) // SLC_GROUP, step, 0)
    l = l_sc[...]
    o = acc_sc[...] / jnp.where(l > 0, l, 1.0)
    o_ref[...] = _to_natural([o[:, r * Q_BLOCK:(r + 1) * Q_BLOCK] for r in range(C_REP)], eye_ref[...])


def _slc_attention(lists, qt, ks, vst, sel, *, stride):
    b, _, s = qt.shape
    g = C_KV_HEADS
    nq = s // Q_BLOCK
    rows = C_REP * HEAD_DIM
    width = C_REP * Q_BLOCK
    eye = jnp.asarray(np.eye(LANES), BF16)
    assert AUG_ROWS * (SLC_GROUP + 1) <= LANES
    slots = jnp.asarray(_slot_pattern(), BF16)
    grid_spec = pltpu.PrefetchScalarGridSpec(
        num_scalar_prefetch=1,
        grid=(b, g, nq),
        in_specs=[
            pl.BlockSpec((None, rows, Q_BLOCK), lambda bb, gg, i, bits: (bb, gg, i)),
            pl.BlockSpec((None, s, 2 * LANES), lambda bb, gg, i, bits: (bb, 0, 0)),
            pl.BlockSpec((None, nq, None, HEAD_DIM, Q_BLOCK), lambda bb, gg, i, bits: (bb, 0, gg, 0, 0)),
            pl.BlockSpec((None, None, None, nq * AUG_ROWS, Q_BLOCK), lambda bb, gg, i, bits: (bb, gg, i, 0, 0)),
            pl.BlockSpec((LANES, LANES), lambda bb, gg, i, bits: (0, 0)),
            pl.BlockSpec(slots.shape, lambda bb, gg, i, bits: (0, 0, 0)),
        ],
        out_specs=pl.BlockSpec((Q_BLOCK, rows), lambda bb, gg, i, bits: (bb * nq + i, gg)),
        scratch_shapes=[
            pltpu.VMEM((2 * LANES, width), BF16),
            pltpu.VMEM((1, width), F32),
            pltpu.VMEM((1, width), F32),
            pltpu.VMEM((HEAD_DIM, width), F32),
        ],
    )
    return pl.pallas_call(
        functools.partial(_slc_body, nq=nq, stride=stride),
        grid_spec=grid_spec,
        out_shape=jax.ShapeDtypeStruct((b * s, g * rows), F32),
        compiler_params=_cparams(("parallel", "parallel", "parallel")),
    )(lists, qt, ks, vst, sel, eye, slots)


def _merge_body(x_ref, g_ref, oa0, la0, oa1, la1, oa2, la2, ob_ref, ocmp_ref, oslc_ref, owin_ref, cg_ref,
                p4t_ref, p16t_ref, ex_ref, wg0_ref, wg1_ref, wg2_ref, wa_ref, wb_ref, wc_ref,
                out_ref, h_ref, oall_ref):
    @pl.when(pl.program_id(1) == 0)
    def _():
        h_ref[...] = _rms_rows(x_ref[...], g_ref[...])

        def natural(ref, pt_ref):
            hi, lo = _split(ref[...].reshape(TM, A_OUT))
            return _dot(pt_ref[...], hi) + _dot(pt_ref[...], lo)

        o0, l0 = oa0[...], la0[...]
        o1, l1 = natural(oa1, p4t_ref), natural(la1, p4t_ref)
        o2, l2 = natural(oa2, p16t_ref), natural(la2, p16t_ref)
        mx = jnp.maximum(jnp.maximum(l0, l1), l2)
        e0, e1, e2 = jnp.exp(l0 - mx), jnp.exp(l1 - mx), jnp.exp(l2 - mx)
        oall_ref[:, 0:A_OUT] = ((e0 * o0 + e1 * o1 + e2 * o2) / (e0 + e1 + e2)).astype(BF16)
        oall_ref[:, A_OUT:A_OUT + B_HEADS * HEAD_DIM] = ob_ref[...].astype(BF16)
        cg_split = jnp.concatenate(_split(cg_ref[...]), axis=1)
        o_c = None
        for w, o_ref in enumerate((ocmp_ref, oslc_ref, owin_ref)):
            term = _dot(cg_split, ex_ref[w]) * o_ref[...]
            o_c = term if o_c is None else o_c + term
        oall_ref[:, A_OUT + B_HEADS * HEAD_DIM:] = o_c.astype(BF16)

    h = h_ref[...]
    c0, c1 = A_OUT, A_OUT + B_HEADS * HEAD_DIM
    merged = jax.nn.sigmoid(_dot(h, wg0_ref[...])) * _dot(oall_ref[:, 0:c0], wa_ref[...])
    merged += jax.nn.sigmoid(_dot(h, wg1_ref[...])) * _dot(oall_ref[:, c0:c1], wb_ref[...])
    merged += jax.nn.sigmoid(_dot(h, wg2_ref[...])) * _dot(oall_ref[:, c1:], wc_ref[...])
    out_ref[...] = merged.astype(BF16)


def _merge(x, g, a_outs, ob, ocmp, oslc, owin, cg, w_gate, wa, wb, wc, ex, *, tn=512):
    t, d = x.shape
    per16 = CHUNK16 // TM
    n_t = d // tn
    (oa0, la0), (oa1, la1), (oa2, la2) = a_outs

    def rows(a):
        return pl.BlockSpec((TM, a.shape[1]), lambda i, n: (i, 0))

    a1_spec = pl.BlockSpec((None, 4, Q_BLOCK, A_OUT), lambda i, n: (i, 0, 0, 0))
    a2_spec = pl.BlockSpec((None, 16, TM // 16, A_OUT), lambda i, n: (i // per16, 0, i % per16, 0))
    p4t = jnp.asarray(_deinterleave(TM, 4).T, BF16)
    p16t = jnp.asarray(_deinterleave(TM, 16).T, BF16)
    in_specs = [
        rows(x), _resident((1, d)),
        rows(oa0), rows(la0), a1_spec, a1_spec, a2_spec, a2_spec,
        rows(ob), rows(ocmp), rows(oslc), rows(owin), rows(cg),
        _resident((TM, TM)), _resident((TM, TM)), _resident(ex.shape),
        pl.BlockSpec((d, tn), lambda i, n: (0, n)),
        pl.BlockSpec((d, tn), lambda i, n: (0, n + n_t)),
        pl.BlockSpec((d, tn), lambda i, n: (0, n + 2 * n_t)),
        pl.BlockSpec((wa.shape[0], tn), lambda i, n: (0, n)),
        pl.BlockSpec((wb.shape[0], tn), lambda i, n: (0, n)),
        pl.BlockSpec((wc.shape[0], tn), lambda i, n: (0, n)),
    ]
    return pl.pallas_call(
        _merge_body,
        grid=(t // TM, n_t),
        in_specs=in_specs,
        out_specs=pl.BlockSpec((TM, tn), lambda i, n: (i, n)),
        out_shape=jax.ShapeDtypeStruct((t, d), BF16),
        scratch_shapes=[pltpu.VMEM((TM, d), BF16), pltpu.VMEM((TM, wa.shape[0] + wb.shape[0] + wc.shape[0]), BF16)],
        compiler_params=_cparams(("parallel", "arbitrary")),
    )(x, g.reshape(1, d), oa0, la0, oa1, la1, oa2, la2, ob, ocmp, oslc, owin, cg, p4t, p16t, ex,
      w_gate, w_gate, w_gate, wa, wb, wc)


def _out_body(x_ref, m_ref, w_ref, o_ref):
    o_ref[...] = x_ref[...] + _dot(m_ref[...], w_ref[...])


def _out_proj(x, merged, w_out):
    t, d = x.shape
    rows = pl.BlockSpec((TM, d), lambda i: (i, 0))
    return pl.pallas_call(
        _out_body,
        grid=(t // TM,),
        in_specs=[rows, rows, _resident((d, d))],
        out_specs=rows,
        out_shape=jax.ShapeDtypeStruct((t, d), F32),
        compiler_params=_cparams(("parallel",)),
    )(x, merged, w_out)


def _qkv_column_params(qk_gain):
    flag, gain, scale = [], [], []
    one = jnp.ones((HEAD_DIM,), F32)

    def add(n_heads, normed, is_q, gvec, units=1.0):
        for _ in range(n_heads):
            flag.append(np.full((HEAD_DIM,), 1.0 if normed else 0.0, np.float32))
            gain.append(gvec if normed else one)
            scale.append(np.full((HEAD_DIM,), units * HEAD_DIM ** -0.5 if is_q else 1.0, np.float32))

    for _ in range(len(A_GROUPS)):
        add(A_HEADS_PER_GROUP, True, True, qk_gain[0, 0])
        add(A_HEADS_PER_GROUP, True, False, qk_gain[0, 1])
        add(A_HEADS_PER_GROUP, False, False, one)
    add(B_HEADS, True, True, qk_gain[1, 0])
    add(B_KV_HEADS, True, False, qk_gain[1, 1])
    add(B_KV_HEADS, False, False, one)
    add(C_HEADS, True, True, qk_gain[2, 0], units=LOG2E)
    for normed in (False, False, True, False, True, False):
        add(C_KV_HEADS, normed, False, qk_gain[2, 1])
    flag = np.concatenate(flag)
    assert flag.shape[0] == QKV_COLS
    return jnp.asarray(flag), jnp.concatenate(gain) * jnp.asarray(np.concatenate(scale))


def _overlap_t(n_slc, n_pad, n_cmp):
    n = np.arange(n_pad)[None, :]
    j = np.arange(n_slc)[:, None]
    start, end = CMP_STRIDE * n, CMP_STRIDE * n + CMP_BLOCK - 1
    ov = (start <= SLC_BLOCK * j + SLC_BLOCK - 1) & (end >= SLC_BLOCK * j) & (n < n_cmp)
    return jnp.asarray(ov, BF16)


def _gate_expand():
    ex = np.zeros((3, LANES, C_HEADS * HEAD_DIM), np.float32)
    for w in range(3):
        for h in range(C_HEADS):
            ex[w, h * 3 + w, h * HEAD_DIM:(h + 1) * HEAD_DIM] = 1.0
    return jnp.asarray(np.concatenate([ex, ex], axis=1), BF16)


def _compress_weights(cmp_pos, cmp_w1, cmp_w2):
    n_q = 2 * C_KV_HEADS
    w1 = cmp_w1.reshape(2, 2, CMP_STRIDE, HEAD_DIM, CMP_HIDDEN)
    w1q = jnp.repeat(w1, C_KV_HEADS, axis=0)
    wb = jnp.einsum("qhcdn,qp->hcqdpn", w1q, jnp.eye(n_q, dtype=F32))
    wb = wb.reshape(2, CMP_STRIDE, n_q * HEAD_DIM, n_q * CMP_HIDDEN).astype(BF16)
    pos = cmp_pos.reshape(2, 2, CMP_STRIDE, HEAD_DIM)
    prow = jnp.repeat(pos, C_KV_HEADS, axis=0).transpose(1, 2, 0, 3).reshape(2, CMP_STRIDE, 1, n_q * HEAD_DIM)
    prow = jnp.broadcast_to(prow, (2, CMP_STRIDE, 8, n_q * HEAD_DIM)).astype(BF16)
    eye_g = jnp.eye(C_KV_HEADS, dtype=F32)
    w2k = jnp.kron(eye_g, cmp_w2[0]).astype(BF16)
    w2vt = jnp.kron(eye_g, cmp_w2[1]).T.astype(BF16)
    return wb, prow, w2k, w2vt


def _token_mixing(x, b, s, mix_norm, w_in, qk_gain, sinks, cmp_pos, cmp_w1, cmp_w2, w_a, w_b, w_c):
    t, d = x.shape
    assert s % CHUNK16 == 0 and d % 512 == 0
    c_gate_cols = 3 * C_HEADS
    w_qkv = w_in[:, :QKV_COLS + LANES].astype(BF16)
    flag, gs = _qkv_column_params(qk_gain)
    a0, a1, a2, bsec, cq, ckv, cmpd, qt, ks, vst, cg = _qkv_proj(x, mix_norm, w_qkv, flag, gs, b, s)

    a_outs = [_dilated_group(a0.reshape(b, s, SEC), 0, b, s), _dilated_group(a1, 1, b, s),
              _dilated_group(a2, 2, b, s)]
    a_outs[0] = tuple(v.reshape(t, A_OUT) for v in a_outs[0])
    o_b = _sink_swa(bsec.reshape(b, s, SEC), sinks.astype(F32), b, s).reshape(t, -1)
    o_win = _nsa_window(cq.reshape(b, s, SEC), ckv.reshape(b, s, SEC), b, s).reshape(t, -1)

    n_chunks = s // CMP_STRIDE
    n_cmp = (s - CMP_BLOCK) // CMP_STRIDE + 1
    n_slc = s // SLC_BLOCK
    nq = s // Q_BLOCK
    kg = jnp.tile(qk_gain[2, 1], C_KV_HEADS).reshape(1, LANES)
    kc, vct = _compress(cmpd, *_compress_weights(cmp_pos, cmp_w1, cmp_w2), kg)
    o_cmp, sel, cnt = _cmp_select(qt, kc, vct, _overlap_t(n_slc, n_chunks, n_cmp), n_cmp=n_cmp)

    act = (cnt[:, :, :, 0, :] > 0).reshape(b, C_KV_HEADS, nq, nq, 2).any(axis=-1)
    act = act & (jnp.arange(nq)[None, :] < jnp.arange(nq)[:, None])
    order = jnp.argsort(jnp.logical_not(act), axis=-1, stable=True).astype(jnp.int32)
    count = jnp.sum(act, axis=-1, dtype=jnp.int32)[..., None]
    lists = jnp.concatenate([count, order] + [jnp.zeros_like(count)] * (SLC_GROUP - 1), axis=-1)
    o_slc = _slc_attention(lists.reshape(-1), qt, ks.reshape(b, s, 2 * LANES),
                           vst.reshape(b, nq, C_KV_HEADS, HEAD_DIM, Q_BLOCK), sel, stride=nq + SLC_GROUP)

    return _merge(x, mix_norm, a_outs, o_b, o_cmp, o_slc, o_win, cg,
                  w_in[:, QKV_COLS + c_gate_cols:].astype(BF16), w_a.astype(BF16), w_b.astype(BF16),
                  w_c.astype(BF16), _gate_expand())


def kernel(x, ffn1_norm, ffn1_w_gu, ffn1_w_down, mix_norm, w_in, qk_gain, sinks, cmp_pos, cmp_w1, cmp_w2,
           w_branch_a, w_branch_b, w_branch_c, w_out, ffn2_norm, ffn2_w_gu, ffn2_w_down):
    b, s, d = x.shape
    h = x.reshape(b * s, d)
    w1_gu, w1_down = ffn1_w_gu.astype(BF16), ffn1_w_down.astype(BF16)
    w2_gu, w2_down = ffn2_w_gu.astype(BF16), ffn2_w_down.astype(BF16)
    for l in range(ffn1_norm.shape[0]):
        h = _ffn(h, ffn1_norm[l], w1_gu, w1_down, l)
        merged = _token_mixing(h, b, s, mix_norm[l], w_in[l], qk_gain[l], sinks[l], cmp_pos[l], cmp_w1[l],
                               cmp_w2[l], w_branch_a[l], w_branch_b[l], w_branch_c[l])
        h = _out_proj(h, merged, w_out[l].astype(BF16))
        h = _ffn(h, ffn2_norm[l], w2_gu, w2_down, l)
    return h.reshape(b, s, d)
```

```python
import functools
import math

import numpy as np
import jax
import jax.numpy as jnp
from jax import lax
from jax.experimental import pallas as pl
from jax.experimental.pallas import tpu as pltpu

F32 = jnp.float32
BF16 = jnp.bfloat16

HEAD_DIM = 64
Q_BLOCK = 128
LANES = 128
A_GROUPS = ((128, 1), (512, 4), (2048, 16))
A_HEADS_PER_GROUP = 4
A_HEADS = 12
A_OUT = A_HEADS_PER_GROUP * HEAD_DIM
B_HEADS = 8
B_KV_HEADS = 2
B_WINDOW = 128
C_HEADS = 12
C_KV_HEADS = 2
C_REP = C_HEADS // C_KV_HEADS
CMP_BLOCK = 32
CMP_STRIDE = 16
CMP_HIDDEN = 256
SLC_BLOCK = 64
SLC_TOPK = 16
C_WINDOW = 512
RMS_EPS = 1e-6
NEG_INF = -1e30
SEC = 768
N_SEC = 6
QKV_COLS = SEC * N_SEC
TM = 512
CHUNK16 = Q_BLOCK * 16
VMEM_LIMIT = 56 * 1024 * 1024


def _slopes(n):
    return [float(2.0 ** (-8.0 * (h + 1) / n)) for h in range(n)]


def _cparams(sem):
    return pltpu.CompilerParams(dimension_semantics=sem, vmem_limit_bytes=VMEM_LIMIT)


def _dot(a, b):
    return jnp.dot(a, b, preferred_element_type=F32)


def _nt_dot(a, b):
    return lax.dot_general(a, b, (((1,), (1,)), ((), ())), preferred_element_type=F32)


def _split(v):
    hi = v.astype(BF16)
    return hi, (v - hi.astype(F32)).astype(BF16)


def _resident(shape):
    return pl.BlockSpec(shape, lambda *_: (0,) * len(shape), pipeline_mode=pl.Buffered(1))


def _rms_rows(x, g):
    ms = jnp.mean(x * x, axis=-1, keepdims=True)
    return (x * lax.rsqrt(ms + RMS_EPS) * g).astype(BF16)


def _deinterleave(n, d):
    p = np.zeros((n, n), np.float32)
    r = np.arange(n // d)
    for c in range(d):
        p[c * (n // d) + r, d * r + c] = 1.0
    return p


AUG_POS = 0
AUG_ROWS = 16
MASK_BIG = 1e30
SLC_GROUP = 7
LOG2E = math.log2(math.e)


def _key_pattern(n):
    pat = np.zeros((n, LANES), np.float32)
    pat[:, AUG_POS] = pat[:, AUG_POS + 1] = np.arange(n) % Q_BLOCK
    return pat


def _slot_pattern():
    pat = np.zeros((SLC_GROUP, Q_BLOCK, 2 * LANES), np.float32)
    r = np.arange(Q_BLOCK)
    for u in range(SLC_GROUP):
        pat[u, :, LANES + AUG_ROWS * (u + 1)] = r < SLC_BLOCK
        pat[u, :, LANES + AUG_ROWS * (u + 1) + 1] = r >= SLC_BLOCK
    return pat


def _block_spread(n_tiles):
    m = np.zeros((n_tiles * AUG_ROWS, 2 * n_tiles), np.float32)
    jj = np.arange(n_tiles)
    for e in range(2):
        m[AUG_ROWS * jj + e, 2 * jj + e] = 1.0
    return m


def _ffn_body(x_ref, g_ref, wg_ref, wu_ref, wd_ref, o_ref, h_ref, *, n_f):
    f = pl.program_id(1)

    @pl.when(f == 0)
    def _():
        h_ref[...] = _rms_rows(x_ref[...], g_ref[...])
        o_ref[...] = jnp.zeros_like(o_ref)

    h = h_ref[...]
    gate = _dot(h, wg_ref[...])
    up = _dot(h, wu_ref[...])
    act = (gate * jax.nn.sigmoid(gate) * up).astype(BF16)
    o_ref[...] += _dot(act, wd_ref[...])

    @pl.when(f == n_f - 1)
    def _():
        o_ref[...] = x_ref[...] + 0.5 * o_ref[...]


def _ffn(x, g, w_gu, w_down, layer, *, tm=1024, tf=512):
    t, d = x.shape
    d_ff = w_down.shape[1]
    n_f = d_ff // tf
    rows = pl.BlockSpec((tm, d), lambda i, f: (i, 0))
    return pl.pallas_call(
        functools.partial(_ffn_body, n_f=n_f),
        grid=(t // tm, n_f),
        in_specs=[
            rows,
            pl.BlockSpec((1, d), lambda i, f: (0, 0)),
            pl.BlockSpec((None, d, tf), lambda i, f: (layer, 0, f)),
            pl.BlockSpec((None, d, tf), lambda i, f: (layer, 0, f + n_f)),
            pl.BlockSpec((None, tf, d), lambda i, f: (layer, f, 0)),
        ],
        out_specs=rows,
        out_shape=jax.ShapeDtypeStruct((t, d), F32),
        scratch_shapes=[pltpu.VMEM((tm, d), BF16)],
        compiler_params=_cparams(("parallel", "arbitrary")),
    )(x, g.reshape(1, d), w_gu, w_gu, w_down)


NORM_TILE = 256
NORM_TILES = {0: (0, 1), 1: (0, 1), 2: (0, 1), 3: (0, 1, 2), 4: (0, 1, 2), 5: (1, 2)}


def _head_sumsq(y, bd):
    return _dot((y * y).astype(BF16), bd)


def _qkv_body(x_ref, g_ref, w_ref, flag_ref, gs_ref, bd_ref, p4_ref, p16_ref, eye_ref, kpat_ref,
              a0_ref, a1_ref, a2_ref, b_ref, cq_ref, ckv_ref, cmpd_ref, qt_ref, ks_ref, vst_ref, cg_ref):
    h = _rms_rows(x_ref[...], g_ref[...])
    bd = bd_ref[...]

    def section(k):
        sl = slice(k * SEC, (k + 1) * SEC)
        if k < len(A_GROUPS):
            y = jnp.concatenate([_dot(h, w_ref[:, part * A_HEADS * HEAD_DIM + k * A_OUT:
                                                  part * A_HEADS * HEAD_DIM + (k + 1) * A_OUT])
                                 for part in range(3)], axis=1)
        else:
            y = _dot(h, w_ref[:, sl])
        tiles = []
        for c in range(SEC // NORM_TILE):
            yc = y[:, c * NORM_TILE:(c + 1) * NORM_TILE]
            if c in NORM_TILES[k]:
                cols = slice(k * SEC + c * NORM_TILE, k * SEC + (c + 1) * NORM_TILE)
                inv = lax.rsqrt(_head_sumsq(yc, bd) * (1.0 / HEAD_DIM) + RMS_EPS)
                yc = yc * jnp.where(flag_ref[:, cols] > 0, inv, 1.0) * gs_ref[:, cols]
            tiles.append(yc.astype(BF16))
        return jnp.concatenate(tiles, axis=1)

    a0_ref[...] = section(0)
    a1_ref[...] = _dot(p4_ref[...], section(1)).astype(BF16).reshape(a1_ref.shape)
    a2_ref[...] = _dot(p16_ref[...], section(2)).astype(BF16).reshape(a2_ref.shape)
    b_ref[...] = section(3)
    y_cq = section(4)
    cq_ref[...] = y_cq
    qt_ref[...] = _nt_dot(eye_ref[...], y_cq).astype(BF16)
    y_ckv = section(5)
    ckv_ref[...] = y_ckv
    cmpd_ref[...] = _dot(p16_ref[...], y_ckv[:, 0:2 * LANES]).astype(BF16).reshape(cmpd_ref.shape)
    ks_ref[:, 0:LANES] = y_ckv[:, 2 * LANES:3 * LANES]
    ks_ref[:, LANES:2 * LANES] = kpat_ref[...]
    eye = eye_ref[0:LANES, 0:LANES]
    for kb in range(vst_ref.shape[0]):
        vt = _nt_dot(eye, y_ckv[kb * Q_BLOCK:(kb + 1) * Q_BLOCK, 3 * LANES:4 * LANES]).astype(BF16)
        for gg in range(C_KV_HEADS):
            vst_ref[kb, gg] = vt[gg * HEAD_DIM:(gg + 1) * HEAD_DIM]
    cg_ref[...] = jax.nn.sigmoid(_dot(h, w_ref[:, QKV_COLS:QKV_COLS + LANES]))


def _qkv_proj(x, g, w, flag, gs, b, s):
    t, d = x.shape
    tiles_per_batch = s // TM
    per16 = CHUNK16 // TM
    bd = jnp.asarray(np.kron(np.eye(NORM_TILE // HEAD_DIM), np.ones((HEAD_DIM, HEAD_DIM))), BF16)
    p4 = jnp.asarray(_deinterleave(TM, 4), BF16)
    p16 = jnp.asarray(_deinterleave(TM, 16), BF16)
    eye = jnp.asarray(np.eye(SEC), BF16)
    nat = pl.BlockSpec((TM, SEC), lambda i: (i, 0))
    out_specs = [
        nat,
        pl.BlockSpec((None, 4, Q_BLOCK, SEC), lambda i: (i, 0, 0, 0)),
        pl.BlockSpec((None, 16, TM // 16, SEC), lambda i: (i // per16, 0, i % per16, 0)),
        nat, nat, nat,
        pl.BlockSpec((None, 16, TM // 16, 2 * LANES), lambda i: (i // tiles_per_batch, 0, i % tiles_per_batch, 0)),
        pl.BlockSpec((None, SEC, TM), lambda i: (i // tiles_per_batch, 0, i % tiles_per_batch)),
        pl.BlockSpec((TM, 2 * LANES), lambda i: (i, 0)),
        pl.BlockSpec((TM // Q_BLOCK, C_KV_HEADS, HEAD_DIM, Q_BLOCK), lambda i: (i, 0, 0, 0)),
        pl.BlockSpec((TM, LANES), lambda i: (i, 0)),
    ]
    out_shape = [
        jax.ShapeDtypeStruct((t, SEC), BF16),
        jax.ShapeDtypeStruct((t // TM, 4, Q_BLOCK, SEC), BF16),
        jax.ShapeDtypeStruct((t // CHUNK16, 16, Q_BLOCK, SEC), BF16),
        jax.ShapeDtypeStruct((t, SEC), BF16),
        jax.ShapeDtypeStruct((t, SEC), BF16),
        jax.ShapeDtypeStruct((t, SEC), BF16),
        jax.ShapeDtypeStruct((b, 16, s // 16, 2 * LANES), BF16),
        jax.ShapeDtypeStruct((b, SEC, s), BF16),
        jax.ShapeDtypeStruct((t, 2 * LANES), BF16),
        jax.ShapeDtypeStruct((t // Q_BLOCK, C_KV_HEADS, HEAD_DIM, Q_BLOCK), BF16),
        jax.ShapeDtypeStruct((t, LANES), F32),
    ]
    n_w = w.shape[1]
    return pl.pallas_call(
        _qkv_body,
        grid=(t // TM,),
        in_specs=[
            pl.BlockSpec((TM, d), lambda i: (i, 0)),
            _resident((1, d)),
            _resident((d, n_w)),
            _resident((1, QKV_COLS)),
            _resident((1, QKV_COLS)),
            _resident((NORM_TILE, NORM_TILE)),
            _resident((TM, TM)),
            _resident((TM, TM)),
            _resident((SEC, SEC)),
            _resident((TM, LANES)),
        ],
        out_specs=out_specs,
        out_shape=out_shape,
        compiler_params=_cparams(("parallel",)),
    )(x, g.reshape(1, d), w, flag.reshape(1, -1), gs.reshape(1, -1), bd, p4, p16, eye,
      jnp.asarray(_key_pattern(TM), BF16))


def _banded_body(*refs, nb, qb, keys, heads, k_off, v_off, n_pairs, q_axis, use_sinks, with_lse, stack, fold,
                 log2_units):
    refs = list(refs)
    q_ref = refs.pop(0)
    kv_refs = [refs.pop(0) for _ in range({"self": 1, "window": 2, "blocks": nb + 1}[keys])]
    qc_ref, kaug_ref, band_ref = refs.pop(0), refs.pop(0), refs.pop(0)
    sink_ref = refs.pop(0) if use_sinks else None
    o_ref = refs.pop(0)
    lse_ref = refs.pop(0) if with_lse else None

    i = pl.program_id(q_axis)
    nk = (nb + 1) * Q_BLOCK
    col = lax.broadcasted_iota(jnp.int32, (Q_BLOCK, nk), 1)
    band = band_ref[...]

    def start_mask(sub):
        if sub > 0 and sub >= nb:
            return band
        return band + jnp.where(col < (nb - (i * qb + sub)) * Q_BLOCK, -MASK_BIG, 0.0)

    rel_f = (nb * Q_BLOCK + lax.broadcasted_iota(jnp.int32, (Q_BLOCK, nk), 0) - col).astype(F32)
    lane = lax.broadcasted_iota(jnp.int32, (Q_BLOCK, LANES), 1)
    low_half = lane < HEAD_DIM
    kaug = kaug_ref[...]

    def q_cols(sub, c0):
        if len(q_ref.shape) == 3:
            return q_ref[sub, :, c0:c0 + LANES]
        return q_ref[sub * Q_BLOCK:(sub + 1) * Q_BLOCK, c0:c0 + LANES]

    kv_cache = {}

    def swap_halves(tile, swapped):
        return pltpu.roll(tile.astype(F32), HEAD_DIM, 1).astype(BF16) if swapped else tile

    def kv_tile(sub, off, kv_pair, swapped):
        c0 = off + kv_pair * LANES
        if keys == "window":
            key = (off, kv_pair, swapped)
            if key not in kv_cache:
                kv_cache[key] = swap_halves(jnp.concatenate([r[:, c0:c0 + LANES] for r in kv_refs], axis=0), swapped)
            first = qb - nb + sub
            return kv_cache[key][first * Q_BLOCK:(first + nb + 1) * Q_BLOCK]
        key = (sub, off, kv_pair, swapped)
        if key not in kv_cache:
            if keys == "self":
                blocks = [kv_refs[0][:, c0:c0 + LANES] if sub == 0 else q_cols(sub - 1, c0), q_cols(sub, c0)]
            else:
                blocks = [r[:, c0:c0 + LANES] for r in kv_refs]
            kv_cache[key] = swap_halves(jnp.concatenate(blocks, axis=0), swapped)
        return kv_cache[key]

    classes = {}
    for head in heads:
        pair, half, kv_pair, kv_half, slope, hidx = head
        key = (kv_pair, kv_half != half) if stack else hidx
        classes.setdefault(key, []).append(head)

    outs = [[[None, None] for _ in range(n_pairs)] for _ in range(qb)]
    lses = [[[None, None] for _ in range(n_pairs)] for _ in range(qb)]
    groups = [(sub, members) for sub in range(qb) for members in classes.values()]

    scores = []
    for sub, members in groups:
        n_h = len(members)
        kv_pair, swapped = members[0][2], members[0][3] != members[0][1]
        mask = start_mask(sub)
        q_rows = []
        for pair, half, _, _, _, hidx in members:
            qp = q_cols(sub, pair * LANES)
            own = low_half if half == 0 else jnp.logical_not(low_half)
            qm = jnp.where(own, qp, jnp.zeros_like(qp))
            q_rows.append(jnp.concatenate([qm, qc_ref[hidx]], axis=1) if fold else qm)
        if fold:
            k_aug = jnp.concatenate([kv_tile(sub, k_off, kv_pair, swapped), kaug], axis=1)
            s = _nt_dot(jnp.concatenate(q_rows, axis=0), k_aug)
            s = (s.reshape(n_h, Q_BLOCK, nk) + mask[None]).reshape(n_h * Q_BLOCK, nk)
        else:
            bias = jnp.concatenate([mask - member[4] * rel_f for member in members], axis=0)
            s = _nt_dot(jnp.concatenate(q_rows, axis=0), kv_tile(sub, k_off, kv_pair, swapped)) + bias
        scores.append(s)

    probs = []
    for (_, members), s in zip(groups, scores):
        m = jnp.max(s, axis=1, keepdims=True)
        if use_sinks:
            assert len(members) == 1
            sink = sink_ref[members[0][5]]
            m = jnp.maximum(m, sink)
        p = jnp.exp2(s - m) if log2_units else jnp.exp(s - m)
        den = jnp.sum(p, axis=1, keepdims=True)
        if use_sinks:
            den = den + jnp.exp(sink - m)
        probs.append((p.astype(BF16), m, den))

    for (sub, members), (p, m, den) in zip(groups, probs):
        kv_pair, swapped = members[0][2], members[0][3] != members[0][1]
        r = _dot(p, kv_tile(sub, v_off, kv_pair, swapped)) / den
        lse = m + jnp.log(den) if with_lse else None
        for k, (pair, half, _, _, _, _) in enumerate(members):
            outs[sub][pair][half] = r[k * Q_BLOCK:(k + 1) * Q_BLOCK]
            if with_lse:
                lses[sub][pair][half] = jnp.broadcast_to(lse[k * Q_BLOCK:(k + 1) * Q_BLOCK], (Q_BLOCK, LANES))

    def store(ref, sub, sl, value):
        if len(ref.shape) == 3:
            ref[sub, :, sl] = value
        else:
            ref[sub * Q_BLOCK:(sub + 1) * Q_BLOCK, sl] = value

    for sub in range(qb):
        for pair in range(n_pairs):
            sl = slice(pair * LANES, (pair + 1) * LANES)
            store(o_ref, sub, sl, jnp.where(low_half, outs[sub][pair][0], outs[sub][pair][1]))
            if with_lse:
                store(lse_ref, sub, sl, jnp.where(low_half, lses[sub][pair][0], lses[sub][pair][1]))


def _banded_consts(heads, nb, max_dist):
    nk = (nb + 1) * Q_BLOCK
    slope = np.asarray([h[4] for h in heads], np.float32)[:, None]
    q_dist = (nb * Q_BLOCK + np.arange(Q_BLOCK, dtype=np.float32))[None, :]
    ones = np.ones_like(q_dist)
    vals = jnp.asarray(np.stack([slope * ones, slope * Q_BLOCK * ones, -slope * q_dist], axis=-1))
    hi = vals.astype(BF16)
    lo = (vals - hi.astype(F32)).astype(BF16)
    cols = jnp.stack([hi[..., 0], lo[..., 0], hi[..., 1], lo[..., 1], hi[..., 2], lo[..., 2]], axis=-1)
    qc = jnp.pad(cols, ((0, 0), (0, 0), (0, LANES - cols.shape[-1])))
    kaug = np.zeros((nk, LANES), np.float32)
    kaug[:, 0] = kaug[:, 1] = np.arange(nk) % Q_BLOCK
    kaug[:, 2] = kaug[:, 3] = np.arange(nk) // Q_BLOCK
    kaug[:, 4] = kaug[:, 5] = 1.0
    rel = nb * Q_BLOCK + np.arange(Q_BLOCK)[:, None] - np.arange(nk)[None, :]
    band = np.where((rel >= 0) & (rel <= max_dist), 0.0, -MASK_BIG).astype(np.float32)
    return [qc, jnp.asarray(kaug, BF16), jnp.asarray(band)]


BAND_QB = 4
CMP_QB = 2


def _banded_call(q_arr, kv_arr, *, grid, q_spec, kv_specs, out_spec, out_shape, out_cols, nb, qb, max_dist, heads,
                 k_off, v_off, q_axis, sinks=None, with_lse=False, stack=False, fold=False, log2_units=False):
    assert not (log2_units and (with_lse or sinks is not None))
    if len(kv_specs) == 1 and nb == 1:
        keys = "self"
    elif qb > 1:
        keys = "window"
        assert len(kv_specs) == 2 and nb <= qb
    else:
        keys = "blocks"
        assert len(kv_specs) == nb + 1
    consts = _banded_consts(heads, nb, max_dist)
    in_specs = [q_spec] + list(kv_specs) + [pl.BlockSpec(c.shape, lambda *_, nd=c.ndim: (0,) * nd) for c in consts]
    args = [q_arr] + [kv_arr] * len(kv_specs) + consts
    if sinks is not None:
        in_specs.append(pl.BlockSpec(memory_space=pltpu.SMEM))
        args.append(sinks)
    oshape = jax.ShapeDtypeStruct(out_shape, F32)
    body = functools.partial(_banded_body, nb=nb, qb=qb, keys=keys, heads=heads, k_off=k_off, v_off=v_off,
                             n_pairs=out_cols // LANES, q_axis=q_axis, use_sinks=sinks is not None,
                             with_lse=with_lse, stack=stack, fold=fold, log2_units=log2_units)
    return pl.pallas_call(
        body,
        grid=grid,
        in_specs=in_specs,
        out_specs=[out_spec, out_spec] if with_lse else out_spec,
        out_shape=[oshape, oshape] if with_lse else oshape,
        compiler_params=_cparams(("parallel",) * len(grid)),
    )(*args)


def _row_specs(b, s, qb, out_cols):
    q_spec = pl.BlockSpec((None, qb * Q_BLOCK, SEC), lambda bb, i: (bb, i, 0))
    prev = pl.BlockSpec((None, Q_BLOCK, SEC), lambda bb, i: (bb, jnp.maximum(qb * i - 1, 0), 0))
    out_spec = pl.BlockSpec((None, qb * Q_BLOCK, out_cols), lambda bb, i: (bb, i, 0))
    return dict(grid=(b, s // (qb * Q_BLOCK)), q_axis=1, q_spec=q_spec, kv_specs=[prev], out_spec=out_spec,
                out_shape=(b, s, out_cols), qb=qb)


def _dilated_group(arr, gi, b, s):
    window, dil = A_GROUPS[gi]
    slopes = _slopes(A_HEADS)
    heads = tuple((hh // 2, hh % 2, hh // 2, hh % 2, slopes[gi * A_HEADS_PER_GROUP + hh] * dil, hh)
                  for hh in range(A_HEADS_PER_GROUP))
    common = dict(out_cols=A_OUT, nb=1, max_dist=window // dil, heads=heads, k_off=256, v_off=512,
                  with_lse=True, stack=False)
    if dil == 1:
        return _banded_call(arr, arr, **_row_specs(b, s, BAND_QB, A_OUT), **common)
    nc = s // (Q_BLOCK * dil)
    qb = BAND_QB if nc % BAND_QB == 0 else 1
    q_spec = pl.BlockSpec((qb, None, Q_BLOCK, SEC), lambda bb, c, i: ((bb * nc) // qb + i, c, 0, 0))
    prev = pl.BlockSpec((None, None, Q_BLOCK, SEC), lambda bb, c, i: (bb * nc + jnp.maximum(qb * i - 1, 0), c, 0, 0))
    out_spec = pl.BlockSpec((qb, None, Q_BLOCK, A_OUT), lambda bb, c, i: ((bb * nc) // qb + i, c, 0, 0))
    return _banded_call(arr, arr, grid=(b, dil, nc // qb), q_axis=2, q_spec=q_spec, kv_specs=[prev],
                        out_spec=out_spec, out_shape=(b * nc, dil, Q_BLOCK, A_OUT), qb=qb, **common)


def _sink_swa(arr, sinks, b, s):
    slopes = _slopes(B_HEADS)
    rep = B_HEADS // B_KV_HEADS
    heads = tuple((h // 2, h % 2, 0, h // rep, slopes[h], h) for h in range(B_HEADS))
    return _banded_call(arr, arr, **_row_specs(b, s, BAND_QB, B_HEADS * HEAD_DIM), out_cols=B_HEADS * HEAD_DIM,
                        nb=1, max_dist=B_WINDOW - 1, heads=heads, k_off=512, v_off=640, sinks=sinks)


def _nsa_window(cq, ckv, b, s):
    slopes = _slopes(C_HEADS)
    heads = tuple((h // 2, h % 2, 0, h // C_REP, slopes[h] * LOG2E, h) for h in range(C_HEADS))
    nb = -(-(C_WINDOW - 1) // Q_BLOCK)
    qb = nb
    blk = (None, qb * Q_BLOCK, SEC)
    kv_specs = [pl.BlockSpec(blk, lambda bb, i: (bb, jnp.maximum(i - 1, 0), 0)),
                pl.BlockSpec(blk, lambda bb, i: (bb, i, 0))]
    out_cols = C_HEADS * HEAD_DIM
    return _banded_call(
        cq, ckv, grid=(b, s // (qb * Q_BLOCK)), q_axis=1, qb=qb, log2_units=True,
        q_spec=pl.BlockSpec(blk, lambda bb, i: (bb, i, 0)), kv_specs=kv_specs,
        out_spec=pl.BlockSpec((None, qb * Q_BLOCK, out_cols), lambda bb, i: (bb, i, 0)),
        out_shape=(b, s, out_cols), out_cols=out_cols, nb=nb, max_dist=C_WINDOW - 1, heads=heads,
        k_off=512, v_off=640, stack=True, fold=True)


def _compress_body(t_ref, wb_ref, prow_ref, w2k_ref, w2vt_ref, kg_ref, bd_ref, kc_ref, vct_ref, *, n_chunks):
    hid_cols = 2 * C_KV_HEADS * CMP_HIDDEN
    u = jnp.zeros((n_chunks, hid_cols), F32)
    v = jnp.zeros((n_chunks, hid_cols), F32)
    pc = jnp.zeros((1, hid_cols), F32)
    for c in range(CMP_STRIDE):
        tc = t_ref[c]
        u = u + _dot(tc, wb_ref[0, c])
        v = v + _dot(tc, wb_ref[1, c])
        pc = pc + _dot(prow_ref[0, c], wb_ref[0, c])[0:1] + _dot(prow_ref[1, c], wb_ref[1, c])[0:1]
    hsum = u + pltpu.roll(v, n_chunks - 1, 0) + pc
    hid = (hsum * jax.nn.sigmoid(hsum)).astype(BF16)
    half = C_KV_HEADS * CMP_HIDDEN
    k = _dot(hid[:, :half], w2k_ref[...])
    hi, lo = _split(k * k)
    ss = _dot(hi, bd_ref[...]) + _dot(lo, bd_ref[...])
    kc_ref[...] = (k * lax.rsqrt(ss * (1.0 / HEAD_DIM) + RMS_EPS) * kg_ref[...]).astype(BF16)
    vct_ref[...] = _nt_dot(w2vt_ref[...], hid[:, half:]).astype(BF16)


def _compress(cmpd, wb, prow, w2k, w2vt, kg):
    b, _, n_chunks, width = cmpd.shape
    bd = jnp.asarray(np.kron(np.eye(LANES // HEAD_DIM), np.ones((HEAD_DIM, HEAD_DIM))), BF16)
    return pl.pallas_call(
        functools.partial(_compress_body, n_chunks=n_chunks),
        grid=(b,),
        in_specs=[
            pl.BlockSpec((None, CMP_STRIDE, n_chunks, width), lambda bb: (bb, 0, 0, 0)),
            _resident(wb.shape), _resident(prow.shape), _resident(w2k.shape), _resident(w2vt.shape),
            _resident((1, LANES)), _resident((LANES, LANES)),
        ],
        out_specs=[
            pl.BlockSpec((None, n_chunks, LANES), lambda bb: (bb, 0, 0)),
            pl.BlockSpec((None, LANES, n_chunks), lambda bb: (bb, 0, 0)),
        ],
        out_shape=[
            jax.ShapeDtypeStruct((b, n_chunks, LANES), BF16),
            jax.ShapeDtypeStruct((b, LANES, n_chunks), BF16),
        ],
        compiler_params=_cparams(("parallel",)),
    )(cmpd, wb, prow, w2k, w2vt, kg, bd)


def _to_natural(ot_list, eye):
    pairs = []
    for k in range(0, len(ot_list), 2):
        hi, lo = _split(jnp.concatenate([ot_list[k], ot_list[k + 1]], axis=0))
        pairs.append(_nt_dot(eye, hi) + _nt_dot(eye, lo))
    return jnp.concatenate(pairs, axis=1)


def _cmp_body(qt_ref, kc_ref, vct_ref, ovt_ref, eye_ref, spread_ref, cmask_ref,
              o_ref, selm_ref, cnt_ref, sel_sc, *, n_top):
    g = pl.program_id(1)
    i = pl.program_id(2)
    n_pad = kc_ref.shape[0]
    n_slc = ovt_ref.shape[0]
    qw = qt_ref.shape[1]
    qb = qw // Q_BLOCK
    per_q = Q_BLOCK // CMP_STRIDE
    own_rows = (lax.broadcasted_iota(jnp.int32, (LANES, qw), 0) // HEAD_DIM) == g
    slopes = [sl * LOG2E for sl in _slopes(C_HEADS)]

    def attend(rows):
        kc = kc_ref[0:rows, :]
        vct = vct_ref[:, 0:rows]
        mask = jnp.concatenate(
            [cmask_ref[pl.ds(pl.multiple_of(n_pad - per_q * (qb * i + h), 8), rows), :] for h in range(qb)], axis=1)
        n_f = (CMP_STRIDE * lax.broadcasted_iota(jnp.int32, (rows, qw), 0)).astype(F32)
        psum = jnp.zeros((rows, qw), F32)
        outs = []
        for r in range(C_REP):
            slope = jnp.where(g == 0, slopes[r], slopes[C_REP + r])
            qt = qt_ref[r * HEAD_DIM:(r + 1) * HEAD_DIM, :]
            q_pad = jnp.where(own_rows, jnp.concatenate([qt, qt], axis=0), jnp.zeros((LANES, qw), BF16))
            s = _dot(kc, q_pad) + (slope * n_f + mask)
            m = jnp.maximum(jnp.max(s, axis=0, keepdims=True), -1e20)
            e = jnp.exp2(s - m)
            den = jnp.sum(e, axis=0, keepdims=True)
            p = e * (1.0 / jnp.where(den > 0, den, 1.0))
            psum = psum + p
            both = _dot(vct, p.astype(BF16))
            outs.append(jnp.where(g == 0, both[:HEAD_DIM], both[HEAD_DIM:]))
        o_ref[...] = _to_natural(outs, eye_ref[...])

        n_j = rows * CMP_STRIDE // SLC_BLOCK
        hi, lo = _split(psum)
        ovt = ovt_ref[0:n_j, 0:rows]
        imp = _dot(ovt, hi) + _dot(ovt, lo)
        j_idx = lax.broadcasted_iota(jnp.int32, (n_j, qw), 0)
        t_q = i * qw + lax.broadcasted_iota(jnp.int32, (n_j, qw), 1)
        cur = lax.shift_right_logical(t_q, int(math.log2(SLC_BLOCK)))
        forced = ((j_idx == 0) | (j_idx == cur) | (j_idx == cur - 1)) & (j_idx <= cur)
        v = jnp.where((j_idx <= cur) & jnp.logical_not(forced), imp, -1.0)
        sel = jnp.where(forced, 1.0, 0.0)
        for _ in range(n_top - 3):
            m = jnp.max(v, axis=0, keepdims=True)
            first = jnp.min(jnp.where((v == m) & (m >= 0.0), j_idx, n_slc), axis=0, keepdims=True)
            pick = j_idx == first
            sel = jnp.where(pick, 1.0, sel)
            v = jnp.where(pick, -1.0, v)
        sel_sc[0:n_j, :] = sel
        if n_j < n_slc:
            sel_sc[n_j:, :] = jnp.zeros((n_slc - n_j, qw), F32)

    n_var = n_pad // LANES
    for var in range(n_var):
        pl.when((qb * i + qb - 1) // (LANES // per_q) == var)(functools.partial(attend, (var + 1) * LANES))

    sel = sel_sc[...]
    neg = jnp.where(sel > 0, 0.0, -MASK_BIG).astype(BF16)
    mask_rows = _dot(spread_ref[...], neg).astype(BF16)
    sel_b = sel.astype(BF16)
    for h in range(qb):
        lanes = slice(h * Q_BLOCK, (h + 1) * Q_BLOCK)
        selm_ref[h] = mask_rows[:, lanes]
        cnt_ref[h] = _nt_dot(jnp.ones((8, Q_BLOCK), BF16), sel_b[:, lanes])


def _cmp_select(qt, kc, vct, ovt, *, n_cmp):
    b, _, s = qt.shape
    g = C_KV_HEADS
    n_pad = kc.shape[1]
    n_slc = ovt.shape[0]
    nq = s // Q_BLOCK
    rows = C_REP * HEAD_DIM
    qb = CMP_QB if nq % CMP_QB == 0 else 1
    qw = qb * Q_BLOCK
    steps = nq // qb
    eye = jnp.asarray(np.eye(qw), BF16)
    n_top = min(SLC_TOPK, n_slc)
    assert n_top > 3 and n_cmp == n_pad - 1 and n_pad % LANES == 0
    n_rel = np.arange(-n_pad, n_pad)[:, None]
    cmask = np.where(CMP_STRIDE * n_rel + CMP_BLOCK - 1 <= np.arange(Q_BLOCK)[None, :], 0.0, -MASK_BIG)
    return pl.pallas_call(
        functools.partial(_cmp_body, n_top=n_top),
        grid=(b, g, steps),
        in_specs=[
            pl.BlockSpec((None, rows, qw), lambda bb, gg, i: (bb, gg, i)),
            pl.BlockSpec((None, n_pad, LANES), lambda bb, gg, i: (bb, 0, 0)),
            pl.BlockSpec((None, LANES, n_pad), lambda bb, gg, i: (bb, 0, 0)),
            pl.BlockSpec((n_slc, n_pad), lambda bb, gg, i: (0, 0)),
            pl.BlockSpec((qw, qw), lambda bb, gg, i: (0, 0)),
            pl.BlockSpec((nq * AUG_ROWS, n_slc), lambda bb, gg, i: (0, 0)),
            pl.BlockSpec((2 * n_pad, Q_BLOCK), lambda bb, gg, i: (0, 0)),
        ],
        out_specs=[
            pl.BlockSpec((qw, rows), lambda bb, gg, i: (bb * steps + i, gg)),
            pl.BlockSpec((None, None, qb, nq * AUG_ROWS, Q_BLOCK), lambda bb, gg, i: (bb, gg, i, 0, 0)),
            pl.BlockSpec((None, None, qb, 8, n_slc), lambda bb, gg, i: (bb, gg, i, 0, 0)),
        ],
        out_shape=[
            jax.ShapeDtypeStruct((b * s, g * rows), F32),
            jax.ShapeDtypeStruct((b, g, nq, nq * AUG_ROWS, Q_BLOCK), BF16),
            jax.ShapeDtypeStruct((b, g, nq, 8, n_slc), F32),
        ],
        scratch_shapes=[pltpu.VMEM((n_slc, qw), F32)],
        compiler_params=_cparams(("parallel", "parallel", "parallel")),
    )(qt, kc, vct, ovt, eye, jnp.asarray(_block_spread(nq), BF16), jnp.asarray(cmask, F32))


def _slc_body(list_ref, qt_ref, ks_ref, vst_ref, selm_ref, eye_ref, slot_ref, srow_ref, o_ref, qaug, m_sc, l_sc,
              acc_sc, *, nq, stride):
    bb = pl.program_id(0)
    g = pl.program_id(1)
    i = pl.program_id(2)
    width = C_REP * Q_BLOCK
    slopes = [sl * LOG2E for sl in _slopes(C_HEADS)]
    slope_s = [jnp.where(g == 0, slopes[r], slopes[C_REP + r]) for r in range(C_REP)]

    own_rows = (lax.broadcasted_iota(jnp.int32, (LANES, width), 0) // HEAD_DIM) == g
    q6 = jnp.concatenate([qt_ref[r * HEAD_DIM:(r + 1) * HEAD_DIM, :] for r in range(C_REP)], axis=1)
    qaug[0:LANES, :] = jnp.where(own_rows, jnp.concatenate([q6, q6], axis=0), jnp.zeros((LANES, width), BF16))
    qaug[LANES:, :] = srow_ref[...]

    q_loc = lax.broadcasted_iota(jnp.int32, (Q_BLOCK, Q_BLOCK), 1)
    k_loc = lax.broadcasted_iota(jnp.int32, (Q_BLOCK, Q_BLOCK), 0)
    causal = jnp.where(k_loc > q_loc, -MASK_BIG, 0.0)

    m_sc[...] = jnp.full(m_sc.shape, NEG_INF, F32)
    l_sc[...] = jnp.zeros(l_sc.shape, F32)
    acc_sc[...] = jnp.zeros(acc_sc.shape, F32)

    def accumulate(tiles, own_first):
        keys = []
        for u, (jj, _) in enumerate(tiles):
            keys.append(ks_ref[pl.ds(pl.multiple_of(jj * Q_BLOCK, Q_BLOCK), Q_BLOCK), :] + slot_ref[u])
            rows = selm_ref[pl.ds(pl.multiple_of(jj * AUG_ROWS, AUG_ROWS), AUG_ROWS), :]
            lo = LANES + AUG_ROWS * (u + 1)
            qaug[lo:lo + AUG_ROWS, :] = jnp.concatenate([rows] * C_REP, axis=1)
        st = _dot(jnp.concatenate(keys, axis=0), qaug[...])
        ps, alphas = [], []
        for r in range(C_REP):
            sl = slice(r * Q_BLOCK, (r + 1) * Q_BLOCK)
            m_old = m_sc[:, sl]
            m_new = m_old
            parts = []
            for u, (jj, extra) in enumerate(tiles):
                s = st[u * Q_BLOCK:(u + 1) * Q_BLOCK, sl]
                if own_first and u == 0:
                    s = s + causal
                c = slope_s[r] * ((jj - i) * Q_BLOCK).astype(F32) + extra
                m_new = jnp.maximum(m_new, jnp.max(s, axis=0, keepdims=True) + c)
                parts.append((s, c))
            alpha = jnp.exp2(m_old - m_new)
            l_new = alpha * l_sc[:, sl]
            p_rows = []
            for s, c in parts:
                p = jnp.exp2(s + (c - m_new))
                l_new = l_new + jnp.sum(p, axis=0, keepdims=True)
                p_rows.append(p.astype(BF16))
            l_sc[:, sl] = l_new
            m_sc[:, sl] = m_new
            ps.append(jnp.concatenate(p_rows, axis=0))
            alphas.append(alpha)
        values = jnp.concatenate([vst_ref[jj] for jj, _ in tiles], axis=1)
        pv = _dot(values, jnp.concatenate(ps, axis=1))
        acc_sc[...] = jnp.concatenate(alphas, axis=1) * acc_sc[...] + pv

    base = ((bb * pl.num_programs(1) + g) * nq + i) * stride
    count = list_ref[base]

    def listed(slot):
        return list_ref[base + 1 + slot], jnp.where(slot < count, 0.0, -MASK_BIG)

    accumulate([(i, 0.0)] + [listed(u) for u in range(SLC_GROUP - 1)], True)

    def step(k, carry):
        accumulate([listed(SLC_GROUP - 1 + SLC_GROUP * k + u) for u in range(SLC_GROUP)], False)
        return carry

    rest = jnp.maximum(count - (SLC_GROUP - 1), 0)
    lax.fori_loop(0, (rest + SLC_GROUP - 1) // SLC_GROUP, step, 0)
    l = l_sc[...]
    o = acc_sc[...] / jnp.where(l > 0, l, 1.0)
    o_ref[...] = _to_natural([o[:, r * Q_BLOCK:(r + 1) * Q_BLOCK] for r in range(C_REP)], eye_ref[...])


def _slc_attention(lists, qt, ks, vst, sel, *, stride):
    b, _, s = qt.shape
    g = C_KV_HEADS
    nq = s // Q_BLOCK
    rows = C_REP * HEAD_DIM
    width = C_REP * Q_BLOCK
    eye = jnp.asarray(np.eye(LANES), BF16)
    assert AUG_ROWS * (SLC_GROUP + 1) <= LANES
    slots = jnp.asarray(_slot_pattern(), BF16)
    slope = jnp.asarray(np.repeat(np.asarray(_slopes(C_HEADS), np.float32) * LOG2E, Q_BLOCK).reshape(g, 1, width))
    s_hi = slope.astype(BF16)
    s_lo = (slope - s_hi.astype(F32)).astype(BF16)
    srow = jnp.concatenate([s_hi, s_lo, jnp.zeros((g, LANES - 2, width), BF16)], axis=1)
    grid_spec = pltpu.PrefetchScalarGridSpec(
        num_scalar_prefetch=1,
        grid=(b, g, nq),
        in_specs=[
            pl.BlockSpec((None, rows, Q_BLOCK), lambda bb, gg, i, bits: (bb, gg, i)),
            pl.BlockSpec((None, s, 2 * LANES), lambda bb, gg, i, bits: (bb, 0, 0)),
            pl.BlockSpec((None, nq, None, HEAD_DIM, Q_BLOCK), lambda bb, gg, i, bits: (bb, 0, gg, 0, 0)),
            pl.BlockSpec((None, None, None, nq * AUG_ROWS, Q_BLOCK), lambda bb, gg, i, bits: (bb, gg, i, 0, 0)),
            pl.BlockSpec((LANES, LANES), lambda bb, gg, i, bits: (0, 0)),
            pl.BlockSpec(slots.shape, lambda bb, gg, i, bits: (0, 0, 0)),
            pl.BlockSpec((None, LANES, width), lambda bb, gg, i, bits: (gg, 0, 0)),
        ],
        out_specs=pl.BlockSpec((Q_BLOCK, rows), lambda bb, gg, i, bits: (bb * nq + i, gg)),
        scratch_shapes=[
            pltpu.VMEM((2 * LANES, width), BF16),
            pltpu.VMEM((1, width), F32),
            pltpu.VMEM((1, width), F32),
            pltpu.VMEM((HEAD_DIM, width), F32),
        ],
    )
    return pl.pallas_call(
        functools.partial(_slc_body, nq=nq, stride=stride),
        grid_spec=grid_spec,
        out_shape=jax.ShapeDtypeStruct((b * s, g * rows), F32),
        compiler_params=_cparams(("parallel", "parallel", "parallel")),
    )(lists, qt, ks, vst, sel, eye, slots, srow)


def _merge_body(x_ref, g_ref, oa0, la0, oa1, la1, oa2, la2, ob_ref, ocmp_ref, oslc_ref, owin_ref, cg_ref,
                p4t_ref, p16t_ref, ex_ref, wg0_ref, wg1_ref, wg2_ref, wa_ref, wb_ref, wc_ref,
                out_ref, h_ref, oall_ref):
    @pl.when(pl.program_id(1) == 0)
    def _():
        h_ref[...] = _rms_rows(x_ref[...], g_ref[...])

        def natural(ref, pt_ref):
            hi, lo = _split(ref[...].reshape(TM, A_OUT))
            return _dot(pt_ref[...], hi) + _dot(pt_ref[...], lo)

        o0, l0 = oa0[...], la0[...]
        o1, l1 = natural(oa1, p4t_ref), natural(la1, p4t_ref)
        o2, l2 = natural(oa2, p16t_ref), natural(la2, p16t_ref)
        mx = jnp.maximum(jnp.maximum(l0, l1), l2)
        e0, e1, e2 = jnp.exp(l0 - mx), jnp.exp(l1 - mx), jnp.exp(l2 - mx)
        oall_ref[:, 0:A_OUT] = ((e0 * o0 + e1 * o1 + e2 * o2) / (e0 + e1 + e2)).astype(BF16)
        oall_ref[:, A_OUT:A_OUT + B_HEADS * HEAD_DIM] = ob_ref[...].astype(BF16)
        cg_split = jnp.concatenate(_split(cg_ref[...]), axis=1)
        o_c = None
        for w, o_ref in enumerate((ocmp_ref, oslc_ref, owin_ref)):
            term = _dot(cg_split, ex_ref[w]) * o_ref[...]
            o_c = term if o_c is None else o_c + term
        oall_ref[:, A_OUT + B_HEADS * HEAD_DIM:] = o_c.astype(BF16)

    h = h_ref[...]
    c0, c1 = A_OUT, A_OUT + B_HEADS * HEAD_DIM
    merged = jax.nn.sigmoid(_dot(h, wg0_ref[...])) * _dot(oall_ref[:, 0:c0], wa_ref[...])
    merged += jax.nn.sigmoid(_dot(h, wg1_ref[...])) * _dot(oall_ref[:, c0:c1], wb_ref[...])
    merged += jax.nn.sigmoid(_dot(h, wg2_ref[...])) * _dot(oall_ref[:, c1:], wc_ref[...])
    out_ref[...] = merged.astype(BF16)


def _merge(x, g, a_outs, ob, ocmp, oslc, owin, cg, w_gate, wa, wb, wc, ex, *, tn=512):
    t, d = x.shape
    per16 = CHUNK16 // TM
    n_t = d // tn
    (oa0, la0), (oa1, la1), (oa2, la2) = a_outs

    def rows(a):
        return pl.BlockSpec((TM, a.shape[1]), lambda i, n: (i, 0))

    a1_spec = pl.BlockSpec((None, 4, Q_BLOCK, A_OUT), lambda i, n: (i, 0, 0, 0))
    a2_spec = pl.BlockSpec((None, 16, TM // 16, A_OUT), lambda i, n: (i // per16, 0, i % per16, 0))
    p4t = jnp.asarray(_deinterleave(TM, 4).T, BF16)
    p16t = jnp.asarray(_deinterleave(TM, 16).T, BF16)
    in_specs = [
        rows(x), _resident((1, d)),
        rows(oa0), rows(la0), a1_spec, a1_spec, a2_spec, a2_spec,
        rows(ob), rows(ocmp), rows(oslc), rows(owin), rows(cg),
        _resident((TM, TM)), _resident((TM, TM)), _resident(ex.shape),
        pl.BlockSpec((d, tn), lambda i, n: (0, n)),
        pl.BlockSpec((d, tn), lambda i, n: (0, n + n_t)),
        pl.BlockSpec((d, tn), lambda i, n: (0, n + 2 * n_t)),
        pl.BlockSpec((wa.shape[0], tn), lambda i, n: (0, n)),
        pl.BlockSpec((wb.shape[0], tn), lambda i, n: (0, n)),
        pl.BlockSpec((wc.shape[0], tn), lambda i, n: (0, n)),
    ]
    return pl.pallas_call(
        _merge_body,
        grid=(t // TM, n_t),
        in_specs=in_specs,
        out_specs=pl.BlockSpec((TM, tn), lambda i, n: (i, n)),
        out_shape=jax.ShapeDtypeStruct((t, d), BF16),
        scratch_shapes=[pltpu.VMEM((TM, d), BF16), pltpu.VMEM((TM, wa.shape[0] + wb.shape[0] + wc.shape[0]), BF16)],
        compiler_params=_cparams(("parallel", "arbitrary")),
    )(x, g.reshape(1, d), oa0, la0, oa1, la1, oa2, la2, ob, ocmp, oslc, owin, cg, p4t, p16t, ex,
      w_gate, w_gate, w_gate, wa, wb, wc)


def _out_body(x_ref, m_ref, w_ref, o_ref):
    o_ref[...] = x_ref[...] + _dot(m_ref[...], w_ref[...])


def _out_proj(x, merged, w_out):
    t, d = x.shape
    rows = pl.BlockSpec((TM, d), lambda i: (i, 0))
    return pl.pallas_call(
        _out_body,
        grid=(t // TM,),
        in_specs=[rows, rows, _resident((d, d))],
        out_specs=rows,
        out_shape=jax.ShapeDtypeStruct((t, d), F32),
        compiler_params=_cparams(("parallel",)),
    )(x, merged, w_out)


def _qkv_column_params(qk_gain):
    flag, gain, scale = [], [], []
    one = jnp.ones((HEAD_DIM,), F32)

    def add(n_heads, normed, is_q, gvec, units=1.0):
        for _ in range(n_heads):
            flag.append(np.full((HEAD_DIM,), 1.0 if normed else 0.0, np.float32))
            gain.append(gvec if normed else one)
            scale.append(np.full((HEAD_DIM,), units * HEAD_DIM ** -0.5 if is_q else 1.0, np.float32))

    for _ in range(len(A_GROUPS)):
        add(A_HEADS_PER_GROUP, True, True, qk_gain[0, 0])
        add(A_HEADS_PER_GROUP, True, False, qk_gain[0, 1])
        add(A_HEADS_PER_GROUP, False, False, one)
    add(B_HEADS, True, True, qk_gain[1, 0])
    add(B_KV_HEADS, True, False, qk_gain[1, 1])
    add(B_KV_HEADS, False, False, one)
    add(C_HEADS, True, True, qk_gain[2, 0], units=LOG2E)
    for normed in (False, False, True, False, True, False):
        add(C_KV_HEADS, normed, False, qk_gain[2, 1])
    flag = np.concatenate(flag)
    assert flag.shape[0] == QKV_COLS
    return jnp.asarray(flag), jnp.concatenate(gain) * jnp.asarray(np.concatenate(scale))


def _overlap_t(n_slc, n_pad, n_cmp):
    n = np.arange(n_pad)[None, :]
    j = np.arange(n_slc)[:, None]
    start, end = CMP_STRIDE * n, CMP_STRIDE * n + CMP_BLOCK - 1
    ov = (start <= SLC_BLOCK * j + SLC_BLOCK - 1) & (end >= SLC_BLOCK * j) & (n < n_cmp)
    return jnp.asarray(ov, BF16)


def _gate_expand():
    ex = np.zeros((3, LANES, C_HEADS * HEAD_DIM), np.float32)
    for w in range(3):
        for h in range(C_HEADS):
            ex[w, h * 3 + w, h * HEAD_DIM:(h + 1) * HEAD_DIM] = 1.0
    return jnp.asarray(np.concatenate([ex, ex], axis=1), BF16)


def _compress_weights(cmp_pos, cmp_w1, cmp_w2):
    n_q = 2 * C_KV_HEADS
    w1 = cmp_w1.reshape(2, 2, CMP_STRIDE, HEAD_DIM, CMP_HIDDEN)
    w1q = jnp.repeat(w1, C_KV_HEADS, axis=0)
    wb = jnp.einsum("qhcdn,qp->hcqdpn", w1q, jnp.eye(n_q, dtype=F32))
    wb = wb.reshape(2, CMP_STRIDE, n_q * HEAD_DIM, n_q * CMP_HIDDEN).astype(BF16)
    pos = cmp_pos.reshape(2, 2, CMP_STRIDE, HEAD_DIM)
    prow = jnp.repeat(pos, C_KV_HEADS, axis=0).transpose(1, 2, 0, 3).reshape(2, CMP_STRIDE, 1, n_q * HEAD_DIM)
    prow = jnp.broadcast_to(prow, (2, CMP_STRIDE, 8, n_q * HEAD_DIM)).astype(BF16)
    eye_g = jnp.eye(C_KV_HEADS, dtype=F32)
    w2k = jnp.kron(eye_g, cmp_w2[0]).astype(BF16)
    w2vt = jnp.kron(eye_g, cmp_w2[1]).T.astype(BF16)
    return wb, prow, w2k, w2vt


def _token_mixing(x, b, s, mix_norm, w_in, qk_gain, sinks, cmp_pos, cmp_w1, cmp_w2, w_a, w_b, w_c):
    t, d = x.shape
    assert s % CHUNK16 == 0 and d % 512 == 0
    c_gate_cols = 3 * C_HEADS
    w_qkv = w_in[:, :QKV_COLS + LANES].astype(BF16)
    flag, gs = _qkv_column_params(qk_gain)
    a0, a1, a2, bsec, cq, ckv, cmpd, qt, ks, vst, cg = _qkv_proj(x, mix_norm, w_qkv, flag, gs, b, s)

    a_outs = [_dilated_group(a0.reshape(b, s, SEC), 0, b, s), _dilated_group(a1, 1, b, s),
              _dilated_group(a2, 2, b, s)]
    a_outs[0] = tuple(v.reshape(t, A_OUT) for v in a_outs[0])
    o_b = _sink_swa(bsec.reshape(b, s, SEC), sinks.astype(F32), b, s).reshape(t, -1)
    o_win = _nsa_window(cq.reshape(b, s, SEC), ckv.reshape(b, s, SEC), b, s).reshape(t, -1)

    n_chunks = s // CMP_STRIDE
    n_cmp = (s - CMP_BLOCK) // CMP_STRIDE + 1
    n_slc = s // SLC_BLOCK
    nq = s // Q_BLOCK
    kg = jnp.tile(qk_gain[2, 1], C_KV_HEADS).reshape(1, LANES)
    kc, vct = _compress(cmpd, *_compress_weights(cmp_pos, cmp_w1, cmp_w2), kg)
    o_cmp, sel, cnt = _cmp_select(qt, kc, vct, _overlap_t(n_slc, n_chunks, n_cmp), n_cmp=n_cmp)

    act = (cnt[:, :, :, 0, :] > 0).reshape(b, C_KV_HEADS, nq, nq, 2).any(axis=-1)
    act = act & (jnp.arange(nq)[None, :] < jnp.arange(nq)[:, None])
    order = jnp.argsort(jnp.logical_not(act), axis=-1, stable=True).astype(jnp.int32)
    count = jnp.sum(act, axis=-1, dtype=jnp.int32)[..., None]
    lists = jnp.concatenate([count, order] + [jnp.zeros_like(count)] * (SLC_GROUP - 1), axis=-1)
    o_slc = _slc_attention(lists.reshape(-1), qt, ks.reshape(b, s, 2 * LANES),
                           vst.reshape(b, nq, C_KV_HEADS, HEAD_DIM, Q_BLOCK), sel, stride=nq + SLC_GROUP)

    return _merge(x, mix_norm, a_outs, o_b, o_cmp, o_slc, o_win, cg,
                  w_in[:, QKV_COLS + c_gate_cols:].astype(BF16), w_a.astype(BF16), w_b.astype(BF16),
                  w_c.astype(BF16), _gate_expand())


def kernel(x, ffn1_norm, ffn1_w_gu, ffn1_w_down, mix_norm, w_in, qk_gain, sinks, cmp_pos, cmp_w1, cmp_w2,
           w_branch_a, w_branch_b, w_branch_c, w_out, ffn2_norm, ffn2_w_gu, ffn2_w_down):
    b, s, d = x.shape
    h = x.reshape(b * s, d)
    w1_gu, w1_down = ffn1_w_gu.astype(BF16), ffn1_w_down.astype(BF16)
    w2_gu, w2_down = ffn2_w_gu.astype(BF16), ffn2_w_down.astype(BF16)
    for l in range(ffn1_norm.shape[0]):
        h = _ffn(h, ffn1_norm[l], w1_gu, w1_down, l)
        merged = _token_mixing(h, b, s, mix_norm[l], w_in[l], qk_gain[l], sinks[l], cmp_pos[l], cmp_w1[l],
                               cmp_w2[l], w_branch_a[l], w_branch_b[l], w_branch_c[l])
        h = _out_proj(h, merged, w_out[l].astype(BF16))
        h = _ffn(h, ffn2_norm[l], w2_gu, w2_down, l)
    return h.reshape(b, s, d)
```

```python
import functools
import math

import numpy as np
import jax
import jax.numpy as jnp
from jax import lax
from jax.experimental import pallas as pl
from jax.experimental.pallas import tpu as pltpu

F32 = jnp.float32
BF16 = jnp.bfloat16

HEAD_DIM = 64
Q_BLOCK = 128
LANES = 128
A_GROUPS = ((128, 1), (512, 4), (2048, 16))
A_HEADS_PER_GROUP = 4
A_HEADS = 12
A_OUT = A_HEADS_PER_GROUP * HEAD_DIM
B_HEADS = 8
B_KV_HEADS = 2
B_WINDOW = 128
C_HEADS = 12
C_KV_HEADS = 2
C_REP = C_HEADS // C_KV_HEADS
CMP_BLOCK = 32
CMP_STRIDE = 16
CMP_HIDDEN = 256
SLC_BLOCK = 64
SLC_TOPK = 16
C_WINDOW = 512
RMS_EPS = 1e-6
NEG_INF = -1e30
SEC = 768
N_SEC = 6
QKV_COLS = SEC * N_SEC
TM = 512
CHUNK16 = Q_BLOCK * 16
VMEM_LIMIT = 56 * 1024 * 1024


def _slopes(n):
    return [float(2.0 ** (-8.0 * (h + 1) / n)) for h in range(n)]


def _cparams(sem):
    return pltpu.CompilerParams(dimension_semantics=sem, vmem_limit_bytes=VMEM_LIMIT)


def _dot(a, b):
    return jnp.dot(a, b, preferred_element_type=F32)


def _nt_dot(a, b):
    return lax.dot_general(a, b, (((1,), (1,)), ((), ())), preferred_element_type=F32)


def _split(v):
    hi = v.astype(BF16)
    return hi, (v - hi.astype(F32)).astype(BF16)


def _resident(shape):
    return pl.BlockSpec(shape, lambda *_: (0,) * len(shape), pipeline_mode=pl.Buffered(1))


def _rms_rows(x, g):
    ms = jnp.mean(x * x, axis=-1, keepdims=True)
    return (x * lax.rsqrt(ms + RMS_EPS) * g).astype(BF16)


def _deinterleave(n, d):
    p = np.zeros((n, n), np.float32)
    r = np.arange(n // d)
    for c in range(d):
        p[c * (n // d) + r, d * r + c] = 1.0
    return p


AUG_POS = 0
AUG_ROWS = 16
MASK_BIG = 1e30
SLC_GROUP = 7
LOG2E = math.log2(math.e)


def _key_pattern(n):
    pat = np.zeros((n, LANES), np.float32)
    pat[:, AUG_POS] = pat[:, AUG_POS + 1] = np.arange(n) % Q_BLOCK
    return pat


def _slot_pattern():
    pat = np.zeros((SLC_GROUP, Q_BLOCK, 2 * LANES), np.float32)
    r = np.arange(Q_BLOCK)
    for u in range(SLC_GROUP):
        pat[u, :, LANES + AUG_ROWS * (u + 1)] = r < SLC_BLOCK
        pat[u, :, LANES + AUG_ROWS * (u + 1) + 1] = r >= SLC_BLOCK
    return pat


def _block_spread(n_tiles):
    m = np.zeros((n_tiles * AUG_ROWS, 2 * n_tiles), np.float32)
    jj = np.arange(n_tiles)
    for e in range(2):
        m[AUG_ROWS * jj + e, 2 * jj + e] = 1.0
    return m


def _ffn_body(x_ref, g_ref, wg_ref, wu_ref, wd_ref, o_ref, h_ref, *, n_f):
    f = pl.program_id(1)

    @pl.when(f == 0)
    def _():
        h_ref[...] = _rms_rows(x_ref[...], g_ref[...])
        o_ref[...] = jnp.zeros_like(o_ref)

    h = h_ref[...]
    gate = _dot(h, wg_ref[...])
    up = _dot(h, wu_ref[...])
    act = (gate * jax.nn.sigmoid(gate) * up).astype(BF16)
    o_ref[...] += _dot(act, wd_ref[...])

    @pl.when(f == n_f - 1)
    def _():
        o_ref[...] = x_ref[...] + 0.5 * o_ref[...]


def _ffn(x, g, w_gu, w_down, layer, *, tm=1024, tf=512):
    t, d = x.shape
    d_ff = w_down.shape[1]
    n_f = d_ff // tf
    rows = pl.BlockSpec((tm, d), lambda i, f: (i, 0))
    return pl.pallas_call(
        functools.partial(_ffn_body, n_f=n_f),
        grid=(t // tm, n_f),
        in_specs=[
            rows,
            pl.BlockSpec((1, d), lambda i, f: (0, 0)),
            pl.BlockSpec((None, d, tf), lambda i, f: (layer, 0, f)),
            pl.BlockSpec((None, d, tf), lambda i, f: (layer, 0, f + n_f)),
            pl.BlockSpec((None, tf, d), lambda i, f: (layer, f, 0)),
        ],
        out_specs=rows,
        out_shape=jax.ShapeDtypeStruct((t, d), F32),
        scratch_shapes=[pltpu.VMEM((tm, d), BF16)],
        compiler_params=_cparams(("parallel", "arbitrary")),
    )(x, g.reshape(1, d), w_gu, w_gu, w_down)


NORM_TILE = 256
NORM_TILES = {0: (0, 1), 1: (0, 1), 2: (0, 1), 3: (0, 1, 2), 4: (0, 1, 2), 5: (1, 2)}


def _head_sumsq(y, bd):
    return _dot((y * y).astype(BF16), bd)


def _qkv_body(x_ref, g_ref, w_ref, flag_ref, gs_ref, bd_ref, p4_ref, p16_ref, eye_ref, kpat_ref,
              a0_ref, a1_ref, a2_ref, b_ref, cq_ref, ckv_ref, cmpd_ref, qt_ref, ks_ref, vst_ref, cg_ref):
    h = _rms_rows(x_ref[...], g_ref[...])
    bd = bd_ref[...]

    def project(k):
        if k < len(A_GROUPS):
            return jnp.concatenate([_dot(h, w_ref[:, part * A_HEADS * HEAD_DIM + k * A_OUT:
                                                     part * A_HEADS * HEAD_DIM + (k + 1) * A_OUT])
                                    for part in range(3)], axis=1)
        return _dot(h, w_ref[:, k * SEC:(k + 1) * SEC])

    def finish(k, y):
        tiles = []
        for c in range(SEC // NORM_TILE):
            yc = y[:, c * NORM_TILE:(c + 1) * NORM_TILE]
            if c in NORM_TILES[k]:
                cols = slice(k * SEC + c * NORM_TILE, k * SEC + (c + 1) * NORM_TILE)
                inv = lax.rsqrt(_head_sumsq(yc, bd) * (1.0 / HEAD_DIM) + RMS_EPS)
                yc = yc * jnp.where(flag_ref[:, cols] > 0, inv, 1.0) * gs_ref[:, cols]
            tiles.append(yc.astype(BF16))
        return jnp.concatenate(tiles, axis=1)

    sec = []
    for first in range(0, N_SEC, 3):
        raw = [project(k) for k in range(first, first + 3)]
        sec += [finish(first + k, y) for k, y in enumerate(raw)]
    cg_logits = _dot(h, w_ref[:, QKV_COLS:QKV_COLS + LANES])

    a0_ref[...] = sec[0]
    a1_ref[...] = _dot(p4_ref[...], sec[1]).astype(BF16).reshape(a1_ref.shape)
    a2_ref[...] = _dot(p16_ref[...], sec[2]).astype(BF16).reshape(a2_ref.shape)
    b_ref[...] = sec[3]
    y_cq = sec[4]
    cq_ref[...] = y_cq
    qt_ref[...] = _nt_dot(eye_ref[...], y_cq).astype(BF16)
    y_ckv = sec[5]
    ckv_ref[...] = y_ckv
    cmpd_ref[...] = _dot(p16_ref[...], y_ckv[:, 0:2 * LANES]).astype(BF16).reshape(cmpd_ref.shape)
    ks_ref[:, 0:LANES] = y_ckv[:, 2 * LANES:3 * LANES]
    ks_ref[:, LANES:2 * LANES] = kpat_ref[...]
    eye = eye_ref[0:LANES, 0:LANES]
    for kb in range(vst_ref.shape[0]):
        vt = _nt_dot(eye, y_ckv[kb * Q_BLOCK:(kb + 1) * Q_BLOCK, 3 * LANES:4 * LANES]).astype(BF16)
        for gg in range(C_KV_HEADS):
            vst_ref[kb, gg] = vt[gg * HEAD_DIM:(gg + 1) * HEAD_DIM]
    cg_ref[...] = jax.nn.sigmoid(cg_logits)


def _qkv_proj(x, g, w, flag, gs, b, s):
    t, d = x.shape
    tiles_per_batch = s // TM
    per16 = CHUNK16 // TM
    bd = jnp.asarray(np.kron(np.eye(NORM_TILE // HEAD_DIM), np.ones((HEAD_DIM, HEAD_DIM))), BF16)
    p4 = jnp.asarray(_deinterleave(TM, 4), BF16)
    p16 = jnp.asarray(_deinterleave(TM, 16), BF16)
    eye = jnp.asarray(np.eye(SEC), BF16)
    nat = pl.BlockSpec((TM, SEC), lambda i: (i, 0))
    out_specs = [
        nat,
        pl.BlockSpec((None, 4, Q_BLOCK, SEC), lambda i: (i, 0, 0, 0)),
        pl.BlockSpec((None, 16, TM // 16, SEC), lambda i: (i // per16, 0, i % per16, 0)),
        nat, nat, nat,
        pl.BlockSpec((None, 16, TM // 16, 2 * LANES), lambda i: (i // tiles_per_batch, 0, i % tiles_per_batch, 0)),
        pl.BlockSpec((None, SEC, TM), lambda i: (i // tiles_per_batch, 0, i % tiles_per_batch)),
        pl.BlockSpec((TM, 2 * LANES), lambda i: (i, 0)),
        pl.BlockSpec((TM // Q_BLOCK, C_KV_HEADS, HEAD_DIM, Q_BLOCK), lambda i: (i, 0, 0, 0)),
        pl.BlockSpec((TM, LANES), lambda i: (i, 0)),
    ]
    out_shape = [
        jax.ShapeDtypeStruct((t, SEC), BF16),
        jax.ShapeDtypeStruct((t // TM, 4, Q_BLOCK, SEC), BF16),
        jax.ShapeDtypeStruct((t // CHUNK16, 16, Q_BLOCK, SEC), BF16),
        jax.ShapeDtypeStruct((t, SEC), BF16),
        jax.ShapeDtypeStruct((t, SEC), BF16),
        jax.ShapeDtypeStruct((t, SEC), BF16),
        jax.ShapeDtypeStruct((b, 16, s // 16, 2 * LANES), BF16),
        jax.ShapeDtypeStruct((b, SEC, s), BF16),
        jax.ShapeDtypeStruct((t, 2 * LANES), BF16),
        jax.ShapeDtypeStruct((t // Q_BLOCK, C_KV_HEADS, HEAD_DIM, Q_BLOCK), BF16),
        jax.ShapeDtypeStruct((t, LANES), F32),
    ]
    n_w = w.shape[1]
    return pl.pallas_call(
        _qkv_body,
        grid=(t // TM,),
        in_specs=[
            pl.BlockSpec((TM, d), lambda i: (i, 0)),
            _resident((1, d)),
            _resident((d, n_w)),
            _resident((1, QKV_COLS)),
            _resident((1, QKV_COLS)),
            _resident((NORM_TILE, NORM_TILE)),
            _resident((TM, TM)),
            _resident((TM, TM)),
            _resident((SEC, SEC)),
            _resident((TM, LANES)),
        ],
        out_specs=out_specs,
        out_shape=out_shape,
        compiler_params=_cparams(("parallel",)),
    )(x, g.reshape(1, d), w, flag.reshape(1, -1), gs.reshape(1, -1), bd, p4, p16, eye,
      jnp.asarray(_key_pattern(TM), BF16))


def _banded_body(*refs, nb, qb, keys, heads, k_off, v_off, n_pairs, q_axis, use_sinks, with_lse, stack, fold,
                 log2_units):
    refs = list(refs)
    q_ref = refs.pop(0)
    kv_refs = [refs.pop(0) for _ in range({"self": 1, "window": 2, "blocks": nb + 1}[keys])]
    qc_ref, kaug_ref, band_ref = refs.pop(0), refs.pop(0), refs.pop(0)
    sink_ref = refs.pop(0) if use_sinks else None
    o_ref = refs.pop(0)
    lse_ref = refs.pop(0) if with_lse else None

    i = pl.program_id(q_axis)
    nk = (nb + 1) * Q_BLOCK
    col = lax.broadcasted_iota(jnp.int32, (Q_BLOCK, nk), 1)
    band = band_ref[...]

    def start_mask(sub):
        if sub > 0 and sub >= nb:
            return band
        return band + jnp.where(col < (nb - (i * qb + sub)) * Q_BLOCK, -MASK_BIG, 0.0)

    rel_f = (nb * Q_BLOCK + lax.broadcasted_iota(jnp.int32, (Q_BLOCK, nk), 0) - col).astype(F32)
    lane = lax.broadcasted_iota(jnp.int32, (Q_BLOCK, LANES), 1)
    low_half = lane < HEAD_DIM
    kaug = kaug_ref[...]

    def q_cols(sub, c0):
        if len(q_ref.shape) == 3:
            return q_ref[sub, :, c0:c0 + LANES]
        return q_ref[sub * Q_BLOCK:(sub + 1) * Q_BLOCK, c0:c0 + LANES]

    kv_cache = {}

    def swap_halves(tile, swapped):
        return pltpu.roll(tile.astype(F32), HEAD_DIM, 1).astype(BF16) if swapped else tile

    def kv_tile(sub, off, kv_pair, swapped):
        c0 = off + kv_pair * LANES
        if keys == "window":
            key = (off, kv_pair, swapped)
            if key not in kv_cache:
                kv_cache[key] = swap_halves(jnp.concatenate([r[:, c0:c0 + LANES] for r in kv_refs], axis=0), swapped)
            first = qb - nb + sub
            return kv_cache[key][first * Q_BLOCK:(first + nb + 1) * Q_BLOCK]
        key = (sub, off, kv_pair, swapped)
        if key not in kv_cache:
            if keys == "self":
                blocks = [kv_refs[0][:, c0:c0 + LANES] if sub == 0 else q_cols(sub - 1, c0), q_cols(sub, c0)]
            else:
                blocks = [r[:, c0:c0 + LANES] for r in kv_refs]
            kv_cache[key] = swap_halves(jnp.concatenate(blocks, axis=0), swapped)
        return kv_cache[key]

    classes = {}
    for head in heads:
        pair, half, kv_pair, kv_half, slope, hidx = head
        key = (kv_pair, kv_half != half) if stack else hidx
        classes.setdefault(key, []).append(head)

    outs = [[[None, None] for _ in range(n_pairs)] for _ in range(qb)]
    lses = [[[None, None] for _ in range(n_pairs)] for _ in range(qb)]
    groups = [(sub, members) for sub in range(qb) for members in classes.values()]

    scores = []
    for sub, members in groups:
        n_h = len(members)
        kv_pair, swapped = members[0][2], members[0][3] != members[0][1]
        mask = start_mask(sub)
        q_rows = []
        for pair, half, _, _, _, hidx in members:
            qp = q_cols(sub, pair * LANES)
            own = low_half if half == 0 else jnp.logical_not(low_half)
            qm = jnp.where(own, qp, jnp.zeros_like(qp))
            q_rows.append(jnp.concatenate([qm, qc_ref[hidx]], axis=1) if fold else qm)
        if fold:
            k_aug = jnp.concatenate([kv_tile(sub, k_off, kv_pair, swapped), kaug], axis=1)
            s = _nt_dot(jnp.concatenate(q_rows, axis=0), k_aug)
            s = (s.reshape(n_h, Q_BLOCK, nk) + mask[None]).reshape(n_h * Q_BLOCK, nk)
        else:
            bias = jnp.concatenate([mask - member[4] * rel_f for member in members], axis=0)
            s = _nt_dot(jnp.concatenate(q_rows, axis=0), kv_tile(sub, k_off, kv_pair, swapped)) + bias
        scores.append(s)

    probs = []
    for (_, members), s in zip(groups, scores):
        m = jnp.max(s, axis=1, keepdims=True)
        if use_sinks:
            assert len(members) == 1
            sink = sink_ref[members[0][5]]
            m = jnp.maximum(m, sink)
        p = jnp.exp2(s - m) if log2_units else jnp.exp(s - m)
        den = jnp.sum(p, axis=1, keepdims=True)
        if use_sinks:
            den = den + jnp.exp(sink - m)
        probs.append((p.astype(BF16), m, den))

    for (sub, members), (p, m, den) in zip(groups, probs):
        kv_pair, swapped = members[0][2], members[0][3] != members[0][1]
        r = _dot(p, kv_tile(sub, v_off, kv_pair, swapped)) / den
        lse = m + jnp.log(den) if with_lse else None
        for k, (pair, half, _, _, _, _) in enumerate(members):
            outs[sub][pair][half] = r[k * Q_BLOCK:(k + 1) * Q_BLOCK]
            if with_lse:
                lses[sub][pair][half] = jnp.broadcast_to(lse[k * Q_BLOCK:(k + 1) * Q_BLOCK], (Q_BLOCK, LANES))

    def store(ref, sub, sl, value):
        if len(ref.shape) == 3:
            ref[sub, :, sl] = value
        else:
            ref[sub * Q_BLOCK:(sub + 1) * Q_BLOCK, sl] = value

    for sub in range(qb):
        for pair in range(n_pairs):
            sl = slice(pair * LANES, (pair + 1) * LANES)
            store(o_ref, sub, sl, jnp.where(low_half, outs[sub][pair][0], outs[sub][pair][1]))
            if with_lse:
                store(lse_ref, sub, sl, jnp.where(low_half, lses[sub][pair][0], lses[sub][pair][1]))


def _banded_consts(heads, nb, max_dist):
    nk = (nb + 1) * Q_BLOCK
    slope = np.asarray([h[4] for h in heads], np.float32)[:, None]
    q_dist = (nb * Q_BLOCK + np.arange(Q_BLOCK, dtype=np.float32))[None, :]
    ones = np.ones_like(q_dist)
    vals = jnp.asarray(np.stack([slope * ones, slope * Q_BLOCK * ones, -slope * q_dist], axis=-1))
    hi = vals.astype(BF16)
    lo = (vals - hi.astype(F32)).astype(BF16)
    cols = jnp.stack([hi[..., 0], lo[..., 0], hi[..., 1], lo[..., 1], hi[..., 2], lo[..., 2]], axis=-1)
    qc = jnp.pad(cols, ((0, 0), (0, 0), (0, LANES - cols.shape[-1])))
    kaug = np.zeros((nk, LANES), np.float32)
    kaug[:, 0] = kaug[:, 1] = np.arange(nk) % Q_BLOCK
    kaug[:, 2] = kaug[:, 3] = np.arange(nk) // Q_BLOCK
    kaug[:, 4] = kaug[:, 5] = 1.0
    rel = nb * Q_BLOCK + np.arange(Q_BLOCK)[:, None] - np.arange(nk)[None, :]
    band = np.where((rel >= 0) & (rel <= max_dist), 0.0, -MASK_BIG).astype(np.float32)
    return [qc, jnp.asarray(kaug, BF16), jnp.asarray(band)]


BAND_QB = 4
CMP_QB = 1


def _banded_call(q_arr, kv_arr, *, grid, q_spec, kv_specs, out_spec, out_shape, out_cols, nb, qb, max_dist, heads,
                 k_off, v_off, q_axis, sinks=None, with_lse=False, stack=False, fold=False, log2_units=False):
    assert not (log2_units and (with_lse or sinks is not None))
    if len(kv_specs) == 1 and nb == 1:
        keys = "self"
    elif qb > 1:
        keys = "window"
        assert len(kv_specs) == 2 and nb <= qb
    else:
        keys = "blocks"
        assert len(kv_specs) == nb + 1
    consts = _banded_consts(heads, nb, max_dist)
    in_specs = [q_spec] + list(kv_specs) + [pl.BlockSpec(c.shape, lambda *_, nd=c.ndim: (0,) * nd) for c in consts]
    args = [q_arr] + [kv_arr] * len(kv_specs) + consts
    if sinks is not None:
        in_specs.append(pl.BlockSpec(memory_space=pltpu.SMEM))
        args.append(sinks)
    oshape = jax.ShapeDtypeStruct(out_shape, F32)
    body = functools.partial(_banded_body, nb=nb, qb=qb, keys=keys, heads=heads, k_off=k_off, v_off=v_off,
                             n_pairs=out_cols // LANES, q_axis=q_axis, use_sinks=sinks is not None,
                             with_lse=with_lse, stack=stack, fold=fold, log2_units=log2_units)
    return pl.pallas_call(
        body,
        grid=grid,
        in_specs=in_specs,
        out_specs=[out_spec, out_spec] if with_lse else out_spec,
        out_shape=[oshape, oshape] if with_lse else oshape,
        compiler_params=_cparams(("parallel",) * len(grid)),
    )(*args)


def _row_specs(b, s, qb, out_cols):
    q_spec = pl.BlockSpec((None, qb * Q_BLOCK, SEC), lambda bb, i: (bb, i, 0))
    prev = pl.BlockSpec((None, Q_BLOCK, SEC), lambda bb, i: (bb, jnp.maximum(qb * i - 1, 0), 0))
    out_spec = pl.BlockSpec((None, qb * Q_BLOCK, out_cols), lambda bb, i: (bb, i, 0))
    return dict(grid=(b, s // (qb * Q_BLOCK)), q_axis=1, q_spec=q_spec, kv_specs=[prev], out_spec=out_spec,
                out_shape=(b, s, out_cols), qb=qb)


def _dilated_group(arr, gi, b, s):
    window, dil = A_GROUPS[gi]
    slopes = _slopes(A_HEADS)
    heads = tuple((hh // 2, hh % 2, hh // 2, hh % 2, slopes[gi * A_HEADS_PER_GROUP + hh] * dil, hh)
                  for hh in range(A_HEADS_PER_GROUP))
    common = dict(out_cols=A_OUT, nb=1, max_dist=window // dil, heads=heads, k_off=256, v_off=512,
                  with_lse=True, stack=False)
    if dil == 1:
        return _banded_call(arr, arr, **_row_specs(b, s, BAND_QB, A_OUT), **common)
    nc = s // (Q_BLOCK * dil)
    qb = BAND_QB if nc % BAND_QB == 0 else 1
    q_spec = pl.BlockSpec((qb, None, Q_BLOCK, SEC), lambda bb, c, i: ((bb * nc) // qb + i, c, 0, 0))
    prev = pl.BlockSpec((None, None, Q_BLOCK, SEC), lambda bb, c, i: (bb * nc + jnp.maximum(qb * i - 1, 0), c, 0, 0))
    out_spec = pl.BlockSpec((qb, None, Q_BLOCK, A_OUT), lambda bb, c, i: ((bb * nc) // qb + i, c, 0, 0))
    return _banded_call(arr, arr, grid=(b, dil, nc // qb), q_axis=2, q_spec=q_spec, kv_specs=[prev],
                        out_spec=out_spec, out_shape=(b * nc, dil, Q_BLOCK, A_OUT), qb=qb, **common)


def _sink_swa(arr, sinks, b, s):
    slopes = _slopes(B_HEADS)
    rep = B_HEADS // B_KV_HEADS
    heads = tuple((h // 2, h % 2, 0, h // rep, slopes[h], h) for h in range(B_HEADS))
    return _banded_call(arr, arr, **_row_specs(b, s, BAND_QB, B_HEADS * HEAD_DIM), out_cols=B_HEADS * HEAD_DIM,
                        nb=1, max_dist=B_WINDOW - 1, heads=heads, k_off=512, v_off=640, sinks=sinks)


def _nsa_window(cq, ckv, b, s):
    slopes = _slopes(C_HEADS)
    heads = tuple((h // 2, h % 2, 0, h // C_REP, slopes[h] * LOG2E, h) for h in range(C_HEADS))
    nb = -(-(C_WINDOW - 1) // Q_BLOCK)
    qb = nb
    blk = (None, qb * Q_BLOCK, SEC)
    kv_specs = [pl.BlockSpec(blk, lambda bb, i: (bb, jnp.maximum(i - 1, 0), 0)),
                pl.BlockSpec(blk, lambda bb, i: (bb, i, 0))]
    out_cols = C_HEADS * HEAD_DIM
    return _banded_call(
        cq, ckv, grid=(b, s // (qb * Q_BLOCK)), q_axis=1, qb=qb, log2_units=True,
        q_spec=pl.BlockSpec(blk, lambda bb, i: (bb, i, 0)), kv_specs=kv_specs,
        out_spec=pl.BlockSpec((None, qb * Q_BLOCK, out_cols), lambda bb, i: (bb, i, 0)),
        out_shape=(b, s, out_cols), out_cols=out_cols, nb=nb, max_dist=C_WINDOW - 1, heads=heads,
        k_off=512, v_off=640, stack=True, fold=True)


def _compress_body(t_ref, wb_ref, prow_ref, w2k_ref, w2vt_ref, kg_ref, bd_ref, kc_ref, vct_ref, *, n_chunks):
    hid_cols = 2 * C_KV_HEADS * CMP_HIDDEN
    u = jnp.zeros((n_chunks, hid_cols), F32)
    v = jnp.zeros((n_chunks, hid_cols), F32)
    pc = jnp.zeros((1, hid_cols), F32)
    for c in range(CMP_STRIDE):
        tc = t_ref[c]
        u = u + _dot(tc, wb_ref[0, c])
        v = v + _dot(tc, wb_ref[1, c])
        pc = pc + _dot(prow_ref[0, c], wb_ref[0, c])[0:1] + _dot(prow_ref[1, c], wb_ref[1, c])[0:1]
    hsum = u + pltpu.roll(v, n_chunks - 1, 0) + pc
    hid = (hsum * jax.nn.sigmoid(hsum)).astype(BF16)
    half = C_KV_HEADS * CMP_HIDDEN
    k = _dot(hid[:, :half], w2k_ref[...])
    hi, lo = _split(k * k)
    ss = _dot(hi, bd_ref[...]) + _dot(lo, bd_ref[...])
    kc_ref[...] = (k * lax.rsqrt(ss * (1.0 / HEAD_DIM) + RMS_EPS) * kg_ref[...]).astype(BF16)
    vct_ref[...] = _nt_dot(w2vt_ref[...], hid[:, half:]).astype(BF16)


def _compress(cmpd, wb, prow, w2k, w2vt, kg):
    b, _, n_chunks, width = cmpd.shape
    bd = jnp.asarray(np.kron(np.eye(LANES // HEAD_DIM), np.ones((HEAD_DIM, HEAD_DIM))), BF16)
    return pl.pallas_call(
        functools.partial(_compress_body, n_chunks=n_chunks),
        grid=(b,),
        in_specs=[
            pl.BlockSpec((None, CMP_STRIDE, n_chunks, width), lambda bb: (bb, 0, 0, 0)),
            _resident(wb.shape), _resident(prow.shape), _resident(w2k.shape), _resident(w2vt.shape),
            _resident((1, LANES)), _resident((LANES, LANES)),
        ],
        out_specs=[
            pl.BlockSpec((None, n_chunks, LANES), lambda bb: (bb, 0, 0)),
            pl.BlockSpec((None, LANES, n_chunks), lambda bb: (bb, 0, 0)),
        ],
        out_shape=[
            jax.ShapeDtypeStruct((b, n_chunks, LANES), BF16),
            jax.ShapeDtypeStruct((b, LANES, n_chunks), BF16),
        ],
        compiler_params=_cparams(("parallel",)),
    )(cmpd, wb, prow, w2k, w2vt, kg, bd)


def _to_natural(ot_list, eye):
    pairs = []
    for k in range(0, len(ot_list), 2):
        hi, lo = _split(jnp.concatenate([ot_list[k], ot_list[k + 1]], axis=0))
        pairs.append(_nt_dot(eye, hi) + _nt_dot(eye, lo))
    return jnp.concatenate(pairs, axis=1)


def _cmp_body(qt_ref, kc_ref, vct_ref, ovt_ref, eye_ref, spread_ref, cmask_ref,
              o_ref, selm_ref, cnt_ref, sel_sc, *, n_top):
    g = pl.program_id(1)
    i = pl.program_id(2)
    n_pad = kc_ref.shape[0]
    n_slc = ovt_ref.shape[0]
    qw = qt_ref.shape[1]
    qb = qw // Q_BLOCK
    per_q = Q_BLOCK // CMP_STRIDE
    own_rows = (lax.broadcasted_iota(jnp.int32, (LANES, qw), 0) // HEAD_DIM) == g
    slopes = [sl * LOG2E for sl in _slopes(C_HEADS)]

    def attend(rows):
        kc = kc_ref[0:rows, :]
        vct = vct_ref[:, 0:rows]
        mask = jnp.concatenate(
            [cmask_ref[pl.ds(pl.multiple_of(n_pad - per_q * (qb * i + h), 8), rows), :] for h in range(qb)], axis=1)
        n_f = (CMP_STRIDE * lax.broadcasted_iota(jnp.int32, (rows, qw), 0)).astype(F32)
        psum = jnp.zeros((rows, qw), F32)
        outs = []
        for r in range(C_REP):
            slope = jnp.where(g == 0, slopes[r], slopes[C_REP + r])
            qt = qt_ref[r * HEAD_DIM:(r + 1) * HEAD_DIM, :]
            q_pad = jnp.where(own_rows, jnp.concatenate([qt, qt], axis=0), jnp.zeros((LANES, qw), BF16))
            s = _dot(kc, q_pad) + (slope * n_f + mask)
            m = jnp.maximum(jnp.max(s, axis=0, keepdims=True), -1e20)
            e = jnp.exp2(s - m)
            den = jnp.sum(e, axis=0, keepdims=True)
            p = e * (1.0 / jnp.where(den > 0, den, 1.0))
            psum = psum + p
            both = _dot(vct, p.astype(BF16))
            outs.append(jnp.where(g == 0, both[:HEAD_DIM], both[HEAD_DIM:]))
        o_ref[...] = _to_natural(outs, eye_ref[...])

        n_j = rows * CMP_STRIDE // SLC_BLOCK
        hi, lo = _split(psum)
        ovt = ovt_ref[0:n_j, 0:rows]
        imp = _dot(ovt, hi) + _dot(ovt, lo)
        j_idx = lax.broadcasted_iota(jnp.int32, (n_j, qw), 0)
        t_q = i * qw + lax.broadcasted_iota(jnp.int32, (n_j, qw), 1)
        cur = lax.shift_right_logical(t_q, int(math.log2(SLC_BLOCK)))
        forced = ((j_idx == 0) | (j_idx == cur) | (j_idx == cur - 1)) & (j_idx <= cur)
        v = jnp.where((j_idx <= cur) & jnp.logical_not(forced), imp, -1.0)
        sel = jnp.where(forced, 1.0, 0.0)
        for _ in range(n_top - 3):
            m = jnp.max(v, axis=0, keepdims=True)
            first = jnp.min(jnp.where((v == m) & (m >= 0.0), j_idx, n_slc), axis=0, keepdims=True)
            pick = j_idx == first
            sel = jnp.where(pick, 1.0, sel)
            v = jnp.where(pick, -1.0, v)
        sel_sc[0:n_j, :] = sel
        if n_j < n_slc:
            sel_sc[n_j:, :] = jnp.zeros((n_slc - n_j, qw), F32)

    n_var = n_pad // LANES
    for var in range(n_var):
        pl.when((qb * i + qb - 1) // (LANES // per_q) == var)(functools.partial(attend, (var + 1) * LANES))

    sel = sel_sc[...]
    neg = jnp.where(sel > 0, 0.0, -MASK_BIG).astype(BF16)
    mask_rows = _dot(spread_ref[...], neg).astype(BF16)
    sel_b = sel.astype(BF16)
    for h in range(qb):
        lanes = slice(h * Q_BLOCK, (h + 1) * Q_BLOCK)
        selm_ref[h] = mask_rows[:, lanes]
        cnt_ref[h] = _nt_dot(jnp.ones((8, Q_BLOCK), BF16), sel_b[:, lanes])


def _cmp_select(qt, kc, vct, ovt, *, n_cmp):
    b, _, s = qt.shape
    g = C_KV_HEADS
    n_pad = kc.shape[1]
    n_slc = ovt.shape[0]
    nq = s // Q_BLOCK
    rows = C_REP * HEAD_DIM
    qb = CMP_QB if nq % CMP_QB == 0 else 1
    qw = qb * Q_BLOCK
    steps = nq // qb
    eye = jnp.asarray(np.eye(qw), BF16)
    n_top = min(SLC_TOPK, n_slc)
    assert n_top > 3 and n_cmp == n_pad - 1 and n_pad % LANES == 0
    n_rel = np.arange(-n_pad, n_pad)[:, None]
    cmask = np.where(CMP_STRIDE * n_rel + CMP_BLOCK - 1 <= np.arange(Q_BLOCK)[None, :], 0.0, -MASK_BIG)
    return pl.pallas_call(
        functools.partial(_cmp_body, n_top=n_top),
        grid=(b, g, steps),
        in_specs=[
            pl.BlockSpec((None, rows, qw), lambda bb, gg, i: (bb, gg, i)),
            pl.BlockSpec((None, n_pad, LANES), lambda bb, gg, i: (bb, 0, 0)),
            pl.BlockSpec((None, LANES, n_pad), lambda bb, gg, i: (bb, 0, 0)),
            pl.BlockSpec((n_slc, n_pad), lambda bb, gg, i: (0, 0)),
            pl.BlockSpec((qw, qw), lambda bb, gg, i: (0, 0)),
            pl.BlockSpec((nq * AUG_ROWS, n_slc), lambda bb, gg, i: (0, 0)),
            pl.BlockSpec((2 * n_pad, Q_BLOCK), lambda bb, gg, i: (0, 0)),
        ],
        out_specs=[
            pl.BlockSpec((qw, rows), lambda bb, gg, i: (bb * steps + i, gg)),
            pl.BlockSpec((None, None, qb, nq * AUG_ROWS, Q_BLOCK), lambda bb, gg, i: (bb, gg, i, 0, 0)),
            pl.BlockSpec((None, None, qb, 8, n_slc), lambda bb, gg, i: (bb, gg, i, 0, 0)),
        ],
        out_shape=[
            jax.ShapeDtypeStruct((b * s, g * rows), F32),
            jax.ShapeDtypeStruct((b, g, nq, nq * AUG_ROWS, Q_BLOCK), BF16),
            jax.ShapeDtypeStruct((b, g, nq, 8, n_slc), F32),
        ],
        scratch_shapes=[pltpu.VMEM((n_slc, qw), F32)],
        compiler_params=_cparams(("parallel", "parallel", "parallel")),
    )(qt, kc, vct, ovt, eye, jnp.asarray(_block_spread(nq), BF16), jnp.asarray(cmask, F32))


def _slc_body(list_ref, qt_ref, ks_ref, vst_ref, selm_ref, eye_ref, slot_ref, o_ref, qaug, m_sc, l_sc, acc_sc,
              *, nq, stride):
    bb = pl.program_id(0)
    g = pl.program_id(1)
    i = pl.program_id(2)
    width = C_REP * Q_BLOCK
    slopes = [sl * LOG2E for sl in _slopes(C_HEADS)]
    slope_s = [jnp.where(g == 0, slopes[r], slopes[C_REP + r]) for r in range(C_REP)]

    own_rows = (lax.broadcasted_iota(jnp.int32, (LANES, width), 0) // HEAD_DIM) == g
    q6 = jnp.concatenate([qt_ref[r * HEAD_DIM:(r + 1) * HEAD_DIM, :] for r in range(C_REP)], axis=1)
    qaug[0:LANES, :] = jnp.where(own_rows, jnp.concatenate([q6, q6], axis=0), jnp.zeros((LANES, width), BF16))
    head = lax.broadcasted_iota(jnp.int32, (LANES, width), 1) // Q_BLOCK
    row = lax.broadcasted_iota(jnp.int32, (LANES, width), 0)
    slope_t = jnp.zeros((LANES, width), F32)
    for r in range(C_REP):
        slope_t = jnp.where(head == r, slope_s[r], slope_t)
    s_hi, s_lo = _split(slope_t)
    slope_rows = jnp.where(row == AUG_POS, s_hi.astype(F32), jnp.where(row == AUG_POS + 1, s_lo.astype(F32), 0.0))
    qaug[LANES:, :] = slope_rows.astype(BF16)

    q_loc = lax.broadcasted_iota(jnp.int32, (Q_BLOCK, Q_BLOCK), 1)
    k_loc = lax.broadcasted_iota(jnp.int32, (Q_BLOCK, Q_BLOCK), 0)
    causal = jnp.where(k_loc > q_loc, -MASK_BIG, 0.0)

    m_sc[...] = jnp.full(m_sc.shape, NEG_INF, F32)
    l_sc[...] = jnp.zeros(l_sc.shape, F32)
    acc_sc[...] = jnp.zeros(acc_sc.shape, F32)

    def accumulate(tiles, own_first):
        keys = []
        for u, (jj, _) in enumerate(tiles):
            keys.append(ks_ref[pl.ds(pl.multiple_of(jj * Q_BLOCK, Q_BLOCK), Q_BLOCK), :] + slot_ref[u])
            rows = selm_ref[pl.ds(pl.multiple_of(jj * AUG_ROWS, AUG_ROWS), AUG_ROWS), :]
            lo = LANES + AUG_ROWS * (u + 1)
            qaug[lo:lo + AUG_ROWS, :] = jnp.concatenate([rows] * C_REP, axis=1)
        st = _dot(jnp.concatenate(keys, axis=0), qaug[...])
        ps, alphas = [], []
        for r in range(C_REP):
            sl = slice(r * Q_BLOCK, (r + 1) * Q_BLOCK)
            m_old = m_sc[:, sl]
            m_new = m_old
            parts = []
            for u, (jj, extra) in enumerate(tiles):
                s = st[u * Q_BLOCK:(u + 1) * Q_BLOCK, sl]
                if own_first and u == 0:
                    s = s + causal
                c = slope_s[r] * ((jj - i) * Q_BLOCK).astype(F32) + extra
                m_new = jnp.maximum(m_new, jnp.max(s, axis=0, keepdims=True) + c)
                parts.append((s, c))
            alpha = jnp.exp2(m_old - m_new)
            l_new = alpha * l_sc[:, sl]
            p_rows = []
            for s, c in parts:
                p = jnp.exp2(s + (c - m_new))
                l_new = l_new + jnp.sum(p, axis=0, keepdims=True)
                p_rows.append(p.astype(BF16))
            l_sc[:, sl] = l_new
            m_sc[:, sl] = m_new
            ps.append(jnp.concatenate(p_rows, axis=0))
            alphas.append(alpha)
        values = jnp.concatenate([vst_ref[jj] for jj, _ in tiles], axis=1)
        pv = _dot(values, jnp.concatenate(ps, axis=1))
        acc_sc[...] = jnp.concatenate(alphas, axis=1) * acc_sc[...] + pv

    base = ((bb * pl.num_programs(1) + g) * nq + i) * stride
    count = list_ref[base]

    def listed(slot):
        return list_ref[base + 1 + slot], jnp.where(slot < count, 0.0, -MASK_BIG)

    accumulate([(i, 0.0)] + [listed(u) for u in range(SLC_GROUP - 1)], True)

    def step(k, carry):
        accumulate([listed(SLC_GROUP - 1 + SLC_GROUP * k + u) for u in range(SLC_GROUP)], False)
        return carry

    rest = jnp.maximum(count - (SLC_GROUP - 1), 0)
    lax.fori_loop(0, (rest + SLC_GROUP - 1) // SLC_GROUP, step, 0)
    l = l_sc[...]
    o = acc_sc[...] / jnp.where(l > 0, l, 1.0)
    o_ref[...] = _to_natural([o[:, r * Q_BLOCK:(r + 1) * Q_BLOCK] for r in range(C_REP)], eye_ref[...])


def _slc_attention(lists, qt, ks, vst, sel, *, stride):
    b, _, s = qt.shape
    g = C_KV_HEADS
    nq = s // Q_BLOCK
    rows = C_REP * HEAD_DIM
    width = C_REP * Q_BLOCK
    eye = jnp.asarray(np.eye(LANES), BF16)
    assert AUG_ROWS * (SLC_GROUP + 1) <= LANES
    slots = jnp.asarray(_slot_pattern(), BF16)
    grid_spec = pltpu.PrefetchScalarGridSpec(
        num_scalar_prefetch=1,
        grid=(b, g, nq),
        in_specs=[
            pl.BlockSpec((None, rows, Q_BLOCK), lambda bb, gg, i, bits: (bb, gg, i)),
            pl.BlockSpec((None, s, 2 * LANES), lambda bb, gg, i, bits: (bb, 0, 0)),
            pl.BlockSpec((None, nq, None, HEAD_DIM, Q_BLOCK), lambda bb, gg, i, bits: (bb, 0, gg, 0, 0)),
            pl.BlockSpec((None, None, None, nq * AUG_ROWS, Q_BLOCK), lambda bb, gg, i, bits: (bb, gg, i, 0, 0)),
            pl.BlockSpec((LANES, LANES), lambda bb, gg, i, bits: (0, 0)),
            pl.BlockSpec(slots.shape, lambda bb, gg, i, bits: (0, 0, 0)),
        ],
        out_specs=pl.BlockSpec((Q_BLOCK, rows), lambda bb, gg, i, bits: (bb * nq + i, gg)),
        scratch_shapes=[
            pltpu.VMEM((2 * LANES, width), BF16),
            pltpu.VMEM((1, width), F32),
            pltpu.VMEM((1, width), F32),
            pltpu.VMEM((HEAD_DIM, width), F32),
        ],
    )
    return pl.pallas_call(
        functools.partial(_slc_body, nq=nq, stride=stride),
        grid_spec=grid_spec,
        out_shape=jax.ShapeDtypeStruct((b * s, g * rows), F32),
        compiler_params=_cparams(("parallel", "parallel", "parallel")),
    )(lists, qt, ks, vst, sel, eye, slots)


def _merge_body(x_ref, g_ref, oa0, la0, oa1, la1, oa2, la2, ob_ref, ocmp_ref, oslc_ref, owin_ref, cg_ref,
                p4t_ref, p16t_ref, ex_ref, wg0_ref, wg1_ref, wg2_ref, wa_ref, wb_ref, wc_ref,
                out_ref, h_ref, oall_ref):
    @pl.when(pl.program_id(1) == 0)
    def _():
        def natural(ref, pt_ref):
            hi, lo = _split(ref[...].reshape(TM, A_OUT))
            return _dot(pt_ref[...], hi) + _dot(pt_ref[...], lo)

        o1, l1 = natural(oa1, p4t_ref), natural(la1, p4t_ref)
        o2, l2 = natural(oa2, p16t_ref), natural(la2, p16t_ref)
        cg_split = jnp.concatenate(_split(cg_ref[...]), axis=1)
        gate_c = [_dot(cg_split, ex_ref[w]) for w in range(3)]

        h_ref[...] = _rms_rows(x_ref[...], g_ref[...])
        o0, l0 = oa0[...], la0[...]
        mx = jnp.maximum(jnp.maximum(l0, l1), l2)
        e0, e1, e2 = jnp.exp(l0 - mx), jnp.exp(l1 - mx), jnp.exp(l2 - mx)
        oall_ref[:, 0:A_OUT] = ((e0 * o0 + e1 * o1 + e2 * o2) / (e0 + e1 + e2)).astype(BF16)
        oall_ref[:, A_OUT:A_OUT + B_HEADS * HEAD_DIM] = ob_ref[...].astype(BF16)
        o_c = gate_c[0] * ocmp_ref[...] + gate_c[1] * oslc_ref[...] + gate_c[2] * owin_ref[...]
        oall_ref[:, A_OUT + B_HEADS * HEAD_DIM:] = o_c.astype(BF16)

    h = h_ref[...]
    c0, c1 = A_OUT, A_OUT + B_HEADS * HEAD_DIM
    merged = jax.nn.sigmoid(_dot(h, wg0_ref[...])) * _dot(oall_ref[:, 0:c0], wa_ref[...])
    merged += jax.nn.sigmoid(_dot(h, wg1_ref[...])) * _dot(oall_ref[:, c0:c1], wb_ref[...])
    merged += jax.nn.sigmoid(_dot(h, wg2_ref[...])) * _dot(oall_ref[:, c1:], wc_ref[...])
    out_ref[...] = merged.astype(BF16)


def _merge(x, g, a_outs, ob, ocmp, oslc, owin, cg, w_gate, wa, wb, wc, ex, *, tn=512):
    t, d = x.shape
    per16 = CHUNK16 // TM
    n_t = d // tn
    (oa0, la0), (oa1, la1), (oa2, la2) = a_outs

    def rows(a):
        return pl.BlockSpec((TM, a.shape[1]), lambda i, n: (i, 0))

    a1_spec = pl.BlockSpec((None, 4, Q_BLOCK, A_OUT), lambda i, n: (i, 0, 0, 0))
    a2_spec = pl.BlockSpec((None, 16, TM // 16, A_OUT), lambda i, n: (i // per16, 0, i % per16, 0))
    p4t = jnp.asarray(_deinterleave(TM, 4).T, BF16)
    p16t = jnp.asarray(_deinterleave(TM, 16).T, BF16)
    in_specs = [
        rows(x), _resident((1, d)),
        rows(oa0), rows(la0), a1_spec, a1_spec, a2_spec, a2_spec,
        rows(ob), rows(ocmp), rows(oslc), rows(owin), rows(cg),
        _resident((TM, TM)), _resident((TM, TM)), _resident(ex.shape),
        pl.BlockSpec((d, tn), lambda i, n: (0, n)),
        pl.BlockSpec((d, tn), lambda i, n: (0, n + n_t)),
        pl.BlockSpec((d, tn), lambda i, n: (0, n + 2 * n_t)),
        pl.BlockSpec((wa.shape[0], tn), lambda i, n: (0, n)),
        pl.BlockSpec((wb.shape[0], tn), lambda i, n: (0, n)),
        pl.BlockSpec((wc.shape[0], tn), lambda i, n: (0, n)),
    ]
    return pl.pallas_call(
        _merge_body,
        grid=(t // TM, n_t),
        in_specs=in_specs,
        out_specs=pl.BlockSpec((TM, tn), lambda i, n: (i, n)),
        out_shape=jax.ShapeDtypeStruct((t, d), BF16),
        scratch_shapes=[pltpu.VMEM((TM, d), BF16), pltpu.VMEM((TM, wa.shape[0] + wb.shape[0] + wc.shape[0]), BF16)],
        compiler_params=_cparams(("parallel", "arbitrary")),
    )(x, g.reshape(1, d), oa0, la0, oa1, la1, oa2, la2, ob, ocmp, oslc, owin, cg, p4t, p16t, ex,
      w_gate, w_gate, w_gate, wa, wb, wc)


def _out_body(x_ref, m_ref, w_ref, o_ref):
    o_ref[...] = x_ref[...] + _dot(m_ref[...], w_ref[...])


def _out_proj(x, merged, w_out):
    t, d = x.shape
    rows = pl.BlockSpec((TM, d), lambda i: (i, 0))
    return pl.pallas_call(
        _out_body,
        grid=(t // TM,),
        in_specs=[rows, rows, _resident((d, d))],
        out_specs=rows,
        out_shape=jax.ShapeDtypeStruct((t, d), F32),
        compiler_params=_cparams(("parallel",)),
    )(x, merged, w_out)


def _qkv_column_params(qk_gain):
    flag, gain, scale = [], [], []
    one = jnp.ones((HEAD_DIM,), F32)

    def add(n_heads, normed, is_q, gvec, units=1.0):
        for _ in range(n_heads):
            flag.append(np.full((HEAD_DIM,), 1.0 if normed else 0.0, np.float32))
            gain.append(gvec if normed else one)
            scale.append(np.full((HEAD_DIM,), units * HEAD_DIM ** -0.5 if is_q else 1.0, np.float32))

    for _ in range(len(A_GROUPS)):
        add(A_HEADS_PER_GROUP, True, True, qk_gain[0, 0])
        add(A_HEADS_PER_GROUP, True, False, qk_gain[0, 1])
        add(A_HEADS_PER_GROUP, False, False, one)
    add(B_HEADS, True, True, qk_gain[1, 0])
    add(B_KV_HEADS, True, False, qk_gain[1, 1])
    add(B_KV_HEADS, False, False, one)
    add(C_HEADS, True, True, qk_gain[2, 0], units=LOG2E)
    for normed in (False, False, True, False, True, False):
        add(C_KV_HEADS, normed, False, qk_gain[2, 1])
    flag = np.concatenate(flag)
    assert flag.shape[0] == QKV_COLS
    return jnp.asarray(flag), jnp.concatenate(gain) * jnp.asarray(np.concatenate(scale))


def _overlap_t(n_slc, n_pad, n_cmp):
    n = np.arange(n_pad)[None, :]
    j = np.arange(n_slc)[:, None]
    start, end = CMP_STRIDE * n, CMP_STRIDE * n + CMP_BLOCK - 1
    ov = (start <= SLC_BLOCK * j + SLC_BLOCK - 1) & (end >= SLC_BLOCK * j) & (n < n_cmp)
    return jnp.asarray(ov, BF16)


def _gate_expand():
    ex = np.zeros((3, LANES, C_HEADS * HEAD_DIM), np.float32)
    for w in range(3):
        for h in range(C_HEADS):
            ex[w, h * 3 + w, h * HEAD_DIM:(h + 1) * HEAD_DIM] = 1.0
    return jnp.asarray(np.concatenate([ex, ex], axis=1), BF16)


def _compress_weights(cmp_pos, cmp_w1, cmp_w2):
    n_q = 2 * C_KV_HEADS
    w1 = cmp_w1.reshape(2, 2, CMP_STRIDE, HEAD_DIM, CMP_HIDDEN)
    w1q = jnp.repeat(w1, C_KV_HEADS, axis=0)
    wb = jnp.einsum("qhcdn,qp->hcqdpn", w1q, jnp.eye(n_q, dtype=F32))
    wb = wb.reshape(2, CMP_STRIDE, n_q * HEAD_DIM, n_q * CMP_HIDDEN).astype(BF16)
    pos = cmp_pos.reshape(2, 2, CMP_STRIDE, HEAD_DIM)
    prow = jnp.repeat(pos, C_KV_HEADS, axis=0).transpose(1, 2, 0, 3).reshape(2, CMP_STRIDE, 1, n_q * HEAD_DIM)
    prow = jnp.broadcast_to(prow, (2, CMP_STRIDE, 8, n_q * HEAD_DIM)).astype(BF16)
    eye_g = jnp.eye(C_KV_HEADS, dtype=F32)
    w2k = jnp.kron(eye_g, cmp_w2[0]).astype(BF16)
    w2vt = jnp.kron(eye_g, cmp_w2[1]).T.astype(BF16)
    return wb, prow, w2k, w2vt


def _token_mixing(x, b, s, mix_norm, w_in, qk_gain, sinks, cmp_pos, cmp_w1, cmp_w2, w_a, w_b, w_c):
    t, d = x.shape
    assert s % CHUNK16 == 0 and d % 512 == 0
    c_gate_cols = 3 * C_HEADS
    w_qkv = w_in[:, :QKV_COLS + LANES].astype(BF16)
    flag, gs = _qkv_column_params(qk_gain)
    a0, a1, a2, bsec, cq, ckv, cmpd, qt, ks, vst, cg = _qkv_proj(x, mix_norm, w_qkv, flag, gs, b, s)

    a_outs = [_dilated_group(a0.reshape(b, s, SEC), 0, b, s), _dilated_group(a1, 1, b, s),
              _dilated_group(a2, 2, b, s)]
    a_outs[0] = tuple(v.reshape(t, A_OUT) for v in a_outs[0])
    o_b = _sink_swa(bsec.reshape(b, s, SEC), sinks.astype(F32), b, s).reshape(t, -1)
    o_win = _nsa_window(cq.reshape(b, s, SEC), ckv.reshape(b, s, SEC), b, s).reshape(t, -1)

    n_chunks = s // CMP_STRIDE
    n_cmp = (s - CMP_BLOCK) // CMP_STRIDE + 1
    n_slc = s // SLC_BLOCK
    nq = s // Q_BLOCK
    kg = jnp.tile(qk_gain[2, 1], C_KV_HEADS).reshape(1, LANES)
    kc, vct = _compress(cmpd, *_compress_weights(cmp_pos, cmp_w1, cmp_w2), kg)
    o_cmp, sel, cnt = _cmp_select(qt, kc, vct, _overlap_t(n_slc, n_chunks, n_cmp), n_cmp=n_cmp)

    act = (cnt[:, :, :, 0, :] > 0).reshape(b, C_KV_HEADS, nq, nq, 2).any(axis=-1)
    act = act & (jnp.arange(nq)[None, :] < jnp.arange(nq)[:, None])
    order = jnp.argsort(jnp.logical_not(act), axis=-1, stable=True).astype(jnp.int32)
    count = jnp.sum(act, axis=-1, dtype=jnp.int32)[..., None]
    lists = jnp.concatenate([count, order] + [jnp.zeros_like(count)] * (SLC_GROUP - 1), axis=-1)
    o_slc = _slc_attention(lists.reshape(-1), qt, ks.reshape(b, s, 2 * LANES),
                           vst.reshape(b, nq, C_KV_HEADS, HEAD_DIM, Q_BLOCK), sel, stride=nq + SLC_GROUP)

    return _merge(x, mix_norm, a_outs, o_b, o_cmp, o_slc, o_win, cg,
                  w_in[:, QKV_COLS + c_gate_cols:].astype(BF16), w_a.astype(BF16), w_b.astype(BF16),
                  w_c.astype(BF16), _gate_expand())


def kernel(x, ffn1_norm, ffn1_w_gu, ffn1_w_down, mix_norm, w_in, qk_gain, sinks, cmp_pos, cmp_w1, cmp_w2,
           w_branch_a, w_branch_b, w_branch_c, w_out, ffn2_norm, ffn2_w_gu, ffn2_w_down):
    b, s, d = x.shape
    h = x.reshape(b * s, d)
    w1_gu, w1_down = ffn1_w_gu.astype(BF16), ffn1_w_down.astype(BF16)
    w2_gu, w2_down = ffn2_w_gu.astype(BF16), ffn2_w_down.astype(BF16)
    for l in range(ffn1_norm.shape[0]):
        h = _ffn(h, ffn1_norm[l], w1_gu, w1_down, l)
        merged = _token_mixing(h, b, s, mix_norm[l], w_in[l], qk_gain[l], sinks[l], cmp_pos[l], cmp_w1[l],
                               cmp_w2[l], w_branch_a[l], w_branch_b[l], w_branch_c[l])
        h = _out_proj(h, merged, w_out[l].astype(BF16))
        h = _ffn(h, ffn2_norm[l], w2_gu, w2_down, l)
    return h.reshape(b, s, d)
```

```python
import functools
import math

import numpy as np
import jax
import jax.numpy as jnp
from jax import lax
from jax.experimental import pallas as pl
from jax.experimental.pallas import tpu as pltpu

F32 = jnp.float32
BF16 = jnp.bfloat16

HEAD_DIM = 64
Q_BLOCK = 128
LANES = 128
A_GROUPS = ((128, 1), (512, 4), (2048, 16))
A_HEADS_PER_GROUP = 4
A_HEADS = 12
A_OUT = A_HEADS_PER_GROUP * HEAD_DIM
B_HEADS = 8
B_KV_HEADS = 2
B_WINDOW = 128
C_HEADS = 12
C_KV_HEADS = 2
C_REP = C_HEADS // C_KV_HEADS
CMP_BLOCK = 32
CMP_STRIDE = 16
CMP_HIDDEN = 256
SLC_BLOCK = 64
SLC_TOPK = 16
C_WINDOW = 512
RMS_EPS = 1e-6
NEG_INF = -1e30
SEC = 768
N_SEC = 6
QKV_COLS = SEC * N_SEC
TM = 512
CHUNK16 = Q_BLOCK * 16
VMEM_LIMIT = 56 * 1024 * 1024


def _slopes(n):
    return [float(2.0 ** (-8.0 * (h + 1) / n)) for h in range(n)]


def _cparams(sem):
    return pltpu.CompilerParams(dimension_semantics=sem, vmem_limit_bytes=VMEM_LIMIT)


def _dot(a, b):
    return jnp.dot(a, b, preferred_element_type=F32)


def _nt_dot(a, b):
    return lax.dot_general(a, b, (((1,), (1,)), ((), ())), preferred_element_type=F32)


def _split(v):
    hi = v.astype(BF16)
    return hi, (v - hi.astype(F32)).astype(BF16)


def _resident(shape):
    return pl.BlockSpec(shape, lambda *_: (0,) * len(shape), pipeline_mode=pl.Buffered(1))


def _rms_rows(x, g):
    ms = jnp.mean(x * x, axis=-1, keepdims=True)
    return (x * lax.rsqrt(ms + RMS_EPS) * g).astype(BF16)


def _deinterleave(n, d):
    p = np.zeros((n, n), np.float32)
    r = np.arange(n // d)
    for c in range(d):
        p[c * (n // d) + r, d * r + c] = 1.0
    return p


AUG_POS = 0
AUG_ROWS = 16
MASK_BIG = 1e30
SLC_GROUP = 7
LOG2E = math.log2(math.e)


def _key_pattern(n):
    pat = np.zeros((n, LANES), np.float32)
    pat[:, AUG_POS] = pat[:, AUG_POS + 1] = np.arange(n) % Q_BLOCK
    return pat


def _slot_pattern():
    pat = np.zeros((SLC_GROUP, Q_BLOCK, 2 * LANES), np.float32)
    r = np.arange(Q_BLOCK)
    for u in range(SLC_GROUP):
        pat[u, :, LANES + AUG_ROWS * (u + 1)] = r < SLC_BLOCK
        pat[u, :, LANES + AUG_ROWS * (u + 1) + 1] = r >= SLC_BLOCK
    return pat


def _block_spread(n_tiles):
    m = np.zeros((n_tiles * AUG_ROWS, 2 * n_tiles), np.float32)
    jj = np.arange(n_tiles)
    for e in range(2):
        m[AUG_ROWS * jj + e, 2 * jj + e] = 1.0
    return m


def _ffn_body(x_ref, g_ref, wg_ref, wu_ref, wd_ref, o_ref, h_ref, *, n_f):
    f = pl.program_id(1)

    @pl.when(f == 0)
    def _():
        h_ref[...] = _rms_rows(x_ref[...], g_ref[...])
        o_ref[...] = jnp.zeros_like(o_ref)

    h = h_ref[...]
    gate = _dot(h, wg_ref[...])
    up = _dot(h, wu_ref[...])
    act = (gate * jax.nn.sigmoid(gate) * up).astype(BF16)
    o_ref[...] += _dot(act, wd_ref[...])

    @pl.when(f == n_f - 1)
    def _():
        o_ref[...] = x_ref[...] + 0.5 * o_ref[...]


def _ffn(x, g, w_gu, w_down, layer, *, tm=1024, tf=512):
    t, d = x.shape
    d_ff = w_down.shape[1]
    n_f = d_ff // tf
    rows = pl.BlockSpec((tm, d), lambda i, f: (i, 0))
    return pl.pallas_call(
        functools.partial(_ffn_body, n_f=n_f),
        grid=(t // tm, n_f),
        in_specs=[
            rows,
            pl.BlockSpec((1, d), lambda i, f: (0, 0)),
            pl.BlockSpec((None, d, tf), lambda i, f: (layer, 0, f)),
            pl.BlockSpec((None, d, tf), lambda i, f: (layer, 0, f + n_f)),
            pl.BlockSpec((None, tf, d), lambda i, f: (layer, f, 0)),
        ],
        out_specs=rows,
        out_shape=jax.ShapeDtypeStruct((t, d), F32),
        scratch_shapes=[pltpu.VMEM((tm, d), BF16)],
        compiler_params=_cparams(("parallel", "arbitrary")),
    )(x, g.reshape(1, d), w_gu, w_gu, w_down)


NORM_TILE = 256
NORM_TILES = {0: (0, 1), 1: (0, 1), 2: (0, 1), 3: (0, 1, 2), 4: (0, 1, 2), 5: (1, 2)}


def _head_sumsq(y, bd):
    return _dot((y * y).astype(BF16), bd)


def _qkv_body(x_ref, g_ref, w_ref, flag_ref, gs_ref, bd_ref, p4_ref, p16_ref, eye_ref, kpat_ref,
              a0_ref, a1_ref, a2_ref, b_ref, cq_ref, ckv_ref, cmpd_ref, qt_ref, ks_ref, vst_ref, cg_ref):
    h = _rms_rows(x_ref[...], g_ref[...])
    bd = bd_ref[...]

    def project(k):
        if k < len(A_GROUPS):
            return jnp.concatenate([_dot(h, w_ref[:, part * A_HEADS * HEAD_DIM + k * A_OUT:
                                                     part * A_HEADS * HEAD_DIM + (k + 1) * A_OUT])
                                    for part in range(3)], axis=1)
        return _dot(h, w_ref[:, k * SEC:(k + 1) * SEC])

    def finish(k, y):
        tiles = []
        for c in range(SEC // NORM_TILE):
            yc = y[:, c * NORM_TILE:(c + 1) * NORM_TILE]
            if c in NORM_TILES[k]:
                cols = slice(k * SEC + c * NORM_TILE, k * SEC + (c + 1) * NORM_TILE)
                inv = lax.rsqrt(_head_sumsq(yc, bd) * (1.0 / HEAD_DIM) + RMS_EPS)
                yc = yc * jnp.where(flag_ref[:, cols] > 0, inv, 1.0) * gs_ref[:, cols]
            tiles.append(yc.astype(BF16))
        return jnp.concatenate(tiles, axis=1)

    sec = []
    for first in range(0, N_SEC, 3):
        raw = [project(k) for k in range(first, first + 3)]
        sec += [finish(first + k, y) for k, y in enumerate(raw)]
    cg_logits = _dot(h, w_ref[:, QKV_COLS:QKV_COLS + LANES])

    a0_ref[...] = sec[0]
    a1_ref[...] = _dot(p4_ref[...], sec[1]).astype(BF16).reshape(a1_ref.shape)
    a2_ref[...] = _dot(p16_ref[...], sec[2]).astype(BF16).reshape(a2_ref.shape)
    b_ref[...] = sec[3]
    y_cq = sec[4]
    cq_ref[...] = y_cq
    qt_ref[...] = _nt_dot(eye_ref[...], y_cq).astype(BF16)
    y_ckv = sec[5]
    ckv_ref[...] = y_ckv
    cmpd_ref[...] = _dot(p16_ref[...], y_ckv[:, 0:2 * LANES]).astype(BF16).reshape(cmpd_ref.shape)
    ks_ref[:, 0:LANES] = y_ckv[:, 2 * LANES:3 * LANES]
    ks_ref[:, LANES:2 * LANES] = kpat_ref[...]
    eye = eye_ref[0:LANES, 0:LANES]
    for kb in range(vst_ref.shape[0]):
        vt = _nt_dot(eye, y_ckv[kb * Q_BLOCK:(kb + 1) * Q_BLOCK, 3 * LANES:4 * LANES]).astype(BF16)
        for gg in range(C_KV_HEADS):
            vst_ref[kb, gg] = vt[gg * HEAD_DIM:(gg + 1) * HEAD_DIM]
    cg_ref[...] = jax.nn.sigmoid(cg_logits)


def _qkv_proj(x, g, w, flag, gs, b, s):
    t, d = x.shape
    tiles_per_batch = s // TM
    per16 = CHUNK16 // TM
    bd = jnp.asarray(np.kron(np.eye(NORM_TILE // HEAD_DIM), np.ones((HEAD_DIM, HEAD_DIM))), BF16)
    p4 = jnp.asarray(_deinterleave(TM, 4), BF16)
    p16 = jnp.asarray(_deinterleave(TM, 16), BF16)
    eye = jnp.asarray(np.eye(SEC), BF16)
    nat = pl.BlockSpec((TM, SEC), lambda i: (i, 0))
    out_specs = [
        nat,
        pl.BlockSpec((None, 4, Q_BLOCK, SEC), lambda i: (i, 0, 0, 0)),
        pl.BlockSpec((None, 16, TM // 16, SEC), lambda i: (i // per16, 0, i % per16, 0)),
        nat, nat, nat,
        pl.BlockSpec((None, 16, TM // 16, 2 * LANES), lambda i: (i // tiles_per_batch, 0, i % tiles_per_batch, 0)),
        pl.BlockSpec((None, SEC, TM), lambda i: (i // tiles_per_batch, 0, i % tiles_per_batch)),
        pl.BlockSpec((TM, 2 * LANES), lambda i: (i, 0)),
        pl.BlockSpec((TM // Q_BLOCK, C_KV_HEADS, HEAD_DIM, Q_BLOCK), lambda i: (i, 0, 0, 0)),
        pl.BlockSpec((TM, LANES), lambda i: (i, 0)),
    ]
    out_shape = [
        jax.ShapeDtypeStruct((t, SEC), BF16),
        jax.ShapeDtypeStruct((t // TM, 4, Q_BLOCK, SEC), BF16),
        jax.ShapeDtypeStruct((t // CHUNK16, 16, Q_BLOCK, SEC), BF16),
        jax.ShapeDtypeStruct((t, SEC), BF16),
        jax.ShapeDtypeStruct((t, SEC), BF16),
        jax.ShapeDtypeStruct((t, SEC), BF16),
        jax.ShapeDtypeStruct((b, 16, s // 16, 2 * LANES), BF16),
        jax.ShapeDtypeStruct((b, SEC, s), BF16),
        jax.ShapeDtypeStruct((t, 2 * LANES), BF16),
        jax.ShapeDtypeStruct((t // Q_BLOCK, C_KV_HEADS, HEAD_DIM, Q_BLOCK), BF16),
        jax.ShapeDtypeStruct((t, LANES), F32),
    ]
    n_w = w.shape[1]
    return pl.pallas_call(
        _qkv_body,
        grid=(t // TM,),
        in_specs=[
            pl.BlockSpec((TM, d), lambda i: (i, 0)),
            _resident((1, d)),
            _resident((d, n_w)),
            _resident((1, QKV_COLS)),
            _resident((1, QKV_COLS)),
            _resident((NORM_TILE, NORM_TILE)),
            _resident((TM, TM)),
            _resident((TM, TM)),
            _resident((SEC, SEC)),
            _resident((TM, LANES)),
        ],
        out_specs=out_specs,
        out_shape=out_shape,
        compiler_params=_cparams(("parallel",)),
    )(x, g.reshape(1, d), w, flag.reshape(1, -1), gs.reshape(1, -1), bd, p4, p16, eye,
      jnp.asarray(_key_pattern(TM), BF16))


def _banded_body(*refs, nb, qb, keys, heads, k_off, v_off, n_pairs, q_axis, use_sinks, with_lse, stack, fold,
                 log2_units):
    refs = list(refs)
    q_ref = refs.pop(0)
    kv_refs = [refs.pop(0) for _ in range({"self": 1, "window": 2, "blocks": nb + 1}[keys])]
    qc_ref, kaug_ref, band_ref = refs.pop(0), refs.pop(0), refs.pop(0)
    sink_ref = refs.pop(0) if use_sinks else None
    o_ref = refs.pop(0)
    lse_ref = refs.pop(0) if with_lse else None

    i = pl.program_id(q_axis)
    nk = (nb + 1) * Q_BLOCK
    col = lax.broadcasted_iota(jnp.int32, (Q_BLOCK, nk), 1)
    band = band_ref[...]

    def start_mask(sub):
        if sub > 0 and sub >= nb:
            return band
        return band + jnp.where(col < (nb - (i * qb + sub)) * Q_BLOCK, -MASK_BIG, 0.0)

    rel_f = (nb * Q_BLOCK + lax.broadcasted_iota(jnp.int32, (Q_BLOCK, nk), 0) - col).astype(F32)
    lane = lax.broadcasted_iota(jnp.int32, (Q_BLOCK, LANES), 1)
    low_half = lane < HEAD_DIM
    kaug = kaug_ref[...]

    def q_cols(sub, c0):
        if len(q_ref.shape) == 3:
            return q_ref[sub, :, c0:c0 + LANES]
        return q_ref[sub * Q_BLOCK:(sub + 1) * Q_BLOCK, c0:c0 + LANES]

    kv_cache = {}

    def swap_halves(tile, swapped):
        return pltpu.roll(tile.astype(F32), HEAD_DIM, 1).astype(BF16) if swapped else tile

    def kv_tile(sub, off, kv_pair, swapped):
        c0 = off + kv_pair * LANES
        if keys == "window":
            key = (off, kv_pair, swapped)
            if key not in kv_cache:
                kv_cache[key] = swap_halves(jnp.concatenate([r[:, c0:c0 + LANES] for r in kv_refs], axis=0), swapped)
            first = qb - nb + sub
            return kv_cache[key][first * Q_BLOCK:(first + nb + 1) * Q_BLOCK]
        key = (sub, off, kv_pair, swapped)
        if key not in kv_cache:
            if keys == "self":
                blocks = [kv_refs[0][:, c0:c0 + LANES] if sub == 0 else q_cols(sub - 1, c0), q_cols(sub, c0)]
            else:
                blocks = [r[:, c0:c0 + LANES] for r in kv_refs]
            kv_cache[key] = swap_halves(jnp.concatenate(blocks, axis=0), swapped)
        return kv_cache[key]

    classes = {}
    for head in heads:
        pair, half, kv_pair, kv_half, slope, hidx = head
        key = (kv_pair, kv_half != half) if stack else hidx
        classes.setdefault(key, []).append(head)

    outs = [[[None, None] for _ in range(n_pairs)] for _ in range(qb)]
    lses = [[[None, None] for _ in range(n_pairs)] for _ in range(qb)]
    groups = [(sub, members) for sub in range(qb) for members in classes.values()]

    scores = []
    for sub, members in groups:
        n_h = len(members)
        kv_pair, swapped = members[0][2], members[0][3] != members[0][1]
        mask = start_mask(sub)
        q_rows = []
        for pair, half, _, _, _, hidx in members:
            qp = q_cols(sub, pair * LANES)
            own = low_half if half == 0 else jnp.logical_not(low_half)
            qm = jnp.where(own, qp, jnp.zeros_like(qp))
            q_rows.append(jnp.concatenate([qm, qc_ref[hidx]], axis=1) if fold else qm)
        if fold:
            k_aug = jnp.concatenate([kv_tile(sub, k_off, kv_pair, swapped), kaug], axis=1)
            s = _nt_dot(jnp.concatenate(q_rows, axis=0), k_aug)
            s = (s.reshape(n_h, Q_BLOCK, nk) + mask[None]).reshape(n_h * Q_BLOCK, nk)
        else:
            bias = jnp.concatenate([mask - member[4] * rel_f for member in members], axis=0)
            s = _nt_dot(jnp.concatenate(q_rows, axis=0), kv_tile(sub, k_off, kv_pair, swapped)) + bias
        scores.append(s)

    probs = []
    for (_, members), s in zip(groups, scores):
        m = jnp.max(s, axis=1, keepdims=True)
        if use_sinks:
            assert len(members) == 1
            sink = sink_ref[members[0][5]]
            m = jnp.maximum(m, sink)
        p = jnp.exp2(s - m) if log2_units else jnp.exp(s - m)
        den = jnp.sum(p, axis=1, keepdims=True)
        if use_sinks:
            den = den + jnp.exp(sink - m)
        probs.append((p.astype(BF16), m, den))

    for (sub, members), (p, m, den) in zip(groups, probs):
        kv_pair, swapped = members[0][2], members[0][3] != members[0][1]
        r = _dot(p, kv_tile(sub, v_off, kv_pair, swapped)) / den
        lse = m + jnp.log(den) if with_lse else None
        for k, (pair, half, _, _, _, _) in enumerate(members):
            outs[sub][pair][half] = r[k * Q_BLOCK:(k + 1) * Q_BLOCK]
            if with_lse:
                lses[sub][pair][half] = jnp.broadcast_to(lse[k * Q_BLOCK:(k + 1) * Q_BLOCK], (Q_BLOCK, LANES))

    def store(ref, sub, sl, value):
        if len(ref.shape) == 3:
            ref[sub, :, sl] = value
        else:
            ref[sub * Q_BLOCK:(sub + 1) * Q_BLOCK, sl] = value

    for sub in range(qb):
        for pair in range(n_pairs):
            sl = slice(pair * LANES, (pair + 1) * LANES)
            store(o_ref, sub, sl, jnp.where(low_half, outs[sub][pair][0], outs[sub][pair][1]))
            if with_lse:
                store(lse_ref, sub, sl, jnp.where(low_half, lses[sub][pair][0], lses[sub][pair][1]))


def _banded_consts(heads, nb, max_dist):
    nk = (nb + 1) * Q_BLOCK
    slope = np.asarray([h[4] for h in heads], np.float32)[:, None]
    q_dist = (nb * Q_BLOCK + np.arange(Q_BLOCK, dtype=np.float32))[None, :]
    ones = np.ones_like(q_dist)
    vals = jnp.asarray(np.stack([slope * ones, slope * Q_BLOCK * ones, -slope * q_dist], axis=-1))
    hi = vals.astype(BF16)
    lo = (vals - hi.astype(F32)).astype(BF16)
    cols = jnp.stack([hi[..., 0], lo[..., 0], hi[..., 1], lo[..., 1], hi[..., 2], lo[..., 2]], axis=-1)
    qc = jnp.pad(cols, ((0, 0), (0, 0), (0, LANES - cols.shape[-1])))
    kaug = np.zeros((nk, LANES), np.float32)
    kaug[:, 0] = kaug[:, 1] = np.arange(nk) % Q_BLOCK
    kaug[:, 2] = kaug[:, 3] = np.arange(nk) // Q_BLOCK
    kaug[:, 4] = kaug[:, 5] = 1.0
    rel = nb * Q_BLOCK + np.arange(Q_BLOCK)[:, None] - np.arange(nk)[None, :]
    band = np.where((rel >= 0) & (rel <= max_dist), 0.0, -MASK_BIG).astype(np.float32)
    return [qc, jnp.asarray(kaug, BF16), jnp.asarray(band)]


BAND_QB = 4
CMP_QB = 1


def _banded_call(q_arr, kv_arr, *, grid, q_spec, kv_specs, out_spec, out_shape, out_cols, nb, qb, max_dist, heads,
                 k_off, v_off, q_axis, sinks=None, with_lse=False, stack=False, fold=False, log2_units=False):
    assert not (log2_units and (with_lse or sinks is not None))
    if len(kv_specs) == 1 and nb == 1:
        keys = "self"
    elif qb > 1:
        keys = "window"
        assert len(kv_specs) == 2 and nb <= qb
    else:
        keys = "blocks"
        assert len(kv_specs) == nb + 1
    consts = _banded_consts(heads, nb, max_dist)
    in_specs = [q_spec] + list(kv_specs) + [pl.BlockSpec(c.shape, lambda *_, nd=c.ndim: (0,) * nd) for c in consts]
    args = [q_arr] + [kv_arr] * len(kv_specs) + consts
    if sinks is not None:
        in_specs.append(pl.BlockSpec(memory_space=pltpu.SMEM))
        args.append(sinks)
    oshape = jax.ShapeDtypeStruct(out_shape, F32)
    body = functools.partial(_banded_body, nb=nb, qb=qb, keys=keys, heads=heads, k_off=k_off, v_off=v_off,
                             n_pairs=out_cols // LANES, q_axis=q_axis, use_sinks=sinks is not None,
                             with_lse=with_lse, stack=stack, fold=fold, log2_units=log2_units)
    return pl.pallas_call(
        body,
        grid=grid,
        in_specs=in_specs,
        out_specs=[out_spec, out_spec] if with_lse else out_spec,
        out_shape=[oshape, oshape] if with_lse else oshape,
        compiler_params=_cparams(("parallel",) * len(grid)),
    )(*args)


def _row_specs(b, s, qb, out_cols):
    q_spec = pl.BlockSpec((None, qb * Q_BLOCK, SEC), lambda bb, i: (bb, i, 0))
    prev = pl.BlockSpec((None, Q_BLOCK, SEC), lambda bb, i: (bb, jnp.maximum(qb * i - 1, 0), 0))
    out_spec = pl.BlockSpec((None, qb * Q_BLOCK, out_cols), lambda bb, i: (bb, i, 0))
    return dict(grid=(b, s // (qb * Q_BLOCK)), q_axis=1, q_spec=q_spec, kv_specs=[prev], out_spec=out_spec,
                out_shape=(b, s, out_cols), qb=qb)


def _dilated_group(arr, gi, b, s):
    window, dil = A_GROUPS[gi]
    slopes = _slopes(A_HEADS)
    heads = tuple((hh // 2, hh % 2, hh // 2, hh % 2, slopes[gi * A_HEADS_PER_GROUP + hh] * dil, hh)
                  for hh in range(A_HEADS_PER_GROUP))
    common = dict(out_cols=A_OUT, nb=1, max_dist=window // dil, heads=heads, k_off=256, v_off=512,
                  with_lse=True, stack=False)
    if dil == 1:
        return _banded_call(arr, arr, **_row_specs(b, s, BAND_QB, A_OUT), **common)
    nc = s // (Q_BLOCK * dil)
    qb = BAND_QB if nc % BAND_QB == 0 else 1
    q_spec = pl.BlockSpec((qb, None, Q_BLOCK, SEC), lambda bb, c, i: ((bb * nc) // qb + i, c, 0, 0))
    prev = pl.BlockSpec((None, None, Q_BLOCK, SEC), lambda bb, c, i: (bb * nc + jnp.maximum(qb * i - 1, 0), c, 0, 0))
    out_spec = pl.BlockSpec((qb, None, Q_BLOCK, A_OUT), lambda bb, c, i: ((bb * nc) // qb + i, c, 0, 0))
    return _banded_call(arr, arr, grid=(b, dil, nc // qb), q_axis=2, q_spec=q_spec, kv_specs=[prev],
                        out_spec=out_spec, out_shape=(b * nc, dil, Q_BLOCK, A_OUT), qb=qb, **common)


def _sink_swa(arr, sinks, b, s):
    slopes = _slopes(B_HEADS)
    rep = B_HEADS // B_KV_HEADS
    heads = tuple((h // 2, h % 2, 0, h // rep, slopes[h], h) for h in range(B_HEADS))
    return _banded_call(arr, arr, **_row_specs(b, s, BAND_QB, B_HEADS * HEAD_DIM), out_cols=B_HEADS * HEAD_DIM,
                        nb=1, max_dist=B_WINDOW - 1, heads=heads, k_off=512, v_off=640, sinks=sinks)


def _nsa_window(cq, ckv, b, s):
    slopes = _slopes(C_HEADS)
    heads = tuple((h // 2, h % 2, 0, h // C_REP, slopes[h] * LOG2E, h) for h in range(C_HEADS))
    nb = -(-(C_WINDOW - 1) // Q_BLOCK)
    qb = nb
    blk = (None, qb * Q_BLOCK, SEC)
    kv_specs = [pl.BlockSpec(blk, lambda bb, i: (bb, jnp.maximum(i - 1, 0), 0)),
                pl.BlockSpec(blk, lambda bb, i: (bb, i, 0))]
    out_cols = C_HEADS * HEAD_DIM
    return _banded_call(
        cq, ckv, grid=(b, s // (qb * Q_BLOCK)), q_axis=1, qb=qb, log2_units=True,
        q_spec=pl.BlockSpec(blk, lambda bb, i: (bb, i, 0)), kv_specs=kv_specs,
        out_spec=pl.BlockSpec((None, qb * Q_BLOCK, out_cols), lambda bb, i: (bb, i, 0)),
        out_shape=(b, s, out_cols), out_cols=out_cols, nb=nb, max_dist=C_WINDOW - 1, heads=heads,
        k_off=512, v_off=640, stack=True, fold=True)


def _compress_body(t_ref, wb_ref, prow_ref, w2k_ref, w2vt_ref, kg_ref, bd_ref, kc_ref, vct_ref, *, n_chunks):
    hid_cols = 2 * C_KV_HEADS * CMP_HIDDEN
    u = jnp.zeros((n_chunks, hid_cols), F32)
    v = jnp.zeros((n_chunks, hid_cols), F32)
    pc = jnp.zeros((1, hid_cols), F32)
    for c in range(CMP_STRIDE):
        tc = t_ref[c]
        u = u + _dot(tc, wb_ref[0, c])
        v = v + _dot(tc, wb_ref[1, c])
        pc = pc + _dot(prow_ref[0, c], wb_ref[0, c])[0:1] + _dot(prow_ref[1, c], wb_ref[1, c])[0:1]
    hsum = u + pltpu.roll(v, n_chunks - 1, 0) + pc
    hid = (hsum * jax.nn.sigmoid(hsum)).astype(BF16)
    half = C_KV_HEADS * CMP_HIDDEN
    k = _dot(hid[:, :half], w2k_ref[...])
    hi, lo = _split(k * k)
    ss = _dot(hi, bd_ref[...]) + _dot(lo, bd_ref[...])
    kc_ref[...] = (k * lax.rsqrt(ss * (1.0 / HEAD_DIM) + RMS_EPS) * kg_ref[...]).astype(BF16)
    vct_ref[...] = _nt_dot(w2vt_ref[...], hid[:, half:]).astype(BF16)


def _compress(cmpd, wb, prow, w2k, w2vt, kg):
    b, _, n_chunks, width = cmpd.shape
    bd = jnp.asarray(np.kron(np.eye(LANES // HEAD_DIM), np.ones((HEAD_DIM, HEAD_DIM))), BF16)
    return pl.pallas_call(
        functools.partial(_compress_body, n_chunks=n_chunks),
        grid=(b,),
        in_specs=[
            pl.BlockSpec((None, CMP_STRIDE, n_chunks, width), lambda bb: (bb, 0, 0, 0)),
            _resident(wb.shape), _resident(prow.shape), _resident(w2k.shape), _resident(w2vt.shape),
            _resident((1, LANES)), _resident((LANES, LANES)),
        ],
        out_specs=[
            pl.BlockSpec((None, n_chunks, LANES), lambda bb: (bb, 0, 0)),
            pl.BlockSpec((None, LANES, n_chunks), lambda bb: (bb, 0, 0)),
        ],
        out_shape=[
            jax.ShapeDtypeStruct((b, n_chunks, LANES), BF16),
            jax.ShapeDtypeStruct((b, LANES, n_chunks), BF16),
        ],
        compiler_params=_cparams(("parallel",)),
    )(cmpd, wb, prow, w2k, w2vt, kg, bd)


def _to_natural(ot_list, eye):
    pairs = []
    for k in range(0, len(ot_list), 2):
        hi, lo = _split(jnp.concatenate([ot_list[k], ot_list[k + 1]], axis=0))
        pairs.append(_nt_dot(eye, hi) + _nt_dot(eye, lo))
    return jnp.concatenate(pairs, axis=1)


def _cmp_body(qt_ref, kc_ref, vct_ref, ovt_ref, eye_ref, spread_ref, cmask_ref,
              o_ref, selm_ref, cnt_ref, sel_sc, *, n_top):
    g = pl.program_id(1)
    i = pl.program_id(2)
    n_pad = kc_ref.shape[0]
    n_slc = ovt_ref.shape[0]
    qw = qt_ref.shape[1]
    qb = qw // Q_BLOCK
    per_q = Q_BLOCK // CMP_STRIDE
    own_rows = (lax.broadcasted_iota(jnp.int32, (LANES, qw), 0) // HEAD_DIM) == g
    slopes = [sl * LOG2E for sl in _slopes(C_HEADS)]

    def attend(rows):
        kc = kc_ref[0:rows, :]
        vct = vct_ref[:, 0:rows]
        mask = jnp.concatenate(
            [cmask_ref[pl.ds(pl.multiple_of(n_pad - per_q * (qb * i + h), 8), rows), :] for h in range(qb)], axis=1)
        n_f = (CMP_STRIDE * lax.broadcasted_iota(jnp.int32, (rows, qw), 0)).astype(F32)
        psum = jnp.zeros((rows, qw), F32)
        outs = []
        for r in range(C_REP):
            slope = jnp.where(g == 0, slopes[r], slopes[C_REP + r])
            qt = qt_ref[r * HEAD_DIM:(r + 1) * HEAD_DIM, :]
            q_pad = jnp.where(own_rows, jnp.concatenate([qt, qt], axis=0), jnp.zeros((LANES, qw), BF16))
            s = _dot(kc, q_pad) + (slope * n_f + mask)
            m = jnp.maximum(jnp.max(s, axis=0, keepdims=True), -1e20)
            e = jnp.exp2(s - m)
            den = jnp.sum(e, axis=0, keepdims=True)
            p = e * (1.0 / jnp.where(den > 0, den, 1.0))
            psum = psum + p
            both = _dot(vct, p.astype(BF16))
            outs.append(jnp.where(g == 0, both[:HEAD_DIM], both[HEAD_DIM:]))
        o_ref[...] = _to_natural(outs, eye_ref[...])

        n_j = rows * CMP_STRIDE // SLC_BLOCK
        hi, lo = _split(psum)
        ovt = ovt_ref[0:n_j, 0:rows]
        imp = _dot(ovt, hi) + _dot(ovt, lo)
        j_idx = lax.broadcasted_iota(jnp.int32, (n_j, qw), 0)
        t_q = i * qw + lax.broadcasted_iota(jnp.int32, (n_j, qw), 1)
        cur = lax.shift_right_logical(t_q, int(math.log2(SLC_BLOCK)))
        forced = ((j_idx == 0) | (j_idx == cur) | (j_idx == cur - 1)) & (j_idx <= cur)
        v = jnp.where((j_idx <= cur) & jnp.logical_not(forced), imp, -1.0)
        sel = jnp.where(forced, 1.0, 0.0)
        for _ in range(n_top - 3):
            m = jnp.max(v, axis=0, keepdims=True)
            first = jnp.min(jnp.where((v == m) & (m >= 0.0), j_idx, n_slc), axis=0, keepdims=True)
            pick = j_idx == first
            sel = jnp.where(pick, 1.0, sel)
            v = jnp.where(pick, -1.0, v)
        sel_sc[0:n_j, :] = sel
        if n_j < n_slc:
            sel_sc[n_j:, :] = jnp.zeros((n_slc - n_j, qw), F32)

    n_var = n_pad // LANES
    for var in range(n_var):
        pl.when((qb * i + qb - 1) // (LANES // per_q) == var)(functools.partial(attend, (var + 1) * LANES))

    sel = sel_sc[...]
    neg = jnp.where(sel > 0, 0.0, -MASK_BIG).astype(BF16)
    mask_rows = _dot(spread_ref[...], neg).astype(BF16)
    sel_b = sel.astype(BF16)
    for h in range(qb):
        lanes = slice(h * Q_BLOCK, (h + 1) * Q_BLOCK)
        selm_ref[h] = mask_rows[:, lanes]
        cnt_ref[h] = _nt_dot(jnp.ones((8, Q_BLOCK), BF16), sel_b[:, lanes])


def _cmp_select(qt, kc, vct, ovt, *, n_cmp):
    b, _, s = qt.shape
    g = C_KV_HEADS
    n_pad = kc.shape[1]
    n_slc = ovt.shape[0]
    nq = s // Q_BLOCK
    rows = C_REP * HEAD_DIM
    qb = CMP_QB if nq % CMP_QB == 0 else 1
    qw = qb * Q_BLOCK
    steps = nq // qb
    eye = jnp.asarray(np.eye(qw), BF16)
    n_top = min(SLC_TOPK, n_slc)
    assert n_top > 3 and n_cmp == n_pad - 1 and n_pad % LANES == 0
    n_rel = np.arange(-n_pad, n_pad)[:, None]
    cmask = np.where(CMP_STRIDE * n_rel + CMP_BLOCK - 1 <= np.arange(Q_BLOCK)[None, :], 0.0, -MASK_BIG)
    return pl.pallas_call(
        functools.partial(_cmp_body, n_top=n_top),
        grid=(b, g, steps),
        in_specs=[
            pl.BlockSpec((None, rows, qw), lambda bb, gg, i: (bb, gg, i)),
            pl.BlockSpec((None, n_pad, LANES), lambda bb, gg, i: (bb, 0, 0)),
            pl.BlockSpec((None, LANES, n_pad), lambda bb, gg, i: (bb, 0, 0)),
            pl.BlockSpec((n_slc, n_pad), lambda bb, gg, i: (0, 0)),
            pl.BlockSpec((qw, qw), lambda bb, gg, i: (0, 0)),
            pl.BlockSpec((nq * AUG_ROWS, n_slc), lambda bb, gg, i: (0, 0)),
            pl.BlockSpec((2 * n_pad, Q_BLOCK), lambda bb, gg, i: (0, 0)),
        ],
        out_specs=[
            pl.BlockSpec((qw, rows), lambda bb, gg, i: (bb * steps + i, gg)),
            pl.BlockSpec((None, None, qb, nq * AUG_ROWS, Q_BLOCK), lambda bb, gg, i: (bb, gg, i, 0, 0)),
            pl.BlockSpec((None, None, qb, 8, n_slc), lambda bb, gg, i: (bb, gg, i, 0, 0)),
        ],
        out_shape=[
            jax.ShapeDtypeStruct((b * s, g * rows), F32),
            jax.ShapeDtypeStruct((b, g, nq, nq * AUG_ROWS, Q_BLOCK), BF16),
            jax.ShapeDtypeStruct((b, g, nq, 8, n_slc), F32),
        ],
        scratch_shapes=[pltpu.VMEM((n_slc, qw), F32)],
        compiler_params=_cparams(("parallel", "parallel", "parallel")),
    )(qt, kc, vct, ovt, eye, jnp.asarray(_block_spread(nq), BF16), jnp.asarray(cmask, F32))


def _slc_body(list_ref, qt_ref, ks_ref, vst_ref, selm_ref, eye_ref, slot_ref, o_ref, qaug, m_sc, l_sc, acc_sc,
              *, nq, stride):
    bb = pl.program_id(0)
    g = pl.program_id(1)
    i = pl.program_id(2)
    width = C_REP * Q_BLOCK
    slopes = [sl * LOG2E for sl in _slopes(C_HEADS)]
    slope_s = [jnp.where(g == 0, slopes[r], slopes[C_REP + r]) for r in range(C_REP)]

    own_rows = (lax.broadcasted_iota(jnp.int32, (LANES, width), 0) // HEAD_DIM) == g
    q6 = jnp.concatenate([qt_ref[r * HEAD_DIM:(r + 1) * HEAD_DIM, :] for r in range(C_REP)], axis=1)
    qaug[0:LANES, :] = jnp.where(own_rows, jnp.concatenate([q6, q6], axis=0), jnp.zeros((LANES, width), BF16))
    head = lax.broadcasted_iota(jnp.int32, (LANES, width), 1) // Q_BLOCK
    row = lax.broadcasted_iota(jnp.int32, (LANES, width), 0)
    slope_t = jnp.zeros((LANES, width), F32)
    for r in range(C_REP):
        slope_t = jnp.where(head == r, slope_s[r], slope_t)
    s_hi, s_lo = _split(slope_t)
    slope_rows = jnp.where(row == AUG_POS, s_hi.astype(F32), jnp.where(row == AUG_POS + 1, s_lo.astype(F32), 0.0))
    qaug[LANES:, :] = slope_rows.astype(BF16)

    q_loc = lax.broadcasted_iota(jnp.int32, (Q_BLOCK, Q_BLOCK), 1)
    k_loc = lax.broadcasted_iota(jnp.int32, (Q_BLOCK, Q_BLOCK), 0)
    causal = jnp.where(k_loc > q_loc, -MASK_BIG, 0.0)

    m_sc[...] = jnp.full(m_sc.shape, NEG_INF, F32)
    l_sc[...] = jnp.zeros(l_sc.shape, F32)
    acc_sc[...] = jnp.zeros(acc_sc.shape, F32)

    def accumulate(tiles, own_first):
        keys = []
        for u, (jj, _) in enumerate(tiles):
            keys.append(ks_ref[pl.ds(pl.multiple_of(jj * Q_BLOCK, Q_BLOCK), Q_BLOCK), :] + slot_ref[u])
            rows = selm_ref[pl.ds(pl.multiple_of(jj * AUG_ROWS, AUG_ROWS), AUG_ROWS), :]
            lo = LANES + AUG_ROWS * (u + 1)
            qaug[lo:lo + AUG_ROWS, :] = jnp.concatenate([rows] * C_REP, axis=1)
        st = _dot(jnp.concatenate(keys, axis=0), qaug[...])
        ps, alphas = [], []
        for r in range(C_REP):
            sl = slice(r * Q_BLOCK, (r + 1) * Q_BLOCK)
            m_old = m_sc[:, sl]
            m_new = m_old
            parts = []
            for u, (jj, extra) in enumerate(tiles):
                s = st[u * Q_BLOCK:(u + 1) * Q_BLOCK, sl]
                if own_first and u == 0:
                    s = s + causal
                c = slope_s[r] * ((jj - i) * Q_BLOCK).astype(F32) + extra
                m_new = jnp.maximum(m_new, jnp.max(s, axis=0, keepdims=True) + c)
                parts.append((s, c))
            alpha = jnp.exp2(m_old - m_new)
            l_new = alpha * l_sc[:, sl]
            p_rows = []
            for s, c in parts:
                p = jnp.exp2(s + (c - m_new))
                l_new = l_new + jnp.sum(p, axis=0, keepdims=True)
                p_rows.append(p.astype(BF16))
            l_sc[:, sl] = l_new
            m_sc[:, sl] = m_new
            ps.append(jnp.concatenate(p_rows, axis=0))
            alphas.append(alpha)
        values = jnp.concatenate([vst_ref[jj] for jj, _ in tiles], axis=1)
        pv = _dot(values, jnp.concatenate(ps, axis=1))
        acc_sc[...] = jnp.concatenate(alphas, axis=1) * acc_sc[...] + pv

    base = ((bb * pl.num_programs(1) + g) * nq + i) * stride
    count = list_ref[base]

    def listed(slot):
        return list_ref[base + 1 + slot], jnp.where(slot < count, 0.0, -MASK_BIG)

    accumulate([(i, 0.0)] + [listed(u) for u in range(SLC_GROUP - 1)], True)

    def step(k, carry):
        accumulate([listed(SLC_GROUP - 1 + SLC_GROUP * k + u) for u in range(SLC_GROUP)], False)
        return carry

    rest = jnp.maximum(count - (SLC_GROUP - 1), 0)
    lax.fori_loop(0, (rest + SLC_GROUP - 1) // SLC_GROUP, step, 0)
    l = l_sc[...]
    o = acc_sc[...] / jnp.where(l > 0, l, 1.0)
    o_ref[...] = _to_natural([o[:, r * Q_BLOCK:(r + 1) * Q_BLOCK] for r in range(C_REP)], eye_ref[...])


def _slc_attention(lists, qt, ks, vst, sel, *, stride):
    b, _, s = qt.shape
    g = C_KV_HEADS
    nq = s // Q_BLOCK
    rows = C_REP * HEAD_DIM
    width = C_REP * Q_BLOCK
    eye = jnp.asarray(np.eye(LANES), BF16)
    assert AUG_ROWS * (SLC_GROUP + 1) <= LANES
    slots = jnp.asarray(_slot_pattern(), BF16)
    grid_spec = pltpu.PrefetchScalarGridSpec(
        num_scalar_prefetch=1,
        grid=(b, g, nq),
        in_specs=[
            pl.BlockSpec((None, rows, Q_BLOCK), lambda bb, gg, i, bits: (bb, gg, i)),
            pl.BlockSpec((None, s, 2 * LANES), lambda bb, gg, i, bits: (bb, 0, 0)),
            pl.BlockSpec((None, nq, None, HEAD_DIM, Q_BLOCK), lambda bb, gg, i, bits: (bb, 0, gg, 0, 0)),
            pl.BlockSpec((None, None, None, nq * AUG_ROWS, Q_BLOCK), lambda bb, gg, i, bits: (bb, gg, i, 0, 0)),
            pl.BlockSpec((LANES, LANES), lambda bb, gg, i, bits: (0, 0)),
            pl.BlockSpec(slots.shape, lambda bb, gg, i, bits: (0, 0, 0)),
        ],
        out_specs=pl.BlockSpec((Q_BLOCK, rows), lambda bb, gg, i, bits: (bb * nq + i, gg)),
        scratch_shapes=[
            pltpu.VMEM((2 * LANES, width), BF16),
            pltpu.VMEM((1, width), F32),
            pltpu.VMEM((1, width), F32),
            pltpu.VMEM((HEAD_DIM, width), F32),
        ],
    )
    return pl.pallas_call(
        functools.partial(_slc_body, nq=nq, stride=stride),
        grid_spec=grid_spec,
        out_shape=jax.ShapeDtypeStruct((b * s, g * rows), F32),
        compiler_params=_cparams(("parallel", "parallel", "parallel")),
    )(lists, qt, ks, vst, sel, eye, slots)


def _merge_body(x_ref, g_ref, oa0, la0, oa1, la1, oa2, la2, ob_ref, ocmp_ref, oslc_ref, owin_ref, cg_ref,
                p4t_ref, p16t_ref, ex_ref, wg0_ref, wg1_ref, wg2_ref, wa_ref, wb_ref, wc_ref,
                out_ref, h_ref, oall_ref):
    @pl.when(pl.program_id(1) == 0)
    def _():
        def natural(ref, pt_ref):
            hi, lo = _split(ref[...].reshape(TM, A_OUT))
            return _dot(pt_ref[...], hi) + _dot(pt_ref[...], lo)

        o1, l1 = natural(oa1, p4t_ref), natural(la1, p4t_ref)
        o2, l2 = natural(oa2, p16t_ref), natural(la2, p16t_ref)
        cg_split = jnp.concatenate(_split(cg_ref[...]), axis=1)
        gate_c = [_dot(cg_split, ex_ref[w]) for w in range(3)]

        h_ref[...] = _rms_rows(x_ref[...], g_ref[...])
        o0, l0 = oa0[...], la0[...]
        mx = jnp.maximum(jnp.maximum(l0, l1), l2)
        e0, e1, e2 = jnp.exp(l0 - mx), jnp.exp(l1 - mx), jnp.exp(l2 - mx)
        oall_ref[:, 0:A_OUT] = ((e0 * o0 + e1 * o1 + e2 * o2) / (e0 + e1 + e2)).astype(BF16)
        oall_ref[:, A_OUT:A_OUT + B_HEADS * HEAD_DIM] = ob_ref[...].astype(BF16)
        o_c = gate_c[0] * ocmp_ref[...] + gate_c[1] * oslc_ref[...] + gate_c[2] * owin_ref[...]
        oall_ref[:, A_OUT + B_HEADS * HEAD_DIM:] = o_c.astype(BF16)

    h = h_ref[...]
    c0, c1 = A_OUT, A_OUT + B_HEADS * HEAD_DIM
    merged = jax.nn.sigmoid(_dot(h, wg0_ref[...])) * _dot(oall_ref[:, 0:c0], wa_ref[...])
    merged += jax.nn.sigmoid(_dot(h, wg1_ref[...])) * _dot(oall_ref[:, c0:c1], wb_ref[...])
    merged += jax.nn.sigmoid(_dot(h, wg2_ref[...])) * _dot(oall_ref[:, c1:], wc_ref[...])
    out_ref[...] = merged.astype(BF16)


def _merge(x, g, a_outs, ob, ocmp, oslc, owin, cg, w_gate, wa, wb, wc, ex, *, tn=512):
    t, d = x.shape
    per16 = CHUNK16 // TM
    n_t = d // tn
    (oa0, la0), (oa1, la1), (oa2, la2) = a_outs

    last = t // TM - 1

    def tile(i, n, ahead):
        return jnp.minimum(i + jnp.where(n >= ahead, 1, 0), last)

    def rows(a, ahead):
        return pl.BlockSpec((TM, a.shape[1]), lambda i, n: (tile(i, n, ahead), 0))

    a1_spec = pl.BlockSpec((None, 4, Q_BLOCK, A_OUT), lambda i, n: (tile(i, n, 1), 0, 0, 0))
    a2_spec = pl.BlockSpec((None, 16, TM // 16, A_OUT),
                           lambda i, n: (tile(i, n, 1) // per16, 0, tile(i, n, 1) % per16, 0))
    p4t = jnp.asarray(_deinterleave(TM, 4).T, BF16)
    p16t = jnp.asarray(_deinterleave(TM, 16).T, BF16)
    in_specs = [
        rows(x, 1), _resident((1, d)),
        rows(oa0, 1), rows(la0, 1), a1_spec, a1_spec, a2_spec, a2_spec,
        rows(ob, 3), rows(ocmp, 2), rows(oslc, 2), rows(owin, 2), rows(cg, 3),
        _resident((TM, TM)), _resident((TM, TM)), _resident(ex.shape),
        pl.BlockSpec((d, tn), lambda i, n: (0, n)),
        pl.BlockSpec((d, tn), lambda i, n: (0, n + n_t)),
        pl.BlockSpec((d, tn), lambda i, n: (0, n + 2 * n_t)),
        pl.BlockSpec((wa.shape[0], tn), lambda i, n: (0, n)),
        pl.BlockSpec((wb.shape[0], tn), lambda i, n: (0, n)),
        pl.BlockSpec((wc.shape[0], tn), lambda i, n: (0, n)),
    ]
    return pl.pallas_call(
        _merge_body,
        grid=(t // TM, n_t),
        in_specs=in_specs,
        out_specs=pl.BlockSpec((TM, tn), lambda i, n: (i, n)),
        out_shape=jax.ShapeDtypeStruct((t, d), BF16),
        scratch_shapes=[pltpu.VMEM((TM, d), BF16), pltpu.VMEM((TM, wa.shape[0] + wb.shape[0] + wc.shape[0]), BF16)],
        compiler_params=_cparams(("parallel", "arbitrary")),
    )(x, g.reshape(1, d), oa0, la0, oa1, la1, oa2, la2, ob, ocmp, oslc, owin, cg, p4t, p16t, ex,
      w_gate, w_gate, w_gate, wa, wb, wc)


def _out_body(x_ref, m_ref, w_ref, o_ref):
    o_ref[...] = x_ref[...] + _dot(m_ref[...], w_ref[...])


def _out_proj(x, merged, w_out):
    t, d = x.shape
    rows = pl.BlockSpec((TM, d), lambda i: (i, 0))
    return pl.pallas_call(
        _out_body,
        grid=(t // TM,),
        in_specs=[rows, rows, _resident((d, d))],
        out_specs=rows,
        out_shape=jax.ShapeDtypeStruct((t, d), F32),
        compiler_params=_cparams(("parallel",)),
    )(x, merged, w_out)


def _qkv_column_params(qk_gain):
    flag, gain, scale = [], [], []
    one = jnp.ones((HEAD_DIM,), F32)

    def add(n_heads, normed, is_q, gvec, units=1.0):
        for _ in range(n_heads):
            flag.append(np.full((HEAD_DIM,), 1.0 if normed else 0.0, np.float32))
            gain.append(gvec if normed else one)
            scale.append(np.full((HEAD_DIM,), units * HEAD_DIM ** -0.5 if is_q else 1.0, np.float32))

    for _ in range(len(A_GROUPS)):
        add(A_HEADS_PER_GROUP, True, True, qk_gain[0, 0])
        add(A_HEADS_PER_GROUP, True, False, qk_gain[0, 1])
        add(A_HEADS_PER_GROUP, False, False, one)
    add(B_HEADS, True, True, qk_gain[1, 0])
    add(B_KV_HEADS, True, False, qk_gain[1, 1])
    add(B_KV_HEADS, False, False, one)
    add(C_HEADS, True, True, qk_gain[2, 0], units=LOG2E)
    for normed in (False, False, True, False, True, False):
        add(C_KV_HEADS, normed, False, qk_gain[2, 1])
    flag = np.concatenate(flag)
    assert flag.shape[0] == QKV_COLS
    return jnp.asarray(flag), jnp.concatenate(gain) * jnp.asarray(np.concatenate(scale))


def _overlap_t(n_slc, n_pad, n_cmp):
    n = np.arange(n_pad)[None, :]
    j = np.arange(n_slc)[:, None]
    start, end = CMP_STRIDE * n, CMP_STRIDE * n + CMP_BLOCK - 1
    ov = (start <= SLC_BLOCK * j + SLC_BLOCK - 1) & (end >= SLC_BLOCK * j) & (n < n_cmp)
    return jnp.asarray(ov, BF16)


def _gate_expand():
    ex = np.zeros((3, LANES, C_HEADS * HEAD_DIM), np.float32)
    for w in range(3):
        for h in range(C_HEADS):
            ex[w, h * 3 + w, h * HEAD_DIM:(h + 1) * HEAD_DIM] = 1.0
    return jnp.asarray(np.concatenate([ex, ex], axis=1), BF16)


def _compress_weights(cmp_pos, cmp_w1, cmp_w2):
    n_q = 2 * C_KV_HEADS
    w1 = cmp_w1.reshape(2, 2, CMP_STRIDE, HEAD_DIM, CMP_HIDDEN)
    w1q = jnp.repeat(w1, C_KV_HEADS, axis=0)
    wb = jnp.einsum("qhcdn,qp->hcqdpn", w1q, jnp.eye(n_q, dtype=F32))
    wb = wb.reshape(2, CMP_STRIDE, n_q * HEAD_DIM, n_q * CMP_HIDDEN).astype(BF16)
    pos = cmp_pos.reshape(2, 2, CMP_STRIDE, HEAD_DIM)
    prow = jnp.repeat(pos, C_KV_HEADS, axis=0).transpose(1, 2, 0, 3).reshape(2, CMP_STRIDE, 1, n_q * HEAD_DIM)
    prow = jnp.broadcast_to(prow, (2, CMP_STRIDE, 8, n_q * HEAD_DIM)).astype(BF16)
    eye_g = jnp.eye(C_KV_HEADS, dtype=F32)
    w2k = jnp.kron(eye_g, cmp_w2[0]).astype(BF16)
    w2vt = jnp.kron(eye_g, cmp_w2[1]).T.astype(BF16)
    return wb, prow, w2k, w2vt


def _token_mixing(x, b, s, mix_norm, w_in, qk_gain, sinks, cmp_pos, cmp_w1, cmp_w2, w_a, w_b, w_c):
    t, d = x.shape
    assert s % CHUNK16 == 0 and d % 512 == 0
    c_gate_cols = 3 * C_HEADS
    w_qkv = w_in[:, :QKV_COLS + LANES].astype(BF16)
    flag, gs = _qkv_column_params(qk_gain)
    a0, a1, a2, bsec, cq, ckv, cmpd, qt, ks, vst, cg = _qkv_proj(x, mix_norm, w_qkv, flag, gs, b, s)

    a_outs = [_dilated_group(a0.reshape(b, s, SEC), 0, b, s), _dilated_group(a1, 1, b, s),
              _dilated_group(a2, 2, b, s)]
    a_outs[0] = tuple(v.reshape(t, A_OUT) for v in a_outs[0])
    o_b = _sink_swa(bsec.reshape(b, s, SEC), sinks.astype(F32), b, s).reshape(t, -1)
    o_win = _nsa_window(cq.reshape(b, s, SEC), ckv.reshape(b, s, SEC), b, s).reshape(t, -1)

    n_chunks = s // CMP_STRIDE
    n_cmp = (s - CMP_BLOCK) // CMP_STRIDE + 1
    n_slc = s // SLC_BLOCK
    nq = s // Q_BLOCK
    kg = jnp.tile(qk_gain[2, 1], C_KV_HEADS).reshape(1, LANES)
    kc, vct = _compress(cmpd, *_compress_weights(cmp_pos, cmp_w1, cmp_w2), kg)
    o_cmp, sel, cnt = _cmp_select(qt, kc, vct, _overlap_t(n_slc, n_chunks, n_cmp), n_cmp=n_cmp)

    act = (cnt[:, :, :, 0, :] > 0).reshape(b, C_KV_HEADS, nq, nq, 2).any(axis=-1)
    act = act & (jnp.arange(nq)[None, :] < jnp.arange(nq)[:, None])
    order = jnp.argsort(jnp.logical_not(act), axis=-1, stable=True).astype(jnp.int32)
    count = jnp.sum(act, axis=-1, dtype=jnp.int32)[..., None]
    lists = jnp.concatenate([count, order] + [jnp.zeros_like(count)] * (SLC_GROUP - 1), axis=-1)
    o_slc = _slc_attention(lists.reshape(-1), qt, ks.reshape(b, s, 2 * LANES),
                           vst.reshape(b, nq, C_KV_HEADS, HEAD_DIM, Q_BLOCK), sel, stride=nq + SLC_GROUP)

    return _merge(x, mix_norm, a_outs, o_b, o_cmp, o_slc, o_win, cg,
                  w_in[:, QKV_COLS + c_gate_cols:].astype(BF16), w_a.astype(BF16), w_b.astype(BF16),
                  w_c.astype(BF16), _gate_expand())


def kernel(x, ffn1_norm, ffn1_w_gu, ffn1_w_down, mix_norm, w_in, qk_gain, sinks, cmp_pos, cmp_w1, cmp_w2,
           w_branch_a, w_branch_b, w_branch_c, w_out, ffn2_norm, ffn2_w_gu, ffn2_w_down):
    b, s, d = x.shape
    h = x.reshape(b * s, d)
    w1_gu, w1_down = ffn1_w_gu.astype(BF16), ffn1_w_down.astype(BF16)
    w2_gu, w2_down = ffn2_w_gu.astype(BF16), ffn2_w_down.astype(BF16)
    for l in range(ffn1_norm.shape[0]):
        h = _ffn(h, ffn1_norm[l], w1_gu, w1_down, l)
        merged = _token_mixing(h, b, s, mix_norm[l], w_in[l], qk_gain[l], sinks[l], cmp_pos[l], cmp_w1[l],
                               cmp_w2[l], w_branch_a[l], w_branch_b[l], w_branch_c[l])
        h = _out_proj(h, merged, w_out[l].astype(BF16))
        h = _ffn(h, ffn2_norm[l], w2_gu, w2_down, l)
    return h.reshape(b, s, d)
```

```python
import functools
import math

import numpy as np
import jax
import jax.numpy as jnp
from jax import lax
from jax.experimental import pallas as pl
from jax.experimental.pallas import tpu as pltpu

F32 = jnp.float32
BF16 = jnp.bfloat16

HEAD_DIM = 64
Q_BLOCK = 128
LANES = 128
A_GROUPS = ((128, 1), (512, 4), (2048, 16))
A_HEADS_PER_GROUP = 4
A_HEADS = 12
A_OUT = A_HEADS_PER_GROUP * HEAD_DIM
B_HEADS = 8
B_KV_HEADS = 2
B_WINDOW = 128
C_HEADS = 12
C_KV_HEADS = 2
C_REP = C_HEADS // C_KV_HEADS
CMP_BLOCK = 32
CMP_STRIDE = 16
CMP_HIDDEN = 256
SLC_BLOCK = 64
SLC_TOPK = 16
C_WINDOW = 512
RMS_EPS = 1e-6
NEG_INF = -1e30
SEC = 768
N_SEC = 6
QKV_COLS = SEC * N_SEC
TM = 512
CHUNK16 = Q_BLOCK * 16
VMEM_LIMIT = 56 * 1024 * 1024


def _slopes(n):
    return [float(2.0 ** (-8.0 * (h + 1) / n)) for h in range(n)]


def _cparams(sem):
    return pltpu.CompilerParams(dimension_semantics=sem, vmem_limit_bytes=VMEM_LIMIT)


def _dot(a, b):
    return jnp.dot(a, b, preferred_element_type=F32)


def _nt_dot(a, b):
    return lax.dot_general(a, b, (((1,), (1,)), ((), ())), preferred_element_type=F32)


def _split(v):
    hi = v.astype(BF16)
    return hi, (v - hi.astype(F32)).astype(BF16)


def _resident(shape):
    return pl.BlockSpec(shape, lambda *_: (0,) * len(shape), pipeline_mode=pl.Buffered(1))


def _rms_rows(x, g):
    ms = jnp.mean(x * x, axis=-1, keepdims=True)
    return (x * lax.rsqrt(ms + RMS_EPS) * g).astype(BF16)


def _deinterleave(n, d):
    p = np.zeros((n, n), np.float32)
    r = np.arange(n // d)
    for c in range(d):
        p[c * (n // d) + r, d * r + c] = 1.0
    return p


AUG_POS = 0
AUG_ROWS = 16
MASK_BIG = 1e30
SLC_GROUP = 7
LOG2E = math.log2(math.e)


def _key_pattern(n):
    pat = np.zeros((n, LANES), np.float32)
    pat[:, AUG_POS] = pat[:, AUG_POS + 1] = np.arange(n) % Q_BLOCK
    return pat


def _slot_pattern():
    pat = np.zeros((SLC_GROUP, Q_BLOCK, 2 * LANES), np.float32)
    r = np.arange(Q_BLOCK)
    for u in range(SLC_GROUP):
        pat[u, :, LANES + AUG_ROWS * (u + 1)] = r < SLC_BLOCK
        pat[u, :, LANES + AUG_ROWS * (u + 1) + 1] = r >= SLC_BLOCK
    return pat


def _block_spread(n_tiles):
    m = np.zeros((n_tiles * AUG_ROWS, 2 * n_tiles), np.float32)
    jj = np.arange(n_tiles)
    for e in range(2):
        m[AUG_ROWS * jj + e, 2 * jj + e] = 1.0
    return m


def _ffn_body(x_ref, g_ref, wg_ref, wu_ref, wd_ref, o_ref, h_ref, *, n_f):
    f = pl.program_id(1)

    @pl.when(f == 0)
    def _():
        h_ref[...] = _rms_rows(x_ref[...], g_ref[...])
        o_ref[...] = jnp.zeros_like(o_ref)

    h = h_ref[...]
    gate = _dot(h, wg_ref[...])
    up = _dot(h, wu_ref[...])
    act = (gate * jax.nn.sigmoid(gate) * up).astype(BF16)
    o_ref[...] += _dot(act, wd_ref[...])

    @pl.when(f == n_f - 1)
    def _():
        o_ref[...] = x_ref[...] + 0.5 * o_ref[...]


def _ffn(x, g, w_gu, w_down, layer, *, tm=1024, tf=512):
    t, d = x.shape
    d_ff = w_down.shape[1]
    n_f = d_ff // tf
    rows = pl.BlockSpec((tm, d), lambda i, f: (i, 0))
    return pl.pallas_call(
        functools.partial(_ffn_body, n_f=n_f),
        grid=(t // tm, n_f),
        in_specs=[
            rows,
            pl.BlockSpec((1, d), lambda i, f: (0, 0)),
            pl.BlockSpec((None, d, tf), lambda i, f: (layer, 0, f)),
            pl.BlockSpec((None, d, tf), lambda i, f: (layer, 0, f + n_f)),
            pl.BlockSpec((None, tf, d), lambda i, f: (layer, f, 0)),
        ],
        out_specs=rows,
        out_shape=jax.ShapeDtypeStruct((t, d), F32),
        scratch_shapes=[pltpu.VMEM((tm, d), BF16)],
        compiler_params=_cparams(("parallel", "arbitrary")),
    )(x, g.reshape(1, d), w_gu, w_gu, w_down)


NORM_TILE = 256
NORM_TILES = {0: (0, 1), 1: (0, 1), 2: (0, 1), 3: (0, 1, 2), 4: (0, 1, 2), 5: (1, 2)}


def _head_sumsq(y, bd):
    return _dot((y * y).astype(BF16), bd)


def _qkv_body(x_ref, g_ref, w_ref, flag_ref, gs_ref, bd_ref, p4_ref, p16_ref, eye_ref, kpat_ref,
              a0_ref, a1_ref, a2_ref, b_ref, cq_ref, ckv_ref, cmpd_ref, qt_ref, ks_ref, vst_ref, cg_ref):
    h = _rms_rows(x_ref[...], g_ref[...])
    bd = bd_ref[...]

    def project(k):
        if k < len(A_GROUPS):
            return jnp.concatenate([_dot(h, w_ref[:, part * A_HEADS * HEAD_DIM + k * A_OUT:
                                                     part * A_HEADS * HEAD_DIM + (k + 1) * A_OUT])
                                    for part in range(3)], axis=1)
        return _dot(h, w_ref[:, k * SEC:(k + 1) * SEC])

    def finish(k, y):
        tiles = []
        for c in range(SEC // NORM_TILE):
            yc = y[:, c * NORM_TILE:(c + 1) * NORM_TILE]
            if c in NORM_TILES[k]:
                cols = slice(k * SEC + c * NORM_TILE, k * SEC + (c + 1) * NORM_TILE)
                inv = lax.rsqrt(_head_sumsq(yc, bd) * (1.0 / HEAD_DIM) + RMS_EPS)
                yc = yc * jnp.where(flag_ref[:, cols] > 0, inv, 1.0) * gs_ref[:, cols]
            tiles.append(yc.astype(BF16))
        return jnp.concatenate(tiles, axis=1)

    sec = []
    for first in range(0, N_SEC, 3):
        raw = [project(k) for k in range(first, first + 3)]
        sec += [finish(first + k, y) for k, y in enumerate(raw)]
    cg_logits = _dot(h, w_ref[:, QKV_COLS:QKV_COLS + LANES])

    a0_ref[...] = sec[0]
    a1_ref[...] = _dot(p4_ref[...], sec[1]).astype(BF16).reshape(a1_ref.shape)
    a2_ref[...] = _dot(p16_ref[...], sec[2]).astype(BF16).reshape(a2_ref.shape)
    b_ref[...] = sec[3]
    y_cq = sec[4]
    cq_ref[...] = y_cq
    qt_ref[...] = _nt_dot(eye_ref[...], y_cq).astype(BF16)
    y_ckv = sec[5]
    ckv_ref[...] = y_ckv
    cmpd_ref[...] = _dot(p16_ref[...], y_ckv[:, 0:2 * LANES]).astype(BF16).reshape(cmpd_ref.shape)
    ks_ref[:, 0:LANES] = y_ckv[:, 2 * LANES:3 * LANES]
    ks_ref[:, LANES:2 * LANES] = kpat_ref[...]
    eye = eye_ref[0:LANES, 0:LANES]
    for kb in range(vst_ref.shape[0]):
        vt = _nt_dot(eye, y_ckv[kb * Q_BLOCK:(kb + 1) * Q_BLOCK, 3 * LANES:4 * LANES]).astype(BF16)
        for gg in range(C_KV_HEADS):
            vst_ref[kb, gg] = vt[gg * HEAD_DIM:(gg + 1) * HEAD_DIM]
    cg_ref[...] = jax.nn.sigmoid(cg_logits)


def _qkv_proj(x, g, w, flag, gs, b, s):
    t, d = x.shape
    tiles_per_batch = s // TM
    per16 = CHUNK16 // TM
    bd = jnp.asarray(np.kron(np.eye(NORM_TILE // HEAD_DIM), np.ones((HEAD_DIM, HEAD_DIM))), BF16)
    p4 = jnp.asarray(_deinterleave(TM, 4), BF16)
    p16 = jnp.asarray(_deinterleave(TM, 16), BF16)
    eye = jnp.asarray(np.eye(SEC), BF16)
    nat = pl.BlockSpec((TM, SEC), lambda i: (i, 0))
    out_specs = [
        nat,
        pl.BlockSpec((None, 4, Q_BLOCK, SEC), lambda i: (i, 0, 0, 0)),
        pl.BlockSpec((None, 16, TM // 16, SEC), lambda i: (i // per16, 0, i % per16, 0)),
        nat, nat, nat,
        pl.BlockSpec((None, 16, TM // 16, 2 * LANES), lambda i: (i // tiles_per_batch, 0, i % tiles_per_batch, 0)),
        pl.BlockSpec((None, SEC, TM), lambda i: (i // tiles_per_batch, 0, i % tiles_per_batch)),
        pl.BlockSpec((TM, 2 * LANES), lambda i: (i, 0)),
        pl.BlockSpec((TM // Q_BLOCK, C_KV_HEADS, HEAD_DIM, Q_BLOCK), lambda i: (i, 0, 0, 0)),
        pl.BlockSpec((TM, LANES), lambda i: (i, 0)),
    ]
    out_shape = [
        jax.ShapeDtypeStruct((t, SEC), BF16),
        jax.ShapeDtypeStruct((t // TM, 4, Q_BLOCK, SEC), BF16),
        jax.ShapeDtypeStruct((t // CHUNK16, 16, Q_BLOCK, SEC), BF16),
        jax.ShapeDtypeStruct((t, SEC), BF16),
        jax.ShapeDtypeStruct((t, SEC), BF16),
        jax.ShapeDtypeStruct((t, SEC), BF16),
        jax.ShapeDtypeStruct((b, 16, s // 16, 2 * LANES), BF16),
        jax.ShapeDtypeStruct((b, SEC, s), BF16),
        jax.ShapeDtypeStruct((t, 2 * LANES), BF16),
        jax.ShapeDtypeStruct((t // Q_BLOCK, C_KV_HEADS, HEAD_DIM, Q_BLOCK), BF16),
        jax.ShapeDtypeStruct((t, LANES), F32),
    ]
    n_w = w.shape[1]
    return pl.pallas_call(
        _qkv_body,
        grid=(t // TM,),
        in_specs=[
            pl.BlockSpec((TM, d), lambda i: (i, 0)),
            _resident((1, d)),
            _resident((d, n_w)),
            _resident((1, QKV_COLS)),
            _resident((1, QKV_COLS)),
            _resident((NORM_TILE, NORM_TILE)),
            _resident((TM, TM)),
            _resident((TM, TM)),
            _resident((SEC, SEC)),
            _resident((TM, LANES)),
        ],
        out_specs=out_specs,
        out_shape=out_shape,
        compiler_params=_cparams(("parallel",)),
    )(x, g.reshape(1, d), w, flag.reshape(1, -1), gs.reshape(1, -1), bd, p4, p16, eye,
      jnp.asarray(_key_pattern(TM), BF16))


def _banded_body(*refs, nb, qb, keys, heads, k_off, v_off, n_pairs, q_axis, use_sinks, with_lse, stack, fold,
                 log2_units):
    refs = list(refs)
    q_ref = refs.pop(0)
    kv_refs = [refs.pop(0) for _ in range({"self": 1, "window": 2, "blocks": nb + 1}[keys])]
    qc_ref, kaug_ref, band_ref = refs.pop(0), refs.pop(0), refs.pop(0)
    sink_ref = refs.pop(0) if use_sinks else None
    o_ref = refs.pop(0)
    lse_ref = refs.pop(0) if with_lse else None

    i = pl.program_id(q_axis)
    nk = (nb + 1) * Q_BLOCK
    col = lax.broadcasted_iota(jnp.int32, (Q_BLOCK, nk), 1)
    band = band_ref[...]

    def start_mask(sub):
        if sub > 0 and sub >= nb:
            return band
        return band + jnp.where(col < (nb - (i * qb + sub)) * Q_BLOCK, -MASK_BIG, 0.0)

    rel_f = (nb * Q_BLOCK + lax.broadcasted_iota(jnp.int32, (Q_BLOCK, nk), 0) - col).astype(F32)
    lane = lax.broadcasted_iota(jnp.int32, (Q_BLOCK, LANES), 1)
    low_half = lane < HEAD_DIM
    kaug = kaug_ref[...]

    def q_cols(sub, c0):
        if len(q_ref.shape) == 3:
            return q_ref[sub, :, c0:c0 + LANES]
        return q_ref[sub * Q_BLOCK:(sub + 1) * Q_BLOCK, c0:c0 + LANES]

    kv_cache = {}

    def swap_halves(tile, swapped):
        return pltpu.roll(tile.astype(F32), HEAD_DIM, 1).astype(BF16) if swapped else tile

    def kv_tile(sub, off, kv_pair, swapped):
        c0 = off + kv_pair * LANES
        if keys == "window":
            key = (off, kv_pair, swapped)
            if key not in kv_cache:
                kv_cache[key] = swap_halves(jnp.concatenate([r[:, c0:c0 + LANES] for r in kv_refs], axis=0), swapped)
            first = qb - nb + sub
            return kv_cache[key][first * Q_BLOCK:(first + nb + 1) * Q_BLOCK]
        key = (sub, off, kv_pair, swapped)
        if key not in kv_cache:
            if keys == "self":
                blocks = [kv_refs[0][:, c0:c0 + LANES] if sub == 0 else q_cols(sub - 1, c0), q_cols(sub, c0)]
            else:
                blocks = [r[:, c0:c0 + LANES] for r in kv_refs]
            kv_cache[key] = swap_halves(jnp.concatenate(blocks, axis=0), swapped)
        return kv_cache[key]

    classes = {}
    for head in heads:
        pair, half, kv_pair, kv_half, slope, hidx = head
        key = (kv_pair, kv_half != half) if stack else hidx
        classes.setdefault(key, []).append(head)

    outs = [[[None, None] for _ in range(n_pairs)] for _ in range(qb)]
    lses = [[[None, None] for _ in range(n_pairs)] for _ in range(qb)]
    groups = [(sub, members) for sub in range(qb) for members in classes.values()]

    scores = []
    for sub, members in groups:
        n_h = len(members)
        kv_pair, swapped = members[0][2], members[0][3] != members[0][1]
        mask = start_mask(sub)
        q_rows = []
        for pair, half, _, _, _, hidx in members:
            qp = q_cols(sub, pair * LANES)
            own = low_half if half == 0 else jnp.logical_not(low_half)
            qm = jnp.where(own, qp, jnp.zeros_like(qp))
            q_rows.append(jnp.concatenate([qm, qc_ref[hidx]], axis=1) if fold else qm)
        if fold:
            k_aug = jnp.concatenate([kv_tile(sub, k_off, kv_pair, swapped), kaug], axis=1)
            s = _nt_dot(jnp.concatenate(q_rows, axis=0), k_aug)
            s = (s.reshape(n_h, Q_BLOCK, nk) + mask[None]).reshape(n_h * Q_BLOCK, nk)
        else:
            bias = jnp.concatenate([mask - member[4] * rel_f for member in members], axis=0)
            s = _nt_dot(jnp.concatenate(q_rows, axis=0), kv_tile(sub, k_off, kv_pair, swapped)) + bias
        scores.append(s)

    probs = []
    for (_, members), s in zip(groups, scores):
        m = jnp.max(s, axis=1, keepdims=True)
        if use_sinks:
            assert len(members) == 1
            sink = sink_ref[members[0][5]]
            m = jnp.maximum(m, sink)
        p = jnp.exp2(s - m) if log2_units else jnp.exp(s - m)
        den = jnp.sum(p, axis=1, keepdims=True)
        if use_sinks:
            den = den + jnp.exp(sink - m)
        probs.append((p.astype(BF16), m, den))

    for (sub, members), (p, m, den) in zip(groups, probs):
        kv_pair, swapped = members[0][2], members[0][3] != members[0][1]
        r = _dot(p, kv_tile(sub, v_off, kv_pair, swapped)) / den
        lse = m + jnp.log(den) if with_lse else None
        for k, (pair, half, _, _, _, _) in enumerate(members):
            outs[sub][pair][half] = r[k * Q_BLOCK:(k + 1) * Q_BLOCK]
            if with_lse:
                lses[sub][pair][half] = jnp.broadcast_to(lse[k * Q_BLOCK:(k + 1) * Q_BLOCK], (Q_BLOCK, LANES))

    def store(ref, sub, sl, value):
        if len(ref.shape) == 3:
            ref[sub, :, sl] = value
        else:
            ref[sub * Q_BLOCK:(sub + 1) * Q_BLOCK, sl] = value

    for sub in range(qb):
        for pair in range(n_pairs):
            sl = slice(pair * LANES, (pair + 1) * LANES)
            store(o_ref, sub, sl, jnp.where(low_half, outs[sub][pair][0], outs[sub][pair][1]))
            if with_lse:
                store(lse_ref, sub, sl, jnp.where(low_half, lses[sub][pair][0], lses[sub][pair][1]))


def _banded_consts(heads, nb, max_dist):
    nk = (nb + 1) * Q_BLOCK
    slope = np.asarray([h[4] for h in heads], np.float32)[:, None]
    q_dist = (nb * Q_BLOCK + np.arange(Q_BLOCK, dtype=np.float32))[None, :]
    ones = np.ones_like(q_dist)
    vals = jnp.asarray(np.stack([slope * ones, slope * Q_BLOCK * ones, -slope * q_dist], axis=-1))
    hi = vals.astype(BF16)
    lo = (vals - hi.astype(F32)).astype(BF16)
    cols = jnp.stack([hi[..., 0], lo[..., 0], hi[..., 1], lo[..., 1], hi[..., 2], lo[..., 2]], axis=-1)
    qc = jnp.pad(cols, ((0, 0), (0, 0), (0, LANES - cols.shape[-1])))
    kaug = np.zeros((nk, LANES), np.float32)
    kaug[:, 0] = kaug[:, 1] = np.arange(nk) % Q_BLOCK
    kaug[:, 2] = kaug[:, 3] = np.arange(nk) // Q_BLOCK
    kaug[:, 4] = kaug[:, 5] = 1.0
    rel = nb * Q_BLOCK + np.arange(Q_BLOCK)[:, None] - np.arange(nk)[None, :]
    band = np.where((rel >= 0) & (rel <= max_dist), 0.0, -MASK_BIG).astype(np.float32)
    return [qc, jnp.asarray(kaug, BF16), jnp.asarray(band)]


BAND_QB = 4
CMP_QB = 1


def _banded_call(q_arr, kv_arr, *, grid, q_spec, kv_specs, out_spec, out_shape, out_cols, nb, qb, max_dist, heads,
                 k_off, v_off, q_axis, sinks=None, with_lse=False, stack=False, fold=False, log2_units=False):
    assert not (log2_units and (with_lse or sinks is not None))
    if len(kv_specs) == 1 and nb == 1:
        keys = "self"
    elif qb > 1:
        keys = "window"
        assert len(kv_specs) == 2 and nb <= qb
    else:
        keys = "blocks"
        assert len(kv_specs) == nb + 1
    consts = _banded_consts(heads, nb, max_dist)
    in_specs = [q_spec] + list(kv_specs) + [pl.BlockSpec(c.shape, lambda *_, nd=c.ndim: (0,) * nd) for c in consts]
    args = [q_arr] + [kv_arr] * len(kv_specs) + consts
    if sinks is not None:
        in_specs.append(pl.BlockSpec(memory_space=pltpu.SMEM))
        args.append(sinks)
    oshape = jax.ShapeDtypeStruct(out_shape, F32)
    body = functools.partial(_banded_body, nb=nb, qb=qb, keys=keys, heads=heads, k_off=k_off, v_off=v_off,
                             n_pairs=out_cols // LANES, q_axis=q_axis, use_sinks=sinks is not None,
                             with_lse=with_lse, stack=stack, fold=fold, log2_units=log2_units)
    return pl.pallas_call(
        body,
        grid=grid,
        in_specs=in_specs,
        out_specs=[out_spec, out_spec] if with_lse else out_spec,
        out_shape=[oshape, oshape] if with_lse else oshape,
        compiler_params=_cparams(("parallel",) * len(grid)),
    )(*args)


def _row_specs(b, s, qb, out_cols):
    q_spec = pl.BlockSpec((None, qb * Q_BLOCK, SEC), lambda bb, i: (bb, i, 0))
    prev = pl.BlockSpec((None, Q_BLOCK, SEC), lambda bb, i: (bb, jnp.maximum(qb * i - 1, 0), 0))
    out_spec = pl.BlockSpec((None, qb * Q_BLOCK, out_cols), lambda bb, i: (bb, i, 0))
    return dict(grid=(b, s // (qb * Q_BLOCK)), q_axis=1, q_spec=q_spec, kv_specs=[prev], out_spec=out_spec,
                out_shape=(b, s, out_cols), qb=qb)


def _dilated_group(arr, gi, b, s):
    window, dil = A_GROUPS[gi]
    slopes = _slopes(A_HEADS)
    heads = tuple((hh // 2, hh % 2, hh // 2, hh % 2, slopes[gi * A_HEADS_PER_GROUP + hh] * dil, hh)
                  for hh in range(A_HEADS_PER_GROUP))
    common = dict(out_cols=A_OUT, nb=1, max_dist=window // dil, heads=heads, k_off=256, v_off=512,
                  with_lse=True, stack=False)
    if dil == 1:
        return _banded_call(arr, arr, **_row_specs(b, s, BAND_QB, A_OUT), **common)
    nc = s // (Q_BLOCK * dil)
    qb = BAND_QB if nc % BAND_QB == 0 else 1
    q_spec = pl.BlockSpec((qb, None, Q_BLOCK, SEC), lambda bb, c, i: ((bb * nc) // qb + i, c, 0, 0))
    prev = pl.BlockSpec((None, None, Q_BLOCK, SEC), lambda bb, c, i: (bb * nc + jnp.maximum(qb * i - 1, 0), c, 0, 0))
    out_spec = pl.BlockSpec((qb, None, Q_BLOCK, A_OUT), lambda bb, c, i: ((bb * nc) // qb + i, c, 0, 0))
    return _banded_call(arr, arr, grid=(b, dil, nc // qb), q_axis=2, q_spec=q_spec, kv_specs=[prev],
                        out_spec=out_spec, out_shape=(b * nc, dil, Q_BLOCK, A_OUT), qb=qb, **common)


def _sink_swa(arr, sinks, b, s):
    slopes = _slopes(B_HEADS)
    rep = B_HEADS // B_KV_HEADS
    heads = tuple((h // 2, h % 2, 0, h // rep, slopes[h], h) for h in range(B_HEADS))
    return _banded_call(arr, arr, **_row_specs(b, s, BAND_QB, B_HEADS * HEAD_DIM), out_cols=B_HEADS * HEAD_DIM,
                        nb=1, max_dist=B_WINDOW - 1, heads=heads, k_off=512, v_off=640, sinks=sinks)


def _nsa_window(cq, ckv, b, s):
    slopes = _slopes(C_HEADS)
    heads = tuple((h // 2, h % 2, 0, h // C_REP, slopes[h] * LOG2E, h) for h in range(C_HEADS))
    nb = -(-(C_WINDOW - 1) // Q_BLOCK)
    qb = nb
    blk = (None, qb * Q_BLOCK, SEC)
    kv_specs = [pl.BlockSpec(blk, lambda bb, i: (bb, jnp.maximum(i - 1, 0), 0)),
                pl.BlockSpec(blk, lambda bb, i: (bb, i, 0))]
    out_cols = C_HEADS * HEAD_DIM
    return _banded_call(
        cq, ckv, grid=(b, s // (qb * Q_BLOCK)), q_axis=1, qb=qb, log2_units=True,
        q_spec=pl.BlockSpec(blk, lambda bb, i: (bb, i, 0)), kv_specs=kv_specs,
        out_spec=pl.BlockSpec((None, qb * Q_BLOCK, out_cols), lambda bb, i: (bb, i, 0)),
        out_shape=(b, s, out_cols), out_cols=out_cols, nb=nb, max_dist=C_WINDOW - 1, heads=heads,
        k_off=512, v_off=640, stack=True, fold=True)


def _compress_body(t_ref, wb_ref, prow_ref, w2k_ref, w2vt_ref, kg_ref, bd_ref, kc_ref, vct_ref, *, n_chunks):
    hid_cols = 2 * C_KV_HEADS * CMP_HIDDEN
    u = jnp.zeros((n_chunks, hid_cols), F32)
    v = jnp.zeros((n_chunks, hid_cols), F32)
    pc = jnp.zeros((1, hid_cols), F32)
    for c in range(CMP_STRIDE):
        tc = t_ref[c]
        u = u + _dot(tc, wb_ref[0, c])
        v = v + _dot(tc, wb_ref[1, c])
        pc = pc + _dot(prow_ref[0, c], wb_ref[0, c])[0:1] + _dot(prow_ref[1, c], wb_ref[1, c])[0:1]
    hsum = u + pltpu.roll(v, n_chunks - 1, 0) + pc
    hid = (hsum * jax.nn.sigmoid(hsum)).astype(BF16)
    half = C_KV_HEADS * CMP_HIDDEN
    k = _dot(hid[:, :half], w2k_ref[...])
    hi, lo = _split(k * k)
    ss = _dot(hi, bd_ref[...]) + _dot(lo, bd_ref[...])
    kc_ref[...] = (k * lax.rsqrt(ss * (1.0 / HEAD_DIM) + RMS_EPS) * kg_ref[...]).astype(BF16)
    vct_ref[...] = _nt_dot(w2vt_ref[...], hid[:, half:]).astype(BF16)


def _compress(cmpd, wb, prow, w2k, w2vt, kg):
    b, _, n_chunks, width = cmpd.shape
    bd = jnp.asarray(np.kron(np.eye(LANES // HEAD_DIM), np.ones((HEAD_DIM, HEAD_DIM))), BF16)
    return pl.pallas_call(
        functools.partial(_compress_body, n_chunks=n_chunks),
        grid=(b,),
        in_specs=[
            pl.BlockSpec((None, CMP_STRIDE, n_chunks, width), lambda bb: (bb, 0, 0, 0)),
            _resident(wb.shape), _resident(prow.shape), _resident(w2k.shape), _resident(w2vt.shape),
            _resident((1, LANES)), _resident((LANES, LANES)),
        ],
        out_specs=[
            pl.BlockSpec((None, n_chunks, LANES), lambda bb: (bb, 0, 0)),
            pl.BlockSpec((None, LANES, n_chunks), lambda bb: (bb, 0, 0)),
        ],
        out_shape=[
            jax.ShapeDtypeStruct((b, n_chunks, LANES), BF16),
            jax.ShapeDtypeStruct((b, LANES, n_chunks), BF16),
        ],
        compiler_params=_cparams(("parallel",)),
    )(cmpd, wb, prow, w2k, w2vt, kg, bd)


def _to_natural(ot_list, eye):
    pairs = []
    for k in range(0, len(ot_list), 2):
        hi, lo = _split(jnp.concatenate([ot_list[k], ot_list[k + 1]], axis=0))
        pairs.append(_nt_dot(eye, hi) + _nt_dot(eye, lo))
    return jnp.concatenate(pairs, axis=1)


def _cmp_body(qt_ref, kc_ref, vct_ref, ovt_ref, eye_ref, spread_ref, cmask_ref,
              o_ref, selm_ref, cnt_ref, sel_sc, *, n_top):
    g = pl.program_id(1)
    i = pl.program_id(2)
    n_pad = kc_ref.shape[0]
    n_slc = ovt_ref.shape[0]
    qw = qt_ref.shape[1]
    qb = qw // Q_BLOCK
    per_q = Q_BLOCK // CMP_STRIDE
    own_rows = (lax.broadcasted_iota(jnp.int32, (LANES, qw), 0) // HEAD_DIM) == g
    slopes = [sl * LOG2E for sl in _slopes(C_HEADS)]

    def attend(rows):
        kc = kc_ref[0:rows, :]
        vct = vct_ref[:, 0:rows]
        mask = jnp.concatenate(
            [cmask_ref[pl.ds(pl.multiple_of(n_pad - per_q * (qb * i + h), 8), rows), :] for h in range(qb)], axis=1)
        n_f = (CMP_STRIDE * lax.broadcasted_iota(jnp.int32, (rows, qw), 0)).astype(F32)
        psum = jnp.zeros((rows, qw), F32)
        outs = []
        for r in range(C_REP):
            slope = jnp.where(g == 0, slopes[r], slopes[C_REP + r])
            qt = qt_ref[r * HEAD_DIM:(r + 1) * HEAD_DIM, :]
            q_pad = jnp.where(own_rows, jnp.concatenate([qt, qt], axis=0), jnp.zeros((LANES, qw), BF16))
            s = _dot(kc, q_pad) + (slope * n_f + mask)
            m = jnp.maximum(jnp.max(s, axis=0, keepdims=True), -1e20)
            e = jnp.exp2(s - m)
            den = jnp.sum(e, axis=0, keepdims=True)
            p = e * (1.0 / jnp.where(den > 0, den, 1.0))
            psum = psum + p
            both = _dot(vct, p.astype(BF16))
            outs.append(jnp.where(g == 0, both[:HEAD_DIM], both[HEAD_DIM:]))
        o_ref[...] = _to_natural(outs, eye_ref[...])

        n_j = rows * CMP_STRIDE // SLC_BLOCK
        hi, lo = _split(psum)
        ovt = ovt_ref[0:n_j, 0:rows]
        imp = _dot(ovt, hi) + _dot(ovt, lo)
        j_idx = lax.broadcasted_iota(jnp.int32, (n_j, qw), 0)
        t_q = i * qw + lax.broadcasted_iota(jnp.int32, (n_j, qw), 1)
        cur = lax.shift_right_logical(t_q, int(math.log2(SLC_BLOCK)))
        forced = ((j_idx == 0) | (j_idx == cur) | (j_idx == cur - 1)) & (j_idx <= cur)
        v = jnp.where((j_idx <= cur) & jnp.logical_not(forced), imp, -1.0)
        sel = jnp.where(forced, 1.0, 0.0)
        for _ in range(n_top - 3):
            m = jnp.max(v, axis=0, keepdims=True)
            first = jnp.min(jnp.where((v == m) & (m >= 0.0), j_idx, n_slc), axis=0, keepdims=True)
            pick = j_idx == first
            sel = jnp.where(pick, 1.0, sel)
            v = jnp.where(pick, -1.0, v)
        sel_sc[0:n_j, :] = sel
        if n_j < n_slc:
            sel_sc[n_j:, :] = jnp.zeros((n_slc - n_j, qw), F32)

    n_var = n_pad // LANES
    for var in range(n_var):
        pl.when((qb * i + qb - 1) // (LANES // per_q) == var)(functools.partial(attend, (var + 1) * LANES))

    sel = sel_sc[...]
    neg = jnp.where(sel > 0, 0.0, -MASK_BIG).astype(BF16)
    mask_rows = _dot(spread_ref[...], neg).astype(BF16)
    sel_b = sel.astype(BF16)
    for h in range(qb):
        lanes = slice(h * Q_BLOCK, (h + 1) * Q_BLOCK)
        selm_ref[h] = mask_rows[:, lanes]
        cnt_ref[h] = _nt_dot(jnp.ones((8, Q_BLOCK), BF16), sel_b[:, lanes])


def _cmp_select(qt, kc, vct, ovt, *, n_cmp):
    b, _, s = qt.shape
    g = C_KV_HEADS
    n_pad = kc.shape[1]
    n_slc = ovt.shape[0]
    nq = s // Q_BLOCK
    rows = C_REP * HEAD_DIM
    qb = CMP_QB if nq % CMP_QB == 0 else 1
    qw = qb * Q_BLOCK
    steps = nq // qb
    eye = jnp.asarray(np.eye(qw), BF16)
    n_top = min(SLC_TOPK, n_slc)
    assert n_top > 3 and n_cmp == n_pad - 1 and n_pad % LANES == 0
    n_rel = np.arange(-n_pad, n_pad)[:, None]
    cmask = np.where(CMP_STRIDE * n_rel + CMP_BLOCK - 1 <= np.arange(Q_BLOCK)[None, :], 0.0, -MASK_BIG)
    return pl.pallas_call(
        functools.partial(_cmp_body, n_top=n_top),
        grid=(b, g, steps),
        in_specs=[
            pl.BlockSpec((None, rows, qw), lambda bb, gg, i: (bb, gg, i)),
            pl.BlockSpec((None, n_pad, LANES), lambda bb, gg, i: (bb, 0, 0)),
            pl.BlockSpec((None, LANES, n_pad), lambda bb, gg, i: (bb, 0, 0)),
            pl.BlockSpec((n_slc, n_pad), lambda bb, gg, i: (0, 0)),
            pl.BlockSpec((qw, qw), lambda bb, gg, i: (0, 0)),
            pl.BlockSpec((nq * AUG_ROWS, n_slc), lambda bb, gg, i: (0, 0)),
            pl.BlockSpec((2 * n_pad, Q_BLOCK), lambda bb, gg, i: (0, 0)),
        ],
        out_specs=[
            pl.BlockSpec((qw, rows), lambda bb, gg, i: (bb * steps + i, gg)),
            pl.BlockSpec((None, None, qb, nq * AUG_ROWS, Q_BLOCK), lambda bb, gg, i: (bb, gg, i, 0, 0)),
            pl.BlockSpec((None, None, qb, 8, n_slc), lambda bb, gg, i: (bb, gg, i, 0, 0)),
        ],
        out_shape=[
            jax.ShapeDtypeStruct((b * s, g * rows), F32),
            jax.ShapeDtypeStruct((b, g, nq, nq * AUG_ROWS, Q_BLOCK), BF16),
            jax.ShapeDtypeStruct((b, g, nq, 8, n_slc), F32),
        ],
        scratch_shapes=[pltpu.VMEM((n_slc, qw), F32)],
        compiler_params=_cparams(("parallel", "parallel", "parallel")),
    )(qt, kc, vct, ovt, eye, jnp.asarray(_block_spread(nq), BF16), jnp.asarray(cmask, F32))


def _slc_body(list_ref, qt_ref, ks_ref, vst_ref, selm_ref, eye_ref, slot_ref, o_ref, qaug, m_sc, l_sc, acc_sc,
              *, nq, stride):
    bb = pl.program_id(0)
    g = pl.program_id(1)
    i = pl.program_id(2)
    width = C_REP * Q_BLOCK
    slopes = [sl * LOG2E for sl in _slopes(C_HEADS)]
    slope_s = [jnp.where(g == 0, slopes[r], slopes[C_REP + r]) for r in range(C_REP)]

    own_rows = (lax.broadcasted_iota(jnp.int32, (LANES, width), 0) // HEAD_DIM) == g
    q6 = jnp.concatenate([qt_ref[r * HEAD_DIM:(r + 1) * HEAD_DIM, :] for r in range(C_REP)], axis=1)
    qaug[0:LANES, :] = jnp.where(own_rows, jnp.concatenate([q6, q6], axis=0), jnp.zeros((LANES, width), BF16))
    head = lax.broadcasted_iota(jnp.int32, (LANES, width), 1) // Q_BLOCK
    row = lax.broadcasted_iota(jnp.int32, (LANES, width), 0)
    slope_t = jnp.zeros((LANES, width), F32)
    for r in range(C_REP):
        slope_t = jnp.where(head == r, slope_s[r], slope_t)
    s_hi, s_lo = _split(slope_t)
    slope_rows = jnp.where(row == AUG_POS, s_hi.astype(F32), jnp.where(row == AUG_POS + 1, s_lo.astype(F32), 0.0))
    qaug[LANES:, :] = slope_rows.astype(BF16)

    q_loc = lax.broadcasted_iota(jnp.int32, (Q_BLOCK, Q_BLOCK), 1)
    k_loc = lax.broadcasted_iota(jnp.int32, (Q_BLOCK, Q_BLOCK), 0)
    causal = jnp.where(k_loc > q_loc, -MASK_BIG, 0.0)

    m_sc[...] = jnp.full(m_sc.shape, NEG_INF, F32)
    l_sc[...] = jnp.zeros(l_sc.shape, F32)
    acc_sc[...] = jnp.zeros(acc_sc.shape, F32)

    def accumulate(tiles, own_first):
        keys = []
        for u, (jj, _) in enumerate(tiles):
            keys.append(ks_ref[pl.ds(pl.multiple_of(jj * Q_BLOCK, Q_BLOCK), Q_BLOCK), :] + slot_ref[u])
            rows = selm_ref[pl.ds(pl.multiple_of(jj * AUG_ROWS, AUG_ROWS), AUG_ROWS), :]
            lo = LANES + AUG_ROWS * (u + 1)
            qaug[lo:lo + AUG_ROWS, :] = jnp.concatenate([rows] * C_REP, axis=1)
        st = _dot(jnp.concatenate(keys, axis=0), qaug[...])
        ps, alphas = [], []
        for r in range(C_REP):
            sl = slice(r * Q_BLOCK, (r + 1) * Q_BLOCK)
            m_old = m_sc[:, sl]
            m_new = m_old
            parts = []
            for u, (jj, extra) in enumerate(tiles):
                s = st[u * Q_BLOCK:(u + 1) * Q_BLOCK, sl]
                if own_first and u == 0:
                    s = s + causal
                c = slope_s[r] * ((jj - i) * Q_BLOCK).astype(F32) + extra
                m_new = jnp.maximum(m_new, jnp.max(s, axis=0, keepdims=True) + c)
                parts.append((s, c))
            alpha = jnp.exp2(m_old - m_new)
            l_new = alpha * l_sc[:, sl]
            p_rows = []
            for s, c in parts:
                p = jnp.exp2(s + (c - m_new))
                l_new = l_new + jnp.sum(p, axis=0, keepdims=True)
                p_rows.append(p.astype(BF16))
            l_sc[:, sl] = l_new
            m_sc[:, sl] = m_new
            ps.append(jnp.concatenate(p_rows, axis=0))
            alphas.append(alpha)
        values = jnp.concatenate([vst_ref[jj] for jj, _ in tiles], axis=1)
        pv = _dot(values, jnp.concatenate(ps, axis=1))
        acc_sc[...] = jnp.concatenate(alphas, axis=1) * acc_sc[...] + pv

    base = ((bb * pl.num_programs(1) + g) * nq + i) * stride
    count = list_ref[base]

    def listed(slot):
        return list_ref[base + 1 + slot], jnp.where(slot < count, 0.0, -MASK_BIG)

    accumulate([(i, 0.0)] + [listed(u) for u in range(SLC_GROUP - 1)], True)

    def step(k, carry):
        accumulate([listed(SLC_GROUP - 1 + SLC_GROUP * k + u) for u in range(SLC_GROUP)], False)
        return carry

    rest = jnp.maximum(count - (SLC_GROUP - 1), 0)
    full = rest // SLC_GROUP
    lax.fori_loop(0, full, step, 0)
    left = rest - full * SLC_GROUP
    first_left = SLC_GROUP - 1 + full * SLC_GROUP
    for size in range(2, SLC_GROUP + 1, 2):
        @pl.when((left > size - 2) & (left <= size))
        def _(size=size):
            accumulate([listed(first_left + u) for u in range(size)], False)
    l = l_sc[...]
    o = acc_sc[...] / jnp.where(l > 0, l, 1.0)
    o_ref[...] = _to_natural([o[:, r * Q_BLOCK:(r + 1) * Q_BLOCK] for r in range(C_REP)], eye_ref[...])


def _slc_attention(lists, qt, ks, vst, sel, *, stride):
    b, _, s = qt.shape
    g = C_KV_HEADS
    nq = s // Q_BLOCK
    rows = C_REP * HEAD_DIM
    width = C_REP * Q_BLOCK
    eye = jnp.asarray(np.eye(LANES), BF16)
    assert AUG_ROWS * (SLC_GROUP + 1) <= LANES
    slots = jnp.asarray(_slot_pattern(), BF16)
    grid_spec = pltpu.PrefetchScalarGridSpec(
        num_scalar_prefetch=1,
        grid=(b, g, nq),
        in_specs=[
            pl.BlockSpec((None, rows, Q_BLOCK), lambda bb, gg, i, bits: (bb, gg, i)),
            pl.BlockSpec((None, s, 2 * LANES), lambda bb, gg, i, bits: (bb, 0, 0)),
            pl.BlockSpec((None, nq, None, HEAD_DIM, Q_BLOCK), lambda bb, gg, i, bits: (bb, 0, gg, 0, 0)),
            pl.BlockSpec((None, None, None, nq * AUG_ROWS, Q_BLOCK), lambda bb, gg, i, bits: (bb, gg, i, 0, 0)),
            pl.BlockSpec((LANES, LANES), lambda bb, gg, i, bits: (0, 0)),
            pl.BlockSpec(slots.shape, lambda bb, gg, i, bits: (0, 0, 0)),
        ],
        out_specs=pl.BlockSpec((Q_BLOCK, rows), lambda bb, gg, i, bits: (bb * nq + i, gg)),
        scratch_shapes=[
            pltpu.VMEM((2 * LANES, width), BF16),
            pltpu.VMEM((1, width), F32),
            pltpu.VMEM((1, width), F32),
            pltpu.VMEM((HEAD_DIM, width), F32),
        ],
    )
    return pl.pallas_call(
        functools.partial(_slc_body, nq=nq, stride=stride),
        grid_spec=grid_spec,
        out_shape=jax.ShapeDtypeStruct((b * s, g * rows), F32),
        compiler_params=_cparams(("parallel", "parallel", "parallel")),
    )(lists, qt, ks, vst, sel, eye, slots)


def _merge_body(x_ref, g_ref, oa0, la0, oa1, la1, oa2, la2, ob_ref, ocmp_ref, oslc_ref, owin_ref, cg_ref,
                p4t_ref, p16t_ref, ex_ref, wg0_ref, wg1_ref, wg2_ref, wa_ref, wb_ref, wc_ref,
                out_ref, h_ref, oall_ref):
    @pl.when(pl.program_id(1) == 0)
    def _():
        def natural(ref, pt_ref):
            hi, lo = _split(ref[...].reshape(TM, A_OUT))
            return _dot(pt_ref[...], hi) + _dot(pt_ref[...], lo)

        o1, l1 = natural(oa1, p4t_ref), natural(la1, p4t_ref)
        o2, l2 = natural(oa2, p16t_ref), natural(la2, p16t_ref)
        cg_split = jnp.concatenate(_split(cg_ref[...]), axis=1)
        gate_c = [_dot(cg_split, ex_ref[w]) for w in range(3)]

        h_ref[...] = _rms_rows(x_ref[...], g_ref[...])
        o0, l0 = oa0[...], la0[...]
        mx = jnp.maximum(jnp.maximum(l0, l1), l2)
        e0, e1, e2 = jnp.exp(l0 - mx), jnp.exp(l1 - mx), jnp.exp(l2 - mx)
        oall_ref[:, 0:A_OUT] = ((e0 * o0 + e1 * o1 + e2 * o2) / (e0 + e1 + e2)).astype(BF16)
        oall_ref[:, A_OUT:A_OUT + B_HEADS * HEAD_DIM] = ob_ref[...].astype(BF16)
        o_c = gate_c[0] * ocmp_ref[...] + gate_c[1] * oslc_ref[...] + gate_c[2] * owin_ref[...]
        oall_ref[:, A_OUT + B_HEADS * HEAD_DIM:] = o_c.astype(BF16)

    h = h_ref[...]
    c0, c1 = A_OUT, A_OUT + B_HEADS * HEAD_DIM
    merged = jax.nn.sigmoid(_dot(h, wg0_ref[...])) * _dot(oall_ref[:, 0:c0], wa_ref[...])
    merged += jax.nn.sigmoid(_dot(h, wg1_ref[...])) * _dot(oall_ref[:, c0:c1], wb_ref[...])
    merged += jax.nn.sigmoid(_dot(h, wg2_ref[...])) * _dot(oall_ref[:, c1:], wc_ref[...])
    out_ref[...] = merged.astype(BF16)


def _merge(x, g, a_outs, ob, ocmp, oslc, owin, cg, w_gate, wa, wb, wc, ex, *, tn=512):
    t, d = x.shape
    per16 = CHUNK16 // TM
    n_t = d // tn
    (oa0, la0), (oa1, la1), (oa2, la2) = a_outs

    last = t // TM - 1

    def tile(i, n, ahead):
        return jnp.minimum(i + jnp.where(n >= ahead, 1, 0), last)

    def rows(a, ahead):
        return pl.BlockSpec((TM, a.shape[1]), lambda i, n: (tile(i, n, ahead), 0))

    a1_spec = pl.BlockSpec((None, 4, Q_BLOCK, A_OUT), lambda i, n: (tile(i, n, 1), 0, 0, 0))
    a2_spec = pl.BlockSpec((None, 16, TM // 16, A_OUT),
                           lambda i, n: (tile(i, n, 1) // per16, 0, tile(i, n, 1) % per16, 0))
    p4t = jnp.asarray(_deinterleave(TM, 4).T, BF16)
    p16t = jnp.asarray(_deinterleave(TM, 16).T, BF16)
    in_specs = [
        rows(x, 1), _resident((1, d)),
        rows(oa0, 1), rows(la0, 1), a1_spec, a1_spec, a2_spec, a2_spec,
        rows(ob, 3), rows(ocmp, 2), rows(oslc, 2), rows(owin, 2), rows(cg, 3),
        _resident((TM, TM)), _resident((TM, TM)), _resident(ex.shape),
        pl.BlockSpec((d, tn), lambda i, n: (0, n)),
        pl.BlockSpec((d, tn), lambda i, n: (0, n + n_t)),
        pl.BlockSpec((d, tn), lambda i, n: (0, n + 2 * n_t)),
        pl.BlockSpec((wa.shape[0], tn), lambda i, n: (0, n)),
        pl.BlockSpec((wb.shape[0], tn), lambda i, n: (0, n)),
        pl.BlockSpec((wc.shape[0], tn), lambda i, n: (0, n)),
    ]
    return pl.pallas_call(
        _merge_body,
        grid=(t // TM, n_t),
        in_specs=in_specs,
        out_specs=pl.BlockSpec((TM, tn), lambda i, n: (i, n)),
        out_shape=jax.ShapeDtypeStruct((t, d), BF16),
        scratch_shapes=[pltpu.VMEM((TM, d), BF16), pltpu.VMEM((TM, wa.shape[0] + wb.shape[0] + wc.shape[0]), BF16)],
        compiler_params=_cparams(("parallel", "arbitrary")),
    )(x, g.reshape(1, d), oa0, la0, oa1, la1, oa2, la2, ob, ocmp, oslc, owin, cg, p4t, p16t, ex,
      w_gate, w_gate, w_gate, wa, wb, wc)


def _out_body(x_ref, m_ref, w_ref, o_ref):
    o_ref[...] = x_ref[...] + _dot(m_ref[...], w_ref[...])


def _out_proj(x, merged, w_out):
    t, d = x.shape
    rows = pl.BlockSpec((TM, d), lambda i: (i, 0))
    return pl.pallas_call(
        _out_body,
        grid=(t // TM,),
        in_specs=[rows, rows, _resident((d, d))],
        out_specs=rows,
        out_shape=jax.ShapeDtypeStruct((t, d), F32),
        compiler_params=_cparams(("parallel",)),
    )(x, merged, w_out)


def _qkv_column_params(qk_gain):
    flag, gain, scale = [], [], []
    one = jnp.ones((HEAD_DIM,), F32)

    def add(n_heads, normed, is_q, gvec, units=1.0):
        for _ in range(n_heads):
            flag.append(np.full((HEAD_DIM,), 1.0 if normed else 0.0, np.float32))
            gain.append(gvec if normed else one)
            scale.append(np.full((HEAD_DIM,), units * HEAD_DIM ** -0.5 if is_q else 1.0, np.float32))

    for _ in range(len(A_GROUPS)):
        add(A_HEADS_PER_GROUP, True, True, qk_gain[0, 0])
        add(A_HEADS_PER_GROUP, True, False, qk_gain[0, 1])
        add(A_HEADS_PER_GROUP, False, False, one)
    add(B_HEADS, True, True, qk_gain[1, 0])
    add(B_KV_HEADS, True, False, qk_gain[1, 1])
    add(B_KV_HEADS, False, False, one)
    add(C_HEADS, True, True, qk_gain[2, 0], units=LOG2E)
    for normed in (False, False, True, False, True, False):
        add(C_KV_HEADS, normed, False, qk_gain[2, 1])
    flag = np.concatenate(flag)
    assert flag.shape[0] == QKV_COLS
    return jnp.asarray(flag), jnp.concatenate(gain) * jnp.asarray(np.concatenate(scale))


def _overlap_t(n_slc, n_pad, n_cmp):
    n = np.arange(n_pad)[None, :]
    j = np.arange(n_slc)[:, None]
    start, end = CMP_STRIDE * n, CMP_STRIDE * n + CMP_BLOCK - 1
    ov = (start <= SLC_BLOCK * j + SLC_BLOCK - 1) & (end >= SLC_BLOCK * j) & (n < n_cmp)
    return jnp.asarray(ov, BF16)


def _gate_expand():
    ex = np.zeros((3, LANES, C_HEADS * HEAD_DIM), np.float32)
    for w in range(3):
        for h in range(C_HEADS):
            ex[w, h * 3 + w, h * HEAD_DIM:(h + 1) * HEAD_DIM] = 1.0
    return jnp.asarray(np.concatenate([ex, ex], axis=1), BF16)


def _compress_weights(cmp_pos, cmp_w1, cmp_w2):
    n_q = 2 * C_KV_HEADS
    w1 = cmp_w1.reshape(2, 2, CMP_STRIDE, HEAD_DIM, CMP_HIDDEN)
    w1q = jnp.repeat(w1, C_KV_HEADS, axis=0)
    wb = jnp.einsum("qhcdn,qp->hcqdpn", w1q, jnp.eye(n_q, dtype=F32))
    wb = wb.reshape(2, CMP_STRIDE, n_q * HEAD_DIM, n_q * CMP_HIDDEN).astype(BF16)
    pos = cmp_pos.reshape(2, 2, CMP_STRIDE, HEAD_DIM)
    prow = jnp.repeat(pos, C_KV_HEADS, axis=0).transpose(1, 2, 0, 3).reshape(2, CMP_STRIDE, 1, n_q * HEAD_DIM)
    prow = jnp.broadcast_to(prow, (2, CMP_STRIDE, 8, n_q * HEAD_DIM)).astype(BF16)
    eye_g = jnp.eye(C_KV_HEADS, dtype=F32)
    w2k = jnp.kron(eye_g, cmp_w2[0]).astype(BF16)
    w2vt = jnp.kron(eye_g, cmp_w2[1]).T.astype(BF16)
    return wb, prow, w2k, w2vt


def _token_mixing(x, b, s, mix_norm, w_in, qk_gain, sinks, cmp_pos, cmp_w1, cmp_w2, w_a, w_b, w_c):
    t, d = x.shape
    assert s % CHUNK16 == 0 and d % 512 == 0
    c_gate_cols = 3 * C_HEADS
    w_qkv = w_in[:, :QKV_COLS + LANES].astype(BF16)
    flag, gs = _qkv_column_params(qk_gain)
    a0, a1, a2, bsec, cq, ckv, cmpd, qt, ks, vst, cg = _qkv_proj(x, mix_norm, w_qkv, flag, gs, b, s)

    a_outs = [_dilated_group(a0.reshape(b, s, SEC), 0, b, s), _dilated_group(a1, 1, b, s),
              _dilated_group(a2, 2, b, s)]
    a_outs[0] = tuple(v.reshape(t, A_OUT) for v in a_outs[0])
    o_b = _sink_swa(bsec.reshape(b, s, SEC), sinks.astype(F32), b, s).reshape(t, -1)
    o_win = _nsa_window(cq.reshape(b, s, SEC), ckv.reshape(b, s, SEC), b, s).reshape(t, -1)

    n_chunks = s // CMP_STRIDE
    n_cmp = (s - CMP_BLOCK) // CMP_STRIDE + 1
    n_slc = s // SLC_BLOCK
    nq = s // Q_BLOCK
    kg = jnp.tile(qk_gain[2, 1], C_KV_HEADS).reshape(1, LANES)
    kc, vct = _compress(cmpd, *_compress_weights(cmp_pos, cmp_w1, cmp_w2), kg)
    o_cmp, sel, cnt = _cmp_select(qt, kc, vct, _overlap_t(n_slc, n_chunks, n_cmp), n_cmp=n_cmp)

    act = (cnt[:, :, :, 0, :] > 0).reshape(b, C_KV_HEADS, nq, nq, 2).any(axis=-1)
    act = act & (jnp.arange(nq)[None, :] < jnp.arange(nq)[:, None])
    order = jnp.argsort(jnp.logical_not(act), axis=-1, stable=True).astype(jnp.int32)
    count = jnp.sum(act, axis=-1, dtype=jnp.int32)[..., None]
    lists = jnp.concatenate([count, order] + [jnp.zeros_like(count)] * (SLC_GROUP - 1), axis=-1)
    o_slc = _slc_attention(lists.reshape(-1), qt, ks.reshape(b, s, 2 * LANES),
                           vst.reshape(b, nq, C_KV_HEADS, HEAD_DIM, Q_BLOCK), sel, stride=nq + SLC_GROUP)

    return _merge(x, mix_norm, a_outs, o_b, o_cmp, o_slc, o_win, cg,
                  w_in[:, QKV_COLS + c_gate_cols:].astype(BF16), w_a.astype(BF16), w_b.astype(BF16),
                  w_c.astype(BF16), _gate_expand())


def kernel(x, ffn1_norm, ffn1_w_gu, ffn1_w_down, mix_norm, w_in, qk_gain, sinks, cmp_pos, cmp_w1, cmp_w2,
           w_branch_a, w_branch_b, w_branch_c, w_out, ffn2_norm, ffn2_w_gu, ffn2_w_down):
    b, s, d = x.shape
    h = x.reshape(b * s, d)
    w1_gu, w1_down = ffn1_w_gu.astype(BF16), ffn1_w_down.astype(BF16)
    w2_gu, w2_down = ffn2_w_gu.astype(BF16), ffn2_w_down.astype(BF16)
    for l in range(ffn1_norm.shape[0]):
        h = _ffn(h, ffn1_norm[l], w1_gu, w1_down, l)
        merged = _token_mixing(h, b, s, mix_norm[l], w_in[l], qk_gain[l], sinks[l], cmp_pos[l], cmp_w1[l],
                               cmp_w2[l], w_branch_a[l], w_branch_b[l], w_branch_c[l])
        h = _out_proj(h, merged, w_out[l].astype(BF16))
        h = _ffn(h, ffn2_norm[l], w2_gu, w2_down, l)
    return h.reshape(b, s, d)
```

```python
import functools
import math

import numpy as np
import jax
import jax.numpy as jnp
from jax import lax
from jax.experimental import pallas as pl
from jax.experimental.pallas import tpu as pltpu

F32 = jnp.float32
BF16 = jnp.bfloat16

HEAD_DIM = 64
Q_BLOCK = 128
LANES = 128
A_GROUPS = ((128, 1), (512, 4), (2048, 16))
A_HEADS_PER_GROUP = 4
A_HEADS = 12
A_OUT = A_HEADS_PER_GROUP * HEAD_DIM
B_HEADS = 8
B_KV_HEADS = 2
B_WINDOW = 128
C_HEADS = 12
C_KV_HEADS = 2
C_REP = C_HEADS // C_KV_HEADS
CMP_BLOCK = 32
CMP_STRIDE = 16
CMP_HIDDEN = 256
SLC_BLOCK = 64
SLC_TOPK = 16
C_WINDOW = 512
RMS_EPS = 1e-6
NEG_INF = -1e30
SEC = 768
N_SEC = 6
QKV_COLS = SEC * N_SEC
TM = 512
CHUNK16 = Q_BLOCK * 16
VMEM_LIMIT = 56 * 1024 * 1024


def _slopes(n):
    return [float(2.0 ** (-8.0 * (h + 1) / n)) for h in range(n)]


def _cparams(sem):
    return pltpu.CompilerParams(dimension_semantics=sem, vmem_limit_bytes=VMEM_LIMIT)


def _dot(a, b):
    return jnp.dot(a, b, preferred_element_type=F32)


def _nt_dot(a, b):
    return lax.dot_general(a, b, (((1,), (1,)), ((), ())), preferred_element_type=F32)


def _split(v):
    hi = v.astype(BF16)
    return hi, (v - hi.astype(F32)).astype(BF16)


def _resident(shape):
    return pl.BlockSpec(shape, lambda *_: (0,) * len(shape), pipeline_mode=pl.Buffered(1))


def _rms_rows(x, g):
    ms = jnp.mean(x * x, axis=-1, keepdims=True)
    return (x * lax.rsqrt(ms + RMS_EPS) * g).astype(BF16)


def _deinterleave(n, d):
    p = np.zeros((n, n), np.float32)
    r = np.arange(n // d)
    for c in range(d):
        p[c * (n // d) + r, d * r + c] = 1.0
    return p


AUG_POS = 0
AUG_ROWS = 16
MASK_BIG = 1e30
SLC_GROUP = 7
LOG2E = math.log2(math.e)


def _key_pattern(n):
    pat = np.zeros((n, LANES), np.float32)
    pat[:, AUG_POS] = pat[:, AUG_POS + 1] = np.arange(n) % Q_BLOCK
    return pat


def _slot_pattern():
    pat = np.zeros((SLC_GROUP, Q_BLOCK, 2 * LANES), np.float32)
    r = np.arange(Q_BLOCK)
    for u in range(SLC_GROUP):
        pat[u, :, LANES + AUG_ROWS * (u + 1)] = r < SLC_BLOCK
        pat[u, :, LANES + AUG_ROWS * (u + 1) + 1] = r >= SLC_BLOCK
    return pat


def _block_spread(n_tiles):
    m = np.zeros((n_tiles * AUG_ROWS, 2 * n_tiles), np.float32)
    jj = np.arange(n_tiles)
    for e in range(2):
        m[AUG_ROWS * jj + e, 2 * jj + e] = 1.0
    return m


def _ffn_body(x_ref, g_ref, wg_ref, wu_ref, wd_ref, o_ref, h_ref, *, n_f):
    f = pl.program_id(1)

    @pl.when(f == 0)
    def _():
        h_ref[...] = _rms_rows(x_ref[...], g_ref[...])
        o_ref[...] = jnp.zeros_like(o_ref)

    h = h_ref[...]
    gate = _dot(h, wg_ref[...])
    up = _dot(h, wu_ref[...])
    act = (gate * jax.nn.sigmoid(gate) * up).astype(BF16)
    o_ref[...] += _dot(act, wd_ref[...])

    @pl.when(f == n_f - 1)
    def _():
        o_ref[...] = x_ref[...] + 0.5 * o_ref[...]


def _ffn(x, g, w_gu, w_down, layer, *, tm=1024, tf=512):
    t, d = x.shape
    d_ff = w_down.shape[1]
    n_f = d_ff // tf
    rows = pl.BlockSpec((tm, d), lambda i, f: (i, 0))
    return pl.pallas_call(
        functools.partial(_ffn_body, n_f=n_f),
        grid=(t // tm, n_f),
        in_specs=[
            rows,
            pl.BlockSpec((1, d), lambda i, f: (0, 0)),
            pl.BlockSpec((None, d, tf), lambda i, f: (layer, 0, f)),
            pl.BlockSpec((None, d, tf), lambda i, f: (layer, 0, f + n_f)),
            pl.BlockSpec((None, tf, d), lambda i, f: (layer, f, 0)),
        ],
        out_specs=rows,
        out_shape=jax.ShapeDtypeStruct((t, d), F32),
        scratch_shapes=[pltpu.VMEM((tm, d), BF16)],
        compiler_params=_cparams(("parallel", "arbitrary")),
    )(x, g.reshape(1, d), w_gu, w_gu, w_down)


NORM_TILE = 256
NORM_TILES = {0: (0, 1), 1: (0, 1), 2: (0, 1), 3: (0, 1, 2), 4: (0, 1, 2), 5: (1, 2)}


def _head_sumsq(y, bd):
    return _dot((y * y).astype(BF16), bd)


def _qkv_body(x_ref, g_ref, w_ref, flag_ref, gs_ref, bd_ref, p4_ref, p16_ref, eye_ref, kpat_ref,
              a0_ref, a1_ref, a2_ref, b_ref, cq_ref, ckv_ref, cmpd_ref, qt_ref, ks_ref, vst_ref, cg_ref):
    h = _rms_rows(x_ref[...], g_ref[...])
    bd = bd_ref[...]

    def project(k):
        if k < len(A_GROUPS):
            return jnp.concatenate([_dot(h, w_ref[:, part * A_HEADS * HEAD_DIM + k * A_OUT:
                                                     part * A_HEADS * HEAD_DIM + (k + 1) * A_OUT])
                                    for part in range(3)], axis=1)
        return _dot(h, w_ref[:, k * SEC:(k + 1) * SEC])

    def finish(k, y):
        tiles = []
        for c in range(SEC // NORM_TILE):
            yc = y[:, c * NORM_TILE:(c + 1) * NORM_TILE]
            if c in NORM_TILES[k]:
                cols = slice(k * SEC + c * NORM_TILE, k * SEC + (c + 1) * NORM_TILE)
                inv = lax.rsqrt(_head_sumsq(yc, bd) * (1.0 / HEAD_DIM) + RMS_EPS)
                yc = yc * jnp.where(flag_ref[:, cols] > 0, inv, 1.0) * gs_ref[:, cols]
            tiles.append(yc.astype(BF16))
        return jnp.concatenate(tiles, axis=1)

    sec = []
    for first in range(0, N_SEC, 3):
        raw = [project(k) for k in range(first, first + 3)]
        sec += [finish(first + k, y) for k, y in enumerate(raw)]
    cg_logits = _dot(h, w_ref[:, QKV_COLS:QKV_COLS + LANES])

    a0_ref[...] = sec[0]
    a1_ref[...] = _dot(p4_ref[...], sec[1]).astype(BF16).reshape(a1_ref.shape)
    a2_ref[...] = _dot(p16_ref[...], sec[2]).astype(BF16).reshape(a2_ref.shape)
    b_ref[...] = sec[3]
    y_cq = sec[4]
    cq_ref[...] = y_cq
    qt_ref[...] = _nt_dot(eye_ref[...], y_cq).astype(BF16)
    y_ckv = sec[5]
    ckv_ref[...] = y_ckv
    cmpd_ref[...] = _dot(p16_ref[...], y_ckv[:, 0:2 * LANES]).astype(BF16).reshape(cmpd_ref.shape)
    ks_ref[:, 0:LANES] = y_ckv[:, 2 * LANES:3 * LANES]
    ks_ref[:, LANES:2 * LANES] = kpat_ref[...]
    eye = eye_ref[0:LANES, 0:LANES]
    for kb in range(vst_ref.shape[0]):
        vt = _nt_dot(eye, y_ckv[kb * Q_BLOCK:(kb + 1) * Q_BLOCK, 3 * LANES:4 * LANES]).astype(BF16)
        for gg in range(C_KV_HEADS):
            vst_ref[kb, gg] = vt[gg * HEAD_DIM:(gg + 1) * HEAD_DIM]
    cg_ref[...] = jax.nn.sigmoid(cg_logits)


def _qkv_proj(x, g, w, flag, gs, b, s):
    t, d = x.shape
    tiles_per_batch = s // TM
    per16 = CHUNK16 // TM
    bd = jnp.asarray(np.kron(np.eye(NORM_TILE // HEAD_DIM), np.ones((HEAD_DIM, HEAD_DIM))), BF16)
    p4 = jnp.asarray(_deinterleave(TM, 4), BF16)
    p16 = jnp.asarray(_deinterleave(TM, 16), BF16)
    eye = jnp.asarray(np.eye(SEC), BF16)
    nat = pl.BlockSpec((TM, SEC), lambda i: (i, 0))
    out_specs = [
        nat,
        pl.BlockSpec((None, 4, Q_BLOCK, SEC), lambda i: (i, 0, 0, 0)),
        pl.BlockSpec((None, 16, TM // 16, SEC), lambda i: (i // per16, 0, i % per16, 0)),
        nat, nat, nat,
        pl.BlockSpec((None, 16, TM // 16, 2 * LANES), lambda i: (i // tiles_per_batch, 0, i % tiles_per_batch, 0)),
        pl.BlockSpec((None, SEC, TM), lambda i: (i // tiles_per_batch, 0, i % tiles_per_batch)),
        pl.BlockSpec((TM, 2 * LANES), lambda i: (i, 0)),
        pl.BlockSpec((TM // Q_BLOCK, C_KV_HEADS, HEAD_DIM, Q_BLOCK), lambda i: (i, 0, 0, 0)),
        pl.BlockSpec((TM, LANES), lambda i: (i, 0)),
    ]
    out_shape = [
        jax.ShapeDtypeStruct((t, SEC), BF16),
        jax.ShapeDtypeStruct((t // TM, 4, Q_BLOCK, SEC), BF16),
        jax.ShapeDtypeStruct((t // CHUNK16, 16, Q_BLOCK, SEC), BF16),
        jax.ShapeDtypeStruct((t, SEC), BF16),
        jax.ShapeDtypeStruct((t, SEC), BF16),
        jax.ShapeDtypeStruct((t, SEC), BF16),
        jax.ShapeDtypeStruct((b, 16, s // 16, 2 * LANES), BF16),
        jax.ShapeDtypeStruct((b, SEC, s), BF16),
        jax.ShapeDtypeStruct((t, 2 * LANES), BF16),
        jax.ShapeDtypeStruct((t // Q_BLOCK, C_KV_HEADS, HEAD_DIM, Q_BLOCK), BF16),
        jax.ShapeDtypeStruct((t, LANES), F32),
    ]
    n_w = w.shape[1]
    return pl.pallas_call(
        _qkv_body,
        grid=(t // TM,),
        in_specs=[
            pl.BlockSpec((TM, d), lambda i: (i, 0)),
            _resident((1, d)),
            _resident((d, n_w)),
            _resident((1, QKV_COLS)),
            _resident((1, QKV_COLS)),
            _resident((NORM_TILE, NORM_TILE)),
            _resident((TM, TM)),
            _resident((TM, TM)),
            _resident((SEC, SEC)),
            _resident((TM, LANES)),
        ],
        out_specs=out_specs,
        out_shape=out_shape,
        compiler_params=_cparams(("parallel",)),
    )(x, g.reshape(1, d), w, flag.reshape(1, -1), gs.reshape(1, -1), bd, p4, p16, eye,
      jnp.asarray(_key_pattern(TM), BF16))


def _banded_body(*refs, nb, qb, keys, heads, k_off, v_off, n_pairs, q_axis, use_sinks, with_lse, stack, fold,
                 log2_units):
    refs = list(refs)
    q_ref = refs.pop(0)
    kv_refs = [refs.pop(0) for _ in range({"self": 1, "window": 2, "blocks": nb + 1}[keys])]
    qc_ref, kaug_ref, band_ref = refs.pop(0), refs.pop(0), refs.pop(0)
    sink_ref = refs.pop(0) if use_sinks else None
    o_ref = refs.pop(0)
    lse_ref = refs.pop(0) if with_lse else None

    i = pl.program_id(q_axis)
    nk = (nb + 1) * Q_BLOCK
    col = lax.broadcasted_iota(jnp.int32, (Q_BLOCK, nk), 1)
    band = band_ref[...]

    def start_mask(sub):
        if sub > 0 and sub >= nb:
            return band
        return band + jnp.where(col < (nb - (i * qb + sub)) * Q_BLOCK, -MASK_BIG, 0.0)

    rel_f = (nb * Q_BLOCK + lax.broadcasted_iota(jnp.int32, (Q_BLOCK, nk), 0) - col).astype(F32)
    lane = lax.broadcasted_iota(jnp.int32, (Q_BLOCK, LANES), 1)
    low_half = lane < HEAD_DIM
    kaug = kaug_ref[...]

    def q_cols(sub, c0):
        if len(q_ref.shape) == 3:
            return q_ref[sub, :, c0:c0 + LANES]
        return q_ref[sub * Q_BLOCK:(sub + 1) * Q_BLOCK, c0:c0 + LANES]

    kv_cache = {}

    def swap_halves(tile, swapped):
        return pltpu.roll(tile.astype(F32), HEAD_DIM, 1).astype(BF16) if swapped else tile

    def kv_tile(sub, off, kv_pair, swapped):
        c0 = off + kv_pair * LANES
        if keys == "window":
            key = (off, kv_pair, swapped)
            if key not in kv_cache:
                kv_cache[key] = swap_halves(jnp.concatenate([r[:, c0:c0 + LANES] for r in kv_refs], axis=0), swapped)
            first = qb - nb + sub
            return kv_cache[key][first * Q_BLOCK:(first + nb + 1) * Q_BLOCK]
        key = (sub, off, kv_pair, swapped)
        if key not in kv_cache:
            if keys == "self":
                blocks = [kv_refs[0][:, c0:c0 + LANES] if sub == 0 else q_cols(sub - 1, c0), q_cols(sub, c0)]
            else:
                blocks = [r[:, c0:c0 + LANES] for r in kv_refs]
            kv_cache[key] = swap_halves(jnp.concatenate(blocks, axis=0), swapped)
        return kv_cache[key]

    classes = {}
    for head in heads:
        pair, half, kv_pair, kv_half, slope, hidx = head
        key = (kv_pair, kv_half != half) if stack else hidx
        classes.setdefault(key, []).append(head)

    outs = [[[None, None] for _ in range(n_pairs)] for _ in range(qb)]
    lses = [[[None, None] for _ in range(n_pairs)] for _ in range(qb)]
    groups = [(sub, members) for sub in range(qb) for members in classes.values()]

    scores = []
    for sub, members in groups:
        n_h = len(members)
        kv_pair, swapped = members[0][2], members[0][3] != members[0][1]
        mask = start_mask(sub)
        q_rows = []
        for pair, half, _, _, _, hidx in members:
            qp = q_cols(sub, pair * LANES)
            own = low_half if half == 0 else jnp.logical_not(low_half)
            qm = jnp.where(own, qp, jnp.zeros_like(qp))
            q_rows.append(jnp.concatenate([qm, qc_ref[hidx]], axis=1) if fold else qm)
        if fold:
            k_aug = jnp.concatenate([kv_tile(sub, k_off, kv_pair, swapped), kaug], axis=1)
            s = _nt_dot(jnp.concatenate(q_rows, axis=0), k_aug)
            s = (s.reshape(n_h, Q_BLOCK, nk) + mask[None]).reshape(n_h * Q_BLOCK, nk)
        else:
            bias = jnp.concatenate([mask - member[4] * rel_f for member in members], axis=0)
            s = _nt_dot(jnp.concatenate(q_rows, axis=0), kv_tile(sub, k_off, kv_pair, swapped)) + bias
        scores.append(s)

    probs = []
    for (_, members), s in zip(groups, scores):
        m = jnp.max(s, axis=1, keepdims=True)
        if use_sinks:
            assert len(members) == 1
            sink = sink_ref[members[0][5]]
            m = jnp.maximum(m, sink)
        p = jnp.exp2(s - m) if log2_units else jnp.exp(s - m)
        den = jnp.sum(p, axis=1, keepdims=True)
        if use_sinks:
            den = den + jnp.exp(sink - m)
        probs.append((p.astype(BF16), m, den))

    for (sub, members), (p, m, den) in zip(groups, probs):
        kv_pair, swapped = members[0][2], members[0][3] != members[0][1]
        r = _dot(p, kv_tile(sub, v_off, kv_pair, swapped)) / den
        lse = m + jnp.log(den) if with_lse else None
        for k, (pair, half, _, _, _, _) in enumerate(members):
            outs[sub][pair][half] = r[k * Q_BLOCK:(k + 1) * Q_BLOCK]
            if with_lse:
                lses[sub][pair][half] = jnp.broadcast_to(lse[k * Q_BLOCK:(k + 1) * Q_BLOCK], (Q_BLOCK, LANES))

    def store(ref, sub, sl, value):
        if len(ref.shape) == 3:
            ref[sub, :, sl] = value
        else:
            ref[sub * Q_BLOCK:(sub + 1) * Q_BLOCK, sl] = value

    for sub in range(qb):
        for pair in range(n_pairs):
            sl = slice(pair * LANES, (pair + 1) * LANES)
            store(o_ref, sub, sl, jnp.where(low_half, outs[sub][pair][0], outs[sub][pair][1]))
            if with_lse:
                store(lse_ref, sub, sl, jnp.where(low_half, lses[sub][pair][0], lses[sub][pair][1]))


def _banded_consts(heads, nb, max_dist):
    nk = (nb + 1) * Q_BLOCK
    slope = np.asarray([h[4] for h in heads], np.float32)[:, None]
    q_dist = (nb * Q_BLOCK + np.arange(Q_BLOCK, dtype=np.float32))[None, :]
    ones = np.ones_like(q_dist)
    vals = jnp.asarray(np.stack([slope * ones, slope * Q_BLOCK * ones, -slope * q_dist], axis=-1))
    hi = vals.astype(BF16)
    lo = (vals - hi.astype(F32)).astype(BF16)
    cols = jnp.stack([hi[..., 0], lo[..., 0], hi[..., 1], lo[..., 1], hi[..., 2], lo[..., 2]], axis=-1)
    qc = jnp.pad(cols, ((0, 0), (0, 0), (0, LANES - cols.shape[-1])))
    kaug = np.zeros((nk, LANES), np.float32)
    kaug[:, 0] = kaug[:, 1] = np.arange(nk) % Q_BLOCK
    kaug[:, 2] = kaug[:, 3] = np.arange(nk) // Q_BLOCK
    kaug[:, 4] = kaug[:, 5] = 1.0
    rel = nb * Q_BLOCK + np.arange(Q_BLOCK)[:, None] - np.arange(nk)[None, :]
    band = np.where((rel >= 0) & (rel <= max_dist), 0.0, -MASK_BIG).astype(np.float32)
    return [qc, jnp.asarray(kaug, BF16), jnp.asarray(band)]


BAND_QB = 4
CMP_QB = 1


def _banded_call(q_arr, kv_arr, *, grid, q_spec, kv_specs, out_spec, out_shape, out_cols, nb, qb, max_dist, heads,
                 k_off, v_off, q_axis, sinks=None, with_lse=False, stack=False, fold=False, log2_units=False):
    assert not (log2_units and (with_lse or sinks is not None))
    if len(kv_specs) == 1 and nb == 1:
        keys = "self"
    elif qb > 1:
        keys = "window"
        assert len(kv_specs) == 2 and nb <= qb
    else:
        keys = "blocks"
        assert len(kv_specs) == nb + 1
    consts = _banded_consts(heads, nb, max_dist)
    in_specs = [q_spec] + list(kv_specs) + [pl.BlockSpec(c.shape, lambda *_, nd=c.ndim: (0,) * nd) for c in consts]
    args = [q_arr] + [kv_arr] * len(kv_specs) + consts
    if sinks is not None:
        in_specs.append(pl.BlockSpec(memory_space=pltpu.SMEM))
        args.append(sinks)
    oshape = jax.ShapeDtypeStruct(out_shape, F32)
    body = functools.partial(_banded_body, nb=nb, qb=qb, keys=keys, heads=heads, k_off=k_off, v_off=v_off,
                             n_pairs=out_cols // LANES, q_axis=q_axis, use_sinks=sinks is not None,
                             with_lse=with_lse, stack=stack, fold=fold, log2_units=log2_units)
    return pl.pallas_call(
        body,
        grid=grid,
        in_specs=in_specs,
        out_specs=[out_spec, out_spec] if with_lse else out_spec,
        out_shape=[oshape, oshape] if with_lse else oshape,
        compiler_params=_cparams(("parallel",) * len(grid)),
    )(*args)


def _row_specs(b, s, qb, out_cols):
    q_spec = pl.BlockSpec((None, qb * Q_BLOCK, SEC), lambda bb, i: (bb, i, 0))
    prev = pl.BlockSpec((None, Q_BLOCK, SEC), lambda bb, i: (bb, jnp.maximum(qb * i - 1, 0), 0))
    out_spec = pl.BlockSpec((None, qb * Q_BLOCK, out_cols), lambda bb, i: (bb, i, 0))
    return dict(grid=(b, s // (qb * Q_BLOCK)), q_axis=1, q_spec=q_spec, kv_specs=[prev], out_spec=out_spec,
                out_shape=(b, s, out_cols), qb=qb)


def _dilated_group(arr, gi, b, s):
    window, dil = A_GROUPS[gi]
    slopes = _slopes(A_HEADS)
    heads = tuple((hh // 2, hh % 2, hh // 2, hh % 2, slopes[gi * A_HEADS_PER_GROUP + hh] * dil, hh)
                  for hh in range(A_HEADS_PER_GROUP))
    common = dict(out_cols=A_OUT, nb=1, max_dist=window // dil, heads=heads, k_off=256, v_off=512,
                  with_lse=True, stack=False)
    if dil == 1:
        return _banded_call(arr, arr, **_row_specs(b, s, BAND_QB, A_OUT), **common)
    nc = s // (Q_BLOCK * dil)
    qb = BAND_QB if nc % BAND_QB == 0 else 1
    q_spec = pl.BlockSpec((qb, None, Q_BLOCK, SEC), lambda bb, c, i: ((bb * nc) // qb + i, c, 0, 0))
    prev = pl.BlockSpec((None, None, Q_BLOCK, SEC), lambda bb, c, i: (bb * nc + jnp.maximum(qb * i - 1, 0), c, 0, 0))
    out_spec = pl.BlockSpec((qb, None, Q_BLOCK, A_OUT), lambda bb, c, i: ((bb * nc) // qb + i, c, 0, 0))
    return _banded_call(arr, arr, grid=(b, dil, nc // qb), q_axis=2, q_spec=q_spec, kv_specs=[prev],
                        out_spec=out_spec, out_shape=(b * nc, dil, Q_BLOCK, A_OUT), qb=qb, **common)


def _sink_swa(arr, sinks, b, s):
    slopes = _slopes(B_HEADS)
    rep = B_HEADS // B_KV_HEADS
    heads = tuple((h // 2, h % 2, 0, h // rep, slopes[h], h) for h in range(B_HEADS))
    return _banded_call(arr, arr, **_row_specs(b, s, BAND_QB, B_HEADS * HEAD_DIM), out_cols=B_HEADS * HEAD_DIM,
                        nb=1, max_dist=B_WINDOW - 1, heads=heads, k_off=512, v_off=640, sinks=sinks)


def _nsa_window(cq, ckv, b, s):
    slopes = _slopes(C_HEADS)
    heads = tuple((h // 2, h % 2, 0, h // C_REP, slopes[h] * LOG2E, h) for h in range(C_HEADS))
    nb = -(-(C_WINDOW - 1) // Q_BLOCK)
    qb = nb
    blk = (None, qb * Q_BLOCK, SEC)
    kv_specs = [pl.BlockSpec(blk, lambda bb, i: (bb, jnp.maximum(i - 1, 0), 0)),
                pl.BlockSpec(blk, lambda bb, i: (bb, i, 0))]
    out_cols = C_HEADS * HEAD_DIM
    return _banded_call(
        cq, ckv, grid=(b, s // (qb * Q_BLOCK)), q_axis=1, qb=qb, log2_units=True,
        q_spec=pl.BlockSpec(blk, lambda bb, i: (bb, i, 0)), kv_specs=kv_specs,
        out_spec=pl.BlockSpec((None, qb * Q_BLOCK, out_cols), lambda bb, i: (bb, i, 0)),
        out_shape=(b, s, out_cols), out_cols=out_cols, nb=nb, max_dist=C_WINDOW - 1, heads=heads,
        k_off=512, v_off=640, stack=True, fold=True)


def _compress_body(t_ref, wb_ref, prow_ref, w2k_ref, w2vt_ref, kg_ref, bd_ref, kc_ref, vct_ref, *, n_chunks):
    hid_cols = 2 * C_KV_HEADS * CMP_HIDDEN
    u = jnp.zeros((n_chunks, hid_cols), F32)
    v = jnp.zeros((n_chunks, hid_cols), F32)
    pc = jnp.zeros((1, hid_cols), F32)
    for c in range(CMP_STRIDE):
        tc = t_ref[c]
        u = u + _dot(tc, wb_ref[0, c])
        v = v + _dot(tc, wb_ref[1, c])
        pc = pc + _dot(prow_ref[0, c], wb_ref[0, c])[0:1] + _dot(prow_ref[1, c], wb_ref[1, c])[0:1]
    hsum = u + pltpu.roll(v, n_chunks - 1, 0) + pc
    hid = (hsum * jax.nn.sigmoid(hsum)).astype(BF16)
    half = C_KV_HEADS * CMP_HIDDEN
    k = _dot(hid[:, :half], w2k_ref[...])
    hi, lo = _split(k * k)
    ss = _dot(hi, bd_ref[...]) + _dot(lo, bd_ref[...])
    kc_ref[...] = (k * lax.rsqrt(ss * (1.0 / HEAD_DIM) + RMS_EPS) * kg_ref[...]).astype(BF16)
    vct_ref[...] = _nt_dot(w2vt_ref[...], hid[:, half:]).astype(BF16)


def _compress(cmpd, wb, prow, w2k, w2vt, kg):
    b, _, n_chunks, width = cmpd.shape
    bd = jnp.asarray(np.kron(np.eye(LANES // HEAD_DIM), np.ones((HEAD_DIM, HEAD_DIM))), BF16)
    return pl.pallas_call(
        functools.partial(_compress_body, n_chunks=n_chunks),
        grid=(b,),
        in_specs=[
            pl.BlockSpec((None, CMP_STRIDE, n_chunks, width), lambda bb: (bb, 0, 0, 0)),
            _resident(wb.shape), _resident(prow.shape), _resident(w2k.shape), _resident(w2vt.shape),
            _resident((1, LANES)), _resident((LANES, LANES)),
        ],
        out_specs=[
            pl.BlockSpec((None, n_chunks, LANES), lambda bb: (bb, 0, 0)),
            pl.BlockSpec((None, LANES, n_chunks), lambda bb: (bb, 0, 0)),
        ],
        out_shape=[
            jax.ShapeDtypeStruct((b, n_chunks, LANES), BF16),
            jax.ShapeDtypeStruct((b, LANES, n_chunks), BF16),
        ],
        compiler_params=_cparams(("parallel",)),
    )(cmpd, wb, prow, w2k, w2vt, kg, bd)


def _to_natural(ot_list, eye):
    pairs = []
    for k in range(0, len(ot_list), 2):
        hi, lo = _split(jnp.concatenate([ot_list[k], ot_list[k + 1]], axis=0))
        pairs.append(_nt_dot(eye, hi) + _nt_dot(eye, lo))
    return jnp.concatenate(pairs, axis=1)


def _cmp_body(qt_ref, kc_ref, vct_ref, ovt_ref, eye_ref, spread_ref, cmask_ref,
              o_ref, selm_ref, cnt_ref, sel_sc, *, n_top):
    g = pl.program_id(1)
    i = pl.program_id(2)
    n_pad = kc_ref.shape[0]
    n_slc = ovt_ref.shape[0]
    qw = qt_ref.shape[1]
    qb = qw // Q_BLOCK
    per_q = Q_BLOCK // CMP_STRIDE
    own_rows = (lax.broadcasted_iota(jnp.int32, (LANES, qw), 0) // HEAD_DIM) == g
    slopes = [sl * LOG2E for sl in _slopes(C_HEADS)]

    def attend(rows):
        kc = kc_ref[0:rows, :]
        vct = vct_ref[:, 0:rows]
        mask = jnp.concatenate(
            [cmask_ref[pl.ds(pl.multiple_of(n_pad - per_q * (qb * i + h), 8), rows), :] for h in range(qb)], axis=1)
        n_f = (CMP_STRIDE * lax.broadcasted_iota(jnp.int32, (rows, qw), 0)).astype(F32)
        psum = jnp.zeros((rows, qw), F32)
        outs = []
        for r in range(C_REP):
            slope = jnp.where(g == 0, slopes[r], slopes[C_REP + r])
            qt = qt_ref[r * HEAD_DIM:(r + 1) * HEAD_DIM, :]
            q_pad = jnp.where(own_rows, jnp.concatenate([qt, qt], axis=0), jnp.zeros((LANES, qw), BF16))
            s = _dot(kc, q_pad) + (slope * n_f + mask)
            m = jnp.maximum(jnp.max(s, axis=0, keepdims=True), -1e20)
            e = jnp.exp2(s - m)
            den = jnp.sum(e, axis=0, keepdims=True)
            p = e * (1.0 / jnp.where(den > 0, den, 1.0))
            psum = psum + p
            both = _dot(vct, p.astype(BF16))
            outs.append(jnp.where(g == 0, both[:HEAD_DIM], both[HEAD_DIM:]))
        o_ref[...] = _to_natural(outs, eye_ref[...])

        n_j = rows * CMP_STRIDE // SLC_BLOCK
        hi, lo = _split(psum)
        ovt = ovt_ref[0:n_j, 0:rows]
        imp = _dot(ovt, hi) + _dot(ovt, lo)
        j_idx = lax.broadcasted_iota(jnp.int32, (n_j, qw), 0)
        t_q = i * qw + lax.broadcasted_iota(jnp.int32, (n_j, qw), 1)
        cur = lax.shift_right_logical(t_q, int(math.log2(SLC_BLOCK)))
        forced = ((j_idx == 0) | (j_idx == cur) | (j_idx == cur - 1)) & (j_idx <= cur)
        v = jnp.where((j_idx <= cur) & jnp.logical_not(forced), imp, -1.0)
        sel = jnp.where(forced, 1.0, 0.0)
        for _ in range(n_top - 3):
            m = jnp.max(v, axis=0, keepdims=True)
            first = jnp.min(jnp.where((v == m) & (m > -0.5), j_idx, n_slc), axis=0, keepdims=True)
            pick = j_idx == first
            sel = jnp.where(pick, 1.0, sel)
            v = jnp.where(pick, -1.0, v)
        sel_sc[0:n_j, :] = sel
        if n_j < n_slc:
            sel_sc[n_j:, :] = jnp.zeros((n_slc - n_j, qw), F32)

    n_var = n_pad // LANES
    for var in range(n_var):
        pl.when((qb * i + qb - 1) // (LANES // per_q) == var)(functools.partial(attend, (var + 1) * LANES))

    sel = sel_sc[...]
    neg = jnp.where(sel > 0, 0.0, -MASK_BIG).astype(BF16)
    mask_rows = _dot(spread_ref[...], neg).astype(BF16)
    sel_b = sel.astype(BF16)
    for h in range(qb):
        lanes = slice(h * Q_BLOCK, (h + 1) * Q_BLOCK)
        selm_ref[h] = mask_rows[:, lanes]
        cnt_ref[h] = _nt_dot(jnp.ones((8, Q_BLOCK), BF16), sel_b[:, lanes])


def _cmp_select(qt, kc, vct, ovt, *, n_cmp):
    b, _, s = qt.shape
    g = C_KV_HEADS
    n_pad = kc.shape[1]
    n_slc = ovt.shape[0]
    nq = s // Q_BLOCK
    rows = C_REP * HEAD_DIM
    qb = CMP_QB if nq % CMP_QB == 0 else 1
    qw = qb * Q_BLOCK
    steps = nq // qb
    eye = jnp.asarray(np.eye(qw), BF16)
    n_top = min(SLC_TOPK, n_slc)
    assert n_top > 3 and n_cmp == n_pad - 1 and n_pad % LANES == 0
    n_rel = np.arange(-n_pad, n_pad)[:, None]
    cmask = np.where(CMP_STRIDE * n_rel + CMP_BLOCK - 1 <= np.arange(Q_BLOCK)[None, :], 0.0, -MASK_BIG)
    return pl.pallas_call(
        functools.partial(_cmp_body, n_top=n_top),
        grid=(b, g, steps),
        in_specs=[
            pl.BlockSpec((None, rows, qw), lambda bb, gg, i: (bb, gg, i)),
            pl.BlockSpec((None, n_pad, LANES), lambda bb, gg, i: (bb, 0, 0)),
            pl.BlockSpec((None, LANES, n_pad), lambda bb, gg, i: (bb, 0, 0)),
            pl.BlockSpec((n_slc, n_pad), lambda bb, gg, i: (0, 0)),
            pl.BlockSpec((qw, qw), lambda bb, gg, i: (0, 0)),
            pl.BlockSpec((nq * AUG_ROWS, n_slc), lambda bb, gg, i: (0, 0)),
            pl.BlockSpec((2 * n_pad, Q_BLOCK), lambda bb, gg, i: (0, 0)),
        ],
        out_specs=[
            pl.BlockSpec((qw, rows), lambda bb, gg, i: (bb * steps + i, gg)),
            pl.BlockSpec((None, None, qb, nq * AUG_ROWS, Q_BLOCK), lambda bb, gg, i: (bb, gg, i, 0, 0)),
            pl.BlockSpec((None, None, qb, 8, n_slc), lambda bb, gg, i: (bb, gg, i, 0, 0)),
        ],
        out_shape=[
            jax.ShapeDtypeStruct((b * s, g * rows), F32),
            jax.ShapeDtypeStruct((b, g, nq, nq * AUG_ROWS, Q_BLOCK), BF16),
            jax.ShapeDtypeStruct((b, g, nq, 8, n_slc), F32),
        ],
        scratch_shapes=[pltpu.VMEM((n_slc, qw), F32)],
        compiler_params=_cparams(("parallel", "parallel", "parallel")),
    )(qt, kc, vct, ovt, eye, jnp.asarray(_block_spread(nq), BF16), jnp.asarray(cmask, F32))


def _slc_body(list_ref, qt_ref, ks_ref, vst_ref, selm_ref, eye_ref, slot_ref, o_ref, qaug, m_sc, l_sc, acc_sc,
              *, nq, stride):
    bb = pl.program_id(0)
    g = pl.program_id(1)
    i = pl.program_id(2)
    width = C_REP * Q_BLOCK
    slopes = [sl * LOG2E for sl in _slopes(C_HEADS)]
    slope_s = [jnp.where(g == 0, slopes[r], slopes[C_REP + r]) for r in range(C_REP)]

    own_rows = (lax.broadcasted_iota(jnp.int32, (LANES, width), 0) // HEAD_DIM) == g
    q6 = jnp.concatenate([qt_ref[r * HEAD_DIM:(r + 1) * HEAD_DIM, :] for r in range(C_REP)], axis=1)
    qaug[0:LANES, :] = jnp.where(own_rows, jnp.concatenate([q6, q6], axis=0), jnp.zeros((LANES, width), BF16))
    head = lax.broadcasted_iota(jnp.int32, (LANES, width), 1) // Q_BLOCK
    row = lax.broadcasted_iota(jnp.int32, (LANES, width), 0)
    slope_t = jnp.zeros((LANES, width), F32)
    for r in range(C_REP):
        slope_t = jnp.where(head == r, slope_s[r], slope_t)
    s_hi, s_lo = _split(slope_t)
    slope_rows = jnp.where(row == AUG_POS, s_hi.astype(F32), jnp.where(row == AUG_POS + 1, s_lo.astype(F32), 0.0))
    qaug[LANES:, :] = slope_rows.astype(BF16)

    q_loc = lax.broadcasted_iota(jnp.int32, (Q_BLOCK, Q_BLOCK), 1)
    k_loc = lax.broadcasted_iota(jnp.int32, (Q_BLOCK, Q_BLOCK), 0)
    causal = jnp.where(k_loc > q_loc, -MASK_BIG, 0.0)

    m_sc[...] = jnp.full(m_sc.shape, NEG_INF, F32)
    l_sc[...] = jnp.zeros(l_sc.shape, F32)
    acc_sc[...] = jnp.zeros(acc_sc.shape, F32)

    def accumulate(tiles, own_first):
        keys = []
        for u, (jj, _) in enumerate(tiles):
            keys.append(ks_ref[pl.ds(pl.multiple_of(jj * Q_BLOCK, Q_BLOCK), Q_BLOCK), :] + slot_ref[u])
            rows = selm_ref[pl.ds(pl.multiple_of(jj * AUG_ROWS, AUG_ROWS), AUG_ROWS), :]
            lo = LANES + AUG_ROWS * (u + 1)
            qaug[lo:lo + AUG_ROWS, :] = jnp.concatenate([rows] * C_REP, axis=1)
        st = _dot(jnp.concatenate(keys, axis=0), qaug[...])
        ps, alphas = [], []
        for r in range(C_REP):
            sl = slice(r * Q_BLOCK, (r + 1) * Q_BLOCK)
            m_old = m_sc[:, sl]
            m_new = m_old
            parts = []
            for u, (jj, extra) in enumerate(tiles):
                s = st[u * Q_BLOCK:(u + 1) * Q_BLOCK, sl]
                if own_first and u == 0:
                    s = s + causal
                c = slope_s[r] * ((jj - i) * Q_BLOCK).astype(F32) + extra
                m_new = jnp.maximum(m_new, jnp.max(s, axis=0, keepdims=True) + c)
                parts.append((s, c))
            alpha = jnp.exp2(m_old - m_new)
            l_new = alpha * l_sc[:, sl]
            p_rows = []
            for s, c in parts:
                p = jnp.exp2(s + (c - m_new))
                l_new = l_new + jnp.sum(p, axis=0, keepdims=True)
                p_rows.append(p.astype(BF16))
            l_sc[:, sl] = l_new
            m_sc[:, sl] = m_new
            ps.append(jnp.concatenate(p_rows, axis=0))
            alphas.append(alpha)
        values = jnp.concatenate([vst_ref[jj] for jj, _ in tiles], axis=1)
        pv = _dot(values, jnp.concatenate(ps, axis=1))
        acc_sc[...] = jnp.concatenate(alphas, axis=1) * acc_sc[...] + pv

    base = ((bb * pl.num_programs(1) + g) * nq + i) * stride
    count = list_ref[base]

    def listed(slot):
        return list_ref[base + 1 + slot], jnp.where(slot < count, 0.0, -MASK_BIG)

    accumulate([(i, 0.0)] + [listed(u) for u in range(SLC_GROUP - 1)], True)

    def step(k, carry):
        accumulate([listed(SLC_GROUP - 1 + SLC_GROUP * k + u) for u in range(SLC_GROUP)], False)
        return carry

    rest = jnp.maximum(count - (SLC_GROUP - 1), 0)
    full = rest // SLC_GROUP
    lax.fori_loop(0, full, step, 0)
    left = rest - full * SLC_GROUP
    first_left = SLC_GROUP - 1 + full * SLC_GROUP
    for size in range(1, SLC_GROUP):
        @pl.when(left == size)
        def _(size=size):
            accumulate([listed(first_left + u) for u in range(size)], False)
    l = l_sc[...]
    o = acc_sc[...] / jnp.where(l > 0, l, 1.0)
    o_ref[...] = _to_natural([o[:, r * Q_BLOCK:(r + 1) * Q_BLOCK] for r in range(C_REP)], eye_ref[...])


def _slc_attention(lists, qt, ks, vst, sel, *, stride):
    b, _, s = qt.shape
    g = C_KV_HEADS
    nq = s // Q_BLOCK
    rows = C_REP * HEAD_DIM
    width = C_REP * Q_BLOCK
    eye = jnp.asarray(np.eye(LANES), BF16)
    assert AUG_ROWS * (SLC_GROUP + 1) <= LANES
    slots = jnp.asarray(_slot_pattern(), BF16)
    grid_spec = pltpu.PrefetchScalarGridSpec(
        num_scalar_prefetch=1,
        grid=(b, g, nq),
        in_specs=[
            pl.BlockSpec((None, rows, Q_BLOCK), lambda bb, gg, i, bits: (bb, gg, i)),
            pl.BlockSpec((None, s, 2 * LANES), lambda bb, gg, i, bits: (bb, 0, 0)),
            pl.BlockSpec((None, nq, None, HEAD_DIM, Q_BLOCK), lambda bb, gg, i, bits: (bb, 0, gg, 0, 0)),
            pl.BlockSpec((None, None, None, nq * AUG_ROWS, Q_BLOCK), lambda bb, gg, i, bits: (bb, gg, i, 0, 0)),
            pl.BlockSpec((LANES, LANES), lambda bb, gg, i, bits: (0, 0)),
            pl.BlockSpec(slots.shape, lambda bb, gg, i, bits: (0, 0, 0)),
        ],
        out_specs=pl.BlockSpec((Q_BLOCK, rows), lambda bb, gg, i, bits: (bb * nq + i, gg)),
        scratch_shapes=[
            pltpu.VMEM((2 * LANES, width), BF16),
            pltpu.VMEM((1, width), F32),
            pltpu.VMEM((1, width), F32),
            pltpu.VMEM((HEAD_DIM, width), F32),
        ],
    )
    return pl.pallas_call(
        functools.partial(_slc_body, nq=nq, stride=stride),
        grid_spec=grid_spec,
        out_shape=jax.ShapeDtypeStruct((b * s, g * rows), F32),
        compiler_params=_cparams(("parallel", "parallel", "parallel")),
    )(lists, qt, ks, vst, sel, eye, slots)


def _merge_body(x_ref, g_ref, oa0, la0, oa1, la1, oa2, la2, ob_ref, ocmp_ref, oslc_ref, owin_ref, cg_ref,
                p4t_ref, p16t_ref, ex_ref, wg0_ref, wg1_ref, wg2_ref, wa_ref, wb_ref, wc_ref,
                out_ref, h_ref, oall_ref):
    @pl.when(pl.program_id(1) == 0)
    def _():
        def natural(ref, pt_ref):
            hi, lo = _split(ref[...].reshape(TM, A_OUT))
            return _dot(pt_ref[...], hi) + _dot(pt_ref[...], lo)

        o1, l1 = natural(oa1, p4t_ref), natural(la1, p4t_ref)
        o2, l2 = natural(oa2, p16t_ref), natural(la2, p16t_ref)
        cg_split = jnp.concatenate(_split(cg_ref[...]), axis=1)
        gate_c = [_dot(cg_split, ex_ref[w]) for w in range(3)]

        h_ref[...] = _rms_rows(x_ref[...], g_ref[...])
        o0, l0 = oa0[...], la0[...]
        mx = jnp.maximum(jnp.maximum(l0, l1), l2)
        e0, e1, e2 = jnp.exp(l0 - mx), jnp.exp(l1 - mx), jnp.exp(l2 - mx)
        oall_ref[:, 0:A_OUT] = ((e0 * o0 + e1 * o1 + e2 * o2) / (e0 + e1 + e2)).astype(BF16)
        oall_ref[:, A_OUT:A_OUT + B_HEADS * HEAD_DIM] = ob_ref[...].astype(BF16)
        o_c = gate_c[0] * ocmp_ref[...] + gate_c[1] * oslc_ref[...] + gate_c[2] * owin_ref[...]
        oall_ref[:, A_OUT + B_HEADS * HEAD_DIM:] = o_c.astype(BF16)

    h = h_ref[...]
    c0, c1 = A_OUT, A_OUT + B_HEADS * HEAD_DIM
    merged = jax.nn.sigmoid(_dot(h, wg0_ref[...])) * _dot(oall_ref[:, 0:c0], wa_ref[...])
    merged += jax.nn.sigmoid(_dot(h, wg1_ref[...])) * _dot(oall_ref[:, c0:c1], wb_ref[...])
    merged += jax.nn.sigmoid(_dot(h, wg2_ref[...])) * _dot(oall_ref[:, c1:], wc_ref[...])
    out_ref[...] = merged.astype(BF16)


def _merge(x, g, a_outs, ob, ocmp, oslc, owin, cg, w_gate, wa, wb, wc, ex, *, tn=512):
    t, d = x.shape
    per16 = CHUNK16 // TM
    n_t = d // tn
    (oa0, la0), (oa1, la1), (oa2, la2) = a_outs

    last = t // TM - 1

    def tile(i, n, ahead):
        return jnp.minimum(i + jnp.where(n >= ahead, 1, 0), last)

    def rows(a, ahead):
        return pl.BlockSpec((TM, a.shape[1]), lambda i, n: (tile(i, n, ahead), 0))

    a1_spec = pl.BlockSpec((None, 4, Q_BLOCK, A_OUT), lambda i, n: (tile(i, n, 1), 0, 0, 0))
    a2_spec = pl.BlockSpec((None, 16, TM // 16, A_OUT),
                           lambda i, n: (tile(i, n, 1) // per16, 0, tile(i, n, 1) % per16, 0))
    p4t = jnp.asarray(_deinterleave(TM, 4).T, BF16)
    p16t = jnp.asarray(_deinterleave(TM, 16).T, BF16)
    in_specs = [
        rows(x, 1), _resident((1, d)),
        rows(oa0, 1), rows(la0, 1), a1_spec, a1_spec, a2_spec, a2_spec,
        rows(ob, 3), rows(ocmp, 2), rows(oslc, 2), rows(owin, 2), rows(cg, 3),
        _resident((TM, TM)), _resident((TM, TM)), _resident(ex.shape),
        pl.BlockSpec((d, tn), lambda i, n: (0, n)),
        pl.BlockSpec((d, tn), lambda i, n: (0, n + n_t)),
        pl.BlockSpec((d, tn), lambda i, n: (0, n + 2 * n_t)),
        pl.BlockSpec((wa.shape[0], tn), lambda i, n: (0, n)),
        pl.BlockSpec((wb.shape[0], tn), lambda i, n: (0, n)),
        pl.BlockSpec((wc.shape[0], tn), lambda i, n: (0, n)),
    ]
    return pl.pallas_call(
        _merge_body,
        grid=(t // TM, n_t),
        in_specs=in_specs,
        out_specs=pl.BlockSpec((TM, tn), lambda i, n: (i, n)),
        out_shape=jax.ShapeDtypeStruct((t, d), BF16),
        scratch_shapes=[pltpu.VMEM((TM, d), BF16), pltpu.VMEM((TM, wa.shape[0] + wb.shape[0] + wc.shape[0]), BF16)],
        compiler_params=_cparams(("parallel", "arbitrary")),
    )(x, g.reshape(1, d), oa0, la0, oa1, la1, oa2, la2, ob, ocmp, oslc, owin, cg, p4t, p16t, ex,
      w_gate, w_gate, w_gate, wa, wb, wc)


def _out_body(x_ref, m_ref, w_ref, o_ref):
    o_ref[...] = x_ref[...] + _dot(m_ref[...], w_ref[...])


def _out_proj(x, merged, w_out):
    t, d = x.shape
    rows = pl.BlockSpec((TM, d), lambda i: (i, 0))
    return pl.pallas_call(
        _out_body,
        grid=(t // TM,),
        in_specs=[rows, rows, _resident((d, d))],
        out_specs=rows,
        out_shape=jax.ShapeDtypeStruct((t, d), F32),
        compiler_params=_cparams(("parallel",)),
    )(x, merged, w_out)


def _qkv_column_params(qk_gain):
    flag, gain, scale = [], [], []
    one = jnp.ones((HEAD_DIM,), F32)

    def add(n_heads, normed, is_q, gvec, units=1.0):
        for _ in range(n_heads):
            flag.append(np.full((HEAD_DIM,), 1.0 if normed else 0.0, np.float32))
            gain.append(gvec if normed else one)
            scale.append(np.full((HEAD_DIM,), units * HEAD_DIM ** -0.5 if is_q else 1.0, np.float32))

    for _ in range(len(A_GROUPS)):
        add(A_HEADS_PER_GROUP, True, True, qk_gain[0, 0])
        add(A_HEADS_PER_GROUP, True, False, qk_gain[0, 1])
        add(A_HEADS_PER_GROUP, False, False, one)
    add(B_HEADS, True, True, qk_gain[1, 0])
    add(B_KV_HEADS, True, False, qk_gain[1, 1])
    add(B_KV_HEADS, False, False, one)
    add(C_HEADS, True, True, qk_gain[2, 0], units=LOG2E)
    for normed in (False, False, True, False, True, False):
        add(C_KV_HEADS, normed, False, qk_gain[2, 1])
    flag = np.concatenate(flag)
    assert flag.shape[0] == QKV_COLS
    return jnp.asarray(flag), jnp.concatenate(gain) * jnp.asarray(np.concatenate(scale))


def _overlap_t(n_slc, n_pad, n_cmp):
    n = np.arange(n_pad)[None, :]
    j = np.arange(n_slc)[:, None]
    start, end = CMP_STRIDE * n, CMP_STRIDE * n + CMP_BLOCK - 1
    ov = (start <= SLC_BLOCK * j + SLC_BLOCK - 1) & (end >= SLC_BLOCK * j) & (n < n_cmp)
    return jnp.asarray(ov, BF16)


def _gate_expand():
    ex = np.zeros((3, LANES, C_HEADS * HEAD_DIM), np.float32)
    for w in range(3):
        for h in range(C_HEADS):
            ex[w, h * 3 + w, h * HEAD_DIM:(h + 1) * HEAD_DIM] = 1.0
    return jnp.asarray(np.concatenate([ex, ex], axis=1), BF16)


def _compress_weights(cmp_pos, cmp_w1, cmp_w2):
    n_q = 2 * C_KV_HEADS
    w1 = cmp_w1.reshape(2, 2, CMP_STRIDE, HEAD_DIM, CMP_HIDDEN)
    w1q = jnp.repeat(w1, C_KV_HEADS, axis=0)
    wb = jnp.einsum("qhcdn,qp->hcqdpn", w1q, jnp.eye(n_q, dtype=F32))
    wb = wb.reshape(2, CMP_STRIDE, n_q * HEAD_DIM, n_q * CMP_HIDDEN).astype(BF16)
    pos = cmp_pos.reshape(2, 2, CMP_STRIDE, HEAD_DIM)
    prow = jnp.repeat(pos, C_KV_HEADS, axis=0).transpose(1, 2, 0, 3).reshape(2, CMP_STRIDE, 1, n_q * HEAD_DIM)
    prow = jnp.broadcast_to(prow, (2, CMP_STRIDE, 8, n_q * HEAD_DIM)).astype(BF16)
    eye_g = jnp.eye(C_KV_HEADS, dtype=F32)
    w2k = jnp.kron(eye_g, cmp_w2[0]).astype(BF16)
    w2vt = jnp.kron(eye_g, cmp_w2[1]).T.astype(BF16)
    return wb, prow, w2k, w2vt


def _token_mixing(x, b, s, mix_norm, w_in, qk_gain, sinks, cmp_pos, cmp_w1, cmp_w2, w_a, w_b, w_c):
    t, d = x.shape
    assert s % CHUNK16 == 0 and d % 512 == 0
    c_gate_cols = 3 * C_HEADS
    w_qkv = w_in[:, :QKV_COLS + LANES].astype(BF16)
    flag, gs = _qkv_column_params(qk_gain)
    a0, a1, a2, bsec, cq, ckv, cmpd, qt, ks, vst, cg = _qkv_proj(x, mix_norm, w_qkv, flag, gs, b, s)

    a_outs = [_dilated_group(a0.reshape(b, s, SEC), 0, b, s), _dilated_group(a1, 1, b, s),
              _dilated_group(a2, 2, b, s)]
    a_outs[0] = tuple(v.reshape(t, A_OUT) for v in a_outs[0])
    o_b = _sink_swa(bsec.reshape(b, s, SEC), sinks.astype(F32), b, s).reshape(t, -1)
    o_win = _nsa_window(cq.reshape(b, s, SEC), ckv.reshape(b, s, SEC), b, s).reshape(t, -1)

    n_chunks = s // CMP_STRIDE
    n_cmp = (s - CMP_BLOCK) // CMP_STRIDE + 1
    n_slc = s // SLC_BLOCK
    nq = s // Q_BLOCK
    kg = jnp.tile(qk_gain[2, 1], C_KV_HEADS).reshape(1, LANES)
    kc, vct = _compress(cmpd, *_compress_weights(cmp_pos, cmp_w1, cmp_w2), kg)
    o_cmp, sel, cnt = _cmp_select(qt, kc, vct, _overlap_t(n_slc, n_chunks, n_cmp), n_cmp=n_cmp)

    act = (cnt[:, :, :, 0, :] > 0).reshape(b, C_KV_HEADS, nq, nq, 2).any(axis=-1)
    act = act & (jnp.arange(nq)[None, :] < jnp.arange(nq)[:, None])
    order = jnp.argsort(jnp.logical_not(act), axis=-1, stable=True).astype(jnp.int32)
    count = jnp.sum(act, axis=-1, dtype=jnp.int32)[..., None]
    lists = jnp.concatenate([count, order] + [jnp.zeros_like(count)] * (SLC_GROUP - 1), axis=-1)
    o_slc = _slc_attention(lists.reshape(-1), qt, ks.reshape(b, s, 2 * LANES),
                           vst.reshape(b, nq, C_KV_HEADS, HEAD_DIM, Q_BLOCK), sel, stride=nq + SLC_GROUP)

    return _merge(x, mix_norm, a_outs, o_b, o_cmp, o_slc, o_win, cg,
                  w_in[:, QKV_COLS + c_gate_cols:].astype(BF16), w_a.astype(BF16), w_b.astype(BF16),
                  w_c.astype(BF16), _gate_expand())


def kernel(x, ffn1_norm, ffn1_w_gu, ffn1_w_down, mix_norm, w_in, qk_gain, sinks, cmp_pos, cmp_w1, cmp_w2,
           w_branch_a, w_branch_b, w_branch_c, w_out, ffn2_norm, ffn2_w_gu, ffn2_w_down):
    b, s, d = x.shape
    h = x.reshape(b * s, d)
    w1_gu, w1_down = ffn1_w_gu.astype(BF16), ffn1_w_down.astype(BF16)
    w2_gu, w2_down = ffn2_w_gu.astype(BF16), ffn2_w_down.astype(BF16)
    for l in range(ffn1_norm.shape[0]):
        h = _ffn(h, ffn1_norm[l], w1_gu, w1_down, l)
        merged = _token_mixing(h, b, s, mix_norm[l], w_in[l], qk_gain[l], sinks[l], cmp_pos[l], cmp_w1[l],
                               cmp_w2[l], w_branch_a[l], w_branch_b[l], w_branch_c[l])
        h = _out_proj(h, merged, w_out[l].astype(BF16))
        h = _ffn(h, ffn2_norm[l], w2_gu, w2_down, l)
    return h.reshape(b, s, d)
```

```python
import functools
import math

import numpy as np
import jax
import jax.numpy as jnp
from jax import lax
from jax.experimental import pallas as pl
from jax.experimental.pallas import tpu as pltpu

F32 = jnp.float32
BF16 = jnp.bfloat16

HEAD_DIM = 64
Q_BLOCK = 128
LANES = 128
A_GROUPS = ((128, 1), (512, 4), (2048, 16))
A_HEADS_PER_GROUP = 4
A_HEADS = 12
A_OUT = A_HEADS_PER_GROUP * HEAD_DIM
B_HEADS = 8
B_KV_HEADS = 2
B_WINDOW = 128
C_HEADS = 12
C_KV_HEADS = 2
C_REP = C_HEADS // C_KV_HEADS
CMP_BLOCK = 32
CMP_STRIDE = 16
CMP_HIDDEN = 256
SLC_BLOCK = 64
SLC_TOPK = 16
C_WINDOW = 512
RMS_EPS = 1e-6
NEG_INF = -1e30
SEC = 768
N_SEC = 6
QKV_COLS = SEC * N_SEC
TM = 512
CHUNK16 = Q_BLOCK * 16
VMEM_LIMIT = 56 * 1024 * 1024


def _slopes(n):
    return [float(2.0 ** (-8.0 * (h + 1) / n)) for h in range(n)]


def _cparams(sem):
    return pltpu.CompilerParams(dimension_semantics=sem, vmem_limit_bytes=VMEM_LIMIT)


def _dot(a, b):
    return jnp.dot(a, b, preferred_element_type=F32)


def _nt_dot(a, b):
    return lax.dot_general(a, b, (((1,), (1,)), ((), ())), preferred_element_type=F32)


def _split(v):
    hi = v.astype(BF16)
    return hi, (v - hi.astype(F32)).astype(BF16)


def _resident(shape):
    return pl.BlockSpec(shape, lambda *_: (0,) * len(shape), pipeline_mode=pl.Buffered(1))


def _rms_rows(x, g):
    ms = jnp.mean(x * x, axis=-1, keepdims=True)
    return (x * lax.rsqrt(ms + RMS_EPS) * g).astype(BF16)


def _deinterleave(n, d):
    p = np.zeros((n, n), np.float32)
    r = np.arange(n // d)
    for c in range(d):
        p[c * (n // d) + r, d * r + c] = 1.0
    return p


AUG_POS = 0
AUG_ROWS = 16
MASK_BIG = 1e30
SLC_GROUP = 7
LOG2E = math.log2(math.e)


def _key_pattern(n):
    pat = np.zeros((n, LANES), np.float32)
    pat[:, AUG_POS] = pat[:, AUG_POS + 1] = np.arange(n) % Q_BLOCK
    return pat


def _slot_pattern():
    pat = np.zeros((SLC_GROUP, Q_BLOCK, 2 * LANES), np.float32)
    r = np.arange(Q_BLOCK)
    for u in range(SLC_GROUP):
        pat[u, :, LANES + AUG_ROWS * (u + 1)] = r < SLC_BLOCK
        pat[u, :, LANES + AUG_ROWS * (u + 1) + 1] = r >= SLC_BLOCK
    return pat


def _block_spread(n_tiles):
    m = np.zeros((n_tiles * AUG_ROWS, 2 * n_tiles), np.float32)
    jj = np.arange(n_tiles)
    for e in range(2):
        m[AUG_ROWS * jj + e, 2 * jj + e] = 1.0
    return m


FFN_NORM_CHUNKS = 4


def _ffn_body(x_ref, g_ref, wg_ref, wu_ref, wd_ref, o_ref, h_ref, *, n_f):
    f = pl.program_id(1)

    def contribution(h):
        gate = _dot(h, wg_ref[...])
        up = _dot(h, wu_ref[...])
        act = (gate * jax.nn.sigmoid(gate) * up).astype(BF16)
        return _dot(act, wd_ref[...])

    @pl.when(f == 0)
    def _():
        chunk = x_ref.shape[0] // FFN_NORM_CHUNKS
        for c in range(FFN_NORM_CHUNKS):
            rows = slice(c * chunk, (c + 1) * chunk)
            h = _rms_rows(x_ref[rows, :], g_ref[...])
            h_ref[rows, :] = h
            o_ref[rows, :] = contribution(h)

    @pl.when(f > 0)
    def _():
        o_ref[...] += contribution(h_ref[...])

    @pl.when(f == n_f - 1)
    def _():
        o_ref[...] = x_ref[...] + 0.5 * o_ref[...]


def _ffn(x, g, w_gu, w_down, layer, *, tm=1024, tf=512):
    t, d = x.shape
    d_ff = w_down.shape[1]
    n_f = d_ff // tf
    rows = pl.BlockSpec((tm, d), lambda i, f: (i, 0))
    return pl.pallas_call(
        functools.partial(_ffn_body, n_f=n_f),
        grid=(t // tm, n_f),
        in_specs=[
            rows,
            pl.BlockSpec((1, d), lambda i, f: (0, 0)),
            pl.BlockSpec((None, d, tf), lambda i, f: (layer, 0, f)),
            pl.BlockSpec((None, d, tf), lambda i, f: (layer, 0, f + n_f)),
            pl.BlockSpec((None, tf, d), lambda i, f: (layer, f, 0)),
        ],
        out_specs=rows,
        out_shape=jax.ShapeDtypeStruct((t, d), F32),
        scratch_shapes=[pltpu.VMEM((tm, d), BF16)],
        compiler_params=_cparams(("parallel", "arbitrary")),
    )(x, g.reshape(1, d), w_gu, w_gu, w_down)


NORM_TILE = 256
NORM_TILES = {0: (0, 1), 1: (0, 1), 2: (0, 1), 3: (0, 1, 2), 4: (0, 1, 2), 5: (1, 2)}


def _head_sumsq(y, bd):
    return _dot((y * y).astype(BF16), bd)


def _qkv_body(x_ref, g_ref, w_ref, flag_ref, gs_ref, bd_ref, p4_ref, p16_ref, eye_ref, kpat_ref,
              a0_ref, a1_ref, a2_ref, b_ref, cq_ref, ckv_ref, cmpd_ref, qt_ref, ks_ref, vst_ref, cg_ref):
    h = _rms_rows(x_ref[...], g_ref[...])
    bd = bd_ref[...]

    def project(k):
        if k < len(A_GROUPS):
            return jnp.concatenate([_dot(h, w_ref[:, part * A_HEADS * HEAD_DIM + k * A_OUT:
                                                     part * A_HEADS * HEAD_DIM + (k + 1) * A_OUT])
                                    for part in range(3)], axis=1)
        return _dot(h, w_ref[:, k * SEC:(k + 1) * SEC])

    def finish(k, y):
        tiles = []
        for c in range(SEC // NORM_TILE):
            yc = y[:, c * NORM_TILE:(c + 1) * NORM_TILE]
            if c in NORM_TILES[k]:
                cols = slice(k * SEC + c * NORM_TILE, k * SEC + (c + 1) * NORM_TILE)
                inv = lax.rsqrt(_head_sumsq(yc, bd) * (1.0 / HEAD_DIM) + RMS_EPS)
                yc = yc * jnp.where(flag_ref[:, cols] > 0, inv, 1.0) * gs_ref[:, cols]
            tiles.append(yc.astype(BF16))
        return jnp.concatenate(tiles, axis=1)

    sec = []
    for first in range(0, N_SEC, 3):
        raw = [project(k) for k in range(first, first + 3)]
        sec += [finish(first + k, y) for k, y in enumerate(raw)]
    cg_logits = _dot(h, w_ref[:, QKV_COLS:QKV_COLS + LANES])

    a0_ref[...] = sec[0]
    a1_ref[...] = _dot(p4_ref[...], sec[1]).astype(BF16).reshape(a1_ref.shape)
    a2_ref[...] = _dot(p16_ref[...], sec[2]).astype(BF16).reshape(a2_ref.shape)
    b_ref[...] = sec[3]
    y_cq = sec[4]
    cq_ref[...] = y_cq
    qt_ref[...] = _nt_dot(eye_ref[...], y_cq).astype(BF16)
    y_ckv = sec[5]
    ckv_ref[...] = y_ckv
    cmpd_ref[...] = _dot(p16_ref[...], y_ckv[:, 0:2 * LANES]).astype(BF16).reshape(cmpd_ref.shape)
    ks_ref[:, 0:LANES] = y_ckv[:, 2 * LANES:3 * LANES]
    ks_ref[:, LANES:2 * LANES] = kpat_ref[...]
    eye = eye_ref[0:LANES, 0:LANES]
    for kb in range(vst_ref.shape[0]):
        vt = _nt_dot(eye, y_ckv[kb * Q_BLOCK:(kb + 1) * Q_BLOCK, 3 * LANES:4 * LANES]).astype(BF16)
        for gg in range(C_KV_HEADS):
            vst_ref[kb, gg] = vt[gg * HEAD_DIM:(gg + 1) * HEAD_DIM]
    cg_ref[...] = jax.nn.sigmoid(cg_logits)


def _qkv_proj(x, g, w, flag, gs, b, s):
    t, d = x.shape
    tiles_per_batch = s // TM
    per16 = CHUNK16 // TM
    bd = jnp.asarray(np.kron(np.eye(NORM_TILE // HEAD_DIM), np.ones((HEAD_DIM, HEAD_DIM))), BF16)
    p4 = jnp.asarray(_deinterleave(TM, 4), BF16)
    p16 = jnp.asarray(_deinterleave(TM, 16), BF16)
    eye = jnp.asarray(np.eye(SEC), BF16)
    nat = pl.BlockSpec((TM, SEC), lambda i: (i, 0))
    out_specs = [
        nat,
        pl.BlockSpec((None, 4, Q_BLOCK, SEC), lambda i: (i, 0, 0, 0)),
        pl.BlockSpec((None, 16, TM // 16, SEC), lambda i: (i // per16, 0, i % per16, 0)),
        nat, nat, nat,
        pl.BlockSpec((None, 16, TM // 16, 2 * LANES), lambda i: (i // tiles_per_batch, 0, i % tiles_per_batch, 0)),
        pl.BlockSpec((None, SEC, TM), lambda i: (i // tiles_per_batch, 0, i % tiles_per_batch)),
        pl.BlockSpec((TM, 2 * LANES), lambda i: (i, 0)),
        pl.BlockSpec((TM // Q_BLOCK, C_KV_HEADS, HEAD_DIM, Q_BLOCK), lambda i: (i, 0, 0, 0)),
        pl.BlockSpec((TM, LANES), lambda i: (i, 0)),
    ]
    out_shape = [
        jax.ShapeDtypeStruct((t, SEC), BF16),
        jax.ShapeDtypeStruct((t // TM, 4, Q_BLOCK, SEC), BF16),
        jax.ShapeDtypeStruct((t // CHUNK16, 16, Q_BLOCK, SEC), BF16),
        jax.ShapeDtypeStruct((t, SEC), BF16),
        jax.ShapeDtypeStruct((t, SEC), BF16),
        jax.ShapeDtypeStruct((t, SEC), BF16),
        jax.ShapeDtypeStruct((b, 16, s // 16, 2 * LANES), BF16),
        jax.ShapeDtypeStruct((b, SEC, s), BF16),
        jax.ShapeDtypeStruct((t, 2 * LANES), BF16),
        jax.ShapeDtypeStruct((t // Q_BLOCK, C_KV_HEADS, HEAD_DIM, Q_BLOCK), BF16),
        jax.ShapeDtypeStruct((t, LANES), F32),
    ]
    n_w = w.shape[1]
    return pl.pallas_call(
        _qkv_body,
        grid=(t // TM,),
        in_specs=[
            pl.BlockSpec((TM, d), lambda i: (i, 0)),
            _resident((1, d)),
            _resident((d, n_w)),
            _resident((1, QKV_COLS)),
            _resident((1, QKV_COLS)),
            _resident((NORM_TILE, NORM_TILE)),
            _resident((TM, TM)),
            _resident((TM, TM)),
            _resident((SEC, SEC)),
            _resident((TM, LANES)),
        ],
        out_specs=out_specs,
        out_shape=out_shape,
        compiler_params=_cparams(("parallel",)),
    )(x, g.reshape(1, d), w, flag.reshape(1, -1), gs.reshape(1, -1), bd, p4, p16, eye,
      jnp.asarray(_key_pattern(TM), BF16))


def _banded_body(*refs, nb, qb, keys, heads, k_off, v_off, n_pairs, q_axis, use_sinks, with_lse, stack, fold,
                 log2_units):
    refs = list(refs)
    q_ref = refs.pop(0)
    kv_refs = [refs.pop(0) for _ in range({"self": 1, "window": 2, "blocks": nb + 1}[keys])]
    qc_ref, kaug_ref, band_ref = refs.pop(0), refs.pop(0), refs.pop(0)
    sink_ref = refs.pop(0) if use_sinks else None
    o_ref = refs.pop(0)
    lse_ref = refs.pop(0) if with_lse else None

    i = pl.program_id(q_axis)
    nk = (nb + 1) * Q_BLOCK
    col = lax.broadcasted_iota(jnp.int32, (Q_BLOCK, nk), 1)
    band = band_ref[...]

    def start_mask(sub):
        if sub > 0 and sub >= nb:
            return band
        return band + jnp.where(col < (nb - (i * qb + sub)) * Q_BLOCK, -MASK_BIG, 0.0)

    rel_f = (nb * Q_BLOCK + lax.broadcasted_iota(jnp.int32, (Q_BLOCK, nk), 0) - col).astype(F32)
    lane = lax.broadcasted_iota(jnp.int32, (Q_BLOCK, LANES), 1)
    low_half = lane < HEAD_DIM
    kaug = kaug_ref[...]

    def q_cols(sub, c0):
        if len(q_ref.shape) == 3:
            return q_ref[sub, :, c0:c0 + LANES]
        return q_ref[sub * Q_BLOCK:(sub + 1) * Q_BLOCK, c0:c0 + LANES]

    kv_cache = {}

    def swap_halves(tile, swapped):
        return pltpu.roll(tile.astype(F32), HEAD_DIM, 1).astype(BF16) if swapped else tile

    def kv_tile(sub, off, kv_pair, swapped):
        c0 = off + kv_pair * LANES
        if keys == "window":
            key = (off, kv_pair, swapped)
            if key not in kv_cache:
                kv_cache[key] = swap_halves(jnp.concatenate([r[:, c0:c0 + LANES] for r in kv_refs], axis=0), swapped)
            first = qb - nb + sub
            return kv_cache[key][first * Q_BLOCK:(first + nb + 1) * Q_BLOCK]
        key = (sub, off, kv_pair, swapped)
        if key not in kv_cache:
            if keys == "self":
                blocks = [kv_refs[0][:, c0:c0 + LANES] if sub == 0 else q_cols(sub - 1, c0), q_cols(sub, c0)]
            else:
                blocks = [r[:, c0:c0 + LANES] for r in kv_refs]
            kv_cache[key] = swap_halves(jnp.concatenate(blocks, axis=0), swapped)
        return kv_cache[key]

    classes = {}
    for head in heads:
        pair, half, kv_pair, kv_half, slope, hidx = head
        key = (kv_pair, kv_half != half) if stack else hidx
        classes.setdefault(key, []).append(head)

    outs = [[[None, None] for _ in range(n_pairs)] for _ in range(qb)]
    lses = [[[None, None] for _ in range(n_pairs)] for _ in range(qb)]
    groups = [(sub, members) for sub in range(qb) for members in classes.values()]

    scores = []
    for sub, members in groups:
        n_h = len(members)
        kv_pair, swapped = members[0][2], members[0][3] != members[0][1]
        mask = start_mask(sub)
        q_rows = []
        for pair, half, _, _, _, hidx in members:
            qp = q_cols(sub, pair * LANES)
            own = low_half if half == 0 else jnp.logical_not(low_half)
            qm = jnp.where(own, qp, jnp.zeros_like(qp))
            q_rows.append(jnp.concatenate([qm, qc_ref[hidx]], axis=1) if fold else qm)
        if fold:
            k_aug = jnp.concatenate([kv_tile(sub, k_off, kv_pair, swapped), kaug], axis=1)
            s = _nt_dot(jnp.concatenate(q_rows, axis=0), k_aug)
            s = (s.reshape(n_h, Q_BLOCK, nk) + mask[None]).reshape(n_h * Q_BLOCK, nk)
        else:
            bias = jnp.concatenate([mask - member[4] * rel_f for member in members], axis=0)
            s = _nt_dot(jnp.concatenate(q_rows, axis=0), kv_tile(sub, k_off, kv_pair, swapped)) + bias
        scores.append(s)

    probs = []
    for (_, members), s in zip(groups, scores):
        m = jnp.max(s, axis=1, keepdims=True)
        if use_sinks:
            assert len(members) == 1
            sink = sink_ref[members[0][5]]
            m = jnp.maximum(m, sink)
        p = jnp.exp2(s - m) if log2_units else jnp.exp(s - m)
        den = jnp.sum(p, axis=1, keepdims=True)
        if use_sinks:
            den = den + jnp.exp(sink - m)
        probs.append((p.astype(BF16), m, den))

    for (sub, members), (p, m, den) in zip(groups, probs):
        kv_pair, swapped = members[0][2], members[0][3] != members[0][1]
        r = _dot(p, kv_tile(sub, v_off, kv_pair, swapped)) / den
        lse = m + jnp.log(den) if with_lse else None
        for k, (pair, half, _, _, _, _) in enumerate(members):
            outs[sub][pair][half] = r[k * Q_BLOCK:(k + 1) * Q_BLOCK]
            if with_lse:
                lses[sub][pair][half] = jnp.broadcast_to(lse[k * Q_BLOCK:(k + 1) * Q_BLOCK], (Q_BLOCK, LANES))

    def store(ref, sub, sl, value):
        if len(ref.shape) == 3:
            ref[sub, :, sl] = value
        else:
            ref[sub * Q_BLOCK:(sub + 1) * Q_BLOCK, sl] = value

    for sub in range(qb):
        for pair in range(n_pairs):
            sl = slice(pair * LANES, (pair + 1) * LANES)
            store(o_ref, sub, sl, jnp.where(low_half, outs[sub][pair][0], outs[sub][pair][1]))
            if with_lse:
                store(lse_ref, sub, sl, jnp.where(low_half, lses[sub][pair][0], lses[sub][pair][1]))


def _banded_consts(heads, nb, max_dist):
    nk = (nb + 1) * Q_BLOCK
    slope = np.asarray([h[4] for h in heads], np.float32)[:, None]
    q_dist = (nb * Q_BLOCK + np.arange(Q_BLOCK, dtype=np.float32))[None, :]
    ones = np.ones_like(q_dist)
    vals = jnp.asarray(np.stack([slope * ones, slope * Q_BLOCK * ones, -slope * q_dist], axis=-1))
    hi = vals.astype(BF16)
    lo = (vals - hi.astype(F32)).astype(BF16)
    cols = jnp.stack([hi[..., 0], lo[..., 0], hi[..., 1], lo[..., 1], hi[..., 2], lo[..., 2]], axis=-1)
    qc = jnp.pad(cols, ((0, 0), (0, 0), (0, LANES - cols.shape[-1])))
    kaug = np.zeros((nk, LANES), np.float32)
    kaug[:, 0] = kaug[:, 1] = np.arange(nk) % Q_BLOCK
    kaug[:, 2] = kaug[:, 3] = np.arange(nk) // Q_BLOCK
    kaug[:, 4] = kaug[:, 5] = 1.0
    rel = nb * Q_BLOCK + np.arange(Q_BLOCK)[:, None] - np.arange(nk)[None, :]
    band = np.where((rel >= 0) & (rel <= max_dist), 0.0, -MASK_BIG).astype(np.float32)
    return [qc, jnp.asarray(kaug, BF16), jnp.asarray(band)]


BAND_QB = 4
CMP_QB = 1


def _banded_call(q_arr, kv_arr, *, grid, q_spec, kv_specs, out_spec, out_shape, out_cols, nb, qb, max_dist, heads,
                 k_off, v_off, q_axis, sinks=None, with_lse=False, stack=False, fold=False, log2_units=False):
    assert not (log2_units and (with_lse or sinks is not None))
    if len(kv_specs) == 1 and nb == 1:
        keys = "self"
    elif qb > 1:
        keys = "window"
        assert len(kv_specs) == 2 and nb <= qb
    else:
        keys = "blocks"
        assert len(kv_specs) == nb + 1
    consts = _banded_consts(heads, nb, max_dist)
    in_specs = [q_spec] + list(kv_specs) + [pl.BlockSpec(c.shape, lambda *_, nd=c.ndim: (0,) * nd) for c in consts]
    args = [q_arr] + [kv_arr] * len(kv_specs) + consts
    if sinks is not None:
        in_specs.append(pl.BlockSpec(memory_space=pltpu.SMEM))
        args.append(sinks)
    oshape = jax.ShapeDtypeStruct(out_shape, F32)
    body = functools.partial(_banded_body, nb=nb, qb=qb, keys=keys, heads=heads, k_off=k_off, v_off=v_off,
                             n_pairs=out_cols // LANES, q_axis=q_axis, use_sinks=sinks is not None,
                             with_lse=with_lse, stack=stack, fold=fold, log2_units=log2_units)
    return pl.pallas_call(
        body,
        grid=grid,
        in_specs=in_specs,
        out_specs=[out_spec, out_spec] if with_lse else out_spec,
        out_shape=[oshape, oshape] if with_lse else oshape,
        compiler_params=_cparams(("parallel",) * len(grid)),
    )(*args)


def _row_specs(b, s, qb, out_cols):
    q_spec = pl.BlockSpec((None, qb * Q_BLOCK, SEC), lambda bb, i: (bb, i, 0))
    prev = pl.BlockSpec((None, Q_BLOCK, SEC), lambda bb, i: (bb, jnp.maximum(qb * i - 1, 0), 0))
    out_spec = pl.BlockSpec((None, qb * Q_BLOCK, out_cols), lambda bb, i: (bb, i, 0))
    return dict(grid=(b, s // (qb * Q_BLOCK)), q_axis=1, q_spec=q_spec, kv_specs=[prev], out_spec=out_spec,
                out_shape=(b, s, out_cols), qb=qb)


def _dilated_group(arr, gi, b, s):
    window, dil = A_GROUPS[gi]
    slopes = _slopes(A_HEADS)
    heads = tuple((hh // 2, hh % 2, hh // 2, hh % 2, slopes[gi * A_HEADS_PER_GROUP + hh] * dil, hh)
                  for hh in range(A_HEADS_PER_GROUP))
    common = dict(out_cols=A_OUT, nb=1, max_dist=window // dil, heads=heads, k_off=256, v_off=512,
                  with_lse=True, stack=False)
    if dil == 1:
        return _banded_call(arr, arr, **_row_specs(b, s, BAND_QB, A_OUT), **common)
    nc = s // (Q_BLOCK * dil)
    qb = BAND_QB if nc % BAND_QB == 0 else 1
    q_spec = pl.BlockSpec((qb, None, Q_BLOCK, SEC), lambda bb, c, i: ((bb * nc) // qb + i, c, 0, 0))
    prev = pl.BlockSpec((None, None, Q_BLOCK, SEC), lambda bb, c, i: (bb * nc + jnp.maximum(qb * i - 1, 0), c, 0, 0))
    out_spec = pl.BlockSpec((qb, None, Q_BLOCK, A_OUT), lambda bb, c, i: ((bb * nc) // qb + i, c, 0, 0))
    return _banded_call(arr, arr, grid=(b, dil, nc // qb), q_axis=2, q_spec=q_spec, kv_specs=[prev],
                        out_spec=out_spec, out_shape=(b * nc, dil, Q_BLOCK, A_OUT), qb=qb, **common)


def _sink_swa(arr, sinks, b, s):
    slopes = _slopes(B_HEADS)
    rep = B_HEADS // B_KV_HEADS
    heads = tuple((h // 2, h % 2, 0, h // rep, slopes[h], h) for h in range(B_HEADS))
    return _banded_call(arr, arr, **_row_specs(b, s, BAND_QB, B_HEADS * HEAD_DIM), out_cols=B_HEADS * HEAD_DIM,
                        nb=1, max_dist=B_WINDOW - 1, heads=heads, k_off=512, v_off=640, sinks=sinks)


def _nsa_window(cq, ckv, b, s):
    slopes = _slopes(C_HEADS)
    heads = tuple((h // 2, h % 2, 0, h // C_REP, slopes[h] * LOG2E, h) for h in range(C_HEADS))
    nb = -(-(C_WINDOW - 1) // Q_BLOCK)
    qb = nb
    blk = (None, qb * Q_BLOCK, SEC)
    kv_specs = [pl.BlockSpec(blk, lambda bb, i: (bb, jnp.maximum(i - 1, 0), 0)),
                pl.BlockSpec(blk, lambda bb, i: (bb, i, 0))]
    out_cols = C_HEADS * HEAD_DIM
    return _banded_call(
        cq, ckv, grid=(b, s // (qb * Q_BLOCK)), q_axis=1, qb=qb, log2_units=True,
        q_spec=pl.BlockSpec(blk, lambda bb, i: (bb, i, 0)), kv_specs=kv_specs,
        out_spec=pl.BlockSpec((None, qb * Q_BLOCK, out_cols), lambda bb, i: (bb, i, 0)),
        out_shape=(b, s, out_cols), out_cols=out_cols, nb=nb, max_dist=C_WINDOW - 1, heads=heads,
        k_off=512, v_off=640, stack=True, fold=True)


def _compress_body(t_ref, wb_ref, prow_ref, w2k_ref, w2vt_ref, kg_ref, bd_ref, kc_ref, vct_ref, *, n_chunks):
    hid_cols = 2 * C_KV_HEADS * CMP_HIDDEN
    u = jnp.zeros((n_chunks, hid_cols), F32)
    v = jnp.zeros((n_chunks, hid_cols), F32)
    pc = jnp.zeros((1, hid_cols), F32)
    for c in range(CMP_STRIDE):
        tc = t_ref[c]
        u = u + _dot(tc, wb_ref[0, c])
        v = v + _dot(tc, wb_ref[1, c])
        pc = pc + _dot(prow_ref[0, c], wb_ref[0, c])[0:1] + _dot(prow_ref[1, c], wb_ref[1, c])[0:1]
    hsum = u + pltpu.roll(v, n_chunks - 1, 0) + pc
    hid = (hsum * jax.nn.sigmoid(hsum)).astype(BF16)
    half = C_KV_HEADS * CMP_HIDDEN
    k = _dot(hid[:, :half], w2k_ref[...])
    hi, lo = _split(k * k)
    ss = _dot(hi, bd_ref[...]) + _dot(lo, bd_ref[...])
    kc_ref[...] = (k * lax.rsqrt(ss * (1.0 / HEAD_DIM) + RMS_EPS) * kg_ref[...]).astype(BF16)
    vct_ref[...] = _nt_dot(w2vt_ref[...], hid[:, half:]).astype(BF16)


def _compress(cmpd, wb, prow, w2k, w2vt, kg):
    b, _, n_chunks, width = cmpd.shape
    bd = jnp.asarray(np.kron(np.eye(LANES // HEAD_DIM), np.ones((HEAD_DIM, HEAD_DIM))), BF16)
    return pl.pallas_call(
        functools.partial(_compress_body, n_chunks=n_chunks),
        grid=(b,),
        in_specs=[
            pl.BlockSpec((None, CMP_STRIDE, n_chunks, width), lambda bb: (bb, 0, 0, 0)),
            _resident(wb.shape), _resident(prow.shape), _resident(w2k.shape), _resident(w2vt.shape),
            _resident((1, LANES)), _resident((LANES, LANES)),
        ],
        out_specs=[
            pl.BlockSpec((None, n_chunks, LANES), lambda bb: (bb, 0, 0)),
            pl.BlockSpec((None, LANES, n_chunks), lambda bb: (bb, 0, 0)),
        ],
        out_shape=[
            jax.ShapeDtypeStruct((b, n_chunks, LANES), BF16),
            jax.ShapeDtypeStruct((b, LANES, n_chunks), BF16),
        ],
        compiler_params=_cparams(("parallel",)),
    )(cmpd, wb, prow, w2k, w2vt, kg, bd)


def _to_natural(ot_list, eye):
    pairs = []
    for k in range(0, len(ot_list), 2):
        hi, lo = _split(jnp.concatenate([ot_list[k], ot_list[k + 1]], axis=0))
        pairs.append(_nt_dot(eye, hi) + _nt_dot(eye, lo))
    return jnp.concatenate(pairs, axis=1)


def _cmp_body(qt_ref, kc_ref, vct_ref, ovt_ref, eye_ref, spread_ref, cmask_ref,
              o_ref, selm_ref, cnt_ref, sel_sc, *, n_top):
    g = pl.program_id(1)
    i = pl.program_id(2)
    n_pad = kc_ref.shape[0]
    n_slc = ovt_ref.shape[0]
    qw = qt_ref.shape[1]
    qb = qw // Q_BLOCK
    per_q = Q_BLOCK // CMP_STRIDE
    own_rows = (lax.broadcasted_iota(jnp.int32, (LANES, qw), 0) // HEAD_DIM) == g
    slopes = [sl * LOG2E for sl in _slopes(C_HEADS)]

    def attend(rows):
        kc = kc_ref[0:rows, :]
        vct = vct_ref[:, 0:rows]
        mask = jnp.concatenate(
            [cmask_ref[pl.ds(pl.multiple_of(n_pad - per_q * (qb * i + h), 8), rows), :] for h in range(qb)], axis=1)
        n_f = (CMP_STRIDE * lax.broadcasted_iota(jnp.int32, (rows, qw), 0)).astype(F32)
        psum = jnp.zeros((rows, qw), F32)
        outs = []
        for r in range(C_REP):
            slope = jnp.where(g == 0, slopes[r], slopes[C_REP + r])
            qt = qt_ref[r * HEAD_DIM:(r + 1) * HEAD_DIM, :]
            q_pad = jnp.where(own_rows, jnp.concatenate([qt, qt], axis=0), jnp.zeros((LANES, qw), BF16))
            s = _dot(kc, q_pad) + (slope * n_f + mask)
            m = jnp.maximum(jnp.max(s, axis=0, keepdims=True), -1e20)
            e = jnp.exp2(s - m)
            den = jnp.sum(e, axis=0, keepdims=True)
            p = e * (1.0 / jnp.where(den > 0, den, 1.0))
            psum = psum + p
            both = _dot(vct, p.astype(BF16))
            outs.append(jnp.where(g == 0, both[:HEAD_DIM], both[HEAD_DIM:]))
        o_ref[...] = _to_natural(outs, eye_ref[...])

        n_j = rows * CMP_STRIDE // SLC_BLOCK
        hi, lo = _split(psum)
        ovt = ovt_ref[0:n_j, 0:rows]
        imp = _dot(ovt, hi) + _dot(ovt, lo)
        j_idx = lax.broadcasted_iota(jnp.int32, (n_j, qw), 0)
        t_q = i * qw + lax.broadcasted_iota(jnp.int32, (n_j, qw), 1)
        cur = lax.shift_right_logical(t_q, int(math.log2(SLC_BLOCK)))
        forced = ((j_idx == 0) | (j_idx == cur) | (j_idx == cur - 1)) & (j_idx <= cur)
        v = jnp.where((j_idx <= cur) & jnp.logical_not(forced), imp, -1.0)
        sel = jnp.where(forced, 1.0, 0.0)
        for _ in range(n_top - 3):
            m = jnp.max(v, axis=0, keepdims=True)
            first = jnp.min(jnp.where((v == m) & (m > -0.5), j_idx, n_slc), axis=0, keepdims=True)
            pick = j_idx == first
            sel = jnp.where(pick, 1.0, sel)
            v = jnp.where(pick, -1.0, v)
        sel_sc[0:n_j, :] = sel
        if n_j < n_slc:
            sel_sc[n_j:, :] = jnp.zeros((n_slc - n_j, qw), F32)

    n_var = n_pad // LANES
    for var in range(n_var):
        pl.when((qb * i + qb - 1) // (LANES // per_q) == var)(functools.partial(attend, (var + 1) * LANES))

    sel = sel_sc[...]
    neg = jnp.where(sel > 0, 0.0, -MASK_BIG).astype(BF16)
    mask_rows = _dot(spread_ref[...], neg).astype(BF16)
    sel_b = sel.astype(BF16)
    for h in range(qb):
        lanes = slice(h * Q_BLOCK, (h + 1) * Q_BLOCK)
        selm_ref[h] = mask_rows[:, lanes]
        cnt_ref[h] = _nt_dot(jnp.ones((8, Q_BLOCK), BF16), sel_b[:, lanes])


def _cmp_select(qt, kc, vct, ovt, *, n_cmp):
    b, _, s = qt.shape
    g = C_KV_HEADS
    n_pad = kc.shape[1]
    n_slc = ovt.shape[0]
    nq = s // Q_BLOCK
    rows = C_REP * HEAD_DIM
    qb = CMP_QB if nq % CMP_QB == 0 else 1
    qw = qb * Q_BLOCK
    steps = nq // qb
    eye = jnp.asarray(np.eye(qw), BF16)
    n_top = min(SLC_TOPK, n_slc)
    assert n_top > 3 and n_cmp == n_pad - 1 and n_pad % LANES == 0
    n_rel = np.arange(-n_pad, n_pad)[:, None]
    cmask = np.where(CMP_STRIDE * n_rel + CMP_BLOCK - 1 <= np.arange(Q_BLOCK)[None, :], 0.0, -MASK_BIG)
    return pl.pallas_call(
        functools.partial(_cmp_body, n_top=n_top),
        grid=(b, g, steps),
        in_specs=[
            pl.BlockSpec((None, rows, qw), lambda bb, gg, i: (bb, gg, i)),
            pl.BlockSpec((None, n_pad, LANES), lambda bb, gg, i: (bb, 0, 0)),
            pl.BlockSpec((None, LANES, n_pad), lambda bb, gg, i: (bb, 0, 0)),
            pl.BlockSpec((n_slc, n_pad), lambda bb, gg, i: (0, 0)),
            pl.BlockSpec((qw, qw), lambda bb, gg, i: (0, 0)),
            pl.BlockSpec((nq * AUG_ROWS, n_slc), lambda bb, gg, i: (0, 0)),
            pl.BlockSpec((2 * n_pad, Q_BLOCK), lambda bb, gg, i: (0, 0)),
        ],
        out_specs=[
            pl.BlockSpec((qw, rows), lambda bb, gg, i: (bb * steps + i, gg)),
            pl.BlockSpec((None, None, qb, nq * AUG_ROWS, Q_BLOCK), lambda bb, gg, i: (bb, gg, i, 0, 0)),
            pl.BlockSpec((None, None, qb, 8, n_slc), lambda bb, gg, i: (bb, gg, i, 0, 0)),
        ],
        out_shape=[
            jax.ShapeDtypeStruct((b * s, g * rows), F32),
            jax.ShapeDtypeStruct((b, g, nq, nq * AUG_ROWS, Q_BLOCK), BF16),
            jax.ShapeDtypeStruct((b, g, nq, 8, n_slc), F32),
        ],
        scratch_shapes=[pltpu.VMEM((n_slc, qw), F32)],
        compiler_params=_cparams(("parallel", "parallel", "parallel")),
    )(qt, kc, vct, ovt, eye, jnp.asarray(_block_spread(nq), BF16), jnp.asarray(cmask, F32))


def _slc_body(list_ref, qt_ref, ks_ref, vst_ref, selm_ref, eye_ref, slot_ref, o_ref, qaug, m_sc, l_sc, acc_sc,
              *, nq, stride):
    bb = pl.program_id(0)
    g = pl.program_id(1)
    i = pl.program_id(2)
    width = C_REP * Q_BLOCK
    slopes = [sl * LOG2E for sl in _slopes(C_HEADS)]
    slope_s = [jnp.where(g == 0, slopes[r], slopes[C_REP + r]) for r in range(C_REP)]

    own_rows = (lax.broadcasted_iota(jnp.int32, (LANES, width), 0) // HEAD_DIM) == g
    q6 = jnp.concatenate([qt_ref[r * HEAD_DIM:(r + 1) * HEAD_DIM, :] for r in range(C_REP)], axis=1)
    qaug[0:LANES, :] = jnp.where(own_rows, jnp.concatenate([q6, q6], axis=0), jnp.zeros((LANES, width), BF16))
    head = lax.broadcasted_iota(jnp.int32, (LANES, width), 1) // Q_BLOCK
    row = lax.broadcasted_iota(jnp.int32, (LANES, width), 0)
    slope_t = jnp.zeros((LANES, width), F32)
    for r in range(C_REP):
        slope_t = jnp.where(head == r, slope_s[r], slope_t)
    s_hi, s_lo = _split(slope_t)
    slope_rows = jnp.where(row == AUG_POS, s_hi.astype(F32), jnp.where(row == AUG_POS + 1, s_lo.astype(F32), 0.0))
    qaug[LANES:, :] = slope_rows.astype(BF16)

    q_loc = lax.broadcasted_iota(jnp.int32, (Q_BLOCK, Q_BLOCK), 1)
    k_loc = lax.broadcasted_iota(jnp.int32, (Q_BLOCK, Q_BLOCK), 0)
    causal = jnp.where(k_loc > q_loc, -MASK_BIG, 0.0)

    m_sc[...] = jnp.full(m_sc.shape, NEG_INF, F32)
    l_sc[...] = jnp.zeros(l_sc.shape, F32)
    acc_sc[...] = jnp.zeros(acc_sc.shape, F32)

    def accumulate(tiles, own_first):
        keys = []
        for u, (jj, _) in enumerate(tiles):
            keys.append(ks_ref[pl.ds(pl.multiple_of(jj * Q_BLOCK, Q_BLOCK), Q_BLOCK), :] + slot_ref[u])
            rows = selm_ref[pl.ds(pl.multiple_of(jj * AUG_ROWS, AUG_ROWS), AUG_ROWS), :]
            lo = LANES + AUG_ROWS * (u + 1)
            qaug[lo:lo + AUG_ROWS, :] = jnp.concatenate([rows] * C_REP, axis=1)
        st = _dot(jnp.concatenate(keys, axis=0), qaug[...])
        ps, alphas = [], []
        for r in range(C_REP):
            sl = slice(r * Q_BLOCK, (r + 1) * Q_BLOCK)
            m_old = m_sc[:, sl]
            m_new = m_old
            parts = []
            for u, (jj, extra) in enumerate(tiles):
                s = st[u * Q_BLOCK:(u + 1) * Q_BLOCK, sl]
                if own_first and u == 0:
                    s = s + causal
                c = slope_s[r] * ((jj - i) * Q_BLOCK).astype(F32) + extra
                m_new = jnp.maximum(m_new, jnp.max(s, axis=0, keepdims=True) + c)
                parts.append((s, c))
            alpha = jnp.exp2(m_old - m_new)
            l_new = alpha * l_sc[:, sl]
            p_rows = []
            for s, c in parts:
                p = jnp.exp2(s + (c - m_new))
                l_new = l_new + jnp.sum(p, axis=0, keepdims=True)
                p_rows.append(p.astype(BF16))
            l_sc[:, sl] = l_new
            m_sc[:, sl] = m_new
            ps.append(jnp.concatenate(p_rows, axis=0))
            alphas.append(alpha)
        values = jnp.concatenate([vst_ref[jj] for jj, _ in tiles], axis=1)
        pv = _dot(values, jnp.concatenate(ps, axis=1))
        acc_sc[...] = jnp.concatenate(alphas, axis=1) * acc_sc[...] + pv

    base = ((bb * pl.num_programs(1) + g) * nq + i) * stride
    count = list_ref[base]

    def listed(slot):
        return list_ref[base + 1 + slot], jnp.where(slot < count, 0.0, -MASK_BIG)

    accumulate([(i, 0.0)] + [listed(u) for u in range(SLC_GROUP - 1)], True)

    def step(k, carry):
        accumulate([listed(SLC_GROUP - 1 + SLC_GROUP * k + u) for u in range(SLC_GROUP)], False)
        return carry

    rest = jnp.maximum(count - (SLC_GROUP - 1), 0)
    full = rest // SLC_GROUP
    lax.fori_loop(0, full, step, 0)
    left = rest - full * SLC_GROUP
    first_left = SLC_GROUP - 1 + full * SLC_GROUP
    for size in range(1, SLC_GROUP):
        @pl.when(left == size)
        def _(size=size):
            accumulate([listed(first_left + u) for u in range(size)], False)
    l = l_sc[...]
    o = acc_sc[...] / jnp.where(l > 0, l, 1.0)
    o_ref[...] = _to_natural([o[:, r * Q_BLOCK:(r + 1) * Q_BLOCK] for r in range(C_REP)], eye_ref[...])


def _slc_attention(lists, qt, ks, vst, sel, *, stride):
    b, _, s = qt.shape
    g = C_KV_HEADS
    nq = s // Q_BLOCK
    rows = C_REP * HEAD_DIM
    width = C_REP * Q_BLOCK
    eye = jnp.asarray(np.eye(LANES), BF16)
    assert AUG_ROWS * (SLC_GROUP + 1) <= LANES
    slots = jnp.asarray(_slot_pattern(), BF16)
    grid_spec = pltpu.PrefetchScalarGridSpec(
        num_scalar_prefetch=1,
        grid=(b, g, nq),
        in_specs=[
            pl.BlockSpec((None, rows, Q_BLOCK), lambda bb, gg, i, bits: (bb, gg, i)),
            pl.BlockSpec((None, s, 2 * LANES), lambda bb, gg, i, bits: (bb, 0, 0)),
            pl.BlockSpec((None, nq, None, HEAD_DIM, Q_BLOCK), lambda bb, gg, i, bits: (bb, 0, gg, 0, 0)),
            pl.BlockSpec((None, None, None, nq * AUG_ROWS, Q_BLOCK), lambda bb, gg, i, bits: (bb, gg, i, 0, 0)),
            pl.BlockSpec((LANES, LANES), lambda bb, gg, i, bits: (0, 0)),
            pl.BlockSpec(slots.shape, lambda bb, gg, i, bits: (0, 0, 0)),
        ],
        out_specs=pl.BlockSpec((Q_BLOCK, rows), lambda bb, gg, i, bits: (bb * nq + i, gg)),
        scratch_shapes=[
            pltpu.VMEM((2 * LANES, width), BF16),
            pltpu.VMEM((1, width), F32),
            pltpu.VMEM((1, width), F32),
            pltpu.VMEM((HEAD_DIM, width), F32),
        ],
    )
    return pl.pallas_call(
        functools.partial(_slc_body, nq=nq, stride=stride),
        grid_spec=grid_spec,
        out_shape=jax.ShapeDtypeStruct((b * s, g * rows), F32),
        compiler_params=_cparams(("parallel", "parallel", "parallel")),
    )(lists, qt, ks, vst, sel, eye, slots)


def _merge_body(x_ref, g_ref, oa0, la0, oa1, la1, oa2, la2, ob_ref, ocmp_ref, oslc_ref, owin_ref, cg_ref,
                p4t_ref, p16t_ref, ex_ref, wg0_ref, wg1_ref, wg2_ref, wa_ref, wb_ref, wc_ref,
                out_ref, h_ref, oall_ref):
    @pl.when(pl.program_id(1) == 0)
    def _():
        def natural(ref, pt_ref):
            hi, lo = _split(ref[...].reshape(TM, A_OUT))
            return _dot(pt_ref[...], hi) + _dot(pt_ref[...], lo)

        o1, l1 = natural(oa1, p4t_ref), natural(la1, p4t_ref)
        o2, l2 = natural(oa2, p16t_ref), natural(la2, p16t_ref)
        cg_split = jnp.concatenate(_split(cg_ref[...]), axis=1)
        gate_c = [_dot(cg_split, ex_ref[w]) for w in range(3)]

        h_ref[...] = _rms_rows(x_ref[...], g_ref[...])
        o0, l0 = oa0[...], la0[...]
        mx = jnp.maximum(jnp.maximum(l0, l1), l2)
        e0, e1, e2 = jnp.exp(l0 - mx), jnp.exp(l1 - mx), jnp.exp(l2 - mx)
        oall_ref[:, 0:A_OUT] = ((e0 * o0 + e1 * o1 + e2 * o2) / (e0 + e1 + e2)).astype(BF16)
        oall_ref[:, A_OUT:A_OUT + B_HEADS * HEAD_DIM] = ob_ref[...].astype(BF16)
        o_c = gate_c[0] * ocmp_ref[...] + gate_c[1] * oslc_ref[...] + gate_c[2] * owin_ref[...]
        oall_ref[:, A_OUT + B_HEADS * HEAD_DIM:] = o_c.astype(BF16)

    h = h_ref[...]
    c0, c1 = A_OUT, A_OUT + B_HEADS * HEAD_DIM
    merged = jax.nn.sigmoid(_dot(h, wg0_ref[...])) * _dot(oall_ref[:, 0:c0], wa_ref[...])
    merged += jax.nn.sigmoid(_dot(h, wg1_ref[...])) * _dot(oall_ref[:, c0:c1], wb_ref[...])
    merged += jax.nn.sigmoid(_dot(h, wg2_ref[...])) * _dot(oall_ref[:, c1:], wc_ref[...])
    out_ref[...] = merged.astype(BF16)


def _merge(x, g, a_outs, ob, ocmp, oslc, owin, cg, w_gate, wa, wb, wc, ex, *, tn=512):
    t, d = x.shape
    per16 = CHUNK16 // TM
    n_t = d // tn
    (oa0, la0), (oa1, la1), (oa2, la2) = a_outs

    last = t // TM - 1

    def tile(i, n, ahead):
        return jnp.minimum(i + jnp.where(n >= ahead, 1, 0), last)

    def rows(a, ahead):
        return pl.BlockSpec((TM, a.shape[1]), lambda i, n: (tile(i, n, ahead), 0))

    a1_spec = pl.BlockSpec((None, 4, Q_BLOCK, A_OUT), lambda i, n: (tile(i, n, 1), 0, 0, 0))
    a2_spec = pl.BlockSpec((None, 16, TM // 16, A_OUT),
                           lambda i, n: (tile(i, n, 1) // per16, 0, tile(i, n, 1) % per16, 0))
    p4t = jnp.asarray(_deinterleave(TM, 4).T, BF16)
    p16t = jnp.asarray(_deinterleave(TM, 16).T, BF16)
    in_specs = [
        rows(x, 1), _resident((1, d)),
        rows(oa0, 1), rows(la0, 1), a1_spec, a1_spec, a2_spec, a2_spec,
        rows(ob, 3), rows(ocmp, 2), rows(oslc, 2), rows(owin, 2), rows(cg, 3),
        _resident((TM, TM)), _resident((TM, TM)), _resident(ex.shape),
        pl.BlockSpec((d, tn), lambda i, n: (0, n)),
        pl.BlockSpec((d, tn), lambda i, n: (0, n + n_t)),
        pl.BlockSpec((d, tn), lambda i, n: (0, n + 2 * n_t)),
        pl.BlockSpec((wa.shape[0], tn), lambda i, n: (0, n)),
        pl.BlockSpec((wb.shape[0], tn), lambda i, n: (0, n)),
        pl.BlockSpec((wc.shape[0], tn), lambda i, n: (0, n)),
    ]
    return pl.pallas_call(
        _merge_body,
        grid=(t // TM, n_t),
        in_specs=in_specs,
        out_specs=pl.BlockSpec((TM, tn), lambda i, n: (i, n)),
        out_shape=jax.ShapeDtypeStruct((t, d), BF16),
        scratch_shapes=[pltpu.VMEM((TM, d), BF16), pltpu.VMEM((TM, wa.shape[0] + wb.shape[0] + wc.shape[0]), BF16)],
        compiler_params=_cparams(("parallel", "arbitrary")),
    )(x, g.reshape(1, d), oa0, la0, oa1, la1, oa2, la2, ob, ocmp, oslc, owin, cg, p4t, p16t, ex,
      w_gate, w_gate, w_gate, wa, wb, wc)


def _out_body(x_ref, m_ref, w_ref, o_ref):
    o_ref[...] = x_ref[...] + _dot(m_ref[...], w_ref[...])


def _out_proj(x, merged, w_out):
    t, d = x.shape
    rows = pl.BlockSpec((TM, d), lambda i: (i, 0))
    return pl.pallas_call(
        _out_body,
        grid=(t // TM,),
        in_specs=[rows, rows, _resident((d, d))],
        out_specs=rows,
        out_shape=jax.ShapeDtypeStruct((t, d), F32),
        compiler_params=_cparams(("parallel",)),
    )(x, merged, w_out)


def _qkv_column_params(qk_gain):
    flag, gain, scale = [], [], []
    one = jnp.ones((HEAD_DIM,), F32)

    def add(n_heads, normed, is_q, gvec, units=1.0):
        for _ in range(n_heads):
            flag.append(np.full((HEAD_DIM,), 1.0 if normed else 0.0, np.float32))
            gain.append(gvec if normed else one)
            scale.append(np.full((HEAD_DIM,), units * HEAD_DIM ** -0.5 if is_q else 1.0, np.float32))

    for _ in range(len(A_GROUPS)):
        add(A_HEADS_PER_GROUP, True, True, qk_gain[0, 0])
        add(A_HEADS_PER_GROUP, True, False, qk_gain[0, 1])
        add(A_HEADS_PER_GROUP, False, False, one)
    add(B_HEADS, True, True, qk_gain[1, 0])
    add(B_KV_HEADS, True, False, qk_gain[1, 1])
    add(B_KV_HEADS, False, False, one)
    add(C_HEADS, True, True, qk_gain[2, 0], units=LOG2E)
    for normed in (False, False, True, False, True, False):
        add(C_KV_HEADS, normed, False, qk_gain[2, 1])
    flag = np.concatenate(flag)
    assert flag.shape[0] == QKV_COLS
    return jnp.asarray(flag), jnp.concatenate(gain) * jnp.asarray(np.concatenate(scale))


def _overlap_t(n_slc, n_pad, n_cmp):
    n = np.arange(n_pad)[None, :]
    j = np.arange(n_slc)[:, None]
    start, end = CMP_STRIDE * n, CMP_STRIDE * n + CMP_BLOCK - 1
    ov = (start <= SLC_BLOCK * j + SLC_BLOCK - 1) & (end >= SLC_BLOCK * j) & (n < n_cmp)
    return jnp.asarray(ov, BF16)


def _gate_expand():
    ex = np.zeros((3, LANES, C_HEADS * HEAD_DIM), np.float32)
    for w in range(3):
        for h in range(C_HEADS):
            ex[w, h * 3 + w, h * HEAD_DIM:(h + 1) * HEAD_DIM] = 1.0
    return jnp.asarray(np.concatenate([ex, ex], axis=1), BF16)


def _compress_weights(cmp_pos, cmp_w1, cmp_w2):
    n_q = 2 * C_KV_HEADS
    w1 = cmp_w1.reshape(2, 2, CMP_STRIDE, HEAD_DIM, CMP_HIDDEN)
    w1q = jnp.repeat(w1, C_KV_HEADS, axis=0)
    wb = jnp.einsum("qhcdn,qp->hcqdpn", w1q, jnp.eye(n_q, dtype=F32))
    wb = wb.reshape(2, CMP_STRIDE, n_q * HEAD_DIM, n_q * CMP_HIDDEN).astype(BF16)
    pos = cmp_pos.reshape(2, 2, CMP_STRIDE, HEAD_DIM)
    prow = jnp.repeat(pos, C_KV_HEADS, axis=0).transpose(1, 2, 0, 3).reshape(2, CMP_STRIDE, 1, n_q * HEAD_DIM)
    prow = jnp.broadcast_to(prow, (2, CMP_STRIDE, 8, n_q * HEAD_DIM)).astype(BF16)
    eye_g = jnp.eye(C_KV_HEADS, dtype=F32)
    w2k = jnp.kron(eye_g, cmp_w2[0]).astype(BF16)
    w2vt = jnp.kron(eye_g, cmp_w2[1]).T.astype(BF16)
    return wb, prow, w2k, w2vt


def _token_mixing(x, b, s, mix_norm, w_in, qk_gain, sinks, cmp_pos, cmp_w1, cmp_w2, w_a, w_b, w_c):
    t, d = x.shape
    assert s % CHUNK16 == 0 and d % 512 == 0
    c_gate_cols = 3 * C_HEADS
    w_qkv = w_in[:, :QKV_COLS + LANES].astype(BF16)
    flag, gs = _qkv_column_params(qk_gain)
    a0, a1, a2, bsec, cq, ckv, cmpd, qt, ks, vst, cg = _qkv_proj(x, mix_norm, w_qkv, flag, gs, b, s)

    a_outs = [_dilated_group(a0.reshape(b, s, SEC), 0, b, s), _dilated_group(a1, 1, b, s),
              _dilated_group(a2, 2, b, s)]
    a_outs[0] = tuple(v.reshape(t, A_OUT) for v in a_outs[0])
    o_b = _sink_swa(bsec.reshape(b, s, SEC), sinks.astype(F32), b, s).reshape(t, -1)
    o_win = _nsa_window(cq.reshape(b, s, SEC), ckv.reshape(b, s, SEC), b, s).reshape(t, -1)

    n_chunks = s // CMP_STRIDE
    n_cmp = (s - CMP_BLOCK) // CMP_STRIDE + 1
    n_slc = s // SLC_BLOCK
    nq = s // Q_BLOCK
    kg = jnp.tile(qk_gain[2, 1], C_KV_HEADS).reshape(1, LANES)
    kc, vct = _compress(cmpd, *_compress_weights(cmp_pos, cmp_w1, cmp_w2), kg)
    o_cmp, sel, cnt = _cmp_select(qt, kc, vct, _overlap_t(n_slc, n_chunks, n_cmp), n_cmp=n_cmp)

    act = (cnt[:, :, :, 0, :] > 0).reshape(b, C_KV_HEADS, nq, nq, 2).any(axis=-1)
    act = act & (jnp.arange(nq)[None, :] < jnp.arange(nq)[:, None])
    order = jnp.argsort(jnp.logical_not(act), axis=-1, stable=True).astype(jnp.int32)
    count = jnp.sum(act, axis=-1, dtype=jnp.int32)[..., None]
    lists = jnp.concatenate([count, order] + [jnp.zeros_like(count)] * (SLC_GROUP - 1), axis=-1)
    o_slc = _slc_attention(lists.reshape(-1), qt, ks.reshape(b, s, 2 * LANES),
                           vst.reshape(b, nq, C_KV_HEADS, HEAD_DIM, Q_BLOCK), sel, stride=nq + SLC_GROUP)

    return _merge(x, mix_norm, a_outs, o_b, o_cmp, o_slc, o_win, cg,
                  w_in[:, QKV_COLS + c_gate_cols:].astype(BF16), w_a.astype(BF16), w_b.astype(BF16),
                  w_c.astype(BF16), _gate_expand())


def kernel(x, ffn1_norm, ffn1_w_gu, ffn1_w_down, mix_norm, w_in, qk_gain, sinks, cmp_pos, cmp_w1, cmp_w2,
           w_branch_a, w_branch_b, w_branch_c, w_out, ffn2_norm, ffn2_w_gu, ffn2_w_down):
    b, s, d = x.shape
    h = x.reshape(b * s, d)
    w1_gu, w1_down = ffn1_w_gu.astype(BF16), ffn1_w_down.astype(BF16)
    w2_gu, w2_down = ffn2_w_gu.astype(BF16), ffn2_w_down.astype(BF16)
    for l in range(ffn1_norm.shape[0]):
        h = _ffn(h, ffn1_norm[l], w1_gu, w1_down, l)
        merged = _token_mixing(h, b, s, mix_norm[l], w_in[l], qk_gain[l], sinks[l], cmp_pos[l], cmp_w1[l],
                               cmp_w2[l], w_branch_a[l], w_branch_b[l], w_branch_c[l])
        h = _out_proj(h, merged, w_out[l].astype(BF16))
        h = _ffn(h, ffn2_norm[l], w2_gu, w2_down, l)
    return h.reshape(b, s, d)
```

```python
import functools
import math

import numpy as np
import jax
import jax.numpy as jnp
from jax import lax
from jax.experimental import pallas as pl
from jax.experimental.pallas import tpu as pltpu

F32 = jnp.float32
BF16 = jnp.bfloat16

HEAD_DIM = 64
Q_BLOCK = 128
LANES = 128
A_GROUPS = ((128, 1), (512, 4), (2048, 16))
A_HEADS_PER_GROUP = 4
A_HEADS = 12
A_OUT = A_HEADS_PER_GROUP * HEAD_DIM
B_HEADS = 8
B_KV_HEADS = 2
B_WINDOW = 128
C_HEADS = 12
C_KV_HEADS = 2
C_REP = C_HEADS // C_KV_HEADS
CMP_BLOCK = 32
CMP_STRIDE = 16
CMP_HIDDEN = 256
SLC_BLOCK = 64
SLC_TOPK = 16
C_WINDOW = 512
RMS_EPS = 1e-6
NEG_INF = -1e30
SEC = 768
N_SEC = 6
QKV_COLS = SEC * N_SEC
TM = 512
CHUNK16 = Q_BLOCK * 16
VMEM_LIMIT = 56 * 1024 * 1024


def _slopes(n):
    return [float(2.0 ** (-8.0 * (h + 1) / n)) for h in range(n)]


def _cparams(sem):
    return pltpu.CompilerParams(dimension_semantics=sem, vmem_limit_bytes=VMEM_LIMIT)


def _dot(a, b):
    return jnp.dot(a, b, preferred_element_type=F32)


def _nt_dot(a, b):
    return lax.dot_general(a, b, (((1,), (1,)), ((), ())), preferred_element_type=F32)


def _split(v):
    hi = v.astype(BF16)
    return hi, (v - hi.astype(F32)).astype(BF16)


def _resident(shape):
    return pl.BlockSpec(shape, lambda *_: (0,) * len(shape), pipeline_mode=pl.Buffered(1))


def _rms_rows(x, g):
    ms = jnp.mean(x * x, axis=-1, keepdims=True)
    return (x * lax.rsqrt(ms + RMS_EPS) * g).astype(BF16)


def _deinterleave(n, d):
    p = np.zeros((n, n), np.float32)
    r = np.arange(n // d)
    for c in range(d):
        p[c * (n // d) + r, d * r + c] = 1.0
    return p


AUG_POS = 0
AUG_ROWS = 16
MASK_BIG = 1e30
SLC_GROUP = 7
LOG2E = math.log2(math.e)


def _key_pattern(n):
    pat = np.zeros((n, LANES), np.float32)
    pat[:, AUG_POS] = pat[:, AUG_POS + 1] = np.arange(n) % Q_BLOCK
    return pat


def _slot_pattern():
    pat = np.zeros((SLC_GROUP, Q_BLOCK, 2 * LANES), np.float32)
    r = np.arange(Q_BLOCK)
    for u in range(SLC_GROUP):
        pat[u, :, LANES + AUG_ROWS * (u + 1)] = r < SLC_BLOCK
        pat[u, :, LANES + AUG_ROWS * (u + 1) + 1] = r >= SLC_BLOCK
    return pat


def _block_spread(n_tiles):
    m = np.zeros((n_tiles * AUG_ROWS, 2 * n_tiles), np.float32)
    jj = np.arange(n_tiles)
    for e in range(2):
        m[AUG_ROWS * jj + e, 2 * jj + e] = 1.0
    return m


FFN_NORM_CHUNKS = 4


def _ffn_body(x_ref, g_ref, wg_ref, wu_ref, wd_ref, o_ref, h_ref, *, n_f):
    f = pl.program_id(1)

    def contribution(h):
        gate = _dot(h, wg_ref[...])
        up = _dot(h, wu_ref[...])
        act = (gate * jax.nn.sigmoid(gate) * up).astype(BF16)
        return _dot(act, wd_ref[...])

    @pl.when(f == 0)
    def _():
        chunk = x_ref.shape[0] // FFN_NORM_CHUNKS
        for c in range(FFN_NORM_CHUNKS):
            rows = slice(c * chunk, (c + 1) * chunk)
            h = _rms_rows(x_ref[rows, :], g_ref[...])
            h_ref[rows, :] = h
            o_ref[rows, :] = contribution(h)

    @pl.when((f > 0) & (f < n_f - 1))
    def _():
        o_ref[...] += contribution(h_ref[...])

    @pl.when(f == n_f - 1)
    def _():
        o_ref[...] = x_ref[...] + 0.5 * (o_ref[...] + contribution(h_ref[...]))


def _ffn(x, g, w_gu, w_down, layer, *, tm=1024, tf=512):
    t, d = x.shape
    d_ff = w_down.shape[1]
    n_f = d_ff // tf
    assert n_f >= 2
    rows = pl.BlockSpec((tm, d), lambda i, f: (i, 0))
    return pl.pallas_call(
        functools.partial(_ffn_body, n_f=n_f),
        grid=(t // tm, n_f),
        in_specs=[
            rows,
            pl.BlockSpec((1, d), lambda i, f: (0, 0)),
            pl.BlockSpec((None, d, tf), lambda i, f: (layer, 0, f)),
            pl.BlockSpec((None, d, tf), lambda i, f: (layer, 0, f + n_f)),
            pl.BlockSpec((None, tf, d), lambda i, f: (layer, f, 0)),
        ],
        out_specs=rows,
        out_shape=jax.ShapeDtypeStruct((t, d), F32),
        scratch_shapes=[pltpu.VMEM((tm, d), BF16)],
        compiler_params=_cparams(("parallel", "arbitrary")),
    )(x, g.reshape(1, d), w_gu, w_gu, w_down)


NORM_TILE = 256
NORM_TILES = {0: (0, 1), 1: (0, 1), 2: (0, 1), 3: (0, 1, 2), 4: (0, 1, 2), 5: (1, 2)}


def _head_sumsq(y, bd):
    return _dot((y * y).astype(BF16), bd)


def _qkv_body(x_ref, g_ref, w_ref, flag_ref, gs_ref, bd_ref, p4_ref, p16_ref, eye_ref, kpat_ref,
              a0_ref, a1_ref, a2_ref, b_ref, cq_ref, ckv_ref, cmpd_ref, qt_ref, ks_ref, vst_ref, cg_ref):
    h = _rms_rows(x_ref[...], g_ref[...])
    bd = bd_ref[...]

    def project(k):
        if k < len(A_GROUPS):
            return jnp.concatenate([_dot(h, w_ref[:, part * A_HEADS * HEAD_DIM + k * A_OUT:
                                                     part * A_HEADS * HEAD_DIM + (k + 1) * A_OUT])
                                    for part in range(3)], axis=1)
        return _dot(h, w_ref[:, k * SEC:(k + 1) * SEC])

    def finish(k, y):
        tiles = []
        for c in range(SEC // NORM_TILE):
            yc = y[:, c * NORM_TILE:(c + 1) * NORM_TILE]
            if c in NORM_TILES[k]:
                cols = slice(k * SEC + c * NORM_TILE, k * SEC + (c + 1) * NORM_TILE)
                inv = lax.rsqrt(_head_sumsq(yc, bd) * (1.0 / HEAD_DIM) + RMS_EPS)
                yc = yc * jnp.where(flag_ref[:, cols] > 0, inv, 1.0) * gs_ref[:, cols]
            tiles.append(yc.astype(BF16))
        return jnp.concatenate(tiles, axis=1)

    sec = []
    for first in range(0, N_SEC, 3):
        raw = [project(k) for k in range(first, first + 3)]
        sec += [finish(first + k, y) for k, y in enumerate(raw)]
    cg_logits = _dot(h, w_ref[:, QKV_COLS:QKV_COLS + LANES])

    a0_ref[...] = sec[0]
    a1_ref[...] = _dot(p4_ref[...], sec[1]).astype(BF16).reshape(a1_ref.shape)
    a2_ref[...] = _dot(p16_ref[...], sec[2]).astype(BF16).reshape(a2_ref.shape)
    b_ref[...] = sec[3]
    y_cq = sec[4]
    cq_ref[...] = y_cq
    qt_ref[...] = _nt_dot(eye_ref[...], y_cq).astype(BF16)
    y_ckv = sec[5]
    ckv_ref[...] = y_ckv
    cmpd_ref[...] = _dot(p16_ref[...], y_ckv[:, 0:2 * LANES]).astype(BF16).reshape(cmpd_ref.shape)
    ks_ref[:, 0:LANES] = y_ckv[:, 2 * LANES:3 * LANES]
    ks_ref[:, LANES:2 * LANES] = kpat_ref[...]
    eye = eye_ref[0:LANES, 0:LANES]
    for kb in range(vst_ref.shape[0]):
        vt = _nt_dot(eye, y_ckv[kb * Q_BLOCK:(kb + 1) * Q_BLOCK, 3 * LANES:4 * LANES]).astype(BF16)
        for gg in range(C_KV_HEADS):
            vst_ref[kb, gg] = vt[gg * HEAD_DIM:(gg + 1) * HEAD_DIM]
    cg_ref[...] = jax.nn.sigmoid(cg_logits)


def _qkv_proj(x, g, w, flag, gs, b, s):
    t, d = x.shape
    tiles_per_batch = s // TM
    per16 = CHUNK16 // TM
    bd = jnp.asarray(np.kron(np.eye(NORM_TILE // HEAD_DIM), np.ones((HEAD_DIM, HEAD_DIM))), BF16)
    p4 = jnp.asarray(_deinterleave(TM, 4), BF16)
    p16 = jnp.asarray(_deinterleave(TM, 16), BF16)
    eye = jnp.asarray(np.eye(SEC), BF16)
    nat = pl.BlockSpec((TM, SEC), lambda i: (i, 0))
    out_specs = [
        nat,
        pl.BlockSpec((None, 4, Q_BLOCK, SEC), lambda i: (i, 0, 0, 0)),
        pl.BlockSpec((None, 16, TM // 16, SEC), lambda i: (i // per16, 0, i % per16, 0)),
        nat, nat, nat,
        pl.BlockSpec((None, 16, TM // 16, 2 * LANES), lambda i: (i // tiles_per_batch, 0, i % tiles_per_batch, 0)),
        pl.BlockSpec((None, SEC, TM), lambda i: (i // tiles_per_batch, 0, i % tiles_per_batch)),
        pl.BlockSpec((TM, 2 * LANES), lambda i: (i, 0)),
        pl.BlockSpec((TM // Q_BLOCK, C_KV_HEADS, HEAD_DIM, Q_BLOCK), lambda i: (i, 0, 0, 0)),
        pl.BlockSpec((TM, LANES), lambda i: (i, 0)),
    ]
    out_shape = [
        jax.ShapeDtypeStruct((t, SEC), BF16),
        jax.ShapeDtypeStruct((t // TM, 4, Q_BLOCK, SEC), BF16),
        jax.ShapeDtypeStruct((t // CHUNK16, 16, Q_BLOCK, SEC), BF16),
        jax.ShapeDtypeStruct((t, SEC), BF16),
        jax.ShapeDtypeStruct((t, SEC), BF16),
        jax.ShapeDtypeStruct((t, SEC), BF16),
        jax.ShapeDtypeStruct((b, 16, s // 16, 2 * LANES), BF16),
        jax.ShapeDtypeStruct((b, SEC, s), BF16),
        jax.ShapeDtypeStruct((t, 2 * LANES), BF16),
        jax.ShapeDtypeStruct((t // Q_BLOCK, C_KV_HEADS, HEAD_DIM, Q_BLOCK), BF16),
        jax.ShapeDtypeStruct((t, LANES), F32),
    ]
    n_w = w.shape[1]
    return pl.pallas_call(
        _qkv_body,
        grid=(t // TM,),
        in_specs=[
            pl.BlockSpec((TM, d), lambda i: (i, 0)),
            _resident((1, d)),
            _resident((d, n_w)),
            _resident((1, QKV_COLS)),
            _resident((1, QKV_COLS)),
            _resident((NORM_TILE, NORM_TILE)),
            _resident((TM, TM)),
            _resident((TM, TM)),
            _resident((SEC, SEC)),
            _resident((TM, LANES)),
        ],
        out_specs=out_specs,
        out_shape=out_shape,
        compiler_params=_cparams(("parallel",)),
    )(x, g.reshape(1, d), w, flag.reshape(1, -1), gs.reshape(1, -1), bd, p4, p16, eye,
      jnp.asarray(_key_pattern(TM), BF16))


def _banded_body(*refs, nb, qb, keys, heads, k_off, v_off, n_pairs, q_axis, use_sinks, with_lse, stack, fold,
                 log2_units):
    refs = list(refs)
    q_ref = refs.pop(0)
    kv_refs = [refs.pop(0) for _ in range({"self": 1, "window": 2, "blocks": nb + 1}[keys])]
    qc_ref, kaug_ref, band_ref = refs.pop(0), refs.pop(0), refs.pop(0)
    sink_ref = refs.pop(0) if use_sinks else None
    o_ref = refs.pop(0)
    lse_ref = refs.pop(0) if with_lse else None

    i = pl.program_id(q_axis)
    nk = (nb + 1) * Q_BLOCK
    col = lax.broadcasted_iota(jnp.int32, (Q_BLOCK, nk), 1)
    band = band_ref[...]

    def start_mask(sub):
        if sub > 0 and sub >= nb:
            return band
        return band + jnp.where(col < (nb - (i * qb + sub)) * Q_BLOCK, -MASK_BIG, 0.0)

    rel_f = (nb * Q_BLOCK + lax.broadcasted_iota(jnp.int32, (Q_BLOCK, nk), 0) - col).astype(F32)
    lane = lax.broadcasted_iota(jnp.int32, (Q_BLOCK, LANES), 1)
    low_half = lane < HEAD_DIM
    kaug = kaug_ref[...]

    def q_cols(sub, c0):
        if len(q_ref.shape) == 3:
            return q_ref[sub, :, c0:c0 + LANES]
        return q_ref[sub * Q_BLOCK:(sub + 1) * Q_BLOCK, c0:c0 + LANES]

    kv_cache = {}

    def swap_halves(tile, swapped):
        return pltpu.roll(tile.astype(F32), HEAD_DIM, 1).astype(BF16) if swapped else tile

    def kv_tile(sub, off, kv_pair, swapped):
        c0 = off + kv_pair * LANES
        if keys == "window":
            key = (off, kv_pair, swapped)
            if key not in kv_cache:
                kv_cache[key] = swap_halves(jnp.concatenate([r[:, c0:c0 + LANES] for r in kv_refs], axis=0), swapped)
            first = qb - nb + sub
            return kv_cache[key][first * Q_BLOCK:(first + nb + 1) * Q_BLOCK]
        key = (sub, off, kv_pair, swapped)
        if key not in kv_cache:
            if keys == "self":
                blocks = [kv_refs[0][:, c0:c0 + LANES] if sub == 0 else q_cols(sub - 1, c0), q_cols(sub, c0)]
            else:
                blocks = [r[:, c0:c0 + LANES] for r in kv_refs]
            kv_cache[key] = swap_halves(jnp.concatenate(blocks, axis=0), swapped)
        return kv_cache[key]

    classes = {}
    for head in heads:
        pair, half, kv_pair, kv_half, slope, hidx = head
        key = (kv_pair, kv_half != half) if stack else hidx
        classes.setdefault(key, []).append(head)

    outs = [[[None, None] for _ in range(n_pairs)] for _ in range(qb)]
    lses = [[[None, None] for _ in range(n_pairs)] for _ in range(qb)]
    groups = [(sub, members) for sub in range(qb) for members in classes.values()]

    scores = []
    for sub, members in groups:
        n_h = len(members)
        kv_pair, swapped = members[0][2], members[0][3] != members[0][1]
        mask = start_mask(sub)
        q_rows = []
        for pair, half, _, _, _, hidx in members:
            qp = q_cols(sub, pair * LANES)
            own = low_half if half == 0 else jnp.logical_not(low_half)
            qm = jnp.where(own, qp, jnp.zeros_like(qp))
            q_rows.append(jnp.concatenate([qm, qc_ref[hidx]], axis=1) if fold else qm)
        if fold:
            k_aug = jnp.concatenate([kv_tile(sub, k_off, kv_pair, swapped), kaug], axis=1)
            s = _nt_dot(jnp.concatenate(q_rows, axis=0), k_aug)
            s = (s.reshape(n_h, Q_BLOCK, nk) + mask[None]).reshape(n_h * Q_BLOCK, nk)
        else:
            bias = jnp.concatenate([mask - member[4] * rel_f for member in members], axis=0)
            s = _nt_dot(jnp.concatenate(q_rows, axis=0), kv_tile(sub, k_off, kv_pair, swapped)) + bias
        scores.append(s)

    probs = []
    for (_, members), s in zip(groups, scores):
        m = jnp.max(s, axis=1, keepdims=True)
        if use_sinks:
            assert len(members) == 1
            sink = sink_ref[members[0][5]]
            m = jnp.maximum(m, sink)
        p = jnp.exp2(s - m) if log2_units else jnp.exp(s - m)
        den = jnp.sum(p, axis=1, keepdims=True)
        if use_sinks:
            den = den + jnp.exp(sink - m)
        probs.append((p.astype(BF16), m, den))

    for (sub, members), (p, m, den) in zip(groups, probs):
        kv_pair, swapped = members[0][2], members[0][3] != members[0][1]
        r = _dot(p, kv_tile(sub, v_off, kv_pair, swapped)) / den
        lse = m + jnp.log(den) if with_lse else None
        for k, (pair, half, _, _, _, _) in enumerate(members):
            outs[sub][pair][half] = r[k * Q_BLOCK:(k + 1) * Q_BLOCK]
            if with_lse:
                lses[sub][pair][half] = jnp.broadcast_to(lse[k * Q_BLOCK:(k + 1) * Q_BLOCK], (Q_BLOCK, LANES))

    def store(ref, sub, sl, value):
        if len(ref.shape) == 3:
            ref[sub, :, sl] = value
        else:
            ref[sub * Q_BLOCK:(sub + 1) * Q_BLOCK, sl] = value

    for sub in range(qb):
        for pair in range(n_pairs):
            sl = slice(pair * LANES, (pair + 1) * LANES)
            store(o_ref, sub, sl, jnp.where(low_half, outs[sub][pair][0], outs[sub][pair][1]))
            if with_lse:
                store(lse_ref, sub, sl, jnp.where(low_half, lses[sub][pair][0], lses[sub][pair][1]))


def _banded_consts(heads, nb, max_dist):
    nk = (nb + 1) * Q_BLOCK
    slope = np.asarray([h[4] for h in heads], np.float32)[:, None]
    q_dist = (nb * Q_BLOCK + np.arange(Q_BLOCK, dtype=np.float32))[None, :]
    ones = np.ones_like(q_dist)
    vals = jnp.asarray(np.stack([slope * ones, slope * Q_BLOCK * ones, -slope * q_dist], axis=-1))
    hi = vals.astype(BF16)
    lo = (vals - hi.astype(F32)).astype(BF16)
    cols = jnp.stack([hi[..., 0], lo[..., 0], hi[..., 1], lo[..., 1], hi[..., 2], lo[..., 2]], axis=-1)
    qc = jnp.pad(cols, ((0, 0), (0, 0), (0, LANES - cols.shape[-1])))
    kaug = np.zeros((nk, LANES), np.float32)
    kaug[:, 0] = kaug[:, 1] = np.arange(nk) % Q_BLOCK
    kaug[:, 2] = kaug[:, 3] = np.arange(nk) // Q_BLOCK
    kaug[:, 4] = kaug[:, 5] = 1.0
    rel = nb * Q_BLOCK + np.arange(Q_BLOCK)[:, None] - np.arange(nk)[None, :]
    band = np.where((rel >= 0) & (rel <= max_dist), 0.0, -MASK_BIG).astype(np.float32)
    return [qc, jnp.asarray(kaug, BF16), jnp.asarray(band)]


BAND_QB = 4
CMP_QB = 1


def _banded_call(q_arr, kv_arr, *, grid, q_spec, kv_specs, out_spec, out_shape, out_cols, nb, qb, max_dist, heads,
                 k_off, v_off, q_axis, sinks=None, with_lse=False, stack=False, fold=False, log2_units=False):
    assert not (log2_units and (with_lse or sinks is not None))
    if len(kv_specs) == 1 and nb == 1:
        keys = "self"
    elif qb > 1:
        keys = "window"
        assert len(kv_specs) == 2 and nb <= qb
    else:
        keys = "blocks"
        assert len(kv_specs) == nb + 1
    consts = _banded_consts(heads, nb, max_dist)
    in_specs = [q_spec] + list(kv_specs) + [pl.BlockSpec(c.shape, lambda *_, nd=c.ndim: (0,) * nd) for c in consts]
    args = [q_arr] + [kv_arr] * len(kv_specs) + consts
    if sinks is not None:
        in_specs.append(pl.BlockSpec(memory_space=pltpu.SMEM))
        args.append(sinks)
    oshape = jax.ShapeDtypeStruct(out_shape, F32)
    body = functools.partial(_banded_body, nb=nb, qb=qb, keys=keys, heads=heads, k_off=k_off, v_off=v_off,
                             n_pairs=out_cols // LANES, q_axis=q_axis, use_sinks=sinks is not None,
                             with_lse=with_lse, stack=stack, fold=fold, log2_units=log2_units)
    return pl.pallas_call(
        body,
        grid=grid,
        in_specs=in_specs,
        out_specs=[out_spec, out_spec] if with_lse else out_spec,
        out_shape=[oshape, oshape] if with_lse else oshape,
        compiler_params=_cparams(("parallel",) * len(grid)),
    )(*args)


def _row_specs(b, s, qb, out_cols):
    q_spec = pl.BlockSpec((None, qb * Q_BLOCK, SEC), lambda bb, i: (bb, i, 0))
    prev = pl.BlockSpec((None, Q_BLOCK, SEC), lambda bb, i: (bb, jnp.maximum(qb * i - 1, 0), 0))
    out_spec = pl.BlockSpec((None, qb * Q_BLOCK, out_cols), lambda bb, i: (bb, i, 0))
    return dict(grid=(b, s // (qb * Q_BLOCK)), q_axis=1, q_spec=q_spec, kv_specs=[prev], out_spec=out_spec,
                out_shape=(b, s, out_cols), qb=qb)


def _dilated_group(arr, gi, b, s):
    window, dil = A_GROUPS[gi]
    slopes = _slopes(A_HEADS)
    heads = tuple((hh // 2, hh % 2, hh // 2, hh % 2, slopes[gi * A_HEADS_PER_GROUP + hh] * dil, hh)
                  for hh in range(A_HEADS_PER_GROUP))
    common = dict(out_cols=A_OUT, nb=1, max_dist=window // dil, heads=heads, k_off=256, v_off=512,
                  with_lse=True, stack=False)
    if dil == 1:
        return _banded_call(arr, arr, **_row_specs(b, s, BAND_QB, A_OUT), **common)
    nc = s // (Q_BLOCK * dil)
    qb = BAND_QB if nc % BAND_QB == 0 else 1
    q_spec = pl.BlockSpec((qb, None, Q_BLOCK, SEC), lambda bb, c, i: ((bb * nc) // qb + i, c, 0, 0))
    prev = pl.BlockSpec((None, None, Q_BLOCK, SEC), lambda bb, c, i: (bb * nc + jnp.maximum(qb * i - 1, 0), c, 0, 0))
    out_spec = pl.BlockSpec((qb, None, Q_BLOCK, A_OUT), lambda bb, c, i: ((bb * nc) // qb + i, c, 0, 0))
    return _banded_call(arr, arr, grid=(b, dil, nc // qb), q_axis=2, q_spec=q_spec, kv_specs=[prev],
                        out_spec=out_spec, out_shape=(b * nc, dil, Q_BLOCK, A_OUT), qb=qb, **common)


def _sink_swa(arr, sinks, b, s):
    slopes = _slopes(B_HEADS)
    rep = B_HEADS // B_KV_HEADS
    heads = tuple((h // 2, h % 2, 0, h // rep, slopes[h], h) for h in range(B_HEADS))
    return _banded_call(arr, arr, **_row_specs(b, s, BAND_QB, B_HEADS * HEAD_DIM), out_cols=B_HEADS * HEAD_DIM,
                        nb=1, max_dist=B_WINDOW - 1, heads=heads, k_off=512, v_off=640, sinks=sinks)


def _nsa_window(cq, ckv, b, s):
    slopes = _slopes(C_HEADS)
    heads = tuple((h // 2, h % 2, 0, h // C_REP, slopes[h] * LOG2E, h) for h in range(C_HEADS))
    nb = -(-(C_WINDOW - 1) // Q_BLOCK)
    qb = nb
    blk = (None, qb * Q_BLOCK, SEC)
    kv_specs = [pl.BlockSpec(blk, lambda bb, i: (bb, jnp.maximum(i - 1, 0), 0)),
                pl.BlockSpec(blk, lambda bb, i: (bb, i, 0))]
    out_cols = C_HEADS * HEAD_DIM
    return _banded_call(
        cq, ckv, grid=(b, s // (qb * Q_BLOCK)), q_axis=1, qb=qb, log2_units=True,
        q_spec=pl.BlockSpec(blk, lambda bb, i: (bb, i, 0)), kv_specs=kv_specs,
        out_spec=pl.BlockSpec((None, qb * Q_BLOCK, out_cols), lambda bb, i: (bb, i, 0)),
        out_shape=(b, s, out_cols), out_cols=out_cols, nb=nb, max_dist=C_WINDOW - 1, heads=heads,
        k_off=512, v_off=640, stack=True, fold=True)


def _compress_body(t_ref, wb_ref, prow_ref, w2k_ref, w2vt_ref, kg_ref, bd_ref, kc_ref, vct_ref, *, n_chunks):
    hid_cols = 2 * C_KV_HEADS * CMP_HIDDEN
    u = jnp.zeros((n_chunks, hid_cols), F32)
    v = jnp.zeros((n_chunks, hid_cols), F32)
    pc = jnp.zeros((1, hid_cols), F32)
    for c in range(CMP_STRIDE):
        tc = t_ref[c]
        u = u + _dot(tc, wb_ref[0, c])
        v = v + _dot(tc, wb_ref[1, c])
        pc = pc + _dot(prow_ref[0, c], wb_ref[0, c])[0:1] + _dot(prow_ref[1, c], wb_ref[1, c])[0:1]
    hsum = u + pltpu.roll(v, n_chunks - 1, 0) + pc
    hid = (hsum * jax.nn.sigmoid(hsum)).astype(BF16)
    half = C_KV_HEADS * CMP_HIDDEN
    k = _dot(hid[:, :half], w2k_ref[...])
    hi, lo = _split(k * k)
    ss = _dot(hi, bd_ref[...]) + _dot(lo, bd_ref[...])
    kc_ref[...] = (k * lax.rsqrt(ss * (1.0 / HEAD_DIM) + RMS_EPS) * kg_ref[...]).astype(BF16)
    vct_ref[...] = _nt_dot(w2vt_ref[...], hid[:, half:]).astype(BF16)


def _compress(cmpd, wb, prow, w2k, w2vt, kg):
    b, _, n_chunks, width = cmpd.shape
    bd = jnp.asarray(np.kron(np.eye(LANES // HEAD_DIM), np.ones((HEAD_DIM, HEAD_DIM))), BF16)
    return pl.pallas_call(
        functools.partial(_compress_body, n_chunks=n_chunks),
        grid=(b,),
        in_specs=[
            pl.BlockSpec((None, CMP_STRIDE, n_chunks, width), lambda bb: (bb, 0, 0, 0)),
            _resident(wb.shape), _resident(prow.shape), _resident(w2k.shape), _resident(w2vt.shape),
            _resident((1, LANES)), _resident((LANES, LANES)),
        ],
        out_specs=[
            pl.BlockSpec((None, n_chunks, LANES), lambda bb: (bb, 0, 0)),
            pl.BlockSpec((None, LANES, n_chunks), lambda bb: (bb, 0, 0)),
        ],
        out_shape=[
            jax.ShapeDtypeStruct((b, n_chunks, LANES), BF16),
            jax.ShapeDtypeStruct((b, LANES, n_chunks), BF16),
        ],
        compiler_params=_cparams(("parallel",)),
    )(cmpd, wb, prow, w2k, w2vt, kg, bd)


def _to_natural(ot_list, eye):
    pairs = []
    for k in range(0, len(ot_list), 2):
        hi, lo = _split(jnp.concatenate([ot_list[k], ot_list[k + 1]], axis=0))
        pairs.append(_nt_dot(eye, hi) + _nt_dot(eye, lo))
    return jnp.concatenate(pairs, axis=1)


def _cmp_body(qt_ref, kc_ref, vct_ref, ovt_ref, eye_ref, spread_ref, cmask_ref,
              o_ref, selm_ref, cnt_ref, sel_sc, *, n_top):
    g = pl.program_id(1)
    i = pl.program_id(2)
    n_pad = kc_ref.shape[0]
    n_slc = ovt_ref.shape[0]
    qw = qt_ref.shape[1]
    qb = qw // Q_BLOCK
    per_q = Q_BLOCK // CMP_STRIDE
    own_rows = (lax.broadcasted_iota(jnp.int32, (LANES, qw), 0) // HEAD_DIM) == g
    slopes = [sl * LOG2E for sl in _slopes(C_HEADS)]

    def attend(rows):
        kc = kc_ref[0:rows, :]
        vct = vct_ref[:, 0:rows]
        mask = jnp.concatenate(
            [cmask_ref[pl.ds(pl.multiple_of(n_pad - per_q * (qb * i + h), 8), rows), :] for h in range(qb)], axis=1)
        n_f = (CMP_STRIDE * lax.broadcasted_iota(jnp.int32, (rows, qw), 0)).astype(F32)
        psum = jnp.zeros((rows, qw), F32)
        outs = []
        for r in range(C_REP):
            slope = jnp.where(g == 0, slopes[r], slopes[C_REP + r])
            qt = qt_ref[r * HEAD_DIM:(r + 1) * HEAD_DIM, :]
            q_pad = jnp.where(own_rows, jnp.concatenate([qt, qt], axis=0), jnp.zeros((LANES, qw), BF16))
            s = _dot(kc, q_pad) + (slope * n_f + mask)
            m = jnp.maximum(jnp.max(s, axis=0, keepdims=True), -1e20)
            e = jnp.exp2(s - m)
            den = jnp.sum(e, axis=0, keepdims=True)
            p = e * (1.0 / jnp.where(den > 0, den, 1.0))
            psum = psum + p
            both = _dot(vct, p.astype(BF16))
            outs.append(jnp.where(g == 0, both[:HEAD_DIM], both[HEAD_DIM:]))
        o_ref[...] = _to_natural(outs, eye_ref[...])

        n_j = rows * CMP_STRIDE // SLC_BLOCK
        hi, lo = _split(psum)
        ovt = ovt_ref[0:n_j, 0:rows]
        imp = _dot(ovt, hi) + _dot(ovt, lo)
        j_idx = lax.broadcasted_iota(jnp.int32, (n_j, qw), 0)
        t_q = i * qw + lax.broadcasted_iota(jnp.int32, (n_j, qw), 1)
        cur = lax.shift_right_logical(t_q, int(math.log2(SLC_BLOCK)))
        forced = ((j_idx == 0) | (j_idx == cur) | (j_idx == cur - 1)) & (j_idx <= cur)
        v = jnp.where((j_idx <= cur) & jnp.logical_not(forced), imp, -1.0)
        sel = jnp.where(forced, 1.0, 0.0)
        for _ in range(n_top - 3):
            m = jnp.max(v, axis=0, keepdims=True)
            first = jnp.min(jnp.where((v == m) & (m > -0.5), j_idx, n_slc), axis=0, keepdims=True)
            pick = j_idx == first
            sel = jnp.where(pick, 1.0, sel)
            v = jnp.where(pick, -1.0, v)
        sel_sc[0:n_j, :] = sel
        if n_j < n_slc:
            sel_sc[n_j:, :] = jnp.zeros((n_slc - n_j, qw), F32)

    n_var = n_pad // LANES
    for var in range(n_var):
        pl.when((qb * i + qb - 1) // (LANES // per_q) == var)(functools.partial(attend, (var + 1) * LANES))

    sel = sel_sc[...]
    neg = jnp.where(sel > 0, 0.0, -MASK_BIG).astype(BF16)
    mask_rows = _dot(spread_ref[...], neg).astype(BF16)
    sel_b = sel.astype(BF16)
    for h in range(qb):
        lanes = slice(h * Q_BLOCK, (h + 1) * Q_BLOCK)
        selm_ref[h] = mask_rows[:, lanes]
        cnt_ref[h] = _nt_dot(jnp.ones((8, Q_BLOCK), BF16), sel_b[:, lanes])


def _cmp_select(qt, kc, vct, ovt, *, n_cmp):
    b, _, s = qt.shape
    g = C_KV_HEADS
    n_pad = kc.shape[1]
    n_slc = ovt.shape[0]
    nq = s // Q_BLOCK
    rows = C_REP * HEAD_DIM
    qb = CMP_QB if nq % CMP_QB == 0 else 1
    qw = qb * Q_BLOCK
    steps = nq // qb
    eye = jnp.asarray(np.eye(qw), BF16)
    n_top = min(SLC_TOPK, n_slc)
    assert n_top > 3 and n_cmp == n_pad - 1 and n_pad % LANES == 0
    n_rel = np.arange(-n_pad, n_pad)[:, None]
    cmask = np.where(CMP_STRIDE * n_rel + CMP_BLOCK - 1 <= np.arange(Q_BLOCK)[None, :], 0.0, -MASK_BIG)
    return pl.pallas_call(
        functools.partial(_cmp_body, n_top=n_top),
        grid=(b, g, steps),
        in_specs=[
            pl.BlockSpec((None, rows, qw), lambda bb, gg, i: (bb, gg, i)),
            pl.BlockSpec((None, n_pad, LANES), lambda bb, gg, i: (bb, 0, 0)),
            pl.BlockSpec((None, LANES, n_pad), lambda bb, gg, i: (bb, 0, 0)),
            pl.BlockSpec((n_slc, n_pad), lambda bb, gg, i: (0, 0)),
            pl.BlockSpec((qw, qw), lambda bb, gg, i: (0, 0)),
            pl.BlockSpec((nq * AUG_ROWS, n_slc), lambda bb, gg, i: (0, 0)),
            pl.BlockSpec((2 * n_pad, Q_BLOCK), lambda bb, gg, i: (0, 0)),
        ],
        out_specs=[
            pl.BlockSpec((qw, rows), lambda bb, gg, i: (bb * steps + i, gg)),
            pl.BlockSpec((None, None, qb, nq * AUG_ROWS, Q_BLOCK), lambda bb, gg, i: (bb, gg, i, 0, 0)),
            pl.BlockSpec((None, None, qb, 8, n_slc), lambda bb, gg, i: (bb, gg, i, 0, 0)),
        ],
        out_shape=[
            jax.ShapeDtypeStruct((b * s, g * rows), F32),
            jax.ShapeDtypeStruct((b, g, nq, nq * AUG_ROWS, Q_BLOCK), BF16),
            jax.ShapeDtypeStruct((b, g, nq, 8, n_slc), F32),
        ],
        scratch_shapes=[pltpu.VMEM((n_slc, qw), F32)],
        compiler_params=_cparams(("parallel", "parallel", "parallel")),
    )(qt, kc, vct, ovt, eye, jnp.asarray(_block_spread(nq), BF16), jnp.asarray(cmask, F32))


def _slc_body(list_ref, qt_ref, ks_ref, vst_ref, selm_ref, eye_ref, slot_ref, o_ref, qaug, m_sc, l_sc, acc_sc,
              *, nq, stride):
    bb = pl.program_id(0)
    g = pl.program_id(1)
    i = pl.program_id(2)
    width = C_REP * Q_BLOCK
    slopes = [sl * LOG2E for sl in _slopes(C_HEADS)]
    slope_s = [jnp.where(g == 0, slopes[r], slopes[C_REP + r]) for r in range(C_REP)]

    own_rows = (lax.broadcasted_iota(jnp.int32, (LANES, width), 0) // HEAD_DIM) == g
    q6 = jnp.concatenate([qt_ref[r * HEAD_DIM:(r + 1) * HEAD_DIM, :] for r in range(C_REP)], axis=1)
    qaug[0:LANES, :] = jnp.where(own_rows, jnp.concatenate([q6, q6], axis=0), jnp.zeros((LANES, width), BF16))
    head = lax.broadcasted_iota(jnp.int32, (LANES, width), 1) // Q_BLOCK
    row = lax.broadcasted_iota(jnp.int32, (LANES, width), 0)
    slope_t = jnp.zeros((LANES, width), F32)
    for r in range(C_REP):
        slope_t = jnp.where(head == r, slope_s[r], slope_t)
    s_hi, s_lo = _split(slope_t)
    slope_rows = jnp.where(row == AUG_POS, s_hi.astype(F32), jnp.where(row == AUG_POS + 1, s_lo.astype(F32), 0.0))
    qaug[LANES:, :] = slope_rows.astype(BF16)

    q_loc = lax.broadcasted_iota(jnp.int32, (Q_BLOCK, Q_BLOCK), 1)
    k_loc = lax.broadcasted_iota(jnp.int32, (Q_BLOCK, Q_BLOCK), 0)
    causal = jnp.where(k_loc > q_loc, -MASK_BIG, 0.0)

    m_sc[...] = jnp.full(m_sc.shape, NEG_INF, F32)
    l_sc[...] = jnp.zeros(l_sc.shape, F32)
    acc_sc[...] = jnp.zeros(acc_sc.shape, F32)

    def accumulate(tiles, own_first):
        keys = []
        for u, (jj, _) in enumerate(tiles):
            keys.append(ks_ref[pl.ds(pl.multiple_of(jj * Q_BLOCK, Q_BLOCK), Q_BLOCK), :] + slot_ref[u])
            rows = selm_ref[pl.ds(pl.multiple_of(jj * AUG_ROWS, AUG_ROWS), AUG_ROWS), :]
            lo = LANES + AUG_ROWS * (u + 1)
            qaug[lo:lo + AUG_ROWS, :] = jnp.concatenate([rows] * C_REP, axis=1)
        st = _dot(jnp.concatenate(keys, axis=0), qaug[...])
        ps, alphas = [], []
        for r in range(C_REP):
            sl = slice(r * Q_BLOCK, (r + 1) * Q_BLOCK)
            m_old = m_sc[:, sl]
            m_new = m_old
            parts = []
            for u, (jj, extra) in enumerate(tiles):
                s = st[u * Q_BLOCK:(u + 1) * Q_BLOCK, sl]
                if own_first and u == 0:
                    s = s + causal
                c = slope_s[r] * ((jj - i) * Q_BLOCK).astype(F32) + extra
                m_new = jnp.maximum(m_new, jnp.max(s, axis=0, keepdims=True) + c)
                parts.append((s, c))
            alpha = jnp.exp2(m_old - m_new)
            l_new = alpha * l_sc[:, sl]
            p_rows = []
            for s, c in parts:
                p = jnp.exp2(s + (c - m_new))
                l_new = l_new + jnp.sum(p, axis=0, keepdims=True)
                p_rows.append(p.astype(BF16))
            l_sc[:, sl] = l_new
            m_sc[:, sl] = m_new
            ps.append(jnp.concatenate(p_rows, axis=0))
            alphas.append(alpha)
        values = jnp.concatenate([vst_ref[jj] for jj, _ in tiles], axis=1)
        pv = _dot(values, jnp.concatenate(ps, axis=1))
        acc_sc[...] = jnp.concatenate(alphas, axis=1) * acc_sc[...] + pv

    base = ((bb * pl.num_programs(1) + g) * nq + i) * stride
    count = list_ref[base]

    def listed(slot):
        return list_ref[base + 1 + slot], jnp.where(slot < count, 0.0, -MASK_BIG)

    accumulate([(i, 0.0)] + [listed(u) for u in range(SLC_GROUP - 1)], True)

    def step(k, carry):
        accumulate([listed(SLC_GROUP - 1 + SLC_GROUP * k + u) for u in range(SLC_GROUP)], False)
        return carry

    rest = jnp.maximum(count - (SLC_GROUP - 1), 0)
    full = rest // SLC_GROUP
    lax.fori_loop(0, full, step, 0)
    left = rest - full * SLC_GROUP
    first_left = SLC_GROUP - 1 + full * SLC_GROUP
    for size in range(1, SLC_GROUP):
        @pl.when(left == size)
        def _(size=size):
            accumulate([listed(first_left + u) for u in range(size)], False)
    l = l_sc[...]
    o = acc_sc[...] / jnp.where(l > 0, l, 1.0)
    o_ref[...] = _to_natural([o[:, r * Q_BLOCK:(r + 1) * Q_BLOCK] for r in range(C_REP)], eye_ref[...])


def _slc_attention(lists, qt, ks, vst, sel, *, stride):
    b, _, s = qt.shape
    g = C_KV_HEADS
    nq = s // Q_BLOCK
    rows = C_REP * HEAD_DIM
    width = C_REP * Q_BLOCK
    eye = jnp.asarray(np.eye(LANES), BF16)
    assert AUG_ROWS * (SLC_GROUP + 1) <= LANES
    slots = jnp.asarray(_slot_pattern(), BF16)
    grid_spec = pltpu.PrefetchScalarGridSpec(
        num_scalar_prefetch=1,
        grid=(b, g, nq),
        in_specs=[
            pl.BlockSpec((None, rows, Q_BLOCK), lambda bb, gg, i, bits: (bb, gg, i)),
            pl.BlockSpec((None, s, 2 * LANES), lambda bb, gg, i, bits: (bb, 0, 0)),
            pl.BlockSpec((None, nq, None, HEAD_DIM, Q_BLOCK), lambda bb, gg, i, bits: (bb, 0, gg, 0, 0)),
            pl.BlockSpec((None, None, None, nq * AUG_ROWS, Q_BLOCK), lambda bb, gg, i, bits: (bb, gg, i, 0, 0)),
            pl.BlockSpec((LANES, LANES), lambda bb, gg, i, bits: (0, 0)),
            pl.BlockSpec(slots.shape, lambda bb, gg, i, bits: (0, 0, 0)),
        ],
        out_specs=pl.BlockSpec((Q_BLOCK, rows), lambda bb, gg, i, bits: (bb * nq + i, gg)),
        scratch_shapes=[
            pltpu.VMEM((2 * LANES, width), BF16),
            pltpu.VMEM((1, width), F32),
            pltpu.VMEM((1, width), F32),
            pltpu.VMEM((HEAD_DIM, width), F32),
        ],
    )
    return pl.pallas_call(
        functools.partial(_slc_body, nq=nq, stride=stride),
        grid_spec=grid_spec,
        out_shape=jax.ShapeDtypeStruct((b * s, g * rows), F32),
        compiler_params=_cparams(("parallel", "parallel", "parallel")),
    )(lists, qt, ks, vst, sel, eye, slots)


def _merge_body(x_ref, g_ref, oa0, la0, oa1, la1, oa2, la2, ob_ref, ocmp_ref, oslc_ref, owin_ref, cg_ref,
                p4t_ref, p16t_ref, ex_ref, wg0_ref, wg1_ref, wg2_ref, wa_ref, wb_ref, wc_ref,
                out_ref, h_ref, oall_ref):
    @pl.when(pl.program_id(1) == 0)
    def _():
        def natural(ref, pt_ref):
            hi, lo = _split(ref[...].reshape(TM, A_OUT))
            return _dot(pt_ref[...], hi) + _dot(pt_ref[...], lo)

        o1, l1 = natural(oa1, p4t_ref), natural(la1, p4t_ref)
        o2, l2 = natural(oa2, p16t_ref), natural(la2, p16t_ref)
        cg_split = jnp.concatenate(_split(cg_ref[...]), axis=1)
        gate_c = [_dot(cg_split, ex_ref[w]) for w in range(3)]

        h_ref[...] = _rms_rows(x_ref[...], g_ref[...])
        o0, l0 = oa0[...], la0[...]
        mx = jnp.maximum(jnp.maximum(l0, l1), l2)
        e0, e1, e2 = jnp.exp(l0 - mx), jnp.exp(l1 - mx), jnp.exp(l2 - mx)
        oall_ref[:, 0:A_OUT] = ((e0 * o0 + e1 * o1 + e2 * o2) / (e0 + e1 + e2)).astype(BF16)
        oall_ref[:, A_OUT:A_OUT + B_HEADS * HEAD_DIM] = ob_ref[...].astype(BF16)
        o_c = gate_c[0] * ocmp_ref[...] + gate_c[1] * oslc_ref[...] + gate_c[2] * owin_ref[...]
        oall_ref[:, A_OUT + B_HEADS * HEAD_DIM:] = o_c.astype(BF16)

    h = h_ref[...]
    c0, c1 = A_OUT, A_OUT + B_HEADS * HEAD_DIM
    merged = jax.nn.sigmoid(_dot(h, wg0_ref[...])) * _dot(oall_ref[:, 0:c0], wa_ref[...])
    merged += jax.nn.sigmoid(_dot(h, wg1_ref[...])) * _dot(oall_ref[:, c0:c1], wb_ref[...])
    merged += jax.nn.sigmoid(_dot(h, wg2_ref[...])) * _dot(oall_ref[:, c1:], wc_ref[...])
    out_ref[...] = merged.astype(BF16)


def _merge(x, g, a_outs, ob, ocmp, oslc, owin, cg, w_gate, wa, wb, wc, ex, *, tn=512):
    t, d = x.shape
    per16 = CHUNK16 // TM
    n_t = d // tn
    (oa0, la0), (oa1, la1), (oa2, la2) = a_outs

    last = t // TM - 1

    def tile(i, n, ahead):
        return jnp.minimum(i + jnp.where(n >= ahead, 1, 0), last)

    def rows(a, ahead):
        return pl.BlockSpec((TM, a.shape[1]), lambda i, n: (tile(i, n, ahead), 0))

    a1_spec = pl.BlockSpec((None, 4, Q_BLOCK, A_OUT), lambda i, n: (tile(i, n, 1), 0, 0, 0))
    a2_spec = pl.BlockSpec((None, 16, TM // 16, A_OUT),
                           lambda i, n: (tile(i, n, 1) // per16, 0, tile(i, n, 1) % per16, 0))
    p4t = jnp.asarray(_deinterleave(TM, 4).T, BF16)
    p16t = jnp.asarray(_deinterleave(TM, 16).T, BF16)
    in_specs = [
        rows(x, 1), _resident((1, d)),
        rows(oa0, 1), rows(la0, 1), a1_spec, a1_spec, a2_spec, a2_spec,
        rows(ob, 3), rows(ocmp, 2), rows(oslc, 2), rows(owin, 2), rows(cg, 3),
        _resident((TM, TM)), _resident((TM, TM)), _resident(ex.shape),
        pl.BlockSpec((d, tn), lambda i, n: (0, n)),
        pl.BlockSpec((d, tn), lambda i, n: (0, n + n_t)),
        pl.BlockSpec((d, tn), lambda i, n: (0, n + 2 * n_t)),
        pl.BlockSpec((wa.shape[0], tn), lambda i, n: (0, n)),
        pl.BlockSpec((wb.shape[0], tn), lambda i, n: (0, n)),
        pl.BlockSpec((wc.shape[0], tn), lambda i, n: (0, n)),
    ]
    return pl.pallas_call(
        _merge_body,
        grid=(t // TM, n_t),
        in_specs=in_specs,
        out_specs=pl.BlockSpec((TM, tn), lambda i, n: (i, n)),
        out_shape=jax.ShapeDtypeStruct((t, d), BF16),
        scratch_shapes=[pltpu.VMEM((TM, d), BF16), pltpu.VMEM((TM, wa.shape[0] + wb.shape[0] + wc.shape[0]), BF16)],
        compiler_params=_cparams(("parallel", "arbitrary")),
    )(x, g.reshape(1, d), oa0, la0, oa1, la1, oa2, la2, ob, ocmp, oslc, owin, cg, p4t, p16t, ex,
      w_gate, w_gate, w_gate, wa, wb, wc)


def _out_body(x_ref, m_ref, w_ref, o_ref):
    o_ref[...] = x_ref[...] + _dot(m_ref[...], w_ref[...])


def _out_proj(x, merged, w_out):
    t, d = x.shape
    rows = pl.BlockSpec((TM, d), lambda i: (i, 0))
    return pl.pallas_call(
        _out_body,
        grid=(t // TM,),
        in_specs=[rows, rows, _resident((d, d))],
        out_specs=rows,
        out_shape=jax.ShapeDtypeStruct((t, d), F32),
        compiler_params=_cparams(("parallel",)),
    )(x, merged, w_out)


def _qkv_column_params(qk_gain):
    flag, gain, scale = [], [], []
    one = jnp.ones((HEAD_DIM,), F32)

    def add(n_heads, normed, is_q, gvec, units=1.0):
        for _ in range(n_heads):
            flag.append(np.full((HEAD_DIM,), 1.0 if normed else 0.0, np.float32))
            gain.append(gvec if normed else one)
            scale.append(np.full((HEAD_DIM,), units * HEAD_DIM ** -0.5 if is_q else 1.0, np.float32))

    for _ in range(len(A_GROUPS)):
        add(A_HEADS_PER_GROUP, True, True, qk_gain[0, 0])
        add(A_HEADS_PER_GROUP, True, False, qk_gain[0, 1])
        add(A_HEADS_PER_GROUP, False, False, one)
    add(B_HEADS, True, True, qk_gain[1, 0])
    add(B_KV_HEADS, True, False, qk_gain[1, 1])
    add(B_KV_HEADS, False, False, one)
    add(C_HEADS, True, True, qk_gain[2, 0], units=LOG2E)
    for normed in (False, False, True, False, True, False):
        add(C_KV_HEADS, normed, False, qk_gain[2, 1])
    flag = np.concatenate(flag)
    assert flag.shape[0] == QKV_COLS
    return jnp.asarray(flag), jnp.concatenate(gain) * jnp.asarray(np.concatenate(scale))


def _overlap_t(n_slc, n_pad, n_cmp):
    n = np.arange(n_pad)[None, :]
    j = np.arange(n_slc)[:, None]
    start, end = CMP_STRIDE * n, CMP_STRIDE * n + CMP_BLOCK - 1
    ov = (start <= SLC_BLOCK * j + SLC_BLOCK - 1) & (end >= SLC_BLOCK * j) & (n < n_cmp)
    return jnp.asarray(ov, BF16)


def _gate_expand():
    ex = np.zeros((3, LANES, C_HEADS * HEAD_DIM), np.float32)
    for w in range(3):
        for h in range(C_HEADS):
            ex[w, h * 3 + w, h * HEAD_DIM:(h + 1) * HEAD_DIM] = 1.0
    return jnp.asarray(np.concatenate([ex, ex], axis=1), BF16)


def _compress_weights(cmp_pos, cmp_w1, cmp_w2):
    n_q = 2 * C_KV_HEADS
    w1 = cmp_w1.reshape(2, 2, CMP_STRIDE, HEAD_DIM, CMP_HIDDEN)
    w1q = jnp.repeat(w1, C_KV_HEADS, axis=0)
    wb = jnp.einsum("qhcdn,qp->hcqdpn", w1q, jnp.eye(n_q, dtype=F32))
    wb = wb.reshape(2, CMP_STRIDE, n_q * HEAD_DIM, n_q * CMP_HIDDEN).astype(BF16)
    pos = cmp_pos.reshape(2, 2, CMP_STRIDE, HEAD_DIM)
    prow = jnp.repeat(pos, C_KV_HEADS, axis=0).transpose(1, 2, 0, 3).reshape(2, CMP_STRIDE, 1, n_q * HEAD_DIM)
    prow = jnp.broadcast_to(prow, (2, CMP_STRIDE, 8, n_q * HEAD_DIM)).astype(BF16)
    eye_g = jnp.eye(C_KV_HEADS, dtype=F32)
    w2k = jnp.kron(eye_g, cmp_w2[0]).astype(BF16)
    w2vt = jnp.kron(eye_g, cmp_w2[1]).T.astype(BF16)
    return wb, prow, w2k, w2vt


def _token_mixing(x, b, s, mix_norm, w_in, qk_gain, sinks, cmp_pos, cmp_w1, cmp_w2, w_a, w_b, w_c):
    t, d = x.shape
    assert s % CHUNK16 == 0 and d % 512 == 0
    c_gate_cols = 3 * C_HEADS
    w_qkv = w_in[:, :QKV_COLS + LANES].astype(BF16)
    flag, gs = _qkv_column_params(qk_gain)
    a0, a1, a2, bsec, cq, ckv, cmpd, qt, ks, vst, cg = _qkv_proj(x, mix_norm, w_qkv, flag, gs, b, s)

    a_outs = [_dilated_group(a0.reshape(b, s, SEC), 0, b, s), _dilated_group(a1, 1, b, s),
              _dilated_group(a2, 2, b, s)]
    a_outs[0] = tuple(v.reshape(t, A_OUT) for v in a_outs[0])
    o_b = _sink_swa(bsec.reshape(b, s, SEC), sinks.astype(F32), b, s).reshape(t, -1)
    o_win = _nsa_window(cq.reshape(b, s, SEC), ckv.reshape(b, s, SEC), b, s).reshape(t, -1)

    n_chunks = s // CMP_STRIDE
    n_cmp = (s - CMP_BLOCK) // CMP_STRIDE + 1
    n_slc = s // SLC_BLOCK
    nq = s // Q_BLOCK
    kg = jnp.tile(qk_gain[2, 1], C_KV_HEADS).reshape(1, LANES)
    kc, vct = _compress(cmpd, *_compress_weights(cmp_pos, cmp_w1, cmp_w2), kg)
    o_cmp, sel, cnt = _cmp_select(qt, kc, vct, _overlap_t(n_slc, n_chunks, n_cmp), n_cmp=n_cmp)

    act = (cnt[:, :, :, 0, :] > 0).reshape(b, C_KV_HEADS, nq, nq, 2).any(axis=-1)
    act = act & (jnp.arange(nq)[None, :] < jnp.arange(nq)[:, None])
    order = jnp.argsort(jnp.logical_not(act), axis=-1, stable=True).astype(jnp.int32)
    count = jnp.sum(act, axis=-1, dtype=jnp.int32)[..., None]
    lists = jnp.concatenate([count, order] + [jnp.zeros_like(count)] * (SLC_GROUP - 1), axis=-1)
    o_slc = _slc_attention(lists.reshape(-1), qt, ks.reshape(b, s, 2 * LANES),
                           vst.reshape(b, nq, C_KV_HEADS, HEAD_DIM, Q_BLOCK), sel, stride=nq + SLC_GROUP)

    return _merge(x, mix_norm, a_outs, o_b, o_cmp, o_slc, o_win, cg,
                  w_in[:, QKV_COLS + c_gate_cols:].astype(BF16), w_a.astype(BF16), w_b.astype(BF16),
                  w_c.astype(BF16), _gate_expand())


def kernel(x, ffn1_norm, ffn1_w_gu, ffn1_w_down, mix_norm, w_in, qk_gain, sinks, cmp_pos, cmp_w1, cmp_w2,
           w_branch_a, w_branch_b, w_branch_c, w_out, ffn2_norm, ffn2_w_gu, ffn2_w_down):
    b, s, d = x.shape
    h = x.reshape(b * s, d)
    w1_gu, w1_down = ffn1_w_gu.astype(BF16), ffn1_w_down.astype(BF16)
    w2_gu, w2_down = ffn2_w_gu.astype(BF16), ffn2_w_down.astype(BF16)
    for l in range(ffn1_norm.shape[0]):
        h = _ffn(h, ffn1_norm[l], w1_gu, w1_down, l)
        merged = _token_mixing(h, b, s, mix_norm[l], w_in[l], qk_gain[l], sinks[l], cmp_pos[l], cmp_w1[l],
                               cmp_w2[l], w_branch_a[l], w_branch_b[l], w_branch_c[l])
        h = _out_proj(h, merged, w_out[l].astype(BF16))
        h = _ffn(h, ffn2_norm[l], w2_gu, w2_down, l)
    return h.reshape(b, s, d)
```

```python
import functools
import math

import numpy as np
import jax
import jax.numpy as jnp
from jax import lax
from jax.experimental import pallas as pl
from jax.experimental.pallas import tpu as pltpu

F32 = jnp.float32
BF16 = jnp.bfloat16

HEAD_DIM = 64
Q_BLOCK = 128
LANES = 128
A_GROUPS = ((128, 1), (512, 4), (2048, 16))
A_HEADS_PER_GROUP = 4
A_HEADS = 12
A_OUT = A_HEADS_PER_GROUP * HEAD_DIM
B_HEADS = 8
B_KV_HEADS = 2
B_WINDOW = 128
C_HEADS = 12
C_KV_HEADS = 2
C_REP = C_HEADS // C_KV_HEADS
CMP_BLOCK = 32
CMP_STRIDE = 16
CMP_HIDDEN = 256
SLC_BLOCK = 64
SLC_TOPK = 16
C_WINDOW = 512
RMS_EPS = 1e-6
NEG_INF = -1e30
SEC = 768
N_SEC = 6
QKV_COLS = SEC * N_SEC
TM = 512
CHUNK16 = Q_BLOCK * 16
VMEM_LIMIT = 56 * 1024 * 1024


def _slopes(n):
    return [float(2.0 ** (-8.0 * (h + 1) / n)) for h in range(n)]


def _cparams(sem):
    return pltpu.CompilerParams(dimension_semantics=sem, vmem_limit_bytes=VMEM_LIMIT)


def _dot(a, b):
    return jnp.dot(a, b, preferred_element_type=F32)


def _nt_dot(a, b):
    return lax.dot_general(a, b, (((1,), (1,)), ((), ())), preferred_element_type=F32)


def _split(v):
    hi = v.astype(BF16)
    return hi, (v - hi.astype(F32)).astype(BF16)


def _resident(shape):
    return pl.BlockSpec(shape, lambda *_: (0,) * len(shape), pipeline_mode=pl.Buffered(1))


def _rms_rows(x, g):
    ms = jnp.mean(x * x, axis=-1, keepdims=True)
    return (x * lax.rsqrt(ms + RMS_EPS) * g).astype(BF16)


def _deinterleave(n, d):
    p = np.zeros((n, n), np.float32)
    r = np.arange(n // d)
    for c in range(d):
        p[c * (n // d) + r, d * r + c] = 1.0
    return p


AUG_POS = 0
AUG_ROWS = 16
MASK_BIG = 1e30
SLC_GROUP = 7
LOG2E = math.log2(math.e)


def _key_pattern(n):
    pat = np.zeros((n, LANES), np.float32)
    pat[:, AUG_POS] = pat[:, AUG_POS + 1] = np.arange(n) % Q_BLOCK
    return pat


def _slot_pattern():
    pat = np.zeros((SLC_GROUP, Q_BLOCK, 2 * LANES), np.float32)
    r = np.arange(Q_BLOCK)
    for u in range(SLC_GROUP):
        pat[u, :, LANES + AUG_ROWS * (u + 1)] = r < SLC_BLOCK
        pat[u, :, LANES + AUG_ROWS * (u + 1) + 1] = r >= SLC_BLOCK
    return pat


def _block_spread(n_tiles):
    m = np.zeros((n_tiles * AUG_ROWS, 2 * n_tiles), np.float32)
    jj = np.arange(n_tiles)
    for e in range(2):
        m[AUG_ROWS * jj + e, 2 * jj + e] = 1.0
    return m


FFN_NORM_CHUNKS = 4


def _ffn_body(x_ref, g_ref, wg_ref, wu_ref, wd_ref, o_ref, h_ref, *, n_f):
    f = pl.program_id(1)

    def contribution(h):
        gate = _dot(h, wg_ref[...])
        up = _dot(h, wu_ref[...])
        act = (gate * jax.nn.sigmoid(gate) * up).astype(BF16)
        return _dot(act, wd_ref[...])

    @pl.when(f == 0)
    def _():
        chunk = x_ref.shape[0] // FFN_NORM_CHUNKS
        for c in range(FFN_NORM_CHUNKS):
            rows = slice(c * chunk, (c + 1) * chunk)
            h = _rms_rows(x_ref[rows, :], g_ref[...])
            h_ref[rows, :] = h
            o_ref[rows, :] = contribution(h)

    @pl.when((f > 0) & (f < n_f - 1))
    def _():
        o_ref[...] += contribution(h_ref[...])

    @pl.when(f == n_f - 1)
    def _():
        o_ref[...] = x_ref[...] + 0.5 * (o_ref[...] + contribution(h_ref[...]))


def _ffn(x, g, w_gu, w_down, layer, *, tm=1024, tf=512):
    t, d = x.shape
    d_ff = w_down.shape[1]
    n_f = d_ff // tf
    assert n_f >= 2
    rows = pl.BlockSpec((tm, d), lambda i, f: (i, 0))
    return pl.pallas_call(
        functools.partial(_ffn_body, n_f=n_f),
        grid=(t // tm, n_f),
        in_specs=[
            rows,
            pl.BlockSpec((1, d), lambda i, f: (0, 0)),
            pl.BlockSpec((None, d, tf), lambda i, f: (layer, 0, f)),
            pl.BlockSpec((None, d, tf), lambda i, f: (layer, 0, f + n_f)),
            pl.BlockSpec((None, tf, d), lambda i, f: (layer, f, 0)),
        ],
        out_specs=rows,
        out_shape=jax.ShapeDtypeStruct((t, d), F32),
        scratch_shapes=[pltpu.VMEM((tm, d), BF16)],
        compiler_params=_cparams(("parallel", "arbitrary")),
    )(x, g.reshape(1, d), w_gu, w_gu, w_down)


NORM_TILE = 256
NORM_TILES = {0: (0, 1), 1: (0, 1), 2: (0, 1), 3: (0, 1, 2), 4: (0, 1, 2), 5: (1, 2)}


def _head_sumsq(y, bd):
    return _dot((y * y).astype(BF16), bd)


def _qkv_body(x_ref, g_ref, w_ref, flag_ref, gs_ref, bd_ref, p4_ref, p16_ref, eye_ref, kpat_ref,
              a0_ref, a1_ref, a2_ref, b_ref, cq_ref, ckv_ref, cmpd_ref, qt_ref, ks_ref, vst_ref, cg_ref):
    h = _rms_rows(x_ref[...], g_ref[...])
    bd = bd_ref[...]

    def project(k):
        if k < len(A_GROUPS):
            return jnp.concatenate([_dot(h, w_ref[:, part * A_HEADS * HEAD_DIM + k * A_OUT:
                                                     part * A_HEADS * HEAD_DIM + (k + 1) * A_OUT])
                                    for part in range(3)], axis=1)
        return _dot(h, w_ref[:, k * SEC:(k + 1) * SEC])

    def finish(k, y):
        tiles = []
        for c in range(SEC // NORM_TILE):
            yc = y[:, c * NORM_TILE:(c + 1) * NORM_TILE]
            if c in NORM_TILES[k]:
                cols = slice(k * SEC + c * NORM_TILE, k * SEC + (c + 1) * NORM_TILE)
                inv = lax.rsqrt(_head_sumsq(yc, bd) * (1.0 / HEAD_DIM) + RMS_EPS)
                yc = yc * jnp.where(flag_ref[:, cols] > 0, inv, 1.0) * gs_ref[:, cols]
            tiles.append(yc.astype(BF16))
        return jnp.concatenate(tiles, axis=1)

    sec = []
    for first in range(0, N_SEC, 3):
        raw = [project(k) for k in range(first, first + 3)]
        sec += [finish(first + k, y) for k, y in enumerate(raw)]
    cg_logits = _dot(h, w_ref[:, QKV_COLS:QKV_COLS + LANES])

    a0_ref[...] = sec[0]
    a1_ref[...] = _dot(p4_ref[...], sec[1]).astype(BF16).reshape(a1_ref.shape)
    a2_ref[...] = _dot(p16_ref[...], sec[2]).astype(BF16).reshape(a2_ref.shape)
    b_ref[...] = sec[3]
    y_cq = sec[4]
    cq_ref[...] = y_cq
    qt_ref[...] = _nt_dot(eye_ref[...], y_cq).astype(BF16)
    y_ckv = sec[5]
    ckv_ref[...] = y_ckv
    cmpd_ref[...] = _dot(p16_ref[...], y_ckv[:, 0:2 * LANES]).astype(BF16).reshape(cmpd_ref.shape)
    ks_ref[:, 0:LANES] = y_ckv[:, 2 * LANES:3 * LANES]
    ks_ref[:, LANES:2 * LANES] = kpat_ref[...]
    eye = eye_ref[0:LANES, 0:LANES]
    for kb in range(vst_ref.shape[0]):
        vt = _nt_dot(eye, y_ckv[kb * Q_BLOCK:(kb + 1) * Q_BLOCK, 3 * LANES:4 * LANES]).astype(BF16)
        for gg in range(C_KV_HEADS):
            vst_ref[kb, gg] = vt[gg * HEAD_DIM:(gg + 1) * HEAD_DIM]
    cg_ref[...] = jax.nn.sigmoid(cg_logits)


def _qkv_proj(x, g, w, flag, gs, b, s):
    t, d = x.shape
    tiles_per_batch = s // TM
    per16 = CHUNK16 // TM
    bd = jnp.asarray(np.kron(np.eye(NORM_TILE // HEAD_DIM), np.ones((HEAD_DIM, HEAD_DIM))), BF16)
    p4 = jnp.asarray(_deinterleave(TM, 4), BF16)
    p16 = jnp.asarray(_deinterleave(TM, 16), BF16)
    eye = jnp.asarray(np.eye(SEC), BF16)
    nat = pl.BlockSpec((TM, SEC), lambda i: (i, 0))
    out_specs = [
        nat,
        pl.BlockSpec((None, 4, Q_BLOCK, SEC), lambda i: (i, 0, 0, 0)),
        pl.BlockSpec((None, 16, TM // 16, SEC), lambda i: (i // per16, 0, i % per16, 0)),
        nat, nat, nat,
        pl.BlockSpec((None, 16, TM // 16, 2 * LANES), lambda i: (i // tiles_per_batch, 0, i % tiles_per_batch, 0)),
        pl.BlockSpec((None, SEC, TM), lambda i: (i // tiles_per_batch, 0, i % tiles_per_batch)),
        pl.BlockSpec((TM, 2 * LANES), lambda i: (i, 0)),
        pl.BlockSpec((TM // Q_BLOCK, C_KV_HEADS, HEAD_DIM, Q_BLOCK), lambda i: (i, 0, 0, 0)),
        pl.BlockSpec((TM, LANES), lambda i: (i, 0)),
    ]
    out_shape = [
        jax.ShapeDtypeStruct((t, SEC), BF16),
        jax.ShapeDtypeStruct((t // TM, 4, Q_BLOCK, SEC), BF16),
        jax.ShapeDtypeStruct((t // CHUNK16, 16, Q_BLOCK, SEC), BF16),
        jax.ShapeDtypeStruct((t, SEC), BF16),
        jax.ShapeDtypeStruct((t, SEC), BF16),
        jax.ShapeDtypeStruct((t, SEC), BF16),
        jax.ShapeDtypeStruct((b, 16, s // 16, 2 * LANES), BF16),
        jax.ShapeDtypeStruct((b, SEC, s), BF16),
        jax.ShapeDtypeStruct((t, 2 * LANES), BF16),
        jax.ShapeDtypeStruct((t // Q_BLOCK, C_KV_HEADS, HEAD_DIM, Q_BLOCK), BF16),
        jax.ShapeDtypeStruct((t, LANES), F32),
    ]
    n_w = w.shape[1]
    return pl.pallas_call(
        _qkv_body,
        grid=(t // TM,),
        in_specs=[
            pl.BlockSpec((TM, d), lambda i: (i, 0)),
            _resident((1, d)),
            _resident((d, n_w)),
            _resident((1, QKV_COLS)),
            _resident((1, QKV_COLS)),
            _resident((NORM_TILE, NORM_TILE)),
            _resident((TM, TM)),
            _resident((TM, TM)),
            _resident((SEC, SEC)),
            _resident((TM, LANES)),
        ],
        out_specs=out_specs,
        out_shape=out_shape,
        compiler_params=_cparams(("parallel",)),
    )(x, g.reshape(1, d), w, flag.reshape(1, -1), gs.reshape(1, -1), bd, p4, p16, eye,
      jnp.asarray(_key_pattern(TM), BF16))


def _banded_body(*refs, nb, qb, keys, heads, k_off, v_off, n_pairs, q_axis, use_sinks, with_lse, stack, fold,
                 log2_units):
    refs = list(refs)
    q_ref = refs.pop(0)
    kv_refs = [refs.pop(0) for _ in range({"self": 1, "window": 2, "blocks": nb + 1}[keys])]
    qc_ref, kaug_ref, band_ref = refs.pop(0), refs.pop(0), refs.pop(0)
    sink_ref = refs.pop(0) if use_sinks else None
    o_ref = refs.pop(0)
    lse_ref = refs.pop(0) if with_lse else None

    i = pl.program_id(q_axis)
    nk = (nb + 1) * Q_BLOCK
    col = lax.broadcasted_iota(jnp.int32, (Q_BLOCK, nk), 1)
    band = band_ref[...]

    def start_mask(sub):
        if sub > 0 and sub >= nb:
            return band
        return band + jnp.where(col < (nb - (i * qb + sub)) * Q_BLOCK, -MASK_BIG, 0.0)

    rel_f = (nb * Q_BLOCK + lax.broadcasted_iota(jnp.int32, (Q_BLOCK, nk), 0) - col).astype(F32)
    lane = lax.broadcasted_iota(jnp.int32, (Q_BLOCK, LANES), 1)
    low_half = lane < HEAD_DIM
    kaug = kaug_ref[...]

    def q_cols(sub, c0):
        if len(q_ref.shape) == 3:
            return q_ref[sub, :, c0:c0 + LANES]
        return q_ref[sub * Q_BLOCK:(sub + 1) * Q_BLOCK, c0:c0 + LANES]

    kv_cache = {}

    def swap_halves(tile, swapped):
        return pltpu.roll(tile.astype(F32), HEAD_DIM, 1).astype(BF16) if swapped else tile

    def kv_tile(sub, off, kv_pair, swapped):
        c0 = off + kv_pair * LANES
        if keys == "window":
            key = (off, kv_pair, swapped)
            if key not in kv_cache:
                kv_cache[key] = swap_halves(jnp.concatenate([r[:, c0:c0 + LANES] for r in kv_refs], axis=0), swapped)
            first = qb - nb + sub
            return kv_cache[key][first * Q_BLOCK:(first + nb + 1) * Q_BLOCK]
        key = (sub, off, kv_pair, swapped)
        if key not in kv_cache:
            if keys == "self":
                blocks = [kv_refs[0][:, c0:c0 + LANES] if sub == 0 else q_cols(sub - 1, c0), q_cols(sub, c0)]
            else:
                blocks = [r[:, c0:c0 + LANES] for r in kv_refs]
            kv_cache[key] = swap_halves(jnp.concatenate(blocks, axis=0), swapped)
        return kv_cache[key]

    classes = {}
    for head in heads:
        pair, half, kv_pair, kv_half, slope, hidx = head
        key = (kv_pair, kv_half != half) if stack else hidx
        classes.setdefault(key, []).append(head)

    outs = [[[None, None] for _ in range(n_pairs)] for _ in range(qb)]
    lses = [[[None, None] for _ in range(n_pairs)] for _ in range(qb)]
    groups = [(sub, members) for sub in range(qb) for members in classes.values()]

    scores = []
    for sub, members in groups:
        n_h = len(members)
        kv_pair, swapped = members[0][2], members[0][3] != members[0][1]
        mask = start_mask(sub)
        q_rows = []
        for pair, half, _, _, _, hidx in members:
            qp = q_cols(sub, pair * LANES)
            own = low_half if half == 0 else jnp.logical_not(low_half)
            qm = jnp.where(own, qp, jnp.zeros_like(qp))
            q_rows.append(jnp.concatenate([qm, qc_ref[hidx]], axis=1) if fold else qm)
        if fold:
            k_aug = jnp.concatenate([kv_tile(sub, k_off, kv_pair, swapped), kaug], axis=1)
            s = _nt_dot(jnp.concatenate(q_rows, axis=0), k_aug)
            s = (s.reshape(n_h, Q_BLOCK, nk) + mask[None]).reshape(n_h * Q_BLOCK, nk)
        else:
            bias = jnp.concatenate([mask - member[4] * rel_f for member in members], axis=0)
            s = _nt_dot(jnp.concatenate(q_rows, axis=0), kv_tile(sub, k_off, kv_pair, swapped)) + bias
        scores.append(s)

    probs = []
    for (_, members), s in zip(groups, scores):
        m = jnp.max(s, axis=1, keepdims=True)
        if use_sinks:
            assert len(members) == 1
            sink = sink_ref[members[0][5]]
            m = jnp.maximum(m, sink)
        p = jnp.exp2(s - m) if log2_units else jnp.exp(s - m)
        den = jnp.sum(p, axis=1, keepdims=True)
        if use_sinks:
            den = den + jnp.exp(sink - m)
        probs.append((p.astype(BF16), m, den))

    for (sub, members), (p, m, den) in zip(groups, probs):
        kv_pair, swapped = members[0][2], members[0][3] != members[0][1]
        r = _dot(p, kv_tile(sub, v_off, kv_pair, swapped)) / den
        lse = m + jnp.log(den) if with_lse else None
        for k, (pair, half, _, _, _, _) in enumerate(members):
            outs[sub][pair][half] = r[k * Q_BLOCK:(k + 1) * Q_BLOCK]
            if with_lse:
                lses[sub][pair][half] = jnp.broadcast_to(lse[k * Q_BLOCK:(k + 1) * Q_BLOCK], (Q_BLOCK, LANES))

    def store(ref, sub, sl, value):
        if len(ref.shape) == 3:
            ref[sub, :, sl] = value
        else:
            ref[sub * Q_BLOCK:(sub + 1) * Q_BLOCK, sl] = value

    for sub in range(qb):
        for pair in range(n_pairs):
            sl = slice(pair * LANES, (pair + 1) * LANES)
            store(o_ref, sub, sl, jnp.where(low_half, outs[sub][pair][0], outs[sub][pair][1]))
            if with_lse:
                store(lse_ref, sub, sl, jnp.where(low_half, lses[sub][pair][0], lses[sub][pair][1]))


def _banded_consts(heads, nb, max_dist):
    nk = (nb + 1) * Q_BLOCK
    slope = np.asarray([h[4] for h in heads], np.float32)[:, None]
    q_dist = (nb * Q_BLOCK + np.arange(Q_BLOCK, dtype=np.float32))[None, :]
    ones = np.ones_like(q_dist)
    vals = jnp.asarray(np.stack([slope * ones, slope * Q_BLOCK * ones, -slope * q_dist], axis=-1))
    hi = vals.astype(BF16)
    lo = (vals - hi.astype(F32)).astype(BF16)
    cols = jnp.stack([hi[..., 0], lo[..., 0], hi[..., 1], lo[..., 1], hi[..., 2], lo[..., 2]], axis=-1)
    qc = jnp.pad(cols, ((0, 0), (0, 0), (0, LANES - cols.shape[-1])))
    kaug = np.zeros((nk, LANES), np.float32)
    kaug[:, 0] = kaug[:, 1] = np.arange(nk) % Q_BLOCK
    kaug[:, 2] = kaug[:, 3] = np.arange(nk) // Q_BLOCK
    kaug[:, 4] = kaug[:, 5] = 1.0
    rel = nb * Q_BLOCK + np.arange(Q_BLOCK)[:, None] - np.arange(nk)[None, :]
    band = np.where((rel >= 0) & (rel <= max_dist), 0.0, -MASK_BIG).astype(np.float32)
    return [qc, jnp.asarray(kaug, BF16), jnp.asarray(band)]


BAND_QB = 4
CMP_QB = 1


def _banded_call(q_arr, kv_arr, *, grid, q_spec, kv_specs, out_spec, out_shape, out_cols, nb, qb, max_dist, heads,
                 k_off, v_off, q_axis, sinks=None, with_lse=False, stack=False, fold=False, log2_units=False):
    assert not (log2_units and (with_lse or sinks is not None))
    if len(kv_specs) == 1 and nb == 1:
        keys = "self"
    elif qb > 1:
        keys = "window"
        assert len(kv_specs) == 2 and nb <= qb
    else:
        keys = "blocks"
        assert len(kv_specs) == nb + 1
    consts = _banded_consts(heads, nb, max_dist)
    in_specs = [q_spec] + list(kv_specs) + [pl.BlockSpec(c.shape, lambda *_, nd=c.ndim: (0,) * nd) for c in consts]
    args = [q_arr] + [kv_arr] * len(kv_specs) + consts
    if sinks is not None:
        in_specs.append(pl.BlockSpec(memory_space=pltpu.SMEM))
        args.append(sinks)
    oshape = jax.ShapeDtypeStruct(out_shape, F32)
    body = functools.partial(_banded_body, nb=nb, qb=qb, keys=keys, heads=heads, k_off=k_off, v_off=v_off,
                             n_pairs=out_cols // LANES, q_axis=q_axis, use_sinks=sinks is not None,
                             with_lse=with_lse, stack=stack, fold=fold, log2_units=log2_units)
    return pl.pallas_call(
        body,
        grid=grid,
        in_specs=in_specs,
        out_specs=[out_spec, out_spec] if with_lse else out_spec,
        out_shape=[oshape, oshape] if with_lse else oshape,
        compiler_params=_cparams(("parallel",) * len(grid)),
    )(*args)


def _row_specs(b, s, qb, out_cols):
    q_spec = pl.BlockSpec((None, qb * Q_BLOCK, SEC), lambda bb, i: (bb, i, 0))
    prev = pl.BlockSpec((None, Q_BLOCK, SEC), lambda bb, i: (bb, jnp.maximum(qb * i - 1, 0), 0))
    out_spec = pl.BlockSpec((None, qb * Q_BLOCK, out_cols), lambda bb, i: (bb, i, 0))
    return dict(grid=(b, s // (qb * Q_BLOCK)), q_axis=1, q_spec=q_spec, kv_specs=[prev], out_spec=out_spec,
                out_shape=(b, s, out_cols), qb=qb)


def _dilated_group(arr, gi, b, s):
    window, dil = A_GROUPS[gi]
    slopes = _slopes(A_HEADS)
    heads = tuple((hh // 2, hh % 2, hh // 2, hh % 2, slopes[gi * A_HEADS_PER_GROUP + hh] * dil, hh)
                  for hh in range(A_HEADS_PER_GROUP))
    common = dict(out_cols=A_OUT, nb=1, max_dist=window // dil, heads=heads, k_off=256, v_off=512,
                  with_lse=True, stack=False)
    if dil == 1:
        return _banded_call(arr, arr, **_row_specs(b, s, BAND_QB, A_OUT), **common)
    nc = s // (Q_BLOCK * dil)
    qb = BAND_QB if nc % BAND_QB == 0 else 1
    q_spec = pl.BlockSpec((qb, None, Q_BLOCK, SEC), lambda bb, c, i: ((bb * nc) // qb + i, c, 0, 0))
    prev = pl.BlockSpec((None, None, Q_BLOCK, SEC), lambda bb, c, i: (bb * nc + jnp.maximum(qb * i - 1, 0), c, 0, 0))
    out_spec = pl.BlockSpec((qb, None, Q_BLOCK, A_OUT), lambda bb, c, i: ((bb * nc) // qb + i, c, 0, 0))
    return _banded_call(arr, arr, grid=(b, dil, nc // qb), q_axis=2, q_spec=q_spec, kv_specs=[prev],
                        out_spec=out_spec, out_shape=(b * nc, dil, Q_BLOCK, A_OUT), qb=qb, **common)


def _sink_swa(arr, sinks, b, s):
    slopes = _slopes(B_HEADS)
    rep = B_HEADS // B_KV_HEADS
    heads = tuple((h // 2, h % 2, 0, h // rep, slopes[h], h) for h in range(B_HEADS))
    return _banded_call(arr, arr, **_row_specs(b, s, BAND_QB, B_HEADS * HEAD_DIM), out_cols=B_HEADS * HEAD_DIM,
                        nb=1, max_dist=B_WINDOW - 1, heads=heads, k_off=512, v_off=640, sinks=sinks)


def _nsa_window(cq, ckv, b, s):
    slopes = _slopes(C_HEADS)
    heads = tuple((h // 2, h % 2, 0, h // C_REP, slopes[h] * LOG2E, h) for h in range(C_HEADS))
    nb = -(-(C_WINDOW - 1) // Q_BLOCK)
    qb = nb
    blk = (None, qb * Q_BLOCK, SEC)
    kv_specs = [pl.BlockSpec(blk, lambda bb, i: (bb, jnp.maximum(i - 1, 0), 0)),
                pl.BlockSpec(blk, lambda bb, i: (bb, i, 0))]
    out_cols = C_HEADS * HEAD_DIM
    return _banded_call(
        cq, ckv, grid=(b, s // (qb * Q_BLOCK)), q_axis=1, qb=qb, log2_units=True,
        q_spec=pl.BlockSpec(blk, lambda bb, i: (bb, i, 0)), kv_specs=kv_specs,
        out_spec=pl.BlockSpec((None, qb * Q_BLOCK, out_cols), lambda bb, i: (bb, i, 0)),
        out_shape=(b, s, out_cols), out_cols=out_cols, nb=nb, max_dist=C_WINDOW - 1, heads=heads,
        k_off=512, v_off=640, stack=True, fold=True)


def _compress_body(t_ref, wb_ref, prow_ref, w2k_ref, w2vt_ref, kg_ref, bd_ref, kc_ref, vct_ref, *, n_chunks):
    hid_cols = 2 * C_KV_HEADS * CMP_HIDDEN
    u = jnp.zeros((n_chunks, hid_cols), F32)
    v = jnp.zeros((n_chunks, hid_cols), F32)
    pc = jnp.zeros((1, hid_cols), F32)
    for c in range(CMP_STRIDE):
        tc = t_ref[c]
        u = u + _dot(tc, wb_ref[0, c])
        v = v + _dot(tc, wb_ref[1, c])
        pc = pc + _dot(prow_ref[0, c], wb_ref[0, c])[0:1] + _dot(prow_ref[1, c], wb_ref[1, c])[0:1]
    hsum = u + pltpu.roll(v, n_chunks - 1, 0) + pc
    hid = (hsum * jax.nn.sigmoid(hsum)).astype(BF16)
    half = C_KV_HEADS * CMP_HIDDEN
    k = _dot(hid[:, :half], w2k_ref[...])
    hi, lo = _split(k * k)
    ss = _dot(hi, bd_ref[...]) + _dot(lo, bd_ref[...])
    kc_ref[...] = (k * lax.rsqrt(ss * (1.0 / HEAD_DIM) + RMS_EPS) * kg_ref[...]).astype(BF16)
    vct_ref[...] = _nt_dot(w2vt_ref[...], hid[:, half:]).astype(BF16)


def _compress(cmpd, wb, prow, w2k, w2vt, kg):
    b, _, n_chunks, width = cmpd.shape
    bd = jnp.asarray(np.kron(np.eye(LANES // HEAD_DIM), np.ones((HEAD_DIM, HEAD_DIM))), BF16)
    return pl.pallas_call(
        functools.partial(_compress_body, n_chunks=n_chunks),
        grid=(b,),
        in_specs=[
            pl.BlockSpec((None, CMP_STRIDE, n_chunks, width), lambda bb: (bb, 0, 0, 0)),
            _resident(wb.shape), _resident(prow.shape), _resident(w2k.shape), _resident(w2vt.shape),
            _resident((1, LANES)), _resident((LANES, LANES)),
        ],
        out_specs=[
            pl.BlockSpec((None, n_chunks, LANES), lambda bb: (bb, 0, 0)),
            pl.BlockSpec((None, LANES, n_chunks), lambda bb: (bb, 0, 0)),
        ],
        out_shape=[
            jax.ShapeDtypeStruct((b, n_chunks, LANES), BF16),
            jax.ShapeDtypeStruct((b, LANES, n_chunks), BF16),
        ],
        compiler_params=_cparams(("parallel",)),
    )(cmpd, wb, prow, w2k, w2vt, kg, bd)


def _to_natural(ot_list, eye):
    pairs = []
    for k in range(0, len(ot_list), 2):
        hi, lo = _split(jnp.concatenate([ot_list[k], ot_list[k + 1]], axis=0))
        pairs.append(_nt_dot(eye, hi) + _nt_dot(eye, lo))
    return jnp.concatenate(pairs, axis=1)


def _cmp_body(qt_ref, kc_ref, vct_ref, ovt_ref, eye_ref, spread_ref, cmask_ref,
              o_ref, selm_ref, cnt_ref, sel_sc, *, n_top):
    g = pl.program_id(1)
    i = pl.program_id(2)
    n_pad = kc_ref.shape[0]
    n_slc = ovt_ref.shape[0]
    qw = qt_ref.shape[1]
    qb = qw // Q_BLOCK
    per_q = Q_BLOCK // CMP_STRIDE
    own_rows = (lax.broadcasted_iota(jnp.int32, (LANES, qw), 0) // HEAD_DIM) == g
    slopes = [sl * LOG2E for sl in _slopes(C_HEADS)]

    def attend(rows):
        kc = kc_ref[0:rows, :]
        vct = vct_ref[:, 0:rows]
        mask = jnp.concatenate(
            [cmask_ref[pl.ds(pl.multiple_of(n_pad - per_q * (qb * i + h), 8), rows), :] for h in range(qb)], axis=1)
        n_f = (CMP_STRIDE * lax.broadcasted_iota(jnp.int32, (rows, qw), 0)).astype(F32)
        psum = jnp.zeros((rows, qw), F32)
        outs = []
        for r in range(C_REP):
            slope = jnp.where(g == 0, slopes[r], slopes[C_REP + r])
            qt = qt_ref[r * HEAD_DIM:(r + 1) * HEAD_DIM, :]
            q_pad = jnp.where(own_rows, jnp.concatenate([qt, qt], axis=0), jnp.zeros((LANES, qw), BF16))
            s = _dot(kc, q_pad) + (slope * n_f + mask)
            m = jnp.maximum(jnp.max(s, axis=0, keepdims=True), -1e20)
            e = jnp.exp2(s - m)
            den = jnp.sum(e, axis=0, keepdims=True)
            p = e * (1.0 / jnp.where(den > 0, den, 1.0))
            psum = psum + p
            both = _dot(vct, p.astype(BF16))
            outs.append(jnp.where(g == 0, both[:HEAD_DIM], both[HEAD_DIM:]))
        o_ref[...] = _to_natural(outs, eye_ref[...])

        n_j = rows * CMP_STRIDE // SLC_BLOCK
        hi, lo = _split(psum)
        ovt = ovt_ref[0:n_j, 0:rows]
        imp = _dot(ovt, hi) + _dot(ovt, lo)
        j_idx = lax.broadcasted_iota(jnp.int32, (n_j, qw), 0)
        t_q = i * qw + lax.broadcasted_iota(jnp.int32, (n_j, qw), 1)
        cur = lax.shift_right_logical(t_q, int(math.log2(SLC_BLOCK)))
        forced = ((j_idx == 0) | (j_idx == cur) | (j_idx == cur - 1)) & (j_idx <= cur)
        v = jnp.where((j_idx <= cur) & jnp.logical_not(forced), imp, -1.0)
        sel = jnp.where(forced, 1.0, 0.0)
        for _ in range(n_top - 3):
            m = jnp.max(v, axis=0, keepdims=True)
            first = jnp.min(jnp.where((v == m) & (m > -0.5), j_idx, n_slc), axis=0, keepdims=True)
            pick = j_idx == first
            sel = jnp.where(pick, 1.0, sel)
            v = jnp.where(pick, -1.0, v)
        sel_sc[0:n_j, :] = sel
        if n_j < n_slc:
            sel_sc[n_j:, :] = jnp.zeros((n_slc - n_j, qw), F32)

    n_var = n_pad // LANES
    for var in range(n_var):
        pl.when((qb * i + qb - 1) // (LANES // per_q) == var)(functools.partial(attend, (var + 1) * LANES))

    sel = sel_sc[...]
    neg = jnp.where(sel > 0, 0.0, -MASK_BIG).astype(BF16)
    mask_rows = _dot(spread_ref[...], neg).astype(BF16)
    sel_b = sel.astype(BF16)
    for h in range(qb):
        lanes = slice(h * Q_BLOCK, (h + 1) * Q_BLOCK)
        selm_ref[h] = mask_rows[:, lanes]
        cnt_ref[h] = _nt_dot(jnp.ones((8, Q_BLOCK), BF16), sel_b[:, lanes])


def _cmp_select(qt, kc, vct, ovt, *, n_cmp):
    b, _, s = qt.shape
    g = C_KV_HEADS
    n_pad = kc.shape[1]
    n_slc = ovt.shape[0]
    nq = s // Q_BLOCK
    rows = C_REP * HEAD_DIM
    qb = CMP_QB if nq % CMP_QB == 0 else 1
    qw = qb * Q_BLOCK
    steps = nq // qb
    eye = jnp.asarray(np.eye(qw), BF16)
    n_top = min(SLC_TOPK, n_slc)
    assert n_top > 3 and n_cmp == n_pad - 1 and n_pad % LANES == 0
    n_rel = np.arange(-n_pad, n_pad)[:, None]
    cmask = np.where(CMP_STRIDE * n_rel + CMP_BLOCK - 1 <= np.arange(Q_BLOCK)[None, :], 0.0, -MASK_BIG)
    return pl.pallas_call(
        functools.partial(_cmp_body, n_top=n_top),
        grid=(b, g, steps),
        in_specs=[
            pl.BlockSpec((None, rows, qw), lambda bb, gg, i: (bb, gg, i)),
            pl.BlockSpec((None, n_pad, LANES), lambda bb, gg, i: (bb, 0, 0)),
            pl.BlockSpec((None, LANES, n_pad), lambda bb, gg, i: (bb, 0, 0)),
            pl.BlockSpec((n_slc, n_pad), lambda bb, gg, i: (0, 0)),
            pl.BlockSpec((qw, qw), lambda bb, gg, i: (0, 0)),
            pl.BlockSpec((nq * AUG_ROWS, n_slc), lambda bb, gg, i: (0, 0)),
            pl.BlockSpec((2 * n_pad, Q_BLOCK), lambda bb, gg, i: (0, 0)),
        ],
        out_specs=[
            pl.BlockSpec((qw, rows), lambda bb, gg, i: (bb * steps + i, gg)),
            pl.BlockSpec((None, None, qb, nq * AUG_ROWS, Q_BLOCK), lambda bb, gg, i: (bb, gg, i, 0, 0)),
            pl.BlockSpec((None, None, qb, 8, n_slc), lambda bb, gg, i: (bb, gg, i, 0, 0)),
        ],
        out_shape=[
            jax.ShapeDtypeStruct((b * s, g * rows), F32),
            jax.ShapeDtypeStruct((b, g, nq, nq * AUG_ROWS, Q_BLOCK), BF16),
            jax.ShapeDtypeStruct((b, g, nq, 8, n_slc), F32),
        ],
        scratch_shapes=[pltpu.VMEM((n_slc, qw), F32)],
        compiler_params=_cparams(("parallel", "parallel", "parallel")),
    )(qt, kc, vct, ovt, eye, jnp.asarray(_block_spread(nq), BF16), jnp.asarray(cmask, F32))


def _slc_body(list_ref, qt_ref, ks_ref, vst_ref, selm_ref, eye_ref, slot_ref, o_ref, qaug, m_sc, l_sc, acc_sc,
              *, nq, stride):
    bb = pl.program_id(0)
    g = pl.program_id(1)
    i = pl.program_id(2)
    width = C_REP * Q_BLOCK
    slopes = [sl * LOG2E for sl in _slopes(C_HEADS)]
    slope_s = [jnp.where(g == 0, slopes[r], slopes[C_REP + r]) for r in range(C_REP)]

    own_rows = (lax.broadcasted_iota(jnp.int32, (LANES, width), 0) // HEAD_DIM) == g
    q6 = jnp.concatenate([qt_ref[r * HEAD_DIM:(r + 1) * HEAD_DIM, :] for r in range(C_REP)], axis=1)
    qaug[0:LANES, :] = jnp.where(own_rows, jnp.concatenate([q6, q6], axis=0), jnp.zeros((LANES, width), BF16))
    head = lax.broadcasted_iota(jnp.int32, (LANES, width), 1) // Q_BLOCK
    row = lax.broadcasted_iota(jnp.int32, (LANES, width), 0)
    slope_t = jnp.zeros((LANES, width), F32)
    for r in range(C_REP):
        slope_t = jnp.where(head == r, slope_s[r], slope_t)
    s_hi, s_lo = _split(slope_t)
    slope_rows = jnp.where(row == AUG_POS, s_hi.astype(F32), jnp.where(row == AUG_POS + 1, s_lo.astype(F32), 0.0))
    qaug[LANES:, :] = slope_rows.astype(BF16)

    q_loc = lax.broadcasted_iota(jnp.int32, (Q_BLOCK, Q_BLOCK), 1)
    k_loc = lax.broadcasted_iota(jnp.int32, (Q_BLOCK, Q_BLOCK), 0)
    causal = jnp.where(k_loc > q_loc, -MASK_BIG, 0.0)

    m_sc[...] = jnp.full(m_sc.shape, NEG_INF, F32)
    l_sc[...] = jnp.zeros(l_sc.shape, F32)
    acc_sc[...] = jnp.zeros(acc_sc.shape, F32)

    def accumulate(tiles, own_first):
        keys = []
        for u, (jj, _) in enumerate(tiles):
            keys.append(ks_ref[pl.ds(pl.multiple_of(jj * Q_BLOCK, Q_BLOCK), Q_BLOCK), :] + slot_ref[u])
            rows = selm_ref[pl.ds(pl.multiple_of(jj * AUG_ROWS, AUG_ROWS), AUG_ROWS), :]
            lo = LANES + AUG_ROWS * (u + 1)
            qaug[lo:lo + AUG_ROWS, :] = jnp.concatenate([rows] * C_REP, axis=1)
        st = _dot(jnp.concatenate(keys, axis=0), qaug[...])
        ps, alphas = [], []
        for r in range(C_REP):
            sl = slice(r * Q_BLOCK, (r + 1) * Q_BLOCK)
            m_old = m_sc[:, sl]
            m_new = m_old
            parts = []
            for u, (jj, extra) in enumerate(tiles):
                s = st[u * Q_BLOCK:(u + 1) * Q_BLOCK, sl]
                if own_first and u == 0:
                    s = s + causal
                c = slope_s[r] * ((jj - i) * Q_BLOCK).astype(F32) + extra
                m_new = jnp.maximum(m_new, jnp.max(s, axis=0, keepdims=True) + c)
                parts.append((s, c))
            alpha = jnp.exp2(m_old - m_new)
            l_new = alpha * l_sc[:, sl]
            p_rows = []
            for s, c in parts:
                p = jnp.exp2(s + (c - m_new))
                l_new = l_new + jnp.sum(p, axis=0, keepdims=True)
                p_rows.append(p.astype(BF16))
            l_sc[:, sl] = l_new
            m_sc[:, sl] = m_new
            ps.append(jnp.concatenate(p_rows, axis=0))
            alphas.append(alpha)
        values = jnp.concatenate([vst_ref[jj] for jj, _ in tiles], axis=1)
        pv = _dot(values, jnp.concatenate(ps, axis=1))
        acc_sc[...] = jnp.concatenate(alphas, axis=1) * acc_sc[...] + pv

    base = ((bb * pl.num_programs(1) + g) * nq + i) * stride
    count = list_ref[base]

    def listed(slot):
        return list_ref[base + 1 + slot], jnp.where(slot < count, 0.0, -MASK_BIG)

    accumulate([(i, 0.0)] + [listed(u) for u in range(SLC_GROUP - 1)], True)

    def step(k, carry):
        accumulate([listed(SLC_GROUP - 1 + SLC_GROUP * k + u) for u in range(SLC_GROUP)], False)
        return carry

    rest = jnp.maximum(count - (SLC_GROUP - 1), 0)
    full = rest // SLC_GROUP
    lax.fori_loop(0, full, step, 0)
    left = rest - full * SLC_GROUP
    first_left = SLC_GROUP - 1 + full * SLC_GROUP
    for size in range(1, SLC_GROUP):
        @pl.when(left == size)
        def _(size=size):
            accumulate([listed(first_left + u) for u in range(size)], False)
    l = l_sc[...]
    o = acc_sc[...] / jnp.where(l > 0, l, 1.0)
    o_ref[...] = _to_natural([o[:, r * Q_BLOCK:(r + 1) * Q_BLOCK] for r in range(C_REP)], eye_ref[...])


def _slc_attention(lists, qt, ks, vst, sel, *, stride):
    b, _, s = qt.shape
    g = C_KV_HEADS
    nq = s // Q_BLOCK
    rows = C_REP * HEAD_DIM
    width = C_REP * Q_BLOCK
    eye = jnp.asarray(np.eye(LANES), BF16)
    assert AUG_ROWS * (SLC_GROUP + 1) <= LANES
    slots = jnp.asarray(_slot_pattern(), BF16)
    grid_spec = pltpu.PrefetchScalarGridSpec(
        num_scalar_prefetch=1,
        grid=(b, g, nq),
        in_specs=[
            pl.BlockSpec((None, rows, Q_BLOCK), lambda bb, gg, i, bits: (bb, gg, i)),
            pl.BlockSpec((None, s, 2 * LANES), lambda bb, gg, i, bits: (bb, 0, 0)),
            pl.BlockSpec((None, nq, None, HEAD_DIM, Q_BLOCK), lambda bb, gg, i, bits: (bb, 0, gg, 0, 0)),
            pl.BlockSpec((None, None, None, nq * AUG_ROWS, Q_BLOCK), lambda bb, gg, i, bits: (bb, gg, i, 0, 0)),
            pl.BlockSpec((LANES, LANES), lambda bb, gg, i, bits: (0, 0)),
            pl.BlockSpec(slots.shape, lambda bb, gg, i, bits: (0, 0, 0)),
        ],
        out_specs=pl.BlockSpec((Q_BLOCK, rows), lambda bb, gg, i, bits: (bb * nq + i, gg)),
        scratch_shapes=[
            pltpu.VMEM((2 * LANES, width), BF16),
            pltpu.VMEM((1, width), F32),
            pltpu.VMEM((1, width), F32),
            pltpu.VMEM((HEAD_DIM, width), F32),
        ],
    )
    return pl.pallas_call(
        functools.partial(_slc_body, nq=nq, stride=stride),
        grid_spec=grid_spec,
        out_shape=jax.ShapeDtypeStruct((b * s, g * rows), F32),
        compiler_params=_cparams(("parallel", "parallel", "parallel")),
    )(lists, qt, ks, vst, sel, eye, slots)


def _merge_body(x_ref, g_ref, oa0, la0, oa1, la1, oa2, la2, ob_ref, ocmp_ref, oslc_ref, owin_ref, cg_ref,
                p4t_ref, p16t_ref, ex_ref, wg0_ref, wg1_ref, wg2_ref, wa_ref, wb_ref, wc_ref,
                out_ref, h_ref, oall_ref):
    c0, c1 = A_OUT, A_OUT + B_HEADS * HEAD_DIM

    def gate_logits(h):
        return [_dot(h, w_ref[...]) for w_ref in (wg0_ref, wg1_ref, wg2_ref)]

    def emit(logits):
        merged = jax.nn.sigmoid(logits[0]) * _dot(oall_ref[:, 0:c0], wa_ref[...])
        merged += jax.nn.sigmoid(logits[1]) * _dot(oall_ref[:, c0:c1], wb_ref[...])
        merged += jax.nn.sigmoid(logits[2]) * _dot(oall_ref[:, c1:], wc_ref[...])
        out_ref[...] = merged.astype(BF16)

    @pl.when(pl.program_id(1) == 0)
    def _():
        h = _rms_rows(x_ref[...], g_ref[...])
        h_ref[...] = h
        logits = gate_logits(h)

        def natural(ref, pt_ref):
            hi, lo = _split(ref[...].reshape(TM, A_OUT))
            return _dot(pt_ref[...], hi) + _dot(pt_ref[...], lo)

        o1, l1 = natural(oa1, p4t_ref), natural(la1, p4t_ref)
        o2, l2 = natural(oa2, p16t_ref), natural(la2, p16t_ref)
        cg_split = jnp.concatenate(_split(cg_ref[...]), axis=1)
        gate_c = [_dot(cg_split, ex_ref[w]) for w in range(3)]

        o0, l0 = oa0[...], la0[...]
        mx = jnp.maximum(jnp.maximum(l0, l1), l2)
        e0, e1, e2 = jnp.exp(l0 - mx), jnp.exp(l1 - mx), jnp.exp(l2 - mx)
        oall_ref[:, 0:A_OUT] = ((e0 * o0 + e1 * o1 + e2 * o2) / (e0 + e1 + e2)).astype(BF16)
        oall_ref[:, A_OUT:A_OUT + B_HEADS * HEAD_DIM] = ob_ref[...].astype(BF16)
        o_c = gate_c[0] * ocmp_ref[...] + gate_c[1] * oslc_ref[...] + gate_c[2] * owin_ref[...]
        oall_ref[:, A_OUT + B_HEADS * HEAD_DIM:] = o_c.astype(BF16)
        emit(logits)

    @pl.when(pl.program_id(1) > 0)
    def _():
        emit(gate_logits(h_ref[...]))


def _merge(x, g, a_outs, ob, ocmp, oslc, owin, cg, w_gate, wa, wb, wc, ex, *, tn=512):
    t, d = x.shape
    per16 = CHUNK16 // TM
    n_t = d // tn
    (oa0, la0), (oa1, la1), (oa2, la2) = a_outs

    last = t // TM - 1

    def tile(i, n, ahead):
        return jnp.minimum(i + jnp.where(n >= ahead, 1, 0), last)

    def rows(a, ahead):
        return pl.BlockSpec((TM, a.shape[1]), lambda i, n: (tile(i, n, ahead), 0))

    a1_spec = pl.BlockSpec((None, 4, Q_BLOCK, A_OUT), lambda i, n: (tile(i, n, 1), 0, 0, 0))
    a2_spec = pl.BlockSpec((None, 16, TM // 16, A_OUT),
                           lambda i, n: (tile(i, n, 1) // per16, 0, tile(i, n, 1) % per16, 0))
    p4t = jnp.asarray(_deinterleave(TM, 4).T, BF16)
    p16t = jnp.asarray(_deinterleave(TM, 16).T, BF16)
    in_specs = [
        rows(x, 1), _resident((1, d)),
        rows(oa0, 1), rows(la0, 1), a1_spec, a1_spec, a2_spec, a2_spec,
        rows(ob, 3), rows(ocmp, 2), rows(oslc, 2), rows(owin, 2), rows(cg, 3),
        _resident((TM, TM)), _resident((TM, TM)), _resident(ex.shape),
        pl.BlockSpec((d, tn), lambda i, n: (0, n)),
        pl.BlockSpec((d, tn), lambda i, n: (0, n + n_t)),
        pl.BlockSpec((d, tn), lambda i, n: (0, n + 2 * n_t)),
        pl.BlockSpec((wa.shape[0], tn), lambda i, n: (0, n)),
        pl.BlockSpec((wb.shape[0], tn), lambda i, n: (0, n)),
        pl.BlockSpec((wc.shape[0], tn), lambda i, n: (0, n)),
    ]
    return pl.pallas_call(
        _merge_body,
        grid=(t // TM, n_t),
        in_specs=in_specs,
        out_specs=pl.BlockSpec((TM, tn), lambda i, n: (i, n)),
        out_shape=jax.ShapeDtypeStruct((t, d), BF16),
        scratch_shapes=[pltpu.VMEM((TM, d), BF16), pltpu.VMEM((TM, wa.shape[0] + wb.shape[0] + wc.shape[0]), BF16)],
        compiler_params=_cparams(("parallel", "arbitrary")),
    )(x, g.reshape(1, d), oa0, la0, oa1, la1, oa2, la2, ob, ocmp, oslc, owin, cg, p4t, p16t, ex,
      w_gate, w_gate, w_gate, wa, wb, wc)


def _out_body(x_ref, m_ref, w_ref, o_ref):
    o_ref[...] = x_ref[...] + _dot(m_ref[...], w_ref[...])


def _out_proj(x, merged, w_out):
    t, d = x.shape
    rows = pl.BlockSpec((TM, d), lambda i: (i, 0))
    return pl.pallas_call(
        _out_body,
        grid=(t // TM,),
        in_specs=[rows, rows, _resident((d, d))],
        out_specs=rows,
        out_shape=jax.ShapeDtypeStruct((t, d), F32),
        compiler_params=_cparams(("parallel",)),
    )(x, merged, w_out)


def _qkv_column_params(qk_gain):
    flag, gain, scale = [], [], []
    one = jnp.ones((HEAD_DIM,), F32)

    def add(n_heads, normed, is_q, gvec, units=1.0):
        for _ in range(n_heads):
            flag.append(np.full((HEAD_DIM,), 1.0 if normed else 0.0, np.float32))
            gain.append(gvec if normed else one)
            scale.append(np.full((HEAD_DIM,), units * HEAD_DIM ** -0.5 if is_q else 1.0, np.float32))

    for _ in range(len(A_GROUPS)):
        add(A_HEADS_PER_GROUP, True, True, qk_gain[0, 0])
        add(A_HEADS_PER_GROUP, True, False, qk_gain[0, 1])
        add(A_HEADS_PER_GROUP, False, False, one)
    add(B_HEADS, True, True, qk_gain[1, 0])
    add(B_KV_HEADS, True, False, qk_gain[1, 1])
    add(B_KV_HEADS, False, False, one)
    add(C_HEADS, True, True, qk_gain[2, 0], units=LOG2E)
    for normed in (False, False, True, False, True, False):
        add(C_KV_HEADS, normed, False, qk_gain[2, 1])
    flag = np.concatenate(flag)
    assert flag.shape[0] == QKV_COLS
    return jnp.asarray(flag), jnp.concatenate(gain) * jnp.asarray(np.concatenate(scale))


def _overlap_t(n_slc, n_pad, n_cmp):
    n = np.arange(n_pad)[None, :]
    j = np.arange(n_slc)[:, None]
    start, end = CMP_STRIDE * n, CMP_STRIDE * n + CMP_BLOCK - 1
    ov = (start <= SLC_BLOCK * j + SLC_BLOCK - 1) & (end >= SLC_BLOCK * j) & (n < n_cmp)
    return jnp.asarray(ov, BF16)


def _gate_expand():
    ex = np.zeros((3, LANES, C_HEADS * HEAD_DIM), np.float32)
    for w in range(3):
        for h in range(C_HEADS):
            ex[w, h * 3 + w, h * HEAD_DIM:(h + 1) * HEAD_DIM] = 1.0
    return jnp.asarray(np.concatenate([ex, ex], axis=1), BF16)


def _compress_weights(cmp_pos, cmp_w1, cmp_w2):
    n_q = 2 * C_KV_HEADS
    w1 = cmp_w1.reshape(2, 2, CMP_STRIDE, HEAD_DIM, CMP_HIDDEN)
    w1q = jnp.repeat(w1, C_KV_HEADS, axis=0)
    wb = jnp.einsum("qhcdn,qp->hcqdpn", w1q, jnp.eye(n_q, dtype=F32))
    wb = wb.reshape(2, CMP_STRIDE, n_q * HEAD_DIM, n_q * CMP_HIDDEN).astype(BF16)
    pos = cmp_pos.reshape(2, 2, CMP_STRIDE, HEAD_DIM)
    prow = jnp.repeat(pos, C_KV_HEADS, axis=0).transpose(1, 2, 0, 3).reshape(2, CMP_STRIDE, 1, n_q * HEAD_DIM)
    prow = jnp.broadcast_to(prow, (2, CMP_STRIDE, 8, n_q * HEAD_DIM)).astype(BF16)
    eye_g = jnp.eye(C_KV_HEADS, dtype=F32)
    w2k = jnp.kron(eye_g, cmp_w2[0]).astype(BF16)
    w2vt = jnp.kron(eye_g, cmp_w2[1]).T.astype(BF16)
    return wb, prow, w2k, w2vt


def _token_mixing(x, b, s, mix_norm, w_in, qk_gain, sinks, cmp_pos, cmp_w1, cmp_w2, w_a, w_b, w_c):
    t, d = x.shape
    assert s % CHUNK16 == 0 and d % 512 == 0
    c_gate_cols = 3 * C_HEADS
    w_qkv = w_in[:, :QKV_COLS + LANES].astype(BF16)
    flag, gs = _qkv_column_params(qk_gain)
    a0, a1, a2, bsec, cq, ckv, cmpd, qt, ks, vst, cg = _qkv_proj(x, mix_norm, w_qkv, flag, gs, b, s)

    a_outs = [_dilated_group(a0.reshape(b, s, SEC), 0, b, s), _dilated_group(a1, 1, b, s),
              _dilated_group(a2, 2, b, s)]
    a_outs[0] = tuple(v.reshape(t, A_OUT) for v in a_outs[0])
    o_b = _sink_swa(bsec.reshape(b, s, SEC), sinks.astype(F32), b, s).reshape(t, -1)
    o_win = _nsa_window(cq.reshape(b, s, SEC), ckv.reshape(b, s, SEC), b, s).reshape(t, -1)

    n_chunks = s // CMP_STRIDE
    n_cmp = (s - CMP_BLOCK) // CMP_STRIDE + 1
    n_slc = s // SLC_BLOCK
    nq = s // Q_BLOCK
    kg = jnp.tile(qk_gain[2, 1], C_KV_HEADS).reshape(1, LANES)
    kc, vct = _compress(cmpd, *_compress_weights(cmp_pos, cmp_w1, cmp_w2), kg)
    o_cmp, sel, cnt = _cmp_select(qt, kc, vct, _overlap_t(n_slc, n_chunks, n_cmp), n_cmp=n_cmp)

    act = (cnt[:, :, :, 0, :] > 0).reshape(b, C_KV_HEADS, nq, nq, 2).any(axis=-1)
    act = act & (jnp.arange(nq)[None, :] < jnp.arange(nq)[:, None])
    order = jnp.argsort(jnp.logical_not(act), axis=-1, stable=True).astype(jnp.int32)
    count = jnp.sum(act, axis=-1, dtype=jnp.int32)[..., None]
    lists = jnp.concatenate([count, order] + [jnp.zeros_like(count)] * (SLC_GROUP - 1), axis=-1)
    o_slc = _slc_attention(lists.reshape(-1), qt, ks.reshape(b, s, 2 * LANES),
                           vst.reshape(b, nq, C_KV_HEADS, HEAD_DIM, Q_BLOCK), sel, stride=nq + SLC_GROUP)

    return _merge(x, mix_norm, a_outs, o_b, o_cmp, o_slc, o_win, cg,
                  w_in[:, QKV_COLS + c_gate_cols:].astype(BF16), w_a.astype(BF16), w_b.astype(BF16),
                  w_c.astype(BF16), _gate_expand())


def kernel(x, ffn1_norm, ffn1_w_gu, ffn1_w_down, mix_norm, w_in, qk_gain, sinks, cmp_pos, cmp_w1, cmp_w2,
           w_branch_a, w_branch_b, w_branch_c, w_out, ffn2_norm, ffn2_w_gu, ffn2_w_down):
    b, s, d = x.shape
    h = x.reshape(b * s, d)
    w1_gu, w1_down = ffn1_w_gu.astype(BF16), ffn1_w_down.astype(BF16)
    w2_gu, w2_down = ffn2_w_gu.astype(BF16), ffn2_w_down.astype(BF16)
    for l in range(ffn1_norm.shape[0]):
        h = _ffn(h, ffn1_norm[l], w1_gu, w1_down, l)
        merged = _token_mixing(h, b, s, mix_norm[l], w_in[l], qk_gain[l], sinks[l], cmp_pos[l], cmp_w1[l],
                               cmp_w2[l], w_branch_a[l], w_branch_b[l], w_branch_c[l])
        h = _out_proj(h, merged, w_out[l].astype(BF16))
        h = _ffn(h, ffn2_norm[l], w2_gu, w2_down, l)
    return h.reshape(b, s, d)
```
